```python
import jax, jax.numpy as jnp
from jax import lax
import numpy as np

D_MODEL = 1024
BATCH = 32
SEQ = 2048
DEPTH = 1

FOX_HEADS = 8
FOX_HEAD_DIM = 64
FOX_WIDTH = FOX_HEADS * FOX_HEAD_DIM
MLA_HEADS = 8
MLA_NOPE_DIM = 64
MLA_ROPE_DIM = 32
MLA_QK_DIM = MLA_NOPE_DIM + MLA_ROPE_DIM
MLA_V_DIM = 64
MLA_Q_RANK = 256
MLA_KV_RANK = 128
MLA_WIDTH = MLA_HEADS * MLA_V_DIM
MIX_WIDTH = FOX_WIDTH + MLA_WIDTH
IN_SPLITS = (FOX_WIDTH, FOX_WIDTH, FOX_WIDTH, FOX_HEADS, MLA_Q_RANK, MLA_KV_RANK, MLA_ROPE_DIM)
IN_WIDTH = sum(IN_SPLITS)
D_FF = -(-8 * D_MODEL // (3 * 256)) * 256

Q_BLOCK = 128
ROPE_THETA = 10000.0
NORM_EPS = 1e-6

kernel_name = "hymba_fox_mla_hybrid_layer"


def rmsnorm(x, g):
    xf = x.astype(jnp.float32)
    y = xf * lax.rsqrt(jnp.mean(xf * xf, axis=-1, keepdims=True) + NORM_EPS)
    return (y * g.astype(jnp.float32)).astype(x.dtype)


def rope_tables(positions, dim):
    inv_freq = ROPE_THETA ** (-jnp.arange(0, dim, 2, dtype=jnp.float32) / dim)
    ang = positions.astype(jnp.float32)[:, :, None] * inv_freq[None, None, :]
    return jnp.cos(ang)[:, None], jnp.sin(ang)[:, None]


def apply_rope(x, cos, sin):
    xf = x.astype(jnp.float32)
    x1, x2 = jnp.split(xf, 2, axis=-1)
    return jnp.concatenate([x1 * cos - x2 * sin, x2 * cos + x1 * sin], axis=-1).astype(x.dtype)


def causal_block_attention(q, k, v, scale, log_decay=None):
    seq = q.shape[2]
    outs = []
    for blk in range(seq // Q_BLOCK):
        q0 = blk * Q_BLOCK
        q1 = q0 + Q_BLOCK
        qb = q[:, :, q0:q1]
        kb = k[:, :, :q1]
        vb = v[:, :, :q1]
        logits = jnp.einsum('bhqd,bhkd->bhqk', qb, kb,
                            preferred_element_type=jnp.float32) * scale
        if log_decay is not None:
            ld = log_decay.astype(jnp.float32)
            logits = logits + (ld[:, :, q0:q1, None] - ld[:, :, None, :q1])
        q_pos = q0 + jnp.arange(Q_BLOCK)
        k_pos = jnp.arange(q1)
        mask = k_pos[None, :] <= q_pos[:, None]
        logits = jnp.where(mask, logits, -jnp.inf)
        p = jax.nn.softmax(logits, axis=-1)
        outs.append(jnp.einsum('bhqk,bhkd->bhqd', p.astype(vb.dtype), vb))
    return jnp.concatenate(outs, axis=2)


def to_heads(t, n_heads):
    b, s, _ = t.shape
    return t.reshape(b, s, n_heads, -1).transpose(0, 2, 1, 3)


def from_heads(t):
    b, h, s, d = t.shape
    return t.transpose(0, 2, 1, 3).reshape(b, s, h * d)


def hybrid_mixer(h, cos, sin, w_in, b_fgate, q_norm_g, w_uq, kv_norm_g, w_ukv,
                 fox_out_g, mla_out_g, w_o):
    b, s, _ = h.shape
    proj = jnp.einsum('bsd,de->bse', h, w_in)
    cuts = list(np.cumsum(IN_SPLITS)[:-1])
    fq, fk, fv, f_logit, q_lat, kv_lat, k_rope = jnp.split(proj, cuts, axis=-1)

    log_f = jax.nn.log_sigmoid((f_logit + b_fgate).astype(jnp.float32))
    c = jnp.cumsum(log_f, axis=1).transpose(0, 2, 1)
    fox_o = causal_block_attention(to_heads(fq, FOX_HEADS), to_heads(fk, FOX_HEADS),
                                   to_heads(fv, FOX_HEADS), FOX_HEAD_DIM ** -0.5, c)
    fox_o = rmsnorm(from_heads(fox_o), fox_out_g)

    q = jnp.einsum('bsr,re->bse', rmsnorm(q_lat, q_norm_g), w_uq)
    q = to_heads(q, MLA_HEADS)
    q_nope, q_pe = q[..., :MLA_NOPE_DIM], q[..., MLA_NOPE_DIM:]
    q = jnp.concatenate([q_nope, apply_rope(q_pe, cos, sin)], axis=-1)
    kv = jnp.einsum('bsr,re->bse', rmsnorm(kv_lat, kv_norm_g), w_ukv)
    kv = to_heads(kv, MLA_HEADS)
    k_nope, v = kv[..., :MLA_NOPE_DIM], kv[..., MLA_NOPE_DIM:]
    k_pe = apply_rope(k_rope[:, None], cos, sin)
    k = jnp.concatenate([k_nope, jnp.broadcast_to(k_pe, (b, MLA_HEADS, s, MLA_ROPE_DIM))], axis=-1)
    mla_o = causal_block_attention(q, k, v, MLA_QK_DIM ** -0.5)
    mla_o = rmsnorm(from_heads(mla_o), mla_out_g)

    return jnp.einsum('bse,ed->bsd', jnp.concatenate([fox_o, mla_o], axis=-1), w_o)


def swiglu(h, w_gate, w_up, w_down):
    g = jnp.einsum('bsd,df->bsf', h, w_gate)
    u = jnp.einsum('bsd,df->bsf', h, w_up)
    return jnp.einsum('bsf,fd->bsd', jax.nn.silu(g) * u, w_down)


def _fwd_setup_inputs(seed: int = 0) -> dict:
    key = jax.random.key(seed)
    ks = jax.random.split(key, 20)

    def w(k, shape, fan_in):
        return jax.random.normal(k, shape, jnp.float32) * fan_in ** -0.5

    def gain(k, shape):
        return 1.0 + 0.05 * jax.random.normal(k, shape, jnp.float32)

    x = jax.random.normal(ks[0], (BATCH, SEQ, D_MODEL), jnp.float32)
    offsets = jax.random.randint(ks[1], (BATCH, 1), 0, 4096, dtype=jnp.int32)
    positions = offsets + jnp.arange(SEQ, dtype=jnp.int32)[None, :]
    return {
        "x": x,
        "positions": positions,
        "norm_mix_g": gain(ks[2], (DEPTH, D_MODEL)),
        "w_in": w(ks[3], (DEPTH, D_MODEL, IN_WIDTH), D_MODEL),
        "b_fgate": 1.0 + 3.0 * jax.random.uniform(ks[4], (DEPTH, FOX_HEADS), jnp.float32),
        "q_norm_g": gain(ks[5], (DEPTH, MLA_Q_RANK)),
        "w_uq": w(ks[6], (DEPTH, MLA_Q_RANK, MLA_HEADS * MLA_QK_DIM), MLA_Q_RANK),
        "kv_norm_g": gain(ks[7], (DEPTH, MLA_KV_RANK)),
        "w_ukv": w(ks[8], (DEPTH, MLA_KV_RANK, MLA_HEADS * (MLA_NOPE_DIM + MLA_V_DIM)), MLA_KV_RANK),
        "fox_out_g": gain(ks[9], (DEPTH, FOX_WIDTH)),
        "mla_out_g": gain(ks[10], (DEPTH, MLA_WIDTH)),
        "w_o": w(ks[11], (DEPTH, MIX_WIDTH, D_MODEL), MIX_WIDTH),
        "norm_ffn_g": gain(ks[12], (DEPTH, D_MODEL)),
        "w_gate": w(ks[13], (DEPTH, D_MODEL, D_FF), D_MODEL),
        "w_up": w(ks[14], (DEPTH, D_MODEL, D_FF), D_MODEL),
        "w_down": w(ks[15], (DEPTH, D_FF, D_MODEL), D_FF),
        "final_norm_g": gain(ks[16], (D_MODEL,)),
    }


def _fwd_reference(x, positions, norm_mix_g, w_in, b_fgate, q_norm_g, w_uq, kv_norm_g, w_ukv,
              fox_out_g, mla_out_g, w_o, norm_ffn_g, w_gate, w_up, w_down, final_norm_g):
    cos, sin = rope_tables(positions, MLA_ROPE_DIM)
    for l in range(DEPTH):
        h = rmsnorm(x, norm_mix_g[l])
        x = x + hybrid_mixer(h, cos, sin, w_in[l], b_fgate[l], q_norm_g[l], w_uq[l],
                             kv_norm_g[l], w_ukv[l], fox_out_g[l], mla_out_g[l], w_o[l])
        h = rmsnorm(x, norm_ffn_g[l])
        x = x + swiglu(h, w_gate[l], w_up[l], w_down[l])
    return rmsnorm(x, final_norm_g)


import jax as _jax
import jax.numpy as _jnp

TWIN_FORMAT = 'train_step'
FWD_PARAMS = ['x', 'positions', 'norm_mix_g', 'w_in', 'b_fgate', 'q_norm_g', 'w_uq', 'kv_norm_g', 'w_ukv', 'fox_out_g', 'mla_out_g', 'w_o', 'norm_ffn_g', 'w_gate', 'w_up', 'w_down', 'final_norm_g']
TWIN_WEIGHTS = ['norm_mix_g', 'w_in', 'b_fgate', 'q_norm_g', 'w_uq', 'kv_norm_g', 'w_ukv', 'fox_out_g', 'mla_out_g', 'w_o', 'norm_ffn_g', 'w_gate', 'w_up', 'w_down', 'final_norm_g']
TWIN_DIFF_INPUT = 'x'
TWIN_INPUTS = ['x', 'positions', 'norm_mix_g', 'w_in', 'b_fgate', 'q_norm_g', 'w_uq', 'kv_norm_g', 'w_ukv', 'fox_out_g', 'mla_out_g', 'w_o', 'norm_ffn_g', 'w_gate', 'w_up', 'w_down', 'final_norm_g', 'loss_target', 'm_norm_mix_g', 'm_w_in', 'm_b_fgate', 'm_q_norm_g', 'm_w_uq', 'm_kv_norm_g', 'm_w_ukv', 'm_fox_out_g', 'm_mla_out_g', 'm_w_o', 'm_norm_ffn_g', 'm_w_gate', 'm_w_up', 'm_w_down', 'm_final_norm_g', 'v_norm_mix_g', 'v_w_in', 'v_b_fgate', 'v_q_norm_g', 'v_w_uq', 'v_kv_norm_g', 'v_w_ukv', 'v_fox_out_g', 'v_mla_out_g', 'v_w_o', 'v_norm_ffn_g', 'v_w_gate', 'v_w_up', 'v_w_down', 'v_final_norm_g']
TWIN_OUTPUTS = ['loss', 'grad_x', 'grad_norm_mix_g', 'grad_w_in', 'grad_b_fgate', 'grad_q_norm_g', 'grad_w_uq', 'grad_kv_norm_g', 'grad_w_ukv', 'grad_fox_out_g', 'grad_mla_out_g', 'grad_w_o', 'grad_norm_ffn_g', 'grad_w_gate', 'grad_w_up', 'grad_w_down', 'grad_final_norm_g', 'delta_norm_mix_g', 'delta_w_in', 'delta_b_fgate', 'delta_q_norm_g', 'delta_w_uq', 'delta_kv_norm_g', 'delta_w_ukv', 'delta_fox_out_g', 'delta_mla_out_g', 'delta_w_o', 'delta_norm_ffn_g', 'delta_w_gate', 'delta_w_up', 'delta_w_down', 'delta_final_norm_g', 'new_m_norm_mix_g', 'new_m_w_in', 'new_m_b_fgate', 'new_m_q_norm_g', 'new_m_w_uq', 'new_m_kv_norm_g', 'new_m_w_ukv', 'new_m_fox_out_g', 'new_m_mla_out_g', 'new_m_w_o', 'new_m_norm_ffn_g', 'new_m_w_gate', 'new_m_w_up', 'new_m_w_down', 'new_m_final_norm_g', 'new_v_norm_mix_g', 'new_v_w_in', 'new_v_b_fgate', 'new_v_q_norm_g', 'new_v_w_uq', 'new_v_kv_norm_g', 'new_v_w_ukv', 'new_v_fox_out_g', 'new_v_mla_out_g', 'new_v_w_o', 'new_v_norm_ffn_g', 'new_v_w_gate', 'new_v_w_up', 'new_v_w_down', 'new_v_final_norm_g']
TWIN_LEAF_KINDS = {'loss': 'loss', 'grad_x': 'grad_x', 'grad_norm_mix_g': 'grad_w', 'grad_w_in': 'grad_w', 'grad_b_fgate': 'grad_w', 'grad_q_norm_g': 'grad_w', 'grad_w_uq': 'grad_w', 'grad_kv_norm_g': 'grad_w', 'grad_w_ukv': 'grad_w', 'grad_fox_out_g': 'grad_w', 'grad_mla_out_g': 'grad_w', 'grad_w_o': 'grad_w', 'grad_norm_ffn_g': 'grad_w', 'grad_w_gate': 'grad_w', 'grad_w_up': 'grad_w', 'grad_w_down': 'grad_w', 'grad_final_norm_g': 'grad_w', 'delta_norm_mix_g': 'delta_w', 'delta_w_in': 'delta_w', 'delta_b_fgate': 'delta_w', 'delta_q_norm_g': 'delta_w', 'delta_w_uq': 'delta_w', 'delta_kv_norm_g': 'delta_w', 'delta_w_ukv': 'delta_w', 'delta_fox_out_g': 'delta_w', 'delta_mla_out_g': 'delta_w', 'delta_w_o': 'delta_w', 'delta_norm_ffn_g': 'delta_w', 'delta_w_gate': 'delta_w', 'delta_w_up': 'delta_w', 'delta_w_down': 'delta_w', 'delta_final_norm_g': 'delta_w', 'new_m_norm_mix_g': 'new_m', 'new_m_w_in': 'new_m', 'new_m_b_fgate': 'new_m', 'new_m_q_norm_g': 'new_m', 'new_m_w_uq': 'new_m', 'new_m_kv_norm_g': 'new_m', 'new_m_w_ukv': 'new_m', 'new_m_fox_out_g': 'new_m', 'new_m_mla_out_g': 'new_m', 'new_m_w_o': 'new_m', 'new_m_norm_ffn_g': 'new_m', 'new_m_w_gate': 'new_m', 'new_m_w_up': 'new_m', 'new_m_w_down': 'new_m', 'new_m_final_norm_g': 'new_m', 'new_v_norm_mix_g': 'new_v', 'new_v_w_in': 'new_v', 'new_v_b_fgate': 'new_v', 'new_v_q_norm_g': 'new_v', 'new_v_w_uq': 'new_v', 'new_v_kv_norm_g': 'new_v', 'new_v_w_ukv': 'new_v', 'new_v_fox_out_g': 'new_v', 'new_v_mla_out_g': 'new_v', 'new_v_w_o': 'new_v', 'new_v_norm_ffn_g': 'new_v', 'new_v_w_gate': 'new_v', 'new_v_w_up': 'new_v', 'new_v_w_down': 'new_v', 'new_v_final_norm_g': 'new_v'}


def _forward(args):
    return _fwd_reference(*[args[k] for k in FWD_PARAMS])


def _output_shape():
    out = _jax.eval_shape(lambda: _forward(_fwd_setup_inputs(0)))
    return out.shape, out.dtype

N_MICROBATCH = 1
ADAM_LR = 0.001
ADAM_B1 = 0.9
ADAM_B2 = 0.999
ADAM_EPS = 1e-08
ADAM_WD = 0.01
ADAM_STEP = 10
PER_EXAMPLE_BATCH_AXIS = {'x': 0, 'positions': 0, 'loss_target': 0}
SHARED_INPUTS = []
_WEIGHT_DTYPES = {'norm_mix_g': _jnp.float32, 'w_in': _jnp.float32, 'b_fgate': _jnp.float32, 'q_norm_g': _jnp.float32, 'w_uq': _jnp.float32, 'kv_norm_g': _jnp.float32, 'w_ukv': _jnp.float32, 'fox_out_g': _jnp.float32, 'mla_out_g': _jnp.float32, 'w_o': _jnp.float32, 'norm_ffn_g': _jnp.float32, 'w_gate': _jnp.float32, 'w_up': _jnp.float32, 'w_down': _jnp.float32, 'final_norm_g': _jnp.float32}
MOMENT_SCALE = {'norm_mix_g': 2.969978e-01, 'w_in': 2.120553e-01, 'b_fgate': 2.226263e+00, 'q_norm_g': 3.032844e-01, 'w_uq': 1.722382e-01, 'kv_norm_g': 9.581051e-01, 'w_ukv': 2.030980e-01, 'fox_out_g': 2.141829e-01, 'mla_out_g': 2.048575e-01, 'w_o': 2.035940e-01, 'norm_ffn_g': 1.403725e-01, 'w_gate': 6.151079e-02, 'w_up': 6.029311e-02, 'w_down': 9.911374e-02, 'final_norm_g': 6.396185e+01}


def _to_microbatches(a, axis):
    t = _jnp.moveaxis(a, axis, 0)
    t = t.reshape((N_MICROBATCH, t.shape[0] // N_MICROBATCH) + t.shape[1:])
    return _jnp.moveaxis(t, 1, axis + 1)


def setup_inputs(seed: int = 0) -> dict:
    inp = _fwd_setup_inputs(seed)
    key = _jax.random.fold_in(_jax.random.key(seed), 7919)
    shape, _ = _output_shape()
    out = dict(inp)
    out["loss_target"] = _jax.random.normal(_jax.random.fold_in(key, 0), shape, _jnp.float32)
    for i, name in enumerate(TWIN_WEIGHTS):
        w = inp[name].astype(_jnp.float32)
        if MOMENT_SCALE is None:
            s = _jnp.sqrt(_jnp.mean(_jnp.square(w)) + 1e-30)
        else:
            s = MOMENT_SCALE[name]
        km, kv = _jax.random.split(_jax.random.fold_in(key, i + 1))
        out[name] = w
        out["m_" + name] = s * _jax.random.normal(km, w.shape, _jnp.float32)
        out["v_" + name] = (s * s) * _jax.random.uniform(kv, w.shape, _jnp.float32, 0.5, 1.5)
    if N_MICROBATCH > 1:
        for name, axis in PER_EXAMPLE_BATCH_AXIS.items():
            out[name] = _to_microbatches(out[name], axis)
    return {'x': out['x'], 'positions': out['positions'], 'norm_mix_g': out['norm_mix_g'], 'w_in': out['w_in'], 'b_fgate': out['b_fgate'], 'q_norm_g': out['q_norm_g'], 'w_uq': out['w_uq'], 'kv_norm_g': out['kv_norm_g'], 'w_ukv': out['w_ukv'], 'fox_out_g': out['fox_out_g'], 'mla_out_g': out['mla_out_g'], 'w_o': out['w_o'], 'norm_ffn_g': out['norm_ffn_g'], 'w_gate': out['w_gate'], 'w_up': out['w_up'], 'w_down': out['w_down'], 'final_norm_g': out['final_norm_g'], 'loss_target': out['loss_target'], 'm_norm_mix_g': out['m_norm_mix_g'], 'm_w_in': out['m_w_in'], 'm_b_fgate': out['m_b_fgate'], 'm_q_norm_g': out['m_q_norm_g'], 'm_w_uq': out['m_w_uq'], 'm_kv_norm_g': out['m_kv_norm_g'], 'm_w_ukv': out['m_w_ukv'], 'm_fox_out_g': out['m_fox_out_g'], 'm_mla_out_g': out['m_mla_out_g'], 'm_w_o': out['m_w_o'], 'm_norm_ffn_g': out['m_norm_ffn_g'], 'm_w_gate': out['m_w_gate'], 'm_w_up': out['m_w_up'], 'm_w_down': out['m_w_down'], 'm_final_norm_g': out['m_final_norm_g'], 'v_norm_mix_g': out['v_norm_mix_g'], 'v_w_in': out['v_w_in'], 'v_b_fgate': out['v_b_fgate'], 'v_q_norm_g': out['v_q_norm_g'], 'v_w_uq': out['v_w_uq'], 'v_kv_norm_g': out['v_kv_norm_g'], 'v_w_ukv': out['v_w_ukv'], 'v_fox_out_g': out['v_fox_out_g'], 'v_mla_out_g': out['v_mla_out_g'], 'v_w_o': out['v_w_o'], 'v_norm_ffn_g': out['v_norm_ffn_g'], 'v_w_gate': out['v_w_gate'], 'v_w_up': out['v_w_up'], 'v_w_down': out['v_w_down'], 'v_final_norm_g': out['v_final_norm_g']}


def _loss(weights, diff, rest, loss_target):
    with _jax.named_scope("forward"):
        args = {**rest, TWIN_DIFF_INPUT: diff, **{k: w.astype(_WEIGHT_DTYPES[k]) for k, w in weights.items()}}
        y = _forward(args)
    with _jax.named_scope("loss_head"):
        err = _jnp.square(y.astype(_jnp.float32) - loss_target)
        return 0.5 * _jnp.sum(_jnp.mean(err, axis=-1)) if err.ndim else 0.5 * err


def _adamw(w, g, m, v):
    m = ADAM_B1 * m + (1.0 - ADAM_B1) * g
    v = ADAM_B2 * v + (1.0 - ADAM_B2) * _jnp.square(g)
    m_hat = m / (1.0 - ADAM_B1 ** ADAM_STEP)
    v_hat = v / (1.0 - ADAM_B2 ** ADAM_STEP)
    delta = -ADAM_LR * (m_hat / (_jnp.sqrt(v_hat) + ADAM_EPS) + ADAM_WD * w)
    return delta, m, v


def reference(x, positions, norm_mix_g, w_in, b_fgate, q_norm_g, w_uq, kv_norm_g, w_ukv, fox_out_g, mla_out_g, w_o, norm_ffn_g, w_gate, w_up, w_down, final_norm_g, loss_target, m_norm_mix_g, m_w_in, m_b_fgate, m_q_norm_g, m_w_uq, m_kv_norm_g, m_w_ukv, m_fox_out_g, m_mla_out_g, m_w_o, m_norm_ffn_g, m_w_gate, m_w_up, m_w_down, m_final_norm_g, v_norm_mix_g, v_w_in, v_b_fgate, v_q_norm_g, v_w_uq, v_kv_norm_g, v_w_ukv, v_fox_out_g, v_mla_out_g, v_w_o, v_norm_ffn_g, v_w_gate, v_w_up, v_w_down, v_final_norm_g):
    given = dict(x=x, positions=positions, norm_mix_g=norm_mix_g, w_in=w_in, b_fgate=b_fgate, q_norm_g=q_norm_g, w_uq=w_uq, kv_norm_g=kv_norm_g, w_ukv=w_ukv, fox_out_g=fox_out_g, mla_out_g=mla_out_g, w_o=w_o, norm_ffn_g=norm_ffn_g, w_gate=w_gate, w_up=w_up, w_down=w_down, final_norm_g=final_norm_g, loss_target=loss_target, m_norm_mix_g=m_norm_mix_g, m_w_in=m_w_in, m_b_fgate=m_b_fgate, m_q_norm_g=m_q_norm_g, m_w_uq=m_w_uq, m_kv_norm_g=m_kv_norm_g, m_w_ukv=m_w_ukv, m_fox_out_g=m_fox_out_g, m_mla_out_g=m_mla_out_g, m_w_o=m_w_o, m_norm_ffn_g=m_norm_ffn_g, m_w_gate=m_w_gate, m_w_up=m_w_up, m_w_down=m_w_down, m_final_norm_g=m_final_norm_g, v_norm_mix_g=v_norm_mix_g, v_w_in=v_w_in, v_b_fgate=v_b_fgate, v_q_norm_g=v_q_norm_g, v_w_uq=v_w_uq, v_kv_norm_g=v_kv_norm_g, v_w_ukv=v_w_ukv, v_fox_out_g=v_fox_out_g, v_mla_out_g=v_mla_out_g, v_w_o=v_w_o, v_norm_ffn_g=v_norm_ffn_g, v_w_gate=v_w_gate, v_w_up=v_w_up, v_w_down=v_w_down, v_final_norm_g=v_final_norm_g)
    weights = {n: given[n] for n in TWIN_WEIGHTS}
    shared = {n: given[n] for n in SHARED_INPUTS}
    per_example = {n: given[n] for n in ['x', 'positions']}
    grad_fn = _jax.value_and_grad(_loss, argnums=(0, 1))

    def one_microbatch(ex, loss_target):
        ex = dict(ex)
        diff = ex.pop(TWIN_DIFF_INPUT)
        return grad_fn(weights, diff, {**shared, **ex}, loss_target)

    if N_MICROBATCH == 1:
        loss, (grad_w, grad_x) = one_microbatch(per_example, given["loss_target"])
    else:
        def body(carry, xs):
            loss_sum, grad_sum = carry
            l_k, (gw_k, gx_k) = one_microbatch(xs[0], xs[1])
            with _jax.named_scope("update"):
                return (loss_sum + l_k, _jax.tree.map(_jnp.add, grad_sum, gw_k)), gx_k

        init = (_jnp.zeros((), _jnp.float32), _jax.tree.map(_jnp.zeros_like, weights))
        (loss, grad_w), grad_x = _jax.lax.scan(body, init, (per_example, given["loss_target"]))
    with _jax.named_scope("update"):
        delta_w, new_m, new_v = {}, {}, {}
        for n in TWIN_WEIGHTS:
            delta_w[n], new_m[n], new_v[n] = _adamw(weights[n], grad_w[n], given["m_" + n], given["v_" + n])
    return (loss, grad_x, *[grad_w[n] for n in TWIN_WEIGHTS], *[delta_w[n] for n in TWIN_WEIGHTS],
            *[new_m[n] for n in TWIN_WEIGHTS], *[new_v[n] for n in TWIN_WEIGHTS])
```

```python
import jax
import jax.numpy as jnp
from jax import lax
from jax.experimental import pallas as pl
from jax.experimental.pallas import tpu as pltpu

F32 = jnp.float32
BF16 = jnp.bfloat16
MESH = pl.DeviceIdType.MESH

N_DEV = 8
HEADS = 8
FOX_DIM = 64
MLA_NOPE = 64
MLA_ROPE = 32
MLA_QK = MLA_NOPE + MLA_ROPE
MLA_V = 64
ROPE_THETA = 10000.0
NORM_EPS = 1e-6
ADAM_LR, ADAM_B1, ADAM_B2, ADAM_EPS, ADAM_WD, ADAM_STEP = 0.001, 0.9, 0.999, 1e-08, 0.01, 10

LANES = 128
MASKED = -1e30
VMEM_LIMIT = 48 * 1024 * 1024

_DIMS = {"nn": (((1,), (0,)), ((), ())), "nt": (((1,), (1,)), ((), ())), "tn": (((0,), (0,)), ((), ()))}


def _params(*sem):
    return pltpu.CompilerParams(dimension_semantics=sem, vmem_limit_bytes=VMEM_LIMIT)


def _dot(a, b, mode):
    return lax.dot_general(a.astype(BF16), b.astype(BF16), _DIMS[mode], preferred_element_type=F32)


def _tile(n, pref, unit=8):
    if n <= pref:
        return n
    t = pref - pref % unit
    while n % t:
        t -= unit
    return t


def _matmul(a, b, mode, out_dtype, name, tm=512, tn=512, tk=None, res=None):
    if mode == "nn":
        (m, kd), n = a.shape, b.shape[1]
    elif mode == "nt":
        (m, kd), n = a.shape, b.shape[0]
    else:
        (kd, m), n = a.shape, b.shape[1]
    tm, tn = _tile(m, tm, LANES if mode == "tn" else 16), _tile(n, tn, LANES)
    tk = kd if tk is None else _tile(kd, tk, LANES)
    nk = kd // tk
    a_spec = pl.BlockSpec((tk, tm), lambda i, j, k: (k, i)) if mode == "tn" else pl.BlockSpec((tm, tk), lambda i, j, k: (i, k))
    b_spec = pl.BlockSpec((tn, tk), lambda i, j, k: (j, k)) if mode == "nt" else pl.BlockSpec((tk, tn), lambda i, j, k: (k, j))
    o_spec = pl.BlockSpec((tm, tn), lambda i, j, k: (i, j))
    has_res = res is not None

    def body(*refs):
        a_ref, b_ref = refs[:2]
        r_ref = refs[2] if has_res else None
        o_ref = refs[3] if has_res else refs[2]

        def finish(acc):
            if has_res:
                acc = acc + r_ref[...]
            o_ref[...] = acc.astype(out_dtype)

        part = _dot(a_ref[...], b_ref[...], mode)
        if nk == 1:
            finish(part)
        else:
            acc_ref = refs[-1]
            k = pl.program_id(2)

            @pl.when(k == 0)
            def _():
                acc_ref[...] = part

            @pl.when(k > 0)
            def _():
                acc_ref[...] += part

            @pl.when(k == nk - 1)
            def _():
                finish(acc_ref[...])

    return pl.pallas_call(
        body, name=name, grid=(m // tm, n // tn, nk),
        in_specs=[a_spec, b_spec] + ([o_spec] if has_res else []), out_specs=o_spec,
        out_shape=jax.ShapeDtypeStruct((m, n), out_dtype),
        scratch_shapes=[pltpu.VMEM((tm, tn), F32)] if nk > 1 else [],
        compiler_params=_params("parallel", "parallel", "arbitrary"),
    )(*([a, b] + ([res] if has_res else [])))


def _rstd(x):
    return lax.rsqrt(jnp.mean(x * x, axis=-1, keepdims=True) + NORM_EPS)


def _norm_bwd(x, g, dy):
    r = _rstd(x)
    xh = x * r
    u = dy * g
    dx = r * (u - xh * jnp.mean(u * xh, axis=-1, keepdims=True))
    return dx, jnp.sum(dy * xh, axis=0, keepdims=True)


def _rmsnorm(x, col, width, g, out_dtype, name):
    t = x.shape[0]
    tm = _tile(t, 512)

    def body(x_ref, g_ref, o_ref):
        xv = x_ref[...]
        o_ref[...] = ((xv * _rstd(xv)) * g_ref[...]).astype(out_dtype)

    return pl.pallas_call(
        body, name=name, grid=(t // tm,),
        in_specs=[pl.BlockSpec((tm, width), lambda i: (i, col)), pl.BlockSpec((1, width), lambda i: (0, 0))],
        out_specs=pl.BlockSpec((tm, width), lambda i: (i, 0)),
        out_shape=jax.ShapeDtypeStruct((t, width), out_dtype),
        compiler_params=_params("parallel"),
    )(x, g)


def _rmsnorm_bwd(x, col, width, g, dy, name, res=None):
    t = x.shape[0]
    tm = _tile(t, 512)
    has_res = res is not None
    row = pl.BlockSpec((tm, width), lambda i: (i, 0))
    vec = pl.BlockSpec((1, width), lambda i: (0, 0))

    def body(*refs):
        x_ref, g_ref, dy_ref = refs[:3]
        dx_ref, dg_ref = refs[-2:]
        dx, dg = _norm_bwd(x_ref[...], g_ref[...], dy_ref[...])
        if has_res:
            dx = dx + refs[3][...]
        dx_ref[...] = dx

        @pl.when(pl.program_id(0) == 0)
        def _():
            dg_ref[...] = jnp.zeros_like(dg_ref)

        dg_ref[...] += dg

    return pl.pallas_call(
        body, name=name, grid=(t // tm,),
        in_specs=[pl.BlockSpec((tm, width), lambda i: (i, col)), vec, row] + ([row] if has_res else []),
        out_specs=(row, vec),
        out_shape=(jax.ShapeDtypeStruct((t, width), F32), jax.ShapeDtypeStruct((1, width), F32)),
        compiler_params=_params("arbitrary"),
    )(*([x, g, dy] + ([res] if has_res else [])))


def _out_norm(fo, mo, gf, gm, name):
    t, w = fo.shape
    tm = _tile(t, 512)
    row = pl.BlockSpec((tm, w), lambda i: (i, 0))
    vec = pl.BlockSpec((1, w), lambda i: (0, 0))

    def body(fo_ref, mo_ref, gf_ref, gm_ref, o_ref):
        f, m = fo_ref[...], mo_ref[...]
        o_ref[:, :w] = ((f * _rstd(f)) * gf_ref[...]).astype(BF16)
        o_ref[:, w:] = ((m * _rstd(m)) * gm_ref[...]).astype(BF16)

    return pl.pallas_call(
        body, name=name, grid=(t // tm,), in_specs=[row, row, vec, vec],
        out_specs=pl.BlockSpec((tm, 2 * w), lambda i: (i, 0)),
        out_shape=jax.ShapeDtypeStruct((t, 2 * w), BF16),
        compiler_params=_params("parallel"),
    )(fo, mo, gf, gm)


def _split3(x):
    hi = x.astype(BF16)
    r1 = x - hi.astype(F32)
    mid = r1.astype(BF16)
    lo = (r1 - mid.astype(F32)).astype(BF16)
    return hi, mid, lo


def _dot_x01(x, m01):
    hi, mid, lo = _split3(x)
    d = lambda p: lax.dot_general(p, m01, _DIMS["nn"], preferred_element_type=F32)
    return (d(lo) + d(mid)) + d(hi)


def _dot_01x(m01, x):
    hi, mid, lo = _split3(x)
    d = lambda p: lax.dot_general(m01, p, _DIMS["nn"], preferred_element_type=F32)
    return (d(lo) + d(mid)) + d(hi)


def _out_norm_bwd(fo, mo, gf, gm, dcat, head_dim, name):
    t, w = fo.shape
    nh = w // head_dim
    tm = _tile(t, 512)
    row = pl.BlockSpec((tm, w), lambda i: (i, 0))
    vec = pl.BlockSpec((1, w), lambda i: (0, 0))
    hrow = pl.BlockSpec((tm, nh), lambda i: (i, 0))

    def body(fo_ref, mo_ref, gf_ref, gm_ref, dc_ref, dfo_ref, dmo_ref, ff_ref, fm_ref, dgf_ref, dgm_ref):
        lane_head = lax.shift_right_logical(lax.broadcasted_iota(jnp.int32, (w, nh), 0), _log2(head_dim))
        sel = (lane_head == lax.broadcasted_iota(jnp.int32, (w, nh), 1)).astype(BF16)
        f, m = fo_ref[...], mo_ref[...]
        dfo, dgf = _norm_bwd(f, gf_ref[...], dc_ref[:, :w])
        dmo, dgm = _norm_bwd(m, gm_ref[...], dc_ref[:, w:])
        dfo_ref[...] = dfo.astype(BF16)
        dmo_ref[...] = dmo.astype(BF16)
        ff_ref[...] = _dot_x01(dfo * f, sel)
        fm_ref[...] = _dot_x01(dmo * m, sel)

        @pl.when(pl.program_id(0) == 0)
        def _():
            dgf_ref[...] = jnp.zeros_like(dgf_ref)
            dgm_ref[...] = jnp.zeros_like(dgm_ref)

        dgf_ref[...] += dgf
        dgm_ref[...] += dgm

    return pl.pallas_call(
        body, name=name, grid=(t // tm,),
        in_specs=[row, row, vec, vec, pl.BlockSpec((tm, 2 * w), lambda i: (i, 0))],
        out_specs=(row, row, hrow, hrow, vec, vec),
        out_shape=(jax.ShapeDtypeStruct((t, w), BF16), jax.ShapeDtypeStruct((t, w), BF16),
                   jax.ShapeDtypeStruct((t, nh), F32), jax.ShapeDtypeStruct((t, nh), F32),
                   jax.ShapeDtypeStruct((1, w), F32), jax.ShapeDtypeStruct((1, w), F32)),
        compiler_params=_params("arbitrary"),
    )(fo, mo, gf, gm, dcat)


def _loss_bwd(x, tgt, g, name):
    t, d = x.shape
    tm = _tile(t, 512)
    row = pl.BlockSpec((tm, d), lambda i: (i, 0))
    vec = pl.BlockSpec((1, d), lambda i: (0, 0))
    one = pl.BlockSpec((1, 1), lambda i: (0, 0))

    def body(x_ref, t_ref, g_ref, dx_ref, dg_ref, loss_ref):
        xv, gv = x_ref[...], g_ref[...]
        diff = (xv * _rstd(xv)) * gv - t_ref[...]
        dx, dg = _norm_bwd(xv, gv, diff / d)
        dx_ref[...] = dx

        @pl.when(pl.program_id(0) == 0)
        def _():
            dg_ref[...] = jnp.zeros_like(dg_ref)
            loss_ref[...] = jnp.zeros_like(loss_ref)

        dg_ref[...] += dg
        loss_ref[...] += 0.5 * jnp.sum(jnp.mean(diff * diff, axis=-1, keepdims=True), axis=0, keepdims=True)

    return pl.pallas_call(
        body, name=name, grid=(t // tm,), in_specs=[row, row, vec], out_specs=(row, vec, one),
        out_shape=(jax.ShapeDtypeStruct((t, d), F32), jax.ShapeDtypeStruct((1, d), F32), jax.ShapeDtypeStruct((1, 1), F32)),
        compiler_params=_params("arbitrary"),
    )(x, tgt, g)


def _swiglu(gu, name):
    t, f2 = gu.shape
    f = f2 // 2
    tm = _tile(t, 256)

    def body(g_ref, u_ref, o_ref):
        g = g_ref[...]
        o_ref[...] = ((g * jax.nn.sigmoid(g)) * u_ref[...]).astype(BF16)

    return pl.pallas_call(
        body, name=name, grid=(t // tm,),
        in_specs=[pl.BlockSpec((tm, f), lambda i: (i, 0)), pl.BlockSpec((tm, f), lambda i: (i, 1))],
        out_specs=pl.BlockSpec((tm, f), lambda i: (i, 0)),
        out_shape=jax.ShapeDtypeStruct((t, f), BF16),
        compiler_params=_params("parallel"),
    )(gu, gu)


def _swiglu_bwd(da, gu, name):
    t, f2 = gu.shape
    f = f2 // 2
    tm = _tile(t, 256)
    col = lambda c: pl.BlockSpec((tm, f), lambda i: (i, c))

    def body(da_ref, g_ref, u_ref, o_ref):
        dav, g = da_ref[...], g_ref[...]
        sg = jax.nn.sigmoid(g)
        o_ref[:, :f] = (dav * u_ref[...] * (sg * (1.0 + g * (1.0 - sg)))).astype(BF16)
        o_ref[:, f:] = (dav * (g * sg)).astype(BF16)

    return pl.pallas_call(
        body, name=name, grid=(t // tm,), in_specs=[col(0), col(0), col(1)],
        out_specs=pl.BlockSpec((tm, f2), lambda i: (i, 0)),
        out_shape=jax.ShapeDtypeStruct((t, f2), BF16),
        compiler_params=_params("parallel"),
    )(da, gu, gu)


def _log2(n):
    assert n & (n - 1) == 0
    return n.bit_length() - 1


def _chunk_scan_mats(rows, grp, reverse):
    ii = lax.broadcasted_iota(jnp.int32, (LANES, LANES), 0)
    jj = lax.broadcasted_iota(jnp.int32, (LANES, LANES), 1)
    within = ((ii >= jj) if reverse else (ii <= jj)).astype(BF16)
    ones = jnp.ones((LANES, LANES), BF16)
    ri = lax.broadcasted_iota(jnp.int32, (rows, rows), 0)
    rj = lax.broadcasted_iota(jnp.int32, (rows, rows), 1)
    sh = _log2(grp)
    same = lax.shift_right_logical(ri, sh) == lax.shift_right_logical(rj, sh)
    across = (same & ((rj > ri) if reverse else (rj < ri))).astype(BF16)
    return within, ones, across


def _running_sum(v, mats):
    within, ones, across = mats
    return _dot_x01(v, within) + _dot_01x(across, _dot_x01(v, ones))


def _fgate(z, bcol, grp, name):
    rows = z.shape[0]

    def body(z_ref, b_ref, c_ref):
        zz = z_ref[...] + b_ref[...]
        log_f = jnp.minimum(zz, 0.0) - jnp.log1p(jnp.exp(-jnp.abs(zz)))
        c_ref[...] = _running_sum(log_f, _chunk_scan_mats(rows, grp, False))

    return pl.pallas_call(body, name=name, out_shape=jax.ShapeDtypeStruct(z.shape, F32),
                          compiler_params=pltpu.CompilerParams(vmem_limit_bytes=VMEM_LIMIT))(z, bcol)


def _fgate_bwd(z, bcol, row_sums, col_sums, grp, name):
    rows = z.shape[0]

    def body(z_ref, b_ref, rs_ref, cs_ref, dz_ref, db_ref):
        zz = z_ref[...] + b_ref[...]
        dz = _running_sum(rs_ref[...] - cs_ref[...], _chunk_scan_mats(rows, grp, True)) * jax.nn.sigmoid(-zz)
        dz_ref[...] = dz
        head = lax.shift_right_logical(lax.broadcasted_iota(jnp.int32, (HEADS, rows), 1), _log2(grp)) & (HEADS - 1)
        sel = (head == lax.broadcasted_iota(jnp.int32, (HEADS, rows), 0)).astype(BF16)
        db_ref[...] = jnp.sum(_dot_01x(sel, dz), axis=1, keepdims=True)

    return pl.pallas_call(
        body, name=name,
        out_shape=(jax.ShapeDtypeStruct(z.shape, F32), jax.ShapeDtypeStruct((HEADS, 1), F32)),
        compiler_params=pltpu.CompilerParams(vmem_limit_bytes=VMEM_LIMIT),
    )(z, bcol, row_sums, col_sums)


def _rope(x1, c1, x2, c2, cs, sn, width, out_dtype, name):
    t = cs.shape[0]
    tm = _tile(t, 1024)
    row = pl.BlockSpec((tm, width), lambda i: (i, 0))

    def body(a_ref, b_ref, c_ref, s_ref, o1_ref, o2_ref):
        a, b, c, s = a_ref[...].astype(F32), b_ref[...].astype(F32), c_ref[...], s_ref[...]
        o1_ref[...] = (a * c - b * s).astype(out_dtype)
        o2_ref[...] = (b * c + a * s).astype(out_dtype)

    return pl.pallas_call(
        body, name=name, grid=(t // tm,),
        in_specs=[pl.BlockSpec((tm, width), lambda i: (i, c1)), pl.BlockSpec((tm, width), lambda i: (i, c2)), row, row],
        out_specs=(row, row),
        out_shape=(jax.ShapeDtypeStruct((t, width), out_dtype), jax.ShapeDtypeStruct((t, width), out_dtype)),
        compiler_params=_params("parallel"),
    )(x1, x2, cs, sn)


def _rope_bwd_shared(d1, d2, cs, sn, name):
    b, h, s, w = d1.shape
    ts = _tile(s, 1024)
    hs = pl.BlockSpec((1, h, ts, w), lambda i, j: (i, 0, j, 0))
    row = pl.BlockSpec((1, ts, w), lambda i, j: (i, j, 0))

    def body(a_ref, b_ref, c_ref, s_ref, o1_ref, o2_ref):
        a, bb = jnp.sum(a_ref[0], axis=0), jnp.sum(b_ref[0], axis=0)
        c, sv = c_ref[0], s_ref[0]
        o1_ref[0] = a * c + bb * sv
        o2_ref[0] = bb * c - a * sv

    return pl.pallas_call(
        body, name=name, grid=(b, s // ts), in_specs=[hs, hs, row, row], out_specs=(row, row),
        out_shape=(jax.ShapeDtypeStruct((b, s, w), F32), jax.ShapeDtypeStruct((b, s, w), F32)),
        compiler_params=_params("parallel", "parallel"),
    )(d1, d2, cs, sn)


def _attn_fwd(q, k, v, cq, ck, scale, tq, name):
    bh, s, dk = q.shape
    dv = v.shape[-1]
    nq = s // tq
    bias = cq is not None

    def body(*refs):
        q_ref, k_ref, v_ref = refs[:3]
        o_ref, lse_ref = refs[-2:]
        i = pl.program_id(1)
        qv = q_ref[0]
        cqv = refs[3][0] if bias else None

        def tile(j, carry, diagonal):
            m, l, acc = carry
            off = pl.multiple_of(j * tq, tq)
            sc = _dot(qv, k_ref[0, pl.ds(off, tq), :], "nt") * scale
            if bias:
                sc = sc + (cqv - refs[4][0, j])
            if diagonal:
                keep = lax.broadcasted_iota(jnp.int32, (tq, tq), 1) <= lax.broadcasted_iota(jnp.int32, (tq, tq), 0)
                sc = jnp.where(keep, sc, MASKED)
            m_new = jnp.maximum(m, jnp.max(sc, axis=1, keepdims=True))
            alpha = jnp.exp(m - m_new)
            p = jnp.exp(sc - m_new)
            l = alpha * l + jnp.sum(p, axis=1, keepdims=True)
            acc = alpha * acc + _dot(p, v_ref[0, pl.ds(off, tq), :], "nn")
            return m_new, l, acc

        init = (jnp.full((tq, 1), MASKED, F32), jnp.zeros((tq, 1), F32), jnp.zeros((tq, dv), F32))
        carry = lax.fori_loop(0, i, lambda j, c: tile(j, c, False), init)
        m, l, acc = tile(i, carry, True)
        o_ref[0] = acc / l
        lse_ref[0] = m + jnp.log(l)

    head = lambda w: pl.BlockSpec((1, s, w), lambda b, i: (b, 0, 0))
    in_specs = [pl.BlockSpec((1, tq, dk), lambda b, i: (b, i, 0)), head(dk), head(dv)]
    args = [q, k, v]
    if bias:
        in_specs += [pl.BlockSpec((1, tq, 1), lambda b, i: (b, i, 0)), pl.BlockSpec((1, nq, 1, tq), lambda b, i: (b, 0, 0, 0))]
        args += [cq, ck]
    return pl.pallas_call(
        body, name=name, grid=(bh, nq), in_specs=in_specs,
        out_specs=(pl.BlockSpec((1, tq, dv), lambda b, i: (b, i, 0)), pl.BlockSpec((1, tq, 1), lambda b, i: (b, i, 0))),
        out_shape=(jax.ShapeDtypeStruct((bh, s, dv), F32), jax.ShapeDtypeStruct((bh, s, 1), F32)),
        compiler_params=_params("parallel", "parallel"),
    )(*args)


def _attn_bwd(q, k, v, do, lse, delta, cq, ck, scale, tq, name):
    bh, s, dk = q.shape
    dv = v.shape[-1]
    nq = s // tq
    bias = cq is not None

    def body(*refs):
        q_ref, k_ref, v_ref, do_ref, lse_ref, dl_ref = refs[:6]
        cq_ref, ck_ref = (refs[6], refs[7]) if bias else (None, None)
        outs = refs[8:] if bias else refs[6:]
        dq_ref, dk_ref, dv_ref = outs[:3]
        dq_ref[...] = jnp.zeros_like(dq_ref)
        if bias:
            outs[4][...] = jnp.zeros_like(outs[4])

        def kv_tile(j, _):
            koff = pl.multiple_of(j * tq, tq)
            kj = k_ref[0, pl.ds(koff, tq), :]
            vj = v_ref[0, pl.ds(koff, tq), :]
            ckj = ck_ref[0, pl.ds(koff, tq), :] if bias else None

            def q_tile(i, carry, diagonal):
                dk_a, dv_a, dc_a = carry
                qoff = pl.multiple_of(i * tq, tq)
                qi = q_ref[0, pl.ds(qoff, tq), :]
                doi = do_ref[0, pl.ds(qoff, tq), :]
                st = _dot(kj, qi, "nt") * scale
                if bias:
                    st = st + (cq_ref[0, i] - ckj)
                if diagonal:
                    keep = lax.broadcasted_iota(jnp.int32, (tq, tq), 1) >= lax.broadcasted_iota(jnp.int32, (tq, tq), 0)
                    st = jnp.where(keep, st, MASKED)
                pt = jnp.exp(st - lse_ref[0, i])
                dst = pt * (_dot(vj, doi, "nt") - dl_ref[0, i])
                dv_a = dv_a + _dot(pt, doi, "nn")
                dsb = (dst * scale).astype(BF16)
                dk_a = dk_a + _dot(dsb, qi, "nn")
                dq_ref[0, pl.ds(qoff, tq), :] += _dot(dsb, kj, "tn")
                if bias:
                    dc_a = dc_a + jnp.sum(dst, axis=1, keepdims=True)
                    outs[4][0, i] += jnp.sum(dst, axis=0, keepdims=True)
                return dk_a, dv_a, dc_a

            carry = q_tile(j, (jnp.zeros((tq, dk), F32), jnp.zeros((tq, dv), F32), jnp.zeros((tq, 1), F32)), True)
            dk_a, dv_a, dc_a = lax.fori_loop(j + 1, nq, lambda i, c: q_tile(i, c, False), carry)
            dk_ref[0, pl.ds(koff, tq), :] = dk_a
            dv_ref[0, pl.ds(koff, tq), :] = dv_a
            if bias:
                outs[3][0, pl.ds(koff, tq), :] = dc_a
            return 0

        lax.fori_loop(0, nq, kv_tile, 0)

    head = lambda w: pl.BlockSpec((1, s, w), lambda b: (b, 0, 0))
    rows = pl.BlockSpec((1, nq, 1, tq), lambda b: (b, 0, 0, 0))
    in_specs = [head(dk), head(dk), head(dv), head(dv), rows, rows]
    args = [q, k, v, do, lse, delta]
    out_specs = [head(dk), head(dk), head(dv)]
    out_shape = [jax.ShapeDtypeStruct((bh, s, dk), F32), jax.ShapeDtypeStruct((bh, s, dk), F32), jax.ShapeDtypeStruct((bh, s, dv), F32)]
    if bias:
        in_specs += [rows, head(1)]
        args += [cq, ck]
        out_specs += [head(1), rows]
        out_shape += [jax.ShapeDtypeStruct((bh, s, 1), F32), jax.ShapeDtypeStruct((bh, nq, 1, tq), F32)]
    return pl.pallas_call(
        body, name=name, grid=(bh,), in_specs=in_specs, out_specs=tuple(out_specs), out_shape=tuple(out_shape),
        compiler_params=_params("parallel"),
    )(*args)


def _my_place():
    return lax.axis_index("x"), lax.axis_index("y"), lax.axis_index("c")


def _flip(p, bit):
    return 1 - p if bit else p


def _relative(x, y, c, k):
    return _flip(x, k & 4), _flip(y, k & 2), _flip(c, k & 1)


def _linear(x, y, c):
    return 4 * x + 2 * y + c


def _all_gather(shard, name):
    r, cdim = shard.shape

    def body(x_ref, out_ref, send_sems, recv_sems, local_sem):
        x, y, c = _my_place()
        me, sibling = (x, y, c), (x, y, 1 - c)
        chips = [(1 - x, y), (x, 1 - y), (1 - x, 1 - y)]

        def slot(px, py, pc):
            return out_ref.at[_linear(px, py, pc)]

        def copy(k, block, to, src=None):
            return pltpu.make_async_remote_copy(
                src_ref=slot(*block) if src is None else src, dst_ref=slot(*block),
                send_sem=send_sems.at[k], recv_sem=recv_sems.at[k], device_id=to, device_id_type=MESH)

        mine = pltpu.make_async_copy(x_ref, slot(*me), local_sem)
        mine.start()
        first = [copy(0, me, sibling, src=x_ref)] + [copy(1 + j, me, (*chip, c), src=x_ref) for j, chip in enumerate(chips)]
        for cp in first:
            cp.start()
        passed = [copy(4 + j, (*chip, c), sibling) for j, chip in enumerate(chips)]
        for j, chip in enumerate(chips):
            copy(1 + j, (*chip, c), me).wait_recv()
            passed[j].start()
        copy(0, sibling, me).wait_recv()
        for j, chip in enumerate(chips):
            copy(4 + j, (*chip, 1 - c), me).wait_recv()
        for cp in first + passed:
            cp.wait_send()
        mine.wait()

    return pl.pallas_call(
        body, name=name, out_shape=jax.ShapeDtypeStruct((N_DEV, r, cdim), shard.dtype),
        in_specs=[pl.BlockSpec(memory_space=pl.ANY)], out_specs=pl.BlockSpec(memory_space=pl.ANY),
        scratch_shapes=[pltpu.SemaphoreType.DMA((7,)), pltpu.SemaphoreType.DMA((7,)), pltpu.SemaphoreType.DMA(())],
    )(shard)


def _exchange(parts, name):
    def body(p_ref, out_ref, send_sems, recv_sems, local_sem):
        x, y, c = _my_place()
        me = _linear(x, y, c)
        mine = pltpu.make_async_copy(p_ref.at[me], out_ref.at[me], local_sem)
        mine.start()
        copies = []
        for k in range(1, N_DEV):
            peer = _relative(x, y, c, k)
            copies.append(pltpu.make_async_remote_copy(
                src_ref=p_ref.at[_linear(*peer)], dst_ref=out_ref.at[me],
                send_sem=send_sems.at[k - 1], recv_sem=recv_sems.at[k - 1], device_id=peer, device_id_type=MESH))
        for cp in copies:
            cp.start()
        for cp in copies:
            cp.wait_recv()
        for cp in copies:
            cp.wait_send()
        mine.wait()

    return pl.pallas_call(
        body, name=name, out_shape=jax.ShapeDtypeStruct(parts.shape, parts.dtype),
        in_specs=[pl.BlockSpec(memory_space=pl.ANY)], out_specs=pl.BlockSpec(memory_space=pl.ANY),
        scratch_shapes=[pltpu.SemaphoreType.DMA((7,)), pltpu.SemaphoreType.DMA((7,)), pltpu.SemaphoreType.DMA(())],
    )(parts)


def _sum_blocks(parts, name):
    n, r, cdim = parts.shape
    tr = _tile(r, 256)

    def body(p_ref, o_ref):
        acc = p_ref[0].astype(F32)
        for d in range(1, n):
            acc = acc + p_ref[d].astype(F32)
        o_ref[...] = acc

    return pl.pallas_call(
        body, name=name, grid=(r // tr,), in_specs=[pl.BlockSpec((n, tr, cdim), lambda i: (0, i, 0))],
        out_specs=pl.BlockSpec((tr, cdim), lambda i: (i, 0)), out_shape=jax.ShapeDtypeStruct((r, cdim), F32),
        compiler_params=_params("parallel"),
    )(parts)


def _adamw_math(w, g, m, v):
    m = ADAM_B1 * m + (1.0 - ADAM_B1) * g
    v = ADAM_B2 * v + (1.0 - ADAM_B2) * (g * g)
    m_hat = m / (1.0 - ADAM_B1 ** ADAM_STEP)
    v_hat = v / (1.0 - ADAM_B2 ** ADAM_STEP)
    delta = -ADAM_LR * (m_hat / (jnp.sqrt(v_hat) + ADAM_EPS) + ADAM_WD * w)
    return delta, m, v


def _adamw(w, g, m, v, name):
    def body(w_ref, g_ref, m_ref, v_ref, d_ref, nm_ref, nv_ref):
        d_ref[...], nm_ref[...], nv_ref[...] = _adamw_math(w_ref[...], g_ref[...], m_ref[...], v_ref[...])

    out = jax.ShapeDtypeStruct(w.shape, F32)
    return pl.pallas_call(body, name=name, out_shape=(out, out, out),
                          compiler_params=pltpu.CompilerParams(vmem_limit_bytes=VMEM_LIMIT))(w, g, m, v)


def _small_all_reduce_adamw(part, w, m, v, name):
    width = part.shape[1]

    def body(p_ref, w_ref, m_ref, v_ref, tot_ref, d_ref, nm_ref, nv_ref, rows, send_sems, recv_sems):
        x, y, c = _my_place()
        me = _linear(x, y, c)
        rows[me] = p_ref[...]
        copies = []
        for k in range(1, N_DEV):
            copies.append(pltpu.make_async_remote_copy(
                src_ref=rows.at[me], dst_ref=rows.at[me], send_sem=send_sems.at[k - 1], recv_sem=recv_sems.at[k - 1],
                device_id=_relative(x, y, c, k), device_id_type=MESH))
        for cp in copies:
            cp.start()
        for cp in copies:
            cp.wait_recv()
        for cp in copies:
            cp.wait_send()
        total = rows[0]
        for d in range(1, N_DEV):
            total = total + rows[d]
        tot_ref[...] = total
        d_ref[...], nm_ref[...], nv_ref[...] = _adamw_math(w_ref[...], total, m_ref[...], v_ref[...])

    out = jax.ShapeDtypeStruct((1, width), F32)
    return pl.pallas_call(
        body, name=name, out_shape=(out, out, out, out),
        scratch_shapes=[pltpu.VMEM((N_DEV, 1, width), F32), pltpu.SemaphoreType.DMA((7,)), pltpu.SemaphoreType.DMA((7,))],
    )(part, w, m, v)


def _pad_rows(a, rows):
    return jnp.pad(a, ((0, rows - a.shape[0]), (0, 0)))


def _pad_lanes(a):
    return jnp.pad(a, ((0, 0), (0, -a.shape[1] % LANES)))


def kernel(x, positions, norm_mix_g, w_in, b_fgate, q_norm_g, w_uq, kv_norm_g, w_ukv, fox_out_g, mla_out_g, w_o, norm_ffn_g, w_gate, w_up, w_down, final_norm_g, loss_target, m_norm_mix_g, m_w_in, m_b_fgate, m_q_norm_g, m_w_uq, m_kv_norm_g, m_w_ukv, m_fox_out_g, m_mla_out_g, m_w_o, m_norm_ffn_g, m_w_gate, m_w_up, m_w_down, m_final_norm_g, v_norm_mix_g, v_w_in, v_b_fgate, v_q_norm_g, v_w_uq, v_kv_norm_g, v_w_ukv, v_fox_out_g, v_mla_out_g, v_w_o, v_norm_ffn_g, v_w_gate, v_w_up, v_w_down, v_final_norm_g):
    bl, s, d = x.shape
    t = bl * s
    bh = bl * HEADS
    tq = _tile(s, 256)
    nq = s // tq
    grp = s // LANES
    fw = HEADS * FOX_DIM
    q_rank, kv_rank = w_uq.shape[1], w_ukv.shape[1]
    in_cols = w_in.shape[2]
    n_in = N_DEV * in_cols
    ff = N_DEV * w_gate.shape[2]
    small = n_in - 3 * fw - HEADS
    half = MLA_ROPE // 2

    tr = lambda w: jnp.transpose(w[0])
    in_rows = -(-in_cols // 16) * 16
    uq_rows = w_uq.shape[2] * q_rank // d
    ukv_rows = w_ukv.shape[2] * kv_rank // d
    pieces = [_pad_rows(tr(w_in), in_rows), _pad_rows(tr(w_uq).reshape(uq_rows, d), -(-uq_rows // 16) * 16),
              tr(w_ukv).reshape(ukv_rows, d), w_o[0], tr(w_gate), tr(w_up), w_down[0]]
    offs = [0]
    for p in pieces:
        offs.append(offs[-1] + p.shape[0])
    packed_rows = -(-offs[-1] // 256) * 256
    shard = _pad_rows(jnp.concatenate(pieces, axis=0), packed_rows).astype(BF16)
    gathered = _all_gather(shard, "gather_weights")

    def full(i, rows):
        return gathered[:, offs[i]:offs[i] + rows]

    w_in_t = full(0, in_cols).reshape(n_in, d)
    w_in_r = jnp.concatenate([w_in_t[:3 * fw], w_in_t[3 * fw + HEADS:], w_in_t[3 * fw:3 * fw + HEADS],
                              jnp.zeros((4 * fw - n_in, d), BF16)], axis=0)
    w_uq_h = full(1, uq_rows).reshape(HEADS, MLA_QK, q_rank)
    w_uq_p = jnp.concatenate([w_uq_h[:, :MLA_NOPE].reshape(-1, q_rank), w_uq_h[:, MLA_NOPE:MLA_NOPE + half].reshape(-1, q_rank),
                              w_uq_h[:, MLA_NOPE + half:].reshape(-1, q_rank)], axis=0)
    w_ukv_h = full(2, ukv_rows).reshape(HEADS, MLA_NOPE + MLA_V, kv_rank)
    w_ukv_p = jnp.concatenate([w_ukv_h[:, :MLA_NOPE].reshape(-1, kv_rank), w_ukv_h[:, MLA_NOPE:].reshape(-1, kv_rank)], axis=0)
    w_o_f = full(3, w_o.shape[1]).reshape(-1, d)
    w_gu_t = jnp.concatenate([full(4, ff // N_DEV).reshape(ff, d), full(5, ff // N_DEV).reshape(ff, d)], axis=0)
    w_down_f = full(6, ff // N_DEV).reshape(ff, d)

    def heads(a, hd):
        return a.reshape(bl, s, HEADS, hd).transpose(0, 2, 1, 3).reshape(bh, s, hd)

    def unheads(a):
        return a.reshape(bl, HEADS, s, -1).transpose(0, 2, 1, 3).reshape(t, -1)

    def per_query_rows(a):
        return a.reshape(bl, s, HEADS).transpose(0, 2, 1).reshape(bh, nq, 1, tq)

    x2d = x.reshape(t, d)
    h1 = _rmsnorm(x2d, 0, d, norm_mix_g, BF16, "norm_mix")
    proj_a = _matmul(h1, w_in_r[:3 * fw], "nt", BF16, "proj_fox")
    proj_b = _matmul(h1, w_in_r[3 * fw:], "nt", F32, "proj_mla")
    o_kvlat, o_krope, o_flogit = q_rank, q_rank + kv_rank, small

    fq, fk, fv = (heads(proj_a[:, i * fw:(i + 1) * fw], FOX_DIM) for i in range(3))
    z = proj_b[:, o_flogit:o_flogit + HEADS].reshape(bl, s, HEADS).transpose(0, 2, 1).reshape(bh * grp, LANES)
    bcol = jnp.broadcast_to(b_fgate.reshape(1, HEADS, 1), (bl, HEADS, grp)).reshape(bh * grp, 1)
    c = _fgate(z, bcol, grp, "forget_gate")
    c_col, c_rows = c.reshape(bh, s, 1), c.reshape(bh, nq, 1, tq)
    fox_o, fox_lse = _attn_fwd(fq, fk, fv, c_col, c_rows, FOX_DIM ** -0.5, tq, "fox_attention")

    qn = _rmsnorm(proj_b, 0, q_rank, q_norm_g, BF16, "norm_q")
    kvn = _rmsnorm(proj_b, o_kvlat // kv_rank, kv_rank, kv_norm_g, BF16, "norm_kv")
    q_all = _matmul(qn, w_uq_p, "nt", F32, "up_q")
    kv_all = _matmul(kvn, w_ukv_p, "nt", BF16, "up_kv")
    inv_freq = ROPE_THETA ** (-jnp.arange(0, MLA_ROPE, 2, dtype=F32) / MLA_ROPE)
    ang = positions.astype(F32).reshape(t, 1) * inv_freq[None, :]
    cos, sin = jnp.cos(ang), jnp.sin(ang)
    cos_h, sin_h = jnp.tile(cos, (1, HEADS)), jnp.tile(sin, (1, HEADS))
    pw = HEADS * half
    q_r1, q_r2 = _rope(q_all, fw // pw, q_all, fw // pw + 1, cos_h, sin_h, pw, BF16, "rope_q")
    k_r1, k_r2 = _rope(proj_b[:, o_krope:o_krope + half], 0, proj_b[:, o_krope + half:o_krope + 2 * half], 0,
                       cos, sin, half, BF16, "rope_k")
    by_head = lambda a, hd: a.reshape(bl, s, HEADS, hd)
    shared = lambda a: jnp.broadcast_to(a.reshape(bl, s, 1, half), (bl, s, HEADS, half))
    to_bh = lambda a: a.transpose(0, 2, 1, 3).reshape(bh, s, -1)
    mq = to_bh(jnp.concatenate([by_head(q_all[:, :fw].astype(BF16), MLA_NOPE), by_head(q_r1, half), by_head(q_r2, half)], axis=-1))
    mk = to_bh(jnp.concatenate([by_head(kv_all[:, :fw], MLA_NOPE), shared(k_r1), shared(k_r2)], axis=-1))
    mv = heads(kv_all[:, fw:], MLA_V)
    mla_o, mla_lse = _attn_fwd(mq, mk, mv, None, None, MLA_QK ** -0.5, tq, "mla_attention")

    fox_o2, mla_o2 = unheads(fox_o), unheads(mla_o)
    cat = _out_norm(fox_o2, mla_o2, fox_out_g, mla_out_g, "norm_out")
    x1 = _matmul(cat, w_o_f, "nn", F32, "proj_out", res=x2d)
    h2 = _rmsnorm(x1, 0, d, norm_ffn_g, BF16, "norm_ffn")
    gu = _matmul(h2, w_gu_t, "nt", F32, "ffn_gate_up")
    act = _swiglu(gu, "swiglu")
    x2 = _matmul(act, w_down_f, "nn", F32, "ffn_down", res=x1)
    dx2, dg_final, loss_part = _loss_bwd(x2, loss_target.reshape(t, d), final_norm_g.reshape(1, d), "final_norm_loss")

    d_act = _matmul(dx2, w_down_f, "nt", F32, "d_ffn_down", tn=ff // 2)
    dgu = _swiglu_bwd(d_act, gu, "d_swiglu")
    dw_down = _matmul(act, dx2, "tn", F32, "dw_down", tn=d, tk=1024)
    dh2 = _matmul(dgu, w_gu_t, "nn", F32, "d_ffn_gate_up")
    dw_gu = _matmul(dgu, h2, "tn", F32, "dw_gate_up", tn=d, tk=1024)
    dx1, dg_ffn = _rmsnorm_bwd(x1, 0, d, norm_ffn_g, dh2, "d_norm_ffn", res=dx2)
    dcat = _matmul(dx1, w_o_f, "nt", F32, "d_proj_out")
    dw_o = _matmul(cat, dx1, "tn", F32, "dw_o", tn=d, tk=1024)
    d_fox_o, d_mla_o, fox_delta, mla_delta, dg_fox, dg_mla = _out_norm_bwd(
        fox_o2, mla_o2, fox_out_g, mla_out_g, dcat, FOX_DIM, "d_norm_out")

    as_rows = lambda a: a.reshape(bh, nq, 1, tq)
    dfq, dfk, dfv, ds_cols, ds_rows = _attn_bwd(fq, fk, fv, heads(d_fox_o, FOX_DIM), as_rows(fox_lse), per_query_rows(fox_delta),
                                                c_rows, c_col, FOX_DIM ** -0.5, tq, "d_fox_attention")
    dz, db_fgate = _fgate_bwd(z, bcol, ds_rows.reshape(bh * grp, LANES), ds_cols.reshape(bh * grp, LANES), grp, "d_forget_gate")
    d_flogit = dz.reshape(bl, HEADS, s).transpose(0, 2, 1).reshape(t, HEADS)

    dmq, dmk, dmv = _attn_bwd(mq, mk, mv, heads(d_mla_o, MLA_V), as_rows(mla_lse), per_query_rows(mla_delta),
                              None, None, MLA_QK ** -0.5, tq, "d_mla_attention")
    dmq4 = dmq.reshape(bl, HEADS, s, MLA_QK).transpose(0, 2, 1, 3)
    dq_r1, dq_r2 = _rope(dmq4[..., MLA_NOPE:MLA_NOPE + half].reshape(t, pw), 0, dmq4[..., MLA_NOPE + half:].reshape(t, pw), 0,
                         cos_h, -sin_h, pw, BF16, "d_rope_q")
    dq_all = jnp.concatenate([dmq4[..., :MLA_NOPE].reshape(t, fw).astype(BF16), dq_r1, dq_r2], axis=1)
    dqn = _matmul(dq_all, w_uq_p, "nn", F32, "d_up_q")
    dw_uq_p = _matmul(dq_all, qn, "tn", F32, "dw_uq", tn=q_rank, tk=1024)
    dq_lat, dg_q = _rmsnorm_bwd(proj_b, 0, q_rank, q_norm_g, dqn, "d_norm_q")
    dmk4 = dmk.reshape(bl, HEADS, s, MLA_QK)
    dk_r1, dk_r2 = _rope_bwd_shared(dmk4[..., MLA_NOPE:MLA_NOPE + half], dmk4[..., MLA_NOPE + half:],
                                    cos.reshape(bl, s, half), sin.reshape(bl, s, half), "d_rope_k")
    dkv_all = jnp.concatenate([unheads(dmk4[..., :MLA_NOPE]), unheads(dmv)], axis=1).astype(BF16)
    dkvn = _matmul(dkv_all, w_ukv_p, "nn", F32, "d_up_kv")
    dw_ukv_p = _matmul(dkv_all, kvn, "tn", F32, "dw_ukv", tn=kv_rank, tk=1024)
    dkv_lat, dg_kv = _rmsnorm_bwd(proj_b, o_kvlat // kv_rank, kv_rank, kv_norm_g, dkvn, "d_norm_kv")

    dproj = jnp.concatenate(
        [unheads(dfq).astype(BF16), unheads(dfk).astype(BF16), unheads(dfv).astype(BF16), dq_lat.astype(BF16),
         dkv_lat.astype(BF16), dk_r1.reshape(t, half).astype(BF16), dk_r2.reshape(t, half).astype(BF16),
         d_flogit.astype(BF16), jnp.zeros((t, 4 * fw - n_in), BF16)], axis=1)
    dh1 = _matmul(dproj, w_in_r, "nn", F32, "d_proj_in")
    dw_in_r = _matmul(dproj, h1, "tn", F32, "dw_in", tn=d, tk=1024)
    grad_x, dg_mix = _rmsnorm_bwd(x2d, 0, d, norm_mix_g, dh1, "d_norm_mix", res=dx1)

    dw_in_t = jnp.concatenate([dw_in_r[:3 * fw], dw_in_r[3 * fw + small:n_in], dw_in_r[3 * fw:3 * fw + small]], axis=0)
    per_dev = lambda a: a.reshape(N_DEV, -1, d)
    pad_dev = lambda a, rows: jnp.pad(a, ((0, 0), (0, rows - a.shape[1]), (0, 0)))
    dw_uq_h = jnp.concatenate([dw_uq_p[:fw].reshape(HEADS, MLA_NOPE, q_rank), dw_uq_p[fw:fw + pw].reshape(HEADS, half, q_rank),
                               dw_uq_p[fw + pw:].reshape(HEADS, half, q_rank)], axis=1)
    dw_ukv_h = jnp.concatenate([dw_ukv_p[:fw].reshape(HEADS, MLA_NOPE, kv_rank), dw_ukv_p[fw:].reshape(HEADS, MLA_V, kv_rank)], axis=1)
    grads = [pad_dev(per_dev(dw_in_t), pieces[0].shape[0]), pad_dev(per_dev(dw_uq_h), pieces[1].shape[0]), per_dev(dw_ukv_h),
             per_dev(dw_o), per_dev(dw_gu[:ff]), per_dev(dw_gu[ff:]), per_dev(dw_down)]
    parts = pad_dev(jnp.concatenate(grads, axis=1), packed_rows).astype(BF16)
    g_shard = _sum_blocks(_exchange(parts, "exchange_grads"), "sum_grads")

    def mine(i, rows):
        return g_shard[offs[i]:offs[i] + rows]

    big = [
        ("w_in", w_in, m_w_in, v_w_in, mine(0, in_cols).T),
        ("w_uq", w_uq, m_w_uq, v_w_uq, mine(1, uq_rows).reshape(-1, q_rank).T),
        ("w_ukv", w_ukv, m_w_ukv, v_w_ukv, mine(2, ukv_rows).reshape(-1, kv_rank).T),
        ("w_o", w_o, m_w_o, v_w_o, mine(3, w_o.shape[1])),
        ("w_gate", w_gate, m_w_gate, v_w_gate, mine(4, ff // N_DEV).T),
        ("w_up", w_up, m_w_up, v_w_up, mine(5, ff // N_DEV).T),
        ("w_down", w_down, m_w_down, v_w_down, mine(6, ff // N_DEV)),
    ]
    out = {}
    for nm, w, m, v, g in big:
        dl, new_m, new_v = _adamw(w[0], g, m[0], v[0], "adamw_" + nm)
        out[nm] = (g[None], dl[None], new_m[None], new_v[None])

    smalls = [("norm_mix_g", norm_mix_g, m_norm_mix_g, v_norm_mix_g, dg_mix),
              ("b_fgate", b_fgate, m_b_fgate, v_b_fgate, db_fgate.reshape(1, HEADS)),
              ("q_norm_g", q_norm_g, m_q_norm_g, v_q_norm_g, dg_q),
              ("kv_norm_g", kv_norm_g, m_kv_norm_g, v_kv_norm_g, dg_kv),
              ("fox_out_g", fox_out_g, m_fox_out_g, v_fox_out_g, dg_fox),
              ("mla_out_g", mla_out_g, m_mla_out_g, v_mla_out_g, dg_mla),
              ("norm_ffn_g", norm_ffn_g, m_norm_ffn_g, v_norm_ffn_g, dg_ffn),
              ("final_norm_g", final_norm_g, m_final_norm_g, v_final_norm_g, dg_final)]
    pack = lambda arrs: jnp.concatenate([_pad_lanes(a.reshape(1, -1)) for a in arrs], axis=1)
    blank = jnp.zeros((1, 1), F32)
    totals = _small_all_reduce_adamw(
        pack([e[4] for e in smalls] + [loss_part]), pack([e[1] for e in smalls] + [blank]),
        pack([e[2] for e in smalls] + [blank]), pack([e[3] for e in smalls] + [blank]), "reduce_small_adamw")
    pos = 0
    for nm, w, _, _, _ in smalls:
        out[nm] = tuple(a[0, pos:pos + w.size].reshape(w.shape) for a in totals)
        pos += -(-w.size // LANES) * LANES
    loss = totals[0][0, pos]

    order = ["norm_mix_g", "w_in", "b_fgate", "q_norm_g", "w_uq", "kv_norm_g", "w_ukv", "fox_out_g", "mla_out_g", "w_o",
             "norm_ffn_g", "w_gate", "w_up", "w_down", "final_norm_g"]
    return (loss, grad_x.reshape(bl, s, d), *[out[n][0] for n in order], *[out[n][1] for n in order],
            *[out[n][2] for n in order], *[out[n][3] for n in order])
```

```python
import jax
import jax.numpy as jnp
from jax import lax
from jax.experimental import pallas as pl
from jax.experimental.pallas import tpu as pltpu

F32 = jnp.float32
BF16 = jnp.bfloat16
MESH = pl.DeviceIdType.MESH

N_DEV = 8
HEADS = 8
HEAD_DIM = 64
PAIRS = HEADS // 2
MLA_ROPE = 32
MLA_QK = HEAD_DIM + MLA_ROPE
ROPE_THETA = 10000.0
NORM_EPS = 1e-6
ADAM_LR, ADAM_B1, ADAM_B2, ADAM_EPS, ADAM_WD, ADAM_STEP = 0.001, 0.9, 0.999, 1e-08, 0.01, 10

LANES = 128
MASKED = -1e30
VMEM_LIMIT = 48 * 1024 * 1024

_DIMS = {"nn": (((1,), (0,)), ((), ())), "nt": (((1,), (1,)), ((), ())), "tn": (((0,), (0,)), ((), ()))}


def _params(*sem):
    return pltpu.CompilerParams(dimension_semantics=sem, vmem_limit_bytes=VMEM_LIMIT)


def _dot(a, b, mode):
    return lax.dot_general(a.astype(BF16), b.astype(BF16), _DIMS[mode], preferred_element_type=F32)


def _tile(n, pref, unit=8):
    if n <= pref:
        return n
    t = pref - pref % unit
    while n % t:
        t -= unit
    return t


def _log2(n):
    assert n & (n - 1) == 0
    return n.bit_length() - 1


def _matmul(a, b, mode, out_dtype, name, tm=512, tn=512, tk=None, res=None):
    if mode == "nn":
        (m, kd), n = a.shape, b.shape[1]
    elif mode == "nt":
        (m, kd), n = a.shape, b.shape[0]
    else:
        (kd, m), n = a.shape, b.shape[1]
    tm, tn = _tile(m, tm, LANES if mode == "tn" else 16), _tile(n, tn, LANES)
    tk = kd if tk is None else _tile(kd, tk, LANES)
    nk = kd // tk
    a_spec = pl.BlockSpec((tk, tm), lambda i, j, k: (k, i)) if mode == "tn" else pl.BlockSpec((tm, tk), lambda i, j, k: (i, k))
    b_spec = pl.BlockSpec((tn, tk), lambda i, j, k: (j, k)) if mode == "nt" else pl.BlockSpec((tk, tn), lambda i, j, k: (k, j))
    o_spec = pl.BlockSpec((tm, tn), lambda i, j, k: (i, j))
    has_res = res is not None

    def body(*refs):
        a_ref, b_ref = refs[:2]
        r_ref = refs[2] if has_res else None
        o_ref = refs[3] if has_res else refs[2]

        def finish(acc):
            if has_res:
                acc = acc + r_ref[...]
            o_ref[...] = acc.astype(out_dtype)

        part = _dot(a_ref[...], b_ref[...], mode)
        if nk == 1:
            finish(part)
        else:
            acc_ref = refs[-1]
            k = pl.program_id(2)

            @pl.when(k == 0)
            def _():
                acc_ref[...] = part

            @pl.when(k > 0)
            def _():
                acc_ref[...] += part

            @pl.when(k == nk - 1)
            def _():
                finish(acc_ref[...])

    return pl.pallas_call(
        body, name=name, grid=(m // tm, n // tn, nk),
        in_specs=[a_spec, b_spec] + ([o_spec] if has_res else []), out_specs=o_spec,
        out_shape=jax.ShapeDtypeStruct((m, n), out_dtype),
        scratch_shapes=[pltpu.VMEM((tm, tn), F32)] if nk > 1 else [],
        compiler_params=_params("parallel", "parallel", "arbitrary"),
    )(*([a, b] + ([res] if has_res else [])))


def _rstd(x):
    return lax.rsqrt(jnp.mean(x * x, axis=-1, keepdims=True) + NORM_EPS)


def _norm_bwd(x, g, dy):
    r = _rstd(x)
    xh = x * r
    u = dy * g
    dx = r * (u - xh * jnp.mean(u * xh, axis=-1, keepdims=True))
    return dx, jnp.sum(dy * xh, axis=0, keepdims=True)


def _rmsnorm(x, col, width, g, out_dtype, name):
    t = x.shape[0]
    tm = _tile(t, 512)

    def body(x_ref, g_ref, o_ref):
        xv = x_ref[...]
        o_ref[...] = ((xv * _rstd(xv)) * g_ref[...]).astype(out_dtype)

    return pl.pallas_call(
        body, name=name, grid=(t // tm,),
        in_specs=[pl.BlockSpec((tm, width), lambda i: (i, col)), pl.BlockSpec((1, width), lambda i: (0, 0))],
        out_specs=pl.BlockSpec((tm, width), lambda i: (i, 0)),
        out_shape=jax.ShapeDtypeStruct((t, width), out_dtype),
        compiler_params=_params("parallel"),
    )(x, g)


def _rmsnorm_bwd(x, col, width, g, dy, name, res=None):
    t = x.shape[0]
    tm = _tile(t, 512)
    has_res = res is not None
    row = pl.BlockSpec((tm, width), lambda i: (i, 0))
    vec = pl.BlockSpec((1, width), lambda i: (0, 0))

    def body(*refs):
        x_ref, g_ref, dy_ref = refs[:3]
        dx_ref, dg_ref = refs[-2:]
        dx, dg = _norm_bwd(x_ref[...], g_ref[...], dy_ref[...])
        if has_res:
            dx = dx + refs[3][...]
        dx_ref[...] = dx

        @pl.when(pl.program_id(0) == 0)
        def _():
            dg_ref[...] = jnp.zeros_like(dg_ref)

        dg_ref[...] += dg

    return pl.pallas_call(
        body, name=name, grid=(t // tm,),
        in_specs=[pl.BlockSpec((tm, width), lambda i: (i, col)), vec, row] + ([row] if has_res else []),
        out_specs=(row, vec),
        out_shape=(jax.ShapeDtypeStruct((t, width), F32), jax.ShapeDtypeStruct((1, width), F32)),
        compiler_params=_params("arbitrary"),
    )(*([x, g, dy] + ([res] if has_res else [])))


def _out_norm(fo, mo, gf, gm, name):
    t, w = fo.shape
    tm = _tile(t, 512)
    row = pl.BlockSpec((tm, w), lambda i: (i, 0))
    vec = pl.BlockSpec((1, w), lambda i: (0, 0))

    def body(fo_ref, mo_ref, gf_ref, gm_ref, o_ref):
        f, m = fo_ref[...], mo_ref[...]
        o_ref[:, :w] = ((f * _rstd(f)) * gf_ref[...]).astype(BF16)
        o_ref[:, w:] = ((m * _rstd(m)) * gm_ref[...]).astype(BF16)

    return pl.pallas_call(
        body, name=name, grid=(t // tm,), in_specs=[row, row, vec, vec],
        out_specs=pl.BlockSpec((tm, 2 * w), lambda i: (i, 0)),
        out_shape=jax.ShapeDtypeStruct((t, 2 * w), BF16),
        compiler_params=_params("parallel"),
    )(fo, mo, gf, gm)


def _split3(x):
    hi = x.astype(BF16)
    r1 = x - hi.astype(F32)
    mid = r1.astype(BF16)
    lo = (r1 - mid.astype(F32)).astype(BF16)
    return hi, mid, lo


def _dot_x01(x, m01):
    hi, mid, lo = _split3(x)
    d = lambda p: lax.dot_general(p, m01, _DIMS["nn"], preferred_element_type=F32)
    return (d(lo) + d(mid)) + d(hi)


def _dot_01x(m01, x):
    hi, mid, lo = _split3(x)
    d = lambda p: lax.dot_general(m01, p, _DIMS["nn"], preferred_element_type=F32)
    return (d(lo) + d(mid)) + d(hi)


def _out_norm_bwd(fo, mo, gf, gm, dcat, name):
    t, w = fo.shape
    nh = w // HEAD_DIM
    tm = _tile(t, 512)
    row = pl.BlockSpec((tm, w), lambda i: (i, 0))
    vec = pl.BlockSpec((1, w), lambda i: (0, 0))
    hrow = pl.BlockSpec((tm, nh), lambda i: (i, 0))

    def body(fo_ref, mo_ref, gf_ref, gm_ref, dc_ref, dfo_ref, dmo_ref, ff_ref, fm_ref, dgf_ref, dgm_ref):
        lane_head = lax.shift_right_logical(lax.broadcasted_iota(jnp.int32, (w, nh), 0), _log2(HEAD_DIM))
        sel = (lane_head == lax.broadcasted_iota(jnp.int32, (w, nh), 1)).astype(BF16)
        f, m = fo_ref[...], mo_ref[...]
        dfo, dgf = _norm_bwd(f, gf_ref[...], dc_ref[:, :w])
        dmo, dgm = _norm_bwd(m, gm_ref[...], dc_ref[:, w:])
        dfo_ref[...] = dfo.astype(BF16)
        dmo_ref[...] = dmo.astype(BF16)
        ff_ref[...] = _dot_x01(dfo * f, sel)
        fm_ref[...] = _dot_x01(dmo * m, sel)

        @pl.when(pl.program_id(0) == 0)
        def _():
            dgf_ref[...] = jnp.zeros_like(dgf_ref)
            dgm_ref[...] = jnp.zeros_like(dgm_ref)

        dgf_ref[...] += dgf
        dgm_ref[...] += dgm

    return pl.pallas_call(
        body, name=name, grid=(t // tm,),
        in_specs=[row, row, vec, vec, pl.BlockSpec((tm, 2 * w), lambda i: (i, 0))],
        out_specs=(row, row, hrow, hrow, vec, vec),
        out_shape=(jax.ShapeDtypeStruct((t, w), BF16), jax.ShapeDtypeStruct((t, w), BF16),
                   jax.ShapeDtypeStruct((t, nh), F32), jax.ShapeDtypeStruct((t, nh), F32),
                   jax.ShapeDtypeStruct((1, w), F32), jax.ShapeDtypeStruct((1, w), F32)),
        compiler_params=_params("arbitrary"),
    )(fo, mo, gf, gm, dcat)


def _loss_bwd(x, tgt, g, name):
    t, d = x.shape
    tm = _tile(t, 512)
    row = pl.BlockSpec((tm, d), lambda i: (i, 0))
    vec = pl.BlockSpec((1, d), lambda i: (0, 0))
    one = pl.BlockSpec((1, 1), lambda i: (0, 0))

    def body(x_ref, t_ref, g_ref, dx_ref, dg_ref, loss_ref):
        xv, gv = x_ref[...], g_ref[...]
        diff = (xv * _rstd(xv)) * gv - t_ref[...]
        dx, dg = _norm_bwd(xv, gv, diff / d)
        dx_ref[...] = dx

        @pl.when(pl.program_id(0) == 0)
        def _():
            dg_ref[...] = jnp.zeros_like(dg_ref)
            loss_ref[...] = jnp.zeros_like(loss_ref)

        dg_ref[...] += dg
        loss_ref[...] += 0.5 * jnp.sum(jnp.mean(diff * diff, axis=-1, keepdims=True), axis=0, keepdims=True)

    return pl.pallas_call(
        body, name=name, grid=(t // tm,), in_specs=[row, row, vec], out_specs=(row, vec, one),
        out_shape=(jax.ShapeDtypeStruct((t, d), F32), jax.ShapeDtypeStruct((1, d), F32), jax.ShapeDtypeStruct((1, 1), F32)),
        compiler_params=_params("arbitrary"),
    )(x, tgt, g)


def _swiglu(gu, name):
    t, f2 = gu.shape
    f = f2 // 2
    tm = _tile(t, 256)

    def body(g_ref, u_ref, o_ref):
        g = g_ref[...]
        o_ref[...] = ((g * jax.nn.sigmoid(g)) * u_ref[...]).astype(BF16)

    return pl.pallas_call(
        body, name=name, grid=(t // tm,),
        in_specs=[pl.BlockSpec((tm, f), lambda i: (i, 0)), pl.BlockSpec((tm, f), lambda i: (i, 1))],
        out_specs=pl.BlockSpec((tm, f), lambda i: (i, 0)),
        out_shape=jax.ShapeDtypeStruct((t, f), BF16),
        compiler_params=_params("parallel"),
    )(gu, gu)


def _swiglu_bwd(da, gu, name):
    t, f2 = gu.shape
    f = f2 // 2
    tm = _tile(t, 256)
    col = lambda c: pl.BlockSpec((tm, f), lambda i: (i, c))

    def body(da_ref, g_ref, u_ref, o_ref):
        dav, g = da_ref[...], g_ref[...]
        sg = jax.nn.sigmoid(g)
        o_ref[:, :f] = (dav * u_ref[...] * (sg * (1.0 + g * (1.0 - sg)))).astype(BF16)
        o_ref[:, f:] = (dav * (g * sg)).astype(BF16)

    return pl.pallas_call(
        body, name=name, grid=(t // tm,), in_specs=[col(0), col(0), col(1)],
        out_specs=pl.BlockSpec((tm, f2), lambda i: (i, 0)),
        out_shape=jax.ShapeDtypeStruct((t, f2), BF16),
        compiler_params=_params("parallel"),
    )(da, gu, gu)


def _chunk_scan_mats(rows, grp, reverse):
    ii = lax.broadcasted_iota(jnp.int32, (LANES, LANES), 0)
    jj = lax.broadcasted_iota(jnp.int32, (LANES, LANES), 1)
    within = ((ii >= jj) if reverse else (ii <= jj)).astype(BF16)
    ones = jnp.ones((LANES, LANES), BF16)
    ri = lax.broadcasted_iota(jnp.int32, (rows, rows), 0)
    rj = lax.broadcasted_iota(jnp.int32, (rows, rows), 1)
    sh = _log2(grp)
    same = lax.shift_right_logical(ri, sh) == lax.shift_right_logical(rj, sh)
    across = (same & ((rj > ri) if reverse else (rj < ri))).astype(BF16)
    return within, ones, across


def _running_sum(v, mats):
    within, ones, across = mats
    return _dot_x01(v, within) + _dot_01x(across, _dot_x01(v, ones))


def _fgate(z, bcol, grp, name):
    rows = z.shape[0]

    def body(z_ref, b_ref, c_ref):
        zz = z_ref[...] + b_ref[...]
        log_f = jnp.minimum(zz, 0.0) - jnp.log1p(jnp.exp(-jnp.abs(zz)))
        c_ref[...] = _running_sum(log_f, _chunk_scan_mats(rows, grp, False))

    return pl.pallas_call(body, name=name, out_shape=jax.ShapeDtypeStruct(z.shape, F32),
                          compiler_params=pltpu.CompilerParams(vmem_limit_bytes=VMEM_LIMIT))(z, bcol)


def _fgate_bwd(z, bcol, row_sums, col_sums, grp, name):
    rows = z.shape[0]

    def body(z_ref, b_ref, rs_ref, cs_ref, dz_ref, db_ref):
        zz = z_ref[...] + b_ref[...]
        dz = _running_sum(rs_ref[...] - cs_ref[...], _chunk_scan_mats(rows, grp, True)) * jax.nn.sigmoid(-zz)
        dz_ref[...] = dz
        head = lax.shift_right_logical(lax.broadcasted_iota(jnp.int32, (HEADS, rows), 1), _log2(grp)) & (HEADS - 1)
        sel = (head == lax.broadcasted_iota(jnp.int32, (HEADS, rows), 0)).astype(BF16)
        db_ref[...] = jnp.sum(_dot_01x(sel, dz), axis=1, keepdims=True)

    return pl.pallas_call(
        body, name=name,
        out_shape=(jax.ShapeDtypeStruct(z.shape, F32), jax.ShapeDtypeStruct((HEADS, 1), F32)),
        compiler_params=pltpu.CompilerParams(vmem_limit_bytes=VMEM_LIMIT),
    )(z, bcol, row_sums, col_sums)


def _rotate(x, cs, sn_signed):
    return x * cs + pltpu.roll(x, LANES // 2, axis=1) * sn_signed


def _rope_fwd(q_raw, nope, proj_b, k_col, cs, sn, name):
    t, qw = q_raw.shape
    tm = _tile(t, 512)
    row = pl.BlockSpec((tm, LANES), lambda i: (i, 0))

    def body(q_ref, k_ref, c_ref, s_ref, qo_ref, ko_ref):
        c, s = c_ref[...], s_ref[...]
        qo_ref[:, :nope] = q_ref[:, :nope].astype(BF16)
        for off in range(nope, qw, LANES):
            qo_ref[:, off:off + LANES] = _rotate(q_ref[:, off:off + LANES], c, s).astype(BF16)
        ko_ref[...] = _rotate(k_ref[...], c, s).astype(BF16)

    return pl.pallas_call(
        body, name=name, grid=(t // tm,),
        in_specs=[pl.BlockSpec((tm, qw), lambda i: (i, 0)), pl.BlockSpec((tm, LANES), lambda i: (i, k_col)), row, row],
        out_specs=(pl.BlockSpec((tm, qw), lambda i: (i, 0)), row),
        out_shape=(jax.ShapeDtypeStruct((t, qw), BF16), jax.ShapeDtypeStruct((t, LANES), BF16)),
        compiler_params=_params("parallel"),
    )(q_raw, proj_b, cs, sn)


def _rope_bwd(dq_pe, dk_pe, cs, sn, name):
    t, qw = dq_pe.shape
    tm = _tile(t, 512)
    row = pl.BlockSpec((tm, LANES), lambda i: (i, 0))
    wide = pl.BlockSpec((tm, qw), lambda i: (i, 0))

    def body(q_ref, k_ref, c_ref, s_ref, qo_ref, ko_ref):
        c, s = c_ref[...], -s_ref[...]
        for off in range(0, qw, LANES):
            qo_ref[:, off:off + LANES] = _rotate(q_ref[:, off:off + LANES], c, s).astype(BF16)
        ko_ref[...] = _rotate(k_ref[...], c, s)

    return pl.pallas_call(
        body, name=name, grid=(t // tm,), in_specs=[wide, row, row, row], out_specs=(wide, row),
        out_shape=(jax.ShapeDtypeStruct((t, qw), BF16), jax.ShapeDtypeStruct((t, LANES), F32)),
        compiler_params=_params("parallel"),
    )(dq_pe, dk_pe, cs, sn)


def _lane_masks(pair, h, pe):
    lane = lax.broadcasted_iota(jnp.int32, (1, LANES), 1)
    in_head = lax.shift_right_logical(lane, _log2(HEAD_DIM)) == h
    in_rope = ((lax.shift_right_logical(lane, _log2(MLA_ROPE // 2)) & 3) == ((2 * pair + h) & 3)) if pe else None
    return in_head, in_rope


def _keep(mask, v):
    return jnp.where(mask, v, jnp.zeros_like(v))


def _attn_fwd(ops, bias, scale, bl, s, tq, name):
    pe = len(ops) == 3
    nq = s // tq
    t = bl * s

    def body(*refs):
        if pe:
            q_ref, qpe_ref, kv_ref, kpe_ref = refs[:4]
            n_in = 4
            q_at = lambda r0, r1: q_ref[r0:r1, :]
            v_at = lambda r0, r1: kv_ref[r0:r1, LANES:]
            kcat = refs[-1]
            kcat[:, :LANES] = kv_ref[:, :LANES]
            kcat[:, LANES:] = kpe_ref[...]
            k_at = lambda r0, r1: kcat[r0:r1, :]
        else:
            qkv_ref = refs[0]
            n_in = 1
            q_at = lambda r0, r1: qkv_ref[r0:r1, :LANES]
            k_at = lambda r0, r1: qkv_ref[r0:r1, LANES:2 * LANES]
            v_at = lambda r0, r1: qkv_ref[r0:r1, 2 * LANES:]
        if bias:
            cq_ref, ck_ref = refs[n_in:n_in + 2]
            n_in += 2
        o_ref, lse_ref = refs[n_in:n_in + 2]
        pair = pl.program_id(1)
        causal = lax.broadcasted_iota(jnp.int32, (tq, tq), 1) <= lax.broadcasted_iota(jnp.int32, (tq, tq), 0)
        o_ref[...] = jnp.zeros_like(o_ref)

        def head(h, carry):
            in_head, in_rope = _lane_masks(pair, h, pe)
            for i in range(nq):
                r0, r1 = i * tq, (i + 1) * tq
                qm = _keep(in_head, q_at(r0, r1))
                if pe:
                    qm = jnp.concatenate([qm, _keep(in_rope, qpe_ref[r0:r1, :])], axis=1)

                def logits(k0, k1):
                    sc = _dot(qm, k_at(k0, k1), "nt") * scale
                    if bias:
                        sc = sc + (cq_ref[h, r0:r1, :] - ck_ref[h, :, k0:k1])
                    return sc

                sd = jnp.where(causal, logits(r0, r1), MASKED)
                m = jnp.max(sd, axis=1, keepdims=True)
                if i:
                    so = logits(0, r0)
                    m = jnp.maximum(m, jnp.max(so, axis=1, keepdims=True))
                pd = jnp.exp(sd - m)
                l = jnp.sum(pd, axis=1, keepdims=True)
                acc = _dot(pd, v_at(r0, r1), "nn")
                if i:
                    po = jnp.exp(so - m)
                    l = l + jnp.sum(po, axis=1, keepdims=True)
                    acc = acc + _dot(po, v_at(0, r0), "nn")
                o_ref[r0:r1, :] = jnp.where(in_head, acc / l, o_ref[r0:r1, :])
                lse_ref[h, r0:r1, :] = m + jnp.log(l)
            return carry

        lax.fori_loop(0, 2, head, 0)

    seq = lambda w, col: pl.BlockSpec((s, w), col)
    if pe:
        in_specs = [seq(LANES, lambda b, p: (b, p)), seq(LANES, lambda b, p: (b, PAIRS + p // 2)),
                    seq(2 * LANES, lambda b, p: (b, p)), seq(LANES, lambda b, p: (b, 0))]
        args = [ops[0], ops[0], ops[1], ops[2]]
        scratch = [pltpu.VMEM((s, 2 * LANES), BF16)]
    else:
        in_specs = [seq(3 * LANES, lambda b, p: (b, p))]
        args = [ops[0]]
        scratch = []
    if bias:
        in_specs += [pl.BlockSpec((2, s, 1), lambda b, p: (b * PAIRS + p, 0, 0)),
                     pl.BlockSpec((2, 1, s), lambda b, p: (b * PAIRS + p, 0, 0))]
        args += list(bias)
    return pl.pallas_call(
        body, name=name, grid=(bl, PAIRS), in_specs=in_specs,
        out_specs=(seq(LANES, lambda b, p: (b, p)), pl.BlockSpec((2, s, 1), lambda b, p: (b * PAIRS + p, 0, 0))),
        out_shape=(jax.ShapeDtypeStruct((t, HEADS * HEAD_DIM), F32), jax.ShapeDtypeStruct((bl * HEADS, s, 1), F32)),
        scratch_shapes=scratch, compiler_params=_params("parallel", "parallel"),
    )(*args)


def _attn_bwd(ops, do, lse, delta, bias, scale, bl, s, tq, name):
    pe = len(ops) == 3
    nq = s // tq
    t = bl * s
    width = 2 * LANES if pe else LANES

    def body(*refs):
        if pe:
            q_ref, qpe_ref, kv_ref, kpe_ref = refs[:4]
            n_in = 4
            k_at = lambda r0, r1: kv_ref[r0:r1, :LANES]
            v_at = lambda r0, r1: kv_ref[r0:r1, LANES:]
        else:
            qkv_ref = refs[0]
            n_in = 1
            k_at = lambda r0, r1: qkv_ref[r0:r1, LANES:2 * LANES]
            v_at = lambda r0, r1: qkv_ref[r0:r1, 2 * LANES:]
        do_ref, lse_ref, dl_ref = refs[n_in:n_in + 3]
        n_in += 3
        if bias:
            cq_ref, ck_ref = refs[n_in:n_in + 2]
            n_in += 2
        if pe:
            dqn_ref, dkv_ref, dqpe_ref, dkpe_ref, dq_acc, qcat = refs[n_in:]
            qcat[:, :LANES] = q_ref[...]
            qcat[:, LANES:] = qpe_ref[...]
            q_at = lambda r0, r1: qcat[r0:r1, :]
            dkv_ref[...] = jnp.zeros_like(dkv_ref)
        else:
            dqkv_ref, cols_ref, rows_ref, dq_acc = refs[n_in:]
            q_at = lambda r0, r1: qkv_ref[r0:r1, :LANES]
            dqkv_ref[...] = jnp.zeros_like(dqkv_ref)
            rows_ref[...] = jnp.zeros_like(rows_ref)
        pair = pl.program_id(1)
        dq_acc[...] = jnp.zeros_like(dq_acc)
        causal = lax.broadcasted_iota(jnp.int32, (tq, tq), 1) >= lax.broadcasted_iota(jnp.int32, (tq, tq), 0)

        def head(h, carry):
            in_head, in_rope = _lane_masks(pair, h, pe)
            for j in range(nq):
                r0, r1 = j * tq, (j + 1) * tq
                kt = _keep(in_head, k_at(r0, r1))
                if pe:
                    kt = jnp.concatenate([kt, _keep(in_rope, kpe_ref[r0:r1, :])], axis=1)
                vt = _keep(in_head, v_at(r0, r1))

                def block(q0, q1, diagonal):
                    qq, dd = q_at(q0, q1), do_ref[q0:q1, :]
                    st = _dot(kt, qq, "nt") * scale
                    if bias:
                        st = st + (cq_ref[h, :, q0:q1] - ck_ref[h, r0:r1, :])
                    if diagonal:
                        st = jnp.where(causal, st, MASKED)
                    pt = jnp.exp(st - lse_ref[h, :, q0:q1])
                    dst = pt * (_dot(vt, dd, "nt") - dl_ref[h, :, q0:q1])
                    dsb = (dst * scale).astype(BF16)
                    dq_acc[q0:q1, :] += _dot(dsb, kt, "tn")
                    if bias:
                        rows_ref[h, :, q0:q1] += jnp.sum(dst, axis=0, keepdims=True)
                    return _dot(pt, dd, "nn"), _dot(dsb, qq, "nn"), (jnp.sum(dst, axis=1, keepdims=True) if bias else None)

                dv_c, dk_c, cs = block(r0, r1, True)
                if r1 < s:
                    dv_o, dk_o, cs_o = block(r1, s, False)
                    dv_c, dk_c = dv_c + dv_o, dk_c + dk_o
                    cs = cs + cs_o if bias else None
                if pe:
                    dkv_ref[r0:r1, :LANES] = jnp.where(in_head, dk_c[:, :LANES].astype(BF16), dkv_ref[r0:r1, :LANES])
                    dkv_ref[r0:r1, LANES:] = jnp.where(in_head, dv_c.astype(BF16), dkv_ref[r0:r1, LANES:])
                    dkpe_ref[r0:r1, :] += _keep(in_rope, dk_c[:, LANES:])
                else:
                    dqkv_ref[r0:r1, LANES:2 * LANES] = jnp.where(in_head, dk_c.astype(BF16), dqkv_ref[r0:r1, LANES:2 * LANES])
                    dqkv_ref[r0:r1, 2 * LANES:] = jnp.where(in_head, dv_c.astype(BF16), dqkv_ref[r0:r1, 2 * LANES:])
                    cols_ref[h, r0:r1, :] = cs
            return carry

        if pe:
            @pl.when(pair == 0)
            def _():
                dkpe_ref[...] = jnp.zeros_like(dkpe_ref)

            @pl.when(pair % 2 == 0)
            def _():
                dqpe_ref[...] = jnp.zeros_like(dqpe_ref)

        lax.fori_loop(0, 2, head, 0)
        if pe:
            dqn_ref[...] = dq_acc[:, :LANES].astype(BF16)
            dqpe_ref[...] += dq_acc[:, LANES:]
        else:
            dqkv_ref[:, :LANES] = dq_acc[...].astype(BF16)

    seq = lambda w, col: pl.BlockSpec((s, w), col)
    per_head_row = pl.BlockSpec((2, 1, s), lambda b, p: (b * PAIRS + p, 0, 0))
    per_head_col = pl.BlockSpec((2, s, 1), lambda b, p: (b * PAIRS + p, 0, 0))
    if pe:
        in_specs = [seq(LANES, lambda b, p: (b, p)), seq(LANES, lambda b, p: (b, PAIRS + p // 2)),
                    seq(2 * LANES, lambda b, p: (b, p)), seq(LANES, lambda b, p: (b, 0))]
        args = [ops[0], ops[0], ops[1], ops[2]]
    else:
        in_specs = [seq(3 * LANES, lambda b, p: (b, p))]
        args = [ops[0]]
    in_specs += [seq(LANES, lambda b, p: (b, p)), per_head_row, per_head_row]
    args += [do, lse, delta]
    if bias:
        in_specs += [per_head_row, per_head_col]
        args += [bias[1], bias[0]]
    scratch = [pltpu.VMEM((s, width), F32)]
    if pe:
        out_specs = (seq(LANES, lambda b, p: (b, p)), seq(2 * LANES, lambda b, p: (b, p)),
                     seq(LANES, lambda b, p: (b, p // 2)), seq(LANES, lambda b, p: (b, 0)))
        out_shape = (jax.ShapeDtypeStruct((t, PAIRS * LANES), BF16), jax.ShapeDtypeStruct((t, PAIRS * 2 * LANES), BF16),
                     jax.ShapeDtypeStruct((t, 2 * LANES), F32), jax.ShapeDtypeStruct((t, LANES), F32))
        scratch.append(pltpu.VMEM((s, 2 * LANES), BF16))
    else:
        out_specs = (seq(3 * LANES, lambda b, p: (b, p)), per_head_col, per_head_row)
        out_shape = (jax.ShapeDtypeStruct((t, PAIRS * 3 * LANES), BF16), jax.ShapeDtypeStruct((bl * HEADS, s, 1), F32),
                     jax.ShapeDtypeStruct((bl * HEADS, 1, s), F32))
    return pl.pallas_call(
        body, name=name, grid=(bl, PAIRS), in_specs=in_specs, out_specs=out_specs, out_shape=out_shape,
        scratch_shapes=scratch, compiler_params=_params("parallel", "arbitrary"),
    )(*args)


def _my_place():
    return lax.axis_index("x"), lax.axis_index("y"), lax.axis_index("c")


def _flip(p, bit):
    return 1 - p if bit else p


def _relative(x, y, c, k):
    return _flip(x, k & 4), _flip(y, k & 2), _flip(c, k & 1)


def _linear(x, y, c):
    return 4 * x + 2 * y + c


def _all_gather(shard, name):
    r, cdim = shard.shape

    def body(x_ref, out_ref, send_sems, recv_sems, local_sem):
        x, y, c = _my_place()
        me, sibling = (x, y, c), (x, y, 1 - c)
        chips = [(1 - x, y), (x, 1 - y), (1 - x, 1 - y)]

        def slot(px, py, pc):
            return out_ref.at[_linear(px, py, pc)]

        def copy(k, block, to, src=None):
            return pltpu.make_async_remote_copy(
                src_ref=slot(*block) if src is None else src, dst_ref=slot(*block),
                send_sem=send_sems.at[k], recv_sem=recv_sems.at[k], device_id=to, device_id_type=MESH)

        mine = pltpu.make_async_copy(x_ref, slot(*me), local_sem)
        mine.start()
        first = [copy(0, me, sibling, src=x_ref)] + [copy(1 + j, me, (*chip, c), src=x_ref) for j, chip in enumerate(chips)]
        for cp in first:
            cp.start()
        passed = [copy(4 + j, (*chip, c), sibling) for j, chip in enumerate(chips)]
        for j, chip in enumerate(chips):
            copy(1 + j, (*chip, c), me).wait_recv()
            passed[j].start()
        copy(0, sibling, me).wait_recv()
        for j, chip in enumerate(chips):
            copy(4 + j, (*chip, 1 - c), me).wait_recv()
        for cp in first + passed:
            cp.wait_send()
        mine.wait()

    return pl.pallas_call(
        body, name=name, out_shape=jax.ShapeDtypeStruct((N_DEV, r, cdim), shard.dtype),
        in_specs=[pl.BlockSpec(memory_space=pl.ANY)], out_specs=pl.BlockSpec(memory_space=pl.ANY),
        scratch_shapes=[pltpu.SemaphoreType.DMA((7,)), pltpu.SemaphoreType.DMA((7,)), pltpu.SemaphoreType.DMA(())],
    )(shard)


def _exchange(parts, name):
    def body(p_ref, out_ref, send_sems, recv_sems, local_sem):
        x, y, c = _my_place()
        me = _linear(x, y, c)
        mine = pltpu.make_async_copy(p_ref.at[me], out_ref.at[me], local_sem)
        mine.start()
        copies = []
        for k in range(1, N_DEV):
            peer = _relative(x, y, c, k)
            copies.append(pltpu.make_async_remote_copy(
                src_ref=p_ref.at[_linear(*peer)], dst_ref=out_ref.at[me],
                send_sem=send_sems.at[k - 1], recv_sem=recv_sems.at[k - 1], device_id=peer, device_id_type=MESH))
        for cp in copies:
            cp.start()
        for cp in copies:
            cp.wait_recv()
        for cp in copies:
            cp.wait_send()
        mine.wait()

    return pl.pallas_call(
        body, name=name, out_shape=jax.ShapeDtypeStruct(parts.shape, parts.dtype),
        in_specs=[pl.BlockSpec(memory_space=pl.ANY)], out_specs=pl.BlockSpec(memory_space=pl.ANY),
        scratch_shapes=[pltpu.SemaphoreType.DMA((7,)), pltpu.SemaphoreType.DMA((7,)), pltpu.SemaphoreType.DMA(())],
    )(parts)


def _sum_blocks(parts, name):
    n, r, cdim = parts.shape
    tr = _tile(r, 256)

    def body(p_ref, o_ref):
        acc = p_ref[0].astype(F32)
        for d in range(1, n):
            acc = acc + p_ref[d].astype(F32)
        o_ref[...] = acc

    return pl.pallas_call(
        body, name=name, grid=(r // tr,), in_specs=[pl.BlockSpec((n, tr, cdim), lambda i: (0, i, 0))],
        out_specs=pl.BlockSpec((tr, cdim), lambda i: (i, 0)), out_shape=jax.ShapeDtypeStruct((r, cdim), F32),
        compiler_params=_params("parallel"),
    )(parts)


def _adamw_math(w, g, m, v):
    m = ADAM_B1 * m + (1.0 - ADAM_B1) * g
    v = ADAM_B2 * v + (1.0 - ADAM_B2) * (g * g)
    m_hat = m / (1.0 - ADAM_B1 ** ADAM_STEP)
    v_hat = v / (1.0 - ADAM_B2 ** ADAM_STEP)
    delta = -ADAM_LR * (m_hat / (jnp.sqrt(v_hat) + ADAM_EPS) + ADAM_WD * w)
    return delta, m, v


def _adamw(w, g, m, v, name):
    def body(w_ref, g_ref, m_ref, v_ref, d_ref, nm_ref, nv_ref):
        d_ref[...], nm_ref[...], nv_ref[...] = _adamw_math(w_ref[...], g_ref[...], m_ref[...], v_ref[...])

    out = jax.ShapeDtypeStruct(w.shape, F32)
    return pl.pallas_call(body, name=name, out_shape=(out, out, out),
                          compiler_params=pltpu.CompilerParams(vmem_limit_bytes=VMEM_LIMIT))(w, g, m, v)


def _small_all_reduce_adamw(part, w, m, v, name):
    width = part.shape[1]

    def body(p_ref, w_ref, m_ref, v_ref, tot_ref, d_ref, nm_ref, nv_ref, rows, send_sems, recv_sems):
        x, y, c = _my_place()
        me = _linear(x, y, c)
        rows[me] = p_ref[...]
        copies = []
        for k in range(1, N_DEV):
            copies.append(pltpu.make_async_remote_copy(
                src_ref=rows.at[me], dst_ref=rows.at[me], send_sem=send_sems.at[k - 1], recv_sem=recv_sems.at[k - 1],
                device_id=_relative(x, y, c, k), device_id_type=MESH))
        for cp in copies:
            cp.start()
        for cp in copies:
            cp.wait_recv()
        for cp in copies:
            cp.wait_send()
        total = rows[0]
        for d in range(1, N_DEV):
            total = total + rows[d]
        tot_ref[...] = total
        d_ref[...], nm_ref[...], nv_ref[...] = _adamw_math(w_ref[...], total, m_ref[...], v_ref[...])

    out = jax.ShapeDtypeStruct((1, width), F32)
    return pl.pallas_call(
        body, name=name, out_shape=(out, out, out, out),
        scratch_shapes=[pltpu.VMEM((N_DEV, 1, width), F32), pltpu.SemaphoreType.DMA((7,)), pltpu.SemaphoreType.DMA((7,))],
    )(part, w, m, v)


def _pad_rows(a, rows):
    return jnp.pad(a, ((0, rows - a.shape[0]), (0, 0)))


def _pad_lanes(a):
    return jnp.pad(a, ((0, 0), (0, -a.shape[1] % LANES)))


def kernel(x, positions, norm_mix_g, w_in, b_fgate, q_norm_g, w_uq, kv_norm_g, w_ukv, fox_out_g, mla_out_g, w_o, norm_ffn_g, w_gate, w_up, w_down, final_norm_g, loss_target, m_norm_mix_g, m_w_in, m_b_fgate, m_q_norm_g, m_w_uq, m_kv_norm_g, m_w_ukv, m_fox_out_g, m_mla_out_g, m_w_o, m_norm_ffn_g, m_w_gate, m_w_up, m_w_down, m_final_norm_g, v_norm_mix_g, v_w_in, v_b_fgate, v_q_norm_g, v_w_uq, v_kv_norm_g, v_w_ukv, v_fox_out_g, v_mla_out_g, v_w_o, v_norm_ffn_g, v_w_gate, v_w_up, v_w_down, v_final_norm_g):
    bl, s, d = x.shape
    t = bl * s
    bh = bl * HEADS
    tq = _tile(s, 256)
    grp = s // LANES
    fw = HEADS * HEAD_DIM
    q_rank, kv_rank = w_uq.shape[1], w_ukv.shape[1]
    in_cols = w_in.shape[2]
    n_in = N_DEV * in_cols
    ff = N_DEV * w_gate.shape[2]
    half = MLA_ROPE // 2
    o_kvlat, o_krope, o_flogit = q_rank, q_rank + kv_rank, q_rank + kv_rank + LANES
    b_cols = -(-(o_flogit + HEADS) // LANES) * LANES

    tr = lambda w: jnp.transpose(w[0])
    in_rows = -(-in_cols // 16) * 16
    uq_rows = w_uq.shape[2] * q_rank // d
    ukv_rows = w_ukv.shape[2] * kv_rank // d
    pieces = [_pad_rows(tr(w_in), in_rows), _pad_rows(tr(w_uq).reshape(uq_rows, d), -(-uq_rows // 16) * 16),
              tr(w_ukv).reshape(ukv_rows, d), w_o[0], tr(w_gate), tr(w_up), w_down[0]]
    offs = [0]
    for p in pieces:
        offs.append(offs[-1] + p.shape[0])
    packed_rows = -(-offs[-1] // 256) * 256
    shard = _pad_rows(jnp.concatenate(pieces, axis=0), packed_rows).astype(BF16)
    gathered = _all_gather(shard, "gather_weights")

    def full(i, rows):
        return gathered[:, offs[i]:offs[i] + rows]

    w_in_t = full(0, in_cols).reshape(n_in, d)
    n_qkv = 3 * fw
    w_in_a = w_in_t[:n_qkv].reshape(3, PAIRS, LANES, d).transpose(1, 0, 2, 3).reshape(n_qkv, d)
    lat0, rope0 = n_qkv + HEADS, n_qkv + HEADS + q_rank + kv_rank
    k_rep = jnp.broadcast_to(w_in_t[rope0:].reshape(2, 1, half, d), (2, 4, half, d)).reshape(LANES, d)
    w_in_b = jnp.concatenate([w_in_t[lat0:rope0], k_rep, w_in_t[n_qkv:lat0],
                              jnp.zeros((b_cols - o_flogit - HEADS, d), BF16)], axis=0)
    w_uq_h = full(1, uq_rows).reshape(HEADS, MLA_QK, q_rank)
    w_uq_pe = jnp.concatenate([w_uq_h[:, HEAD_DIM:HEAD_DIM + half].reshape(2, 1, 4 * half, q_rank),
                               w_uq_h[:, HEAD_DIM + half:].reshape(2, 1, 4 * half, q_rank)], axis=1).reshape(2 * LANES, q_rank)
    w_uq_p = jnp.concatenate([w_uq_h[:, :HEAD_DIM].reshape(fw, q_rank), w_uq_pe], axis=0)
    w_ukv_p = full(2, ukv_rows).reshape(PAIRS, 2, 2, HEAD_DIM, kv_rank).transpose(0, 2, 1, 3, 4).reshape(2 * fw, kv_rank)
    w_o_f = full(3, w_o.shape[1]).reshape(-1, d)
    w_gu_t = jnp.concatenate([full(4, ff // N_DEV).reshape(ff, d), full(5, ff // N_DEV).reshape(ff, d)], axis=0)
    w_down_f = full(6, ff // N_DEV).reshape(ff, d)

    def per_head_rows(a):
        return a.reshape(bl, s, HEADS).transpose(0, 2, 1).reshape(bh, 1, s)

    x2d = x.reshape(t, d)
    h1 = _rmsnorm(x2d, 0, d, norm_mix_g, BF16, "norm_mix")
    proj_a = _matmul(h1, w_in_a, "nt", BF16, "proj_fox", tn=3 * LANES)
    proj_b = _matmul(h1, w_in_b, "nt", F32, "proj_mla", tn=b_cols)

    z = proj_b[:, o_flogit:o_flogit + HEADS].reshape(bl, s, HEADS).transpose(0, 2, 1).reshape(bh * grp, LANES)
    bcol = jnp.broadcast_to(b_fgate.reshape(1, HEADS, 1), (bl, HEADS, grp)).reshape(bh * grp, 1)
    c = _fgate(z, bcol, grp, "forget_gate")
    c_bias = (c.reshape(bh, s, 1), c.reshape(bh, 1, s))
    fox_o, fox_lse = _attn_fwd((proj_a,), c_bias, HEAD_DIM ** -0.5, bl, s, tq, "fox_attention")

    qn = _rmsnorm(proj_b, 0, q_rank, q_norm_g, BF16, "norm_q")
    kvn = _rmsnorm(proj_b, o_kvlat // kv_rank, kv_rank, kv_norm_g, BF16, "norm_kv")
    q_raw = _matmul(qn, w_uq_p, "nt", F32, "up_q", tn=fw + 2 * LANES)
    kv_all = _matmul(kvn, w_ukv_p, "nt", BF16, "up_kv")
    inv_freq = ROPE_THETA ** (-jnp.arange(0, MLA_ROPE, 2, dtype=F32) / MLA_ROPE)
    ang = positions.astype(F32).reshape(t, 1) * inv_freq[None, :]
    cos4, sin4 = jnp.tile(jnp.cos(ang), (1, 4)), jnp.tile(jnp.sin(ang), (1, 4))
    rope_cos, rope_sin = jnp.concatenate([cos4, cos4], axis=1), jnp.concatenate([-sin4, sin4], axis=1)
    q_all, kpe = _rope_fwd(q_raw, fw, proj_b, o_krope // LANES, rope_cos, rope_sin, "rope")
    mla_ops = (q_all, kv_all, kpe)
    mla_o, mla_lse = _attn_fwd(mla_ops, None, MLA_QK ** -0.5, bl, s, tq, "mla_attention")

    cat = _out_norm(fox_o, mla_o, fox_out_g, mla_out_g, "norm_out")
    x1 = _matmul(cat, w_o_f, "nn", F32, "proj_out", res=x2d)
    h2 = _rmsnorm(x1, 0, d, norm_ffn_g, BF16, "norm_ffn")
    gu = _matmul(h2, w_gu_t, "nt", F32, "ffn_gate_up")
    act = _swiglu(gu, "swiglu")
    x2 = _matmul(act, w_down_f, "nn", F32, "ffn_down", res=x1)
    dx2, dg_final, loss_part = _loss_bwd(x2, loss_target.reshape(t, d), final_norm_g.reshape(1, d), "final_norm_loss")

    d_act = _matmul(dx2, w_down_f, "nt", F32, "d_ffn_down", tn=ff // 2)
    dgu = _swiglu_bwd(d_act, gu, "d_swiglu")
    dw_down = _matmul(act, dx2, "tn", BF16, "dw_down", tn=d, tk=1024)
    dh2 = _matmul(dgu, w_gu_t, "nn", F32, "d_ffn_gate_up")
    dw_gu = _matmul(dgu, h2, "tn", BF16, "dw_gate_up", tn=d, tk=1024)
    dx1, dg_ffn = _rmsnorm_bwd(x1, 0, d, norm_ffn_g, dh2, "d_norm_ffn", res=dx2)
    dcat = _matmul(dx1, w_o_f, "nt", F32, "d_proj_out")
    dw_o = _matmul(cat, dx1, "tn", BF16, "dw_o", tn=d, tk=1024)
    d_fox_o, d_mla_o, fox_delta, mla_delta, dg_fox, dg_mla = _out_norm_bwd(fox_o, mla_o, fox_out_g, mla_out_g, dcat, "d_norm_out")

    dproj_a, ds_cols, ds_rows = _attn_bwd((proj_a,), d_fox_o, fox_lse.reshape(bh, 1, s), per_head_rows(fox_delta),
                                          c_bias, HEAD_DIM ** -0.5, bl, s, tq, "d_fox_attention")
    dz, db_fgate = _fgate_bwd(z, bcol, ds_rows.reshape(bh * grp, LANES), ds_cols.reshape(bh * grp, LANES), grp, "d_forget_gate")
    d_flogit = dz.reshape(bl, HEADS, s).transpose(0, 2, 1).reshape(t, HEADS)

    dq_nope, dkv_all, dq_pe, dk_pe = _attn_bwd(mla_ops, d_mla_o, mla_lse.reshape(bh, 1, s), per_head_rows(mla_delta),
                                               None, MLA_QK ** -0.5, bl, s, tq, "d_mla_attention")
    dq_rot, dk_rot = _rope_bwd(dq_pe, dk_pe, rope_cos, rope_sin, "d_rope")
    dqn = _matmul(dq_rot, w_uq_p[fw:], "nn", F32, "d_up_q_rope", res=_matmul(dq_nope, w_uq_p[:fw], "nn", F32, "d_up_q_nope"))
    dw_uq_nope = _matmul(dq_nope, qn, "tn", BF16, "dw_uq_nope", tn=q_rank, tk=1024)
    dw_uq_pe = _matmul(dq_rot, qn, "tn", BF16, "dw_uq_rope", tn=q_rank, tk=1024)
    dq_lat, dg_q = _rmsnorm_bwd(proj_b, 0, q_rank, q_norm_g, dqn, "d_norm_q")
    dkvn = _matmul(dkv_all, w_ukv_p, "nn", F32, "d_up_kv")
    dw_ukv_p = _matmul(dkv_all, kvn, "tn", BF16, "dw_ukv", tn=kv_rank, tk=1024)
    dkv_lat, dg_kv = _rmsnorm_bwd(proj_b, o_kvlat // kv_rank, kv_rank, kv_norm_g, dkvn, "d_norm_kv")

    dproj_b = jnp.concatenate([dq_lat.astype(BF16), dkv_lat.astype(BF16), dk_rot.astype(BF16), d_flogit.astype(BF16),
                               jnp.zeros((t, b_cols - o_flogit - HEADS), BF16)], axis=1)
    dh1 = _matmul(dproj_b, w_in_b, "nn", F32, "d_proj_mla", res=_matmul(dproj_a, w_in_a, "nn", F32, "d_proj_fox"))
    dw_in_a = _matmul(dproj_a, h1, "tn", BF16, "dw_in_fox", tn=d, tk=1024)
    dw_in_b = _matmul(dproj_b, h1, "tn", F32, "dw_in_mla", tn=d, tk=1024)
    grad_x, dg_mix = _rmsnorm_bwd(x2d, 0, d, norm_mix_g, dh1, "d_norm_mix", res=dx1)

    dw_krope = dw_in_b[o_krope:o_flogit].reshape(2, 4, half, d).sum(axis=1).reshape(MLA_ROPE, d)
    dw_in_t = jnp.concatenate([dw_in_a.reshape(PAIRS, 3, LANES, d).transpose(1, 0, 2, 3).reshape(n_qkv, d),
                               dw_in_b[o_flogit:o_flogit + HEADS].astype(BF16), dw_in_b[:o_krope].astype(BF16),
                               dw_krope.astype(BF16)], axis=0)
    per_dev = lambda a: a.reshape(N_DEV, -1, d)
    pad_dev = lambda a, rows: jnp.pad(a, ((0, 0), (0, rows - a.shape[1]), (0, 0)))
    dw_uq_pe5 = dw_uq_pe.reshape(2, 2, 4, half, q_rank)
    dw_uq_h = jnp.concatenate([dw_uq_nope.reshape(HEADS, HEAD_DIM, q_rank), dw_uq_pe5[:, 0].reshape(HEADS, half, q_rank),
                               dw_uq_pe5[:, 1].reshape(HEADS, half, q_rank)], axis=1)
    dw_ukv_h = dw_ukv_p.reshape(PAIRS, 2, 2, HEAD_DIM, kv_rank).transpose(0, 2, 1, 3, 4).reshape(HEADS, 2 * HEAD_DIM, kv_rank)
    grads = [pad_dev(per_dev(dw_in_t), pieces[0].shape[0]), pad_dev(per_dev(dw_uq_h), pieces[1].shape[0]), per_dev(dw_ukv_h),
             per_dev(dw_o), per_dev(dw_gu[:ff]), per_dev(dw_gu[ff:]), per_dev(dw_down)]
    parts = pad_dev(jnp.concatenate(grads, axis=1), packed_rows)
    g_shard = _sum_blocks(_exchange(parts, "exchange_grads"), "sum_grads")

    def mine(i, rows):
        return g_shard[offs[i]:offs[i] + rows]

    big = [
        ("w_in", w_in, m_w_in, v_w_in, mine(0, in_cols).T),
        ("w_uq", w_uq, m_w_uq, v_w_uq, mine(1, uq_rows).reshape(-1, q_rank).T),
        ("w_ukv", w_ukv, m_w_ukv, v_w_ukv, mine(2, ukv_rows).reshape(-1, kv_rank).T),
        ("w_o", w_o, m_w_o, v_w_o, mine(3, w_o.shape[1])),
        ("w_gate", w_gate, m_w_gate, v_w_gate, mine(4, ff // N_DEV).T),
        ("w_up", w_up, m_w_up, v_w_up, mine(5, ff // N_DEV).T),
        ("w_down", w_down, m_w_down, v_w_down, mine(6, ff // N_DEV)),
    ]
    out = {}
    for nm, w, m, v, g in big:
        dl, new_m, new_v = _adamw(w[0], g, m[0], v[0], "adamw_" + nm)
        out[nm] = (g[None], dl[None], new_m[None], new_v[None])

    smalls = [("norm_mix_g", norm_mix_g, m_norm_mix_g, v_norm_mix_g, dg_mix),
              ("b_fgate", b_fgate, m_b_fgate, v_b_fgate, db_fgate.reshape(1, HEADS)),
              ("q_norm_g", q_norm_g, m_q_norm_g, v_q_norm_g, dg_q),
              ("kv_norm_g", kv_norm_g, m_kv_norm_g, v_kv_norm_g, dg_kv),
              ("fox_out_g", fox_out_g, m_fox_out_g, v_fox_out_g, dg_fox),
              ("mla_out_g", mla_out_g, m_mla_out_g, v_mla_out_g, dg_mla),
              ("norm_ffn_g", norm_ffn_g, m_norm_ffn_g, v_norm_ffn_g, dg_ffn),
              ("final_norm_g", final_norm_g, m_final_norm_g, v_final_norm_g, dg_final)]
    pack = lambda arrs: jnp.concatenate([_pad_lanes(a.reshape(1, -1)) for a in arrs], axis=1)
    blank = jnp.zeros((1, 1), F32)
    totals = _small_all_reduce_adamw(
        pack([e[4] for e in smalls] + [loss_part]), pack([e[1] for e in smalls] + [blank]),
        pack([e[2] for e in smalls] + [blank]), pack([e[3] for e in smalls] + [blank]), "reduce_small_adamw")
    pos = 0
    for nm, w, _, _, _ in smalls:
        out[nm] = tuple(a[0, pos:pos + w.size].reshape(w.shape) for a in totals)
        pos += -(-w.size // LANES) * LANES
    loss = totals[0][0, pos]

    order = ["norm_mix_g", "w_in", "b_fgate", "q_norm_g", "w_uq", "kv_norm_g", "w_ukv", "fox_out_g", "mla_out_g", "w_o",
             "norm_ffn_g", "w_gate", "w_up", "w_down", "final_norm_g"]
    return (loss, grad_x.reshape(bl, s, d), *[out[n][0] for n in order], *[out[n][1] for n in order],
            *[out[n][2] for n in order], *[out[n][3] for n in order])
```

```python
import jax
import jax.numpy as jnp
from jax import lax
from jax.experimental import pallas as pl
from jax.experimental.pallas import tpu as pltpu

F32 = jnp.float32
BF16 = jnp.bfloat16
MESH = pl.DeviceIdType.MESH

N_DEV = 8
HEADS = 8
HEAD_DIM = 64
PAIRS = HEADS // 2
MLA_ROPE = 32
MLA_QK = HEAD_DIM + MLA_ROPE
ROPE_THETA = 10000.0
NORM_EPS = 1e-6
ADAM_LR, ADAM_B1, ADAM_B2, ADAM_EPS, ADAM_WD, ADAM_STEP = 0.001, 0.9, 0.999, 1e-08, 0.01, 10

LANES = 128
MASKED = -1e30
VMEM_LIMIT = 48 * 1024 * 1024

_DIMS = {"nn": (((1,), (0,)), ((), ())), "nt": (((1,), (1,)), ((), ())), "tn": (((0,), (0,)), ((), ()))}


def _params(*sem):
    return pltpu.CompilerParams(dimension_semantics=sem, vmem_limit_bytes=VMEM_LIMIT)


def _dot(a, b, mode):
    return lax.dot_general(a.astype(BF16), b.astype(BF16), _DIMS[mode], preferred_element_type=F32)


def _tile(n, pref, unit=8):
    if n <= pref:
        return n
    t = pref - pref % unit
    while n % t:
        t -= unit
    return t


def _log2(n):
    assert n & (n - 1) == 0
    return n.bit_length() - 1


def _matmul(a, b, mode, out_dtype, name, tm=512, tn=512, tk=None, res=None):
    if mode == "nn":
        (m, kd), n = a.shape, b.shape[1]
    elif mode == "nt":
        (m, kd), n = a.shape, b.shape[0]
    else:
        (kd, m), n = a.shape, b.shape[1]
    tm, tn = _tile(m, tm, LANES if mode == "tn" else 16), _tile(n, tn, LANES)
    tk = kd if tk is None else _tile(kd, tk, LANES)
    nk = kd // tk
    a_spec = pl.BlockSpec((tk, tm), lambda i, j, k: (k, i)) if mode == "tn" else pl.BlockSpec((tm, tk), lambda i, j, k: (i, k))
    b_spec = pl.BlockSpec((tn, tk), lambda i, j, k: (j, k)) if mode == "nt" else pl.BlockSpec((tk, tn), lambda i, j, k: (k, j))
    o_spec = pl.BlockSpec((tm, tn), lambda i, j, k: (i, j))
    has_res = res is not None

    def body(*refs):
        a_ref, b_ref = refs[:2]
        r_ref = refs[2] if has_res else None
        o_ref = refs[3] if has_res else refs[2]

        def finish(acc):
            if has_res:
                acc = acc + r_ref[...]
            o_ref[...] = acc.astype(out_dtype)

        part = _dot(a_ref[...], b_ref[...], mode)
        if nk == 1:
            finish(part)
        else:
            acc_ref = refs[-1]
            k = pl.program_id(2)

            @pl.when(k == 0)
            def _():
                acc_ref[...] = part

            @pl.when(k > 0)
            def _():
                acc_ref[...] += part

            @pl.when(k == nk - 1)
            def _():
                finish(acc_ref[...])

    return pl.pallas_call(
        body, name=name, grid=(m // tm, n // tn, nk),
        in_specs=[a_spec, b_spec] + ([o_spec] if has_res else []), out_specs=o_spec,
        out_shape=jax.ShapeDtypeStruct((m, n), out_dtype),
        scratch_shapes=[pltpu.VMEM((tm, tn), F32)] if nk > 1 else [],
        compiler_params=_params("parallel", "parallel", "arbitrary"),
    )(*([a, b] + ([res] if has_res else [])))


def _rstd(x):
    return lax.rsqrt(jnp.mean(x * x, axis=-1, keepdims=True) + NORM_EPS)


def _norm_bwd(x, g, dy):
    r = _rstd(x)
    xh = x * r
    u = dy * g
    dx = r * (u - xh * jnp.mean(u * xh, axis=-1, keepdims=True))
    return dx, jnp.sum(dy * xh, axis=0, keepdims=True)


def _rmsnorm(x, col, width, g, out_dtype, name):
    t = x.shape[0]
    tm = _tile(t, 512)

    def body(x_ref, g_ref, o_ref):
        xv = x_ref[...]
        o_ref[...] = ((xv * _rstd(xv)) * g_ref[...]).astype(out_dtype)

    return pl.pallas_call(
        body, name=name, grid=(t // tm,),
        in_specs=[pl.BlockSpec((tm, width), lambda i: (i, col)), pl.BlockSpec((1, width), lambda i: (0, 0))],
        out_specs=pl.BlockSpec((tm, width), lambda i: (i, 0)),
        out_shape=jax.ShapeDtypeStruct((t, width), out_dtype),
        compiler_params=_params("parallel"),
    )(x, g)


def _rmsnorm_bwd(x, col, width, g, dy, name, res=None, mxu_copy=False):
    t = x.shape[0]
    tm = _tile(t, 512)
    has_res = res is not None
    row = pl.BlockSpec((tm, width), lambda i: (i, 0))
    vec = pl.BlockSpec((1, width), lambda i: (0, 0))

    def body(*refs):
        x_ref, g_ref, dy_ref = refs[:3]
        dx_ref, dg_ref = refs[3 + has_res:5 + has_res]
        dx, dg = _norm_bwd(x_ref[...], g_ref[...], dy_ref[...])
        if has_res:
            dx = dx + refs[3][...]
        dx_ref[...] = dx
        if mxu_copy:
            refs[-1][...] = dx.astype(BF16)

        @pl.when(pl.program_id(0) == 0)
        def _():
            dg_ref[...] = jnp.zeros_like(dg_ref)

        dg_ref[...] += dg

    return pl.pallas_call(
        body, name=name, grid=(t // tm,),
        in_specs=[pl.BlockSpec((tm, width), lambda i: (i, col)), vec, row] + ([row] if has_res else []),
        out_specs=(row, vec) + ((row,) if mxu_copy else ()),
        out_shape=(jax.ShapeDtypeStruct((t, width), F32), jax.ShapeDtypeStruct((1, width), F32))
        + ((jax.ShapeDtypeStruct((t, width), BF16),) if mxu_copy else ()),
        compiler_params=_params("arbitrary"),
    )(*([x, g, dy] + ([res] if has_res else [])))


def _out_norm(fo, mo, gf, gm, name):
    t, w = fo.shape
    tm = _tile(t, 512)
    row = pl.BlockSpec((tm, w), lambda i: (i, 0))
    vec = pl.BlockSpec((1, w), lambda i: (0, 0))

    def body(fo_ref, mo_ref, gf_ref, gm_ref, o_ref):
        f, m = fo_ref[...], mo_ref[...]
        o_ref[:, :w] = ((f * _rstd(f)) * gf_ref[...]).astype(BF16)
        o_ref[:, w:] = ((m * _rstd(m)) * gm_ref[...]).astype(BF16)

    return pl.pallas_call(
        body, name=name, grid=(t // tm,), in_specs=[row, row, vec, vec],
        out_specs=pl.BlockSpec((tm, 2 * w), lambda i: (i, 0)),
        out_shape=jax.ShapeDtypeStruct((t, 2 * w), BF16),
        compiler_params=_params("parallel"),
    )(fo, mo, gf, gm)


def _split3(x):
    hi = x.astype(BF16)
    r1 = x - hi.astype(F32)
    mid = r1.astype(BF16)
    lo = (r1 - mid.astype(F32)).astype(BF16)
    return hi, mid, lo


def _dot_x01(x, m01):
    hi, mid, lo = _split3(x)
    d = lambda p: lax.dot_general(p, m01, _DIMS["nn"], preferred_element_type=F32)
    return (d(lo) + d(mid)) + d(hi)


def _dot_01x(m01, x):
    hi, mid, lo = _split3(x)
    d = lambda p: lax.dot_general(m01, p, _DIMS["nn"], preferred_element_type=F32)
    return (d(lo) + d(mid)) + d(hi)


def _out_norm_bwd(fo, mo, gf, gm, dcat, name):
    t, w = fo.shape
    nh = w // HEAD_DIM
    tm = _tile(t, 512)
    row = pl.BlockSpec((tm, w), lambda i: (i, 0))
    vec = pl.BlockSpec((1, w), lambda i: (0, 0))
    hrow = pl.BlockSpec((tm, nh), lambda i: (i, 0))

    def body(fo_ref, mo_ref, gf_ref, gm_ref, dc_ref, dfo_ref, dmo_ref, ff_ref, fm_ref, dgf_ref, dgm_ref):
        lane_head = lax.shift_right_logical(lax.broadcasted_iota(jnp.int32, (w, nh), 0), _log2(HEAD_DIM))
        sel = (lane_head == lax.broadcasted_iota(jnp.int32, (w, nh), 1)).astype(BF16)
        f, m = fo_ref[...], mo_ref[...]
        dfo, dgf = _norm_bwd(f, gf_ref[...], dc_ref[:, :w])
        dmo, dgm = _norm_bwd(m, gm_ref[...], dc_ref[:, w:])
        dfo_ref[...] = dfo.astype(BF16)
        dmo_ref[...] = dmo.astype(BF16)
        ff_ref[...] = _dot_x01(dfo * f, sel)
        fm_ref[...] = _dot_x01(dmo * m, sel)

        @pl.when(pl.program_id(0) == 0)
        def _():
            dgf_ref[...] = jnp.zeros_like(dgf_ref)
            dgm_ref[...] = jnp.zeros_like(dgm_ref)

        dgf_ref[...] += dgf
        dgm_ref[...] += dgm

    return pl.pallas_call(
        body, name=name, grid=(t // tm,),
        in_specs=[row, row, vec, vec, pl.BlockSpec((tm, 2 * w), lambda i: (i, 0))],
        out_specs=(row, row, hrow, hrow, vec, vec),
        out_shape=(jax.ShapeDtypeStruct((t, w), BF16), jax.ShapeDtypeStruct((t, w), BF16),
                   jax.ShapeDtypeStruct((t, nh), F32), jax.ShapeDtypeStruct((t, nh), F32),
                   jax.ShapeDtypeStruct((1, w), F32), jax.ShapeDtypeStruct((1, w), F32)),
        compiler_params=_params("arbitrary"),
    )(fo, mo, gf, gm, dcat)


def _loss_bwd(x, tgt, g, name):
    t, d = x.shape
    tm = _tile(t, 512)
    row = pl.BlockSpec((tm, d), lambda i: (i, 0))
    vec = pl.BlockSpec((1, d), lambda i: (0, 0))
    one = pl.BlockSpec((1, 1), lambda i: (0, 0))

    def body(x_ref, t_ref, g_ref, dx_ref, dg_ref, loss_ref, dxb_ref):
        xv, gv = x_ref[...], g_ref[...]
        diff = (xv * _rstd(xv)) * gv - t_ref[...]
        dx, dg = _norm_bwd(xv, gv, diff / d)
        dx_ref[...] = dx
        dxb_ref[...] = dx.astype(BF16)

        @pl.when(pl.program_id(0) == 0)
        def _():
            dg_ref[...] = jnp.zeros_like(dg_ref)
            loss_ref[...] = jnp.zeros_like(loss_ref)

        dg_ref[...] += dg
        loss_ref[...] += 0.5 * jnp.sum(jnp.mean(diff * diff, axis=-1, keepdims=True), axis=0, keepdims=True)

    return pl.pallas_call(
        body, name=name, grid=(t // tm,), in_specs=[row, row, vec], out_specs=(row, vec, one, row),
        out_shape=(jax.ShapeDtypeStruct((t, d), F32), jax.ShapeDtypeStruct((1, d), F32), jax.ShapeDtypeStruct((1, 1), F32),
                   jax.ShapeDtypeStruct((t, d), BF16)),
        compiler_params=_params("arbitrary"),
    )(x, tgt, g)


def _ffn_up(h, wg_t, wu_t, name, tm=512, tf=1408):
    t, d = h.shape
    f = wg_t.shape[0]
    tm, tf = _tile(t, tm, 16), _tile(f, tf, LANES)
    tok = pl.BlockSpec((tm, tf), lambda i, j: (i, j))
    wt = pl.BlockSpec((tf, d), lambda i, j: (j, 0))

    def body(h_ref, wg_ref, wu_ref, g_ref, u_ref, a_ref):
        hv = h_ref[...]
        g, u = _dot(hv, wg_ref[...], "nt"), _dot(hv, wu_ref[...], "nt")
        g_ref[...], u_ref[...] = g, u
        a_ref[...] = ((g * jax.nn.sigmoid(g)) * u).astype(BF16)

    return pl.pallas_call(
        body, name=name, grid=(t // tm, f // tf), in_specs=[pl.BlockSpec((tm, d), lambda i, j: (i, 0)), wt, wt],
        out_specs=(tok, tok, tok),
        out_shape=(jax.ShapeDtypeStruct((t, f), F32), jax.ShapeDtypeStruct((t, f), F32), jax.ShapeDtypeStruct((t, f), BF16)),
        compiler_params=_params("parallel", "parallel"),
    )(h, wg_t, wu_t)


def _ffn_down_bwd(dy, w_down, gate, up, name, tm=512, tf=1408):
    t, d = dy.shape
    f = w_down.shape[0]
    tm, tf = _tile(t, tm, 16), _tile(f, tf, LANES)
    tok = pl.BlockSpec((tm, tf), lambda i, j: (i, j))

    def body(dy_ref, w_ref, g_ref, u_ref, dg_ref, du_ref):
        da = _dot(dy_ref[...], w_ref[...], "nt")
        g = g_ref[...]
        sg = jax.nn.sigmoid(g)
        dg_ref[...] = (da * u_ref[...] * (sg * (1.0 + g * (1.0 - sg)))).astype(BF16)
        du_ref[...] = (da * (g * sg)).astype(BF16)

    return pl.pallas_call(
        body, name=name, grid=(t // tm, f // tf),
        in_specs=[pl.BlockSpec((tm, d), lambda i, j: (i, 0)), pl.BlockSpec((tf, d), lambda i, j: (j, 0)), tok, tok],
        out_specs=(tok, tok),
        out_shape=(jax.ShapeDtypeStruct((t, f), BF16), jax.ShapeDtypeStruct((t, f), BF16)),
        compiler_params=_params("parallel", "parallel"),
    )(dy, w_down, gate, up)


def _chunk_scan_mats(rows, grp, reverse):
    ii = lax.broadcasted_iota(jnp.int32, (LANES, LANES), 0)
    jj = lax.broadcasted_iota(jnp.int32, (LANES, LANES), 1)
    within = ((ii >= jj) if reverse else (ii <= jj)).astype(BF16)
    ones = jnp.ones((LANES, LANES), BF16)
    ri = lax.broadcasted_iota(jnp.int32, (rows, rows), 0)
    rj = lax.broadcasted_iota(jnp.int32, (rows, rows), 1)
    sh = _log2(grp)
    same = lax.shift_right_logical(ri, sh) == lax.shift_right_logical(rj, sh)
    across = (same & ((rj > ri) if reverse else (rj < ri))).astype(BF16)
    return within, ones, across


def _running_sum(v, mats):
    within, ones, across = mats
    return _dot_x01(v, within) + _dot_01x(across, _dot_x01(v, ones))


def _fgate(z, bcol, grp, name):
    rows = z.shape[0]

    def body(z_ref, b_ref, c_ref):
        zz = z_ref[...] + b_ref[...]
        log_f = jnp.minimum(zz, 0.0) - jnp.log1p(jnp.exp(-jnp.abs(zz)))
        c_ref[...] = _running_sum(log_f, _chunk_scan_mats(rows, grp, False))

    return pl.pallas_call(body, name=name, out_shape=jax.ShapeDtypeStruct(z.shape, F32),
                          compiler_params=pltpu.CompilerParams(vmem_limit_bytes=VMEM_LIMIT))(z, bcol)


def _fgate_bwd(z, bcol, dc, grp, name):
    rows = z.shape[0]

    def body(z_ref, b_ref, dc_ref, dz_ref, db_ref):
        zz = z_ref[...] + b_ref[...]
        dz = _running_sum(dc_ref[...], _chunk_scan_mats(rows, grp, True)) * jax.nn.sigmoid(-zz)
        dz_ref[...] = dz
        head = lax.shift_right_logical(lax.broadcasted_iota(jnp.int32, (HEADS, rows), 1), _log2(grp)) & (HEADS - 1)
        sel = (head == lax.broadcasted_iota(jnp.int32, (HEADS, rows), 0)).astype(BF16)
        db_ref[...] = jnp.sum(_dot_01x(sel, dz), axis=1, keepdims=True)

    return pl.pallas_call(
        body, name=name,
        out_shape=(jax.ShapeDtypeStruct(z.shape, F32), jax.ShapeDtypeStruct((HEADS, 1), F32)),
        compiler_params=pltpu.CompilerParams(vmem_limit_bytes=VMEM_LIMIT),
    )(z, bcol, dc)


def _rotate(x, cs, sn_signed):
    return x * cs + pltpu.roll(x, LANES // 2, axis=1) * sn_signed


def _rope_fwd(q_raw, nope, proj_b, k_col, cs, sn, name):
    t, qw = q_raw.shape
    tm = _tile(t, 512)
    row = pl.BlockSpec((tm, LANES), lambda i: (i, 0))

    def body(q_ref, k_ref, c_ref, s_ref, qo_ref, ko_ref):
        c, s = c_ref[...], s_ref[...]
        qo_ref[:, :nope] = q_ref[:, :nope].astype(BF16)
        for off in range(nope, qw, LANES):
            qo_ref[:, off:off + LANES] = _rotate(q_ref[:, off:off + LANES], c, s).astype(BF16)
        ko_ref[...] = _rotate(k_ref[...], c, s).astype(BF16)

    return pl.pallas_call(
        body, name=name, grid=(t // tm,),
        in_specs=[pl.BlockSpec((tm, qw), lambda i: (i, 0)), pl.BlockSpec((tm, LANES), lambda i: (i, k_col)), row, row],
        out_specs=(pl.BlockSpec((tm, qw), lambda i: (i, 0)), row),
        out_shape=(jax.ShapeDtypeStruct((t, qw), BF16), jax.ShapeDtypeStruct((t, LANES), BF16)),
        compiler_params=_params("parallel"),
    )(q_raw, proj_b, cs, sn)


def _rope_bwd(dq_pe, dk_pe, cs, sn, name):
    t, qw = dq_pe.shape
    tm = _tile(t, 512)
    row = pl.BlockSpec((tm, LANES), lambda i: (i, 0))
    wide = pl.BlockSpec((tm, qw), lambda i: (i, 0))

    def body(q_ref, k_ref, c_ref, s_ref, qo_ref, ko_ref):
        c, s = c_ref[...], -s_ref[...]
        for off in range(0, qw, LANES):
            qo_ref[:, off:off + LANES] = _rotate(q_ref[:, off:off + LANES], c, s).astype(BF16)
        ko_ref[...] = _rotate(k_ref[...], c, s)

    return pl.pallas_call(
        body, name=name, grid=(t // tm,), in_specs=[wide, row, row, row], out_specs=(wide, row),
        out_shape=(jax.ShapeDtypeStruct((t, qw), BF16), jax.ShapeDtypeStruct((t, LANES), F32)),
        compiler_params=_params("parallel"),
    )(dq_pe, dk_pe, cs, sn)


def _lane_masks(pair, h, pe):
    lane = lax.broadcasted_iota(jnp.int32, (1, LANES), 1)
    in_head = lax.shift_right_logical(lane, _log2(HEAD_DIM)) == h
    in_rope = ((lax.shift_right_logical(lane, _log2(MLA_ROPE // 2)) & 3) == ((2 * pair + h) & 3)) if pe else None
    return in_head, in_rope


def _keep(mask, v):
    return jnp.where(mask, v, jnp.zeros_like(v))


def _to_row(col):
    n = col.shape[0]
    eye = lax.broadcasted_iota(jnp.int32, (n, n), 0) == lax.broadcasted_iota(jnp.int32, (n, n), 1)
    return jnp.sum(jnp.where(eye, col, 0.0), axis=0, keepdims=True)


def _to_col(row):
    n = row.shape[1]
    eye = lax.broadcasted_iota(jnp.int32, (n, n), 0) == lax.broadcasted_iota(jnp.int32, (n, n), 1)
    return jnp.sum(jnp.where(eye, row, 0.0), axis=1, keepdims=True)


def _attn_fwd(ops, bias, scale, bl, s, tq, name):
    pe = len(ops) == 3
    has_bias = bias is not None
    nq = s // tq
    t = bl * s

    def body(*refs):
        if pe:
            q_ref, qpe_ref, kv_ref, kpe_ref = refs[:4]
            n_in = 4
            q_at = lambda r0, r1: q_ref[r0:r1, :]
            v_at = lambda r0, r1: kv_ref[r0:r1, LANES:]
            kcat = refs[-1]
            kcat[:, :LANES] = kv_ref[:, :LANES]
            kcat[:, LANES:] = kpe_ref[...]
            k_at = lambda r0, r1: kcat[r0:r1, :]
        else:
            qkv_ref = refs[0]
            n_in = 1
            q_at = lambda r0, r1: qkv_ref[r0:r1, :LANES]
            k_at = lambda r0, r1: qkv_ref[r0:r1, LANES:2 * LANES]
            v_at = lambda r0, r1: qkv_ref[r0:r1, 2 * LANES:]
        if has_bias:
            c_ref = refs[n_in]
            n_in += 1
        o_ref, lse_ref = refs[n_in:n_in + 2]
        pair = pl.program_id(1)
        causal = lax.broadcasted_iota(jnp.int32, (tq, tq), 1) <= lax.broadcasted_iota(jnp.int32, (tq, tq), 0)
        o_ref[...] = jnp.zeros_like(o_ref)

        def head(h, carry):
            in_head, in_rope = _lane_masks(pair, h, pe)
            for i in range(nq):
                r0, r1 = i * tq, (i + 1) * tq
                qm = _keep(in_head, q_at(r0, r1))
                if pe:
                    qm = jnp.concatenate([qm, _keep(in_rope, qpe_ref[r0:r1, :])], axis=1)
                cq = _to_col(c_ref[h, :, r0:r1]) if has_bias else None

                def logits(k0, k1):
                    sc = _dot(qm, k_at(k0, k1), "nt") * scale
                    if has_bias:
                        sc = sc + (cq - c_ref[h, :, k0:k1])
                    return sc

                sd = jnp.where(causal, logits(r0, r1), MASKED)
                m = jnp.max(sd, axis=1, keepdims=True)
                if i:
                    so = logits(0, r0)
                    m = jnp.maximum(m, jnp.max(so, axis=1, keepdims=True))
                pd = jnp.exp(sd - m)
                l = jnp.sum(pd, axis=1, keepdims=True)
                acc = _dot(pd, v_at(r0, r1), "nn")
                if i:
                    po = jnp.exp(so - m)
                    l = l + jnp.sum(po, axis=1, keepdims=True)
                    acc = acc + _dot(po, v_at(0, r0), "nn")
                o_ref[r0:r1, :] = jnp.where(in_head, acc / l, o_ref[r0:r1, :])
                lse_ref[h, :, r0:r1] = _to_row(m + jnp.log(l))
            return carry

        lax.fori_loop(0, 2, head, 0)

    seq = lambda w, col: pl.BlockSpec((s, w), col)
    if pe:
        in_specs = [seq(LANES, lambda b, p: (b, p)), seq(LANES, lambda b, p: (b, PAIRS + p // 2)),
                    seq(2 * LANES, lambda b, p: (b, p)), seq(LANES, lambda b, p: (b, 0))]
        args = [ops[0], ops[0], ops[1], ops[2]]
        scratch = [pltpu.VMEM((s, 2 * LANES), BF16)]
    else:
        in_specs = [seq(3 * LANES, lambda b, p: (b, p))]
        args = [ops[0]]
        scratch = []
    per_head_row = pl.BlockSpec((2, 1, s), lambda b, p: (b * PAIRS + p, 0, 0))
    if has_bias:
        in_specs.append(per_head_row)
        args.append(bias)
    return pl.pallas_call(
        body, name=name, grid=(bl, PAIRS), in_specs=in_specs,
        out_specs=(seq(LANES, lambda b, p: (b, p)), per_head_row),
        out_shape=(jax.ShapeDtypeStruct((t, HEADS * HEAD_DIM), F32), jax.ShapeDtypeStruct((bl * HEADS, 1, s), F32)),
        scratch_shapes=scratch, compiler_params=_params("parallel", "parallel"),
    )(*args)


def _attn_bwd(ops, do, lse, delta, bias, scale, bl, s, tq, name):
    pe = len(ops) == 3
    has_bias = bias is not None
    nq = s // tq
    t = bl * s
    width = 2 * LANES if pe else LANES

    def body(*refs):
        if pe:
            q_ref, qpe_ref, kv_ref, kpe_ref = refs[:4]
            n_in = 4
            k_at = lambda r0, r1: kv_ref[r0:r1, :LANES]
            v_at = lambda r0, r1: kv_ref[r0:r1, LANES:]
        else:
            qkv_ref = refs[0]
            n_in = 1
            k_at = lambda r0, r1: qkv_ref[r0:r1, LANES:2 * LANES]
            v_at = lambda r0, r1: qkv_ref[r0:r1, 2 * LANES:]
        do_ref, lse_ref, dl_ref = refs[n_in:n_in + 3]
        n_in += 3
        if has_bias:
            c_ref = refs[n_in]
            n_in += 1
        if pe:
            dqn_ref, dkv_ref, dqpe_ref, dkpe_ref, dq_acc, qcat = refs[n_in:]
            qcat[:, :LANES] = q_ref[...]
            qcat[:, LANES:] = qpe_ref[...]
            q_at = lambda r0, r1: qcat[r0:r1, :]
            dkv_ref[...] = jnp.zeros_like(dkv_ref)
        else:
            dqkv_ref, dc_ref, dq_acc = refs[n_in:]
            q_at = lambda r0, r1: qkv_ref[r0:r1, :LANES]
            dqkv_ref[...] = jnp.zeros_like(dqkv_ref)
            dc_ref[...] = jnp.zeros_like(dc_ref)
        pair = pl.program_id(1)
        dq_acc[...] = jnp.zeros_like(dq_acc)
        causal = lax.broadcasted_iota(jnp.int32, (tq, tq), 1) >= lax.broadcasted_iota(jnp.int32, (tq, tq), 0)

        def head(h, carry):
            in_head, in_rope = _lane_masks(pair, h, pe)
            for j in range(nq):
                r0, r1 = j * tq, (j + 1) * tq
                kt = _keep(in_head, k_at(r0, r1))
                if pe:
                    kt = jnp.concatenate([kt, _keep(in_rope, kpe_ref[r0:r1, :])], axis=1)
                vt = _keep(in_head, v_at(r0, r1))
                ck = _to_col(c_ref[h, :, r0:r1]) if has_bias else None

                def block(q0, q1, diagonal):
                    qq, dd = q_at(q0, q1), do_ref[q0:q1, :]
                    st = _dot(kt, qq, "nt") * scale
                    if has_bias:
                        st = st + (c_ref[h, :, q0:q1] - ck)
                    if diagonal:
                        st = jnp.where(causal, st, MASKED)
                    pt = jnp.exp(st - lse_ref[h, :, q0:q1])
                    dst = pt * (_dot(vt, dd, "nt") - dl_ref[h, :, q0:q1])
                    dsb = (dst * scale).astype(BF16)
                    dq_acc[q0:q1, :] += _dot(dsb, kt, "tn")
                    if has_bias:
                        dc_ref[h, :, q0:q1] += jnp.sum(dst, axis=0, keepdims=True)
                    return _dot(pt, dd, "nn"), _dot(dsb, qq, "nn"), (jnp.sum(dst, axis=1, keepdims=True) if has_bias else None)

                dv_c, dk_c, cs = block(r0, r1, True)
                if r1 < s:
                    dv_o, dk_o, cs_o = block(r1, s, False)
                    dv_c, dk_c = dv_c + dv_o, dk_c + dk_o
                    cs = cs + cs_o if has_bias else None
                if pe:
                    dkv_ref[r0:r1, :LANES] = jnp.where(in_head, dk_c[:, :LANES].astype(BF16), dkv_ref[r0:r1, :LANES])
                    dkv_ref[r0:r1, LANES:] = jnp.where(in_head, dv_c.astype(BF16), dkv_ref[r0:r1, LANES:])
                    dkpe_ref[r0:r1, :] += _keep(in_rope, dk_c[:, LANES:])
                else:
                    dqkv_ref[r0:r1, LANES:2 * LANES] = jnp.where(in_head, dk_c.astype(BF16), dqkv_ref[r0:r1, LANES:2 * LANES])
                    dqkv_ref[r0:r1, 2 * LANES:] = jnp.where(in_head, dv_c.astype(BF16), dqkv_ref[r0:r1, 2 * LANES:])
                    dc_ref[h, :, r0:r1] -= _to_row(cs)
            return carry

        if pe:
            @pl.when(pair == 0)
            def _():
                dkpe_ref[...] = jnp.zeros_like(dkpe_ref)

            @pl.when(pair % 2 == 0)
            def _():
                dqpe_ref[...] = jnp.zeros_like(dqpe_ref)

        lax.fori_loop(0, 2, head, 0)
        if pe:
            dqn_ref[...] = dq_acc[:, :LANES].astype(BF16)
            dqpe_ref[...] += dq_acc[:, LANES:]
        else:
            dqkv_ref[:, :LANES] = dq_acc[...].astype(BF16)

    seq = lambda w, col: pl.BlockSpec((s, w), col)
    per_head_row = pl.BlockSpec((2, 1, s), lambda b, p: (b * PAIRS + p, 0, 0))
    if pe:
        in_specs = [seq(LANES, lambda b, p: (b, p)), seq(LANES, lambda b, p: (b, PAIRS + p // 2)),
                    seq(2 * LANES, lambda b, p: (b, p)), seq(LANES, lambda b, p: (b, 0))]
        args = [ops[0], ops[0], ops[1], ops[2]]
    else:
        in_specs = [seq(3 * LANES, lambda b, p: (b, p))]
        args = [ops[0]]
    in_specs += [seq(LANES, lambda b, p: (b, p)), per_head_row, per_head_row]
    args += [do, lse, delta]
    if has_bias:
        in_specs.append(per_head_row)
        args.append(bias)
    scratch = [pltpu.VMEM((s, width), F32)]
    if pe:
        out_specs = (seq(LANES, lambda b, p: (b, p)), seq(2 * LANES, lambda b, p: (b, p)),
                     seq(LANES, lambda b, p: (b, p // 2)), seq(LANES, lambda b, p: (b, 0)))
        out_shape = (jax.ShapeDtypeStruct((t, PAIRS * LANES), BF16), jax.ShapeDtypeStruct((t, PAIRS * 2 * LANES), BF16),
                     jax.ShapeDtypeStruct((t, 2 * LANES), F32), jax.ShapeDtypeStruct((t, LANES), F32))
        scratch.append(pltpu.VMEM((s, 2 * LANES), BF16))
    else:
        out_specs = (seq(3 * LANES, lambda b, p: (b, p)), per_head_row)
        out_shape = (jax.ShapeDtypeStruct((t, PAIRS * 3 * LANES), BF16), jax.ShapeDtypeStruct((bl * HEADS, 1, s), F32))
    return pl.pallas_call(
        body, name=name, grid=(bl, PAIRS), in_specs=in_specs, out_specs=out_specs, out_shape=out_shape,
        scratch_shapes=scratch, compiler_params=_params("parallel", "arbitrary"),
    )(*args)


def _my_place():
    return lax.axis_index("x"), lax.axis_index("y"), lax.axis_index("c")


def _flip(p, bit):
    return 1 - p if bit else p


def _relative(x, y, c, k):
    return _flip(x, k & 4), _flip(y, k & 2), _flip(c, k & 1)


def _linear(x, y, c):
    return 4 * x + 2 * y + c


def _all_gather(shard, name):
    r, cdim = shard.shape

    def body(x_ref, out_ref, send_sems, recv_sems, local_sem):
        x, y, c = _my_place()
        me, sibling = (x, y, c), (x, y, 1 - c)
        chips = [(1 - x, y), (x, 1 - y), (1 - x, 1 - y)]

        def slot(px, py, pc):
            return out_ref.at[_linear(px, py, pc)]

        def copy(k, block, to, src=None):
            return pltpu.make_async_remote_copy(
                src_ref=slot(*block) if src is None else src, dst_ref=slot(*block),
                send_sem=send_sems.at[k], recv_sem=recv_sems.at[k], device_id=to, device_id_type=MESH)

        mine = pltpu.make_async_copy(x_ref, slot(*me), local_sem)
        mine.start()
        first = [copy(0, me, sibling, src=x_ref)] + [copy(1 + j, me, (*chip, c), src=x_ref) for j, chip in enumerate(chips)]
        for cp in first:
            cp.start()
        passed = [copy(4 + j, (*chip, c), sibling) for j, chip in enumerate(chips)]
        for j, chip in enumerate(chips):
            copy(1 + j, (*chip, c), me).wait_recv()
            passed[j].start()
        copy(0, sibling, me).wait_recv()
        for j, chip in enumerate(chips):
            copy(4 + j, (*chip, 1 - c), me).wait_recv()
        for cp in first + passed:
            cp.wait_send()
        mine.wait()

    return pl.pallas_call(
        body, name=name, out_shape=jax.ShapeDtypeStruct((N_DEV, r, cdim), shard.dtype),
        in_specs=[pl.BlockSpec(memory_space=pl.ANY)], out_specs=pl.BlockSpec(memory_space=pl.ANY),
        scratch_shapes=[pltpu.SemaphoreType.DMA((7,)), pltpu.SemaphoreType.DMA((7,)), pltpu.SemaphoreType.DMA(())],
    )(shard)


def _exchange(pieces, name):
    n = len(pieces)
    rows = [p.shape[1] for p in pieces]
    starts = [sum(rows[:i]) for i in range(n)]
    cdim, dtype = pieces[0].shape[2], pieces[0].dtype

    def body(*refs):
        p_refs = refs[:n]
        out_ref, send_sems, recv_sems, local_sem = refs[n:]
        x, y, c = _my_place()
        me = _linear(x, y, c)
        landing = lambda i: out_ref.at[me, pl.ds(starts[i], rows[i])]
        for i in range(n):
            pltpu.make_async_copy(p_refs[i].at[me], landing(i), local_sem).start()
        for k in range(1, N_DEV):
            peer = _relative(x, y, c, k)
            for i in range(n):
                pltpu.make_async_remote_copy(
                    src_ref=p_refs[i].at[_linear(*peer)], dst_ref=landing(i),
                    send_sem=send_sems.at[k - 1], recv_sem=recv_sems.at[k - 1], device_id=peer, device_id_type=MESH).start()
        whole = out_ref.at[me]
        for k in range(1, N_DEV):
            both = pltpu.make_async_remote_copy(
                src_ref=whole, dst_ref=whole, send_sem=send_sems.at[k - 1], recv_sem=recv_sems.at[k - 1],
                device_id=_relative(x, y, c, k), device_id_type=MESH)
            both.wait_recv()
            both.wait_send()
        pltpu.make_async_copy(whole, whole, local_sem).wait()

    anywhere = pl.BlockSpec(memory_space=pl.ANY)
    return pl.pallas_call(
        body, name=name, out_shape=jax.ShapeDtypeStruct((N_DEV, sum(rows), cdim), dtype),
        in_specs=[anywhere] * n, out_specs=anywhere,
        scratch_shapes=[pltpu.SemaphoreType.DMA((7,)), pltpu.SemaphoreType.DMA((7,)), pltpu.SemaphoreType.DMA(())],
    )(*pieces)


def _sum_blocks(parts, name):
    n, r, cdim = parts.shape
    tr = _tile(r, 512, 16)

    def body(p_ref, o_ref):
        acc = p_ref[0].astype(F32)
        for d in range(1, n):
            acc = acc + p_ref[d].astype(F32)
        o_ref[...] = acc

    return pl.pallas_call(
        body, name=name, grid=(r // tr,), in_specs=[pl.BlockSpec((n, tr, cdim), lambda i: (0, i, 0))],
        out_specs=pl.BlockSpec((tr, cdim), lambda i: (i, 0)), out_shape=jax.ShapeDtypeStruct((r, cdim), F32),
        compiler_params=_params("parallel"),
    )(parts)


def _adamw_math(w, g, m, v):
    m = ADAM_B1 * m + (1.0 - ADAM_B1) * g
    v = ADAM_B2 * v + (1.0 - ADAM_B2) * (g * g)
    m_hat = m / (1.0 - ADAM_B1 ** ADAM_STEP)
    v_hat = v / (1.0 - ADAM_B2 ** ADAM_STEP)
    delta = -ADAM_LR * (m_hat / (jnp.sqrt(v_hat) + ADAM_EPS) + ADAM_WD * w)
    return delta, m, v


def _adamw(w, g, m, v, name):
    def body(w_ref, g_ref, m_ref, v_ref, d_ref, nm_ref, nv_ref):
        d_ref[...], nm_ref[...], nv_ref[...] = _adamw_math(w_ref[...], g_ref[...], m_ref[...], v_ref[...])

    out = jax.ShapeDtypeStruct(w.shape, F32)
    return pl.pallas_call(body, name=name, out_shape=(out, out, out),
                          compiler_params=pltpu.CompilerParams(vmem_limit_bytes=VMEM_LIMIT))(w, g, m, v)


def _small_all_reduce_adamw(part, w, m, v, name):
    width = part.shape[1]

    def body(p_ref, w_ref, m_ref, v_ref, tot_ref, d_ref, nm_ref, nv_ref, rows, send_sems, recv_sems):
        x, y, c = _my_place()
        me = _linear(x, y, c)
        rows[me] = p_ref[...]
        copies = []
        for k in range(1, N_DEV):
            copies.append(pltpu.make_async_remote_copy(
                src_ref=rows.at[me], dst_ref=rows.at[me], send_sem=send_sems.at[k - 1], recv_sem=recv_sems.at[k - 1],
                device_id=_relative(x, y, c, k), device_id_type=MESH))
        for cp in copies:
            cp.start()
        for cp in copies:
            cp.wait_recv()
        for cp in copies:
            cp.wait_send()
        total = rows[0]
        for d in range(1, N_DEV):
            total = total + rows[d]
        tot_ref[...] = total
        d_ref[...], nm_ref[...], nv_ref[...] = _adamw_math(w_ref[...], total, m_ref[...], v_ref[...])

    out = jax.ShapeDtypeStruct((1, width), F32)
    return pl.pallas_call(
        body, name=name, out_shape=(out, out, out, out),
        scratch_shapes=[pltpu.VMEM((N_DEV, 1, width), F32), pltpu.SemaphoreType.DMA((7,)), pltpu.SemaphoreType.DMA((7,))],
    )(part, w, m, v)


def _pad_rows(a, rows):
    return jnp.pad(a, ((0, rows - a.shape[0]), (0, 0)))


def _pad_lanes(a):
    return jnp.pad(a, ((0, 0), (0, -a.shape[1] % LANES)))


def kernel(x, positions, norm_mix_g, w_in, b_fgate, q_norm_g, w_uq, kv_norm_g, w_ukv, fox_out_g, mla_out_g, w_o, norm_ffn_g, w_gate, w_up, w_down, final_norm_g, loss_target, m_norm_mix_g, m_w_in, m_b_fgate, m_q_norm_g, m_w_uq, m_kv_norm_g, m_w_ukv, m_fox_out_g, m_mla_out_g, m_w_o, m_norm_ffn_g, m_w_gate, m_w_up, m_w_down, m_final_norm_g, v_norm_mix_g, v_w_in, v_b_fgate, v_q_norm_g, v_w_uq, v_kv_norm_g, v_w_ukv, v_fox_out_g, v_mla_out_g, v_w_o, v_norm_ffn_g, v_w_gate, v_w_up, v_w_down, v_final_norm_g):
    bl, s, d = x.shape
    t = bl * s
    bh = bl * HEADS
    tq = _tile(s, 256)
    grp = s // LANES
    fw = HEADS * HEAD_DIM
    q_rank, kv_rank = w_uq.shape[1], w_ukv.shape[1]
    in_cols = w_in.shape[2]
    n_in = N_DEV * in_cols
    ff = N_DEV * w_gate.shape[2]
    half = MLA_ROPE // 2
    o_kvlat, o_krope, o_flogit = q_rank, q_rank + kv_rank, q_rank + kv_rank + LANES
    b_cols = -(-(o_flogit + HEADS) // LANES) * LANES

    tr = lambda w: jnp.transpose(w[0])
    in_rows = -(-in_cols // 16) * 16
    uq_rows = w_uq.shape[2] * q_rank // d
    ukv_rows = w_ukv.shape[2] * kv_rank // d
    pieces = [_pad_rows(tr(w_in), in_rows), _pad_rows(tr(w_uq).reshape(uq_rows, d), -(-uq_rows // 16) * 16),
              tr(w_ukv).reshape(ukv_rows, d), w_o[0], tr(w_gate), tr(w_up), w_down[0]]
    offs = [0]
    for p in pieces:
        offs.append(offs[-1] + p.shape[0])
    packed_rows = -(-offs[-1] // 256) * 256
    shard = _pad_rows(jnp.concatenate(pieces, axis=0), packed_rows).astype(BF16)
    gathered = _all_gather(shard, "gather_weights")

    def full(i, rows):
        return gathered[:, offs[i]:offs[i] + rows]

    w_in_t = full(0, in_cols).reshape(n_in, d)
    n_qkv = 3 * fw
    w_in_a = w_in_t[:n_qkv].reshape(3, PAIRS, LANES, d).transpose(1, 0, 2, 3).reshape(n_qkv, d)
    lat0, rope0 = n_qkv + HEADS, n_qkv + HEADS + q_rank + kv_rank
    k_rep = jnp.broadcast_to(w_in_t[rope0:].reshape(2, 1, half, d), (2, 4, half, d)).reshape(LANES, d)
    w_in_b = jnp.concatenate([w_in_t[lat0:rope0], k_rep, w_in_t[n_qkv:lat0],
                              jnp.zeros((b_cols - o_flogit - HEADS, d), BF16)], axis=0)
    w_uq_h = full(1, uq_rows).reshape(HEADS, MLA_QK, q_rank)
    w_uq_pe = jnp.concatenate([w_uq_h[:, HEAD_DIM:HEAD_DIM + half].reshape(2, 1, 4 * half, q_rank),
                               w_uq_h[:, HEAD_DIM + half:].reshape(2, 1, 4 * half, q_rank)], axis=1).reshape(2 * LANES, q_rank)
    w_uq_p = jnp.concatenate([w_uq_h[:, :HEAD_DIM].reshape(fw, q_rank), w_uq_pe], axis=0)
    w_ukv_p = full(2, ukv_rows).reshape(PAIRS, 2, 2, HEAD_DIM, kv_rank).transpose(0, 2, 1, 3, 4).reshape(2 * fw, kv_rank)
    w_o_f = full(3, w_o.shape[1]).reshape(-1, d)
    w_gate_t, w_up_t = full(4, ff // N_DEV).reshape(ff, d), full(5, ff // N_DEV).reshape(ff, d)
    w_down_f = full(6, ff // N_DEV).reshape(ff, d)

    def per_head_rows(a):
        return a.reshape(bl, s, HEADS).transpose(0, 2, 1).reshape(bh, 1, s)

    x2d = x.reshape(t, d)
    h1 = _rmsnorm(x2d, 0, d, norm_mix_g, BF16, "norm_mix")
    proj_a = _matmul(h1, w_in_a, "nt", BF16, "proj_fox", tn=3 * LANES)
    proj_b = _matmul(h1, w_in_b, "nt", F32, "proj_mla", tn=b_cols)

    z = proj_b[:, o_flogit:o_flogit + HEADS].reshape(bl, s, HEADS).transpose(0, 2, 1).reshape(bh * grp, LANES)
    bcol = jnp.broadcast_to(b_fgate.reshape(1, HEADS, 1), (bl, HEADS, grp)).reshape(bh * grp, 1)
    c = _fgate(z, bcol, grp, "forget_gate")
    c_bias = c.reshape(bh, 1, s)
    fox_o, fox_lse = _attn_fwd((proj_a,), c_bias, HEAD_DIM ** -0.5, bl, s, tq, "fox_attention")

    qn = _rmsnorm(proj_b, 0, q_rank, q_norm_g, BF16, "norm_q")
    kvn = _rmsnorm(proj_b, o_kvlat // kv_rank, kv_rank, kv_norm_g, BF16, "norm_kv")
    q_raw = _matmul(qn, w_uq_p, "nt", F32, "up_q", tn=fw + 2 * LANES)
    kv_all = _matmul(kvn, w_ukv_p, "nt", BF16, "up_kv")
    inv_freq = ROPE_THETA ** (-jnp.arange(0, MLA_ROPE, 2, dtype=F32) / MLA_ROPE)
    ang = positions.astype(F32).reshape(t, 1) * inv_freq[None, :]
    cos4, sin4 = jnp.tile(jnp.cos(ang), (1, 4)), jnp.tile(jnp.sin(ang), (1, 4))
    rope_cos, rope_sin = jnp.concatenate([cos4, cos4], axis=1), jnp.concatenate([-sin4, sin4], axis=1)
    q_all, kpe = _rope_fwd(q_raw, fw, proj_b, o_krope // LANES, rope_cos, rope_sin, "rope")
    mla_ops = (q_all, kv_all, kpe)
    mla_o, mla_lse = _attn_fwd(mla_ops, None, MLA_QK ** -0.5, bl, s, tq, "mla_attention")

    cat = _out_norm(fox_o, mla_o, fox_out_g, mla_out_g, "norm_out")
    x1 = _matmul(cat, w_o_f, "nn", F32, "proj_out", res=x2d)
    h2 = _rmsnorm(x1, 0, d, norm_ffn_g, BF16, "norm_ffn")
    gate, up, act = _ffn_up(h2, w_gate_t, w_up_t, "ffn_gate_up")
    x2 = _matmul(act, w_down_f, "nn", F32, "ffn_down", res=x1)
    dx2, dg_final, loss_part, dx2_b = _loss_bwd(x2, loss_target.reshape(t, d), final_norm_g.reshape(1, d), "final_norm_loss")

    d_gate, d_up = _ffn_down_bwd(dx2_b, w_down_f, gate, up, "d_ffn_down")
    dw_down = _matmul(act, dx2_b, "tn", BF16, "dw_down", tm=ff // 2, tn=d, tk=1024)
    dh2 = _matmul(d_up, w_up_t, "nn", F32, "d_ffn_up", res=_matmul(d_gate, w_gate_t, "nn", F32, "d_ffn_gate"))
    dw_gate = _matmul(d_gate, h2, "tn", BF16, "dw_gate", tm=ff // 2, tn=d, tk=1024)
    dw_up = _matmul(d_up, h2, "tn", BF16, "dw_up", tm=ff // 2, tn=d, tk=1024)
    dx1, dg_ffn, dx1_b = _rmsnorm_bwd(x1, 0, d, norm_ffn_g, dh2, "d_norm_ffn", res=dx2, mxu_copy=True)
    dcat = _matmul(dx1_b, w_o_f, "nt", F32, "d_proj_out")
    dw_o = _matmul(cat, dx1_b, "tn", BF16, "dw_o", tn=d, tk=1024)
    d_fox_o, d_mla_o, fox_delta, mla_delta, dg_fox, dg_mla = _out_norm_bwd(fox_o, mla_o, fox_out_g, mla_out_g, dcat, "d_norm_out")

    dproj_a, dc = _attn_bwd((proj_a,), d_fox_o, fox_lse, per_head_rows(fox_delta),
                            c_bias, HEAD_DIM ** -0.5, bl, s, tq, "d_fox_attention")
    dz, db_fgate = _fgate_bwd(z, bcol, dc.reshape(bh * grp, LANES), grp, "d_forget_gate")
    d_flogit = dz.reshape(bl, HEADS, s).transpose(0, 2, 1).reshape(t, HEADS)

    dq_nope, dkv_all, dq_pe, dk_pe = _attn_bwd(mla_ops, d_mla_o, mla_lse, per_head_rows(mla_delta),
                                               None, MLA_QK ** -0.5, bl, s, tq, "d_mla_attention")
    dq_rot, dk_rot = _rope_bwd(dq_pe, dk_pe, rope_cos, rope_sin, "d_rope")
    dqn = _matmul(dq_rot, w_uq_p[fw:], "nn", F32, "d_up_q_rope", res=_matmul(dq_nope, w_uq_p[:fw], "nn", F32, "d_up_q_nope"))
    dw_uq_nope = _matmul(dq_nope, qn, "tn", BF16, "dw_uq_nope", tn=q_rank, tk=1024)
    dw_uq_pe = _matmul(dq_rot, qn, "tn", BF16, "dw_uq_rope", tn=q_rank, tk=1024)
    dq_lat, dg_q = _rmsnorm_bwd(proj_b, 0, q_rank, q_norm_g, dqn, "d_norm_q")
    dkvn = _matmul(dkv_all, w_ukv_p, "nn", F32, "d_up_kv")
    dw_ukv_p = _matmul(dkv_all, kvn, "tn", BF16, "dw_ukv", tn=kv_rank, tk=1024)
    dkv_lat, dg_kv = _rmsnorm_bwd(proj_b, o_kvlat // kv_rank, kv_rank, kv_norm_g, dkvn, "d_norm_kv")

    dproj_b = jnp.concatenate([dq_lat.astype(BF16), dkv_lat.astype(BF16), dk_rot.astype(BF16), d_flogit.astype(BF16),
                               jnp.zeros((t, b_cols - o_flogit - HEADS), BF16)], axis=1)
    dh1 = _matmul(dproj_b, w_in_b, "nn", F32, "d_proj_mla", res=_matmul(dproj_a, w_in_a, "nn", F32, "d_proj_fox"))
    dw_in_a = _matmul(dproj_a, h1, "tn", BF16, "dw_in_fox", tn=d, tk=1024)
    dw_in_b = _matmul(dproj_b, h1, "tn", F32, "dw_in_mla", tn=d, tk=1024)
    grad_x, dg_mix = _rmsnorm_bwd(x2d, 0, d, norm_mix_g, dh1, "d_norm_mix", res=dx1)

    dw_krope = dw_in_b[o_krope:o_flogit].reshape(2, 4, half, d).sum(axis=1).reshape(MLA_ROPE, d)
    dw_in_t = jnp.concatenate([dw_in_a.reshape(PAIRS, 3, LANES, d).transpose(1, 0, 2, 3).reshape(n_qkv, d),
                               dw_in_b[o_flogit:o_flogit + HEADS].astype(BF16), dw_in_b[:o_krope].astype(BF16),
                               dw_krope.astype(BF16)], axis=0)
    per_dev = lambda a: a.reshape(N_DEV, -1, d)
    pad_dev = lambda a, rows: jnp.pad(a, ((0, 0), (0, rows - a.shape[1]), (0, 0)))
    dw_uq_pe5 = dw_uq_pe.reshape(2, 2, 4, half, q_rank)
    dw_uq_h = jnp.concatenate([dw_uq_nope.reshape(HEADS, HEAD_DIM, q_rank), dw_uq_pe5[:, 0].reshape(HEADS, half, q_rank),
                               dw_uq_pe5[:, 1].reshape(HEADS, half, q_rank)], axis=1)
    dw_ukv_h = dw_ukv_p.reshape(PAIRS, 2, 2, HEAD_DIM, kv_rank).transpose(0, 2, 1, 3, 4).reshape(HEADS, 2 * HEAD_DIM, kv_rank)
    grads = [pad_dev(per_dev(dw_in_t), pieces[0].shape[0]), pad_dev(per_dev(dw_uq_h), pieces[1].shape[0]), per_dev(dw_ukv_h),
             per_dev(dw_o), per_dev(dw_gate), per_dev(dw_up), per_dev(dw_down)]
    g_shard = _sum_blocks(_exchange(grads, "exchange_grads"), "sum_grads")

    def mine(i, rows):
        return g_shard[offs[i]:offs[i] + rows]

    big = [
        ("w_in", w_in, m_w_in, v_w_in, mine(0, in_cols).T),
        ("w_uq", w_uq, m_w_uq, v_w_uq, mine(1, uq_rows).reshape(-1, q_rank).T),
        ("w_ukv", w_ukv, m_w_ukv, v_w_ukv, mine(2, ukv_rows).reshape(-1, kv_rank).T),
        ("w_o", w_o, m_w_o, v_w_o, mine(3, w_o.shape[1])),
        ("w_gate", w_gate, m_w_gate, v_w_gate, mine(4, ff // N_DEV).T),
        ("w_up", w_up, m_w_up, v_w_up, mine(5, ff // N_DEV).T),
        ("w_down", w_down, m_w_down, v_w_down, mine(6, ff // N_DEV)),
    ]
    out = {}
    for nm, w, m, v, g in big:
        dl, new_m, new_v = _adamw(w[0], g, m[0], v[0], "adamw_" + nm)
        out[nm] = (g[None], dl[None], new_m[None], new_v[None])

    smalls = [("norm_mix_g", norm_mix_g, m_norm_mix_g, v_norm_mix_g, dg_mix),
              ("b_fgate", b_fgate, m_b_fgate, v_b_fgate, db_fgate.reshape(1, HEADS)),
              ("q_norm_g", q_norm_g, m_q_norm_g, v_q_norm_g, dg_q),
              ("kv_norm_g", kv_norm_g, m_kv_norm_g, v_kv_norm_g, dg_kv),
              ("fox_out_g", fox_out_g, m_fox_out_g, v_fox_out_g, dg_fox),
              ("mla_out_g", mla_out_g, m_mla_out_g, v_mla_out_g, dg_mla),
              ("norm_ffn_g", norm_ffn_g, m_norm_ffn_g, v_norm_ffn_g, dg_ffn),
              ("final_norm_g", final_norm_g, m_final_norm_g, v_final_norm_g, dg_final)]
    pack = lambda arrs: jnp.concatenate([_pad_lanes(a.reshape(1, -1)) for a in arrs], axis=1)
    blank = jnp.zeros((1, 1), F32)
    totals = _small_all_reduce_adamw(
        pack([e[4] for e in smalls] + [loss_part]), pack([e[1] for e in smalls] + [blank]),
        pack([e[2] for e in smalls] + [blank]), pack([e[3] for e in smalls] + [blank]), "reduce_small_adamw")
    pos = 0
    for nm, w, _, _, _ in smalls:
        out[nm] = tuple(a[0, pos:pos + w.size].reshape(w.shape) for a in totals)
        pos += -(-w.size // LANES) * LANES
    loss = totals[0][0, pos]

    order = ["norm_mix_g", "w_in", "b_fgate", "q_norm_g", "w_uq", "kv_norm_g", "w_ukv", "fox_out_g", "mla_out_g", "w_o",
             "norm_ffn_g", "w_gate", "w_up", "w_down", "final_norm_g"]
    return (loss, grad_x.reshape(bl, s, d), *[out[n][0] for n in order], *[out[n][1] for n in order],
            *[out[n][2] for n in order], *[out[n][3] for n in order])
```

```python
import jax
import jax.numpy as jnp
from jax import lax
from jax.experimental import pallas as pl
from jax.experimental.pallas import tpu as pltpu

F32 = jnp.float32
BF16 = jnp.bfloat16
MESH = pl.DeviceIdType.MESH

N_DEV = 8
HEADS = 8
HEAD_DIM = 64
PAIRS = HEADS // 2
MLA_ROPE = 32
MLA_QK = HEAD_DIM + MLA_ROPE
ROPE_THETA = 10000.0
NORM_EPS = 1e-6
ADAM_LR, ADAM_B1, ADAM_B2, ADAM_EPS, ADAM_WD, ADAM_STEP = 0.001, 0.9, 0.999, 1e-08, 0.01, 10

LANES = 128
MASKED = -1e30
VMEM_LIMIT = 48 * 1024 * 1024

_DIMS = {"nn": (((1,), (0,)), ((), ())), "nt": (((1,), (1,)), ((), ())), "tn": (((0,), (0,)), ((), ()))}


def _params(*sem):
    return pltpu.CompilerParams(dimension_semantics=sem, vmem_limit_bytes=VMEM_LIMIT)


def _dot(a, b, mode):
    return lax.dot_general(a.astype(BF16), b.astype(BF16), _DIMS[mode], preferred_element_type=F32)


def _tile(n, pref, unit=8):
    if n <= pref:
        return n
    t = pref - pref % unit
    while n % t:
        t -= unit
    return t


def _log2(n):
    assert n & (n - 1) == 0
    return n.bit_length() - 1


def _matmul(a, b, mode, out_dtype, name, tm=512, tn=512, tk=None, res=None):
    if mode == "nn":
        (m, kd), n = a.shape, b.shape[1]
    elif mode == "nt":
        (m, kd), n = a.shape, b.shape[0]
    else:
        (kd, m), n = a.shape, b.shape[1]
    tm, tn = _tile(m, tm, LANES if mode == "tn" else 16), _tile(n, tn, LANES)
    tk = kd if tk is None else _tile(kd, tk, LANES)
    nk = kd // tk
    a_spec = pl.BlockSpec((tk, tm), lambda i, j, k: (k, i)) if mode == "tn" else pl.BlockSpec((tm, tk), lambda i, j, k: (i, k))
    b_spec = pl.BlockSpec((tn, tk), lambda i, j, k: (j, k)) if mode == "nt" else pl.BlockSpec((tk, tn), lambda i, j, k: (k, j))
    o_spec = pl.BlockSpec((tm, tn), lambda i, j, k: (i, j))
    has_res = res is not None

    def body(*refs):
        a_ref, b_ref = refs[:2]
        r_ref = refs[2] if has_res else None
        o_ref = refs[3] if has_res else refs[2]

        def finish(acc):
            if has_res:
                acc = acc + r_ref[...]
            o_ref[...] = acc.astype(out_dtype)

        part = _dot(a_ref[...], b_ref[...], mode)
        if nk == 1:
            finish(part)
        else:
            acc_ref = refs[-1]
            k = pl.program_id(2)

            @pl.when(k == 0)
            def _():
                acc_ref[...] = part

            @pl.when(k > 0)
            def _():
                acc_ref[...] += part

            @pl.when(k == nk - 1)
            def _():
                finish(acc_ref[...])

    return pl.pallas_call(
        body, name=name, grid=(m // tm, n // tn, nk),
        in_specs=[a_spec, b_spec] + ([o_spec] if has_res else []), out_specs=o_spec,
        out_shape=jax.ShapeDtypeStruct((m, n), out_dtype),
        scratch_shapes=[pltpu.VMEM((tm, tn), F32)] if nk > 1 else [],
        compiler_params=_params("parallel", "parallel", "arbitrary"),
    )(*([a, b] + ([res] if has_res else [])))


def _rstd(x):
    return lax.rsqrt(jnp.mean(x * x, axis=-1, keepdims=True) + NORM_EPS)


def _norm_bwd(x, g, dy):
    r = _rstd(x)
    xh = x * r
    u = dy * g
    dx = r * (u - xh * jnp.mean(u * xh, axis=-1, keepdims=True))
    return dx, jnp.sum(dy * xh, axis=0, keepdims=True)


def _rmsnorm(x, col, width, g, out_dtype, name):
    t = x.shape[0]
    tm = _tile(t, 512)

    def body(x_ref, g_ref, o_ref):
        xv = x_ref[...]
        o_ref[...] = ((xv * _rstd(xv)) * g_ref[...]).astype(out_dtype)

    return pl.pallas_call(
        body, name=name, grid=(t // tm,),
        in_specs=[pl.BlockSpec((tm, width), lambda i: (i, col)), pl.BlockSpec((1, width), lambda i: (0, 0))],
        out_specs=pl.BlockSpec((tm, width), lambda i: (i, 0)),
        out_shape=jax.ShapeDtypeStruct((t, width), out_dtype),
        compiler_params=_params("parallel"),
    )(x, g)


def _rmsnorm_bwd(x, col, width, g, dy, name, res=None, mxu_copy=False):
    t = x.shape[0]
    tm = _tile(t, 512)
    has_res = res is not None
    row = pl.BlockSpec((tm, width), lambda i: (i, 0))
    vec = pl.BlockSpec((1, width), lambda i: (0, 0))

    def body(*refs):
        x_ref, g_ref, dy_ref = refs[:3]
        dx_ref, dg_ref = refs[3 + has_res:5 + has_res]
        dx, dg = _norm_bwd(x_ref[...], g_ref[...], dy_ref[...])
        if has_res:
            dx = dx + refs[3][...]
        dx_ref[...] = dx
        if mxu_copy:
            refs[-1][...] = dx.astype(BF16)

        @pl.when(pl.program_id(0) == 0)
        def _():
            dg_ref[...] = jnp.zeros_like(dg_ref)

        dg_ref[...] += dg

    return pl.pallas_call(
        body, name=name, grid=(t // tm,),
        in_specs=[pl.BlockSpec((tm, width), lambda i: (i, col)), vec, row] + ([row] if has_res else []),
        out_specs=(row, vec) + ((row,) if mxu_copy else ()),
        out_shape=(jax.ShapeDtypeStruct((t, width), F32), jax.ShapeDtypeStruct((1, width), F32))
        + ((jax.ShapeDtypeStruct((t, width), BF16),) if mxu_copy else ()),
        compiler_params=_params("arbitrary"),
    )(*([x, g, dy] + ([res] if has_res else [])))


def _out_norm(fo, mo, gf, gm, name):
    t, w = fo.shape
    tm = _tile(t, 512)
    row = pl.BlockSpec((tm, w), lambda i: (i, 0))
    vec = pl.BlockSpec((1, w), lambda i: (0, 0))

    def body(fo_ref, mo_ref, gf_ref, gm_ref, o_ref):
        f, m = fo_ref[...], mo_ref[...]
        o_ref[:, :w] = ((f * _rstd(f)) * gf_ref[...]).astype(BF16)
        o_ref[:, w:] = ((m * _rstd(m)) * gm_ref[...]).astype(BF16)

    return pl.pallas_call(
        body, name=name, grid=(t // tm,), in_specs=[row, row, vec, vec],
        out_specs=pl.BlockSpec((tm, 2 * w), lambda i: (i, 0)),
        out_shape=jax.ShapeDtypeStruct((t, 2 * w), BF16),
        compiler_params=_params("parallel"),
    )(fo, mo, gf, gm)


def _split3(x):
    hi = x.astype(BF16)
    r1 = x - hi.astype(F32)
    mid = r1.astype(BF16)
    lo = (r1 - mid.astype(F32)).astype(BF16)
    return hi, mid, lo


def _dot_x01(x, m01):
    hi, mid, lo = _split3(x)
    d = lambda p: lax.dot_general(p, m01, _DIMS["nn"], preferred_element_type=F32)
    return (d(lo) + d(mid)) + d(hi)


def _dot_01x(m01, x):
    hi, mid, lo = _split3(x)
    d = lambda p: lax.dot_general(m01, p, _DIMS["nn"], preferred_element_type=F32)
    return (d(lo) + d(mid)) + d(hi)


def _out_norm_bwd(fo, mo, gf, gm, dcat, name):
    t, w = fo.shape
    nh = w // HEAD_DIM
    tm = _tile(t, 512)
    row = pl.BlockSpec((tm, w), lambda i: (i, 0))
    vec = pl.BlockSpec((1, w), lambda i: (0, 0))
    hrow = pl.BlockSpec((tm, nh), lambda i: (i, 0))

    def body(fo_ref, mo_ref, gf_ref, gm_ref, dc_ref, dfo_ref, dmo_ref, ff_ref, fm_ref, dgf_ref, dgm_ref):
        lane_head = lax.shift_right_logical(lax.broadcasted_iota(jnp.int32, (w, nh), 0), _log2(HEAD_DIM))
        sel = (lane_head == lax.broadcasted_iota(jnp.int32, (w, nh), 1)).astype(BF16)
        f, m = fo_ref[...], mo_ref[...]
        dfo, dgf = _norm_bwd(f, gf_ref[...], dc_ref[:, :w])
        dmo, dgm = _norm_bwd(m, gm_ref[...], dc_ref[:, w:])
        dfo_ref[...] = dfo.astype(BF16)
        dmo_ref[...] = dmo.astype(BF16)
        ff_ref[...] = _dot_x01(dfo * f, sel)
        fm_ref[...] = _dot_x01(dmo * m, sel)

        @pl.when(pl.program_id(0) == 0)
        def _():
            dgf_ref[...] = jnp.zeros_like(dgf_ref)
            dgm_ref[...] = jnp.zeros_like(dgm_ref)

        dgf_ref[...] += dgf
        dgm_ref[...] += dgm

    return pl.pallas_call(
        body, name=name, grid=(t // tm,),
        in_specs=[row, row, vec, vec, pl.BlockSpec((tm, 2 * w), lambda i: (i, 0))],
        out_specs=(row, row, hrow, hrow, vec, vec),
        out_shape=(jax.ShapeDtypeStruct((t, w), BF16), jax.ShapeDtypeStruct((t, w), BF16),
                   jax.ShapeDtypeStruct((t, nh), F32), jax.ShapeDtypeStruct((t, nh), F32),
                   jax.ShapeDtypeStruct((1, w), F32), jax.ShapeDtypeStruct((1, w), F32)),
        compiler_params=_params("arbitrary"),
    )(fo, mo, gf, gm, dcat)


def _loss_bwd(x, tgt, g, name):
    t, d = x.shape
    tm = _tile(t, 512)
    row = pl.BlockSpec((tm, d), lambda i: (i, 0))
    vec = pl.BlockSpec((1, d), lambda i: (0, 0))
    one = pl.BlockSpec((1, 1), lambda i: (0, 0))

    def body(x_ref, t_ref, g_ref, dx_ref, dg_ref, loss_ref, dxb_ref):
        xv, gv = x_ref[...], g_ref[...]
        diff = (xv * _rstd(xv)) * gv - t_ref[...]
        dx, dg = _norm_bwd(xv, gv, diff / d)
        dx_ref[...] = dx
        dxb_ref[...] = dx.astype(BF16)

        @pl.when(pl.program_id(0) == 0)
        def _():
            dg_ref[...] = jnp.zeros_like(dg_ref)
            loss_ref[...] = jnp.zeros_like(loss_ref)

        dg_ref[...] += dg
        loss_ref[...] += 0.5 * jnp.sum(jnp.mean(diff * diff, axis=-1, keepdims=True), axis=0, keepdims=True)

    return pl.pallas_call(
        body, name=name, grid=(t // tm,), in_specs=[row, row, vec], out_specs=(row, vec, one, row),
        out_shape=(jax.ShapeDtypeStruct((t, d), F32), jax.ShapeDtypeStruct((1, d), F32), jax.ShapeDtypeStruct((1, 1), F32),
                   jax.ShapeDtypeStruct((t, d), BF16)),
        compiler_params=_params("arbitrary"),
    )(x, tgt, g)


def _ffn_up(h, wg_t, wu_t, name, tm=512, tf=1408):
    t, d = h.shape
    f = wg_t.shape[0]
    tm, tf = _tile(t, tm, 16), _tile(f, tf, LANES)
    tok = pl.BlockSpec((tm, tf), lambda i, j: (i, j))
    wt = pl.BlockSpec((tf, d), lambda i, j: (j, 0))

    def body(h_ref, wg_ref, wu_ref, g_ref, u_ref, a_ref):
        hv = h_ref[...]
        g, u = _dot(hv, wg_ref[...], "nt"), _dot(hv, wu_ref[...], "nt")
        g_ref[...], u_ref[...] = g, u
        a_ref[...] = ((g * jax.nn.sigmoid(g)) * u).astype(BF16)

    return pl.pallas_call(
        body, name=name, grid=(t // tm, f // tf), in_specs=[pl.BlockSpec((tm, d), lambda i, j: (i, 0)), wt, wt],
        out_specs=(tok, tok, tok),
        out_shape=(jax.ShapeDtypeStruct((t, f), F32), jax.ShapeDtypeStruct((t, f), F32), jax.ShapeDtypeStruct((t, f), BF16)),
        compiler_params=_params("parallel", "parallel"),
    )(h, wg_t, wu_t)


def _ffn_down_bwd(dy, w_down, gate, up, name, tm=512, tf=1408):
    t, d = dy.shape
    f = w_down.shape[0]
    tm, tf = _tile(t, tm, 16), _tile(f, tf, LANES)
    tok = pl.BlockSpec((tm, tf), lambda i, j: (i, j))

    def body(dy_ref, w_ref, g_ref, u_ref, dg_ref, du_ref):
        da = _dot(dy_ref[...], w_ref[...], "nt")
        g = g_ref[...]
        sg = jax.nn.sigmoid(g)
        dg_ref[...] = (da * u_ref[...] * (sg * (1.0 + g * (1.0 - sg)))).astype(BF16)
        du_ref[...] = (da * (g * sg)).astype(BF16)

    return pl.pallas_call(
        body, name=name, grid=(t // tm, f // tf),
        in_specs=[pl.BlockSpec((tm, d), lambda i, j: (i, 0)), pl.BlockSpec((tf, d), lambda i, j: (j, 0)), tok, tok],
        out_specs=(tok, tok),
        out_shape=(jax.ShapeDtypeStruct((t, f), BF16), jax.ShapeDtypeStruct((t, f), BF16)),
        compiler_params=_params("parallel", "parallel"),
    )(dy, w_down, gate, up)


def _chunk_scan_mats(rows, grp, reverse):
    ii = lax.broadcasted_iota(jnp.int32, (LANES, LANES), 0)
    jj = lax.broadcasted_iota(jnp.int32, (LANES, LANES), 1)
    within = ((ii >= jj) if reverse else (ii <= jj)).astype(BF16)
    ones = jnp.ones((LANES, LANES), BF16)
    ri = lax.broadcasted_iota(jnp.int32, (rows, rows), 0)
    rj = lax.broadcasted_iota(jnp.int32, (rows, rows), 1)
    sh = _log2(grp)
    same = lax.shift_right_logical(ri, sh) == lax.shift_right_logical(rj, sh)
    across = (same & ((rj > ri) if reverse else (rj < ri))).astype(BF16)
    return within, ones, across


def _running_sum(v, mats):
    within, ones, across = mats
    return _dot_x01(v, within) + _dot_01x(across, _dot_x01(v, ones))


def _fgate(z, bcol, grp, name):
    rows = z.shape[0]

    def body(z_ref, b_ref, c_ref):
        zz = z_ref[...] + b_ref[...]
        log_f = jnp.minimum(zz, 0.0) - jnp.log1p(jnp.exp(-jnp.abs(zz)))
        c_ref[...] = _running_sum(log_f, _chunk_scan_mats(rows, grp, False))

    return pl.pallas_call(body, name=name, out_shape=jax.ShapeDtypeStruct(z.shape, F32),
                          compiler_params=pltpu.CompilerParams(vmem_limit_bytes=VMEM_LIMIT))(z, bcol)


def _fgate_bwd(z, bcol, dc, grp, name):
    rows = z.shape[0]

    def body(z_ref, b_ref, dc_ref, dz_ref, db_ref):
        zz = z_ref[...] + b_ref[...]
        dz = _running_sum(dc_ref[...], _chunk_scan_mats(rows, grp, True)) * jax.nn.sigmoid(-zz)
        dz_ref[...] = dz
        head = lax.shift_right_logical(lax.broadcasted_iota(jnp.int32, (HEADS, rows), 1), _log2(grp)) & (HEADS - 1)
        sel = (head == lax.broadcasted_iota(jnp.int32, (HEADS, rows), 0)).astype(BF16)
        db_ref[...] = jnp.sum(_dot_01x(sel, dz), axis=1, keepdims=True)

    return pl.pallas_call(
        body, name=name,
        out_shape=(jax.ShapeDtypeStruct(z.shape, F32), jax.ShapeDtypeStruct((HEADS, 1), F32)),
        compiler_params=pltpu.CompilerParams(vmem_limit_bytes=VMEM_LIMIT),
    )(z, bcol, dc)


def _rotate(x, cs, sn_signed):
    return x * cs + pltpu.roll(x, LANES // 2, axis=1) * sn_signed


def _rope_fwd(q_raw, nope, proj_b, k_col, cs, sn, name):
    t, qw = q_raw.shape
    tm = _tile(t, 512)
    row = pl.BlockSpec((tm, LANES), lambda i: (i, 0))

    def body(q_ref, k_ref, c_ref, s_ref, qo_ref, ko_ref):
        c, s = c_ref[...], s_ref[...]
        qo_ref[:, :nope] = q_ref[:, :nope].astype(BF16)
        for off in range(nope, qw, LANES):
            qo_ref[:, off:off + LANES] = _rotate(q_ref[:, off:off + LANES], c, s).astype(BF16)
        ko_ref[...] = _rotate(k_ref[...], c, s).astype(BF16)

    return pl.pallas_call(
        body, name=name, grid=(t // tm,),
        in_specs=[pl.BlockSpec((tm, qw), lambda i: (i, 0)), pl.BlockSpec((tm, LANES), lambda i: (i, k_col)), row, row],
        out_specs=(pl.BlockSpec((tm, qw), lambda i: (i, 0)), row),
        out_shape=(jax.ShapeDtypeStruct((t, qw), BF16), jax.ShapeDtypeStruct((t, LANES), BF16)),
        compiler_params=_params("parallel"),
    )(q_raw, proj_b, cs, sn)


def _rope_bwd(dq_pe, dk_pe, cs, sn, name):
    t, qw = dq_pe.shape
    tm = _tile(t, 512)
    row = pl.BlockSpec((tm, LANES), lambda i: (i, 0))
    wide = pl.BlockSpec((tm, qw), lambda i: (i, 0))

    def body(q_ref, k_ref, c_ref, s_ref, qo_ref, ko_ref):
        c, s = c_ref[...], -s_ref[...]
        for off in range(0, qw, LANES):
            qo_ref[:, off:off + LANES] = _rotate(q_ref[:, off:off + LANES], c, s).astype(BF16)
        ko_ref[...] = _rotate(k_ref[...], c, s)

    return pl.pallas_call(
        body, name=name, grid=(t // tm,), in_specs=[wide, row, row, row], out_specs=(wide, row),
        out_shape=(jax.ShapeDtypeStruct((t, qw), BF16), jax.ShapeDtypeStruct((t, LANES), F32)),
        compiler_params=_params("parallel"),
    )(dq_pe, dk_pe, cs, sn)


def _lane_masks(pair, h, pe):
    lane = lax.broadcasted_iota(jnp.int32, (1, LANES), 1)
    in_head = lax.shift_right_logical(lane, _log2(HEAD_DIM)) == h
    in_rope = ((lax.shift_right_logical(lane, _log2(MLA_ROPE // 2)) & 3) == ((2 * pair + h) & 3)) if pe else None
    return in_head, in_rope


def _keep(mask, v):
    return jnp.where(mask, v, jnp.zeros_like(v))


def _to_row(col):
    n = col.shape[0]
    eye = lax.broadcasted_iota(jnp.int32, (n, n), 0) == lax.broadcasted_iota(jnp.int32, (n, n), 1)
    return jnp.sum(jnp.where(eye, col, 0.0), axis=0, keepdims=True)


def _to_col(row):
    n = row.shape[1]
    eye = lax.broadcasted_iota(jnp.int32, (n, n), 0) == lax.broadcasted_iota(jnp.int32, (n, n), 1)
    return jnp.sum(jnp.where(eye, row, 0.0), axis=1, keepdims=True)


def _first_step():
    return (pl.program_id(0) == 0) & (pl.program_id(1) == 0)


def _last_step(n0, n1):
    return (pl.program_id(0) == n0 - 1) & (pl.program_id(1) == n1 - 1)


def _attn_fwd(ops, bias, scale, bl, s, tq, name, traffic=None):
    pe = len(ops) == 3
    has_bias = bias is not None
    nq = s // tq
    t = bl * s
    n_carried = len(traffic.pieces) if traffic else 0

    def body(*refs):
        sems = refs[len(refs) - 3:] if traffic else ()
        if pe:
            q_ref, qpe_ref, kv_ref, kpe_ref = refs[:4]
            n_in = 4
            q_at = lambda r0, r1: q_ref[r0:r1, :]
            v_at = lambda r0, r1: kv_ref[r0:r1, LANES:]
            kcat = refs[len(refs) - 1 - len(sems)]
            kcat[:, :LANES] = kv_ref[:, :LANES]
            kcat[:, LANES:] = kpe_ref[...]
            k_at = lambda r0, r1: kcat[r0:r1, :]
        else:
            qkv_ref = refs[0]
            n_in = 1
            q_at = lambda r0, r1: qkv_ref[r0:r1, :LANES]
            k_at = lambda r0, r1: qkv_ref[r0:r1, LANES:2 * LANES]
            v_at = lambda r0, r1: qkv_ref[r0:r1, 2 * LANES:]
        if has_bias:
            c_ref = refs[n_in]
            n_in += 1
        carried_in = refs[n_in:n_in + n_carried]
        n_in += n_carried
        o_ref, lse_ref = refs[n_in:n_in + 2]
        if traffic:
            carried_out = refs[n_in + 2]

            @pl.when(_first_step())
            def _():
                traffic.start(carried_in, carried_out, *sems)

        pair = pl.program_id(1)
        causal = lax.broadcasted_iota(jnp.int32, (tq, tq), 1) <= lax.broadcasted_iota(jnp.int32, (tq, tq), 0)
        o_ref[...] = jnp.zeros_like(o_ref)

        def head(h, carry):
            in_head, in_rope = _lane_masks(pair, h, pe)
            for i in range(nq):
                r0, r1 = i * tq, (i + 1) * tq
                qm = _keep(in_head, q_at(r0, r1))
                if pe:
                    qm = jnp.concatenate([qm, _keep(in_rope, qpe_ref[r0:r1, :])], axis=1)
                cq = _to_col(c_ref[h, :, r0:r1]) if has_bias else None

                def logits(k0, k1):
                    sc = _dot(qm, k_at(k0, k1), "nt") * scale
                    if has_bias:
                        sc = sc + (cq - c_ref[h, :, k0:k1])
                    return sc

                sd = jnp.where(causal, logits(r0, r1), MASKED)
                m = jnp.max(sd, axis=1, keepdims=True)
                if i:
                    so = logits(0, r0)
                    m = jnp.maximum(m, jnp.max(so, axis=1, keepdims=True))
                pd = jnp.exp(sd - m)
                l = jnp.sum(pd, axis=1, keepdims=True)
                acc = _dot(pd, v_at(r0, r1), "nn")
                if i:
                    po = jnp.exp(so - m)
                    l = l + jnp.sum(po, axis=1, keepdims=True)
                    acc = acc + _dot(po, v_at(0, r0), "nn")
                o_ref[r0:r1, :] = jnp.where(in_head, acc / l, o_ref[r0:r1, :])
                lse_ref[h, :, r0:r1] = _to_row(m + jnp.log(l))
            return carry

        lax.fori_loop(0, 2, head, 0)
        if traffic:
            @pl.when(_last_step(bl, PAIRS))
            def _():
                traffic.wait(carried_out, *sems)

    seq = lambda w, col: pl.BlockSpec((s, w), col)
    if pe:
        in_specs = [seq(LANES, lambda b, p: (b, p)), seq(LANES, lambda b, p: (b, PAIRS + p // 2)),
                    seq(2 * LANES, lambda b, p: (b, p)), seq(LANES, lambda b, p: (b, 0))]
        args = [ops[0], ops[0], ops[1], ops[2]]
        scratch = [pltpu.VMEM((s, 2 * LANES), BF16)]
    else:
        in_specs = [seq(3 * LANES, lambda b, p: (b, p))]
        args = [ops[0]]
        scratch = []
    per_head_row = pl.BlockSpec((2, 1, s), lambda b, p: (b * PAIRS + p, 0, 0))
    if has_bias:
        in_specs.append(per_head_row)
        args.append(bias)
    out_specs = [seq(LANES, lambda b, p: (b, p)), per_head_row]
    out_shape = [jax.ShapeDtypeStruct((t, HEADS * HEAD_DIM), F32), jax.ShapeDtypeStruct((bl * HEADS, 1, s), F32)]
    if traffic:
        in_specs += traffic.in_specs
        args += traffic.pieces
        out_specs.append(traffic.out_spec)
        out_shape.append(traffic.out_shape)
        scratch += traffic.scratch
    return pl.pallas_call(
        body, name=name, grid=(bl, PAIRS), in_specs=in_specs, out_specs=tuple(out_specs), out_shape=tuple(out_shape),
        scratch_shapes=scratch, compiler_params=_params(*(("arbitrary", "arbitrary") if traffic else ("parallel", "parallel"))),
    )(*args)


def _attn_bwd(ops, do, lse, delta, bias, scale, bl, s, tq, name, traffic=None):
    pe = len(ops) == 3
    has_bias = bias is not None
    nq = s // tq
    t = bl * s
    width = 2 * LANES if pe else LANES
    n_carried = len(traffic.pieces) if traffic else 0

    def body(*refs):
        if pe:
            q_ref, qpe_ref, kv_ref, kpe_ref = refs[:4]
            n_in = 4
            k_at = lambda r0, r1: kv_ref[r0:r1, :LANES]
            v_at = lambda r0, r1: kv_ref[r0:r1, LANES:]
        else:
            qkv_ref = refs[0]
            n_in = 1
            k_at = lambda r0, r1: qkv_ref[r0:r1, LANES:2 * LANES]
            v_at = lambda r0, r1: qkv_ref[r0:r1, 2 * LANES:]
        do_ref, lse_ref, dl_ref = refs[n_in:n_in + 3]
        n_in += 3
        if has_bias:
            c_ref = refs[n_in]
            n_in += 1
        carried_in = refs[n_in:n_in + n_carried]
        rest = refs[n_in + n_carried:]
        if traffic:
            rest, sems = rest[:-3], rest[-3:]
            carried_out = rest[4 if pe else 2]
            rest = rest[:4 if pe else 2] + rest[(4 if pe else 2) + 1:]

            @pl.when(_first_step())
            def _():
                traffic.start(carried_in, carried_out, *sems)

        if pe:
            dqn_ref, dkv_ref, dqpe_ref, dkpe_ref, dq_acc, qcat = rest
            qcat[:, :LANES] = q_ref[...]
            qcat[:, LANES:] = qpe_ref[...]
            q_at = lambda r0, r1: qcat[r0:r1, :]
            dkv_ref[...] = jnp.zeros_like(dkv_ref)
        else:
            dqkv_ref, dc_ref, dq_acc = rest
            q_at = lambda r0, r1: qkv_ref[r0:r1, :LANES]
            dqkv_ref[...] = jnp.zeros_like(dqkv_ref)
            dc_ref[...] = jnp.zeros_like(dc_ref)
        pair = pl.program_id(1)
        dq_acc[...] = jnp.zeros_like(dq_acc)
        causal = lax.broadcasted_iota(jnp.int32, (tq, tq), 1) >= lax.broadcasted_iota(jnp.int32, (tq, tq), 0)

        def head(h, carry):
            in_head, in_rope = _lane_masks(pair, h, pe)
            for j in range(nq):
                r0, r1 = j * tq, (j + 1) * tq
                kt = _keep(in_head, k_at(r0, r1))
                if pe:
                    kt = jnp.concatenate([kt, _keep(in_rope, kpe_ref[r0:r1, :])], axis=1)
                vt = _keep(in_head, v_at(r0, r1))
                ck = _to_col(c_ref[h, :, r0:r1]) if has_bias else None

                def block(q0, q1, diagonal):
                    qq, dd = q_at(q0, q1), do_ref[q0:q1, :]
                    st = _dot(kt, qq, "nt") * scale
                    if has_bias:
                        st = st + (c_ref[h, :, q0:q1] - ck)
                    if diagonal:
                        st = jnp.where(causal, st, MASKED)
                    pt = jnp.exp(st - lse_ref[h, :, q0:q1])
                    dst = pt * (_dot(vt, dd, "nt") - dl_ref[h, :, q0:q1])
                    dsb = (dst * scale).astype(BF16)
                    dq_acc[q0:q1, :] += _dot(dsb, kt, "tn")
                    if has_bias:
                        dc_ref[h, :, q0:q1] += jnp.sum(dst, axis=0, keepdims=True)
                    return _dot(pt, dd, "nn"), _dot(dsb, qq, "nn"), (jnp.sum(dst, axis=1, keepdims=True) if has_bias else None)

                dv_c, dk_c, cs = block(r0, r1, True)
                if r1 < s:
                    dv_o, dk_o, cs_o = block(r1, s, False)
                    dv_c, dk_c = dv_c + dv_o, dk_c + dk_o
                    cs = cs + cs_o if has_bias else None
                if pe:
                    dkv_ref[r0:r1, :LANES] = jnp.where(in_head, dk_c[:, :LANES].astype(BF16), dkv_ref[r0:r1, :LANES])
                    dkv_ref[r0:r1, LANES:] = jnp.where(in_head, dv_c.astype(BF16), dkv_ref[r0:r1, LANES:])
                    dkpe_ref[r0:r1, :] += _keep(in_rope, dk_c[:, LANES:])
                else:
                    dqkv_ref[r0:r1, LANES:2 * LANES] = jnp.where(in_head, dk_c.astype(BF16), dqkv_ref[r0:r1, LANES:2 * LANES])
                    dqkv_ref[r0:r1, 2 * LANES:] = jnp.where(in_head, dv_c.astype(BF16), dqkv_ref[r0:r1, 2 * LANES:])
                    dc_ref[h, :, r0:r1] -= _to_row(cs)
            return carry

        if pe:
            @pl.when(pair == 0)
            def _():
                dkpe_ref[...] = jnp.zeros_like(dkpe_ref)

            @pl.when(pair % 2 == 0)
            def _():
                dqpe_ref[...] = jnp.zeros_like(dqpe_ref)

        lax.fori_loop(0, 2, head, 0)
        if pe:
            dqn_ref[...] = dq_acc[:, :LANES].astype(BF16)
            dqpe_ref[...] += dq_acc[:, LANES:]
        else:
            dqkv_ref[:, :LANES] = dq_acc[...].astype(BF16)
        if traffic:
            @pl.when(_last_step(bl, PAIRS))
            def _():
                traffic.wait(carried_out, *sems)

    seq = lambda w, col: pl.BlockSpec((s, w), col)
    per_head_row = pl.BlockSpec((2, 1, s), lambda b, p: (b * PAIRS + p, 0, 0))
    if pe:
        in_specs = [seq(LANES, lambda b, p: (b, p)), seq(LANES, lambda b, p: (b, PAIRS + p // 2)),
                    seq(2 * LANES, lambda b, p: (b, p)), seq(LANES, lambda b, p: (b, 0))]
        args = [ops[0], ops[0], ops[1], ops[2]]
    else:
        in_specs = [seq(3 * LANES, lambda b, p: (b, p))]
        args = [ops[0]]
    in_specs += [seq(LANES, lambda b, p: (b, p)), per_head_row, per_head_row]
    args += [do, lse, delta]
    if has_bias:
        in_specs.append(per_head_row)
        args.append(bias)
    scratch = [pltpu.VMEM((s, width), F32)]
    if pe:
        out_specs = (seq(LANES, lambda b, p: (b, p)), seq(2 * LANES, lambda b, p: (b, p)),
                     seq(LANES, lambda b, p: (b, p // 2)), seq(LANES, lambda b, p: (b, 0)))
        out_shape = (jax.ShapeDtypeStruct((t, PAIRS * LANES), BF16), jax.ShapeDtypeStruct((t, PAIRS * 2 * LANES), BF16),
                     jax.ShapeDtypeStruct((t, 2 * LANES), F32), jax.ShapeDtypeStruct((t, LANES), F32))
        scratch.append(pltpu.VMEM((s, 2 * LANES), BF16))
    else:
        out_specs = (seq(3 * LANES, lambda b, p: (b, p)), per_head_row)
        out_shape = (jax.ShapeDtypeStruct((t, PAIRS * 3 * LANES), BF16), jax.ShapeDtypeStruct((bl * HEADS, 1, s), F32))
    if traffic:
        in_specs += traffic.in_specs
        args += traffic.pieces
        out_specs += (traffic.out_spec,)
        out_shape += (traffic.out_shape,)
        scratch += traffic.scratch
    return pl.pallas_call(
        body, name=name, grid=(bl, PAIRS), in_specs=in_specs, out_specs=out_specs, out_shape=out_shape,
        scratch_shapes=scratch, compiler_params=_params("arbitrary" if traffic else "parallel", "arbitrary"),
    )(*args)


def _my_place():
    return lax.axis_index("x"), lax.axis_index("y"), lax.axis_index("c")


def _flip(p, bit):
    return 1 - p if bit else p


def _relative(x, y, c, k):
    return _flip(x, k & 4), _flip(y, k & 2), _flip(c, k & 1)


def _linear(x, y, c):
    return 4 * x + 2 * y + c


def _all_gather(shard, name):
    r, cdim = shard.shape

    def body(x_ref, out_ref, send_sems, recv_sems, local_sem):
        x, y, c = _my_place()
        me, sibling = (x, y, c), (x, y, 1 - c)
        chips = [(1 - x, y), (x, 1 - y), (1 - x, 1 - y)]

        def slot(px, py, pc):
            return out_ref.at[_linear(px, py, pc)]

        def copy(k, block, to, src=None):
            return pltpu.make_async_remote_copy(
                src_ref=slot(*block) if src is None else src, dst_ref=slot(*block),
                send_sem=send_sems.at[k], recv_sem=recv_sems.at[k], device_id=to, device_id_type=MESH)

        mine = pltpu.make_async_copy(x_ref, slot(*me), local_sem)
        mine.start()
        first = [copy(0, me, sibling, src=x_ref)] + [copy(1 + j, me, (*chip, c), src=x_ref) for j, chip in enumerate(chips)]
        for cp in first:
            cp.start()
        passed = [copy(4 + j, (*chip, c), sibling) for j, chip in enumerate(chips)]
        for j, chip in enumerate(chips):
            copy(1 + j, (*chip, c), me).wait_recv()
            passed[j].start()
        copy(0, sibling, me).wait_recv()
        for j, chip in enumerate(chips):
            copy(4 + j, (*chip, 1 - c), me).wait_recv()
        for cp in first + passed:
            cp.wait_send()
        mine.wait()

    return pl.pallas_call(
        body, name=name, out_shape=jax.ShapeDtypeStruct((N_DEV, r, cdim), shard.dtype),
        in_specs=[pl.BlockSpec(memory_space=pl.ANY)], out_specs=pl.BlockSpec(memory_space=pl.ANY),
        scratch_shapes=[pltpu.SemaphoreType.DMA((7,)), pltpu.SemaphoreType.DMA((7,)), pltpu.SemaphoreType.DMA(())],
    )(shard)


class _Traffic:
    def __init__(self, kind, pieces):
        self.kind, self.pieces = kind, list(pieces)
        self.rows = [p.shape[-2] for p in self.pieces]
        self.starts = [sum(self.rows[:i]) for i in range(len(self.rows))]
        anywhere = pl.BlockSpec(memory_space=pl.ANY)
        self.in_specs = [anywhere] * len(self.pieces)
        self.out_spec = anywhere
        self.out_shape = jax.ShapeDtypeStruct((N_DEV, sum(self.rows), self.pieces[0].shape[-1]), self.pieces[0].dtype)
        self.scratch = [pltpu.SemaphoreType.DMA((7,)), pltpu.SemaphoreType.DMA((7,)), pltpu.SemaphoreType.DMA(())]

    def start(self, p_refs, out_ref, send_sems, recv_sems, local_sem):
        x, y, c = _my_place()
        me = _linear(x, y, c)
        mine = lambda i, dev: p_refs[i] if self.kind == "spread" else p_refs[i].at[dev]
        landing = lambda i: out_ref.at[me, pl.ds(self.starts[i], self.rows[i])]
        for i in range(len(p_refs)):
            pltpu.make_async_copy(mine(i, me), landing(i), local_sem).start()
        for k in range(1, N_DEV):
            peer = _relative(x, y, c, k)
            for i in range(len(p_refs)):
                pltpu.make_async_remote_copy(
                    src_ref=mine(i, _linear(*peer)), dst_ref=landing(i),
                    send_sem=send_sems.at[k - 1], recv_sem=recv_sems.at[k - 1], device_id=peer, device_id_type=MESH).start()

    def wait(self, out_ref, send_sems, recv_sems, local_sem):
        x, y, c = _my_place()
        whole = out_ref.at[_linear(x, y, c)]
        for k in range(1, N_DEV):
            both = pltpu.make_async_remote_copy(
                src_ref=whole, dst_ref=whole, send_sem=send_sems.at[k - 1], recv_sem=recv_sems.at[k - 1],
                device_id=_relative(x, y, c, k), device_id_type=MESH)
            both.wait_recv()
            both.wait_send()
        pltpu.make_async_copy(whole, whole, local_sem).wait()


def _exchange(pieces, name):
    traffic = _Traffic("swap", pieces)
    n = len(pieces)

    def body(*refs):
        traffic.start(refs[:n], *refs[n:])
        traffic.wait(*refs[n:])

    return pl.pallas_call(
        body, name=name, out_shape=traffic.out_shape, in_specs=traffic.in_specs, out_specs=traffic.out_spec,
        scratch_shapes=traffic.scratch,
    )(*pieces)


def _sum_blocks(parts, name):
    n, r, cdim = parts.shape
    tr = _tile(r, 640, 16)

    def body(p_ref, o_ref):
        acc = p_ref[0].astype(F32)
        for d in range(1, n):
            acc = acc + p_ref[d].astype(F32)
        o_ref[...] = acc

    return pl.pallas_call(
        body, name=name, grid=(r // tr,), in_specs=[pl.BlockSpec((n, tr, cdim), lambda i: (0, i, 0))],
        out_specs=pl.BlockSpec((tr, cdim), lambda i: (i, 0)), out_shape=jax.ShapeDtypeStruct((r, cdim), F32),
        compiler_params=_params("parallel"),
    )(parts)


def _adamw_math(w, g, m, v):
    m = ADAM_B1 * m + (1.0 - ADAM_B1) * g
    v = ADAM_B2 * v + (1.0 - ADAM_B2) * (g * g)
    m_hat = m / (1.0 - ADAM_B1 ** ADAM_STEP)
    v_hat = v / (1.0 - ADAM_B2 ** ADAM_STEP)
    delta = -ADAM_LR * (m_hat / (jnp.sqrt(v_hat) + ADAM_EPS) + ADAM_WD * w)
    return delta, m, v


def _adamw(w, g, m, v, name):
    def body(w_ref, g_ref, m_ref, v_ref, d_ref, nm_ref, nv_ref):
        d_ref[...], nm_ref[...], nv_ref[...] = _adamw_math(w_ref[...], g_ref[...], m_ref[...], v_ref[...])

    out = jax.ShapeDtypeStruct(w.shape, F32)
    return pl.pallas_call(body, name=name, out_shape=(out, out, out),
                          compiler_params=pltpu.CompilerParams(vmem_limit_bytes=VMEM_LIMIT))(w, g, m, v)


def _small_all_reduce_adamw(part, w, m, v, name):
    width = part.shape[1]

    def body(p_ref, w_ref, m_ref, v_ref, tot_ref, d_ref, nm_ref, nv_ref, rows, send_sems, recv_sems):
        x, y, c = _my_place()
        me = _linear(x, y, c)
        rows[me] = p_ref[...]
        copies = []
        for k in range(1, N_DEV):
            copies.append(pltpu.make_async_remote_copy(
                src_ref=rows.at[me], dst_ref=rows.at[me], send_sem=send_sems.at[k - 1], recv_sem=recv_sems.at[k - 1],
                device_id=_relative(x, y, c, k), device_id_type=MESH))
        for cp in copies:
            cp.start()
        for cp in copies:
            cp.wait_recv()
        for cp in copies:
            cp.wait_send()
        total = rows[0]
        for d in range(1, N_DEV):
            total = total + rows[d]
        tot_ref[...] = total
        d_ref[...], nm_ref[...], nv_ref[...] = _adamw_math(w_ref[...], total, m_ref[...], v_ref[...])

    out = jax.ShapeDtypeStruct((1, width), F32)
    return pl.pallas_call(
        body, name=name, out_shape=(out, out, out, out),
        scratch_shapes=[pltpu.VMEM((N_DEV, 1, width), F32), pltpu.SemaphoreType.DMA((7,)), pltpu.SemaphoreType.DMA((7,))],
    )(part, w, m, v)


def _pad_rows(a, rows):
    return jnp.pad(a, ((0, rows - a.shape[0]), (0, 0)))


def _pad_lanes(a):
    return jnp.pad(a, ((0, 0), (0, -a.shape[1] % LANES)))


def kernel(x, positions, norm_mix_g, w_in, b_fgate, q_norm_g, w_uq, kv_norm_g, w_ukv, fox_out_g, mla_out_g, w_o, norm_ffn_g, w_gate, w_up, w_down, final_norm_g, loss_target, m_norm_mix_g, m_w_in, m_b_fgate, m_q_norm_g, m_w_uq, m_kv_norm_g, m_w_ukv, m_fox_out_g, m_mla_out_g, m_w_o, m_norm_ffn_g, m_w_gate, m_w_up, m_w_down, m_final_norm_g, v_norm_mix_g, v_w_in, v_b_fgate, v_q_norm_g, v_w_uq, v_kv_norm_g, v_w_ukv, v_fox_out_g, v_mla_out_g, v_w_o, v_norm_ffn_g, v_w_gate, v_w_up, v_w_down, v_final_norm_g):
    bl, s, d = x.shape
    t = bl * s
    bh = bl * HEADS
    tq = _tile(s, 256)
    grp = s // LANES
    fw = HEADS * HEAD_DIM
    q_rank, kv_rank = w_uq.shape[1], w_ukv.shape[1]
    in_cols = w_in.shape[2]
    n_in = N_DEV * in_cols
    ff = N_DEV * w_gate.shape[2]
    half = MLA_ROPE // 2
    o_kvlat, o_krope, o_flogit = q_rank, q_rank + kv_rank, q_rank + kv_rank + LANES
    b_cols = -(-(o_flogit + HEADS) // LANES) * LANES

    tr = lambda w: jnp.transpose(w[0])
    in_rows = -(-in_cols // 16) * 16
    uq_rows = w_uq.shape[2] * q_rank // d
    ukv_rows = w_ukv.shape[2] * kv_rank // d
    pieces = [_pad_rows(tr(w_in), in_rows), _pad_rows(tr(w_uq).reshape(uq_rows, d), -(-uq_rows // 16) * 16),
              tr(w_ukv).reshape(ukv_rows, d), w_o[0], tr(w_gate), tr(w_up), w_down[0]]
    pieces = [p.astype(BF16) for p in pieces]
    offs = [0]
    for p in pieces:
        offs.append(offs[-1] + p.shape[0])
    early = 1
    gathered_early = _all_gather(pieces[0], "gather_w_in")

    def full(i, rows):
        src, base = (gathered_early, 0) if i < early else (gathered_late, offs[early])
        return src[:, offs[i] - base:offs[i] - base + rows]

    w_in_t = full(0, in_cols).reshape(n_in, d)
    n_qkv = 3 * fw
    w_in_a = w_in_t[:n_qkv].reshape(3, PAIRS, LANES, d).transpose(1, 0, 2, 3).reshape(n_qkv, d)
    lat0, rope0 = n_qkv + HEADS, n_qkv + HEADS + q_rank + kv_rank
    k_rep = jnp.broadcast_to(w_in_t[rope0:].reshape(2, 1, half, d), (2, 4, half, d)).reshape(LANES, d)
    w_in_b = jnp.concatenate([w_in_t[lat0:rope0], k_rep, w_in_t[n_qkv:lat0],
                              jnp.zeros((b_cols - o_flogit - HEADS, d), BF16)], axis=0)

    def per_head_rows(a):
        return a.reshape(bl, s, HEADS).transpose(0, 2, 1).reshape(bh, 1, s)

    x2d = x.reshape(t, d)
    h1 = _rmsnorm(x2d, 0, d, norm_mix_g, BF16, "norm_mix")
    proj_a = _matmul(h1, w_in_a, "nt", BF16, "proj_fox", tn=3 * LANES)
    proj_b = _matmul(h1, w_in_b, "nt", F32, "proj_mla", tn=b_cols)

    z = proj_b[:, o_flogit:o_flogit + HEADS].reshape(bl, s, HEADS).transpose(0, 2, 1).reshape(bh * grp, LANES)
    bcol = jnp.broadcast_to(b_fgate.reshape(1, HEADS, 1), (bl, HEADS, grp)).reshape(bh * grp, 1)
    c = _fgate(z, bcol, grp, "forget_gate")
    c_bias = c.reshape(bh, 1, s)
    fox_o, fox_lse, gathered_late = _attn_fwd((proj_a,), c_bias, HEAD_DIM ** -0.5, bl, s, tq, "fox_attention",
                                              traffic=_Traffic("spread", pieces[early:]))
    w_uq_h = full(1, uq_rows).reshape(HEADS, MLA_QK, q_rank)
    w_uq_pe = jnp.concatenate([w_uq_h[:, HEAD_DIM:HEAD_DIM + half].reshape(2, 1, 4 * half, q_rank),
                               w_uq_h[:, HEAD_DIM + half:].reshape(2, 1, 4 * half, q_rank)], axis=1).reshape(2 * LANES, q_rank)
    w_uq_p = jnp.concatenate([w_uq_h[:, :HEAD_DIM].reshape(fw, q_rank), w_uq_pe], axis=0)
    w_ukv_p = full(2, ukv_rows).reshape(PAIRS, 2, 2, HEAD_DIM, kv_rank).transpose(0, 2, 1, 3, 4).reshape(2 * fw, kv_rank)
    w_o_f = full(3, w_o.shape[1]).reshape(-1, d)
    w_gate_t, w_up_t = full(4, ff // N_DEV).reshape(ff, d), full(5, ff // N_DEV).reshape(ff, d)
    w_down_f = full(6, ff // N_DEV).reshape(ff, d)

    qn = _rmsnorm(proj_b, 0, q_rank, q_norm_g, BF16, "norm_q")
    kvn = _rmsnorm(proj_b, o_kvlat // kv_rank, kv_rank, kv_norm_g, BF16, "norm_kv")
    q_raw = _matmul(qn, w_uq_p, "nt", F32, "up_q", tn=fw + 2 * LANES)
    kv_all = _matmul(kvn, w_ukv_p, "nt", BF16, "up_kv")
    inv_freq = ROPE_THETA ** (-jnp.arange(0, MLA_ROPE, 2, dtype=F32) / MLA_ROPE)
    ang = positions.astype(F32).reshape(t, 1) * inv_freq[None, :]
    cos4, sin4 = jnp.tile(jnp.cos(ang), (1, 4)), jnp.tile(jnp.sin(ang), (1, 4))
    rope_cos, rope_sin = jnp.concatenate([cos4, cos4], axis=1), jnp.concatenate([-sin4, sin4], axis=1)
    q_all, kpe = _rope_fwd(q_raw, fw, proj_b, o_krope // LANES, rope_cos, rope_sin, "rope")
    mla_ops = (q_all, kv_all, kpe)
    mla_o, mla_lse = _attn_fwd(mla_ops, None, MLA_QK ** -0.5, bl, s, tq, "mla_attention")

    cat = _out_norm(fox_o, mla_o, fox_out_g, mla_out_g, "norm_out")
    x1 = _matmul(cat, w_o_f, "nn", F32, "proj_out", res=x2d)
    h2 = _rmsnorm(x1, 0, d, norm_ffn_g, BF16, "norm_ffn")
    gate, up, act = _ffn_up(h2, w_gate_t, w_up_t, "ffn_gate_up")
    x2 = _matmul(act, w_down_f, "nn", F32, "ffn_down", res=x1)
    dx2, dg_final, loss_part, dx2_b = _loss_bwd(x2, loss_target.reshape(t, d), final_norm_g.reshape(1, d), "final_norm_loss")

    d_gate, d_up = _ffn_down_bwd(dx2_b, w_down_f, gate, up, "d_ffn_down")
    dw_down = _matmul(act, dx2_b, "tn", BF16, "dw_down", tm=ff // 2, tn=d, tk=1024)
    dh2 = _matmul(d_up, w_up_t, "nn", F32, "d_ffn_up", res=_matmul(d_gate, w_gate_t, "nn", F32, "d_ffn_gate"))
    dw_gate = _matmul(d_gate, h2, "tn", BF16, "dw_gate", tm=ff // 2, tn=d, tk=1024)
    dw_up = _matmul(d_up, h2, "tn", BF16, "dw_up", tm=ff // 2, tn=d, tk=1024)
    dx1, dg_ffn, dx1_b = _rmsnorm_bwd(x1, 0, d, norm_ffn_g, dh2, "d_norm_ffn", res=dx2, mxu_copy=True)
    dcat = _matmul(dx1_b, w_o_f, "nt", F32, "d_proj_out")
    dw_o = _matmul(cat, dx1_b, "tn", BF16, "dw_o", tn=d, tk=1024)
    d_fox_o, d_mla_o, fox_delta, mla_delta, dg_fox, dg_mla = _out_norm_bwd(fox_o, mla_o, fox_out_g, mla_out_g, dcat, "d_norm_out")

    per_dev = lambda a: a.reshape(N_DEV, -1, d)
    late_grads = [per_dev(dw_o), per_dev(dw_gate), per_dev(dw_up), per_dev(dw_down)]
    dproj_a, dc, g_late = _attn_bwd((proj_a,), d_fox_o, fox_lse, per_head_rows(fox_delta),
                                    c_bias, HEAD_DIM ** -0.5, bl, s, tq, "d_fox_attention", traffic=_Traffic("swap", late_grads))
    dz, db_fgate = _fgate_bwd(z, bcol, dc.reshape(bh * grp, LANES), grp, "d_forget_gate")
    d_flogit = dz.reshape(bl, HEADS, s).transpose(0, 2, 1).reshape(t, HEADS)

    dq_nope, dkv_all, dq_pe, dk_pe = _attn_bwd(mla_ops, d_mla_o, mla_lse, per_head_rows(mla_delta),
                                               None, MLA_QK ** -0.5, bl, s, tq, "d_mla_attention")
    dq_rot, dk_rot = _rope_bwd(dq_pe, dk_pe, rope_cos, rope_sin, "d_rope")
    dqn = _matmul(dq_rot, w_uq_p[fw:], "nn", F32, "d_up_q_rope", res=_matmul(dq_nope, w_uq_p[:fw], "nn", F32, "d_up_q_nope"))
    dw_uq_nope = _matmul(dq_nope, qn, "tn", BF16, "dw_uq_nope", tn=q_rank, tk=1024)
    dw_uq_pe = _matmul(dq_rot, qn, "tn", BF16, "dw_uq_rope", tn=q_rank, tk=1024)
    dq_lat, dg_q = _rmsnorm_bwd(proj_b, 0, q_rank, q_norm_g, dqn, "d_norm_q")
    dkvn = _matmul(dkv_all, w_ukv_p, "nn", F32, "d_up_kv")
    dw_ukv_p = _matmul(dkv_all, kvn, "tn", BF16, "dw_ukv", tn=kv_rank, tk=1024)
    dkv_lat, dg_kv = _rmsnorm_bwd(proj_b, o_kvlat // kv_rank, kv_rank, kv_norm_g, dkvn, "d_norm_kv")

    dproj_b = jnp.concatenate([dq_lat.astype(BF16), dkv_lat.astype(BF16), dk_rot.astype(BF16), d_flogit.astype(BF16),
                               jnp.zeros((t, b_cols - o_flogit - HEADS), BF16)], axis=1)
    dh1 = _matmul(dproj_b, w_in_b, "nn", F32, "d_proj_mla", res=_matmul(dproj_a, w_in_a, "nn", F32, "d_proj_fox"))
    dw_in_a = _matmul(dproj_a, h1, "tn", BF16, "dw_in_fox", tn=d, tk=1024)
    dw_in_b = _matmul(dproj_b, h1, "tn", F32, "dw_in_mla", tn=d, tk=1024)
    grad_x, dg_mix = _rmsnorm_bwd(x2d, 0, d, norm_mix_g, dh1, "d_norm_mix", res=dx1)

    dw_krope = dw_in_b[o_krope:o_flogit].reshape(2, 4, half, d).sum(axis=1).reshape(MLA_ROPE, d)
    dw_in_t = jnp.concatenate([dw_in_a.reshape(PAIRS, 3, LANES, d).transpose(1, 0, 2, 3).reshape(n_qkv, d),
                               dw_in_b[o_flogit:o_flogit + HEADS].astype(BF16), dw_in_b[:o_krope].astype(BF16),
                               dw_krope.astype(BF16)], axis=0)
    pad_dev = lambda a, rows: jnp.pad(a, ((0, 0), (0, rows - a.shape[1]), (0, 0)))
    dw_uq_pe5 = dw_uq_pe.reshape(2, 2, 4, half, q_rank)
    dw_uq_h = jnp.concatenate([dw_uq_nope.reshape(HEADS, HEAD_DIM, q_rank), dw_uq_pe5[:, 0].reshape(HEADS, half, q_rank),
                               dw_uq_pe5[:, 1].reshape(HEADS, half, q_rank)], axis=1)
    dw_ukv_h = dw_ukv_p.reshape(PAIRS, 2, 2, HEAD_DIM, kv_rank).transpose(0, 2, 1, 3, 4).reshape(HEADS, 2 * HEAD_DIM, kv_rank)
    n_last = 3
    last_grads = [pad_dev(per_dev(dw_in_t), pieces[0].shape[0]), pad_dev(per_dev(dw_uq_h), pieces[1].shape[0]), per_dev(dw_ukv_h)]
    g_last = _sum_blocks(_exchange(last_grads, "exchange_grads"), "sum_grads")
    g_late = _sum_blocks(g_late, "sum_late_grads")

    def mine(i, rows):
        src, base = (g_last, 0) if i < n_last else (g_late, offs[n_last])
        return src[offs[i] - base:offs[i] - base + rows]

    big = [
        ("w_in", w_in, m_w_in, v_w_in, mine(0, in_cols).T),
        ("w_uq", w_uq, m_w_uq, v_w_uq, mine(1, uq_rows).reshape(-1, q_rank).T),
        ("w_ukv", w_ukv, m_w_ukv, v_w_ukv, mine(2, ukv_rows).reshape(-1, kv_rank).T),
        ("w_o", w_o, m_w_o, v_w_o, mine(3, w_o.shape[1])),
        ("w_gate", w_gate, m_w_gate, v_w_gate, mine(4, ff // N_DEV).T),
        ("w_up", w_up, m_w_up, v_w_up, mine(5, ff // N_DEV).T),
        ("w_down", w_down, m_w_down, v_w_down, mine(6, ff // N_DEV)),
    ]
    out = {}
    for nm, w, m, v, g in big:
        dl, new_m, new_v = _adamw(w[0], g, m[0], v[0], "adamw_" + nm)
        out[nm] = (g[None], dl[None], new_m[None], new_v[None])

    smalls = [("norm_mix_g", norm_mix_g, m_norm_mix_g, v_norm_mix_g, dg_mix),
              ("b_fgate", b_fgate, m_b_fgate, v_b_fgate, db_fgate.reshape(1, HEADS)),
              ("q_norm_g", q_norm_g, m_q_norm_g, v_q_norm_g, dg_q),
              ("kv_norm_g", kv_norm_g, m_kv_norm_g, v_kv_norm_g, dg_kv),
              ("fox_out_g", fox_out_g, m_fox_out_g, v_fox_out_g, dg_fox),
              ("mla_out_g", mla_out_g, m_mla_out_g, v_mla_out_g, dg_mla),
              ("norm_ffn_g", norm_ffn_g, m_norm_ffn_g, v_norm_ffn_g, dg_ffn),
              ("final_norm_g", final_norm_g, m_final_norm_g, v_final_norm_g, dg_final)]
    pack = lambda arrs: jnp.concatenate([_pad_lanes(a.reshape(1, -1)) for a in arrs], axis=1)
    blank = jnp.zeros((1, 1), F32)
    totals = _small_all_reduce_adamw(
        pack([e[4] for e in smalls] + [loss_part]), pack([e[1] for e in smalls] + [blank]),
        pack([e[2] for e in smalls] + [blank]), pack([e[3] for e in smalls] + [blank]), "reduce_small_adamw")
    pos = 0
    for nm, w, _, _, _ in smalls:
        out[nm] = tuple(a[0, pos:pos + w.size].reshape(w.shape) for a in totals)
        pos += -(-w.size // LANES) * LANES
    loss = totals[0][0, pos]

    order = ["norm_mix_g", "w_in", "b_fgate", "q_norm_g", "w_uq", "kv_norm_g", "w_ukv", "fox_out_g", "mla_out_g", "w_o",
             "norm_ffn_g", "w_gate", "w_up", "w_down", "final_norm_g"]
    return (loss, grad_x.reshape(bl, s, d), *[out[n][0] for n in order], *[out[n][1] for n in order],
            *[out[n][2] for n in order], *[out[n][3] for n in order])
```

```python
import jax
import jax.numpy as jnp
from jax import lax
from jax.experimental import pallas as pl
from jax.experimental.pallas import tpu as pltpu

F32 = jnp.float32
BF16 = jnp.bfloat16
MESH = pl.DeviceIdType.MESH

N_DEV = 8
HEADS = 8
HEAD_DIM = 64
PAIRS = HEADS // 2
MLA_ROPE = 32
MLA_QK = HEAD_DIM + MLA_ROPE
ROPE_THETA = 10000.0
NORM_EPS = 1e-6
ADAM_LR, ADAM_B1, ADAM_B2, ADAM_EPS, ADAM_WD, ADAM_STEP = 0.001, 0.9, 0.999, 1e-08, 0.01, 10

LANES = 128
MASKED = -1e30
VMEM_LIMIT = 48 * 1024 * 1024

_DIMS = {"nn": (((1,), (0,)), ((), ())), "nt": (((1,), (1,)), ((), ())), "tn": (((0,), (0,)), ((), ()))}


def _params(*sem):
    return pltpu.CompilerParams(dimension_semantics=sem, vmem_limit_bytes=VMEM_LIMIT)


def _dot(a, b, mode):
    return lax.dot_general(a.astype(BF16), b.astype(BF16), _DIMS[mode], preferred_element_type=F32)


def _tile(n, pref, unit=8):
    if n <= pref:
        return n
    t = pref - pref % unit
    while n % t:
        t -= unit
    return t


def _log2(n):
    assert n & (n - 1) == 0
    return n.bit_length() - 1


def _matmul(a, b, mode, out_dtype, name, tm=512, tn=512, tk=None, res=None):
    if mode == "nn":
        (m, kd), n = a.shape, b.shape[1]
    elif mode == "nt":
        (m, kd), n = a.shape, b.shape[0]
    else:
        (kd, m), n = a.shape, b.shape[1]
    tm, tn = _tile(m, tm, LANES if mode == "tn" else 16), _tile(n, tn, LANES)
    tk = kd if tk is None else _tile(kd, tk, LANES)
    nk = kd // tk
    a_spec = pl.BlockSpec((tk, tm), lambda i, j, k: (k, i)) if mode == "tn" else pl.BlockSpec((tm, tk), lambda i, j, k: (i, k))
    b_spec = pl.BlockSpec((tn, tk), lambda i, j, k: (j, k)) if mode == "nt" else pl.BlockSpec((tk, tn), lambda i, j, k: (k, j))
    o_spec = pl.BlockSpec((tm, tn), lambda i, j, k: (i, j))
    has_res = res is not None

    def body(*refs):
        a_ref, b_ref = refs[:2]
        r_ref = refs[2] if has_res else None
        o_ref = refs[3] if has_res else refs[2]

        def finish(acc):
            if has_res:
                acc = acc + r_ref[...]
            o_ref[...] = acc.astype(out_dtype)

        part = _dot(a_ref[...], b_ref[...], mode)
        if nk == 1:
            finish(part)
        else:
            acc_ref = refs[-1]
            k = pl.program_id(2)

            @pl.when(k == 0)
            def _():
                acc_ref[...] = part

            @pl.when(k > 0)
            def _():
                acc_ref[...] += part

            @pl.when(k == nk - 1)
            def _():
                finish(acc_ref[...])

    return pl.pallas_call(
        body, name=name, grid=(m // tm, n // tn, nk),
        in_specs=[a_spec, b_spec] + ([o_spec] if has_res else []), out_specs=o_spec,
        out_shape=jax.ShapeDtypeStruct((m, n), out_dtype),
        scratch_shapes=[pltpu.VMEM((tm, tn), F32)] if nk > 1 else [],
        compiler_params=_params("parallel", "parallel", "arbitrary"),
    )(*([a, b] + ([res] if has_res else [])))


def _rstd(x):
    return lax.rsqrt(jnp.mean(x * x, axis=-1, keepdims=True) + NORM_EPS)


def _norm_bwd(x, g, dy):
    r = _rstd(x)
    xh = x * r
    u = dy * g
    dx = r * (u - xh * jnp.mean(u * xh, axis=-1, keepdims=True))
    return dx, jnp.sum(dy * xh, axis=0, keepdims=True)


def _rmsnorm(x, col, width, g, out_dtype, name, traffic=None):
    t = x.shape[0]
    tm = _tile(t, 512)
    steps = t // tm
    n_carried = len(traffic.pieces) if traffic else 0

    def body(*refs):
        x_ref, g_ref, o_ref = refs[0], refs[1], refs[2 + n_carried]
        if traffic:
            carried_in, carried_out, sems = refs[2:2 + n_carried], refs[3 + n_carried], refs[4 + n_carried:]

            @pl.when(pl.program_id(0) == 0)
            def _():
                traffic.start(carried_in, carried_out, *sems)

        xv = x_ref[...]
        o_ref[...] = ((xv * _rstd(xv)) * g_ref[...]).astype(out_dtype)
        if traffic:
            @pl.when(pl.program_id(0) == steps - 1)
            def _():
                traffic.wait(carried_out, *sems)

    in_specs = [pl.BlockSpec((tm, width), lambda i: (i, col)), pl.BlockSpec((1, width), lambda i: (0, 0))]
    out_specs = [pl.BlockSpec((tm, width), lambda i: (i, 0))]
    out_shape = [jax.ShapeDtypeStruct((t, width), out_dtype)]
    if traffic:
        in_specs += traffic.in_specs
        out_specs.append(traffic.out_spec)
        out_shape.append(traffic.out_shape)
    out = pl.pallas_call(
        body, name=name, grid=(steps,), in_specs=in_specs, out_specs=tuple(out_specs), out_shape=tuple(out_shape),
        scratch_shapes=traffic.scratch if traffic else [],
        compiler_params=_params("arbitrary" if traffic else "parallel"),
    )(x, g, *(traffic.pieces if traffic else []))
    return out if traffic else out[0]


def _rmsnorm_bwd(x, col, width, g, dy, name, res=None, mxu_copy=False, traffic=None):
    t = x.shape[0]
    tm = _tile(t, 512)
    steps = t // tm
    has_res = res is not None
    n_carried = len(traffic.pieces) if traffic else 0
    row = pl.BlockSpec((tm, width), lambda i: (i, 0))
    vec = pl.BlockSpec((1, width), lambda i: (0, 0))

    def body(*refs):
        x_ref, g_ref, dy_ref = refs[:3]
        n_in = 3 + has_res + n_carried
        dx_ref, dg_ref = refs[n_in:n_in + 2]
        if traffic:
            carried_in, carried_out, sems = refs[3 + has_res:n_in], refs[n_in + 2 + mxu_copy], refs[len(refs) - 3:]

            @pl.when(pl.program_id(0) == 0)
            def _():
                traffic.start(carried_in, carried_out, *sems)

        dx, dg = _norm_bwd(x_ref[...], g_ref[...], dy_ref[...])
        if has_res:
            dx = dx + refs[3][...]
        dx_ref[...] = dx
        if mxu_copy:
            refs[n_in + 2][...] = dx.astype(BF16)

        @pl.when(pl.program_id(0) == 0)
        def _():
            dg_ref[...] = jnp.zeros_like(dg_ref)

        dg_ref[...] += dg
        if traffic:
            @pl.when(pl.program_id(0) == steps - 1)
            def _():
                traffic.wait(carried_out, *sems)

    in_specs = [pl.BlockSpec((tm, width), lambda i: (i, col)), vec, row] + ([row] if has_res else [])
    out_specs = (row, vec) + ((row,) if mxu_copy else ())
    out_shape = (jax.ShapeDtypeStruct((t, width), F32), jax.ShapeDtypeStruct((1, width), F32)) \
        + ((jax.ShapeDtypeStruct((t, width), BF16),) if mxu_copy else ())
    if traffic:
        in_specs += traffic.in_specs
        out_specs += (traffic.out_spec,)
        out_shape += (traffic.out_shape,)
    return pl.pallas_call(
        body, name=name, grid=(steps,), in_specs=in_specs, out_specs=out_specs, out_shape=out_shape,
        scratch_shapes=traffic.scratch if traffic else [], compiler_params=_params("arbitrary"),
    )(*([x, g, dy] + ([res] if has_res else []) + (traffic.pieces if traffic else [])))


def _out_norm(fo, mo, gf, gm, name):
    t, w = fo.shape
    tm = _tile(t, 512)
    row = pl.BlockSpec((tm, w), lambda i: (i, 0))
    vec = pl.BlockSpec((1, w), lambda i: (0, 0))

    def body(fo_ref, mo_ref, gf_ref, gm_ref, o_ref):
        f, m = fo_ref[...], mo_ref[...]
        o_ref[:, :w] = ((f * _rstd(f)) * gf_ref[...]).astype(BF16)
        o_ref[:, w:] = ((m * _rstd(m)) * gm_ref[...]).astype(BF16)

    return pl.pallas_call(
        body, name=name, grid=(t // tm,), in_specs=[row, row, vec, vec],
        out_specs=pl.BlockSpec((tm, 2 * w), lambda i: (i, 0)),
        out_shape=jax.ShapeDtypeStruct((t, 2 * w), BF16),
        compiler_params=_params("parallel"),
    )(fo, mo, gf, gm)


def _split3(x):
    hi = x.astype(BF16)
    r1 = x - hi.astype(F32)
    mid = r1.astype(BF16)
    lo = (r1 - mid.astype(F32)).astype(BF16)
    return hi, mid, lo


def _dot_x01(x, m01):
    hi, mid, lo = _split3(x)
    d = lambda p: lax.dot_general(p, m01, _DIMS["nn"], preferred_element_type=F32)
    return (d(lo) + d(mid)) + d(hi)


def _dot_01x(m01, x):
    hi, mid, lo = _split3(x)
    d = lambda p: lax.dot_general(m01, p, _DIMS["nn"], preferred_element_type=F32)
    return (d(lo) + d(mid)) + d(hi)


def _out_norm_bwd(fo, mo, gf, gm, dcat, name):
    t, w = fo.shape
    nh = w // HEAD_DIM
    tm = _tile(t, 512)
    row = pl.BlockSpec((tm, w), lambda i: (i, 0))
    vec = pl.BlockSpec((1, w), lambda i: (0, 0))
    hrow = pl.BlockSpec((tm, nh), lambda i: (i, 0))

    def body(fo_ref, mo_ref, gf_ref, gm_ref, dc_ref, dfo_ref, dmo_ref, ff_ref, fm_ref, dgf_ref, dgm_ref):
        lane_head = lax.shift_right_logical(lax.broadcasted_iota(jnp.int32, (w, nh), 0), _log2(HEAD_DIM))
        sel = (lane_head == lax.broadcasted_iota(jnp.int32, (w, nh), 1)).astype(BF16)
        f, m = fo_ref[...], mo_ref[...]
        dfo, dgf = _norm_bwd(f, gf_ref[...], dc_ref[:, :w])
        dmo, dgm = _norm_bwd(m, gm_ref[...], dc_ref[:, w:])
        dfo_ref[...] = dfo.astype(BF16)
        dmo_ref[...] = dmo.astype(BF16)
        ff_ref[...] = _dot_x01(dfo * f, sel)
        fm_ref[...] = _dot_x01(dmo * m, sel)

        @pl.when(pl.program_id(0) == 0)
        def _():
            dgf_ref[...] = jnp.zeros_like(dgf_ref)
            dgm_ref[...] = jnp.zeros_like(dgm_ref)

        dgf_ref[...] += dgf
        dgm_ref[...] += dgm

    return pl.pallas_call(
        body, name=name, grid=(t // tm,),
        in_specs=[row, row, vec, vec, pl.BlockSpec((tm, 2 * w), lambda i: (i, 0))],
        out_specs=(row, row, hrow, hrow, vec, vec),
        out_shape=(jax.ShapeDtypeStruct((t, w), BF16), jax.ShapeDtypeStruct((t, w), BF16),
                   jax.ShapeDtypeStruct((t, nh), F32), jax.ShapeDtypeStruct((t, nh), F32),
                   jax.ShapeDtypeStruct((1, w), F32), jax.ShapeDtypeStruct((1, w), F32)),
        compiler_params=_params("arbitrary"),
    )(fo, mo, gf, gm, dcat)


def _loss_bwd(x, tgt, g, name):
    t, d = x.shape
    tm = _tile(t, 512)
    row = pl.BlockSpec((tm, d), lambda i: (i, 0))
    vec = pl.BlockSpec((1, d), lambda i: (0, 0))
    one = pl.BlockSpec((1, 1), lambda i: (0, 0))

    def body(x_ref, t_ref, g_ref, dx_ref, dg_ref, loss_ref, dxb_ref):
        xv, gv = x_ref[...], g_ref[...]
        diff = (xv * _rstd(xv)) * gv - t_ref[...]
        dx, dg = _norm_bwd(xv, gv, diff / d)
        dx_ref[...] = dx
        dxb_ref[...] = dx.astype(BF16)

        @pl.when(pl.program_id(0) == 0)
        def _():
            dg_ref[...] = jnp.zeros_like(dg_ref)
            loss_ref[...] = jnp.zeros_like(loss_ref)

        dg_ref[...] += dg
        loss_ref[...] += 0.5 * jnp.sum(jnp.mean(diff * diff, axis=-1, keepdims=True), axis=0, keepdims=True)

    return pl.pallas_call(
        body, name=name, grid=(t // tm,), in_specs=[row, row, vec], out_specs=(row, vec, one, row),
        out_shape=(jax.ShapeDtypeStruct((t, d), F32), jax.ShapeDtypeStruct((1, d), F32), jax.ShapeDtypeStruct((1, 1), F32),
                   jax.ShapeDtypeStruct((t, d), BF16)),
        compiler_params=_params("arbitrary"),
    )(x, tgt, g)


def _ffn_up(h, wg_t, wu_t, name, tm=512, tf=1408):
    t, d = h.shape
    f = wg_t.shape[0]
    tm, tf = _tile(t, tm, 16), _tile(f, tf, LANES)
    tok = pl.BlockSpec((tm, tf), lambda i, j: (i, j))
    wt = pl.BlockSpec((tf, d), lambda i, j: (j, 0))

    def body(h_ref, wg_ref, wu_ref, g_ref, u_ref, a_ref):
        hv = h_ref[...]
        g, u = _dot(hv, wg_ref[...], "nt"), _dot(hv, wu_ref[...], "nt")
        g_ref[...], u_ref[...] = g, u
        a_ref[...] = ((g * jax.nn.sigmoid(g)) * u).astype(BF16)

    return pl.pallas_call(
        body, name=name, grid=(t // tm, f // tf), in_specs=[pl.BlockSpec((tm, d), lambda i, j: (i, 0)), wt, wt],
        out_specs=(tok, tok, tok),
        out_shape=(jax.ShapeDtypeStruct((t, f), F32), jax.ShapeDtypeStruct((t, f), F32), jax.ShapeDtypeStruct((t, f), BF16)),
        compiler_params=_params("parallel", "parallel"),
    )(h, wg_t, wu_t)


def _ffn_down_bwd(dy, w_down, gate, up, name, tm=512, tf=1408):
    t, d = dy.shape
    f = w_down.shape[0]
    tm, tf = _tile(t, tm, 16), _tile(f, tf, LANES)
    tok = pl.BlockSpec((tm, tf), lambda i, j: (i, j))

    def body(dy_ref, w_ref, g_ref, u_ref, dg_ref, du_ref):
        da = _dot(dy_ref[...], w_ref[...], "nt")
        g = g_ref[...]
        sg = jax.nn.sigmoid(g)
        dg_ref[...] = (da * u_ref[...] * (sg * (1.0 + g * (1.0 - sg)))).astype(BF16)
        du_ref[...] = (da * (g * sg)).astype(BF16)

    return pl.pallas_call(
        body, name=name, grid=(t // tm, f // tf),
        in_specs=[pl.BlockSpec((tm, d), lambda i, j: (i, 0)), pl.BlockSpec((tf, d), lambda i, j: (j, 0)), tok, tok],
        out_specs=(tok, tok),
        out_shape=(jax.ShapeDtypeStruct((t, f), BF16), jax.ShapeDtypeStruct((t, f), BF16)),
        compiler_params=_params("parallel", "parallel"),
    )(dy, w_down, gate, up)


def _chunk_scan_mats(rows, grp, reverse):
    ii = lax.broadcasted_iota(jnp.int32, (LANES, LANES), 0)
    jj = lax.broadcasted_iota(jnp.int32, (LANES, LANES), 1)
    within = ((ii >= jj) if reverse else (ii <= jj)).astype(BF16)
    ones = jnp.ones((LANES, LANES), BF16)
    ri = lax.broadcasted_iota(jnp.int32, (rows, rows), 0)
    rj = lax.broadcasted_iota(jnp.int32, (rows, rows), 1)
    sh = _log2(grp)
    same = lax.shift_right_logical(ri, sh) == lax.shift_right_logical(rj, sh)
    across = (same & ((rj > ri) if reverse else (rj < ri))).astype(BF16)
    return within, ones, across


def _running_sum(v, mats):
    within, ones, across = mats
    return _dot_x01(v, within) + _dot_01x(across, _dot_x01(v, ones))


def _fgate(z, bcol, grp, name):
    rows = z.shape[0]

    def body(z_ref, b_ref, c_ref):
        zz = z_ref[...] + b_ref[...]
        log_f = jnp.minimum(zz, 0.0) - jnp.log1p(jnp.exp(-jnp.abs(zz)))
        c_ref[...] = _running_sum(log_f, _chunk_scan_mats(rows, grp, False))

    return pl.pallas_call(body, name=name, out_shape=jax.ShapeDtypeStruct(z.shape, F32),
                          compiler_params=pltpu.CompilerParams(vmem_limit_bytes=VMEM_LIMIT))(z, bcol)


def _fgate_bwd(z, bcol, dc, grp, name):
    rows = z.shape[0]

    def body(z_ref, b_ref, dc_ref, dz_ref, db_ref):
        zz = z_ref[...] + b_ref[...]
        dz = _running_sum(dc_ref[...], _chunk_scan_mats(rows, grp, True)) * jax.nn.sigmoid(-zz)
        dz_ref[...] = dz
        head = lax.shift_right_logical(lax.broadcasted_iota(jnp.int32, (HEADS, rows), 1), _log2(grp)) & (HEADS - 1)
        sel = (head == lax.broadcasted_iota(jnp.int32, (HEADS, rows), 0)).astype(BF16)
        db_ref[...] = jnp.sum(_dot_01x(sel, dz), axis=1, keepdims=True)

    return pl.pallas_call(
        body, name=name,
        out_shape=(jax.ShapeDtypeStruct(z.shape, F32), jax.ShapeDtypeStruct((HEADS, 1), F32)),
        compiler_params=pltpu.CompilerParams(vmem_limit_bytes=VMEM_LIMIT),
    )(z, bcol, dc)


def _rotate(x, cs, sn_signed):
    return x * cs + pltpu.roll(x, LANES // 2, axis=1) * sn_signed


def _rope_fwd(q_raw, nope, proj_b, k_col, cs, sn, name):
    t, qw = q_raw.shape
    tm = _tile(t, 512)
    row = pl.BlockSpec((tm, LANES), lambda i: (i, 0))

    def body(q_ref, k_ref, c_ref, s_ref, qo_ref, ko_ref):
        c, s = c_ref[...], s_ref[...]
        qo_ref[:, :nope] = q_ref[:, :nope].astype(BF16)
        for off in range(nope, qw, LANES):
            qo_ref[:, off:off + LANES] = _rotate(q_ref[:, off:off + LANES], c, s).astype(BF16)
        ko_ref[...] = _rotate(k_ref[...], c, s).astype(BF16)

    return pl.pallas_call(
        body, name=name, grid=(t // tm,),
        in_specs=[pl.BlockSpec((tm, qw), lambda i: (i, 0)), pl.BlockSpec((tm, LANES), lambda i: (i, k_col)), row, row],
        out_specs=(pl.BlockSpec((tm, qw), lambda i: (i, 0)), row),
        out_shape=(jax.ShapeDtypeStruct((t, qw), BF16), jax.ShapeDtypeStruct((t, LANES), BF16)),
        compiler_params=_params("parallel"),
    )(q_raw, proj_b, cs, sn)


def _rope_bwd(dq_pe, dk_pe, cs, sn, name):
    t, qw = dq_pe.shape
    tm = _tile(t, 512)
    row = pl.BlockSpec((tm, LANES), lambda i: (i, 0))
    wide = pl.BlockSpec((tm, qw), lambda i: (i, 0))

    def body(q_ref, k_ref, c_ref, s_ref, qo_ref, ko_ref):
        c, s = c_ref[...], -s_ref[...]
        for off in range(0, qw, LANES):
            qo_ref[:, off:off + LANES] = _rotate(q_ref[:, off:off + LANES], c, s).astype(BF16)
        ko_ref[...] = _rotate(k_ref[...], c, s)

    return pl.pallas_call(
        body, name=name, grid=(t // tm,), in_specs=[wide, row, row, row], out_specs=(wide, row),
        out_shape=(jax.ShapeDtypeStruct((t, qw), BF16), jax.ShapeDtypeStruct((t, LANES), F32)),
        compiler_params=_params("parallel"),
    )(dq_pe, dk_pe, cs, sn)


def _lane_masks(pair, h, pe):
    lane = lax.broadcasted_iota(jnp.int32, (1, LANES), 1)
    in_head = lax.shift_right_logical(lane, _log2(HEAD_DIM)) == h
    in_rope = ((lax.shift_right_logical(lane, _log2(MLA_ROPE // 2)) & 3) == ((2 * pair + h) & 3)) if pe else None
    return in_head, in_rope


def _keep(mask, v):
    return jnp.where(mask, v, jnp.zeros_like(v))


def _to_row(col):
    n = col.shape[0]
    eye = lax.broadcasted_iota(jnp.int32, (n, n), 0) == lax.broadcasted_iota(jnp.int32, (n, n), 1)
    return jnp.sum(jnp.where(eye, col, 0.0), axis=0, keepdims=True)


def _to_col(row):
    n = row.shape[1]
    eye = lax.broadcasted_iota(jnp.int32, (n, n), 0) == lax.broadcasted_iota(jnp.int32, (n, n), 1)
    return jnp.sum(jnp.where(eye, row, 0.0), axis=1, keepdims=True)


def _first_step():
    return (pl.program_id(0) == 0) & (pl.program_id(1) == 0)


def _last_step(n0, n1):
    return (pl.program_id(0) == n0 - 1) & (pl.program_id(1) == n1 - 1)


def _attn_fwd(ops, bias, scale, bl, s, tq, name, traffic=None):
    pe = len(ops) == 3
    has_bias = bias is not None
    nq = s // tq
    t = bl * s
    n_carried = len(traffic.pieces) if traffic else 0

    def body(*refs):
        sems = refs[len(refs) - 3:] if traffic else ()
        if pe:
            q_ref, qpe_ref, kv_ref, kpe_ref = refs[:4]
            n_in = 4
            q_at = lambda r0, r1: q_ref[r0:r1, :]
            v_at = lambda r0, r1: kv_ref[r0:r1, LANES:]
            kcat = refs[len(refs) - 1 - len(sems)]
            kcat[:, :LANES] = kv_ref[:, :LANES]
            kcat[:, LANES:] = kpe_ref[...]
            k_at = lambda r0, r1: kcat[r0:r1, :]
        else:
            qkv_ref = refs[0]
            n_in = 1
            q_at = lambda r0, r1: qkv_ref[r0:r1, :LANES]
            k_at = lambda r0, r1: qkv_ref[r0:r1, LANES:2 * LANES]
            v_at = lambda r0, r1: qkv_ref[r0:r1, 2 * LANES:]
        if has_bias:
            c_ref = refs[n_in]
            n_in += 1
        carried_in = refs[n_in:n_in + n_carried]
        n_in += n_carried
        o_ref, lse_ref = refs[n_in:n_in + 2]
        if traffic:
            carried_out = refs[n_in + 2]

            @pl.when(_first_step())
            def _():
                traffic.start(carried_in, carried_out, *sems)

        pair = pl.program_id(1)
        causal = lax.broadcasted_iota(jnp.int32, (tq, tq), 1) <= lax.broadcasted_iota(jnp.int32, (tq, tq), 0)
        o_ref[...] = jnp.zeros_like(o_ref)

        def head(h, carry):
            in_head, in_rope = _lane_masks(pair, h, pe)
            for i in range(nq):
                r0, r1 = i * tq, (i + 1) * tq
                qm = _keep(in_head, q_at(r0, r1))
                if pe:
                    qm = jnp.concatenate([qm, _keep(in_rope, qpe_ref[r0:r1, :])], axis=1)
                cq = _to_col(c_ref[h, :, r0:r1]) if has_bias else None

                def logits(k0, k1):
                    sc = _dot(qm, k_at(k0, k1), "nt") * scale
                    if has_bias:
                        sc = sc + (cq - c_ref[h, :, k0:k1])
                    return sc

                sd = jnp.where(causal, logits(r0, r1), MASKED)
                m = jnp.max(sd, axis=1, keepdims=True)
                if i:
                    so = logits(0, r0)
                    m = jnp.maximum(m, jnp.max(so, axis=1, keepdims=True))
                pd = jnp.exp(sd - m)
                l = jnp.sum(pd, axis=1, keepdims=True)
                acc = _dot(pd, v_at(r0, r1), "nn")
                if i:
                    po = jnp.exp(so - m)
                    l = l + jnp.sum(po, axis=1, keepdims=True)
                    acc = acc + _dot(po, v_at(0, r0), "nn")
                o_ref[r0:r1, :] = jnp.where(in_head, acc / l, o_ref[r0:r1, :])
                lse_ref[h, :, r0:r1] = _to_row(m + jnp.log(l))
            return carry

        lax.fori_loop(0, 2, head, 0)
        if traffic:
            @pl.when(_last_step(bl, PAIRS))
            def _():
                traffic.wait(carried_out, *sems)

    seq = lambda w, col: pl.BlockSpec((s, w), col)
    if pe:
        in_specs = [seq(LANES, lambda b, p: (b, p)), seq(LANES, lambda b, p: (b, PAIRS + p // 2)),
                    seq(2 * LANES, lambda b, p: (b, p)), seq(LANES, lambda b, p: (b, 0))]
        args = [ops[0], ops[0], ops[1], ops[2]]
        scratch = [pltpu.VMEM((s, 2 * LANES), BF16)]
    else:
        in_specs = [seq(3 * LANES, lambda b, p: (b, p))]
        args = [ops[0]]
        scratch = []
    per_head_row = pl.BlockSpec((2, 1, s), lambda b, p: (b * PAIRS + p, 0, 0))
    if has_bias:
        in_specs.append(per_head_row)
        args.append(bias)
    out_specs = [seq(LANES, lambda b, p: (b, p)), per_head_row]
    out_shape = [jax.ShapeDtypeStruct((t, HEADS * HEAD_DIM), F32), jax.ShapeDtypeStruct((bl * HEADS, 1, s), F32)]
    if traffic:
        in_specs += traffic.in_specs
        args += traffic.pieces
        out_specs.append(traffic.out_spec)
        out_shape.append(traffic.out_shape)
        scratch += traffic.scratch
    return pl.pallas_call(
        body, name=name, grid=(bl, PAIRS), in_specs=in_specs, out_specs=tuple(out_specs), out_shape=tuple(out_shape),
        scratch_shapes=scratch, compiler_params=_params(*(("arbitrary", "arbitrary") if traffic else ("parallel", "parallel"))),
    )(*args)


def _attn_bwd(ops, do, lse, delta, bias, scale, bl, s, tq, name, traffic=None):
    pe = len(ops) == 3
    has_bias = bias is not None
    nq = s // tq
    t = bl * s
    width = 2 * LANES if pe else LANES
    n_carried = len(traffic.pieces) if traffic else 0

    def body(*refs):
        if pe:
            q_ref, qpe_ref, kv_ref, kpe_ref = refs[:4]
            n_in = 4
            k_at = lambda r0, r1: kv_ref[r0:r1, :LANES]
            v_at = lambda r0, r1: kv_ref[r0:r1, LANES:]
        else:
            qkv_ref = refs[0]
            n_in = 1
            k_at = lambda r0, r1: qkv_ref[r0:r1, LANES:2 * LANES]
            v_at = lambda r0, r1: qkv_ref[r0:r1, 2 * LANES:]
        do_ref, lse_ref, dl_ref = refs[n_in:n_in + 3]
        n_in += 3
        if has_bias:
            c_ref = refs[n_in]
            n_in += 1
        carried_in = refs[n_in:n_in + n_carried]
        rest = refs[n_in + n_carried:]
        if traffic:
            rest, sems = rest[:-3], rest[-3:]
            carried_out = rest[4 if pe else 2]
            rest = rest[:4 if pe else 2] + rest[(4 if pe else 2) + 1:]

            @pl.when(_first_step())
            def _():
                traffic.start(carried_in, carried_out, *sems)

        if pe:
            dqn_ref, dkv_ref, dqpe_ref, dkpe_ref, dq_acc, qcat = rest
            qcat[:, :LANES] = q_ref[...]
            qcat[:, LANES:] = qpe_ref[...]
            q_at = lambda r0, r1: qcat[r0:r1, :]
            dkv_ref[...] = jnp.zeros_like(dkv_ref)
        else:
            dqkv_ref, dc_ref, dq_acc = rest
            q_at = lambda r0, r1: qkv_ref[r0:r1, :LANES]
            dqkv_ref[...] = jnp.zeros_like(dqkv_ref)
            dc_ref[...] = jnp.zeros_like(dc_ref)
        pair = pl.program_id(1)
        dq_acc[...] = jnp.zeros_like(dq_acc)
        causal = lax.broadcasted_iota(jnp.int32, (tq, tq), 1) >= lax.broadcasted_iota(jnp.int32, (tq, tq), 0)

        def head(h, carry):
            in_head, in_rope = _lane_masks(pair, h, pe)
            for j in range(nq):
                r0, r1 = j * tq, (j + 1) * tq
                kt = _keep(in_head, k_at(r0, r1))
                if pe:
                    kt = jnp.concatenate([kt, _keep(in_rope, kpe_ref[r0:r1, :])], axis=1)
                vt = _keep(in_head, v_at(r0, r1))
                ck = _to_col(c_ref[h, :, r0:r1]) if has_bias else None

                def block(q0, q1, diagonal):
                    qq, dd = q_at(q0, q1), do_ref[q0:q1, :]
                    st = _dot(kt, qq, "nt") * scale
                    if has_bias:
                        st = st + (c_ref[h, :, q0:q1] - ck)
                    if diagonal:
                        st = jnp.where(causal, st, MASKED)
                    pt = jnp.exp(st - lse_ref[h, :, q0:q1])
                    dst = pt * (_dot(vt, dd, "nt") - dl_ref[h, :, q0:q1])
                    dsb = (dst * scale).astype(BF16)
                    dq_acc[q0:q1, :] += _dot(dsb, kt, "tn")
                    if has_bias:
                        dc_ref[h, :, q0:q1] += jnp.sum(dst, axis=0, keepdims=True)
                    return _dot(pt, dd, "nn"), _dot(dsb, qq, "nn"), (jnp.sum(dst, axis=1, keepdims=True) if has_bias else None)

                dv_c, dk_c, cs = block(r0, r1, True)
                if r1 < s:
                    dv_o, dk_o, cs_o = block(r1, s, False)
                    dv_c, dk_c = dv_c + dv_o, dk_c + dk_o
                    cs = cs + cs_o if has_bias else None
                if pe:
                    dkv_ref[r0:r1, :LANES] = jnp.where(in_head, dk_c[:, :LANES].astype(BF16), dkv_ref[r0:r1, :LANES])
                    dkv_ref[r0:r1, LANES:] = jnp.where(in_head, dv_c.astype(BF16), dkv_ref[r0:r1, LANES:])
                    dkpe_ref[r0:r1, :] += _keep(in_rope, dk_c[:, LANES:])
                else:
                    dqkv_ref[r0:r1, LANES:2 * LANES] = jnp.where(in_head, dk_c.astype(BF16), dqkv_ref[r0:r1, LANES:2 * LANES])
                    dqkv_ref[r0:r1, 2 * LANES:] = jnp.where(in_head, dv_c.astype(BF16), dqkv_ref[r0:r1, 2 * LANES:])
                    dc_ref[h, :, r0:r1] -= _to_row(cs)
            return carry

        if pe:
            @pl.when(pair == 0)
            def _():
                dkpe_ref[...] = jnp.zeros_like(dkpe_ref)

            @pl.when(pair % 2 == 0)
            def _():
                dqpe_ref[...] = jnp.zeros_like(dqpe_ref)

        lax.fori_loop(0, 2, head, 0)
        if pe:
            dqn_ref[...] = dq_acc[:, :LANES].astype(BF16)
            dqpe_ref[...] += dq_acc[:, LANES:]
        else:
            dqkv_ref[:, :LANES] = dq_acc[...].astype(BF16)
        if traffic:
            @pl.when(_last_step(bl, PAIRS))
            def _():
                traffic.wait(carried_out, *sems)

    seq = lambda w, col: pl.BlockSpec((s, w), col)
    per_head_row = pl.BlockSpec((2, 1, s), lambda b, p: (b * PAIRS + p, 0, 0))
    if pe:
        in_specs = [seq(LANES, lambda b, p: (b, p)), seq(LANES, lambda b, p: (b, PAIRS + p // 2)),
                    seq(2 * LANES, lambda b, p: (b, p)), seq(LANES, lambda b, p: (b, 0))]
        args = [ops[0], ops[0], ops[1], ops[2]]
    else:
        in_specs = [seq(3 * LANES, lambda b, p: (b, p))]
        args = [ops[0]]
    in_specs += [seq(LANES, lambda b, p: (b, p)), per_head_row, per_head_row]
    args += [do, lse, delta]
    if has_bias:
        in_specs.append(per_head_row)
        args.append(bias)
    scratch = [pltpu.VMEM((s, width), F32)]
    if pe:
        out_specs = (seq(LANES, lambda b, p: (b, p)), seq(2 * LANES, lambda b, p: (b, p)),
                     seq(LANES, lambda b, p: (b, p // 2)), seq(LANES, lambda b, p: (b, 0)))
        out_shape = (jax.ShapeDtypeStruct((t, PAIRS * LANES), BF16), jax.ShapeDtypeStruct((t, PAIRS * 2 * LANES), BF16),
                     jax.ShapeDtypeStruct((t, 2 * LANES), F32), jax.ShapeDtypeStruct((t, LANES), F32))
        scratch.append(pltpu.VMEM((s, 2 * LANES), BF16))
    else:
        out_specs = (seq(3 * LANES, lambda b, p: (b, p)), per_head_row)
        out_shape = (jax.ShapeDtypeStruct((t, PAIRS * 3 * LANES), BF16), jax.ShapeDtypeStruct((bl * HEADS, 1, s), F32))
    if traffic:
        in_specs += traffic.in_specs
        args += traffic.pieces
        out_specs += (traffic.out_spec,)
        out_shape += (traffic.out_shape,)
        scratch += traffic.scratch
    return pl.pallas_call(
        body, name=name, grid=(bl, PAIRS), in_specs=in_specs, out_specs=out_specs, out_shape=out_shape,
        scratch_shapes=scratch, compiler_params=_params("arbitrary" if traffic else "parallel", "arbitrary"),
    )(*args)


def _my_place():
    return lax.axis_index("x"), lax.axis_index("y"), lax.axis_index("c")


def _flip(p, bit):
    return 1 - p if bit else p


def _relative(x, y, c, k):
    return _flip(x, k & 4), _flip(y, k & 2), _flip(c, k & 1)


def _linear(x, y, c):
    return 4 * x + 2 * y + c


class _Traffic:
    def __init__(self, kind, pieces):
        self.kind, self.pieces = kind, list(pieces)
        self.rows = [p.shape[-2] for p in self.pieces]
        self.starts = [sum(self.rows[:i]) for i in range(len(self.rows))]
        anywhere = pl.BlockSpec(memory_space=pl.ANY)
        self.in_specs = [anywhere] * len(self.pieces)
        self.out_spec = anywhere
        self.out_shape = jax.ShapeDtypeStruct((N_DEV, sum(self.rows), self.pieces[0].shape[-1]), self.pieces[0].dtype)
        self.scratch = [pltpu.SemaphoreType.DMA((7,)), pltpu.SemaphoreType.DMA((7,)), pltpu.SemaphoreType.DMA(())]

    def start(self, p_refs, out_ref, send_sems, recv_sems, local_sem):
        x, y, c = _my_place()
        me = _linear(x, y, c)
        mine = lambda i, dev: p_refs[i] if self.kind == "spread" else p_refs[i].at[dev]
        landing = lambda i: out_ref.at[me, pl.ds(self.starts[i], self.rows[i])]
        for i in range(len(p_refs)):
            pltpu.make_async_copy(mine(i, me), landing(i), local_sem).start()
        for k in range(1, N_DEV):
            peer = _relative(x, y, c, k)
            for i in range(len(p_refs)):
                pltpu.make_async_remote_copy(
                    src_ref=mine(i, _linear(*peer)), dst_ref=landing(i),
                    send_sem=send_sems.at[k - 1], recv_sem=recv_sems.at[k - 1], device_id=peer, device_id_type=MESH).start()

    def wait(self, out_ref, send_sems, recv_sems, local_sem):
        x, y, c = _my_place()
        whole = out_ref.at[_linear(x, y, c)]
        for k in range(1, N_DEV):
            both = pltpu.make_async_remote_copy(
                src_ref=whole, dst_ref=whole, send_sem=send_sems.at[k - 1], recv_sem=recv_sems.at[k - 1],
                device_id=_relative(x, y, c, k), device_id_type=MESH)
            both.wait_recv()
            both.wait_send()
        pltpu.make_async_copy(whole, whole, local_sem).wait()


def _sum_blocks(parts, name):
    n, r, cdim = parts.shape
    tr = _tile(r, 640, 16)

    def body(p_ref, o_ref):
        acc = p_ref[0].astype(F32)
        for d in range(1, n):
            acc = acc + p_ref[d].astype(F32)
        o_ref[...] = acc

    return pl.pallas_call(
        body, name=name, grid=(r // tr,), in_specs=[pl.BlockSpec((n, tr, cdim), lambda i: (0, i, 0))],
        out_specs=pl.BlockSpec((tr, cdim), lambda i: (i, 0)), out_shape=jax.ShapeDtypeStruct((r, cdim), F32),
        compiler_params=_params("parallel"),
    )(parts)


def _adamw_math(w, g, m, v):
    m = ADAM_B1 * m + (1.0 - ADAM_B1) * g
    v = ADAM_B2 * v + (1.0 - ADAM_B2) * (g * g)
    m_hat = m / (1.0 - ADAM_B1 ** ADAM_STEP)
    v_hat = v / (1.0 - ADAM_B2 ** ADAM_STEP)
    delta = -ADAM_LR * (m_hat / (jnp.sqrt(v_hat) + ADAM_EPS) + ADAM_WD * w)
    return delta, m, v


def _adamw(w, g, m, v, name):
    def body(w_ref, g_ref, m_ref, v_ref, d_ref, nm_ref, nv_ref):
        d_ref[...], nm_ref[...], nv_ref[...] = _adamw_math(w_ref[...], g_ref[...], m_ref[...], v_ref[...])

    out = jax.ShapeDtypeStruct(w.shape, F32)
    return pl.pallas_call(body, name=name, out_shape=(out, out, out),
                          compiler_params=pltpu.CompilerParams(vmem_limit_bytes=VMEM_LIMIT))(w, g, m, v)


def _small_all_reduce_adamw(parts, loss_part, ws, ms, vs, name):
    sizes = [p.shape[1] for p in parts] + [1]
    spots = [sum(-(-n // LANES) * LANES for n in sizes[:i]) for i in range(len(sizes))]
    width = spots[-1] + LANES
    k = len(parts)

    def body(*refs):
        p_refs, w_refs, m_refs, v_refs = refs[:k + 1], refs[k + 1:2 * k + 1], refs[2 * k + 1:3 * k + 1], refs[3 * k + 1:4 * k + 1]
        outs, (rows, send_sems, recv_sems) = refs[4 * k + 1:8 * k + 2], refs[8 * k + 2:]
        x, y, c = _my_place()
        me = _linear(x, y, c)
        rows[me] = jnp.zeros((1, width), F32)
        for i in range(k + 1):
            rows[me, :, spots[i]:spots[i] + sizes[i]] = p_refs[i][...]
        copies = []
        for rel in range(1, N_DEV):
            copies.append(pltpu.make_async_remote_copy(
                src_ref=rows.at[me], dst_ref=rows.at[me], send_sem=send_sems.at[rel - 1], recv_sem=recv_sems.at[rel - 1],
                device_id=_relative(x, y, c, rel), device_id_type=MESH))
        for cp in copies:
            cp.start()
        for cp in copies:
            cp.wait_recv()
        for cp in copies:
            cp.wait_send()
        total = rows[0]
        for d in range(1, N_DEV):
            total = total + rows[d]
        for i in range(k):
            g = total[:, spots[i]:spots[i] + sizes[i]]
            outs[4 * i][...] = g
            outs[4 * i + 1][...], outs[4 * i + 2][...], outs[4 * i + 3][...] = _adamw_math(w_refs[i][...], g, m_refs[i][...], v_refs[i][...])
        outs[4 * k][...] = total[:, spots[k]:spots[k] + 1]

    out_shape = [jax.ShapeDtypeStruct((1, n), F32) for n in sizes[:k] for _ in range(4)] + [jax.ShapeDtypeStruct((1, 1), F32)]
    res = pl.pallas_call(
        body, name=name, out_shape=tuple(out_shape),
        scratch_shapes=[pltpu.VMEM((N_DEV, 1, width), F32), pltpu.SemaphoreType.DMA((7,)), pltpu.SemaphoreType.DMA((7,))],
    )(*parts, loss_part, *ws, *ms, *vs)
    return [res[4 * i:4 * i + 4] for i in range(k)], res[4 * k]


def _pad_rows(a, rows):
    return jnp.pad(a, ((0, rows - a.shape[0]), (0, 0)))


def kernel(x, positions, norm_mix_g, w_in, b_fgate, q_norm_g, w_uq, kv_norm_g, w_ukv, fox_out_g, mla_out_g, w_o, norm_ffn_g, w_gate, w_up, w_down, final_norm_g, loss_target, m_norm_mix_g, m_w_in, m_b_fgate, m_q_norm_g, m_w_uq, m_kv_norm_g, m_w_ukv, m_fox_out_g, m_mla_out_g, m_w_o, m_norm_ffn_g, m_w_gate, m_w_up, m_w_down, m_final_norm_g, v_norm_mix_g, v_w_in, v_b_fgate, v_q_norm_g, v_w_uq, v_kv_norm_g, v_w_ukv, v_fox_out_g, v_mla_out_g, v_w_o, v_norm_ffn_g, v_w_gate, v_w_up, v_w_down, v_final_norm_g):
    bl, s, d = x.shape
    t = bl * s
    bh = bl * HEADS
    tq = _tile(s, 256)
    grp = s // LANES
    fw = HEADS * HEAD_DIM
    q_rank, kv_rank = w_uq.shape[1], w_ukv.shape[1]
    in_cols = w_in.shape[2]
    n_in = N_DEV * in_cols
    ff = N_DEV * w_gate.shape[2]
    half = MLA_ROPE // 2
    o_kvlat, o_krope, o_flogit = q_rank, q_rank + kv_rank, q_rank + kv_rank + LANES
    b_cols = -(-(o_flogit + HEADS) // LANES) * LANES

    tr = lambda w: jnp.transpose(w[0])
    in_rows = -(-in_cols // 16) * 16
    uq_rows = w_uq.shape[2] * q_rank // d
    ukv_rows = w_ukv.shape[2] * kv_rank // d
    pieces = [_pad_rows(tr(w_in), in_rows), _pad_rows(tr(w_uq).reshape(uq_rows, d), -(-uq_rows // 16) * 16),
              tr(w_ukv).reshape(ukv_rows, d), w_o[0], tr(w_gate), tr(w_up), w_down[0]]
    pieces = [p.astype(BF16) for p in pieces]
    offs = [0]
    for p in pieces:
        offs.append(offs[-1] + p.shape[0])
    legs = [(0, 1), (1, 5), (5, 7)]
    gathered = {}

    def full(i, rows):
        leg = next(n for n, (lo, hi) in enumerate(legs) if lo <= i < hi)
        base = offs[legs[leg][0]]
        return gathered[leg][:, offs[i] - base:offs[i] - base + rows]

    x2d = x.reshape(t, d)
    h1, gathered[0] = _rmsnorm(x2d, 0, d, norm_mix_g, BF16, "norm_mix", traffic=_Traffic("spread", pieces[0:1]))

    w_in_t = full(0, in_cols).reshape(n_in, d)
    n_qkv = 3 * fw
    w_in_a = w_in_t[:n_qkv].reshape(3, PAIRS, LANES, d).transpose(1, 0, 2, 3).reshape(n_qkv, d)
    lat0, rope0 = n_qkv + HEADS, n_qkv + HEADS + q_rank + kv_rank
    k_rep = jnp.broadcast_to(w_in_t[rope0:].reshape(2, 1, half, d), (2, 4, half, d)).reshape(LANES, d)
    w_in_b = jnp.concatenate([w_in_t[lat0:rope0], k_rep, w_in_t[n_qkv:lat0],
                              jnp.zeros((b_cols - o_flogit - HEADS, d), BF16)], axis=0)

    def per_head_rows(a):
        return a.reshape(bl, s, HEADS).transpose(0, 2, 1).reshape(bh, 1, s)

    proj_a =_matmul(h1, w_in_a, "nt", BF16, "proj_fox", tn=3 * LANES)
    proj_b = _matmul(h1, w_in_b, "nt", F32, "proj_mla", tn=b_cols)

    z = proj_b[:, o_flogit:o_flogit + HEADS].reshape(bl, s, HEADS).transpose(0, 2, 1).reshape(bh * grp, LANES)
    bcol = jnp.broadcast_to(b_fgate.reshape(1, HEADS, 1), (bl, HEADS, grp)).reshape(bh * grp, 1)
    c = _fgate(z, bcol, grp, "forget_gate")
    c_bias = c.reshape(bh, 1, s)
    fox_o, fox_lse, gathered[1] = _attn_fwd((proj_a,), c_bias, HEAD_DIM ** -0.5, bl, s, tq, "fox_attention",
                                            traffic=_Traffic("spread", pieces[legs[1][0]:legs[1][1]]))
    w_uq_h = full(1, uq_rows).reshape(HEADS, MLA_QK, q_rank)
    w_uq_pe = jnp.concatenate([w_uq_h[:, HEAD_DIM:HEAD_DIM + half].reshape(2, 1, 4 * half, q_rank),
                               w_uq_h[:, HEAD_DIM + half:].reshape(2, 1, 4 * half, q_rank)], axis=1).reshape(2 * LANES, q_rank)
    w_uq_p = jnp.concatenate([w_uq_h[:, :HEAD_DIM].reshape(fw, q_rank), w_uq_pe], axis=0)
    w_ukv_p = full(2, ukv_rows).reshape(PAIRS, 2, 2, HEAD_DIM, kv_rank).transpose(0, 2, 1, 3, 4).reshape(2 * fw, kv_rank)
    w_o_f = full(3, w_o.shape[1]).reshape(-1, d)
    w_gate_t = full(4, ff // N_DEV).reshape(ff, d)

    qn = _rmsnorm(proj_b, 0, q_rank, q_norm_g, BF16, "norm_q")
    kvn = _rmsnorm(proj_b, o_kvlat // kv_rank, kv_rank, kv_norm_g, BF16, "norm_kv")
    q_raw = _matmul(qn, w_uq_p, "nt", F32, "up_q", tn=fw + 2 * LANES)
    kv_all = _matmul(kvn, w_ukv_p, "nt", BF16, "up_kv")
    inv_freq = ROPE_THETA ** (-jnp.arange(0, MLA_ROPE, 2, dtype=F32) / MLA_ROPE)
    ang = positions.astype(F32).reshape(t, 1) * inv_freq[None, :]
    cos4, sin4 = jnp.tile(jnp.cos(ang), (1, 4)), jnp.tile(jnp.sin(ang), (1, 4))
    rope_cos, rope_sin = jnp.concatenate([cos4, cos4], axis=1), jnp.concatenate([-sin4, sin4], axis=1)
    q_all, kpe = _rope_fwd(q_raw, fw, proj_b, o_krope // LANES, rope_cos, rope_sin, "rope")
    mla_ops = (q_all, kv_all, kpe)
    mla_o, mla_lse, gathered[2] = _attn_fwd(mla_ops, None, MLA_QK ** -0.5, bl, s, tq, "mla_attention",
                                            traffic=_Traffic("spread", pieces[legs[2][0]:legs[2][1]]))
    w_up_t, w_down_f = full(5, ff // N_DEV).reshape(ff, d), full(6, ff // N_DEV).reshape(ff, d)

    cat = _out_norm(fox_o, mla_o, fox_out_g, mla_out_g, "norm_out")
    x1 = _matmul(cat, w_o_f, "nn", F32, "proj_out", res=x2d)
    h2 = _rmsnorm(x1, 0, d, norm_ffn_g, BF16, "norm_ffn")
    gate, up, act = _ffn_up(h2, w_gate_t, w_up_t, "ffn_gate_up")
    x2 = _matmul(act, w_down_f, "nn", F32, "ffn_down", res=x1)
    dx2, dg_final, loss_part, dx2_b = _loss_bwd(x2, loss_target.reshape(t, d), final_norm_g.reshape(1, d), "final_norm_loss")

    d_gate, d_up = _ffn_down_bwd(dx2_b, w_down_f, gate, up, "d_ffn_down")
    dw_down = _matmul(act, dx2_b, "tn", BF16, "dw_down", tm=ff // 2, tn=d, tk=1024)
    dh2 = _matmul(d_up, w_up_t, "nn", F32, "d_ffn_up", res=_matmul(d_gate, w_gate_t, "nn", F32, "d_ffn_gate"))
    dw_gate = _matmul(d_gate, h2, "tn", BF16, "dw_gate", tm=ff // 2, tn=d, tk=1024)
    dw_up = _matmul(d_up, h2, "tn", BF16, "dw_up", tm=ff // 2, tn=d, tk=1024)
    dx1, dg_ffn, dx1_b = _rmsnorm_bwd(x1, 0, d, norm_ffn_g, dh2, "d_norm_ffn", res=dx2, mxu_copy=True)
    dcat = _matmul(dx1_b, w_o_f, "nt", F32, "d_proj_out")
    dw_o = _matmul(cat, dx1_b, "tn", BF16, "dw_o", tn=d, tk=1024)
    d_fox_o, d_mla_o, fox_delta, mla_delta, dg_fox, dg_mla = _out_norm_bwd(fox_o, mla_o, fox_out_g, mla_out_g, dcat, "d_norm_out")

    per_dev = lambda a: a.reshape(N_DEV, -1, d)
    late_grads = [per_dev(dw_o), per_dev(dw_gate), per_dev(dw_up), per_dev(dw_down)]
    dproj_a, dc, g_late = _attn_bwd((proj_a,), d_fox_o, fox_lse, per_head_rows(fox_delta),
                                    c_bias, HEAD_DIM ** -0.5, bl, s, tq, "d_fox_attention", traffic=_Traffic("swap", late_grads))
    dz, db_fgate = _fgate_bwd(z, bcol, dc.reshape(bh * grp, LANES), grp, "d_forget_gate")
    d_flogit = dz.reshape(bl, HEADS, s).transpose(0, 2, 1).reshape(t, HEADS)

    dq_nope, dkv_all, dq_pe, dk_pe = _attn_bwd(mla_ops, d_mla_o, mla_lse, per_head_rows(mla_delta),
                                               None, MLA_QK ** -0.5, bl, s, tq, "d_mla_attention")
    dq_rot, dk_rot = _rope_bwd(dq_pe, dk_pe, rope_cos, rope_sin, "d_rope")
    dqn = _matmul(dq_rot, w_uq_p[fw:], "nn", F32, "d_up_q_rope", res=_matmul(dq_nope, w_uq_p[:fw], "nn", F32, "d_up_q_nope"))
    dw_uq_nope = _matmul(dq_nope, qn, "tn", BF16, "dw_uq_nope", tn=q_rank, tk=1024)
    dw_uq_pe = _matmul(dq_rot, qn, "tn", BF16, "dw_uq_rope", tn=q_rank, tk=1024)
    dq_lat, dg_q = _rmsnorm_bwd(proj_b, 0, q_rank, q_norm_g, dqn, "d_norm_q")
    dkvn = _matmul(dkv_all, w_ukv_p, "nn", F32, "d_up_kv")
    dw_ukv_p = _matmul(dkv_all, kvn, "tn", BF16, "dw_ukv", tn=kv_rank, tk=1024)
    dkv_lat, dg_kv = _rmsnorm_bwd(proj_b, o_kvlat // kv_rank, kv_rank, kv_norm_g, dkvn, "d_norm_kv")

    dproj_b = jnp.concatenate([dq_lat.astype(BF16), dkv_lat.astype(BF16), dk_rot.astype(BF16), d_flogit.astype(BF16),
                               jnp.zeros((t, b_cols - o_flogit - HEADS), BF16)], axis=1)
    dh1 = _matmul(dproj_b, w_in_b, "nn", F32, "d_proj_mla", res=_matmul(dproj_a, w_in_a, "nn", F32, "d_proj_fox"))
    dw_in_a = _matmul(dproj_a, h1, "tn", BF16, "dw_in_fox", tn=d, tk=1024)
    dw_in_b = _matmul(dproj_b, h1, "tn", F32, "dw_in_mla", tm=b_cols, tn=d, tk=1024)

    dw_krope = dw_in_b[o_krope:o_flogit].reshape(2, 4, half, d).sum(axis=1).reshape(MLA_ROPE, d)
    dw_in_t = jnp.concatenate([dw_in_a.reshape(PAIRS, 3, LANES, d).transpose(1, 0, 2, 3).reshape(n_qkv, d),
                               dw_in_b[o_flogit:o_flogit + HEADS].astype(BF16), dw_in_b[:o_krope].astype(BF16),
                               dw_krope.astype(BF16)], axis=0)
    pad_dev = lambda a, rows: jnp.pad(a, ((0, 0), (0, rows - a.shape[1]), (0, 0)))
    dw_uq_pe5 = dw_uq_pe.reshape(2, 2, 4, half, q_rank)
    dw_uq_h = jnp.concatenate([dw_uq_nope.reshape(HEADS, HEAD_DIM, q_rank), dw_uq_pe5[:, 0].reshape(HEADS, half, q_rank),
                               dw_uq_pe5[:, 1].reshape(HEADS, half, q_rank)], axis=1)
    dw_ukv_h = dw_ukv_p.reshape(PAIRS, 2, 2, HEAD_DIM, kv_rank).transpose(0, 2, 1, 3, 4).reshape(HEADS, 2 * HEAD_DIM, kv_rank)
    n_last = 3
    last_grads = [pad_dev(per_dev(dw_in_t), pieces[0].shape[0]), pad_dev(per_dev(dw_uq_h), pieces[1].shape[0]), per_dev(dw_ukv_h)]
    grad_x, dg_mix, g_last = _rmsnorm_bwd(x2d, 0, d, norm_mix_g, dh1, "d_norm_mix", res=dx1, traffic=_Traffic("swap", last_grads))
    g_last = _sum_blocks(g_last, "sum_last_grads")
    g_late = _sum_blocks(g_late, "sum_late_grads")

    def mine(i, rows):
        src, base = (g_last, 0) if i < n_last else (g_late, offs[n_last])
        return src[offs[i] - base:offs[i] - base + rows]

    big = [
        ("w_in", w_in, m_w_in, v_w_in, mine(0, in_cols).T),
        ("w_uq", w_uq, m_w_uq, v_w_uq, mine(1, uq_rows).reshape(-1, q_rank).T),
        ("w_ukv", w_ukv, m_w_ukv, v_w_ukv, mine(2, ukv_rows).reshape(-1, kv_rank).T),
        ("w_o", w_o, m_w_o, v_w_o, mine(3, w_o.shape[1])),
        ("w_gate", w_gate, m_w_gate, v_w_gate, mine(4, ff // N_DEV).T),
        ("w_up", w_up, m_w_up, v_w_up, mine(5, ff // N_DEV).T),
        ("w_down", w_down, m_w_down, v_w_down, mine(6, ff // N_DEV)),
    ]
    out = {}
    for nm, w, m, v, g in big:
        dl, new_m, new_v = _adamw(w[0], g, m[0], v[0], "adamw_" + nm)
        out[nm] = (g[None], dl[None], new_m[None], new_v[None])

    smalls = [("norm_mix_g", norm_mix_g, m_norm_mix_g, v_norm_mix_g, dg_mix),
              ("b_fgate", b_fgate, m_b_fgate, v_b_fgate, db_fgate.reshape(1, HEADS)),
              ("q_norm_g", q_norm_g, m_q_norm_g, v_q_norm_g, dg_q),
              ("kv_norm_g", kv_norm_g, m_kv_norm_g, v_kv_norm_g, dg_kv),
              ("fox_out_g", fox_out_g, m_fox_out_g, v_fox_out_g, dg_fox),
              ("mla_out_g", mla_out_g, m_mla_out_g, v_mla_out_g, dg_mla),
              ("norm_ffn_g", norm_ffn_g, m_norm_ffn_g, v_norm_ffn_g, dg_ffn),
              ("final_norm_g", final_norm_g, m_final_norm_g, v_final_norm_g, dg_final)]
    flat = lambda a: a.reshape(1, -1)
    results, loss = _small_all_reduce_adamw([e[4] for e in smalls], loss_part, [flat(e[1]) for e in smalls],
                                            [flat(e[2]) for e in smalls], [flat(e[3]) for e in smalls], "reduce_small_adamw")
    for (nm, w, _, _, _), res in zip(smalls, results):
        out[nm] = tuple(a.reshape(w.shape) for a in res)
    loss = loss[0, 0]

    order = ["norm_mix_g", "w_in", "b_fgate", "q_norm_g", "w_uq", "kv_norm_g", "w_ukv", "fox_out_g", "mla_out_g", "w_o",
             "norm_ffn_g", "w_gate", "w_up", "w_down", "final_norm_g"]
    return (loss, grad_x.reshape(bl, s, d), *[out[n][0] for n in order], *[out[n][1] for n in order],
            *[out[n][2] for n in order], *[out[n][3] for n in order])
```

```python
import jax
import jax.numpy as jnp
from jax import lax
from jax.experimental import pallas as pl
from jax.experimental.pallas import tpu as pltpu

F32 = jnp.float32
BF16 = jnp.bfloat16
MESH = pl.DeviceIdType.MESH

N_DEV = 8
HEADS = 8
HEAD_DIM = 64
PAIRS = HEADS // 2
MLA_ROPE = 32
MLA_QK = HEAD_DIM + MLA_ROPE
ROPE_THETA = 10000.0
NORM_EPS = 1e-6
ADAM_LR, ADAM_B1, ADAM_B2, ADAM_EPS, ADAM_WD, ADAM_STEP = 0.001, 0.9, 0.999, 1e-08, 0.01, 10

LANES = 128
MASKED = -1e30
VMEM_LIMIT = 48 * 1024 * 1024

_DIMS = {"nn": (((1,), (0,)), ((), ())), "nt": (((1,), (1,)), ((), ())), "tn": (((0,), (0,)), ((), ()))}


def _params(*sem):
    return pltpu.CompilerParams(dimension_semantics=sem, vmem_limit_bytes=VMEM_LIMIT)


def _dot(a, b, mode):
    return lax.dot_general(a.astype(BF16), b.astype(BF16), _DIMS[mode], preferred_element_type=F32)


def _tile(n, pref, unit=8):
    if n <= pref:
        return n
    t = pref - pref % unit
    while n % t:
        t -= unit
    return t


def _log2(n):
    assert n & (n - 1) == 0
    return n.bit_length() - 1


def _matmul(a, b, mode, out_dtype, name, tm=512, tn=512, tk=None, res=None, traffic=None):
    if mode == "nn":
        (m, kd), n = a.shape, b.shape[1]
    elif mode == "nt":
        (m, kd), n = a.shape, b.shape[0]
    else:
        (kd, m), n = a.shape, b.shape[1]
    tm, tn = _tile(m, tm, LANES if mode == "tn" else 16), _tile(n, tn, LANES)
    tk = kd if tk is None else _tile(kd, tk, LANES)
    nk = kd // tk
    a_spec = pl.BlockSpec((tk, tm), lambda i, j, k: (k, i)) if mode == "tn" else pl.BlockSpec((tm, tk), lambda i, j, k: (i, k))
    b_spec = pl.BlockSpec((tn, tk), lambda i, j, k: (j, k)) if mode == "nt" else pl.BlockSpec((tk, tn), lambda i, j, k: (k, j))
    o_spec = pl.BlockSpec((tm, tn), lambda i, j, k: (i, j))
    has_res = res is not None
    n_carried = len(traffic.pieces) if traffic else 0
    grid = (m // tm, n // tn, nk)

    def body(*refs):
        a_ref, b_ref = refs[:2]
        r_ref = refs[2] if has_res else None
        n_in = 2 + has_res + n_carried
        o_ref = refs[n_in]
        step = [pl.program_id(axis) for axis in range(3)]
        if traffic:
            carried_in, carried_out, sems = refs[2 + has_res:n_in], refs[n_in + 1], refs[len(refs) - 3:]

            @pl.when((step[0] == 0) & (step[1] == 0) & (step[2] == 0))
            def _():
                traffic.start(carried_in, carried_out, *sems)

        def finish(acc):
            if has_res:
                acc = acc + r_ref[...]
            o_ref[...] = acc.astype(out_dtype)

        part = _dot(a_ref[...], b_ref[...], mode)
        if nk == 1:
            finish(part)
        else:
            acc_ref = refs[n_in + 1 + bool(traffic)]

            @pl.when(step[2] == 0)
            def _():
                acc_ref[...] = part

            @pl.when(step[2] > 0)
            def _():
                acc_ref[...] += part

            @pl.when(step[2] == nk - 1)
            def _():
                finish(acc_ref[...])

        if traffic:
            @pl.when((step[0] == grid[0] - 1) & (step[1] == grid[1] - 1) & (step[2] == nk - 1))
            def _():
                traffic.wait(carried_out, *sems)

    in_specs = [a_spec, b_spec] + ([o_spec] if has_res else [])
    out_specs, out_shape = [o_spec], [jax.ShapeDtypeStruct((m, n), out_dtype)]
    scratch = [pltpu.VMEM((tm, tn), F32)] if nk > 1 else []
    if traffic:
        in_specs += traffic.in_specs
        out_specs.append(traffic.out_spec)
        out_shape.append(traffic.out_shape)
        scratch += traffic.scratch
    out = pl.pallas_call(
        body, name=name, grid=grid, in_specs=in_specs, out_specs=tuple(out_specs), out_shape=tuple(out_shape),
        scratch_shapes=scratch,
        compiler_params=_params(*(("arbitrary",) * 3 if traffic else ("parallel", "parallel", "arbitrary"))),
    )(*([a, b] + ([res] if has_res else []) + (traffic.pieces if traffic else [])))
    return out if traffic else out[0]


def _rstd(x):
    return lax.rsqrt(jnp.mean(x * x, axis=-1, keepdims=True) + NORM_EPS)


def _norm_bwd(x, g, dy):
    r = _rstd(x)
    xh = x * r
    u = dy * g
    dx = r * (u - xh * jnp.mean(u * xh, axis=-1, keepdims=True))
    return dx, jnp.sum(dy * xh, axis=0, keepdims=True)


def _rmsnorm(x, col, width, g, out_dtype, name, traffic=None):
    t = x.shape[0]
    tm = _tile(t, 512)
    steps = t // tm
    n_carried = len(traffic.pieces) if traffic else 0

    def body(*refs):
        x_ref, g_ref, o_ref = refs[0], refs[1], refs[2 + n_carried]
        if traffic:
            carried_in, carried_out, sems = refs[2:2 + n_carried], refs[3 + n_carried], refs[4 + n_carried:]

            @pl.when(pl.program_id(0) == 0)
            def _():
                traffic.start(carried_in, carried_out, *sems)

        xv = x_ref[...]
        o_ref[...] = ((xv * _rstd(xv)) * g_ref[...]).astype(out_dtype)
        if traffic:
            @pl.when(pl.program_id(0) == steps - 1)
            def _():
                traffic.wait(carried_out, *sems)

    in_specs = [pl.BlockSpec((tm, width), lambda i: (i, col)), pl.BlockSpec((1, width), lambda i: (0, 0))]
    out_specs = [pl.BlockSpec((tm, width), lambda i: (i, 0))]
    out_shape = [jax.ShapeDtypeStruct((t, width), out_dtype)]
    if traffic:
        in_specs += traffic.in_specs
        out_specs.append(traffic.out_spec)
        out_shape.append(traffic.out_shape)
    out = pl.pallas_call(
        body, name=name, grid=(steps,), in_specs=in_specs, out_specs=tuple(out_specs), out_shape=tuple(out_shape),
        scratch_shapes=traffic.scratch if traffic else [],
        compiler_params=_params("arbitrary" if traffic else "parallel"),
    )(x, g, *(traffic.pieces if traffic else []))
    return out if traffic else out[0]


def _rmsnorm_bwd(x, col, width, g, dy, name, res=None, mxu_copy=False, traffic=None):
    t = x.shape[0]
    tm = _tile(t, 512)
    steps = t // tm
    has_res = res is not None
    n_carried = len(traffic.pieces) if traffic else 0
    row = pl.BlockSpec((tm, width), lambda i: (i, 0))
    vec = pl.BlockSpec((1, width), lambda i: (0, 0))

    def body(*refs):
        x_ref, g_ref, dy_ref = refs[:3]
        n_in = 3 + has_res + n_carried
        dx_ref, dg_ref = refs[n_in:n_in + 2]
        if traffic:
            carried_in, carried_out, sems = refs[3 + has_res:n_in], refs[n_in + 2 + mxu_copy], refs[len(refs) - 3:]

            @pl.when(pl.program_id(0) == 0)
            def _():
                traffic.start(carried_in, carried_out, *sems)

        dx, dg = _norm_bwd(x_ref[...], g_ref[...], dy_ref[...])
        if has_res:
            dx = dx + refs[3][...]
        dx_ref[...] = dx
        if mxu_copy:
            refs[n_in + 2][...] = dx.astype(BF16)

        @pl.when(pl.program_id(0) == 0)
        def _():
            dg_ref[...] = jnp.zeros_like(dg_ref)

        dg_ref[...] += dg
        if traffic:
            @pl.when(pl.program_id(0) == steps - 1)
            def _():
                traffic.wait(carried_out, *sems)

    in_specs = [pl.BlockSpec((tm, width), lambda i: (i, col)), vec, row] + ([row] if has_res else [])
    out_specs = (row, vec) + ((row,) if mxu_copy else ())
    out_shape = (jax.ShapeDtypeStruct((t, width), F32), jax.ShapeDtypeStruct((1, width), F32)) \
        + ((jax.ShapeDtypeStruct((t, width), BF16),) if mxu_copy else ())
    if traffic:
        in_specs += traffic.in_specs
        out_specs += (traffic.out_spec,)
        out_shape += (traffic.out_shape,)
    return pl.pallas_call(
        body, name=name, grid=(steps,), in_specs=in_specs, out_specs=out_specs, out_shape=out_shape,
        scratch_shapes=traffic.scratch if traffic else [], compiler_params=_params("arbitrary"),
    )(*([x, g, dy] + ([res] if has_res else []) + (traffic.pieces if traffic else [])))


def _out_norm(fo, mo, gf, gm, name):
    t, w = fo.shape
    tm = _tile(t, 512)
    row = pl.BlockSpec((tm, w), lambda i: (i, 0))
    vec = pl.BlockSpec((1, w), lambda i: (0, 0))

    def body(fo_ref, mo_ref, gf_ref, gm_ref, o_ref):
        f, m = fo_ref[...], mo_ref[...]
        o_ref[:, :w] = ((f * _rstd(f)) * gf_ref[...]).astype(BF16)
        o_ref[:, w:] = ((m * _rstd(m)) * gm_ref[...]).astype(BF16)

    return pl.pallas_call(
        body, name=name, grid=(t // tm,), in_specs=[row, row, vec, vec],
        out_specs=pl.BlockSpec((tm, 2 * w), lambda i: (i, 0)),
        out_shape=jax.ShapeDtypeStruct((t, 2 * w), BF16),
        compiler_params=_params("parallel"),
    )(fo, mo, gf, gm)


def _split3(x):
    hi = x.astype(BF16)
    r1 = x - hi.astype(F32)
    mid = r1.astype(BF16)
    lo = (r1 - mid.astype(F32)).astype(BF16)
    return hi, mid, lo


def _dot_x01(x, m01):
    hi, mid, lo = _split3(x)
    d = lambda p: lax.dot_general(p, m01, _DIMS["nn"], preferred_element_type=F32)
    return (d(lo) + d(mid)) + d(hi)


def _dot_01x(m01, x):
    hi, mid, lo = _split3(x)
    d = lambda p: lax.dot_general(m01, p, _DIMS["nn"], preferred_element_type=F32)
    return (d(lo) + d(mid)) + d(hi)


def _out_norm_bwd(fo, mo, gf, gm, dcat, name):
    t, w = fo.shape
    nh = w // HEAD_DIM
    tm = _tile(t, 512)
    row = pl.BlockSpec((tm, w), lambda i: (i, 0))
    vec = pl.BlockSpec((1, w), lambda i: (0, 0))
    hrow = pl.BlockSpec((tm, nh), lambda i: (i, 0))

    def body(fo_ref, mo_ref, gf_ref, gm_ref, dc_ref, dfo_ref, dmo_ref, ff_ref, fm_ref, dgf_ref, dgm_ref):
        lane_head = lax.shift_right_logical(lax.broadcasted_iota(jnp.int32, (w, nh), 0), _log2(HEAD_DIM))
        sel = (lane_head == lax.broadcasted_iota(jnp.int32, (w, nh), 1)).astype(BF16)
        f, m = fo_ref[...], mo_ref[...]
        dfo, dgf = _norm_bwd(f, gf_ref[...], dc_ref[:, :w])
        dmo, dgm = _norm_bwd(m, gm_ref[...], dc_ref[:, w:])
        dfo_ref[...] = dfo.astype(BF16)
        dmo_ref[...] = dmo.astype(BF16)
        ff_ref[...] = _dot_x01(dfo * f, sel)
        fm_ref[...] = _dot_x01(dmo * m, sel)

        @pl.when(pl.program_id(0) == 0)
        def _():
            dgf_ref[...] = jnp.zeros_like(dgf_ref)
            dgm_ref[...] = jnp.zeros_like(dgm_ref)

        dgf_ref[...] += dgf
        dgm_ref[...] += dgm

    return pl.pallas_call(
        body, name=name, grid=(t // tm,),
        in_specs=[row, row, vec, vec, pl.BlockSpec((tm, 2 * w), lambda i: (i, 0))],
        out_specs=(row, row, hrow, hrow, vec, vec),
        out_shape=(jax.ShapeDtypeStruct((t, w), BF16), jax.ShapeDtypeStruct((t, w), BF16),
                   jax.ShapeDtypeStruct((t, nh), F32), jax.ShapeDtypeStruct((t, nh), F32),
                   jax.ShapeDtypeStruct((1, w), F32), jax.ShapeDtypeStruct((1, w), F32)),
        compiler_params=_params("arbitrary"),
    )(fo, mo, gf, gm, dcat)


def _loss_bwd(x, tgt, g, name):
    t, d = x.shape
    tm = _tile(t, 512)
    row = pl.BlockSpec((tm, d), lambda i: (i, 0))
    vec = pl.BlockSpec((1, d), lambda i: (0, 0))
    one = pl.BlockSpec((1, 1), lambda i: (0, 0))

    def body(x_ref, t_ref, g_ref, dx_ref, dg_ref, loss_ref, dxb_ref):
        xv, gv = x_ref[...], g_ref[...]
        diff = (xv * _rstd(xv)) * gv - t_ref[...]
        dx, dg = _norm_bwd(xv, gv, diff / d)
        dx_ref[...] = dx
        dxb_ref[...] = dx.astype(BF16)

        @pl.when(pl.program_id(0) == 0)
        def _():
            dg_ref[...] = jnp.zeros_like(dg_ref)
            loss_ref[...] = jnp.zeros_like(loss_ref)

        dg_ref[...] += dg
        loss_ref[...] += 0.5 * jnp.sum(jnp.mean(diff * diff, axis=-1, keepdims=True), axis=0, keepdims=True)

    return pl.pallas_call(
        body, name=name, grid=(t // tm,), in_specs=[row, row, vec], out_specs=(row, vec, one, row),
        out_shape=(jax.ShapeDtypeStruct((t, d), F32), jax.ShapeDtypeStruct((1, d), F32), jax.ShapeDtypeStruct((1, 1), F32),
                   jax.ShapeDtypeStruct((t, d), BF16)),
        compiler_params=_params("arbitrary"),
    )(x, tgt, g)


def _ffn_up(h, wg_t, wu_t, name, tm=512, tf=1408):
    t, d = h.shape
    f = wg_t.shape[0]
    tm, tf = _tile(t, tm, 16), _tile(f, tf, LANES)
    tok = pl.BlockSpec((tm, tf), lambda i, j: (i, j))
    wt = pl.BlockSpec((tf, d), lambda i, j: (j, 0))

    def body(h_ref, wg_ref, wu_ref, g_ref, u_ref, a_ref):
        hv = h_ref[...]
        g, u = _dot(hv, wg_ref[...], "nt"), _dot(hv, wu_ref[...], "nt")
        g_ref[...], u_ref[...] = g, u
        a_ref[...] = ((g * jax.nn.sigmoid(g)) * u).astype(BF16)

    return pl.pallas_call(
        body, name=name, grid=(t // tm, f // tf), in_specs=[pl.BlockSpec((tm, d), lambda i, j: (i, 0)), wt, wt],
        out_specs=(tok, tok, tok),
        out_shape=(jax.ShapeDtypeStruct((t, f), F32), jax.ShapeDtypeStruct((t, f), F32), jax.ShapeDtypeStruct((t, f), BF16)),
        compiler_params=_params("parallel", "parallel"),
    )(h, wg_t, wu_t)


def _ffn_down_bwd(dy, w_down, gate, up, name, tm=512, tf=1408):
    t, d = dy.shape
    f = w_down.shape[0]
    tm, tf = _tile(t, tm, 16), _tile(f, tf, LANES)
    tok = pl.BlockSpec((tm, tf), lambda i, j: (i, j))

    def body(dy_ref, w_ref, g_ref, u_ref, dg_ref, du_ref):
        da = _dot(dy_ref[...], w_ref[...], "nt")
        g = g_ref[...]
        sg = jax.nn.sigmoid(g)
        dg_ref[...] = (da * u_ref[...] * (sg * (1.0 + g * (1.0 - sg)))).astype(BF16)
        du_ref[...] = (da * (g * sg)).astype(BF16)

    return pl.pallas_call(
        body, name=name, grid=(t // tm, f // tf),
        in_specs=[pl.BlockSpec((tm, d), lambda i, j: (i, 0)), pl.BlockSpec((tf, d), lambda i, j: (j, 0)), tok, tok],
        out_specs=(tok, tok),
        out_shape=(jax.ShapeDtypeStruct((t, f), BF16), jax.ShapeDtypeStruct((t, f), BF16)),
        compiler_params=_params("parallel", "parallel"),
    )(dy, w_down, gate, up)


def _chunk_scan_mats(rows, grp, reverse):
    ii = lax.broadcasted_iota(jnp.int32, (LANES, LANES), 0)
    jj = lax.broadcasted_iota(jnp.int32, (LANES, LANES), 1)
    within = ((ii >= jj) if reverse else (ii <= jj)).astype(BF16)
    ones = jnp.ones((LANES, LANES), BF16)
    ri = lax.broadcasted_iota(jnp.int32, (rows, rows), 0)
    rj = lax.broadcasted_iota(jnp.int32, (rows, rows), 1)
    sh = _log2(grp)
    same = lax.shift_right_logical(ri, sh) == lax.shift_right_logical(rj, sh)
    across = (same & ((rj > ri) if reverse else (rj < ri))).astype(BF16)
    return within, ones, across


def _running_sum(v, mats):
    within, ones, across = mats
    return _dot_x01(v, within) + _dot_01x(across, _dot_x01(v, ones))


def _fgate(z, bcol, grp, name):
    rows = z.shape[0]

    def body(z_ref, b_ref, c_ref):
        zz = z_ref[...] + b_ref[...]
        log_f = jnp.minimum(zz, 0.0) - jnp.log1p(jnp.exp(-jnp.abs(zz)))
        c_ref[...] = _running_sum(log_f, _chunk_scan_mats(rows, grp, False))

    return pl.pallas_call(body, name=name, out_shape=jax.ShapeDtypeStruct(z.shape, F32),
                          compiler_params=pltpu.CompilerParams(vmem_limit_bytes=VMEM_LIMIT))(z, bcol)


def _fgate_bwd(z, bcol, dc, grp, name):
    rows = z.shape[0]

    def body(z_ref, b_ref, dc_ref, dz_ref, db_ref):
        zz = z_ref[...] + b_ref[...]
        dz = _running_sum(dc_ref[...], _chunk_scan_mats(rows, grp, True)) * jax.nn.sigmoid(-zz)
        dz_ref[...] = dz
        head = lax.shift_right_logical(lax.broadcasted_iota(jnp.int32, (HEADS, rows), 1), _log2(grp)) & (HEADS - 1)
        sel = (head == lax.broadcasted_iota(jnp.int32, (HEADS, rows), 0)).astype(BF16)
        db_ref[...] = jnp.sum(_dot_01x(sel, dz), axis=1, keepdims=True)

    return pl.pallas_call(
        body, name=name,
        out_shape=(jax.ShapeDtypeStruct(z.shape, F32), jax.ShapeDtypeStruct((HEADS, 1), F32)),
        compiler_params=pltpu.CompilerParams(vmem_limit_bytes=VMEM_LIMIT),
    )(z, bcol, dc)


def _rotate(x, cs, sn_signed):
    return x * cs + pltpu.roll(x, LANES // 2, axis=1) * sn_signed


def _rope_fwd(q_raw, nope, proj_b, k_col, cs, sn, name):
    t, qw = q_raw.shape
    tm = _tile(t, 512)
    row = pl.BlockSpec((tm, LANES), lambda i: (i, 0))

    def body(q_ref, k_ref, c_ref, s_ref, qo_ref, ko_ref):
        c, s = c_ref[...], s_ref[...]
        qo_ref[:, :nope] = q_ref[:, :nope].astype(BF16)
        for off in range(nope, qw, LANES):
            qo_ref[:, off:off + LANES] = _rotate(q_ref[:, off:off + LANES], c, s).astype(BF16)
        ko_ref[...] = _rotate(k_ref[...], c, s).astype(BF16)

    return pl.pallas_call(
        body, name=name, grid=(t // tm,),
        in_specs=[pl.BlockSpec((tm, qw), lambda i: (i, 0)), pl.BlockSpec((tm, LANES), lambda i: (i, k_col)), row, row],
        out_specs=(pl.BlockSpec((tm, qw), lambda i: (i, 0)), row),
        out_shape=(jax.ShapeDtypeStruct((t, qw), BF16), jax.ShapeDtypeStruct((t, LANES), BF16)),
        compiler_params=_params("parallel"),
    )(q_raw, proj_b, cs, sn)


def _rope_bwd(dq_pe, dk_pe, cs, sn, name):
    t, qw = dq_pe.shape
    tm = _tile(t, 512)
    row = pl.BlockSpec((tm, LANES), lambda i: (i, 0))
    wide = pl.BlockSpec((tm, qw), lambda i: (i, 0))

    def body(q_ref, k_ref, c_ref, s_ref, qo_ref, ko_ref):
        c, s = c_ref[...], -s_ref[...]
        for off in range(0, qw, LANES):
            qo_ref[:, off:off + LANES] = _rotate(q_ref[:, off:off + LANES], c, s).astype(BF16)
        ko_ref[...] = _rotate(k_ref[...], c, s)

    return pl.pallas_call(
        body, name=name, grid=(t // tm,), in_specs=[wide, row, row, row], out_specs=(wide, row),
        out_shape=(jax.ShapeDtypeStruct((t, qw), BF16), jax.ShapeDtypeStruct((t, LANES), F32)),
        compiler_params=_params("parallel"),
    )(dq_pe, dk_pe, cs, sn)


def _lane_masks(pair, h, pe):
    lane = lax.broadcasted_iota(jnp.int32, (1, LANES), 1)
    in_head = lax.shift_right_logical(lane, _log2(HEAD_DIM)) == h
    in_rope = ((lax.shift_right_logical(lane, _log2(MLA_ROPE // 2)) & 3) == ((2 * pair + h) & 3)) if pe else None
    return in_head, in_rope


def _keep(mask, v):
    return jnp.where(mask, v, jnp.zeros_like(v))


def _to_row(col):
    n = col.shape[0]
    eye = lax.broadcasted_iota(jnp.int32, (n, n), 0) == lax.broadcasted_iota(jnp.int32, (n, n), 1)
    return jnp.sum(jnp.where(eye, col, 0.0), axis=0, keepdims=True)


def _to_col(row):
    n = row.shape[1]
    eye = lax.broadcasted_iota(jnp.int32, (n, n), 0) == lax.broadcasted_iota(jnp.int32, (n, n), 1)
    return jnp.sum(jnp.where(eye, row, 0.0), axis=1, keepdims=True)


def _first_step():
    return (pl.program_id(0) == 0) & (pl.program_id(1) == 0)


def _last_step(n0, n1):
    return (pl.program_id(0) == n0 - 1) & (pl.program_id(1) == n1 - 1)


def _attn_fwd(ops, bias, scale, bl, s, tq, name, traffic=None):
    pe = len(ops) == 3
    has_bias = bias is not None
    nq = s // tq
    t = bl * s
    n_carried = len(traffic.pieces) if traffic else 0

    def body(*refs):
        sems = refs[len(refs) - 3:] if traffic else ()
        if pe:
            q_ref, qpe_ref, kv_ref, kpe_ref = refs[:4]
            n_in = 4
            q_at = lambda r0, r1: q_ref[r0:r1, :]
            v_at = lambda r0, r1: kv_ref[r0:r1, LANES:]
            kcat = refs[len(refs) - 1 - len(sems)]
            kcat[:, :LANES] = kv_ref[:, :LANES]
            kcat[:, LANES:] = kpe_ref[...]
            k_at = lambda r0, r1: kcat[r0:r1, :]
        else:
            qkv_ref = refs[0]
            n_in = 1
            q_at = lambda r0, r1: qkv_ref[r0:r1, :LANES]
            k_at = lambda r0, r1: qkv_ref[r0:r1, LANES:2 * LANES]
            v_at = lambda r0, r1: qkv_ref[r0:r1, 2 * LANES:]
        if has_bias:
            c_ref = refs[n_in]
            n_in += 1
        carried_in = refs[n_in:n_in + n_carried]
        n_in += n_carried
        o_ref, lse_ref = refs[n_in:n_in + 2]
        if traffic:
            carried_out = refs[n_in + 2]

            @pl.when(_first_step())
            def _():
                traffic.start(carried_in, carried_out, *sems)

        pair = pl.program_id(1)
        causal = lax.broadcasted_iota(jnp.int32, (tq, tq), 1) <= lax.broadcasted_iota(jnp.int32, (tq, tq), 0)
        o_ref[...] = jnp.zeros_like(o_ref)

        def head(h, carry):
            in_head, in_rope = _lane_masks(pair, h, pe)
            for i in range(nq):
                r0, r1 = i * tq, (i + 1) * tq
                qm = _keep(in_head, q_at(r0, r1))
                if pe:
                    qm = jnp.concatenate([qm, _keep(in_rope, qpe_ref[r0:r1, :])], axis=1)
                cq = _to_col(c_ref[h, :, r0:r1]) if has_bias else None

                def logits(k0, k1):
                    sc = _dot(qm, k_at(k0, k1), "nt") * scale
                    if has_bias:
                        sc = sc + (cq - c_ref[h, :, k0:k1])
                    return sc

                sd = jnp.where(causal, logits(r0, r1), MASKED)
                m = jnp.max(sd, axis=1, keepdims=True)
                if i:
                    so = logits(0, r0)
                    m = jnp.maximum(m, jnp.max(so, axis=1, keepdims=True))
                pd = jnp.exp(sd - m)
                l = jnp.sum(pd, axis=1, keepdims=True)
                acc = _dot(pd, v_at(r0, r1), "nn")
                if i:
                    po = jnp.exp(so - m)
                    l = l + jnp.sum(po, axis=1, keepdims=True)
                    acc = acc + _dot(po, v_at(0, r0), "nn")
                o_ref[r0:r1, :] = jnp.where(in_head, acc / l, o_ref[r0:r1, :])
                lse_ref[h, :, r0:r1] = _to_row(m + jnp.log(l))
            return carry

        lax.fori_loop(0, 2, head, 0)
        if traffic:
            @pl.when(_last_step(bl, PAIRS))
            def _():
                traffic.wait(carried_out, *sems)

    seq = lambda w, col: pl.BlockSpec((s, w), col)
    if pe:
        in_specs = [seq(LANES, lambda b, p: (b, p)), seq(LANES, lambda b, p: (b, PAIRS + p // 2)),
                    seq(2 * LANES, lambda b, p: (b, p)), seq(LANES, lambda b, p: (b, 0))]
        args = [ops[0], ops[0], ops[1], ops[2]]
        scratch = [pltpu.VMEM((s, 2 * LANES), BF16)]
    else:
        in_specs = [seq(3 * LANES, lambda b, p: (b, p))]
        args = [ops[0]]
        scratch = []
    per_head_row = pl.BlockSpec((2, 1, s), lambda b, p: (b * PAIRS + p, 0, 0))
    if has_bias:
        in_specs.append(per_head_row)
        args.append(bias)
    out_specs = [seq(LANES, lambda b, p: (b, p)), per_head_row]
    out_shape = [jax.ShapeDtypeStruct((t, HEADS * HEAD_DIM), F32), jax.ShapeDtypeStruct((bl * HEADS, 1, s), F32)]
    if traffic:
        in_specs += traffic.in_specs
        args += traffic.pieces
        out_specs.append(traffic.out_spec)
        out_shape.append(traffic.out_shape)
        scratch += traffic.scratch
    return pl.pallas_call(
        body, name=name, grid=(bl, PAIRS), in_specs=in_specs, out_specs=tuple(out_specs), out_shape=tuple(out_shape),
        scratch_shapes=scratch, compiler_params=_params(*(("arbitrary", "arbitrary") if traffic else ("parallel", "parallel"))),
    )(*args)


def _attn_bwd(ops, do, lse, delta, bias, scale, bl, s, tq, name, traffic=None):
    pe = len(ops) == 3
    has_bias = bias is not None
    nq = s // tq
    t = bl * s
    width = 2 * LANES if pe else LANES
    n_carried = len(traffic.pieces) if traffic else 0

    def body(*refs):
        if pe:
            q_ref, qpe_ref, kv_ref, kpe_ref = refs[:4]
            n_in = 4
            k_at = lambda r0, r1: kv_ref[r0:r1, :LANES]
            v_at = lambda r0, r1: kv_ref[r0:r1, LANES:]
        else:
            qkv_ref = refs[0]
            n_in = 1
            k_at = lambda r0, r1: qkv_ref[r0:r1, LANES:2 * LANES]
            v_at = lambda r0, r1: qkv_ref[r0:r1, 2 * LANES:]
        do_ref, lse_ref, dl_ref = refs[n_in:n_in + 3]
        n_in += 3
        if has_bias:
            c_ref = refs[n_in]
            n_in += 1
        carried_in = refs[n_in:n_in + n_carried]
        rest = refs[n_in + n_carried:]
        if traffic:
            rest, sems = rest[:-3], rest[-3:]
            carried_out = rest[4 if pe else 2]
            rest = rest[:4 if pe else 2] + rest[(4 if pe else 2) + 1:]

            @pl.when(_first_step())
            def _():
                traffic.start(carried_in, carried_out, *sems)

        if pe:
            dqn_ref, dkv_ref, dqpe_ref, dkpe_ref, dq_acc, qcat = rest
            qcat[:, :LANES] = q_ref[...]
            qcat[:, LANES:] = qpe_ref[...]
            q_at = lambda r0, r1: qcat[r0:r1, :]
            dkv_ref[...] = jnp.zeros_like(dkv_ref)
        else:
            dqkv_ref, dc_ref, dq_acc = rest
            q_at = lambda r0, r1: qkv_ref[r0:r1, :LANES]
            dqkv_ref[...] = jnp.zeros_like(dqkv_ref)
            dc_ref[...] = jnp.zeros_like(dc_ref)
        pair = pl.program_id(1)
        dq_acc[...] = jnp.zeros_like(dq_acc)
        causal = lax.broadcasted_iota(jnp.int32, (tq, tq), 1) >= lax.broadcasted_iota(jnp.int32, (tq, tq), 0)

        def head(h, carry):
            in_head, in_rope = _lane_masks(pair, h, pe)
            for j in range(nq):
                r0, r1 = j * tq, (j + 1) * tq
                kt = _keep(in_head, k_at(r0, r1))
                if pe:
                    kt = jnp.concatenate([kt, _keep(in_rope, kpe_ref[r0:r1, :])], axis=1)
                vt = _keep(in_head, v_at(r0, r1))
                ck = _to_col(c_ref[h, :, r0:r1]) if has_bias else None

                def block(q0, q1, diagonal):
                    qq, dd = q_at(q0, q1), do_ref[q0:q1, :]
                    st = _dot(kt, qq, "nt") * scale
                    if has_bias:
                        st = st + (c_ref[h, :, q0:q1] - ck)
                    if diagonal:
                        st = jnp.where(causal, st, MASKED)
                    pt = jnp.exp(st - lse_ref[h, :, q0:q1])
                    dst = pt * (_dot(vt, dd, "nt") - dl_ref[h, :, q0:q1])
                    dsb = (dst * scale).astype(BF16)
                    dq_acc[q0:q1, :] += _dot(dsb, kt, "tn")
                    if has_bias:
                        dc_ref[h, :, q0:q1] += jnp.sum(dst, axis=0, keepdims=True)
                    return _dot(pt, dd, "nn"), _dot(dsb, qq, "nn"), (jnp.sum(dst, axis=1, keepdims=True) if has_bias else None)

                dv_c, dk_c, cs = block(r0, r1, True)
                if r1 < s:
                    dv_o, dk_o, cs_o = block(r1, s, False)
                    dv_c, dk_c = dv_c + dv_o, dk_c + dk_o
                    cs = cs + cs_o if has_bias else None
                if pe:
                    dkv_ref[r0:r1, :LANES] = jnp.where(in_head, dk_c[:, :LANES].astype(BF16), dkv_ref[r0:r1, :LANES])
                    dkv_ref[r0:r1, LANES:] = jnp.where(in_head, dv_c.astype(BF16), dkv_ref[r0:r1, LANES:])
                    dkpe_ref[r0:r1, :] += _keep(in_rope, dk_c[:, LANES:])
                else:
                    dqkv_ref[r0:r1, LANES:2 * LANES] = jnp.where(in_head, dk_c.astype(BF16), dqkv_ref[r0:r1, LANES:2 * LANES])
                    dqkv_ref[r0:r1, 2 * LANES:] = jnp.where(in_head, dv_c.astype(BF16), dqkv_ref[r0:r1, 2 * LANES:])
                    dc_ref[h, :, r0:r1] -= _to_row(cs)
            return carry

        if pe:
            @pl.when(pair == 0)
            def _():
                dkpe_ref[...] = jnp.zeros_like(dkpe_ref)

            @pl.when(pair % 2 == 0)
            def _():
                dqpe_ref[...] = jnp.zeros_like(dqpe_ref)

        lax.fori_loop(0, 2, head, 0)
        if pe:
            dqn_ref[...] = dq_acc[:, :LANES].astype(BF16)
            dqpe_ref[...] += dq_acc[:, LANES:]
        else:
            dqkv_ref[:, :LANES] = dq_acc[...].astype(BF16)
        if traffic:
            @pl.when(_last_step(bl, PAIRS))
            def _():
                traffic.wait(carried_out, *sems)

    seq = lambda w, col: pl.BlockSpec((s, w), col)
    per_head_row = pl.BlockSpec((2, 1, s), lambda b, p: (b * PAIRS + p, 0, 0))
    if pe:
        in_specs = [seq(LANES, lambda b, p: (b, p)), seq(LANES, lambda b, p: (b, PAIRS + p // 2)),
                    seq(2 * LANES, lambda b, p: (b, p)), seq(LANES, lambda b, p: (b, 0))]
        args = [ops[0], ops[0], ops[1], ops[2]]
    else:
        in_specs = [seq(3 * LANES, lambda b, p: (b, p))]
        args = [ops[0]]
    in_specs += [seq(LANES, lambda b, p: (b, p)), per_head_row, per_head_row]
    args += [do, lse, delta]
    if has_bias:
        in_specs.append(per_head_row)
        args.append(bias)
    scratch = [pltpu.VMEM((s, width), F32)]
    if pe:
        out_specs = (seq(LANES, lambda b, p: (b, p)), seq(2 * LANES, lambda b, p: (b, p)),
                     seq(LANES, lambda b, p: (b, p // 2)), seq(LANES, lambda b, p: (b, 0)))
        out_shape = (jax.ShapeDtypeStruct((t, PAIRS * LANES), BF16), jax.ShapeDtypeStruct((t, PAIRS * 2 * LANES), BF16),
                     jax.ShapeDtypeStruct((t, 2 * LANES), F32), jax.ShapeDtypeStruct((t, LANES), F32))
        scratch.append(pltpu.VMEM((s, 2 * LANES), BF16))
    else:
        out_specs = (seq(3 * LANES, lambda b, p: (b, p)), per_head_row)
        out_shape = (jax.ShapeDtypeStruct((t, PAIRS * 3 * LANES), BF16), jax.ShapeDtypeStruct((bl * HEADS, 1, s), F32))
    if traffic:
        in_specs += traffic.in_specs
        args += traffic.pieces
        out_specs += (traffic.out_spec,)
        out_shape += (traffic.out_shape,)
        scratch += traffic.scratch
    return pl.pallas_call(
        body, name=name, grid=(bl, PAIRS), in_specs=in_specs, out_specs=out_specs, out_shape=out_shape,
        scratch_shapes=scratch, compiler_params=_params("arbitrary" if traffic else "parallel", "arbitrary"),
    )(*args)


def _my_place():
    return lax.axis_index("x"), lax.axis_index("y"), lax.axis_index("c")


def _flip(p, bit):
    return 1 - p if bit else p


def _relative(x, y, c, k):
    return _flip(x, k & 4), _flip(y, k & 2), _flip(c, k & 1)


def _linear(x, y, c):
    return 4 * x + 2 * y + c


class _Traffic:
    def __init__(self, kind, pieces):
        self.kind, self.pieces = kind, list(pieces)
        self.rows = [p.shape[-2] for p in self.pieces]
        self.starts = [sum(self.rows[:i]) for i in range(len(self.rows))]
        anywhere = pl.BlockSpec(memory_space=pl.ANY)
        self.in_specs = [anywhere] * len(self.pieces)
        self.out_spec = anywhere
        self.out_shape = jax.ShapeDtypeStruct((N_DEV, sum(self.rows), self.pieces[0].shape[-1]), self.pieces[0].dtype)
        self.scratch = [pltpu.SemaphoreType.DMA((7,)), pltpu.SemaphoreType.DMA((7,)), pltpu.SemaphoreType.DMA(())]

    def start(self, p_refs, out_ref, send_sems, recv_sems, local_sem):
        x, y, c = _my_place()
        me = _linear(x, y, c)
        mine = lambda i, dev: p_refs[i] if self.kind == "spread" else p_refs[i].at[dev]
        landing = lambda i: out_ref.at[me, pl.ds(self.starts[i], self.rows[i])]
        for i in range(len(p_refs)):
            pltpu.make_async_copy(mine(i, me), landing(i), local_sem).start()
        for k in range(1, N_DEV):
            peer = _relative(x, y, c, k)
            for i in range(len(p_refs)):
                pltpu.make_async_remote_copy(
                    src_ref=mine(i, _linear(*peer)), dst_ref=landing(i),
                    send_sem=send_sems.at[k - 1], recv_sem=recv_sems.at[k - 1], device_id=peer, device_id_type=MESH).start()

    def wait(self, out_ref, send_sems, recv_sems, local_sem):
        x, y, c = _my_place()
        whole = out_ref.at[_linear(x, y, c)]
        for k in range(1, N_DEV):
            both = pltpu.make_async_remote_copy(
                src_ref=whole, dst_ref=whole, send_sem=send_sems.at[k - 1], recv_sem=recv_sems.at[k - 1],
                device_id=_relative(x, y, c, k), device_id_type=MESH)
            both.wait_recv()
            both.wait_send()
        pltpu.make_async_copy(whole, whole, local_sem).wait()


def _sum_blocks(parts, name):
    n, r, cdim = parts.shape
    tr = _tile(r, 640, 16)

    def body(p_ref, o_ref):
        acc = p_ref[0].astype(F32)
        for d in range(1, n):
            acc = acc + p_ref[d].astype(F32)
        o_ref[...] = acc

    return pl.pallas_call(
        body, name=name, grid=(r // tr,), in_specs=[pl.BlockSpec((n, tr, cdim), lambda i: (0, i, 0))],
        out_specs=pl.BlockSpec((tr, cdim), lambda i: (i, 0)), out_shape=jax.ShapeDtypeStruct((r, cdim), F32),
        compiler_params=_params("parallel"),
    )(parts)


def _adamw_math(w, g, m, v):
    m = ADAM_B1 * m + (1.0 - ADAM_B1) * g
    v = ADAM_B2 * v + (1.0 - ADAM_B2) * (g * g)
    m_hat = m / (1.0 - ADAM_B1 ** ADAM_STEP)
    v_hat = v / (1.0 - ADAM_B2 ** ADAM_STEP)
    delta = -ADAM_LR * (m_hat / (jnp.sqrt(v_hat) + ADAM_EPS) + ADAM_WD * w)
    return delta, m, v


def _adamw(w, g, m, v, name):
    def body(w_ref, g_ref, m_ref, v_ref, d_ref, nm_ref, nv_ref):
        d_ref[...], nm_ref[...], nv_ref[...] = _adamw_math(w_ref[...], g_ref[...], m_ref[...], v_ref[...])

    out = jax.ShapeDtypeStruct(w.shape, F32)
    return pl.pallas_call(body, name=name, out_shape=(out, out, out),
                          compiler_params=pltpu.CompilerParams(vmem_limit_bytes=VMEM_LIMIT))(w, g, m, v)


def _small_all_reduce_adamw(parts, loss_part, ws, ms, vs, name):
    sizes = [p.shape[1] for p in parts] + [1]
    spots = [sum(-(-n // LANES) * LANES for n in sizes[:i]) for i in range(len(sizes))]
    width = spots[-1] + LANES
    k = len(parts)

    def reduce_body(*refs):
        p_refs, tot_ref, rows, send_sems, recv_sems = refs[:k + 1], *refs[k + 1:]
        x, y, c = _my_place()
        me = _linear(x, y, c)
        rows[me] = jnp.zeros((1, width), F32)
        for i in range(k + 1):
            rows[me, :, spots[i]:spots[i] + sizes[i]] = p_refs[i][...]
        copies = []
        for rel in range(1, N_DEV):
            copies.append(pltpu.make_async_remote_copy(
                src_ref=rows.at[me], dst_ref=rows.at[me], send_sem=send_sems.at[rel - 1], recv_sem=recv_sems.at[rel - 1],
                device_id=_relative(x, y, c, rel), device_id_type=MESH))
        for cp in copies:
            cp.start()
        for cp in copies:
            cp.wait_recv()
        for cp in copies:
            cp.wait_send()
        total = rows[0]
        for d in range(1, N_DEV):
            total = total + rows[d]
        tot_ref[...] = total

    total = pl.pallas_call(
        reduce_body, name=name, out_shape=jax.ShapeDtypeStruct((1, width), F32),
        scratch_shapes=[pltpu.VMEM((N_DEV, 1, width), F32), pltpu.SemaphoreType.DMA((7,)), pltpu.SemaphoreType.DMA((7,))],
    )(*parts, loss_part)

    def adamw_body(*refs):
        tot_ref, w_refs, m_refs, v_refs, outs = refs[0], refs[1:k + 1], refs[k + 1:2 * k + 1], refs[2 * k + 1:3 * k + 1], refs[3 * k + 1:]
        for i in range(k):
            g = tot_ref[:, spots[i]:spots[i] + sizes[i]]
            outs[4 * i][...] = g
            outs[4 * i + 1][...], outs[4 * i + 2][...], outs[4 * i + 3][...] = _adamw_math(w_refs[i][...], g, m_refs[i][...], v_refs[i][...])
        outs[4 * k][...] = tot_ref[:, spots[k]:spots[k] + 1]

    out_shape = [jax.ShapeDtypeStruct((1, n), F32) for n in sizes[:k] for _ in range(4)] + [jax.ShapeDtypeStruct((1, 1), F32)]
    res = pl.pallas_call(adamw_body, name=name + "_adamw", out_shape=tuple(out_shape))(total, *ws, *ms, *vs)
    return [res[4 * i:4 * i + 4] for i in range(k)], res[4 * k]


def _pad_rows(a, rows):
    return jnp.pad(a, ((0, rows - a.shape[0]), (0, 0)))


def kernel(x, positions, norm_mix_g, w_in, b_fgate, q_norm_g, w_uq, kv_norm_g, w_ukv, fox_out_g, mla_out_g, w_o, norm_ffn_g, w_gate, w_up, w_down, final_norm_g, loss_target, m_norm_mix_g, m_w_in, m_b_fgate, m_q_norm_g, m_w_uq, m_kv_norm_g, m_w_ukv, m_fox_out_g, m_mla_out_g, m_w_o, m_norm_ffn_g, m_w_gate, m_w_up, m_w_down, m_final_norm_g, v_norm_mix_g, v_w_in, v_b_fgate, v_q_norm_g, v_w_uq, v_kv_norm_g, v_w_ukv, v_fox_out_g, v_mla_out_g, v_w_o, v_norm_ffn_g, v_w_gate, v_w_up, v_w_down, v_final_norm_g):
    bl, s, d = x.shape
    t = bl * s
    bh = bl * HEADS
    tq = _tile(s, 256)
    grp = s // LANES
    fw = HEADS * HEAD_DIM
    q_rank, kv_rank = w_uq.shape[1], w_ukv.shape[1]
    in_cols = w_in.shape[2]
    n_in = N_DEV * in_cols
    ff = N_DEV * w_gate.shape[2]
    half = MLA_ROPE // 2
    o_kvlat, o_krope, o_flogit = q_rank, q_rank + kv_rank, q_rank + kv_rank + LANES
    b_cols = -(-(o_flogit + HEADS) // LANES) * LANES

    tr = lambda w: jnp.transpose(w[0])
    in_rows = -(-in_cols // 16) * 16
    uq_rows = w_uq.shape[2] * q_rank // d
    ukv_rows = w_ukv.shape[2] * kv_rank // d
    pieces = [_pad_rows(tr(w_in), in_rows), _pad_rows(tr(w_uq).reshape(uq_rows, d), -(-uq_rows // 16) * 16),
              tr(w_ukv).reshape(ukv_rows, d), w_o[0], tr(w_gate), tr(w_up), w_down[0]]
    pieces = [p.astype(BF16) for p in pieces]
    offs = [0]
    for p in pieces:
        offs.append(offs[-1] + p.shape[0])
    legs = [(0, 1), (1, 5), (5, 7)]
    gathered = {}

    def full(i, rows):
        leg = next(n for n, (lo, hi) in enumerate(legs) if lo <= i < hi)
        base = offs[legs[leg][0]]
        return gathered[leg][:, offs[i] - base:offs[i] - base + rows]

    x2d = x.reshape(t, d)
    h1, gathered[0] = _rmsnorm(x2d, 0, d, norm_mix_g, BF16, "norm_mix", traffic=_Traffic("spread", pieces[0:1]))

    w_in_t = full(0, in_cols).reshape(n_in, d)
    n_qkv = 3 * fw
    w_in_a = w_in_t[:n_qkv].reshape(3, PAIRS, LANES, d).transpose(1, 0, 2, 3).reshape(n_qkv, d)
    lat0, rope0 = n_qkv + HEADS, n_qkv + HEADS + q_rank + kv_rank
    k_rep = jnp.broadcast_to(w_in_t[rope0:].reshape(2, 1, half, d), (2, 4, half, d)).reshape(LANES, d)
    w_in_b = jnp.concatenate([w_in_t[lat0:rope0], k_rep, w_in_t[n_qkv:lat0],
                              jnp.zeros((b_cols - o_flogit - HEADS, d), BF16)], axis=0)

    def per_head_rows(a):
        return a.reshape(bl, s, HEADS).transpose(0, 2, 1).reshape(bh, 1, s)

    proj_a =_matmul(h1, w_in_a, "nt", BF16, "proj_fox", tn=3 * LANES)
    proj_b = _matmul(h1, w_in_b, "nt", F32, "proj_mla", tn=b_cols)

    z = proj_b[:, o_flogit:o_flogit + HEADS].reshape(bl, s, HEADS).transpose(0, 2, 1).reshape(bh * grp, LANES)
    bcol = jnp.broadcast_to(b_fgate.reshape(1, HEADS, 1), (bl, HEADS, grp)).reshape(bh * grp, 1)
    c = _fgate(z, bcol, grp, "forget_gate")
    c_bias = c.reshape(bh, 1, s)
    fox_o, fox_lse, gathered[1] = _attn_fwd((proj_a,), c_bias, HEAD_DIM ** -0.5, bl, s, tq, "fox_attention",
                                            traffic=_Traffic("spread", pieces[legs[1][0]:legs[1][1]]))
    w_uq_h = full(1, uq_rows).reshape(HEADS, MLA_QK, q_rank)
    w_uq_pe = jnp.concatenate([w_uq_h[:, HEAD_DIM:HEAD_DIM + half].reshape(2, 1, 4 * half, q_rank),
                               w_uq_h[:, HEAD_DIM + half:].reshape(2, 1, 4 * half, q_rank)], axis=1).reshape(2 * LANES, q_rank)
    w_uq_p = jnp.concatenate([w_uq_h[:, :HEAD_DIM].reshape(fw, q_rank), w_uq_pe], axis=0)
    w_ukv_p = full(2, ukv_rows).reshape(PAIRS, 2, 2, HEAD_DIM, kv_rank).transpose(0, 2, 1, 3, 4).reshape(2 * fw, kv_rank)
    w_o_f = full(3, w_o.shape[1]).reshape(-1, d)
    w_gate_t = full(4, ff // N_DEV).reshape(ff, d)

    qn = _rmsnorm(proj_b, 0, q_rank, q_norm_g, BF16, "norm_q")
    kvn = _rmsnorm(proj_b, o_kvlat // kv_rank, kv_rank, kv_norm_g, BF16, "norm_kv")
    q_raw = _matmul(qn, w_uq_p, "nt", F32, "up_q", tn=fw + 2 * LANES)
    kv_all = _matmul(kvn, w_ukv_p, "nt", BF16, "up_kv")
    inv_freq = ROPE_THETA ** (-jnp.arange(0, MLA_ROPE, 2, dtype=F32) / MLA_ROPE)
    ang = positions.astype(F32).reshape(t, 1) * inv_freq[None, :]
    cos4, sin4 = jnp.tile(jnp.cos(ang), (1, 4)), jnp.tile(jnp.sin(ang), (1, 4))
    rope_cos, rope_sin = jnp.concatenate([cos4, cos4], axis=1), jnp.concatenate([-sin4, sin4], axis=1)
    q_all, kpe = _rope_fwd(q_raw, fw, proj_b, o_krope // LANES, rope_cos, rope_sin, "rope")
    mla_ops = (q_all, kv_all, kpe)
    mla_o, mla_lse, gathered[2] = _attn_fwd(mla_ops, None, MLA_QK ** -0.5, bl, s, tq, "mla_attention",
                                            traffic=_Traffic("spread", pieces[legs[2][0]:legs[2][1]]))
    w_up_t, w_down_f = full(5, ff // N_DEV).reshape(ff, d), full(6, ff // N_DEV).reshape(ff, d)

    cat = _out_norm(fox_o, mla_o, fox_out_g, mla_out_g, "norm_out")
    x1 = _matmul(cat, w_o_f, "nn", F32, "proj_out", res=x2d)
    h2 = _rmsnorm(x1, 0, d, norm_ffn_g, BF16, "norm_ffn")
    gate, up, act = _ffn_up(h2, w_gate_t, w_up_t, "ffn_gate_up")
    x2 = _matmul(act, w_down_f, "nn", F32, "ffn_down", res=x1)
    dx2, dg_final, loss_part, dx2_b = _loss_bwd(x2, loss_target.reshape(t, d), final_norm_g.reshape(1, d), "final_norm_loss")

    d_gate, d_up = _ffn_down_bwd(dx2_b, w_down_f, gate, up, "d_ffn_down")
    dw_down = _matmul(act, dx2_b, "tn", BF16, "dw_down", tm=ff // 2, tn=d, tk=1024)
    dh2 = _matmul(d_up, w_up_t, "nn", F32, "d_ffn_up", res=_matmul(d_gate, w_gate_t, "nn", F32, "d_ffn_gate"))
    dw_gate = _matmul(d_gate, h2, "tn", BF16, "dw_gate", tm=ff // 2, tn=d, tk=1024)
    dw_up = _matmul(d_up, h2, "tn", BF16, "dw_up", tm=ff // 2, tn=d, tk=1024)
    dx1, dg_ffn, dx1_b = _rmsnorm_bwd(x1, 0, d, norm_ffn_g, dh2, "d_norm_ffn", res=dx2, mxu_copy=True)
    dcat = _matmul(dx1_b, w_o_f, "nt", F32, "d_proj_out")
    dw_o = _matmul(cat, dx1_b, "tn", BF16, "dw_o", tn=d, tk=1024)
    d_fox_o, d_mla_o, fox_delta, mla_delta, dg_fox, dg_mla = _out_norm_bwd(fox_o, mla_o, fox_out_g, mla_out_g, dcat, "d_norm_out")

    per_dev = lambda a: a.reshape(N_DEV, -1, d)
    late_grads = [per_dev(dw_o), per_dev(dw_gate), per_dev(dw_up), per_dev(dw_down)]
    dproj_a, dc, g_late = _attn_bwd((proj_a,), d_fox_o, fox_lse, per_head_rows(fox_delta),
                                    c_bias, HEAD_DIM ** -0.5, bl, s, tq, "d_fox_attention", traffic=_Traffic("swap", late_grads))
    dz, db_fgate = _fgate_bwd(z, bcol, dc.reshape(bh * grp, LANES), grp, "d_forget_gate")
    d_flogit = dz.reshape(bl, HEADS, s).transpose(0, 2, 1).reshape(t, HEADS)

    dq_nope, dkv_all, dq_pe, dk_pe = _attn_bwd(mla_ops, d_mla_o, mla_lse, per_head_rows(mla_delta),
                                               None, MLA_QK ** -0.5, bl, s, tq, "d_mla_attention")
    dq_rot, dk_rot = _rope_bwd(dq_pe, dk_pe, rope_cos, rope_sin, "d_rope")
    dqn = _matmul(dq_rot, w_uq_p[fw:], "nn", F32, "d_up_q_rope", res=_matmul(dq_nope, w_uq_p[:fw], "nn", F32, "d_up_q_nope"))
    dw_uq_nope = _matmul(dq_nope, qn, "tn", BF16, "dw_uq_nope", tn=q_rank, tk=1024)
    dw_uq_pe = _matmul(dq_rot, qn, "tn", BF16, "dw_uq_rope", tn=q_rank, tk=1024)
    dq_lat, dg_q = _rmsnorm_bwd(proj_b, 0, q_rank, q_norm_g, dqn, "d_norm_q")
    dkvn = _matmul(dkv_all, w_ukv_p, "nn", F32, "d_up_kv")
    dw_ukv_p = _matmul(dkv_all, kvn, "tn", BF16, "dw_ukv", tn=kv_rank, tk=1024)
    dkv_lat, dg_kv = _rmsnorm_bwd(proj_b, o_kvlat // kv_rank, kv_rank, kv_norm_g, dkvn, "d_norm_kv")

    dproj_b = jnp.concatenate([dq_lat.astype(BF16), dkv_lat.astype(BF16), dk_rot.astype(BF16), d_flogit.astype(BF16),
                               jnp.zeros((t, b_cols - o_flogit - HEADS), BF16)], axis=1)
    dw_in_a = _matmul(dproj_a, h1, "tn", BF16, "dw_in_fox", tn=d, tk=1024)
    dw_in_b = _matmul(dproj_b, h1, "tn", F32, "dw_in_mla", tm=b_cols, tn=d, tk=1024)

    dw_krope = dw_in_b[o_krope:o_flogit].reshape(2, 4, half, d).sum(axis=1).reshape(MLA_ROPE, d)
    dw_in_t = jnp.concatenate([dw_in_a.reshape(PAIRS, 3, LANES, d).transpose(1, 0, 2, 3).reshape(n_qkv, d),
                               dw_in_b[o_flogit:o_flogit + HEADS].astype(BF16), dw_in_b[:o_krope].astype(BF16),
                               dw_krope.astype(BF16)], axis=0)
    pad_dev = lambda a, rows: jnp.pad(a, ((0, 0), (0, rows - a.shape[1]), (0, 0)))
    dw_uq_pe5 = dw_uq_pe.reshape(2, 2, 4, half, q_rank)
    dw_uq_h = jnp.concatenate([dw_uq_nope.reshape(HEADS, HEAD_DIM, q_rank), dw_uq_pe5[:, 0].reshape(HEADS, half, q_rank),
                               dw_uq_pe5[:, 1].reshape(HEADS, half, q_rank)], axis=1)
    dw_ukv_h = dw_ukv_p.reshape(PAIRS, 2, 2, HEAD_DIM, kv_rank).transpose(0, 2, 1, 3, 4).reshape(HEADS, 2 * HEAD_DIM, kv_rank)
    n_last = 3
    last_grads = [pad_dev(per_dev(dw_in_t), pieces[0].shape[0]), pad_dev(per_dev(dw_uq_h), pieces[1].shape[0]), per_dev(dw_ukv_h)]
    dh1_fox, g_last = _matmul(dproj_a, w_in_a, "nn", F32, "d_proj_fox", traffic=_Traffic("swap", last_grads))
    dh1 = _matmul(dproj_b, w_in_b, "nn", F32, "d_proj_mla", res=dh1_fox)
    grad_x, dg_mix = _rmsnorm_bwd(x2d, 0, d, norm_mix_g, dh1, "d_norm_mix", res=dx1)
    g_last = _sum_blocks(g_last, "sum_last_grads")
    g_late = _sum_blocks(g_late, "sum_late_grads")

    def mine(i, rows):
        src, base = (g_last, 0) if i < n_last else (g_late, offs[n_last])
        return src[offs[i] - base:offs[i] - base + rows]

    big = [
        ("w_in", w_in, m_w_in, v_w_in, mine(0, in_cols), True),
        ("w_uq", w_uq, m_w_uq, v_w_uq, mine(1, uq_rows).reshape(-1, q_rank), True),
        ("w_ukv", w_ukv, m_w_ukv, v_w_ukv, mine(2, ukv_rows).reshape(-1, kv_rank), True),
        ("w_o", w_o, m_w_o, v_w_o, mine(3, w_o.shape[1]), False),
        ("w_gate", w_gate, m_w_gate, v_w_gate, mine(4, ff // N_DEV), True),
        ("w_up", w_up, m_w_up, v_w_up, mine(5, ff // N_DEV), True),
        ("w_down", w_down, m_w_down, v_w_down, mine(6, ff // N_DEV), False),
    ]
    out = {}
    for nm, w, m, v, g, transposed in big:
        lay = (lambda a: a[0].T) if transposed else (lambda a: a[0])
        back = (lambda a: a.T[None]) if transposed else (lambda a: a[None])
        dl, new_m, new_v = _adamw(lay(w), g, lay(m), lay(v), "adamw_" + nm)
        out[nm] = (back(g), back(dl), back(new_m), back(new_v))

    smalls = [("norm_mix_g", norm_mix_g, m_norm_mix_g, v_norm_mix_g, dg_mix),
              ("b_fgate", b_fgate, m_b_fgate, v_b_fgate, db_fgate.reshape(1, HEADS)),
              ("q_norm_g", q_norm_g, m_q_norm_g, v_q_norm_g, dg_q),
              ("kv_norm_g", kv_norm_g, m_kv_norm_g, v_kv_norm_g, dg_kv),
              ("fox_out_g", fox_out_g, m_fox_out_g, v_fox_out_g, dg_fox),
              ("mla_out_g", mla_out_g, m_mla_out_g, v_mla_out_g, dg_mla),
              ("norm_ffn_g", norm_ffn_g, m_norm_ffn_g, v_norm_ffn_g, dg_ffn),
              ("final_norm_g", final_norm_g, m_final_norm_g, v_final_norm_g, dg_final)]
    flat = lambda a: a.reshape(1, -1)
    results, loss = _small_all_reduce_adamw([e[4] for e in smalls], loss_part, [flat(e[1]) for e in smalls],
                                            [flat(e[2]) for e in smalls], [flat(e[3]) for e in smalls], "reduce_small_adamw")
    for (nm, w, _, _, _), res in zip(smalls, results):
        out[nm] = tuple(a.reshape(w.shape) for a in res)
    loss = loss[0, 0]

    order = ["norm_mix_g", "w_in", "b_fgate", "q_norm_g", "w_uq", "kv_norm_g", "w_ukv", "fox_out_g", "mla_out_g", "w_o",
             "norm_ffn_g", "w_gate", "w_up", "w_down", "final_norm_g"]
    return (loss, grad_x.reshape(bl, s, d), *[out[n][0] for n in order], *[out[n][1] for n in order],
            *[out[n][2] for n in order], *[out[n][3] for n in order])
```

```python
import math

import jax
import jax.numpy as jnp
from jax import lax
from jax.experimental import pallas as pl
from jax.experimental.pallas import tpu as pltpu

F32 = jnp.float32
BF16 = jnp.bfloat16
MESH = pl.DeviceIdType.MESH

N_DEV = 8
HEADS = 8
HEAD_DIM = 64
PAIRS = HEADS // 2
MLA_ROPE = 32
MLA_QK = HEAD_DIM + MLA_ROPE
ROPE_THETA = 10000.0
NORM_EPS = 1e-6
ADAM_LR, ADAM_B1, ADAM_B2, ADAM_EPS, ADAM_WD, ADAM_STEP = 0.001, 0.9, 0.999, 1e-08, 0.01, 10

LANES = 128
MASKED = -1e30
VMEM_LIMIT = 48 * 1024 * 1024

_DIMS = {"nn": (((1,), (0,)), ((), ())), "nt": (((1,), (1,)), ((), ())), "tn": (((0,), (0,)), ((), ()))}


def _params(*sem):
    return pltpu.CompilerParams(dimension_semantics=sem, vmem_limit_bytes=VMEM_LIMIT)


def _dot(a, b, mode):
    return lax.dot_general(a.astype(BF16), b.astype(BF16), _DIMS[mode], preferred_element_type=F32)


def _tile(n, pref, unit=8):
    if n <= pref:
        return n
    t = pref - pref % unit
    while n % t:
        t -= unit
    return t


def _log2(n):
    assert n & (n - 1) == 0
    return n.bit_length() - 1


def _matmul(a, b, mode, out_dtype, name, tm=512, tn=512, tk=None, res=None, traffic=None):
    if mode == "nn":
        (m, kd), n = a.shape, b.shape[1]
    elif mode == "nt":
        (m, kd), n = a.shape, b.shape[0]
    else:
        (kd, m), n = a.shape, b.shape[1]
    tm, tn = _tile(m, tm, LANES if mode == "tn" else 16), _tile(n, tn, LANES)
    tk = kd if tk is None else _tile(kd, tk, LANES)
    nk = kd // tk
    a_spec = pl.BlockSpec((tk, tm), lambda i, j, k: (k, i)) if mode == "tn" else pl.BlockSpec((tm, tk), lambda i, j, k: (i, k))
    b_spec = pl.BlockSpec((tn, tk), lambda i, j, k: (j, k)) if mode == "nt" else pl.BlockSpec((tk, tn), lambda i, j, k: (k, j))
    o_spec = pl.BlockSpec((tm, tn), lambda i, j, k: (i, j))
    has_res = res is not None
    n_carried = len(traffic.pieces) if traffic else 0
    grid = (m // tm, n // tn, nk)

    def body(*refs):
        a_ref, b_ref = refs[:2]
        r_ref = refs[2] if has_res else None
        n_in = 2 + has_res + n_carried
        o_ref = refs[n_in]
        step = [pl.program_id(axis) for axis in range(3)]
        if traffic:
            carried_in, carried_out, sems = refs[2 + has_res:n_in], refs[n_in + 1], refs[len(refs) - 3:]

            @pl.when((step[0] == 0) & (step[1] == 0) & (step[2] == 0))
            def _():
                traffic.start(carried_in, carried_out, *sems)

        def finish(acc):
            if has_res:
                acc = acc + r_ref[...]
            o_ref[...] = acc.astype(out_dtype)

        part = _dot(a_ref[...], b_ref[...], mode)
        if nk == 1:
            finish(part)
        else:
            acc_ref = refs[n_in + 1 + bool(traffic)]

            @pl.when(step[2] == 0)
            def _():
                acc_ref[...] = part

            @pl.when(step[2] > 0)
            def _():
                acc_ref[...] += part

            @pl.when(step[2] == nk - 1)
            def _():
                finish(acc_ref[...])

        if traffic:
            @pl.when((step[0] == grid[0] - 1) & (step[1] == grid[1] - 1) & (step[2] == nk - 1))
            def _():
                traffic.wait(carried_out, *sems)

    in_specs = [a_spec, b_spec] + ([o_spec] if has_res else [])
    out_specs, out_shape = [o_spec], [jax.ShapeDtypeStruct((m, n), out_dtype)]
    scratch = [pltpu.VMEM((tm, tn), F32)] if nk > 1 else []
    if traffic:
        in_specs += traffic.in_specs
        out_specs.append(traffic.out_spec)
        out_shape.append(traffic.out_shape)
        scratch += traffic.scratch
    out = pl.pallas_call(
        body, name=name, grid=grid, in_specs=in_specs, out_specs=tuple(out_specs), out_shape=tuple(out_shape),
        scratch_shapes=scratch,
        compiler_params=_params(*(("arbitrary",) * 3 if traffic else ("parallel", "parallel", "arbitrary"))),
    )(*([a, b] + ([res] if has_res else []) + (traffic.pieces if traffic else [])))
    return out if traffic else out[0]


def _rstd(x):
    return lax.rsqrt(jnp.mean(x * x, axis=-1, keepdims=True) + NORM_EPS)


def _norm_bwd(x, g, dy):
    r = _rstd(x)
    xh = x * r
    u = dy * g
    dx = r * (u - xh * jnp.mean(u * xh, axis=-1, keepdims=True))
    return dx, jnp.sum(dy * xh, axis=0, keepdims=True)


def _rmsnorm(x, col, width, g, out_dtype, name, traffic=None):
    t = x.shape[0]
    tm = _tile(t, 512)
    steps = t // tm
    n_carried = len(traffic.pieces) if traffic else 0

    def body(*refs):
        x_ref, g_ref, o_ref = refs[0], refs[1], refs[2 + n_carried]
        if traffic:
            carried_in, carried_out, sems = refs[2:2 + n_carried], refs[3 + n_carried], refs[4 + n_carried:]

            @pl.when(pl.program_id(0) == 0)
            def _():
                traffic.start(carried_in, carried_out, *sems)

        xv = x_ref[...]
        o_ref[...] = ((xv * _rstd(xv)) * g_ref[...]).astype(out_dtype)
        if traffic:
            @pl.when(pl.program_id(0) == steps - 1)
            def _():
                traffic.wait(carried_out, *sems)

    in_specs = [pl.BlockSpec((tm, width), lambda i: (i, col)), pl.BlockSpec((1, width), lambda i: (0, 0))]
    out_specs = [pl.BlockSpec((tm, width), lambda i: (i, 0))]
    out_shape = [jax.ShapeDtypeStruct((t, width), out_dtype)]
    if traffic:
        in_specs += traffic.in_specs
        out_specs.append(traffic.out_spec)
        out_shape.append(traffic.out_shape)
    out = pl.pallas_call(
        body, name=name, grid=(steps,), in_specs=in_specs, out_specs=tuple(out_specs), out_shape=tuple(out_shape),
        scratch_shapes=traffic.scratch if traffic else [],
        compiler_params=_params("arbitrary" if traffic else "parallel"),
    )(x, g, *(traffic.pieces if traffic else []))
    return out if traffic else out[0]


def _out_norm(fo, mo, gf, gm, name):
    t, w = fo.shape
    tm = _tile(t, 512)
    row = pl.BlockSpec((tm, w), lambda i: (i, 0))
    vec = pl.BlockSpec((1, w), lambda i: (0, 0))

    def body(fo_ref, mo_ref, gf_ref, gm_ref, o_ref):
        f, m = fo_ref[...], mo_ref[...]
        o_ref[:, :w] = ((f * _rstd(f)) * gf_ref[...]).astype(BF16)
        o_ref[:, w:] = ((m * _rstd(m)) * gm_ref[...]).astype(BF16)

    return pl.pallas_call(
        body, name=name, grid=(t // tm,), in_specs=[row, row, vec, vec],
        out_specs=pl.BlockSpec((tm, 2 * w), lambda i: (i, 0)),
        out_shape=jax.ShapeDtypeStruct((t, 2 * w), BF16),
        compiler_params=_params("parallel"),
    )(fo, mo, gf, gm)


def _split3(x):
    hi = x.astype(BF16)
    r1 = x - hi.astype(F32)
    mid = r1.astype(BF16)
    lo = (r1 - mid.astype(F32)).astype(BF16)
    return hi, mid, lo


def _dot_x01(x, m01):
    hi, mid, lo = _split3(x)
    d = lambda p: lax.dot_general(p, m01, _DIMS["nn"], preferred_element_type=F32)
    return (d(lo) + d(mid)) + d(hi)


def _dot_01x(m01, x):
    hi, mid, lo = _split3(x)
    d = lambda p: lax.dot_general(m01, p, _DIMS["nn"], preferred_element_type=F32)
    return (d(lo) + d(mid)) + d(hi)


def _rows_matmul(terms, rows_in, vecs_in, epilogue, rows_out, sums_out, name, tm=512):
    t = terms[0][0].shape[0]
    tm = _tile(t, tm, 16)
    n_terms, n_rows, n_vecs = len(terms), len(rows_in), len(vecs_in)

    def body(*refs):
        acc = None
        for i, (_, _, mode) in enumerate(terms):
            part = _dot(refs[2 * i][...], refs[2 * i + 1][...], mode)
            acc = part if acc is None else acc + part
        at = 2 * n_terms
        row_vals, sum_vals = epilogue(acc, [r[...] for r in refs[at:at + n_rows]],
                                      [r[...] for r in refs[at + n_rows:at + n_rows + n_vecs]])
        at += n_rows + n_vecs
        for ref, val, (_, dtype) in zip(refs[at:], row_vals, rows_out):
            ref[...] = val.astype(dtype)
        sum_refs = refs[at + len(rows_out):]

        @pl.when(pl.program_id(0) == 0)
        def _():
            for ref in sum_refs:
                ref[...] = jnp.zeros_like(ref)

        for ref, val in zip(sum_refs, sum_vals):
            ref[...] += val

    rows = lambda w: pl.BlockSpec((tm, w), lambda i: (i, 0))
    whole = lambda a: pl.BlockSpec(a.shape, lambda i: (0, 0))
    in_specs, args = [], []
    for a, b, _ in terms:
        in_specs += [rows(a.shape[1]), whole(b)]
        args += [a, b]
    for r in rows_in:
        arr, col, w = r if isinstance(r, tuple) else (r, 0, r.shape[1])
        in_specs.append(pl.BlockSpec((tm, w), lambda i, col=col: (i, col)))
        args.append(arr)
    in_specs += [whole(v) for v in vecs_in]
    args += list(vecs_in)
    return pl.pallas_call(
        body, name=name, grid=(t // tm,), in_specs=in_specs,
        out_specs=tuple([rows(w) for w, _ in rows_out] + [pl.BlockSpec((1, w), lambda i: (0, 0)) for w in sums_out]),
        out_shape=tuple([jax.ShapeDtypeStruct((t, w), dt) for w, dt in rows_out] + [jax.ShapeDtypeStruct((1, w), F32) for w in sums_out]),
        compiler_params=_params("arbitrary"),
    )(*args)


def _residual_norm(acc, rows, vecs):
    x1 = rows[0] + acc
    return [x1, (x1 * _rstd(x1)) * vecs[0]], []


def _residual_loss_bwd(acc, rows, vecs):
    x2, gv = rows[0] + acc, vecs[0]
    diff = (x2 * _rstd(x2)) * gv - rows[1]
    dx, dg = _norm_bwd(x2, gv, diff / x2.shape[1])
    return [dx, dx], [dg, 0.5 * jnp.sum(jnp.mean(diff * diff, axis=-1, keepdims=True), axis=0, keepdims=True)]


def _norm_bwd_residual(acc, rows, vecs):
    dy = acc + rows[2] if len(rows) > 2 else acc
    dx, dg = _norm_bwd(rows[0], vecs[0], dy)
    if len(rows) > 1:
        dx = dx + rows[1]
    return [dx, dx], [dg]


def _out_norm_bwd(acc, rows, vecs):
    (f, m), w = rows, rows[0].shape[1]
    nh = w // HEAD_DIM
    lane_head = lax.shift_right_logical(lax.broadcasted_iota(jnp.int32, (w, nh), 0), _log2(HEAD_DIM))
    sel = (lane_head == lax.broadcasted_iota(jnp.int32, (w, nh), 1)).astype(BF16)
    dfo, dgf = _norm_bwd(f, vecs[0], acc[:, :w])
    dmo, dgm = _norm_bwd(m, vecs[1], acc[:, w:])
    return [dfo, dmo, _dot_x01(dfo * f, sel), _dot_x01(dmo * m, sel)], [dgf, dgm]


def _ffn_up(h, wg_t, wu_t, name, tm=512, tf=1408):
    t, d = h.shape
    f = wg_t.shape[0]
    tm, tf = _tile(t, tm, 16), _tile(f, tf, LANES)
    tok = pl.BlockSpec((tm, tf), lambda i, j: (i, j))
    wt = pl.BlockSpec((tf, d), lambda i, j: (j, 0))

    def body(h_ref, wg_ref, wu_ref, g_ref, u_ref, a_ref):
        hv = h_ref[...]
        g, u = _dot(hv, wg_ref[...], "nt"), _dot(hv, wu_ref[...], "nt")
        g_ref[...], u_ref[...] = g.astype(BF16), u.astype(BF16)
        a_ref[...] = ((g * jax.nn.sigmoid(g)) * u).astype(BF16)

    return pl.pallas_call(
        body, name=name, grid=(t // tm, f // tf), in_specs=[pl.BlockSpec((tm, d), lambda i, j: (i, 0)), wt, wt],
        out_specs=(tok, tok, tok),
        out_shape=(jax.ShapeDtypeStruct((t, f), BF16), jax.ShapeDtypeStruct((t, f), BF16), jax.ShapeDtypeStruct((t, f), BF16)),
        compiler_params=_params("parallel", "parallel"),
    )(h, wg_t, wu_t)


def _ffn_down_bwd(dy, w_down, gate, up, name, tm=512, tf=1408):
    t, d = dy.shape
    f = w_down.shape[0]
    tm, tf = _tile(t, tm, 16), _tile(f, tf, LANES)
    tok = pl.BlockSpec((tm, tf), lambda i, j: (i, j))

    def body(dy_ref, w_ref, g_ref, u_ref, dg_ref, du_ref):
        da = _dot(dy_ref[...], w_ref[...], "nt")
        g = g_ref[...].astype(F32)
        sg = jax.nn.sigmoid(g)
        dg_ref[...] = (da * u_ref[...].astype(F32) * (sg * (1.0 + g * (1.0 - sg)))).astype(BF16)
        du_ref[...] = (da * (g * sg)).astype(BF16)

    return pl.pallas_call(
        body, name=name, grid=(t // tm, f // tf),
        in_specs=[pl.BlockSpec((tm, d), lambda i, j: (i, 0)), pl.BlockSpec((tf, d), lambda i, j: (j, 0)), tok, tok],
        out_specs=(tok, tok),
        out_shape=(jax.ShapeDtypeStruct((t, f), BF16), jax.ShapeDtypeStruct((t, f), BF16)),
        compiler_params=_params("parallel", "parallel"),
    )(dy, w_down, gate, up)


def _chunk_scan_mats(rows, grp, reverse):
    ii = lax.broadcasted_iota(jnp.int32, (LANES, LANES), 0)
    jj = lax.broadcasted_iota(jnp.int32, (LANES, LANES), 1)
    within = ((ii >= jj) if reverse else (ii <= jj)).astype(BF16)
    ones = jnp.ones((LANES, LANES), BF16)
    ri = lax.broadcasted_iota(jnp.int32, (rows, rows), 0)
    rj = lax.broadcasted_iota(jnp.int32, (rows, rows), 1)
    sh = _log2(grp)
    same = lax.shift_right_logical(ri, sh) == lax.shift_right_logical(rj, sh)
    across = (same & ((rj > ri) if reverse else (rj < ri))).astype(BF16)
    return within, ones, across


def _running_sum(v, mats):
    within, ones, across = mats
    return _dot_x01(v, within) + _dot_01x(across, _dot_x01(v, ones))


def _fgate(z, bcol, grp, name):
    rows = z.shape[0]

    def body(z_ref, b_ref, c_ref):
        zz = z_ref[...] + b_ref[...]
        log_f = jnp.minimum(zz, 0.0) - jnp.log1p(jnp.exp(-jnp.abs(zz)))
        c_ref[...] = _running_sum(log_f, _chunk_scan_mats(rows, grp, False))

    return pl.pallas_call(body, name=name, out_shape=jax.ShapeDtypeStruct(z.shape, F32),
                          compiler_params=pltpu.CompilerParams(vmem_limit_bytes=VMEM_LIMIT))(z, bcol)


def _fgate_bwd(z, bcol, dc, grp, name):
    rows = z.shape[0]

    def body(z_ref, b_ref, dc_ref, dz_ref, db_ref):
        zz = z_ref[...] + b_ref[...]
        dz = _running_sum(dc_ref[...], _chunk_scan_mats(rows, grp, True)) * jax.nn.sigmoid(-zz)
        dz_ref[...] = dz
        head = lax.shift_right_logical(lax.broadcasted_iota(jnp.int32, (HEADS, rows), 1), _log2(grp)) & (HEADS - 1)
        sel = (head == lax.broadcasted_iota(jnp.int32, (HEADS, rows), 0)).astype(BF16)
        db_ref[...] = jnp.sum(_dot_01x(sel, dz), axis=1, keepdims=True)

    return pl.pallas_call(
        body, name=name,
        out_shape=(jax.ShapeDtypeStruct(z.shape, F32), jax.ShapeDtypeStruct((HEADS, 1), F32)),
        compiler_params=pltpu.CompilerParams(vmem_limit_bytes=VMEM_LIMIT),
    )(z, bcol, dc)


def _rotate(x, cs, sn_signed):
    return x * cs + pltpu.roll(x, LANES // 2, axis=1) * sn_signed


def _rope_fwd(q_raw, nope, proj_b, k_col, cs, sn, name):
    t, qw = q_raw.shape
    tm = _tile(t, 512)
    row = pl.BlockSpec((tm, LANES), lambda i: (i, 0))

    def body(q_ref, k_ref, c_ref, s_ref, qo_ref, ko_ref):
        c, s = c_ref[...], s_ref[...]
        qo_ref[:, :nope] = q_ref[:, :nope].astype(BF16)
        for off in range(nope, qw, LANES):
            qo_ref[:, off:off + LANES] = _rotate(q_ref[:, off:off + LANES], c, s).astype(BF16)
        ko_ref[...] = _rotate(k_ref[...], c, s).astype(BF16)

    return pl.pallas_call(
        body, name=name, grid=(t // tm,),
        in_specs=[pl.BlockSpec((tm, qw), lambda i: (i, 0)), pl.BlockSpec((tm, LANES), lambda i: (i, k_col)), row, row],
        out_specs=(pl.BlockSpec((tm, qw), lambda i: (i, 0)), row),
        out_shape=(jax.ShapeDtypeStruct((t, qw), BF16), jax.ShapeDtypeStruct((t, LANES), BF16)),
        compiler_params=_params("parallel"),
    )(q_raw, proj_b, cs, sn)


def _rope_bwd(dq_pe, dk_pe, cs, sn, name):
    t, qw = dq_pe.shape
    tm = _tile(t, 512)
    row = pl.BlockSpec((tm, LANES), lambda i: (i, 0))
    wide = pl.BlockSpec((tm, qw), lambda i: (i, 0))

    def body(q_ref, k_ref, c_ref, s_ref, qo_ref, ko_ref):
        c, s = c_ref[...], -s_ref[...]
        for off in range(0, qw, LANES):
            qo_ref[:, off:off + LANES] = _rotate(q_ref[:, off:off + LANES], c, s).astype(BF16)
        ko_ref[...] = _rotate(k_ref[...], c, s)

    return pl.pallas_call(
        body, name=name, grid=(t // tm,), in_specs=[wide, row, row, row], out_specs=(wide, row),
        out_shape=(jax.ShapeDtypeStruct((t, qw), BF16), jax.ShapeDtypeStruct((t, LANES), F32)),
        compiler_params=_params("parallel"),
    )(dq_pe, dk_pe, cs, sn)


def _lane_masks(pair, h, pe):
    lane = lax.broadcasted_iota(jnp.int32, (1, LANES), 1)
    in_head = lax.shift_right_logical(lane, _log2(HEAD_DIM)) == h
    in_rope = ((lax.shift_right_logical(lane, _log2(MLA_ROPE // 2)) & 3) == ((2 * pair + h) & 3)) if pe else None
    return in_head, in_rope


def _keep(mask, v):
    return jnp.where(mask, v, jnp.zeros_like(v))


def _to_row(col):
    n = col.shape[0]
    eye = lax.broadcasted_iota(jnp.int32, (n, n), 0) == lax.broadcasted_iota(jnp.int32, (n, n), 1)
    return jnp.sum(jnp.where(eye, col, 0.0), axis=0, keepdims=True)


def _to_col(row):
    n = row.shape[1]
    eye = lax.broadcasted_iota(jnp.int32, (n, n), 0) == lax.broadcasted_iota(jnp.int32, (n, n), 1)
    return jnp.sum(jnp.where(eye, row, 0.0), axis=1, keepdims=True)


def _first_step():
    return (pl.program_id(0) == 0) & (pl.program_id(1) == 0)


def _last_step(n0, n1):
    return (pl.program_id(0) == n0 - 1) & (pl.program_id(1) == n1 - 1)


def _attn_fwd(ops, bias, scale, bl, s, tq, name, traffic=None):
    pe = len(ops) == 3
    has_bias = bias is not None
    exact_scale = math.frexp(scale)[0] == 0.5
    nq = s // tq
    t = bl * s
    n_carried = len(traffic.pieces) if traffic else 0

    def body(*refs):
        sems = refs[len(refs) - 3:] if traffic else ()
        if pe:
            q_ref, qpe_ref, kv_ref, kpe_ref = refs[:4]
            n_in = 4
            q_at = lambda r0, r1: q_ref[r0:r1, :]
            v_at = lambda r0, r1: kv_ref[r0:r1, LANES:]
            kcat = refs[len(refs) - 1 - len(sems)]
            kcat[:, :LANES] = kv_ref[:, :LANES]
            kcat[:, LANES:] = kpe_ref[...]
            k_at = lambda r0, r1: kcat[r0:r1, :]
        else:
            qkv_ref = refs[0]
            n_in = 1
            q_at = lambda r0, r1: qkv_ref[r0:r1, :LANES]
            k_at = lambda r0, r1: qkv_ref[r0:r1, LANES:2 * LANES]
            v_at = lambda r0, r1: qkv_ref[r0:r1, 2 * LANES:]
        if has_bias:
            c_ref = refs[n_in]
            n_in += 1
        carried_in = refs[n_in:n_in + n_carried]
        n_in += n_carried
        o_ref, lse_ref = refs[n_in:n_in + 2]
        if traffic:
            carried_out = refs[n_in + 2]

            @pl.when(_first_step())
            def _():
                traffic.start(carried_in, carried_out, *sems)

        pair = pl.program_id(1)
        causal = lax.broadcasted_iota(jnp.int32, (tq, tq), 1) <= lax.broadcasted_iota(jnp.int32, (tq, tq), 0)
        o_ref[...] = jnp.zeros_like(o_ref)

        def head(h, carry):
            in_head, in_rope = _lane_masks(pair, h, pe)
            for i in range(nq):
                r0, r1 = i * tq, (i + 1) * tq
                qm = _keep(in_head, q_at(r0, r1))
                if pe:
                    qm = jnp.concatenate([qm, _keep(in_rope, qpe_ref[r0:r1, :])], axis=1)
                if exact_scale:
                    qm = qm * scale

                def logits(k0, k1):
                    sc = _dot(qm, k_at(k0, k1), "nt")
                    if not exact_scale:
                        sc = sc * scale
                    if has_bias:
                        sc = sc - c_ref[h, :, k0:k1]
                    return sc

                sd = jnp.where(causal, logits(r0, r1), MASKED)
                m = jnp.max(sd, axis=1, keepdims=True)
                if i:
                    so = logits(0, r0)
                    m = jnp.maximum(m, jnp.max(so, axis=1, keepdims=True))
                pd = jnp.exp(sd - m)
                l = jnp.sum(pd, axis=1, keepdims=True)
                acc = _dot(pd, v_at(r0, r1), "nn")
                if i:
                    po = jnp.exp(so - m)
                    l = l + jnp.sum(po, axis=1, keepdims=True)
                    acc = acc + _dot(po, v_at(0, r0), "nn")
                o_ref[r0:r1, :] = jnp.where(in_head, acc / l, o_ref[r0:r1, :])
                lse = _to_row(m + jnp.log(l))
                lse_ref[h, :, r0:r1] = lse + c_ref[h, :, r0:r1] if has_bias else lse
            return carry

        lax.fori_loop(0, 2, head, 0)
        if traffic:
            @pl.when(_last_step(bl, PAIRS))
            def _():
                traffic.wait(carried_out, *sems)

    seq = lambda w, col: pl.BlockSpec((s, w), col)
    if pe:
        in_specs = [seq(LANES, lambda b, p: (b, p)), seq(LANES, lambda b, p: (b, PAIRS + p // 2)),
                    seq(2 * LANES, lambda b, p: (b, p)), seq(LANES, lambda b, p: (b, 0))]
        args = [ops[0], ops[0], ops[1], ops[2]]
        scratch = [pltpu.VMEM((s, 2 * LANES), BF16)]
    else:
        in_specs = [seq(3 * LANES, lambda b, p: (b, p))]
        args = [ops[0]]
        scratch = []
    per_head_row = pl.BlockSpec((2, 1, s), lambda b, p: (b * PAIRS + p, 0, 0))
    if has_bias:
        in_specs.append(per_head_row)
        args.append(bias)
    out_specs = [seq(LANES, lambda b, p: (b, p)), per_head_row]
    out_shape = [jax.ShapeDtypeStruct((t, HEADS * HEAD_DIM), F32), jax.ShapeDtypeStruct((bl * HEADS, 1, s), F32)]
    if traffic:
        in_specs += traffic.in_specs
        args += traffic.pieces
        out_specs.append(traffic.out_spec)
        out_shape.append(traffic.out_shape)
        scratch += traffic.scratch
    return pl.pallas_call(
        body, name=name, grid=(bl, PAIRS), in_specs=in_specs, out_specs=tuple(out_specs), out_shape=tuple(out_shape),
        scratch_shapes=scratch, compiler_params=_params(*(("arbitrary", "arbitrary") if traffic else ("parallel", "parallel"))),
    )(*args)


def _attn_bwd(ops, do, lse, delta, bias, scale, bl, s, tq, name, traffic=None):
    pe = len(ops) == 3
    has_bias = bias is not None
    nq = s // tq
    t = bl * s
    width = 2 * LANES if pe else LANES
    n_carried = len(traffic.pieces) if traffic else 0

    def body(*refs):
        if pe:
            q_ref, qpe_ref, kv_ref, kpe_ref = refs[:4]
            n_in = 4
            k_at = lambda r0, r1: kv_ref[r0:r1, :LANES]
            v_at = lambda r0, r1: kv_ref[r0:r1, LANES:]
        else:
            qkv_ref = refs[0]
            n_in = 1
            k_at = lambda r0, r1: qkv_ref[r0:r1, LANES:2 * LANES]
            v_at = lambda r0, r1: qkv_ref[r0:r1, 2 * LANES:]
        do_ref, lse_ref, dl_ref = refs[n_in:n_in + 3]
        n_in += 3
        if has_bias:
            c_ref = refs[n_in]
            n_in += 1
        carried_in = refs[n_in:n_in + n_carried]
        rest = refs[n_in + n_carried:]
        if traffic:
            rest, sems = rest[:-3], rest[-3:]
            carried_out = rest[4 if pe else 2]
            rest = rest[:4 if pe else 2] + rest[(4 if pe else 2) + 1:]

            @pl.when(_first_step())
            def _():
                traffic.start(carried_in, carried_out, *sems)

        if pe:
            dqn_ref, dkv_ref, dqpe_ref, dkpe_ref, dq_acc, qcat = rest
            qcat[:, :LANES] = q_ref[...]
            qcat[:, LANES:] = qpe_ref[...]
            q_at = lambda r0, r1: qcat[r0:r1, :]
            dkv_ref[...] = jnp.zeros_like(dkv_ref)
        else:
            dqkv_ref, dc_ref, dq_acc = rest
            q_at = lambda r0, r1: qkv_ref[r0:r1, :LANES]
            dqkv_ref[...] = jnp.zeros_like(dqkv_ref)
            dc_ref[...] = jnp.zeros_like(dc_ref)
        pair = pl.program_id(1)
        dq_acc[...] = jnp.zeros_like(dq_acc)
        causal = lax.broadcasted_iota(jnp.int32, (tq, tq), 1) >= lax.broadcasted_iota(jnp.int32, (tq, tq), 0)

        def head(h, carry):
            in_head, in_rope = _lane_masks(pair, h, pe)
            for j in range(nq):
                r0, r1 = j * tq, (j + 1) * tq
                kt = _keep(in_head, k_at(r0, r1))
                if pe:
                    kt = jnp.concatenate([kt, _keep(in_rope, kpe_ref[r0:r1, :])], axis=1)
                vt = _keep(in_head, v_at(r0, r1))
                ck = _to_col(c_ref[h, :, r0:r1]) if has_bias else None

                def block(q0, q1, diagonal):
                    qq, dd = q_at(q0, q1), do_ref[q0:q1, :]
                    st = _dot(kt, qq, "nt") * scale
                    if has_bias:
                        st = st + (c_ref[h, :, q0:q1] - ck)
                    if diagonal:
                        st = jnp.where(causal, st, MASKED)
                    pt = jnp.exp(st - lse_ref[h, :, q0:q1])
                    dst = pt * (_dot(vt, dd, "nt") - dl_ref[h, :, q0:q1])
                    dsb = (dst * scale).astype(BF16)
                    dq_acc[q0:q1, :] += _dot(dsb, kt, "tn")
                    if has_bias:
                        dc_ref[h, :, q0:q1] += jnp.sum(dst, axis=0, keepdims=True)
                    return _dot(pt, dd, "nn"), _dot(dsb, qq, "nn"), (jnp.sum(dst, axis=1, keepdims=True) if has_bias else None)

                dv_c, dk_c, cs = block(r0, r1, True)
                if r1 < s:
                    dv_o, dk_o, cs_o = block(r1, s, False)
                    dv_c, dk_c = dv_c + dv_o, dk_c + dk_o
                    cs = cs + cs_o if has_bias else None
                if pe:
                    dkv_ref[r0:r1, :LANES] = jnp.where(in_head, dk_c[:, :LANES].astype(BF16), dkv_ref[r0:r1, :LANES])
                    dkv_ref[r0:r1, LANES:] = jnp.where(in_head, dv_c.astype(BF16), dkv_ref[r0:r1, LANES:])
                    dkpe_ref[r0:r1, :] += _keep(in_rope, dk_c[:, LANES:])
                else:
                    dqkv_ref[r0:r1, LANES:2 * LANES] = jnp.where(in_head, dk_c.astype(BF16), dqkv_ref[r0:r1, LANES:2 * LANES])
                    dqkv_ref[r0:r1, 2 * LANES:] = jnp.where(in_head, dv_c.astype(BF16), dqkv_ref[r0:r1, 2 * LANES:])
                    dc_ref[h, :, r0:r1] -= _to_row(cs)
            return carry

        if pe:
            @pl.when(pair == 0)
            def _():
                dkpe_ref[...] = jnp.zeros_like(dkpe_ref)

            @pl.when(pair % 2 == 0)
            def _():
                dqpe_ref[...] = jnp.zeros_like(dqpe_ref)

        lax.fori_loop(0, 2, head, 0)
        if pe:
            dqn_ref[...] = dq_acc[:, :LANES].astype(BF16)
            dqpe_ref[...] += dq_acc[:, LANES:]
        else:
            dqkv_ref[:, :LANES] = dq_acc[...].astype(BF16)
        if traffic:
            @pl.when(_last_step(bl, PAIRS))
            def _():
                traffic.wait(carried_out, *sems)

    seq = lambda w, col: pl.BlockSpec((s, w), col)
    per_head_row = pl.BlockSpec((2, 1, s), lambda b, p: (b * PAIRS + p, 0, 0))
    if pe:
        in_specs = [seq(LANES, lambda b, p: (b, p)), seq(LANES, lambda b, p: (b, PAIRS + p // 2)),
                    seq(2 * LANES, lambda b, p: (b, p)), seq(LANES, lambda b, p: (b, 0))]
        args = [ops[0], ops[0], ops[1], ops[2]]
    else:
        in_specs = [seq(3 * LANES, lambda b, p: (b, p))]
        args = [ops[0]]
    in_specs += [seq(LANES, lambda b, p: (b, p)), per_head_row, per_head_row]
    args += [do, lse, delta]
    if has_bias:
        in_specs.append(per_head_row)
        args.append(bias)
    scratch = [pltpu.VMEM((s, width), F32)]
    if pe:
        out_specs = (seq(LANES, lambda b, p: (b, p)), seq(2 * LANES, lambda b, p: (b, p)),
                     seq(LANES, lambda b, p: (b, p // 2)), seq(LANES, lambda b, p: (b, 0)))
        out_shape = (jax.ShapeDtypeStruct((t, PAIRS * LANES), BF16), jax.ShapeDtypeStruct((t, PAIRS * 2 * LANES), BF16),
                     jax.ShapeDtypeStruct((t, 2 * LANES), F32), jax.ShapeDtypeStruct((t, LANES), F32))
        scratch.append(pltpu.VMEM((s, 2 * LANES), BF16))
    else:
        out_specs = (seq(3 * LANES, lambda b, p: (b, p)), per_head_row)
        out_shape = (jax.ShapeDtypeStruct((t, PAIRS * 3 * LANES), BF16), jax.ShapeDtypeStruct((bl * HEADS, 1, s), F32))
    if traffic:
        in_specs += traffic.in_specs
        args += traffic.pieces
        out_specs += (traffic.out_spec,)
        out_shape += (traffic.out_shape,)
        scratch += traffic.scratch
    return pl.pallas_call(
        body, name=name, grid=(bl, PAIRS), in_specs=in_specs, out_specs=out_specs, out_shape=out_shape,
        scratch_shapes=scratch, compiler_params=_params("arbitrary" if traffic else "parallel", "arbitrary"),
    )(*args)


def _my_place():
    return lax.axis_index("x"), lax.axis_index("y"), lax.axis_index("c")


def _flip(p, bit):
    return 1 - p if bit else p


def _relative(x, y, c, k):
    return _flip(x, k & 4), _flip(y, k & 2), _flip(c, k & 1)


def _linear(x, y, c):
    return 4 * x + 2 * y + c


class _Traffic:
    def __init__(self, kind, pieces):
        self.kind, self.pieces = kind, list(pieces)
        self.rows = [p.shape[-2] for p in self.pieces]
        self.starts = [sum(self.rows[:i]) for i in range(len(self.rows))]
        anywhere = pl.BlockSpec(memory_space=pl.ANY)
        self.in_specs = [anywhere] * len(self.pieces)
        self.out_spec = anywhere
        self.out_shape = jax.ShapeDtypeStruct((N_DEV, sum(self.rows), self.pieces[0].shape[-1]), self.pieces[0].dtype)
        self.scratch = [pltpu.SemaphoreType.DMA((7,)), pltpu.SemaphoreType.DMA((7,)), pltpu.SemaphoreType.DMA(())]

    def start(self, p_refs, out_ref, send_sems, recv_sems, local_sem):
        x, y, c = _my_place()
        me = _linear(x, y, c)
        mine = lambda i, dev: p_refs[i] if self.kind == "spread" else p_refs[i].at[dev]
        landing = lambda i: out_ref.at[me, pl.ds(self.starts[i], self.rows[i])]
        for i in range(len(p_refs)):
            pltpu.make_async_copy(mine(i, me), landing(i), local_sem).start()
        for k in range(1, N_DEV):
            peer = _relative(x, y, c, k)
            for i in range(len(p_refs)):
                pltpu.make_async_remote_copy(
                    src_ref=mine(i, _linear(*peer)), dst_ref=landing(i),
                    send_sem=send_sems.at[k - 1], recv_sem=recv_sems.at[k - 1], device_id=peer, device_id_type=MESH).start()

    def wait(self, out_ref, send_sems, recv_sems, local_sem):
        x, y, c = _my_place()
        whole = out_ref.at[_linear(x, y, c)]
        for k in range(1, N_DEV):
            both = pltpu.make_async_remote_copy(
                src_ref=whole, dst_ref=whole, send_sem=send_sems.at[k - 1], recv_sem=recv_sems.at[k - 1],
                device_id=_relative(x, y, c, k), device_id_type=MESH)
            both.wait_recv()
            both.wait_send()
        pltpu.make_async_copy(whole, whole, local_sem).wait()


def _sum_blocks(parts, name):
    n, r, cdim = parts.shape
    tr = _tile(r, 640, 16)

    def body(p_ref, o_ref):
        acc = p_ref[0].astype(F32)
        for d in range(1, n):
            acc = acc + p_ref[d].astype(F32)
        o_ref[...] = acc

    return pl.pallas_call(
        body, name=name, grid=(r // tr,), in_specs=[pl.BlockSpec((n, tr, cdim), lambda i: (0, i, 0))],
        out_specs=pl.BlockSpec((tr, cdim), lambda i: (i, 0)), out_shape=jax.ShapeDtypeStruct((r, cdim), F32),
        compiler_params=_params("parallel"),
    )(parts)


def _adamw_math(w, g, m, v):
    m = ADAM_B1 * m + (1.0 - ADAM_B1) * g
    v = ADAM_B2 * v + (1.0 - ADAM_B2) * (g * g)
    m_hat = m / (1.0 - ADAM_B1 ** ADAM_STEP)
    v_hat = v / (1.0 - ADAM_B2 ** ADAM_STEP)
    delta = -ADAM_LR * (m_hat / (jnp.sqrt(v_hat) + ADAM_EPS) + ADAM_WD * w)
    return delta, m, v


def _adamw(w, g, m, v, name):
    def body(w_ref, g_ref, m_ref, v_ref, d_ref, nm_ref, nv_ref):
        d_ref[...], nm_ref[...], nv_ref[...] = _adamw_math(w_ref[...], g_ref[...], m_ref[...], v_ref[...])

    out = jax.ShapeDtypeStruct(w.shape, F32)
    return pl.pallas_call(body, name=name, out_shape=(out, out, out),
                          compiler_params=pltpu.CompilerParams(vmem_limit_bytes=VMEM_LIMIT))(w, g, m, v)


def _small_all_reduce_adamw(parts, loss_part, ws, ms, vs, name):
    sizes = [p.shape[1] for p in parts] + [1]
    spots = [sum(-(-n // LANES) * LANES for n in sizes[:i]) for i in range(len(sizes))]
    width = spots[-1] + LANES
    k = len(parts)

    def reduce_body(*refs):
        p_refs, tot_ref, rows, send_sems, recv_sems = refs[:k + 1], *refs[k + 1:]
        x, y, c = _my_place()
        me = _linear(x, y, c)
        rows[me] = jnp.zeros((1, width), F32)
        for i in range(k + 1):
            rows[me, :, spots[i]:spots[i] + sizes[i]] = p_refs[i][...]
        copies = []
        for rel in range(1, N_DEV):
            copies.append(pltpu.make_async_remote_copy(
                src_ref=rows.at[me], dst_ref=rows.at[me], send_sem=send_sems.at[rel - 1], recv_sem=recv_sems.at[rel - 1],
                device_id=_relative(x, y, c, rel), device_id_type=MESH))
        for cp in copies:
            cp.start()
        for cp in copies:
            cp.wait_recv()
        for cp in copies:
            cp.wait_send()
        total = rows[0]
        for d in range(1, N_DEV):
            total = total + rows[d]
        tot_ref[...] = total

    total = pl.pallas_call(
        reduce_body, name=name, out_shape=jax.ShapeDtypeStruct((1, width), F32),
        scratch_shapes=[pltpu.VMEM((N_DEV, 1, width), F32), pltpu.SemaphoreType.DMA((7,)), pltpu.SemaphoreType.DMA((7,))],
    )(*parts, loss_part)

    def adamw_body(*refs):
        tot_ref, w_refs, m_refs, v_refs, outs = refs[0], refs[1:k + 1], refs[k + 1:2 * k + 1], refs[2 * k + 1:3 * k + 1], refs[3 * k + 1:]
        for i in range(k):
            g = tot_ref[:, spots[i]:spots[i] + sizes[i]]
            outs[4 * i][...] = g
            outs[4 * i + 1][...], outs[4 * i + 2][...], outs[4 * i + 3][...] = _adamw_math(w_refs[i][...], g, m_refs[i][...], v_refs[i][...])
        outs[4 * k][...] = tot_ref[:, spots[k]:spots[k] + 1]

    out_shape = [jax.ShapeDtypeStruct((1, n), F32) for n in sizes[:k] for _ in range(4)] + [jax.ShapeDtypeStruct((1, 1), F32)]
    res = pl.pallas_call(adamw_body, name=name + "_adamw", out_shape=tuple(out_shape))(total, *ws, *ms, *vs)
    return [res[4 * i:4 * i + 4] for i in range(k)], res[4 * k]


def _pad_rows(a, rows):
    return jnp.pad(a, ((0, rows - a.shape[0]), (0, 0)))


def kernel(x, positions, norm_mix_g, w_in, b_fgate, q_norm_g, w_uq, kv_norm_g, w_ukv, fox_out_g, mla_out_g, w_o, norm_ffn_g, w_gate, w_up, w_down, final_norm_g, loss_target, m_norm_mix_g, m_w_in, m_b_fgate, m_q_norm_g, m_w_uq, m_kv_norm_g, m_w_ukv, m_fox_out_g, m_mla_out_g, m_w_o, m_norm_ffn_g, m_w_gate, m_w_up, m_w_down, m_final_norm_g, v_norm_mix_g, v_w_in, v_b_fgate, v_q_norm_g, v_w_uq, v_kv_norm_g, v_w_ukv, v_fox_out_g, v_mla_out_g, v_w_o, v_norm_ffn_g, v_w_gate, v_w_up, v_w_down, v_final_norm_g):
    bl, s, d = x.shape
    t = bl * s
    bh = bl * HEADS
    tq = _tile(s, 256)
    grp = s // LANES
    fw = HEADS * HEAD_DIM
    q_rank, kv_rank = w_uq.shape[1], w_ukv.shape[1]
    in_cols = w_in.shape[2]
    n_in = N_DEV * in_cols
    ff = N_DEV * w_gate.shape[2]
    half = MLA_ROPE // 2
    o_kvlat, o_krope, o_flogit = q_rank, q_rank + kv_rank, q_rank + kv_rank + LANES
    b_cols = -(-(o_flogit + HEADS) // LANES) * LANES

    tr = lambda w: jnp.transpose(w[0])
    in_rows = -(-in_cols // 16) * 16
    uq_rows = w_uq.shape[2] * q_rank // d
    ukv_rows = w_ukv.shape[2] * kv_rank // d
    pieces = [_pad_rows(tr(w_in), in_rows), _pad_rows(tr(w_uq).reshape(uq_rows, d), -(-uq_rows // 16) * 16),
              tr(w_ukv).reshape(ukv_rows, d), w_o[0], tr(w_gate), tr(w_up), w_down[0]]
    pieces = [p.astype(BF16) for p in pieces]
    offs = [0]
    for p in pieces:
        offs.append(offs[-1] + p.shape[0])
    legs = [(0, 1), (1, 5), (5, 7)]
    gathered = {}

    def full(i, rows):
        leg = next(n for n, (lo, hi) in enumerate(legs) if lo <= i < hi)
        base = offs[legs[leg][0]]
        return gathered[leg][:, offs[i] - base:offs[i] - base + rows]

    x2d = x.reshape(t, d)
    h1, gathered[0] = _rmsnorm(x2d, 0, d, norm_mix_g, BF16, "norm_mix", traffic=_Traffic("spread", pieces[0:1]))

    w_in_t = full(0, in_cols).reshape(n_in, d)
    n_qkv = 3 * fw
    w_in_a = w_in_t[:n_qkv].reshape(3, PAIRS, LANES, d).transpose(1, 0, 2, 3).reshape(n_qkv, d)
    lat0, rope0 = n_qkv + HEADS, n_qkv + HEADS + q_rank + kv_rank
    k_rep = jnp.broadcast_to(w_in_t[rope0:].reshape(2, 1, half, d), (2, 4, half, d)).reshape(LANES, d)
    w_in_b = jnp.concatenate([w_in_t[lat0:rope0], k_rep, w_in_t[n_qkv:lat0],
                              jnp.zeros((b_cols - o_flogit - HEADS, d), BF16)], axis=0)

    def per_head_rows(a):
        return a.reshape(bl, s, HEADS).transpose(0, 2, 1).reshape(bh, 1, s)

    proj_a =_matmul(h1, w_in_a, "nt", BF16, "proj_fox", tn=3 * LANES)
    proj_b = _matmul(h1, w_in_b, "nt", F32, "proj_mla", tn=b_cols)

    z = proj_b[:, o_flogit:o_flogit + HEADS].reshape(bl, s, HEADS).transpose(0, 2, 1).reshape(bh * grp, LANES)
    bcol = jnp.broadcast_to(b_fgate.reshape(1, HEADS, 1), (bl, HEADS, grp)).reshape(bh * grp, 1)
    c = _fgate(z, bcol, grp, "forget_gate")
    c_bias = c.reshape(bh, 1, s)
    fox_o, fox_lse, gathered[1] = _attn_fwd((proj_a,), c_bias, HEAD_DIM ** -0.5, bl, s, tq, "fox_attention",
                                            traffic=_Traffic("spread", pieces[legs[1][0]:legs[1][1]]))
    w_uq_h = full(1, uq_rows).reshape(HEADS, MLA_QK, q_rank)
    w_uq_pe = jnp.concatenate([w_uq_h[:, HEAD_DIM:HEAD_DIM + half].reshape(2, 1, 4 * half, q_rank),
                               w_uq_h[:, HEAD_DIM + half:].reshape(2, 1, 4 * half, q_rank)], axis=1).reshape(2 * LANES, q_rank)
    w_uq_p = jnp.concatenate([w_uq_h[:, :HEAD_DIM].reshape(fw, q_rank), w_uq_pe], axis=0)
    w_ukv_p = full(2, ukv_rows).reshape(PAIRS, 2, 2, HEAD_DIM, kv_rank).transpose(0, 2, 1, 3, 4).reshape(2 * fw, kv_rank)
    w_o_f = full(3, w_o.shape[1]).reshape(-1, d)
    w_gate_t = full(4, ff // N_DEV).reshape(ff, d)

    qn = _rmsnorm(proj_b, 0, q_rank, q_norm_g, BF16, "norm_q")
    kvn = _rmsnorm(proj_b, o_kvlat // kv_rank, kv_rank, kv_norm_g, BF16, "norm_kv")
    q_raw = _matmul(qn, w_uq_p, "nt", F32, "up_q", tn=fw + 2 * LANES)
    kv_all = _matmul(kvn, w_ukv_p, "nt", BF16, "up_kv")
    inv_freq = ROPE_THETA ** (-jnp.arange(0, MLA_ROPE, 2, dtype=F32) / MLA_ROPE)
    ang = positions.astype(F32).reshape(t, 1) * inv_freq[None, :]
    cos4, sin4 = jnp.tile(jnp.cos(ang), (1, 4)), jnp.tile(jnp.sin(ang), (1, 4))
    rope_cos, rope_sin = jnp.concatenate([cos4, cos4], axis=1), jnp.concatenate([-sin4, sin4], axis=1)
    q_all, kpe = _rope_fwd(q_raw, fw, proj_b, o_krope // LANES, rope_cos, rope_sin, "rope")
    mla_ops = (q_all, kv_all, kpe)
    mla_o, mla_lse, gathered[2] = _attn_fwd(mla_ops, None, MLA_QK ** -0.5, bl, s, tq, "mla_attention",
                                            traffic=_Traffic("spread", pieces[legs[2][0]:legs[2][1]]))
    w_up_t, w_down_f = full(5, ff // N_DEV).reshape(ff, d), full(6, ff // N_DEV).reshape(ff, d)

    cat = _out_norm(fox_o, mla_o, fox_out_g, mla_out_g, "norm_out")
    both = [(d, F32), (d, BF16)]
    x1, h2 = _rows_matmul([(cat, w_o_f, "nn")], [x2d], [norm_ffn_g], _residual_norm, both, [], "proj_out_norm_ffn")
    gate, up, act = _ffn_up(h2, w_gate_t, w_up_t, "ffn_gate_up")
    dx2, dx2_b, dg_final, loss_part = _rows_matmul(
        [(act, w_down_f, "nn")], [x1, loss_target.reshape(t, d)], [final_norm_g.reshape(1, d)], _residual_loss_bwd,
        both, [d, 1], "ffn_down_final_norm_loss")

    d_gate, d_up = _ffn_down_bwd(dx2_b, w_down_f, gate, up, "d_ffn_down")
    dw_down = _matmul(act, dx2_b, "tn", BF16, "dw_down", tm=ff // 2, tn=d, tk=1024)
    dw_gate = _matmul(d_gate, h2, "tn", BF16, "dw_gate", tm=ff // 2, tn=d, tk=1024)
    dw_up = _matmul(d_up, h2, "tn", BF16, "dw_up", tm=ff // 2, tn=d, tk=1024)
    dx1, dx1_b, dg_ffn = _rows_matmul([(d_gate, w_gate_t, "nn"), (d_up, w_up_t, "nn")], [x1, dx2], [norm_ffn_g],
                                      _norm_bwd_residual, both, [d], "d_ffn_gate_up_norm_ffn", tm=256)
    dw_o = _matmul(cat, dx1_b, "tn", BF16, "dw_o", tn=d, tk=1024)
    d_fox_o, d_mla_o, fox_delta, mla_delta, dg_fox, dg_mla = _rows_matmul(
        [(dx1_b, w_o_f, "nt")], [fox_o, mla_o], [fox_out_g, mla_out_g], _out_norm_bwd,
        [(fw, BF16), (fw, BF16), (HEADS, F32), (HEADS, F32)], [fw, fw], "d_proj_out_norm_out")

    per_dev = lambda a: a.reshape(N_DEV, -1, d)
    late_grads = [per_dev(dw_o), per_dev(dw_gate), per_dev(dw_up), per_dev(dw_down)]
    dproj_a, dc, g_late = _attn_bwd((proj_a,), d_fox_o, fox_lse, per_head_rows(fox_delta),
                                    c_bias, HEAD_DIM ** -0.5, bl, s, tq, "d_fox_attention", traffic=_Traffic("swap", late_grads))
    dz, db_fgate = _fgate_bwd(z, bcol, dc.reshape(bh * grp, LANES), grp, "d_forget_gate")
    d_flogit = dz.reshape(bl, HEADS, s).transpose(0, 2, 1).reshape(t, HEADS)

    dq_nope, dkv_all, dq_pe, dk_pe = _attn_bwd(mla_ops, d_mla_o, mla_lse, per_head_rows(mla_delta),
                                               None, MLA_QK ** -0.5, bl, s, tq, "d_mla_attention")
    dq_rot, dk_rot = _rope_bwd(dq_pe, dk_pe, rope_cos, rope_sin, "d_rope")
    dq_lat, dg_q = _rows_matmul([(dq_nope, w_uq_p[:fw], "nn"), (dq_rot, w_uq_p[fw:], "nn")], [(proj_b, 0, q_rank)], [q_norm_g],
                                _norm_bwd_residual, [(q_rank, BF16)], [q_rank], "d_up_q_norm_q")
    dw_uq_nope = _matmul(dq_nope, qn, "tn", BF16, "dw_uq_nope", tn=q_rank, tk=1024)
    dw_uq_pe = _matmul(dq_rot, qn, "tn", BF16, "dw_uq_rope", tn=q_rank, tk=1024)
    dkv_lat, dg_kv = _rows_matmul([(dkv_all, w_ukv_p, "nn")], [(proj_b, o_kvlat // kv_rank, kv_rank)], [kv_norm_g],
                                  _norm_bwd_residual, [(kv_rank, BF16)], [kv_rank], "d_up_kv_norm_kv")
    dw_ukv_p = _matmul(dkv_all, kvn, "tn", BF16, "dw_ukv", tn=kv_rank, tk=1024)

    dproj_b = jnp.concatenate([dq_lat.astype(BF16), dkv_lat.astype(BF16), dk_rot.astype(BF16), d_flogit.astype(BF16),
                               jnp.zeros((t, b_cols - o_flogit - HEADS), BF16)], axis=1)
    dw_in_a = _matmul(dproj_a, h1, "tn", BF16, "dw_in_fox", tn=d, tk=1024)
    dw_in_b = _matmul(dproj_b, h1, "tn", F32, "dw_in_mla", tm=b_cols, tn=d, tk=1024)

    dw_krope = dw_in_b[o_krope:o_flogit].reshape(2, 4, half, d).sum(axis=1).reshape(MLA_ROPE, d)
    dw_in_t = jnp.concatenate([dw_in_a.reshape(PAIRS, 3, LANES, d).transpose(1, 0, 2, 3).reshape(n_qkv, d),
                               dw_in_b[o_flogit:o_flogit + HEADS].astype(BF16), dw_in_b[:o_krope].astype(BF16),
                               dw_krope.astype(BF16)], axis=0)
    pad_dev = lambda a, rows: jnp.pad(a, ((0, 0), (0, rows - a.shape[1]), (0, 0)))
    dw_uq_pe5 = dw_uq_pe.reshape(2, 2, 4, half, q_rank)
    dw_uq_h = jnp.concatenate([dw_uq_nope.reshape(HEADS, HEAD_DIM, q_rank), dw_uq_pe5[:, 0].reshape(HEADS, half, q_rank),
                               dw_uq_pe5[:, 1].reshape(HEADS, half, q_rank)], axis=1)
    dw_ukv_h = dw_ukv_p.reshape(PAIRS, 2, 2, HEAD_DIM, kv_rank).transpose(0, 2, 1, 3, 4).reshape(HEADS, 2 * HEAD_DIM, kv_rank)
    n_last = 3
    last_grads = [pad_dev(per_dev(dw_in_t), pieces[0].shape[0]), pad_dev(per_dev(dw_uq_h), pieces[1].shape[0]), per_dev(dw_ukv_h)]
    dh1_fox, g_last = _matmul(dproj_a, w_in_a, "nn", F32, "d_proj_fox", traffic=_Traffic("swap", last_grads))
    grad_x, dg_mix = _rows_matmul([(dproj_b, w_in_b, "nn")], [x2d, dx1, dh1_fox], [norm_mix_g], _norm_bwd_residual,
                                  [(d, F32)], [d], "d_proj_mla_norm_mix")
    g_last = _sum_blocks(g_last, "sum_last_grads")
    g_late = _sum_blocks(g_late, "sum_late_grads")

    def mine(i, rows):
        src, base = (g_last, 0) if i < n_last else (g_late, offs[n_last])
        return src[offs[i] - base:offs[i] - base + rows]

    big = [
        ("w_in", w_in, m_w_in, v_w_in, mine(0, in_cols), True),
        ("w_uq", w_uq, m_w_uq, v_w_uq, mine(1, uq_rows).reshape(-1, q_rank), True),
        ("w_ukv", w_ukv, m_w_ukv, v_w_ukv, mine(2, ukv_rows).reshape(-1, kv_rank), True),
        ("w_o", w_o, m_w_o, v_w_o, mine(3, w_o.shape[1]), False),
        ("w_gate", w_gate, m_w_gate, v_w_gate, mine(4, ff // N_DEV), True),
        ("w_up", w_up, m_w_up, v_w_up, mine(5, ff // N_DEV), True),
        ("w_down", w_down, m_w_down, v_w_down, mine(6, ff // N_DEV), False),
    ]
    out = {}
    for nm, w, m, v, g, transposed in big:
        lay = (lambda a: a[0].T) if transposed else (lambda a: a[0])
        back = (lambda a: a.T[None]) if transposed else (lambda a: a[None])
        dl, new_m, new_v = _adamw(lay(w), g, lay(m), lay(v), "adamw_" + nm)
        out[nm] = (back(g), back(dl), back(new_m), back(new_v))

    smalls = [("norm_mix_g", norm_mix_g, m_norm_mix_g, v_norm_mix_g, dg_mix),
              ("b_fgate", b_fgate, m_b_fgate, v_b_fgate, db_fgate.reshape(1, HEADS)),
              ("q_norm_g", q_norm_g, m_q_norm_g, v_q_norm_g, dg_q),
              ("kv_norm_g", kv_norm_g, m_kv_norm_g, v_kv_norm_g, dg_kv),
              ("fox_out_g", fox_out_g, m_fox_out_g, v_fox_out_g, dg_fox),
              ("mla_out_g", mla_out_g, m_mla_out_g, v_mla_out_g, dg_mla),
              ("norm_ffn_g", norm_ffn_g, m_norm_ffn_g, v_norm_ffn_g, dg_ffn),
              ("final_norm_g", final_norm_g, m_final_norm_g, v_final_norm_g, dg_final)]
    flat = lambda a: a.reshape(1, -1)
    results, loss = _small_all_reduce_adamw([e[4] for e in smalls], loss_part, [flat(e[1]) for e in smalls],
                                            [flat(e[2]) for e in smalls], [flat(e[3]) for e in smalls], "reduce_small_adamw")
    for (nm, w, _, _, _), res in zip(smalls, results):
        out[nm] = tuple(a.reshape(w.shape) for a in res)
    loss = loss[0, 0]

    order = ["norm_mix_g", "w_in", "b_fgate", "q_norm_g", "w_uq", "kv_norm_g", "w_ukv", "fox_out_g", "mla_out_g", "w_o",
             "norm_ffn_g", "w_gate", "w_up", "w_down", "final_norm_g"]
    return (loss, grad_x.reshape(bl, s, d), *[out[n][0] for n in order], *[out[n][1] for n in order],
            *[out[n][2] for n in order], *[out[n][3] for n in order])
```

```python
import math

import jax
import jax.numpy as jnp
from jax import lax
from jax.experimental import pallas as pl
from jax.experimental.pallas import tpu as pltpu

F32 = jnp.float32
BF16 = jnp.bfloat16
MESH = pl.DeviceIdType.MESH

N_DEV = 8
HEADS = 8
HEAD_DIM = 64
PAIRS = HEADS // 2
MLA_ROPE = 32
MLA_QK = HEAD_DIM + MLA_ROPE
ROPE_THETA = 10000.0
NORM_EPS = 1e-6
ADAM_LR, ADAM_B1, ADAM_B2, ADAM_EPS, ADAM_WD, ADAM_STEP = 0.001, 0.9, 0.999, 1e-08, 0.01, 10

LANES = 128
MASKED = -1e30
VMEM_LIMIT = 48 * 1024 * 1024

_DIMS = {"nn": (((1,), (0,)), ((), ())), "nt": (((1,), (1,)), ((), ())), "tn": (((0,), (0,)), ((), ()))}


def _params(*sem):
    return pltpu.CompilerParams(dimension_semantics=sem, vmem_limit_bytes=VMEM_LIMIT)


def _dot(a, b, mode):
    return lax.dot_general(a.astype(BF16), b.astype(BF16), _DIMS[mode], preferred_element_type=F32)


def _tile(n, pref, unit=8):
    if n <= pref:
        return n
    t = pref - pref % unit
    while n % t:
        t -= unit
    return t


def _log2(n):
    assert n & (n - 1) == 0
    return n.bit_length() - 1


def _matmul(a, b, mode, out_dtype, name, tm=512, tn=512, tk=None, res=None, traffic=None):
    if mode == "nn":
        (m, kd), n = a.shape, b.shape[1]
    elif mode == "nt":
        (m, kd), n = a.shape, b.shape[0]
    else:
        (kd, m), n = a.shape, b.shape[1]
    tm, tn = _tile(m, tm, LANES if mode == "tn" else 16), _tile(n, tn, LANES)
    tk = kd if tk is None else _tile(kd, tk, LANES)
    nk = kd // tk
    a_spec = pl.BlockSpec((tk, tm), lambda i, j, k: (k, i)) if mode == "tn" else pl.BlockSpec((tm, tk), lambda i, j, k: (i, k))
    b_spec = pl.BlockSpec((tn, tk), lambda i, j, k: (j, k)) if mode == "nt" else pl.BlockSpec((tk, tn), lambda i, j, k: (k, j))
    o_spec = pl.BlockSpec((tm, tn), lambda i, j, k: (i, j))
    has_res = res is not None
    n_carried = len(traffic.pieces) if traffic else 0
    grid = (m // tm, n // tn, nk)

    def body(*refs):
        a_ref, b_ref = refs[:2]
        r_ref = refs[2] if has_res else None
        n_in = 2 + has_res + n_carried
        o_ref = refs[n_in]
        step = [pl.program_id(axis) for axis in range(3)]
        if traffic:
            carried_in, carried_out, sems = refs[2 + has_res:n_in], refs[n_in + 1], refs[len(refs) - 3:]

            @pl.when((step[0] == 0) & (step[1] == 0) & (step[2] == 0))
            def _():
                traffic.start(carried_in, carried_out, *sems)

        def finish(acc):
            if has_res:
                acc = acc + r_ref[...]
            o_ref[...] = acc.astype(out_dtype)

        part = _dot(a_ref[...], b_ref[...], mode)
        if nk == 1:
            finish(part)
        else:
            acc_ref = refs[n_in + 1 + bool(traffic)]

            @pl.when(step[2] == 0)
            def _():
                acc_ref[...] = part

            @pl.when(step[2] > 0)
            def _():
                acc_ref[...] += part

            @pl.when(step[2] == nk - 1)
            def _():
                finish(acc_ref[...])

        if traffic:
            @pl.when((step[0] == grid[0] - 1) & (step[1] == grid[1] - 1) & (step[2] == nk - 1))
            def _():
                traffic.wait(carried_out, *sems)

    in_specs = [a_spec, b_spec] + ([o_spec] if has_res else [])
    out_specs, out_shape = [o_spec], [jax.ShapeDtypeStruct((m, n), out_dtype)]
    scratch = [pltpu.VMEM((tm, tn), F32)] if nk > 1 else []
    if traffic:
        in_specs += traffic.in_specs
        out_specs.append(traffic.out_spec)
        out_shape.append(traffic.out_shape)
        scratch += traffic.scratch
    out = pl.pallas_call(
        body, name=name, grid=grid, in_specs=in_specs, out_specs=tuple(out_specs), out_shape=tuple(out_shape),
        scratch_shapes=scratch,
        compiler_params=_params(*(("arbitrary",) * 3 if traffic else ("parallel", "parallel", "arbitrary"))),
    )(*([a, b] + ([res] if has_res else []) + (traffic.pieces if traffic else [])))
    return out if traffic else out[0]


def _rstd(x):
    return lax.rsqrt(jnp.mean(x * x, axis=-1, keepdims=True) + NORM_EPS)


def _norm_bwd(x, g, dy):
    r = _rstd(x)
    xh = x * r
    u = dy * g
    dx = r * (u - xh * jnp.mean(u * xh, axis=-1, keepdims=True))
    return dx, jnp.sum(dy * xh, axis=0, keepdims=True)


def _rmsnorm(x, col, width, g, out_dtype, name, traffic=None):
    t = x.shape[0]
    tm = _tile(t, 512)
    steps = t // tm
    n_carried = len(traffic.pieces) if traffic else 0

    def body(*refs):
        x_ref, g_ref, o_ref = refs[0], refs[1], refs[2 + n_carried]
        if traffic:
            carried_in, carried_out, sems = refs[2:2 + n_carried], refs[3 + n_carried], refs[4 + n_carried:]

            @pl.when(pl.program_id(0) == 0)
            def _():
                traffic.start(carried_in, carried_out, *sems)

        xv = x_ref[...]
        o_ref[...] = ((xv * _rstd(xv)) * g_ref[...]).astype(out_dtype)
        if traffic:
            @pl.when(pl.program_id(0) == steps - 1)
            def _():
                traffic.wait(carried_out, *sems)

    in_specs = [pl.BlockSpec((tm, width), lambda i: (i, col)), pl.BlockSpec((1, width), lambda i: (0, 0))]
    out_specs = [pl.BlockSpec((tm, width), lambda i: (i, 0))]
    out_shape = [jax.ShapeDtypeStruct((t, width), out_dtype)]
    if traffic:
        in_specs += traffic.in_specs
        out_specs.append(traffic.out_spec)
        out_shape.append(traffic.out_shape)
    out = pl.pallas_call(
        body, name=name, grid=(steps,), in_specs=in_specs, out_specs=tuple(out_specs), out_shape=tuple(out_shape),
        scratch_shapes=traffic.scratch if traffic else [],
        compiler_params=_params("arbitrary" if traffic else "parallel"),
    )(x, g, *(traffic.pieces if traffic else []))
    return out if traffic else out[0]


def _out_norm(fo, mo, gf, gm, name):
    t, w = fo.shape
    tm = _tile(t, 512)
    row = pl.BlockSpec((tm, w), lambda i: (i, 0))
    vec = pl.BlockSpec((1, w), lambda i: (0, 0))

    def body(fo_ref, mo_ref, gf_ref, gm_ref, o_ref):
        f, m = fo_ref[...], mo_ref[...]
        o_ref[:, :w] = ((f * _rstd(f)) * gf_ref[...]).astype(BF16)
        o_ref[:, w:] = ((m * _rstd(m)) * gm_ref[...]).astype(BF16)

    return pl.pallas_call(
        body, name=name, grid=(t // tm,), in_specs=[row, row, vec, vec],
        out_specs=pl.BlockSpec((tm, 2 * w), lambda i: (i, 0)),
        out_shape=jax.ShapeDtypeStruct((t, 2 * w), BF16),
        compiler_params=_params("parallel"),
    )(fo, mo, gf, gm)


def _split3(x):
    hi = x.astype(BF16)
    r1 = x - hi.astype(F32)
    mid = r1.astype(BF16)
    lo = (r1 - mid.astype(F32)).astype(BF16)
    return hi, mid, lo


def _dot_x01(x, m01):
    hi, mid, lo = _split3(x)
    d = lambda p: lax.dot_general(p, m01, _DIMS["nn"], preferred_element_type=F32)
    return (d(lo) + d(mid)) + d(hi)


def _dot_01x(m01, x):
    hi, mid, lo = _split3(x)
    d = lambda p: lax.dot_general(m01, p, _DIMS["nn"], preferred_element_type=F32)
    return (d(lo) + d(mid)) + d(hi)


def _rows_matmul(terms, rows_in, vecs_in, epilogue, rows_out, sums_out, name, tm=512):
    t = terms[0][0].shape[0]
    tm = _tile(t, tm, 16)
    n_terms, n_rows, n_vecs = len(terms), len(rows_in), len(vecs_in)

    def body(*refs):
        acc = None
        for i, (_, _, mode) in enumerate(terms):
            part = _dot(refs[2 * i][...], refs[2 * i + 1][...], mode)
            acc = part if acc is None else acc + part
        at = 2 * n_terms
        row_vals, sum_vals = epilogue(acc, [r[...] for r in refs[at:at + n_rows]],
                                      [r[...] for r in refs[at + n_rows:at + n_rows + n_vecs]])
        at += n_rows + n_vecs
        for ref, val, (_, dtype) in zip(refs[at:], row_vals, rows_out):
            ref[...] = val.astype(dtype)
        sum_refs = refs[at + len(rows_out):]

        @pl.when(pl.program_id(0) == 0)
        def _():
            for ref in sum_refs:
                ref[...] = jnp.zeros_like(ref)

        for ref, val in zip(sum_refs, sum_vals):
            ref[...] += val

    rows = lambda w: pl.BlockSpec((tm, w), lambda i: (i, 0))
    whole = lambda a: pl.BlockSpec(a.shape, lambda i: (0, 0))
    in_specs, args = [], []
    for a, b, _ in terms:
        in_specs += [rows(a.shape[1]), whole(b)]
        args += [a, b]
    for r in rows_in:
        arr, col, w = r if isinstance(r, tuple) else (r, 0, r.shape[1])
        in_specs.append(pl.BlockSpec((tm, w), lambda i, col=col: (i, col)))
        args.append(arr)
    in_specs += [whole(v) for v in vecs_in]
    args += list(vecs_in)
    return pl.pallas_call(
        body, name=name, grid=(t // tm,), in_specs=in_specs,
        out_specs=tuple([rows(w) for w, _ in rows_out] + [pl.BlockSpec((1, w), lambda i: (0, 0)) for w in sums_out]),
        out_shape=tuple([jax.ShapeDtypeStruct((t, w), dt) for w, dt in rows_out] + [jax.ShapeDtypeStruct((1, w), F32) for w in sums_out]),
        compiler_params=_params("arbitrary"),
    )(*args)


def _residual_norm(acc, rows, vecs):
    x1 = rows[0] + acc
    return [x1, (x1 * _rstd(x1)) * vecs[0]], []


def _residual_loss_bwd(acc, rows, vecs):
    x2, gv = rows[0] + acc, vecs[0]
    diff = (x2 * _rstd(x2)) * gv - rows[1]
    dx, dg = _norm_bwd(x2, gv, diff / x2.shape[1])
    return [dx, dx], [dg, 0.5 * jnp.sum(jnp.mean(diff * diff, axis=-1, keepdims=True), axis=0, keepdims=True)]


def _norm_bwd_residual(acc, rows, vecs):
    dy = acc + rows[2] if len(rows) > 2 else acc
    dx, dg = _norm_bwd(rows[0], vecs[0], dy)
    if len(rows) > 1:
        dx = dx + rows[1]
    return [dx, dx], [dg]


def _out_norm_bwd(acc, rows, vecs):
    (f, m), w = rows, rows[0].shape[1]
    nh = w // HEAD_DIM
    lane_head = lax.shift_right_logical(lax.broadcasted_iota(jnp.int32, (w, nh), 0), _log2(HEAD_DIM))
    sel = (lane_head == lax.broadcasted_iota(jnp.int32, (w, nh), 1)).astype(BF16)
    dfo, dgf = _norm_bwd(f, vecs[0], acc[:, :w])
    dmo, dgm = _norm_bwd(m, vecs[1], acc[:, w:])
    return [dfo, dmo, _dot_x01(dfo * f, sel), _dot_x01(dmo * m, sel)], [dgf, dgm]


def _ffn_up(h, wg_t, wu_t, name, tm=512, tf=1408):
    t, d = h.shape
    f = wg_t.shape[0]
    tm, tf = _tile(t, tm, 16), _tile(f, tf, LANES)
    tok = pl.BlockSpec((tm, tf), lambda i, j: (i, j))
    wt = pl.BlockSpec((tf, d), lambda i, j: (j, 0))

    def body(h_ref, wg_ref, wu_ref, g_ref, u_ref, a_ref):
        hv = h_ref[...]
        g, u = _dot(hv, wg_ref[...], "nt"), _dot(hv, wu_ref[...], "nt")
        g_ref[...], u_ref[...] = g.astype(BF16), u.astype(BF16)
        a_ref[...] = ((g * jax.nn.sigmoid(g)) * u).astype(BF16)

    return pl.pallas_call(
        body, name=name, grid=(t // tm, f // tf), in_specs=[pl.BlockSpec((tm, d), lambda i, j: (i, 0)), wt, wt],
        out_specs=(tok, tok, tok),
        out_shape=(jax.ShapeDtypeStruct((t, f), BF16), jax.ShapeDtypeStruct((t, f), BF16), jax.ShapeDtypeStruct((t, f), BF16)),
        compiler_params=_params("parallel", "parallel"),
    )(h, wg_t, wu_t)


def _ffn_down_bwd(dy, w_down, gate, up, name, tm=512, tf=1408):
    t, d = dy.shape
    f = w_down.shape[0]
    tm, tf = _tile(t, tm, 16), _tile(f, tf, LANES)
    tok = pl.BlockSpec((tm, tf), lambda i, j: (i, j))

    def body(dy_ref, w_ref, g_ref, u_ref, dg_ref, du_ref):
        da = _dot(dy_ref[...], w_ref[...], "nt")
        g = g_ref[...].astype(F32)
        sg = jax.nn.sigmoid(g)
        dg_ref[...] = (da * u_ref[...].astype(F32) * (sg * (1.0 + g * (1.0 - sg)))).astype(BF16)
        du_ref[...] = (da * (g * sg)).astype(BF16)

    return pl.pallas_call(
        body, name=name, grid=(t // tm, f // tf),
        in_specs=[pl.BlockSpec((tm, d), lambda i, j: (i, 0)), pl.BlockSpec((tf, d), lambda i, j: (j, 0)), tok, tok],
        out_specs=(tok, tok),
        out_shape=(jax.ShapeDtypeStruct((t, f), BF16), jax.ShapeDtypeStruct((t, f), BF16)),
        compiler_params=_params("parallel", "parallel"),
    )(dy, w_down, gate, up)


def _chunk_scan_mats(rows, grp, reverse):
    ii = lax.broadcasted_iota(jnp.int32, (LANES, LANES), 0)
    jj = lax.broadcasted_iota(jnp.int32, (LANES, LANES), 1)
    within = ((ii >= jj) if reverse else (ii <= jj)).astype(BF16)
    ones = jnp.ones((LANES, LANES), BF16)
    ri = lax.broadcasted_iota(jnp.int32, (rows, rows), 0)
    rj = lax.broadcasted_iota(jnp.int32, (rows, rows), 1)
    sh = _log2(grp)
    same = lax.shift_right_logical(ri, sh) == lax.shift_right_logical(rj, sh)
    across = (same & ((rj > ri) if reverse else (rj < ri))).astype(BF16)
    return within, ones, across


def _running_sum(v, mats):
    within, ones, across = mats
    return _dot_x01(v, within) + _dot_01x(across, _dot_x01(v, ones))


def _fgate(z, bcol, grp, name):
    rows = z.shape[0]

    def body(z_ref, b_ref, c_ref):
        zz = z_ref[...] + b_ref[...]
        log_f = jnp.minimum(zz, 0.0) - jnp.log1p(jnp.exp(-jnp.abs(zz)))
        c_ref[...] = _running_sum(log_f, _chunk_scan_mats(rows, grp, False))

    return pl.pallas_call(body, name=name, out_shape=jax.ShapeDtypeStruct(z.shape, F32),
                          compiler_params=pltpu.CompilerParams(vmem_limit_bytes=VMEM_LIMIT))(z, bcol)


def _fgate_bwd(z, bcol, dc, grp, name):
    rows = z.shape[0]

    def body(z_ref, b_ref, dc_ref, dz_ref, db_ref):
        zz = z_ref[...] + b_ref[...]
        dz = _running_sum(dc_ref[...], _chunk_scan_mats(rows, grp, True)) * jax.nn.sigmoid(-zz)
        dz_ref[...] = dz
        head = lax.shift_right_logical(lax.broadcasted_iota(jnp.int32, (HEADS, rows), 1), _log2(grp)) & (HEADS - 1)
        sel = (head == lax.broadcasted_iota(jnp.int32, (HEADS, rows), 0)).astype(BF16)
        db_ref[...] = jnp.sum(_dot_01x(sel, dz), axis=1, keepdims=True)

    return pl.pallas_call(
        body, name=name,
        out_shape=(jax.ShapeDtypeStruct(z.shape, F32), jax.ShapeDtypeStruct((HEADS, 1), F32)),
        compiler_params=pltpu.CompilerParams(vmem_limit_bytes=VMEM_LIMIT),
    )(z, bcol, dc)


def _rotate(x, cs, sn_signed):
    return x * cs + pltpu.roll(x, LANES // 2, axis=1) * sn_signed


def _mla_prep(proj_b, q_rank, kv_rank, gq, gkv, w_uq_p, w_ukv_p, nope, cs, sn, name):
    t, bw = proj_b.shape
    qw, kvw = w_uq_p.shape[0], w_ukv_p.shape[0]
    tm = _tile(t, 512)
    rows = lambda w: pl.BlockSpec((tm, w), lambda i: (i, 0))
    whole = lambda a: pl.BlockSpec(a.shape, lambda i: (0, 0))

    def body(pb_ref, gq_ref, gkv_ref, wq_ref, wkv_ref, c_ref, s_ref, qn_ref, kvn_ref, q_ref, kv_ref, kpe_ref):
        c, s = c_ref[...], s_ref[...]
        ql, kvl = pb_ref[:, :q_rank], pb_ref[:, q_rank:q_rank + kv_rank]
        qn = ((ql * _rstd(ql)) * gq_ref[...]).astype(BF16)
        kvn = ((kvl * _rstd(kvl)) * gkv_ref[...]).astype(BF16)
        qn_ref[...], kvn_ref[...] = qn, kvn
        q_raw = _dot(qn, wq_ref[...], "nt")
        q_ref[:, :nope] = q_raw[:, :nope].astype(BF16)
        for off in range(nope, qw, LANES):
            q_ref[:, off:off + LANES] = _rotate(q_raw[:, off:off + LANES], c, s).astype(BF16)
        kv_ref[...] = _dot(kvn, wkv_ref[...], "nt").astype(BF16)
        kpe_ref[...] = _rotate(pb_ref[:, q_rank + kv_rank:q_rank + kv_rank + LANES], c, s).astype(BF16)

    return pl.pallas_call(
        body, name=name, grid=(t // tm,),
        in_specs=[rows(bw), whole(gq), whole(gkv), whole(w_uq_p), whole(w_ukv_p), rows(LANES), rows(LANES)],
        out_specs=(rows(q_rank), rows(kv_rank), rows(qw), rows(kvw), rows(LANES)),
        out_shape=(jax.ShapeDtypeStruct((t, q_rank), BF16), jax.ShapeDtypeStruct((t, kv_rank), BF16),
                   jax.ShapeDtypeStruct((t, qw), BF16), jax.ShapeDtypeStruct((t, kvw), BF16), jax.ShapeDtypeStruct((t, LANES), BF16)),
        compiler_params=_params("parallel"),
    )(proj_b, gq, gkv, w_uq_p, w_ukv_p, cs, sn)


def _mla_prep_bwd(dq_nope, dq_pe, dkv_all, dk_pe, d_tail, proj_b, q_rank, kv_rank, gq, gkv, w_uq_p, w_ukv_p, cs, sn, name):
    t, bw = proj_b.shape
    nope, pw = dq_nope.shape[1], dq_pe.shape[1]
    tm = _tile(t, 512)
    rows = lambda w: pl.BlockSpec((tm, w), lambda i: (i, 0))
    whole = lambda a: pl.BlockSpec(a.shape, lambda i: (0, 0))
    o_k = q_rank + kv_rank

    def body(dqn_ref, dqp_ref, dkv_ref, dkp_ref, dt_ref, pb_ref, gq_ref, gkv_ref, wq_ref, wkv_ref, c_ref, s_ref,
             dpb_ref, dqr_ref, dgq_ref, dgkv_ref):
        c, s = c_ref[...], -s_ref[...]
        for off in range(0, pw, LANES):
            dqr_ref[:, off:off + LANES] = _rotate(dqp_ref[:, off:off + LANES], c, s).astype(BF16)
        d_qn = _dot(dqn_ref[...], wq_ref[:nope, :], "nn") + _dot(dqr_ref[...], wq_ref[nope:, :], "nn")
        dq_lat, dgq = _norm_bwd(pb_ref[:, :q_rank], gq_ref[...], d_qn)
        dkv_lat, dgkv = _norm_bwd(pb_ref[:, q_rank:o_k], gkv_ref[...], _dot(dkv_ref[...], wkv_ref[...], "nn"))
        dpb_ref[:, :q_rank] = dq_lat.astype(BF16)
        dpb_ref[:, q_rank:o_k] = dkv_lat.astype(BF16)
        dpb_ref[:, o_k:o_k + LANES] = _rotate(dkp_ref[...], c, s).astype(BF16)
        dpb_ref[:, o_k + LANES:] = dt_ref[...].astype(BF16)

        @pl.when(pl.program_id(0) == 0)
        def _():
            dgq_ref[...] = jnp.zeros_like(dgq_ref)
            dgkv_ref[...] = jnp.zeros_like(dgkv_ref)

        dgq_ref[...] += dgq
        dgkv_ref[...] += dgkv

    return pl.pallas_call(
        body, name=name, grid=(t // tm,),
        in_specs=[rows(nope), rows(pw), rows(dkv_all.shape[1]), rows(LANES), rows(bw - o_k - LANES), rows(bw), whole(gq), whole(gkv),
                  whole(w_uq_p), whole(w_ukv_p), rows(LANES), rows(LANES)],
        out_specs=(rows(bw), rows(pw), whole(gq), whole(gkv)),
        out_shape=(jax.ShapeDtypeStruct((t, bw), BF16), jax.ShapeDtypeStruct((t, pw), BF16),
                   jax.ShapeDtypeStruct(gq.shape, F32), jax.ShapeDtypeStruct(gkv.shape, F32)),
        compiler_params=_params("arbitrary"),
    )(dq_nope, dq_pe, dkv_all, dk_pe, d_tail, proj_b, gq, gkv, w_uq_p, w_ukv_p, cs, sn)


def _lane_masks(pair, h, pe):
    lane = lax.broadcasted_iota(jnp.int32, (1, LANES), 1)
    in_head = lax.shift_right_logical(lane, _log2(HEAD_DIM)) == h
    in_rope = ((lax.shift_right_logical(lane, _log2(MLA_ROPE // 2)) & 3) == ((2 * pair + h) & 3)) if pe else None
    return in_head, in_rope


def _keep(mask, v):
    return jnp.where(mask, v, jnp.zeros_like(v))


def _to_row(col):
    n = col.shape[0]
    eye = lax.broadcasted_iota(jnp.int32, (n, n), 0) == lax.broadcasted_iota(jnp.int32, (n, n), 1)
    return jnp.sum(jnp.where(eye, col, 0.0), axis=0, keepdims=True)


def _to_col(row):
    n = row.shape[1]
    eye = lax.broadcasted_iota(jnp.int32, (n, n), 0) == lax.broadcasted_iota(jnp.int32, (n, n), 1)
    return jnp.sum(jnp.where(eye, row, 0.0), axis=1, keepdims=True)


def _first_step():
    return (pl.program_id(0) == 0) & (pl.program_id(1) == 0)


def _last_step(n0, n1):
    return (pl.program_id(0) == n0 - 1) & (pl.program_id(1) == n1 - 1)


def _attn_fwd(ops, bias, scale, bl, s, tq, name, traffic=None):
    pe = len(ops) == 3
    has_bias = bias is not None
    exact_scale = math.frexp(scale)[0] == 0.5
    nq = s // tq
    t = bl * s
    n_carried = len(traffic.pieces) if traffic else 0

    def body(*refs):
        sems = refs[len(refs) - 3:] if traffic else ()
        if pe:
            q_ref, qpe_ref, kv_ref, kpe_ref = refs[:4]
            n_in = 4
            q_at = lambda r0, r1: q_ref[r0:r1, :]
            v_at = lambda r0, r1: kv_ref[r0:r1, LANES:]
            kcat = refs[len(refs) - 1 - len(sems)]
            kcat[:, :LANES] = kv_ref[:, :LANES]
            kcat[:, LANES:] = kpe_ref[...]
            k_at = lambda r0, r1: kcat[r0:r1, :]
        else:
            qkv_ref = refs[0]
            n_in = 1
            q_at = lambda r0, r1: qkv_ref[r0:r1, :LANES]
            k_at = lambda r0, r1: qkv_ref[r0:r1, LANES:2 * LANES]
            v_at = lambda r0, r1: qkv_ref[r0:r1, 2 * LANES:]
        if has_bias:
            c_ref = refs[n_in]
            n_in += 1
        carried_in = refs[n_in:n_in + n_carried]
        n_in += n_carried
        o_ref, lse_ref = refs[n_in:n_in + 2]
        if traffic:
            carried_out = refs[n_in + 2]

            @pl.when(_first_step())
            def _():
                traffic.start(carried_in, carried_out, *sems)

        pair = pl.program_id(1)
        causal = lax.broadcasted_iota(jnp.int32, (tq, tq), 1) <= lax.broadcasted_iota(jnp.int32, (tq, tq), 0)
        o_ref[...] = jnp.zeros_like(o_ref)

        for i in range(nq):
            for h in range(2):
                in_head, in_rope = _lane_masks(pair, h, pe)
                r0, r1 = i * tq, (i + 1) * tq
                qm = _keep(in_head, q_at(r0, r1))
                if pe:
                    qm = jnp.concatenate([qm, _keep(in_rope, qpe_ref[r0:r1, :])], axis=1)
                if exact_scale:
                    qm = qm * scale

                def logits(k0, k1):
                    sc = _dot(qm, k_at(k0, k1), "nt")
                    if not exact_scale:
                        sc = sc * scale
                    if has_bias:
                        sc = sc - c_ref[h, :, k0:k1]
                    return sc

                sd = jnp.where(causal, logits(r0, r1), MASKED)
                m = jnp.max(sd, axis=1, keepdims=True)
                if i:
                    so = logits(0, r0)
                    m = jnp.maximum(m, jnp.max(so, axis=1, keepdims=True))
                pd = jnp.exp(sd - m)
                l = jnp.sum(pd, axis=1, keepdims=True)
                acc = _dot(pd, v_at(r0, r1), "nn")
                if i:
                    po = jnp.exp(so - m)
                    l = l + jnp.sum(po, axis=1, keepdims=True)
                    acc = acc + _dot(po, v_at(0, r0), "nn")
                o_ref[r0:r1, :] = jnp.where(in_head, acc / l, o_ref[r0:r1, :])
                lse = _to_row(m + jnp.log(l))
                lse_ref[h, :, r0:r1] = lse + c_ref[h, :, r0:r1] if has_bias else lse

        if traffic:
            @pl.when(_last_step(bl, PAIRS))
            def _():
                traffic.wait(carried_out, *sems)

    seq = lambda w, col: pl.BlockSpec((s, w), col)
    if pe:
        in_specs = [seq(LANES, lambda b, p: (b, p)), seq(LANES, lambda b, p: (b, PAIRS + p // 2)),
                    seq(2 * LANES, lambda b, p: (b, p)), seq(LANES, lambda b, p: (b, 0))]
        args = [ops[0], ops[0], ops[1], ops[2]]
        scratch = [pltpu.VMEM((s, 2 * LANES), BF16)]
    else:
        in_specs = [seq(3 * LANES, lambda b, p: (b, p))]
        args = [ops[0]]
        scratch = []
    per_head_row = pl.BlockSpec((2, 1, s), lambda b, p: (b * PAIRS + p, 0, 0))
    if has_bias:
        in_specs.append(per_head_row)
        args.append(bias)
    out_specs = [seq(LANES, lambda b, p: (b, p)), per_head_row]
    out_shape = [jax.ShapeDtypeStruct((t, HEADS * HEAD_DIM), F32), jax.ShapeDtypeStruct((bl * HEADS, 1, s), F32)]
    if traffic:
        in_specs += traffic.in_specs
        args += traffic.pieces
        out_specs.append(traffic.out_spec)
        out_shape.append(traffic.out_shape)
        scratch += traffic.scratch
    return pl.pallas_call(
        body, name=name, grid=(bl, PAIRS), in_specs=in_specs, out_specs=tuple(out_specs), out_shape=tuple(out_shape),
        scratch_shapes=scratch, compiler_params=_params(*(("arbitrary", "arbitrary") if traffic else ("parallel", "parallel"))),
    )(*args)


def _attn_bwd(ops, do, lse, delta, bias, scale, bl, s, tq, name, traffic=None):
    pe = len(ops) == 3
    has_bias = bias is not None
    nq = s // tq
    t = bl * s
    width = 2 * LANES if pe else LANES
    n_carried = len(traffic.pieces) if traffic else 0

    def body(*refs):
        if pe:
            q_ref, qpe_ref, kv_ref, kpe_ref = refs[:4]
            n_in = 4
            k_at = lambda r0, r1: kv_ref[r0:r1, :LANES]
            v_at = lambda r0, r1: kv_ref[r0:r1, LANES:]
        else:
            qkv_ref = refs[0]
            n_in = 1
            k_at = lambda r0, r1: qkv_ref[r0:r1, LANES:2 * LANES]
            v_at = lambda r0, r1: qkv_ref[r0:r1, 2 * LANES:]
        do_ref, lse_ref, dl_ref = refs[n_in:n_in + 3]
        n_in += 3
        if has_bias:
            c_ref = refs[n_in]
            n_in += 1
        carried_in = refs[n_in:n_in + n_carried]
        rest = refs[n_in + n_carried:]
        if traffic:
            rest, sems = rest[:-3], rest[-3:]
            carried_out = rest[4 if pe else 2]
            rest = rest[:4 if pe else 2] + rest[(4 if pe else 2) + 1:]

            @pl.when(_first_step())
            def _():
                traffic.start(carried_in, carried_out, *sems)

        if pe:
            dqn_ref, dkv_ref, dqpe_ref, dkpe_ref, dq_acc, qcat = rest
            qcat[:, :LANES] = q_ref[...]
            qcat[:, LANES:] = qpe_ref[...]
            q_at = lambda r0, r1: qcat[r0:r1, :]
            dkv_ref[...] = jnp.zeros_like(dkv_ref)
        else:
            dqkv_ref, dc_ref, dq_acc = rest
            q_at = lambda r0, r1: qkv_ref[r0:r1, :LANES]
            dqkv_ref[...] = jnp.zeros_like(dqkv_ref)
            dc_ref[...] = jnp.zeros_like(dc_ref)
        pair = pl.program_id(1)
        dq_acc[...] = jnp.zeros_like(dq_acc)
        causal = lax.broadcasted_iota(jnp.int32, (tq, tq), 1) >= lax.broadcasted_iota(jnp.int32, (tq, tq), 0)
        if pe:
            @pl.when(pair == 0)
            def _():
                dkpe_ref[...] = jnp.zeros_like(dkpe_ref)

            @pl.when(pair % 2 == 0)
            def _():
                dqpe_ref[...] = jnp.zeros_like(dqpe_ref)

        for j in range(nq):
            for h in range(2):
                in_head, in_rope = _lane_masks(pair, h, pe)
                r0, r1 = j * tq, (j + 1) * tq
                kt = _keep(in_head, k_at(r0, r1))
                if pe:
                    kt = jnp.concatenate([kt, _keep(in_rope, kpe_ref[r0:r1, :])], axis=1)
                vt = _keep(in_head, v_at(r0, r1))
                ck = _to_col(c_ref[h, :, r0:r1]) if has_bias else None

                def block(q0, q1, diagonal):
                    qq, dd = q_at(q0, q1), do_ref[q0:q1, :]
                    st = _dot(kt, qq, "nt") * scale
                    if has_bias:
                        st = st + (c_ref[h, :, q0:q1] - ck)
                    if diagonal:
                        st = jnp.where(causal, st, MASKED)
                    pt = jnp.exp(st - lse_ref[h, :, q0:q1])
                    dst = pt * (_dot(vt, dd, "nt") - dl_ref[h, :, q0:q1])
                    dsb = (dst * scale).astype(BF16)
                    dq_acc[q0:q1, :] += _dot(dsb, kt, "tn")
                    if has_bias:
                        dc_ref[h, :, q0:q1] += jnp.sum(dst, axis=0, keepdims=True)
                    return _dot(pt, dd, "nn"), _dot(dsb, qq, "nn"), (jnp.sum(dst, axis=1, keepdims=True) if has_bias else None)

                dv_c, dk_c, cs = block(r0, r1, True)
                if r1 < s:
                    dv_o, dk_o, cs_o = block(r1, s, False)
                    dv_c, dk_c = dv_c + dv_o, dk_c + dk_o
                    cs = cs + cs_o if has_bias else None
                if pe:
                    dkv_ref[r0:r1, :LANES] = jnp.where(in_head, dk_c[:, :LANES].astype(BF16), dkv_ref[r0:r1, :LANES])
                    dkv_ref[r0:r1, LANES:] = jnp.where(in_head, dv_c.astype(BF16), dkv_ref[r0:r1, LANES:])
                    dkpe_ref[r0:r1, :] += _keep(in_rope, dk_c[:, LANES:])
                else:
                    dqkv_ref[r0:r1, LANES:2 * LANES] = jnp.where(in_head, dk_c.astype(BF16), dqkv_ref[r0:r1, LANES:2 * LANES])
                    dqkv_ref[r0:r1, 2 * LANES:] = jnp.where(in_head, dv_c.astype(BF16), dqkv_ref[r0:r1, 2 * LANES:])
                    dc_ref[h, :, r0:r1] -= _to_row(cs)

        if pe:
            dqn_ref[...] = dq_acc[:, :LANES].astype(BF16)
            dqpe_ref[...] += dq_acc[:, LANES:]
        else:
            dqkv_ref[:, :LANES] = dq_acc[...].astype(BF16)
        if traffic:
            @pl.when(_last_step(bl, PAIRS))
            def _():
                traffic.wait(carried_out, *sems)

    seq = lambda w, col: pl.BlockSpec((s, w), col)
    per_head_row = pl.BlockSpec((2, 1, s), lambda b, p: (b * PAIRS + p, 0, 0))
    if pe:
        in_specs = [seq(LANES, lambda b, p: (b, p)), seq(LANES, lambda b, p: (b, PAIRS + p // 2)),
                    seq(2 * LANES, lambda b, p: (b, p)), seq(LANES, lambda b, p: (b, 0))]
        args = [ops[0], ops[0], ops[1], ops[2]]
    else:
        in_specs = [seq(3 * LANES, lambda b, p: (b, p))]
        args = [ops[0]]
    in_specs += [seq(LANES, lambda b, p: (b, p)), per_head_row, per_head_row]
    args += [do, lse, delta]
    if has_bias:
        in_specs.append(per_head_row)
        args.append(bias)
    scratch = [pltpu.VMEM((s, width), F32)]
    if pe:
        out_specs = (seq(LANES, lambda b, p: (b, p)), seq(2 * LANES, lambda b, p: (b, p)),
                     seq(LANES, lambda b, p: (b, p // 2)), seq(LANES, lambda b, p: (b, 0)))
        out_shape = (jax.ShapeDtypeStruct((t, PAIRS * LANES), BF16), jax.ShapeDtypeStruct((t, PAIRS * 2 * LANES), BF16),
                     jax.ShapeDtypeStruct((t, 2 * LANES), F32), jax.ShapeDtypeStruct((t, LANES), F32))
        scratch.append(pltpu.VMEM((s, 2 * LANES), BF16))
    else:
        out_specs = (seq(3 * LANES, lambda b, p: (b, p)), per_head_row)
        out_shape = (jax.ShapeDtypeStruct((t, PAIRS * 3 * LANES), BF16), jax.ShapeDtypeStruct((bl * HEADS, 1, s), F32))
    if traffic:
        in_specs += traffic.in_specs
        args += traffic.pieces
        out_specs += (traffic.out_spec,)
        out_shape += (traffic.out_shape,)
        scratch += traffic.scratch
    return pl.pallas_call(
        body, name=name, grid=(bl, PAIRS), in_specs=in_specs, out_specs=out_specs, out_shape=out_shape,
        scratch_shapes=scratch, compiler_params=_params("arbitrary" if traffic else "parallel", "arbitrary"),
    )(*args)


def _my_place():
    return lax.axis_index("x"), lax.axis_index("y"), lax.axis_index("c")


def _flip(p, bit):
    return 1 - p if bit else p


def _relative(x, y, c, k):
    return _flip(x, k & 4), _flip(y, k & 2), _flip(c, k & 1)


def _linear(x, y, c):
    return 4 * x + 2 * y + c


class _Traffic:
    def __init__(self, kind, pieces):
        self.kind, self.pieces = kind, list(pieces)
        self.rows = [p.shape[-2] for p in self.pieces]
        self.starts = [sum(self.rows[:i]) for i in range(len(self.rows))]
        anywhere = pl.BlockSpec(memory_space=pl.ANY)
        self.in_specs = [anywhere] * len(self.pieces)
        self.out_spec = anywhere
        self.out_shape = jax.ShapeDtypeStruct((N_DEV, sum(self.rows), self.pieces[0].shape[-1]), self.pieces[0].dtype)
        self.scratch = [pltpu.SemaphoreType.DMA((7,)), pltpu.SemaphoreType.DMA((7,)), pltpu.SemaphoreType.DMA(())]

    def start(self, p_refs, out_ref, send_sems, recv_sems, local_sem):
        x, y, c = _my_place()
        me = _linear(x, y, c)
        mine = lambda i, dev: p_refs[i] if self.kind == "spread" else p_refs[i].at[dev]
        landing = lambda i: out_ref.at[me, pl.ds(self.starts[i], self.rows[i])]
        for i in range(len(p_refs)):
            pltpu.make_async_copy(mine(i, me), landing(i), local_sem).start()
        for k in range(1, N_DEV):
            peer = _relative(x, y, c, k)
            for i in range(len(p_refs)):
                pltpu.make_async_remote_copy(
                    src_ref=mine(i, _linear(*peer)), dst_ref=landing(i),
                    send_sem=send_sems.at[k - 1], recv_sem=recv_sems.at[k - 1], device_id=peer, device_id_type=MESH).start()

    def wait(self, out_ref, send_sems, recv_sems, local_sem):
        x, y, c = _my_place()
        whole = out_ref.at[_linear(x, y, c)]
        for k in range(1, N_DEV):
            both = pltpu.make_async_remote_copy(
                src_ref=whole, dst_ref=whole, send_sem=send_sems.at[k - 1], recv_sem=recv_sems.at[k - 1],
                device_id=_relative(x, y, c, k), device_id_type=MESH)
            both.wait_recv()
            both.wait_send()
        pltpu.make_async_copy(whole, whole, local_sem).wait()


def _sum_blocks(parts, name):
    n, r, cdim = parts.shape
    tr = _tile(r, 640, 16)

    def body(p_ref, o_ref):
        acc = p_ref[0].astype(F32)
        for d in range(1, n):
            acc = acc + p_ref[d].astype(F32)
        o_ref[...] = acc

    return pl.pallas_call(
        body, name=name, grid=(r // tr,), in_specs=[pl.BlockSpec((n, tr, cdim), lambda i: (0, i, 0))],
        out_specs=pl.BlockSpec((tr, cdim), lambda i: (i, 0)), out_shape=jax.ShapeDtypeStruct((r, cdim), F32),
        compiler_params=_params("parallel"),
    )(parts)


def _adamw_math(w, g, m, v):
    m = ADAM_B1 * m + (1.0 - ADAM_B1) * g
    v = ADAM_B2 * v + (1.0 - ADAM_B2) * (g * g)
    m_hat = m / (1.0 - ADAM_B1 ** ADAM_STEP)
    v_hat = v / (1.0 - ADAM_B2 ** ADAM_STEP)
    delta = -ADAM_LR * (m_hat / (jnp.sqrt(v_hat) + ADAM_EPS) + ADAM_WD * w)
    return delta, m, v


def _adamw(w, g, m, v, name):
    def body(w_ref, g_ref, m_ref, v_ref, d_ref, nm_ref, nv_ref):
        d_ref[...], nm_ref[...], nv_ref[...] = _adamw_math(w_ref[...], g_ref[...], m_ref[...], v_ref[...])

    out = jax.ShapeDtypeStruct(w.shape, F32)
    return pl.pallas_call(body, name=name, out_shape=(out, out, out),
                          compiler_params=pltpu.CompilerParams(vmem_limit_bytes=VMEM_LIMIT))(w, g, m, v)


def _small_all_reduce_adamw(parts, loss_part, ws, ms, vs, name):
    sizes = [p.shape[1] for p in parts] + [1]
    spots = [sum(-(-n // LANES) * LANES for n in sizes[:i]) for i in range(len(sizes))]
    width = spots[-1] + LANES
    k = len(parts)

    def reduce_body(*refs):
        p_refs, tot_ref, rows, send_sems, recv_sems = refs[:k + 1], *refs[k + 1:]
        x, y, c = _my_place()
        me = _linear(x, y, c)
        rows[me] = jnp.zeros((1, width), F32)
        for i in range(k + 1):
            rows[me, :, spots[i]:spots[i] + sizes[i]] = p_refs[i][...]
        copies = []
        for rel in range(1, N_DEV):
            copies.append(pltpu.make_async_remote_copy(
                src_ref=rows.at[me], dst_ref=rows.at[me], send_sem=send_sems.at[rel - 1], recv_sem=recv_sems.at[rel - 1],
                device_id=_relative(x, y, c, rel), device_id_type=MESH))
        for cp in copies:
            cp.start()
        for cp in copies:
            cp.wait_recv()
        for cp in copies:
            cp.wait_send()
        total = rows[0]
        for d in range(1, N_DEV):
            total = total + rows[d]
        tot_ref[...] = total

    total = pl.pallas_call(
        reduce_body, name=name, out_shape=jax.ShapeDtypeStruct((1, width), F32),
        scratch_shapes=[pltpu.VMEM((N_DEV, 1, width), F32), pltpu.SemaphoreType.DMA((7,)), pltpu.SemaphoreType.DMA((7,))],
    )(*parts, loss_part)

    def adamw_body(*refs):
        tot_ref, w_refs, m_refs, v_refs, outs = refs[0], refs[1:k + 1], refs[k + 1:2 * k + 1], refs[2 * k + 1:3 * k + 1], refs[3 * k + 1:]
        for i in range(k):
            g = tot_ref[:, spots[i]:spots[i] + sizes[i]]
            outs[4 * i][...] = g
            outs[4 * i + 1][...], outs[4 * i + 2][...], outs[4 * i + 3][...] = _adamw_math(w_refs[i][...], g, m_refs[i][...], v_refs[i][...])
        outs[4 * k][...] = tot_ref[:, spots[k]:spots[k] + 1]

    out_shape = [jax.ShapeDtypeStruct((1, n), F32) for n in sizes[:k] for _ in range(4)] + [jax.ShapeDtypeStruct((1, 1), F32)]
    res = pl.pallas_call(adamw_body, name=name + "_adamw", out_shape=tuple(out_shape))(total, *ws, *ms, *vs)
    return [res[4 * i:4 * i + 4] for i in range(k)], res[4 * k]


def _pad_rows(a, rows):
    return jnp.pad(a, ((0, rows - a.shape[0]), (0, 0)))


def kernel(x, positions, norm_mix_g, w_in, b_fgate, q_norm_g, w_uq, kv_norm_g, w_ukv, fox_out_g, mla_out_g, w_o, norm_ffn_g, w_gate, w_up, w_down, final_norm_g, loss_target, m_norm_mix_g, m_w_in, m_b_fgate, m_q_norm_g, m_w_uq, m_kv_norm_g, m_w_ukv, m_fox_out_g, m_mla_out_g, m_w_o, m_norm_ffn_g, m_w_gate, m_w_up, m_w_down, m_final_norm_g, v_norm_mix_g, v_w_in, v_b_fgate, v_q_norm_g, v_w_uq, v_kv_norm_g, v_w_ukv, v_fox_out_g, v_mla_out_g, v_w_o, v_norm_ffn_g, v_w_gate, v_w_up, v_w_down, v_final_norm_g):
    bl, s, d = x.shape
    t = bl * s
    bh = bl * HEADS
    tq = _tile(s, 256)
    grp = s // LANES
    fw = HEADS * HEAD_DIM
    q_rank, kv_rank = w_uq.shape[1], w_ukv.shape[1]
    in_cols = w_in.shape[2]
    n_in = N_DEV * in_cols
    ff = N_DEV * w_gate.shape[2]
    half = MLA_ROPE // 2
    o_kvlat, o_krope, o_flogit = q_rank, q_rank + kv_rank, q_rank + kv_rank + LANES
    b_cols = -(-(o_flogit + HEADS) // LANES) * LANES

    tr = lambda w: jnp.transpose(w[0])
    in_rows = -(-in_cols // 16) * 16
    uq_rows = w_uq.shape[2] * q_rank // d
    ukv_rows = w_ukv.shape[2] * kv_rank // d
    pieces = [_pad_rows(tr(w_in), in_rows), _pad_rows(tr(w_uq).reshape(uq_rows, d), -(-uq_rows // 16) * 16),
              tr(w_ukv).reshape(ukv_rows, d), w_o[0], tr(w_gate), tr(w_up), w_down[0]]
    pieces = [p.astype(BF16) for p in pieces]
    offs = [0]
    for p in pieces:
        offs.append(offs[-1] + p.shape[0])
    legs = [(0, 1), (1, 5), (5, 7)]
    gathered = {}

    def full(i, rows):
        leg = next(n for n, (lo, hi) in enumerate(legs) if lo <= i < hi)
        base = offs[legs[leg][0]]
        return gathered[leg][:, offs[i] - base:offs[i] - base + rows]

    x2d = x.reshape(t, d)
    h1, gathered[0] = _rmsnorm(x2d, 0, d, norm_mix_g, BF16, "norm_mix", traffic=_Traffic("spread", pieces[0:1]))

    w_in_t = full(0, in_cols).reshape(n_in, d)
    n_qkv = 3 * fw
    w_in_a = w_in_t[:n_qkv].reshape(3, PAIRS, LANES, d).transpose(1, 0, 2, 3).reshape(n_qkv, d)
    lat0, rope0 = n_qkv + HEADS, n_qkv + HEADS + q_rank + kv_rank
    k_rep = jnp.broadcast_to(w_in_t[rope0:].reshape(2, 1, half, d), (2, 4, half, d)).reshape(LANES, d)
    w_in_b = jnp.concatenate([w_in_t[lat0:rope0], k_rep, w_in_t[n_qkv:lat0],
                              jnp.zeros((b_cols - o_flogit - HEADS, d), BF16)], axis=0)

    def per_head_rows(a):
        return a.reshape(bl, s, HEADS).transpose(0, 2, 1).reshape(bh, 1, s)

    proj_a = _matmul(h1, w_in_a, "nt", BF16, "proj_fox", tm=1024, tn=6 * LANES)
    proj_b = _matmul(h1, w_in_b, "nt", F32, "proj_mla", tm=1024, tn=b_cols)

    z = proj_b[:, o_flogit:o_flogit + HEADS].reshape(bl, s, HEADS).transpose(0, 2, 1).reshape(bh * grp, LANES)
    bcol = jnp.broadcast_to(b_fgate.reshape(1, HEADS, 1), (bl, HEADS, grp)).reshape(bh * grp, 1)
    c = _fgate(z, bcol, grp, "forget_gate")
    c_bias = c.reshape(bh, 1, s)
    fox_o, fox_lse, gathered[1] = _attn_fwd((proj_a,), c_bias, HEAD_DIM ** -0.5, bl, s, tq, "fox_attention",
                                            traffic=_Traffic("spread", pieces[legs[1][0]:legs[1][1]]))
    w_uq_h = full(1, uq_rows).reshape(HEADS, MLA_QK, q_rank)
    w_uq_pe = jnp.concatenate([w_uq_h[:, HEAD_DIM:HEAD_DIM + half].reshape(2, 1, 4 * half, q_rank),
                               w_uq_h[:, HEAD_DIM + half:].reshape(2, 1, 4 * half, q_rank)], axis=1).reshape(2 * LANES, q_rank)
    w_uq_p = jnp.concatenate([w_uq_h[:, :HEAD_DIM].reshape(fw, q_rank), w_uq_pe], axis=0)
    w_ukv_p = full(2, ukv_rows).reshape(PAIRS, 2, 2, HEAD_DIM, kv_rank).transpose(0, 2, 1, 3, 4).reshape(2 * fw, kv_rank)
    w_o_f = full(3, w_o.shape[1]).reshape(-1, d)
    w_gate_t = full(4, ff // N_DEV).reshape(ff, d)

    inv_freq = ROPE_THETA ** (-jnp.arange(0, MLA_ROPE, 2, dtype=F32) / MLA_ROPE)
    ang = positions.astype(F32).reshape(t, 1) * inv_freq[None, :]
    cos4, sin4 = jnp.tile(jnp.cos(ang), (1, 4)), jnp.tile(jnp.sin(ang), (1, 4))
    rope_cos, rope_sin = jnp.concatenate([cos4, cos4], axis=1), jnp.concatenate([-sin4, sin4], axis=1)
    qn, kvn, q_all, kv_all, kpe = _mla_prep(proj_b, q_rank, kv_rank, q_norm_g, kv_norm_g, w_uq_p, w_ukv_p, fw,
                                            rope_cos, rope_sin, "mla_prep")
    mla_ops = (q_all, kv_all, kpe)
    mla_o, mla_lse, gathered[2] = _attn_fwd(mla_ops, None, MLA_QK ** -0.5, bl, s, tq, "mla_attention",
                                            traffic=_Traffic("spread", pieces[legs[2][0]:legs[2][1]]))
    w_up_t, w_down_f = full(5, ff // N_DEV).reshape(ff, d), full(6, ff // N_DEV).reshape(ff, d)

    cat = _out_norm(fox_o, mla_o, fox_out_g, mla_out_g, "norm_out")
    both = [(d, F32), (d, BF16)]
    x1, h2 = _rows_matmul([(cat, w_o_f, "nn")], [x2d], [norm_ffn_g], _residual_norm, both, [], "proj_out_norm_ffn")
    gate, up, act = _ffn_up(h2, w_gate_t, w_up_t, "ffn_gate_up")
    dx2, dx2_b, dg_final, loss_part = _rows_matmul(
        [(act, w_down_f, "nn")], [x1, loss_target.reshape(t, d)], [final_norm_g.reshape(1, d)], _residual_loss_bwd,
        both, [d, 1], "ffn_down_final_norm_loss")

    d_gate, d_up = _ffn_down_bwd(dx2_b, w_down_f, gate, up, "d_ffn_down")
    dw_down = _matmul(act, dx2_b, "tn", BF16, "dw_down", tm=ff // 2, tn=d, tk=1024)
    dw_gate = _matmul(d_gate, h2, "tn", BF16, "dw_gate", tm=ff // 2, tn=d, tk=1024)
    dw_up = _matmul(d_up, h2, "tn", BF16, "dw_up", tm=ff // 2, tn=d, tk=1024)
    dx1, dx1_b, dg_ffn = _rows_matmul([(d_gate, w_gate_t, "nn"), (d_up, w_up_t, "nn")], [x1, dx2], [norm_ffn_g],
                                      _norm_bwd_residual, both, [d], "d_ffn_gate_up_norm_ffn", tm=256)
    dw_o = _matmul(cat, dx1_b, "tn", BF16, "dw_o", tn=d, tk=1024)
    d_fox_o, d_mla_o, fox_delta, mla_delta, dg_fox, dg_mla = _rows_matmul(
        [(dx1_b, w_o_f, "nt")], [fox_o, mla_o], [fox_out_g, mla_out_g], _out_norm_bwd,
        [(fw, BF16), (fw, BF16), (HEADS, F32), (HEADS, F32)], [fw, fw], "d_proj_out_norm_out")

    per_dev = lambda a: a.reshape(N_DEV, -1, d)
    late_grads = [per_dev(dw_o), per_dev(dw_gate), per_dev(dw_up), per_dev(dw_down)]
    dproj_a, dc, g_late = _attn_bwd((proj_a,), d_fox_o, fox_lse, per_head_rows(fox_delta),
                                    c_bias, HEAD_DIM ** -0.5, bl, s, tq, "d_fox_attention", traffic=_Traffic("swap", late_grads))
    dz, db_fgate = _fgate_bwd(z, bcol, dc.reshape(bh * grp, LANES), grp, "d_forget_gate")
    d_flogit = dz.reshape(bl, HEADS, s).transpose(0, 2, 1).reshape(t, HEADS)

    dq_nope, dkv_all, dq_pe, dk_pe = _attn_bwd(mla_ops, d_mla_o, mla_lse, per_head_rows(mla_delta),
                                               None, MLA_QK ** -0.5, bl, s, tq, "d_mla_attention")
    d_tail = jnp.pad(d_flogit, ((0, 0), (0, b_cols - o_flogit - HEADS)))
    dproj_b, dq_rot, dg_q, dg_kv = _mla_prep_bwd(dq_nope, dq_pe, dkv_all, dk_pe, d_tail, proj_b, q_rank, kv_rank,
                                                 q_norm_g, kv_norm_g, w_uq_p, w_ukv_p, rope_cos, rope_sin, "d_mla_prep")
    dw_uq_nope = _matmul(dq_nope, qn, "tn", BF16, "dw_uq_nope", tn=q_rank, tk=1024)
    dw_uq_pe = _matmul(dq_rot, qn, "tn", BF16, "dw_uq_rope", tn=q_rank, tk=1024)
    dw_ukv_p = _matmul(dkv_all, kvn, "tn", BF16, "dw_ukv", tn=kv_rank, tk=1024)
    dw_in_a = _matmul(dproj_a, h1, "tn", BF16, "dw_in_fox", tm=6 * LANES, tn=d, tk=1024)
    dw_in_b = _matmul(dproj_b, h1, "tn", F32, "dw_in_mla", tm=b_cols, tn=d, tk=1024)

    dw_krope = dw_in_b[o_krope:o_flogit].reshape(2, 4, half, d).sum(axis=1).reshape(MLA_ROPE, d)
    dw_in_t = jnp.concatenate([dw_in_a.reshape(PAIRS, 3, LANES, d).transpose(1, 0, 2, 3).reshape(n_qkv, d),
                               dw_in_b[o_flogit:o_flogit + HEADS].astype(BF16), dw_in_b[:o_krope].astype(BF16),
                               dw_krope.astype(BF16)], axis=0)
    pad_dev = lambda a, rows: jnp.pad(a, ((0, 0), (0, rows - a.shape[1]), (0, 0)))
    dw_uq_pe5 = dw_uq_pe.reshape(2, 2, 4, half, q_rank)
    dw_uq_h = jnp.concatenate([dw_uq_nope.reshape(HEADS, HEAD_DIM, q_rank), dw_uq_pe5[:, 0].reshape(HEADS, half, q_rank),
                               dw_uq_pe5[:, 1].reshape(HEADS, half, q_rank)], axis=1)
    dw_ukv_h = dw_ukv_p.reshape(PAIRS, 2, 2, HEAD_DIM, kv_rank).transpose(0, 2, 1, 3, 4).reshape(HEADS, 2 * HEAD_DIM, kv_rank)
    n_last = 3
    last_grads = [pad_dev(per_dev(dw_in_t), pieces[0].shape[0]), pad_dev(per_dev(dw_uq_h), pieces[1].shape[0]), per_dev(dw_ukv_h)]
    dh1_fox, g_last = _matmul(dproj_a, w_in_a, "nn", F32, "d_proj_fox", tn=d, traffic=_Traffic("swap", last_grads))
    grad_x, dg_mix = _rows_matmul([(dproj_b, w_in_b, "nn")], [x2d, dx1, dh1_fox], [norm_mix_g], _norm_bwd_residual,
                                  [(d, F32)], [d], "d_proj_mla_norm_mix")
    g_last = _sum_blocks(g_last, "sum_last_grads")
    g_late = _sum_blocks(g_late, "sum_late_grads")

    def mine(i, rows):
        src, base = (g_last, 0) if i < n_last else (g_late, offs[n_last])
        return src[offs[i] - base:offs[i] - base + rows]

    big = [
        ("w_in", w_in, m_w_in, v_w_in, mine(0, in_cols), True),
        ("w_uq", w_uq, m_w_uq, v_w_uq, mine(1, uq_rows).reshape(-1, q_rank), True),
        ("w_ukv", w_ukv, m_w_ukv, v_w_ukv, mine(2, ukv_rows).reshape(-1, kv_rank), True),
        ("w_o", w_o, m_w_o, v_w_o, mine(3, w_o.shape[1]), False),
        ("w_gate", w_gate, m_w_gate, v_w_gate, mine(4, ff // N_DEV), True),
        ("w_up", w_up, m_w_up, v_w_up, mine(5, ff // N_DEV), True),
        ("w_down", w_down, m_w_down, v_w_down, mine(6, ff // N_DEV), False),
    ]
    out = {}
    for nm, w, m, v, g, transposed in big:
        lay = (lambda a: a[0].T) if transposed else (lambda a: a[0])
        back = (lambda a: a.T[None]) if transposed else (lambda a: a[None])
        dl, new_m, new_v = _adamw(lay(w), g, lay(m), lay(v), "adamw_" + nm)
        out[nm] = (back(g), back(dl), back(new_m), back(new_v))

    smalls = [("norm_mix_g", norm_mix_g, m_norm_mix_g, v_norm_mix_g, dg_mix),
              ("b_fgate", b_fgate, m_b_fgate, v_b_fgate, db_fgate.reshape(1, HEADS)),
              ("q_norm_g", q_norm_g, m_q_norm_g, v_q_norm_g, dg_q),
              ("kv_norm_g", kv_norm_g, m_kv_norm_g, v_kv_norm_g, dg_kv),
              ("fox_out_g", fox_out_g, m_fox_out_g, v_fox_out_g, dg_fox),
              ("mla_out_g", mla_out_g, m_mla_out_g, v_mla_out_g, dg_mla),
              ("norm_ffn_g", norm_ffn_g, m_norm_ffn_g, v_norm_ffn_g, dg_ffn),
              ("final_norm_g", final_norm_g, m_final_norm_g, v_final_norm_g, dg_final)]
    flat = lambda a: a.reshape(1, -1)
    results, loss = _small_all_reduce_adamw([e[4] for e in smalls], loss_part, [flat(e[1]) for e in smalls],
                                            [flat(e[2]) for e in smalls], [flat(e[3]) for e in smalls], "reduce_small_adamw")
    for (nm, w, _, _, _), res in zip(smalls, results):
        out[nm] = tuple(a.reshape(w.shape) for a in res)
    loss = loss[0, 0]

    order = ["norm_mix_g", "w_in", "b_fgate", "q_norm_g", "w_uq", "kv_norm_g", "w_ukv", "fox_out_g", "mla_out_g", "w_o",
             "norm_ffn_g", "w_gate", "w_up", "w_down", "final_norm_g"]
    return (loss, grad_x.reshape(bl, s, d), *[out[n][0] for n in order], *[out[n][1] for n in order],
            *[out[n][2] for n in order], *[out[n][3] for n in order])
```

```python
import math

import jax
import jax.numpy as jnp
from jax import lax
from jax.experimental import pallas as pl
from jax.experimental.pallas import tpu as pltpu

F32 = jnp.float32
BF16 = jnp.bfloat16
MESH = pl.DeviceIdType.MESH

N_DEV = 8
HEADS = 8
HEAD_DIM = 64
PAIRS = HEADS // 2
MLA_ROPE = 32
MLA_QK = HEAD_DIM + MLA_ROPE
ROPE_THETA = 10000.0
NORM_EPS = 1e-6
ADAM_LR, ADAM_B1, ADAM_B2, ADAM_EPS, ADAM_WD, ADAM_STEP = 0.001, 0.9, 0.999, 1e-08, 0.01, 10

LANES = 128
MASKED = -1e30
VMEM_LIMIT = 48 * 1024 * 1024

_DIMS = {"nn": (((1,), (0,)), ((), ())), "nt": (((1,), (1,)), ((), ())), "tn": (((0,), (0,)), ((), ()))}


def _params(*sem):
    return pltpu.CompilerParams(dimension_semantics=sem, vmem_limit_bytes=VMEM_LIMIT)


def _dot(a, b, mode):
    return lax.dot_general(a.astype(BF16), b.astype(BF16), _DIMS[mode], preferred_element_type=F32)


def _tile(n, pref, unit=8):
    if n <= pref:
        return n
    t = pref - pref % unit
    while n % t:
        t -= unit
    return t


def _log2(n):
    assert n & (n - 1) == 0
    return n.bit_length() - 1


def _matmul(a, b, mode, out_dtype, name, tm=512, tn=512, tk=None, res=None, traffic=None):
    if mode == "nn":
        (m, kd), n = a.shape, b.shape[1]
    elif mode == "nt":
        (m, kd), n = a.shape, b.shape[0]
    else:
        (kd, m), n = a.shape, b.shape[1]
    tm, tn = _tile(m, tm, LANES if mode == "tn" else 16), _tile(n, tn, LANES)
    tk = kd if tk is None else _tile(kd, tk, LANES)
    nk = kd // tk
    a_spec = pl.BlockSpec((tk, tm), lambda i, j, k: (k, i)) if mode == "tn" else pl.BlockSpec((tm, tk), lambda i, j, k: (i, k))
    b_spec = pl.BlockSpec((tn, tk), lambda i, j, k: (j, k)) if mode == "nt" else pl.BlockSpec((tk, tn), lambda i, j, k: (k, j))
    o_spec = pl.BlockSpec((tm, tn), lambda i, j, k: (i, j))
    has_res = res is not None
    n_carried = len(traffic.pieces) if traffic else 0
    grid = (m // tm, n // tn, nk)

    def body(*refs):
        a_ref, b_ref = refs[:2]
        r_ref = refs[2] if has_res else None
        n_in = 2 + has_res + n_carried
        o_ref = refs[n_in]
        step = [pl.program_id(axis) for axis in range(3)]
        if traffic:
            carried_in, carried_out, sems = refs[2 + has_res:n_in], refs[n_in + 1], refs[len(refs) - 3:]

            @pl.when((step[0] == 0) & (step[1] == 0) & (step[2] == 0))
            def _():
                traffic.start(carried_in, carried_out, *sems)

        def finish(acc):
            if has_res:
                acc = acc + r_ref[...]
            o_ref[...] = acc.astype(out_dtype)

        part = _dot(a_ref[...], b_ref[...], mode)
        if nk == 1:
            finish(part)
        else:
            acc_ref = refs[n_in + 1 + bool(traffic)]

            @pl.when(step[2] == 0)
            def _():
                acc_ref[...] = part

            @pl.when(step[2] > 0)
            def _():
                acc_ref[...] += part

            @pl.when(step[2] == nk - 1)
            def _():
                finish(acc_ref[...])

        if traffic:
            @pl.when((step[0] == grid[0] - 1) & (step[1] == grid[1] - 1) & (step[2] == nk - 1))
            def _():
                traffic.wait(carried_out, *sems)

    in_specs = [a_spec, b_spec] + ([o_spec] if has_res else [])
    out_specs, out_shape = [o_spec], [jax.ShapeDtypeStruct((m, n), out_dtype)]
    scratch = [pltpu.VMEM((tm, tn), F32)] if nk > 1 else []
    if traffic:
        in_specs += traffic.in_specs
        out_specs.append(traffic.out_spec)
        out_shape.append(traffic.out_shape)
        scratch += traffic.scratch
    out = pl.pallas_call(
        body, name=name, grid=grid, in_specs=in_specs, out_specs=tuple(out_specs), out_shape=tuple(out_shape),
        scratch_shapes=scratch,
        compiler_params=_params(*(("arbitrary",) * 3 if traffic else ("parallel", "parallel", "arbitrary"))),
    )(*([a, b] + ([res] if has_res else []) + (traffic.pieces if traffic else [])))
    return out if traffic else out[0]


def _rstd(x):
    return lax.rsqrt(jnp.mean(x * x, axis=-1, keepdims=True) + NORM_EPS)


def _norm_bwd(x, g, dy):
    r = _rstd(x)
    xh = x * r
    u = dy * g
    dx = r * (u - xh * jnp.mean(u * xh, axis=-1, keepdims=True))
    return dx, jnp.sum(dy * xh, axis=0, keepdims=True)


def _rmsnorm(x, col, width, g, out_dtype, name, traffic=None):
    t = x.shape[0]
    tm = _tile(t, 512)
    steps = t // tm
    n_carried = len(traffic.pieces) if traffic else 0

    def body(*refs):
        x_ref, g_ref, o_ref = refs[0], refs[1], refs[2 + n_carried]
        if traffic:
            carried_in, carried_out, sems = refs[2:2 + n_carried], refs[3 + n_carried], refs[4 + n_carried:]

            @pl.when(pl.program_id(0) == 0)
            def _():
                traffic.start(carried_in, carried_out, *sems)

        xv = x_ref[...]
        o_ref[...] = ((xv * _rstd(xv)) * g_ref[...]).astype(out_dtype)
        if traffic:
            @pl.when(pl.program_id(0) == steps - 1)
            def _():
                traffic.wait(carried_out, *sems)

    in_specs = [pl.BlockSpec((tm, width), lambda i: (i, col)), pl.BlockSpec((1, width), lambda i: (0, 0))]
    out_specs = [pl.BlockSpec((tm, width), lambda i: (i, 0))]
    out_shape = [jax.ShapeDtypeStruct((t, width), out_dtype)]
    if traffic:
        in_specs += traffic.in_specs
        out_specs.append(traffic.out_spec)
        out_shape.append(traffic.out_shape)
    out = pl.pallas_call(
        body, name=name, grid=(steps,), in_specs=in_specs, out_specs=tuple(out_specs), out_shape=tuple(out_shape),
        scratch_shapes=traffic.scratch if traffic else [],
        compiler_params=_params("arbitrary" if traffic else "parallel"),
    )(x, g, *(traffic.pieces if traffic else []))
    return out if traffic else out[0]


def _split3(x):
    hi = x.astype(BF16)
    r1 = x - hi.astype(F32)
    mid = r1.astype(BF16)
    lo = (r1 - mid.astype(F32)).astype(BF16)
    return hi, mid, lo


def _dot_x01(x, m01):
    hi, mid, lo = _split3(x)
    d = lambda p: lax.dot_general(p, m01, _DIMS["nn"], preferred_element_type=F32)
    return (d(lo) + d(mid)) + d(hi)


def _dot_01x(m01, x):
    hi, mid, lo = _split3(x)
    d = lambda p: lax.dot_general(m01, p, _DIMS["nn"], preferred_element_type=F32)
    return (d(lo) + d(mid)) + d(hi)


def _rows_matmul(terms, rows_in, vecs_in, epilogue, rows_out, sums_out, name, tm=512, prologue=None):
    t = rows_in[0].shape[0]
    tm = _tile(t, tm, 16)
    n_rows, n_vecs = len(rows_in), len(vecs_in)
    n_ab = sum(1 + (a is not None) for a, _, _ in terms)

    def body(*refs):
        row_blocks = [r[...] for r in refs[n_ab:n_ab + n_rows]]
        vecs = [r[...] for r in refs[n_ab + n_rows:n_ab + n_rows + n_vecs]]
        made = prologue(row_blocks, vecs) if prologue else None
        acc, at = None, 0
        for a, _, mode in terms:
            lhs = made if a is None else refs[at][...]
            at += a is not None
            part = _dot(lhs, refs[at][...], mode)
            at += 1
            acc = part if acc is None else acc + part
        row_vals, sum_vals = epilogue(acc, row_blocks, vecs)
        if prologue:
            row_vals = [made] + row_vals
        at = n_ab + n_rows + n_vecs
        for ref, val, (_, dtype) in zip(refs[at:], row_vals, rows_out):
            ref[...] = val.astype(dtype)
        sum_refs = refs[at + len(rows_out):]

        @pl.when(pl.program_id(0) == 0)
        def _():
            for ref in sum_refs:
                ref[...] = jnp.zeros_like(ref)

        for ref, val in zip(sum_refs, sum_vals):
            ref[...] += val

    rows = lambda w: pl.BlockSpec((tm, w), lambda i: (i, 0))
    whole = lambda a: pl.BlockSpec(a.shape, lambda i: (0, 0))
    in_specs, args = [], []
    for a, b, _ in terms:
        in_specs += ([rows(a.shape[1])] if a is not None else []) + [whole(b)]
        args += ([a] if a is not None else []) + [b]
    in_specs += [rows(r.shape[1]) for r in rows_in] + [whole(v) for v in vecs_in]
    args += list(rows_in) + list(vecs_in)
    return pl.pallas_call(
        body, name=name, grid=(t // tm,), in_specs=in_specs,
        out_specs=tuple([rows(w) for w, _ in rows_out] + [pl.BlockSpec((1, w), lambda i: (0, 0)) for w in sums_out]),
        out_shape=tuple([jax.ShapeDtypeStruct((t, w), dt) for w, dt in rows_out] + [jax.ShapeDtypeStruct((1, w), F32) for w in sums_out]),
        compiler_params=_params("arbitrary"),
    )(*args)


def _out_norm(rows, vecs):
    (f, m), (gf, gm) = rows[:2], vecs[:2]
    return jnp.concatenate([((f * _rstd(f)) * gf).astype(BF16), ((m * _rstd(m)) * gm).astype(BF16)], axis=1)


def _residual_norm(acc, rows, vecs):
    x1 = rows[-1] + acc
    return [x1, (x1 * _rstd(x1)) * vecs[-1]], []


def _residual_loss_bwd(acc, rows, vecs):
    x2, gv = rows[0] + acc, vecs[0]
    diff = (x2 * _rstd(x2)) * gv - rows[1]
    dx, dg = _norm_bwd(x2, gv, diff / x2.shape[1])
    return [dx, dx], [dg, 0.5 * jnp.sum(jnp.mean(diff * diff, axis=-1, keepdims=True), axis=0, keepdims=True)]


def _norm_bwd_residual(acc, rows, vecs):
    dy = acc + rows[2] if len(rows) > 2 else acc
    dx, dg = _norm_bwd(rows[0], vecs[0], dy)
    if len(rows) > 1:
        dx = dx + rows[1]
    return [dx, dx], [dg]


def _out_norm_bwd(acc, rows, vecs):
    (f, m), w = rows, rows[0].shape[1]
    nh = w // HEAD_DIM
    lane_head = lax.shift_right_logical(lax.broadcasted_iota(jnp.int32, (w, nh), 0), _log2(HEAD_DIM))
    sel = (lane_head == lax.broadcasted_iota(jnp.int32, (w, nh), 1)).astype(BF16)
    dfo, dgf = _norm_bwd(f, vecs[0], acc[:, :w])
    dmo, dgm = _norm_bwd(m, vecs[1], acc[:, w:])
    return [dfo, dmo, _dot_x01(dfo * f, sel), _dot_x01(dmo * m, sel)], [dgf, dgm]


def _ffn_up(h, wg_t, wu_t, name, tm=512, tf=1408):
    t, d = h.shape
    f = wg_t.shape[0]
    tm, tf = _tile(t, tm, 16), _tile(f, tf, LANES)
    tok = pl.BlockSpec((tm, tf), lambda i, j: (i, j))
    wt = pl.BlockSpec((tf, d), lambda i, j: (j, 0))

    def body(h_ref, wg_ref, wu_ref, dg_ref, du_ref, a_ref):
        hv = h_ref[...]
        g, u = _dot(hv, wg_ref[...], "nt"), _dot(hv, wu_ref[...], "nt")
        sg = jax.nn.sigmoid(g)
        silu = g * sg
        dg_ref[...] = (u * (sg * (1.0 + g * (1.0 - sg)))).astype(BF16)
        du_ref[...] = silu.astype(BF16)
        a_ref[...] = (silu * u).astype(BF16)

    return pl.pallas_call(
        body, name=name, grid=(t // tm, f // tf), in_specs=[pl.BlockSpec((tm, d), lambda i, j: (i, 0)), wt, wt],
        out_specs=(tok, tok, tok),
        out_shape=(jax.ShapeDtypeStruct((t, f), BF16), jax.ShapeDtypeStruct((t, f), BF16), jax.ShapeDtypeStruct((t, f), BF16)),
        compiler_params=_params("parallel", "parallel"),
    )(h, wg_t, wu_t)


def _ffn_down_bwd(dy, w_down, act_by_gate, act_by_up, name, tm=512, tf=1408):
    t, d = dy.shape
    f = w_down.shape[0]
    tm, tf = _tile(t, tm, 16), _tile(f, tf, LANES)
    tok = pl.BlockSpec((tm, tf), lambda i, j: (i, j))

    def body(dy_ref, w_ref, g_ref, u_ref, dg_ref, du_ref):
        da = _dot(dy_ref[...], w_ref[...], "nt")
        dg_ref[...] = (da * g_ref[...].astype(F32)).astype(BF16)
        du_ref[...] = (da * u_ref[...].astype(F32)).astype(BF16)

    return pl.pallas_call(
        body, name=name, grid=(t // tm, f // tf),
        in_specs=[pl.BlockSpec((tm, d), lambda i, j: (i, 0)), pl.BlockSpec((tf, d), lambda i, j: (j, 0)), tok, tok],
        out_specs=(tok, tok),
        out_shape=(jax.ShapeDtypeStruct((t, f), BF16), jax.ShapeDtypeStruct((t, f), BF16)),
        compiler_params=_params("parallel", "parallel"),
    )(dy, w_down, act_by_gate, act_by_up)


def _chunk_scan_mats(rows, grp, reverse):
    ii = lax.broadcasted_iota(jnp.int32, (LANES, LANES), 0)
    jj = lax.broadcasted_iota(jnp.int32, (LANES, LANES), 1)
    within = ((ii >= jj) if reverse else (ii <= jj)).astype(BF16)
    ones = jnp.ones((LANES, LANES), BF16)
    ri = lax.broadcasted_iota(jnp.int32, (rows, rows), 0)
    rj = lax.broadcasted_iota(jnp.int32, (rows, rows), 1)
    sh = _log2(grp)
    same = lax.shift_right_logical(ri, sh) == lax.shift_right_logical(rj, sh)
    across = (same & ((rj > ri) if reverse else (rj < ri))).astype(BF16)
    return within, ones, across


def _running_sum(v, mats):
    within, ones, across = mats
    return _dot_x01(v, within) + _dot_01x(across, _dot_x01(v, ones))


def _fgate(z, bcol, grp, name):
    rows = z.shape[0]

    def body(z_ref, b_ref, c_ref):
        zz = z_ref[...] + b_ref[...]
        log_f = jnp.minimum(zz, 0.0) - jnp.log1p(jnp.exp(-jnp.abs(zz)))
        c_ref[...] = _running_sum(log_f, _chunk_scan_mats(rows, grp, False))

    return pl.pallas_call(body, name=name, out_shape=jax.ShapeDtypeStruct(z.shape, F32),
                          compiler_params=pltpu.CompilerParams(vmem_limit_bytes=VMEM_LIMIT))(z, bcol)


def _fgate_bwd(z, bcol, dc, grp, name):
    rows = z.shape[0]

    def body(z_ref, b_ref, dc_ref, dz_ref, db_ref):
        zz = z_ref[...] + b_ref[...]
        dz = _running_sum(dc_ref[...], _chunk_scan_mats(rows, grp, True)) * jax.nn.sigmoid(-zz)
        dz_ref[...] = dz
        head = lax.shift_right_logical(lax.broadcasted_iota(jnp.int32, (HEADS, rows), 1), _log2(grp)) & (HEADS - 1)
        sel = (head == lax.broadcasted_iota(jnp.int32, (HEADS, rows), 0)).astype(BF16)
        db_ref[...] = jnp.sum(_dot_01x(sel, dz), axis=1, keepdims=True)

    return pl.pallas_call(
        body, name=name,
        out_shape=(jax.ShapeDtypeStruct(z.shape, F32), jax.ShapeDtypeStruct((HEADS, 1), F32)),
        compiler_params=pltpu.CompilerParams(vmem_limit_bytes=VMEM_LIMIT),
    )(z, bcol, dc)


def _rotate(x, cs, sn_signed):
    return x * cs + pltpu.roll(x, LANES // 2, axis=1) * sn_signed


def _mla_prep(proj_b, q_rank, kv_rank, gq, gkv, w_uq_p, w_ukv_p, nope, cs, sn, name):
    t, bw = proj_b.shape
    qw, kvw = w_uq_p.shape[0], w_ukv_p.shape[0]
    tm = _tile(t, 512)
    rows = lambda w: pl.BlockSpec((tm, w), lambda i: (i, 0))
    whole = lambda a: pl.BlockSpec(a.shape, lambda i: (0, 0))

    def body(pb_ref, gq_ref, gkv_ref, wq_ref, wkv_ref, c_ref, s_ref, qn_ref, kvn_ref, q_ref, kv_ref, kpe_ref):
        c, s = c_ref[...], s_ref[...]
        ql, kvl = pb_ref[:, :q_rank], pb_ref[:, q_rank:q_rank + kv_rank]
        qn = ((ql * _rstd(ql)) * gq_ref[...]).astype(BF16)
        kvn = ((kvl * _rstd(kvl)) * gkv_ref[...]).astype(BF16)
        qn_ref[...], kvn_ref[...] = qn, kvn
        q_raw = _dot(qn, wq_ref[...], "nt")
        q_ref[:, :nope] = q_raw[:, :nope].astype(BF16)
        for off in range(nope, qw, LANES):
            q_ref[:, off:off + LANES] = _rotate(q_raw[:, off:off + LANES], c, s).astype(BF16)
        kv_ref[...] = _dot(kvn, wkv_ref[...], "nt").astype(BF16)
        kpe_ref[...] = _rotate(pb_ref[:, q_rank + kv_rank:q_rank + kv_rank + LANES], c, s).astype(BF16)

    return pl.pallas_call(
        body, name=name, grid=(t // tm,),
        in_specs=[rows(bw), whole(gq), whole(gkv), whole(w_uq_p), whole(w_ukv_p), rows(LANES), rows(LANES)],
        out_specs=(rows(q_rank), rows(kv_rank), rows(qw), rows(kvw), rows(LANES)),
        out_shape=(jax.ShapeDtypeStruct((t, q_rank), BF16), jax.ShapeDtypeStruct((t, kv_rank), BF16),
                   jax.ShapeDtypeStruct((t, qw), BF16), jax.ShapeDtypeStruct((t, kvw), BF16), jax.ShapeDtypeStruct((t, LANES), BF16)),
        compiler_params=_params("parallel"),
    )(proj_b, gq, gkv, w_uq_p, w_ukv_p, cs, sn)


def _mla_prep_bwd(dq_nope, dq_pe, dkv_all, dk_pe, d_tail, proj_b, q_rank, kv_rank, gq, gkv, w_uq_p, w_ukv_p, cs, sn, name):
    t, bw = proj_b.shape
    nope, pw = dq_nope.shape[1], dq_pe.shape[1]
    tm = _tile(t, 512)
    rows = lambda w: pl.BlockSpec((tm, w), lambda i: (i, 0))
    whole = lambda a: pl.BlockSpec(a.shape, lambda i: (0, 0))
    o_k = q_rank + kv_rank

    def body(dqn_ref, dqp_ref, dkv_ref, dkp_ref, dt_ref, pb_ref, gq_ref, gkv_ref, wq_ref, wkv_ref, c_ref, s_ref,
             dpb_ref, dqr_ref, dgq_ref, dgkv_ref):
        c, s = c_ref[...], -s_ref[...]
        for off in range(0, pw, LANES):
            dqr_ref[:, off:off + LANES] = _rotate(dqp_ref[:, off:off + LANES], c, s).astype(BF16)
        d_qn = _dot(dqn_ref[...], wq_ref[:nope, :], "nn") + _dot(dqr_ref[...], wq_ref[nope:, :], "nn")
        dq_lat, dgq = _norm_bwd(pb_ref[:, :q_rank], gq_ref[...], d_qn)
        dkv_lat, dgkv = _norm_bwd(pb_ref[:, q_rank:o_k], gkv_ref[...], _dot(dkv_ref[...], wkv_ref[...], "nn"))
        dpb_ref[:, :q_rank] = dq_lat.astype(BF16)
        dpb_ref[:, q_rank:o_k] = dkv_lat.astype(BF16)
        dpb_ref[:, o_k:o_k + LANES] = _rotate(dkp_ref[...], c, s).astype(BF16)
        dpb_ref[:, o_k + LANES:] = dt_ref[...].astype(BF16)

        @pl.when(pl.program_id(0) == 0)
        def _():
            dgq_ref[...] = jnp.zeros_like(dgq_ref)
            dgkv_ref[...] = jnp.zeros_like(dgkv_ref)

        dgq_ref[...] += dgq
        dgkv_ref[...] += dgkv

    return pl.pallas_call(
        body, name=name, grid=(t // tm,),
        in_specs=[rows(nope), rows(pw), rows(dkv_all.shape[1]), rows(LANES), rows(bw - o_k - LANES), rows(bw), whole(gq), whole(gkv),
                  whole(w_uq_p), whole(w_ukv_p), rows(LANES), rows(LANES)],
        out_specs=(rows(bw), rows(pw), whole(gq), whole(gkv)),
        out_shape=(jax.ShapeDtypeStruct((t, bw), BF16), jax.ShapeDtypeStruct((t, pw), BF16),
                   jax.ShapeDtypeStruct(gq.shape, F32), jax.ShapeDtypeStruct(gkv.shape, F32)),
        compiler_params=_params("arbitrary"),
    )(dq_nope, dq_pe, dkv_all, dk_pe, d_tail, proj_b, gq, gkv, w_uq_p, w_ukv_p, cs, sn)


def _lane_masks(pair, h, pe):
    lane = lax.broadcasted_iota(jnp.int32, (1, LANES), 1)
    in_head = lax.shift_right_logical(lane, _log2(HEAD_DIM)) == h
    in_rope = ((lax.shift_right_logical(lane, _log2(MLA_ROPE // 2)) & 3) == ((2 * pair + h) & 3)) if pe else None
    return in_head, in_rope


def _keep(mask, v):
    return jnp.where(mask, v, jnp.zeros_like(v))


def _to_row(col):
    n = col.shape[0]
    eye = lax.broadcasted_iota(jnp.int32, (n, n), 0) == lax.broadcasted_iota(jnp.int32, (n, n), 1)
    return jnp.sum(jnp.where(eye, col, 0.0), axis=0, keepdims=True)


def _to_col(row):
    n = row.shape[1]
    eye = lax.broadcasted_iota(jnp.int32, (n, n), 0) == lax.broadcasted_iota(jnp.int32, (n, n), 1)
    return jnp.sum(jnp.where(eye, row, 0.0), axis=1, keepdims=True)


def _first_step():
    return (pl.program_id(0) == 0) & (pl.program_id(1) == 0)


def _last_step(n0, n1):
    return (pl.program_id(0) == n0 - 1) & (pl.program_id(1) == n1 - 1)


def _attn_fwd(ops, bias, scale, bl, s, tq, name, traffic=None):
    pe = len(ops) == 3
    has_bias = bias is not None
    exact_scale = math.frexp(scale)[0] == 0.5
    nq = s // tq
    t = bl * s
    n_carried = len(traffic.pieces) if traffic else 0

    def body(*refs):
        sems = refs[len(refs) - 3:] if traffic else ()
        if pe:
            q_ref, qpe_ref, kv_ref, kpe_ref = refs[:4]
            n_in = 4
            q_at = lambda r0, r1: q_ref[r0:r1, :]
            v_at = lambda r0, r1: kv_ref[r0:r1, LANES:]
            kcat = refs[len(refs) - 1 - len(sems)]
            kcat[:, :LANES] = kv_ref[:, :LANES]
            kcat[:, LANES:] = kpe_ref[...]
            k_at = lambda r0, r1: kcat[r0:r1, :]
        else:
            qkv_ref = refs[0]
            n_in = 1
            q_at = lambda r0, r1: qkv_ref[r0:r1, :LANES]
            k_at = lambda r0, r1: qkv_ref[r0:r1, LANES:2 * LANES]
            v_at = lambda r0, r1: qkv_ref[r0:r1, 2 * LANES:]
        if has_bias:
            c_ref = refs[n_in]
            n_in += 1
        carried_in = refs[n_in:n_in + n_carried]
        n_in += n_carried
        o_ref, lse_ref = refs[n_in:n_in + 2]
        if traffic:
            carried_out = refs[n_in + 2]

            @pl.when(_first_step())
            def _():
                traffic.start(carried_in, carried_out, *sems)

        pair = pl.program_id(1)
        causal = lax.broadcasted_iota(jnp.int32, (tq, tq), 1) <= lax.broadcasted_iota(jnp.int32, (tq, tq), 0)
        o_ref[...] = jnp.zeros_like(o_ref)

        for i in range(nq):
            for h in range(2):
                in_head, in_rope = _lane_masks(pair, h, pe)
                r0, r1 = i * tq, (i + 1) * tq
                qm = _keep(in_head, q_at(r0, r1))
                if pe:
                    qm = jnp.concatenate([qm, _keep(in_rope, qpe_ref[r0:r1, :])], axis=1)
                if exact_scale:
                    qm = qm * scale

                def logits(k0, k1):
                    sc = _dot(qm, k_at(k0, k1), "nt")
                    if not exact_scale:
                        sc = sc * scale
                    if has_bias:
                        sc = sc - c_ref[h, :, k0:k1]
                    return sc

                sd = jnp.where(causal, logits(r0, r1), MASKED)
                m = jnp.max(sd, axis=1, keepdims=True)
                if i:
                    so = logits(0, r0)
                    m = jnp.maximum(m, jnp.max(so, axis=1, keepdims=True))
                pd = jnp.exp(sd - m)
                l = jnp.sum(pd, axis=1, keepdims=True)
                acc = _dot(pd, v_at(r0, r1), "nn")
                if i:
                    po = jnp.exp(so - m)
                    l = l + jnp.sum(po, axis=1, keepdims=True)
                    acc = acc + _dot(po, v_at(0, r0), "nn")
                o_ref[r0:r1, :] = jnp.where(in_head, acc / l, o_ref[r0:r1, :])
                lse = _to_row(m + jnp.log(l))
                lse_ref[h, :, r0:r1] = lse + c_ref[h, :, r0:r1] if has_bias else lse

        if traffic:
            @pl.when(_last_step(bl, PAIRS))
            def _():
                traffic.wait(carried_out, *sems)

    seq = lambda w, col: pl.BlockSpec((s, w), col)
    if pe:
        in_specs = [seq(LANES, lambda b, p: (b, p)), seq(LANES, lambda b, p: (b, PAIRS + p // 2)),
                    seq(2 * LANES, lambda b, p: (b, p)), seq(LANES, lambda b, p: (b, 0))]
        args = [ops[0], ops[0], ops[1], ops[2]]
        scratch = [pltpu.VMEM((s, 2 * LANES), BF16)]
    else:
        in_specs = [seq(3 * LANES, lambda b, p: (b, p))]
        args = [ops[0]]
        scratch = []
    per_head_row = pl.BlockSpec((2, 1, s), lambda b, p: (b * PAIRS + p, 0, 0))
    if has_bias:
        in_specs.append(per_head_row)
        args.append(bias)
    out_specs = [seq(LANES, lambda b, p: (b, p)), per_head_row]
    out_shape = [jax.ShapeDtypeStruct((t, HEADS * HEAD_DIM), F32), jax.ShapeDtypeStruct((bl * HEADS, 1, s), F32)]
    if traffic:
        in_specs += traffic.in_specs
        args += traffic.pieces
        out_specs.append(traffic.out_spec)
        out_shape.append(traffic.out_shape)
        scratch += traffic.scratch
    return pl.pallas_call(
        body, name=name, grid=(bl, PAIRS), in_specs=in_specs, out_specs=tuple(out_specs), out_shape=tuple(out_shape),
        scratch_shapes=scratch, compiler_params=_params(*(("arbitrary", "arbitrary") if traffic else ("parallel", "parallel"))),
    )(*args)


def _attn_bwd(ops, do, lse, delta, bias, scale, bl, s, tq, name, traffic=None):
    pe = len(ops) == 3
    has_bias = bias is not None
    nq = s // tq
    t = bl * s
    width = 2 * LANES if pe else LANES
    n_carried = len(traffic.pieces) if traffic else 0

    def body(*refs):
        if pe:
            q_ref, qpe_ref, kv_ref, kpe_ref = refs[:4]
            n_in = 4
            k_at = lambda r0, r1: kv_ref[r0:r1, :LANES]
            v_at = lambda r0, r1: kv_ref[r0:r1, LANES:]
        else:
            qkv_ref = refs[0]
            n_in = 1
            k_at = lambda r0, r1: qkv_ref[r0:r1, LANES:2 * LANES]
            v_at = lambda r0, r1: qkv_ref[r0:r1, 2 * LANES:]
        do_ref, lse_ref, dl_ref = refs[n_in:n_in + 3]
        n_in += 3
        if has_bias:
            c_ref = refs[n_in]
            n_in += 1
        carried_in = refs[n_in:n_in + n_carried]
        rest = refs[n_in + n_carried:]
        if traffic:
            rest, sems = rest[:-3], rest[-3:]
            carried_out = rest[4 if pe else 2]
            rest = rest[:4 if pe else 2] + rest[(4 if pe else 2) + 1:]

            @pl.when(_first_step())
            def _():
                traffic.start(carried_in, carried_out, *sems)

        if pe:
            dqn_ref, dkv_ref, dqpe_ref, dkpe_ref, dq_acc, qcat = rest
            qcat[:, :LANES] = q_ref[...]
            qcat[:, LANES:] = qpe_ref[...]
            q_at = lambda r0, r1: qcat[r0:r1, :]
            dkv_ref[...] = jnp.zeros_like(dkv_ref)
        else:
            dqkv_ref, dc_ref, dq_acc = rest
            q_at = lambda r0, r1: qkv_ref[r0:r1, :LANES]
            dqkv_ref[...] = jnp.zeros_like(dqkv_ref)
            dc_ref[...] = jnp.zeros_like(dc_ref)
        pair = pl.program_id(1)
        dq_acc[...] = jnp.zeros_like(dq_acc)
        causal = lax.broadcasted_iota(jnp.int32, (tq, tq), 1) >= lax.broadcasted_iota(jnp.int32, (tq, tq), 0)
        if pe:
            @pl.when(pair == 0)
            def _():
                dkpe_ref[...] = jnp.zeros_like(dkpe_ref)

            @pl.when(pair % 2 == 0)
            def _():
                dqpe_ref[...] = jnp.zeros_like(dqpe_ref)

        for j in range(nq):
            for h in range(2):
                in_head, in_rope = _lane_masks(pair, h, pe)
                r0, r1 = j * tq, (j + 1) * tq
                kt = _keep(in_head, k_at(r0, r1))
                if pe:
                    kt = jnp.concatenate([kt, _keep(in_rope, kpe_ref[r0:r1, :])], axis=1)
                vt = _keep(in_head, v_at(r0, r1))
                ck = _to_col(c_ref[h, :, r0:r1]) if has_bias else None

                def block(q0, q1, diagonal):
                    qq, dd = q_at(q0, q1), do_ref[q0:q1, :]
                    st = _dot(kt, qq, "nt") * scale
                    if has_bias:
                        st = st + (c_ref[h, :, q0:q1] - ck)
                    if diagonal:
                        st = jnp.where(causal, st, MASKED)
                    pt = jnp.exp(st - lse_ref[h, :, q0:q1])
                    dst = pt * (_dot(vt, dd, "nt") - dl_ref[h, :, q0:q1])
                    dsb = (dst * scale).astype(BF16)
                    dq_acc[q0:q1, :] += _dot(dsb, kt, "tn")
                    if has_bias:
                        dc_ref[h, :, q0:q1] += jnp.sum(dst, axis=0, keepdims=True)
                    return _dot(pt, dd, "nn"), _dot(dsb, qq, "nn"), (jnp.sum(dst, axis=1, keepdims=True) if has_bias else None)

                dv_c, dk_c, cs = block(r0, r1, True)
                if r1 < s:
                    dv_o, dk_o, cs_o = block(r1, s, False)
                    dv_c, dk_c = dv_c + dv_o, dk_c + dk_o
                    cs = cs + cs_o if has_bias else None
                if pe:
                    dkv_ref[r0:r1, :LANES] = jnp.where(in_head, dk_c[:, :LANES].astype(BF16), dkv_ref[r0:r1, :LANES])
                    dkv_ref[r0:r1, LANES:] = jnp.where(in_head, dv_c.astype(BF16), dkv_ref[r0:r1, LANES:])
                    dkpe_ref[r0:r1, :] += _keep(in_rope, dk_c[:, LANES:])
                else:
                    dqkv_ref[r0:r1, LANES:2 * LANES] = jnp.where(in_head, dk_c.astype(BF16), dqkv_ref[r0:r1, LANES:2 * LANES])
                    dqkv_ref[r0:r1, 2 * LANES:] = jnp.where(in_head, dv_c.astype(BF16), dqkv_ref[r0:r1, 2 * LANES:])
                    dc_ref[h, :, r0:r1] -= _to_row(cs)

        if pe:
            dqn_ref[...] = dq_acc[:, :LANES].astype(BF16)
            dqpe_ref[...] += dq_acc[:, LANES:]
        else:
            dqkv_ref[:, :LANES] = dq_acc[...].astype(BF16)
        if traffic:
            @pl.when(_last_step(bl, PAIRS))
            def _():
                traffic.wait(carried_out, *sems)

    seq = lambda w, col: pl.BlockSpec((s, w), col)
    per_head_row = pl.BlockSpec((2, 1, s), lambda b, p: (b * PAIRS + p, 0, 0))
    if pe:
        in_specs = [seq(LANES, lambda b, p: (b, p)), seq(LANES, lambda b, p: (b, PAIRS + p // 2)),
                    seq(2 * LANES, lambda b, p: (b, p)), seq(LANES, lambda b, p: (b, 0))]
        args = [ops[0], ops[0], ops[1], ops[2]]
    else:
        in_specs = [seq(3 * LANES, lambda b, p: (b, p))]
        args = [ops[0]]
    in_specs += [seq(LANES, lambda b, p: (b, p)), per_head_row, per_head_row]
    args += [do, lse, delta]
    if has_bias:
        in_specs.append(per_head_row)
        args.append(bias)
    scratch = [pltpu.VMEM((s, width), F32)]
    if pe:
        out_specs = (seq(LANES, lambda b, p: (b, p)), seq(2 * LANES, lambda b, p: (b, p)),
                     seq(LANES, lambda b, p: (b, p // 2)), seq(LANES, lambda b, p: (b, 0)))
        out_shape = (jax.ShapeDtypeStruct((t, PAIRS * LANES), BF16), jax.ShapeDtypeStruct((t, PAIRS * 2 * LANES), BF16),
                     jax.ShapeDtypeStruct((t, 2 * LANES), F32), jax.ShapeDtypeStruct((t, LANES), F32))
        scratch.append(pltpu.VMEM((s, 2 * LANES), BF16))
    else:
        out_specs = (seq(3 * LANES, lambda b, p: (b, p)), per_head_row)
        out_shape = (jax.ShapeDtypeStruct((t, PAIRS * 3 * LANES), BF16), jax.ShapeDtypeStruct((bl * HEADS, 1, s), F32))
    if traffic:
        in_specs += traffic.in_specs
        args += traffic.pieces
        out_specs += (traffic.out_spec,)
        out_shape += (traffic.out_shape,)
        scratch += traffic.scratch
    return pl.pallas_call(
        body, name=name, grid=(bl, PAIRS), in_specs=in_specs, out_specs=out_specs, out_shape=out_shape,
        scratch_shapes=scratch, compiler_params=_params("arbitrary" if traffic else "parallel", "arbitrary"),
    )(*args)


def _my_place():
    return lax.axis_index("x"), lax.axis_index("y"), lax.axis_index("c")


def _flip(p, bit):
    return 1 - p if bit else p


def _relative(x, y, c, k):
    return _flip(x, k & 4), _flip(y, k & 2), _flip(c, k & 1)


def _linear(x, y, c):
    return 4 * x + 2 * y + c


class _Traffic:
    def __init__(self, kind, pieces):
        self.kind, self.pieces = kind, list(pieces)
        self.rows = [p.shape[-2] for p in self.pieces]
        self.starts = [sum(self.rows[:i]) for i in range(len(self.rows))]
        anywhere = pl.BlockSpec(memory_space=pl.ANY)
        self.in_specs = [anywhere] * len(self.pieces)
        self.out_spec = anywhere
        self.out_shape = jax.ShapeDtypeStruct((N_DEV, sum(self.rows), self.pieces[0].shape[-1]), self.pieces[0].dtype)
        self.scratch = [pltpu.SemaphoreType.DMA((7,)), pltpu.SemaphoreType.DMA((7,)), pltpu.SemaphoreType.DMA(())]

    def start(self, p_refs, out_ref, send_sems, recv_sems, local_sem):
        x, y, c = _my_place()
        me = _linear(x, y, c)
        mine = lambda i, dev: p_refs[i] if self.kind == "spread" else p_refs[i].at[dev]
        landing = lambda i: out_ref.at[me, pl.ds(self.starts[i], self.rows[i])]
        for i in range(len(p_refs)):
            pltpu.make_async_copy(mine(i, me), landing(i), local_sem).start()
        for k in range(1, N_DEV):
            peer = _relative(x, y, c, k)
            for i in range(len(p_refs)):
                pltpu.make_async_remote_copy(
                    src_ref=mine(i, _linear(*peer)), dst_ref=landing(i),
                    send_sem=send_sems.at[k - 1], recv_sem=recv_sems.at[k - 1], device_id=peer, device_id_type=MESH).start()

    def wait(self, out_ref, send_sems, recv_sems, local_sem):
        x, y, c = _my_place()
        whole = out_ref.at[_linear(x, y, c)]
        for k in range(1, N_DEV):
            both = pltpu.make_async_remote_copy(
                src_ref=whole, dst_ref=whole, send_sem=send_sems.at[k - 1], recv_sem=recv_sems.at[k - 1],
                device_id=_relative(x, y, c, k), device_id_type=MESH)
            both.wait_recv()
            both.wait_send()
        pltpu.make_async_copy(whole, whole, local_sem).wait()


def _sum_blocks(parts, name):
    n, r, cdim = parts.shape
    tr = _tile(r, 640, 16)

    def body(p_ref, o_ref):
        acc = p_ref[0].astype(F32)
        for d in range(1, n):
            acc = acc + p_ref[d].astype(F32)
        o_ref[...] = acc

    return pl.pallas_call(
        body, name=name, grid=(r // tr,), in_specs=[pl.BlockSpec((n, tr, cdim), lambda i: (0, i, 0))],
        out_specs=pl.BlockSpec((tr, cdim), lambda i: (i, 0)), out_shape=jax.ShapeDtypeStruct((r, cdim), F32),
        compiler_params=_params("parallel"),
    )(parts)


def _adamw_math(w, g, m, v):
    m = ADAM_B1 * m + (1.0 - ADAM_B1) * g
    v = ADAM_B2 * v + (1.0 - ADAM_B2) * (g * g)
    m_hat = m / (1.0 - ADAM_B1 ** ADAM_STEP)
    v_hat = v / (1.0 - ADAM_B2 ** ADAM_STEP)
    delta = -ADAM_LR * (m_hat / (jnp.sqrt(v_hat) + ADAM_EPS) + ADAM_WD * w)
    return delta, m, v


def _adamw(w, g, m, v, name):
    def body(w_ref, g_ref, m_ref, v_ref, d_ref, nm_ref, nv_ref):
        d_ref[...], nm_ref[...], nv_ref[...] = _adamw_math(w_ref[...], g_ref[...], m_ref[...], v_ref[...])

    out = jax.ShapeDtypeStruct(w.shape, F32)
    return pl.pallas_call(body, name=name, out_shape=(out, out, out),
                          compiler_params=pltpu.CompilerParams(vmem_limit_bytes=VMEM_LIMIT))(w, g, m, v)


def _small_all_reduce_adamw(parts, loss_part, ws, ms, vs, name):
    sizes = [p.shape[1] for p in parts] + [1]
    spots = [sum(-(-n // LANES) * LANES for n in sizes[:i]) for i in range(len(sizes))]
    width = spots[-1] + LANES
    k = len(parts)

    def reduce_body(*refs):
        p_refs, tot_ref, rows, send_sems, recv_sems = refs[:k + 1], *refs[k + 1:]
        x, y, c = _my_place()
        me = _linear(x, y, c)
        rows[me] = jnp.zeros((1, width), F32)
        for i in range(k + 1):
            rows[me, :, spots[i]:spots[i] + sizes[i]] = p_refs[i][...]
        copies = []
        for rel in range(1, N_DEV):
            copies.append(pltpu.make_async_remote_copy(
                src_ref=rows.at[me], dst_ref=rows.at[me], send_sem=send_sems.at[rel - 1], recv_sem=recv_sems.at[rel - 1],
                device_id=_relative(x, y, c, rel), device_id_type=MESH))
        for cp in copies:
            cp.start()
        for cp in copies:
            cp.wait_recv()
        for cp in copies:
            cp.wait_send()
        total = rows[0]
        for d in range(1, N_DEV):
            total = total + rows[d]
        tot_ref[...] = total

    total = pl.pallas_call(
        reduce_body, name=name, out_shape=jax.ShapeDtypeStruct((1, width), F32),
        scratch_shapes=[pltpu.VMEM((N_DEV, 1, width), F32), pltpu.SemaphoreType.DMA((7,)), pltpu.SemaphoreType.DMA((7,))],
    )(*parts, loss_part)

    def adamw_body(*refs):
        tot_ref, w_refs, m_refs, v_refs, outs = refs[0], refs[1:k + 1], refs[k + 1:2 * k + 1], refs[2 * k + 1:3 * k + 1], refs[3 * k + 1:]
        for i in range(k):
            g = tot_ref[:, spots[i]:spots[i] + sizes[i]]
            outs[4 * i][...] = g
            outs[4 * i + 1][...], outs[4 * i + 2][...], outs[4 * i + 3][...] = _adamw_math(w_refs[i][...], g, m_refs[i][...], v_refs[i][...])
        outs[4 * k][...] = tot_ref[:, spots[k]:spots[k] + 1]

    out_shape = [jax.ShapeDtypeStruct((1, n), F32) for n in sizes[:k] for _ in range(4)] + [jax.ShapeDtypeStruct((1, 1), F32)]
    res = pl.pallas_call(adamw_body, name=name + "_adamw", out_shape=tuple(out_shape))(total, *ws, *ms, *vs)
    return [res[4 * i:4 * i + 4] for i in range(k)], res[4 * k]


def _pad_rows(a, rows):
    return jnp.pad(a, ((0, rows - a.shape[0]), (0, 0)))


def kernel(x, positions, norm_mix_g, w_in, b_fgate, q_norm_g, w_uq, kv_norm_g, w_ukv, fox_out_g, mla_out_g, w_o, norm_ffn_g, w_gate, w_up, w_down, final_norm_g, loss_target, m_norm_mix_g, m_w_in, m_b_fgate, m_q_norm_g, m_w_uq, m_kv_norm_g, m_w_ukv, m_fox_out_g, m_mla_out_g, m_w_o, m_norm_ffn_g, m_w_gate, m_w_up, m_w_down, m_final_norm_g, v_norm_mix_g, v_w_in, v_b_fgate, v_q_norm_g, v_w_uq, v_kv_norm_g, v_w_ukv, v_fox_out_g, v_mla_out_g, v_w_o, v_norm_ffn_g, v_w_gate, v_w_up, v_w_down, v_final_norm_g):
    bl, s, d = x.shape
    t = bl * s
    bh = bl * HEADS
    tq = _tile(s, 256)
    grp = s // LANES
    fw = HEADS * HEAD_DIM
    q_rank, kv_rank = w_uq.shape[1], w_ukv.shape[1]
    in_cols = w_in.shape[2]
    n_in = N_DEV * in_cols
    ff = N_DEV * w_gate.shape[2]
    half = MLA_ROPE // 2
    o_kvlat, o_krope, o_flogit = q_rank, q_rank + kv_rank, q_rank + kv_rank + LANES
    b_cols = -(-(o_flogit + HEADS) // LANES) * LANES

    tr = lambda w: jnp.transpose(w[0])
    in_rows = -(-in_cols // 16) * 16
    uq_rows = w_uq.shape[2] * q_rank // d
    ukv_rows = w_ukv.shape[2] * kv_rank // d
    pieces = [_pad_rows(tr(w_in), in_rows), _pad_rows(tr(w_uq).reshape(uq_rows, d), -(-uq_rows // 16) * 16),
              tr(w_ukv).reshape(ukv_rows, d), w_o[0], tr(w_gate), tr(w_up), w_down[0]]
    pieces = [p.astype(BF16) for p in pieces]
    offs = [0]
    for p in pieces:
        offs.append(offs[-1] + p.shape[0])
    legs = [(0, 1), (1, 5), (5, 7)]
    gathered = {}

    def full(i, rows):
        leg = next(n for n, (lo, hi) in enumerate(legs) if lo <= i < hi)
        base = offs[legs[leg][0]]
        return gathered[leg][:, offs[i] - base:offs[i] - base + rows]

    x2d = x.reshape(t, d)
    h1, gathered[0] = _rmsnorm(x2d, 0, d, norm_mix_g, BF16, "norm_mix", traffic=_Traffic("spread", pieces[0:1]))

    w_in_t = full(0, in_cols).reshape(n_in, d)
    n_qkv = 3 * fw
    w_in_a = w_in_t[:n_qkv].reshape(3, PAIRS, LANES, d).transpose(1, 0, 2, 3).reshape(n_qkv, d)
    lat0, rope0 = n_qkv + HEADS, n_qkv + HEADS + q_rank + kv_rank
    k_rep = jnp.broadcast_to(w_in_t[rope0:].reshape(2, 1, half, d), (2, 4, half, d)).reshape(LANES, d)
    w_in_b = jnp.concatenate([w_in_t[lat0:rope0], k_rep, w_in_t[n_qkv:lat0],
                              jnp.zeros((b_cols - o_flogit - HEADS, d), BF16)], axis=0)

    def per_head_rows(a):
        return a.reshape(bl, s, HEADS).transpose(0, 2, 1).reshape(bh, 1, s)

    proj_a = _matmul(h1, w_in_a, "nt", BF16, "proj_fox", tm=1024, tn=6 * LANES)
    proj_b = _matmul(h1, w_in_b, "nt", F32, "proj_mla", tm=1024, tn=b_cols)

    z = proj_b[:, o_flogit:o_flogit + HEADS].reshape(bl, s, HEADS).transpose(0, 2, 1).reshape(bh * grp, LANES)
    bcol = jnp.broadcast_to(b_fgate.reshape(1, HEADS, 1), (bl, HEADS, grp)).reshape(bh * grp, 1)
    c = _fgate(z, bcol, grp, "forget_gate")
    c_bias = c.reshape(bh, 1, s)
    fox_o, fox_lse, gathered[1] = _attn_fwd((proj_a,), c_bias, HEAD_DIM ** -0.5, bl, s, tq, "fox_attention",
                                            traffic=_Traffic("spread", pieces[legs[1][0]:legs[1][1]]))
    w_uq_h = full(1, uq_rows).reshape(HEADS, MLA_QK, q_rank)
    w_uq_pe = jnp.concatenate([w_uq_h[:, HEAD_DIM:HEAD_DIM + half].reshape(2, 1, 4 * half, q_rank),
                               w_uq_h[:, HEAD_DIM + half:].reshape(2, 1, 4 * half, q_rank)], axis=1).reshape(2 * LANES, q_rank)
    w_uq_p = jnp.concatenate([w_uq_h[:, :HEAD_DIM].reshape(fw, q_rank), w_uq_pe], axis=0)
    w_ukv_p = full(2, ukv_rows).reshape(PAIRS, 2, 2, HEAD_DIM, kv_rank).transpose(0, 2, 1, 3, 4).reshape(2 * fw, kv_rank)
    w_o_f = full(3, w_o.shape[1]).reshape(-1, d)
    w_gate_t = full(4, ff // N_DEV).reshape(ff, d)

    inv_freq = ROPE_THETA ** (-jnp.arange(0, MLA_ROPE, 2, dtype=F32) / MLA_ROPE)
    ang = positions.astype(F32).reshape(t, 1) * inv_freq[None, :]
    cos4, sin4 = jnp.tile(jnp.cos(ang), (1, 4)), jnp.tile(jnp.sin(ang), (1, 4))
    rope_cos, rope_sin = jnp.concatenate([cos4, cos4], axis=1), jnp.concatenate([-sin4, sin4], axis=1)
    qn, kvn, q_all, kv_all, kpe = _mla_prep(proj_b, q_rank, kv_rank, q_norm_g, kv_norm_g, w_uq_p, w_ukv_p, fw,
                                            rope_cos, rope_sin, "mla_prep")
    mla_ops = (q_all, kv_all, kpe)
    mla_o, mla_lse, gathered[2] = _attn_fwd(mla_ops, None, MLA_QK ** -0.5, bl, s, tq, "mla_attention",
                                            traffic=_Traffic("spread", pieces[legs[2][0]:legs[2][1]]))
    w_up_t, w_down_f = full(5, ff // N_DEV).reshape(ff, d), full(6, ff // N_DEV).reshape(ff, d)

    both = [(d, F32), (d, BF16)]
    cat, x1, h2 = _rows_matmul([(None, w_o_f, "nn")], [fox_o, mla_o, x2d], [fox_out_g, mla_out_g, norm_ffn_g], _residual_norm,
                               [(2 * fw, BF16)] + both, [], "norm_out_proj_out_norm_ffn", prologue=_out_norm)
    act_by_gate, act_by_up, act = _ffn_up(h2, w_gate_t, w_up_t, "ffn_gate_up")
    dx2, dx2_b, dg_final, loss_part = _rows_matmul(
        [(act, w_down_f, "nn")], [x1, loss_target.reshape(t, d)], [final_norm_g.reshape(1, d)], _residual_loss_bwd,
        both, [d, 1], "ffn_down_final_norm_loss")

    d_gate, d_up = _ffn_down_bwd(dx2_b, w_down_f, act_by_gate, act_by_up, "d_ffn_down")
    dw_down = _matmul(act, dx2_b, "tn", BF16, "dw_down", tm=ff // 2, tn=d, tk=2048)
    dw_gate = _matmul(d_gate, h2, "tn", BF16, "dw_gate", tm=ff // 2, tn=d, tk=2048)
    dw_up = _matmul(d_up, h2, "tn", BF16, "dw_up", tm=ff // 2, tn=d, tk=2048)
    dx1, dx1_b, dg_ffn = _rows_matmul([(d_gate, w_gate_t, "nn"), (d_up, w_up_t, "nn")], [x1, dx2], [norm_ffn_g],
                                      _norm_bwd_residual, both, [d], "d_ffn_gate_up_norm_ffn", tm=256)
    dw_o = _matmul(cat, dx1_b, "tn", BF16, "dw_o", tn=d, tk=2048)
    d_fox_o, d_mla_o, fox_delta, mla_delta, dg_fox, dg_mla = _rows_matmul(
        [(dx1_b, w_o_f, "nt")], [fox_o, mla_o], [fox_out_g, mla_out_g], _out_norm_bwd,
        [(fw, BF16), (fw, BF16), (HEADS, F32), (HEADS, F32)], [fw, fw], "d_proj_out_norm_out")

    per_dev = lambda a: a.reshape(N_DEV, -1, d)
    late_grads = [per_dev(dw_o), per_dev(dw_gate), per_dev(dw_up), per_dev(dw_down)]
    dproj_a, dc, g_late = _attn_bwd((proj_a,), d_fox_o, fox_lse, per_head_rows(fox_delta),
                                    c_bias, HEAD_DIM ** -0.5, bl, s, tq, "d_fox_attention", traffic=_Traffic("swap", late_grads))
    dz, db_fgate = _fgate_bwd(z, bcol, dc.reshape(bh * grp, LANES), grp, "d_forget_gate")
    d_flogit = dz.reshape(bl, HEADS, s).transpose(0, 2, 1).reshape(t, HEADS)

    dq_nope, dkv_all, dq_pe, dk_pe = _attn_bwd(mla_ops, d_mla_o, mla_lse, per_head_rows(mla_delta),
                                               None, MLA_QK ** -0.5, bl, s, tq, "d_mla_attention")
    d_tail = jnp.pad(d_flogit, ((0, 0), (0, b_cols - o_flogit - HEADS)))
    dproj_b, dq_rot, dg_q, dg_kv = _mla_prep_bwd(dq_nope, dq_pe, dkv_all, dk_pe, d_tail, proj_b, q_rank, kv_rank,
                                                 q_norm_g, kv_norm_g, w_uq_p, w_ukv_p, rope_cos, rope_sin, "d_mla_prep")
    dw_uq_nope = _matmul(dq_nope, qn, "tn", BF16, "dw_uq_nope", tn=q_rank, tk=1024)
    dw_uq_pe = _matmul(dq_rot, qn, "tn", BF16, "dw_uq_rope", tn=q_rank, tk=1024)
    dw_ukv_p = _matmul(dkv_all, kvn, "tn", BF16, "dw_ukv", tn=kv_rank, tk=1024)
    dw_in_a = _matmul(dproj_a, h1, "tn", BF16, "dw_in_fox", tm=6 * LANES, tn=d, tk=2048)
    dw_in_b = _matmul(dproj_b, h1, "tn", F32, "dw_in_mla", tm=b_cols, tn=d, tk=1024)

    dw_krope = dw_in_b[o_krope:o_flogit].reshape(2, 4, half, d).sum(axis=1).reshape(MLA_ROPE, d)
    dw_in_t = jnp.concatenate([dw_in_a.reshape(PAIRS, 3, LANES, d).transpose(1, 0, 2, 3).reshape(n_qkv, d),
                               dw_in_b[o_flogit:o_flogit + HEADS].astype(BF16), dw_in_b[:o_krope].astype(BF16),
                               dw_krope.astype(BF16)], axis=0)
    pad_dev = lambda a, rows: jnp.pad(a, ((0, 0), (0, rows - a.shape[1]), (0, 0)))
    dw_uq_pe5 = dw_uq_pe.reshape(2, 2, 4, half, q_rank)
    dw_uq_h = jnp.concatenate([dw_uq_nope.reshape(HEADS, HEAD_DIM, q_rank), dw_uq_pe5[:, 0].reshape(HEADS, half, q_rank),
                               dw_uq_pe5[:, 1].reshape(HEADS, half, q_rank)], axis=1)
    dw_ukv_h = dw_ukv_p.reshape(PAIRS, 2, 2, HEAD_DIM, kv_rank).transpose(0, 2, 1, 3, 4).reshape(HEADS, 2 * HEAD_DIM, kv_rank)
    n_last = 3
    last_grads = [pad_dev(per_dev(dw_in_t), pieces[0].shape[0]), pad_dev(per_dev(dw_uq_h), pieces[1].shape[0]), per_dev(dw_ukv_h)]
    dh1_fox, g_last = _matmul(dproj_a, w_in_a, "nn", F32, "d_proj_fox", tn=d, traffic=_Traffic("swap", last_grads))
    grad_x, dg_mix = _rows_matmul([(dproj_b, w_in_b, "nn")], [x2d, dx1, dh1_fox], [norm_mix_g], _norm_bwd_residual,
                                  [(d, F32)], [d], "d_proj_mla_norm_mix")
    g_last = _sum_blocks(g_last, "sum_last_grads")
    g_late = _sum_blocks(g_late, "sum_late_grads")

    def mine(i, rows):
        src, base = (g_last, 0) if i < n_last else (g_late, offs[n_last])
        return src[offs[i] - base:offs[i] - base + rows]

    big = [
        ("w_in", w_in, m_w_in, v_w_in, mine(0, in_cols), True),
        ("w_uq", w_uq, m_w_uq, v_w_uq, mine(1, uq_rows).reshape(-1, q_rank), True),
        ("w_ukv", w_ukv, m_w_ukv, v_w_ukv, mine(2, ukv_rows).reshape(-1, kv_rank), True),
        ("w_o", w_o, m_w_o, v_w_o, mine(3, w_o.shape[1]), False),
        ("w_gate", w_gate, m_w_gate, v_w_gate, mine(4, ff // N_DEV), True),
        ("w_up", w_up, m_w_up, v_w_up, mine(5, ff // N_DEV), True),
        ("w_down", w_down, m_w_down, v_w_down, mine(6, ff // N_DEV), False),
    ]
    out = {}
    for nm, w, m, v, g, transposed in big:
        lay = (lambda a: a[0].T) if transposed else (lambda a: a[0])
        back = (lambda a: a.T[None]) if transposed else (lambda a: a[None])
        dl, new_m, new_v = _adamw(lay(w), g, lay(m), lay(v), "adamw_" + nm)
        out[nm] = (back(g), back(dl), back(new_m), back(new_v))

    smalls = [("norm_mix_g", norm_mix_g, m_norm_mix_g, v_norm_mix_g, dg_mix),
              ("b_fgate", b_fgate, m_b_fgate, v_b_fgate, db_fgate.reshape(1, HEADS)),
              ("q_norm_g", q_norm_g, m_q_norm_g, v_q_norm_g, dg_q),
              ("kv_norm_g", kv_norm_g, m_kv_norm_g, v_kv_norm_g, dg_kv),
              ("fox_out_g", fox_out_g, m_fox_out_g, v_fox_out_g, dg_fox),
              ("mla_out_g", mla_out_g, m_mla_out_g, v_mla_out_g, dg_mla),
              ("norm_ffn_g", norm_ffn_g, m_norm_ffn_g, v_norm_ffn_g, dg_ffn),
              ("final_norm_g", final_norm_g, m_final_norm_g, v_final_norm_g, dg_final)]
    flat = lambda a: a.reshape(1, -1)
    results, loss = _small_all_reduce_adamw([e[4] for e in smalls], loss_part, [flat(e[1]) for e in smalls],
                                            [flat(e[2]) for e in smalls], [flat(e[3]) for e in smalls], "reduce_small_adamw")
    for (nm, w, _, _, _), res in zip(smalls, results):
        out[nm] = tuple(a.reshape(w.shape) for a in res)
    loss = loss[0, 0]

    order = ["norm_mix_g", "w_in", "b_fgate", "q_norm_g", "w_uq", "kv_norm_g", "w_ukv", "fox_out_g", "mla_out_g", "w_o",
             "norm_ffn_g", "w_gate", "w_up", "w_down", "final_norm_g"]
    return (loss, grad_x.reshape(bl, s, d), *[out[n][0] for n in order], *[out[n][1] for n in order],
            *[out[n][2] for n in order], *[out[n][3] for n in order])
```

```python
import math

import jax
import jax.numpy as jnp
from jax import lax
from jax.experimental import pallas as pl
from jax.experimental.pallas import tpu as pltpu

F32 = jnp.float32
BF16 = jnp.bfloat16
MESH = pl.DeviceIdType.MESH

N_DEV = 8
HEADS = 8
HEAD_DIM = 64
PAIRS = HEADS // 2
MLA_ROPE = 32
MLA_QK = HEAD_DIM + MLA_ROPE
ROPE_THETA = 10000.0
NORM_EPS = 1e-6
ADAM_LR, ADAM_B1, ADAM_B2, ADAM_EPS, ADAM_WD, ADAM_STEP = 0.001, 0.9, 0.999, 1e-08, 0.01, 10

LANES = 128
MASKED = -1e30
VMEM_LIMIT = 48 * 1024 * 1024

_DIMS = {"nn": (((1,), (0,)), ((), ())), "nt": (((1,), (1,)), ((), ())), "tn": (((0,), (0,)), ((), ()))}


def _params(*sem):
    return pltpu.CompilerParams(dimension_semantics=sem, vmem_limit_bytes=VMEM_LIMIT)


def _dot(a, b, mode):
    return lax.dot_general(a.astype(BF16), b.astype(BF16), _DIMS[mode], preferred_element_type=F32)


def _tile(n, pref, unit=8):
    if n <= pref:
        return n
    t = pref - pref % unit
    while n % t:
        t -= unit
    return t


def _log2(n):
    assert n & (n - 1) == 0
    return n.bit_length() - 1


def _matmul(a, b, mode, out_dtype, name, tm=512, tn=512, tk=None, res=None, traffic=None):
    if mode == "nn":
        (m, kd), n = a.shape, b.shape[1]
    elif mode == "nt":
        (m, kd), n = a.shape, b.shape[0]
    else:
        (kd, m), n = a.shape, b.shape[1]
    tm, tn = _tile(m, tm, LANES if mode == "tn" else 16), _tile(n, tn, LANES)
    tk = kd if tk is None else _tile(kd, tk, LANES)
    nk = kd // tk
    a_spec = pl.BlockSpec((tk, tm), lambda i, j, k: (k, i)) if mode == "tn" else pl.BlockSpec((tm, tk), lambda i, j, k: (i, k))
    b_spec = pl.BlockSpec((tn, tk), lambda i, j, k: (j, k)) if mode == "nt" else pl.BlockSpec((tk, tn), lambda i, j, k: (k, j))
    o_spec = pl.BlockSpec((tm, tn), lambda i, j, k: (i, j))
    has_res = res is not None
    n_carried = len(traffic.pieces) if traffic else 0
    grid = (m // tm, n // tn, nk)

    def body(*refs):
        a_ref, b_ref = refs[:2]
        r_ref = refs[2] if has_res else None
        n_in = 2 + has_res + n_carried
        o_ref = refs[n_in]
        step = [pl.program_id(axis) for axis in range(3)]
        if traffic:
            carried_in, carried_out, sems = refs[2 + has_res:n_in], refs[n_in + 1], refs[len(refs) - 3:]

            @pl.when((step[0] == 0) & (step[1] == 0) & (step[2] == 0))
            def _():
                traffic.start(carried_in, carried_out, *sems)

        def finish(acc):
            if has_res:
                acc = acc + r_ref[...]
            o_ref[...] = acc.astype(out_dtype)

        part = _dot(a_ref[...], b_ref[...], mode)
        if nk == 1:
            finish(part)
        else:
            acc_ref = refs[n_in + 1 + bool(traffic)]

            @pl.when(step[2] == 0)
            def _():
                acc_ref[...] = part

            @pl.when(step[2] > 0)
            def _():
                acc_ref[...] += part

            @pl.when(step[2] == nk - 1)
            def _():
                finish(acc_ref[...])

        if traffic:
            @pl.when((step[0] == grid[0] - 1) & (step[1] == grid[1] - 1) & (step[2] == nk - 1))
            def _():
                traffic.wait(carried_out, *sems)

    in_specs = [a_spec, b_spec] + ([o_spec] if has_res else [])
    out_specs, out_shape = [o_spec], [jax.ShapeDtypeStruct((m, n), out_dtype)]
    scratch = [pltpu.VMEM((tm, tn), F32)] if nk > 1 else []
    if traffic:
        in_specs += traffic.in_specs
        out_specs.append(traffic.out_spec)
        out_shape.append(traffic.out_shape)
        scratch += traffic.scratch
    out = pl.pallas_call(
        body, name=name, grid=grid, in_specs=in_specs, out_specs=tuple(out_specs), out_shape=tuple(out_shape),
        scratch_shapes=scratch,
        compiler_params=_params(*(("arbitrary",) * 3 if traffic else ("parallel", "parallel", "arbitrary"))),
    )(*([a, b] + ([res] if has_res else []) + (traffic.pieces if traffic else [])))
    return out if traffic else out[0]


def _rstd(x):
    return lax.rsqrt(jnp.mean(x * x, axis=-1, keepdims=True) + NORM_EPS)


def _norm_bwd(x, g, dy):
    r = _rstd(x)
    xh = x * r
    u = dy * g
    dx = r * (u - xh * jnp.mean(u * xh, axis=-1, keepdims=True))
    return dx, jnp.sum(dy * xh, axis=0, keepdims=True)


def _rmsnorm(x, col, width, g, out_dtype, name, traffic=None):
    t = x.shape[0]
    tm = _tile(t, 512)
    steps = t // tm
    n_carried = len(traffic.pieces) if traffic else 0

    def body(*refs):
        x_ref, g_ref, o_ref = refs[0], refs[1], refs[2 + n_carried]
        if traffic:
            carried_in, carried_out, sems = refs[2:2 + n_carried], refs[3 + n_carried], refs[4 + n_carried:]

            @pl.when(pl.program_id(0) == 0)
            def _():
                traffic.start(carried_in, carried_out, *sems)

        xv = x_ref[...]
        o_ref[...] = ((xv * _rstd(xv)) * g_ref[...]).astype(out_dtype)
        if traffic:
            @pl.when(pl.program_id(0) == steps - 1)
            def _():
                traffic.wait(carried_out, *sems)

    in_specs = [pl.BlockSpec((tm, width), lambda i: (i, col)), pl.BlockSpec((1, width), lambda i: (0, 0))]
    out_specs = [pl.BlockSpec((tm, width), lambda i: (i, 0))]
    out_shape = [jax.ShapeDtypeStruct((t, width), out_dtype)]
    if traffic:
        in_specs += traffic.in_specs
        out_specs.append(traffic.out_spec)
        out_shape.append(traffic.out_shape)
    out = pl.pallas_call(
        body, name=name, grid=(steps,), in_specs=in_specs, out_specs=tuple(out_specs), out_shape=tuple(out_shape),
        scratch_shapes=traffic.scratch if traffic else [],
        compiler_params=_params("arbitrary" if traffic else "parallel"),
    )(x, g, *(traffic.pieces if traffic else []))
    return out if traffic else out[0]


def _split3(x):
    hi = x.astype(BF16)
    r1 = x - hi.astype(F32)
    mid = r1.astype(BF16)
    lo = (r1 - mid.astype(F32)).astype(BF16)
    return hi, mid, lo


def _dot_x01(x, m01):
    hi, mid, lo = _split3(x)
    d = lambda p: lax.dot_general(p, m01, _DIMS["nn"], preferred_element_type=F32)
    return (d(lo) + d(mid)) + d(hi)


def _dot_01x(m01, x):
    hi, mid, lo = _split3(x)
    d = lambda p: lax.dot_general(m01, p, _DIMS["nn"], preferred_element_type=F32)
    return (d(lo) + d(mid)) + d(hi)


def _rows_matmul(terms, rows_in, vecs_in, epilogue, rows_out, sums_out, name, tm=512, prologue=None):
    t = rows_in[0].shape[0]
    tm = _tile(t, tm, 16)
    n_rows, n_vecs = len(rows_in), len(vecs_in)
    n_ab = sum(1 + (a is not None) for a, _, _ in terms)

    def body(*refs):
        row_blocks = [r[...] for r in refs[n_ab:n_ab + n_rows]]
        vecs = [r[...] for r in refs[n_ab + n_rows:n_ab + n_rows + n_vecs]]
        made = prologue(row_blocks, vecs) if prologue else None
        acc, at = None, 0
        for a, _, mode in terms:
            lhs = made if a is None else refs[at][...]
            at += a is not None
            part = _dot(lhs, refs[at][...], mode)
            at += 1
            acc = part if acc is None else acc + part
        row_vals, sum_vals = epilogue(acc, row_blocks, vecs)
        if prologue:
            row_vals = [made] + row_vals
        at = n_ab + n_rows + n_vecs
        for ref, val, (_, dtype) in zip(refs[at:], row_vals, rows_out):
            ref[...] = val.astype(dtype)
        sum_refs = refs[at + len(rows_out):]

        @pl.when(pl.program_id(0) == 0)
        def _():
            for ref in sum_refs:
                ref[...] = jnp.zeros_like(ref)

        for ref, val in zip(sum_refs, sum_vals):
            ref[...] += val

    rows = lambda w: pl.BlockSpec((tm, w), lambda i: (i, 0))
    whole = lambda a: pl.BlockSpec(a.shape, lambda i: (0, 0))
    in_specs, args = [], []
    for a, b, _ in terms:
        in_specs += ([rows(a.shape[1])] if a is not None else []) + [whole(b)]
        args += ([a] if a is not None else []) + [b]
    in_specs += [rows(r.shape[1]) for r in rows_in] + [whole(v) for v in vecs_in]
    args += list(rows_in) + list(vecs_in)
    return pl.pallas_call(
        body, name=name, grid=(t // tm,), in_specs=in_specs,
        out_specs=tuple([rows(w) for w, _ in rows_out] + [pl.BlockSpec((1, w), lambda i: (0, 0)) for w in sums_out]),
        out_shape=tuple([jax.ShapeDtypeStruct((t, w), dt) for w, dt in rows_out] + [jax.ShapeDtypeStruct((1, w), F32) for w in sums_out]),
        compiler_params=_params("arbitrary"),
    )(*args)


def _out_norm(rows, vecs):
    (f, m), (gf, gm) = rows[:2], vecs[:2]
    return jnp.concatenate([((f * _rstd(f)) * gf).astype(BF16), ((m * _rstd(m)) * gm).astype(BF16)], axis=1)


def _residual_norm(acc, rows, vecs):
    x1 = rows[-1] + acc
    return [x1, (x1 * _rstd(x1)) * vecs[-1]], []


def _residual_loss_bwd(acc, rows, vecs):
    x2, gv = rows[0] + acc, vecs[0]
    diff = (x2 * _rstd(x2)) * gv - rows[1]
    dx, dg = _norm_bwd(x2, gv, diff / x2.shape[1])
    return [dx, dx], [dg, 0.5 * jnp.sum(jnp.mean(diff * diff, axis=-1, keepdims=True), axis=0, keepdims=True)]


def _norm_bwd_residual(acc, rows, vecs):
    dy = acc + rows[2] if len(rows) > 2 else acc
    dx, dg = _norm_bwd(rows[0], vecs[0], dy)
    if len(rows) > 1:
        dx = dx + rows[1]
    return [dx, dx], [dg]


def _out_norm_bwd(acc, rows, vecs):
    (f, m), w = rows, rows[0].shape[1]
    nh = w // HEAD_DIM
    lane_head = lax.shift_right_logical(lax.broadcasted_iota(jnp.int32, (w, nh), 0), _log2(HEAD_DIM))
    sel = (lane_head == lax.broadcasted_iota(jnp.int32, (w, nh), 1)).astype(BF16)
    dfo, dgf = _norm_bwd(f, vecs[0], acc[:, :w])
    dmo, dgm = _norm_bwd(m, vecs[1], acc[:, w:])
    return [dfo, dmo, _dot_x01(dfo * f, sel), _dot_x01(dmo * m, sel)], [dgf, dgm]


def _ffn_up(h, wg_t, wu_t, name, tm=512, tf=1408):
    t, d = h.shape
    f = wg_t.shape[0]
    tm, tf = _tile(t, tm, 16), _tile(f, tf, LANES)
    tok = pl.BlockSpec((tm, tf), lambda i, j: (i, j))
    wt = pl.BlockSpec((tf, d), lambda i, j: (j, 0))

    def body(h_ref, wg_ref, wu_ref, dg_ref, du_ref, a_ref):
        hv = h_ref[...]
        g, u = _dot(hv, wg_ref[...], "nt"), _dot(hv, wu_ref[...], "nt")
        sg = jax.nn.sigmoid(g)
        silu = g * sg
        dg_ref[...] = (u * (sg * (1.0 + g * (1.0 - sg)))).astype(BF16)
        du_ref[...] = silu.astype(BF16)
        a_ref[...] = (silu * u).astype(BF16)

    return pl.pallas_call(
        body, name=name, grid=(t // tm, f // tf), in_specs=[pl.BlockSpec((tm, d), lambda i, j: (i, 0)), wt, wt],
        out_specs=(tok, tok, tok),
        out_shape=(jax.ShapeDtypeStruct((t, f), BF16), jax.ShapeDtypeStruct((t, f), BF16), jax.ShapeDtypeStruct((t, f), BF16)),
        compiler_params=_params("parallel", "parallel"),
    )(h, wg_t, wu_t)


def _ffn_down_bwd(dy, w_down, act_by_gate, act_by_up, name, tm=512, tf=1408):
    t, d = dy.shape
    f = w_down.shape[0]
    tm, tf = _tile(t, tm, 16), _tile(f, tf, LANES)
    tok = pl.BlockSpec((tm, tf), lambda i, j: (i, j))

    def body(dy_ref, w_ref, g_ref, u_ref, dg_ref, du_ref):
        da = _dot(dy_ref[...], w_ref[...], "nt")
        dg_ref[...] = (da * g_ref[...].astype(F32)).astype(BF16)
        du_ref[...] = (da * u_ref[...].astype(F32)).astype(BF16)

    return pl.pallas_call(
        body, name=name, grid=(t // tm, f // tf),
        in_specs=[pl.BlockSpec((tm, d), lambda i, j: (i, 0)), pl.BlockSpec((tf, d), lambda i, j: (j, 0)), tok, tok],
        out_specs=(tok, tok),
        out_shape=(jax.ShapeDtypeStruct((t, f), BF16), jax.ShapeDtypeStruct((t, f), BF16)),
        compiler_params=_params("parallel", "parallel"),
    )(dy, w_down, act_by_gate, act_by_up)


def _chunk_scan_mats(rows, grp, reverse):
    ii = lax.broadcasted_iota(jnp.int32, (LANES, LANES), 0)
    jj = lax.broadcasted_iota(jnp.int32, (LANES, LANES), 1)
    within = ((ii >= jj) if reverse else (ii <= jj)).astype(BF16)
    ones = jnp.ones((LANES, LANES), BF16)
    ri = lax.broadcasted_iota(jnp.int32, (rows, rows), 0)
    rj = lax.broadcasted_iota(jnp.int32, (rows, rows), 1)
    sh = _log2(grp)
    same = lax.shift_right_logical(ri, sh) == lax.shift_right_logical(rj, sh)
    across = (same & ((rj > ri) if reverse else (rj < ri))).astype(BF16)
    return within, ones, across


def _running_sum(v, mats):
    within, ones, across = mats
    return _dot_x01(v, within) + _dot_01x(across, _dot_x01(v, ones))


def _fgate(z, bcol, grp, name):
    rows = z.shape[0]

    def body(z_ref, b_ref, c_ref):
        zz = z_ref[...] + b_ref[...]
        log_f = jnp.minimum(zz, 0.0) - jnp.log1p(jnp.exp(-jnp.abs(zz)))
        c_ref[...] = _running_sum(log_f, _chunk_scan_mats(rows, grp, False))

    return pl.pallas_call(body, name=name, out_shape=jax.ShapeDtypeStruct(z.shape, F32),
                          compiler_params=pltpu.CompilerParams(vmem_limit_bytes=VMEM_LIMIT))(z, bcol)


def _fgate_bwd(z, bcol, dc, grp, name):
    rows = z.shape[0]

    def body(z_ref, b_ref, dc_ref, dz_ref, db_ref):
        zz = z_ref[...] + b_ref[...]
        dz = _running_sum(dc_ref[...], _chunk_scan_mats(rows, grp, True)) * jax.nn.sigmoid(-zz)
        dz_ref[...] = dz
        head = lax.shift_right_logical(lax.broadcasted_iota(jnp.int32, (HEADS, rows), 1), _log2(grp)) & (HEADS - 1)
        sel = (head == lax.broadcasted_iota(jnp.int32, (HEADS, rows), 0)).astype(BF16)
        db_ref[...] = jnp.sum(_dot_01x(sel, dz), axis=1, keepdims=True)

    return pl.pallas_call(
        body, name=name,
        out_shape=(jax.ShapeDtypeStruct(z.shape, F32), jax.ShapeDtypeStruct((HEADS, 1), F32)),
        compiler_params=pltpu.CompilerParams(vmem_limit_bytes=VMEM_LIMIT),
    )(z, bcol, dc)


def _rotate(x, cs, sn_signed):
    return x * cs + pltpu.roll(x, LANES // 2, axis=1) * sn_signed


def _mla_prep(proj_b, q_rank, kv_rank, gq, gkv, w_uq_p, w_ukv_p, nope, cs, sn, name):
    t, bw = proj_b.shape
    qw, kvw = w_uq_p.shape[0], w_ukv_p.shape[0]
    tm = _tile(t, 512)
    rows = lambda w: pl.BlockSpec((tm, w), lambda i: (i, 0))
    whole = lambda a: pl.BlockSpec(a.shape, lambda i: (0, 0))

    def body(pb_ref, gq_ref, gkv_ref, wq_ref, wkv_ref, c_ref, s_ref, qn_ref, kvn_ref, q_ref, kv_ref, kpe_ref):
        c, s = c_ref[...], s_ref[...]
        ql, kvl = pb_ref[:, :q_rank], pb_ref[:, q_rank:q_rank + kv_rank]
        qn = ((ql * _rstd(ql)) * gq_ref[...]).astype(BF16)
        kvn = ((kvl * _rstd(kvl)) * gkv_ref[...]).astype(BF16)
        qn_ref[...], kvn_ref[...] = qn, kvn
        q_raw = _dot(qn, wq_ref[...], "nt")
        q_ref[:, :nope] = q_raw[:, :nope].astype(BF16)
        for off in range(nope, qw, LANES):
            q_ref[:, off:off + LANES] = _rotate(q_raw[:, off:off + LANES], c, s).astype(BF16)
        kv_ref[...] = _dot(kvn, wkv_ref[...], "nt").astype(BF16)
        kpe_ref[...] = _rotate(pb_ref[:, q_rank + kv_rank:q_rank + kv_rank + LANES], c, s).astype(BF16)

    return pl.pallas_call(
        body, name=name, grid=(t // tm,),
        in_specs=[rows(bw), whole(gq), whole(gkv), whole(w_uq_p), whole(w_ukv_p), rows(LANES), rows(LANES)],
        out_specs=(rows(q_rank), rows(kv_rank), rows(qw), rows(kvw), rows(LANES)),
        out_shape=(jax.ShapeDtypeStruct((t, q_rank), BF16), jax.ShapeDtypeStruct((t, kv_rank), BF16),
                   jax.ShapeDtypeStruct((t, qw), BF16), jax.ShapeDtypeStruct((t, kvw), BF16), jax.ShapeDtypeStruct((t, LANES), BF16)),
        compiler_params=_params("parallel"),
    )(proj_b, gq, gkv, w_uq_p, w_ukv_p, cs, sn)


def _mla_prep_bwd(dq_nope, dq_pe, dkv_all, dk_pe, d_tail, proj_b, q_rank, kv_rank, gq, gkv, w_uq_p, w_ukv_p, cs, sn, name):
    t, bw = proj_b.shape
    nope, pw = dq_nope.shape[1], dq_pe.shape[1]
    tm = _tile(t, 512)
    rows = lambda w: pl.BlockSpec((tm, w), lambda i: (i, 0))
    whole = lambda a: pl.BlockSpec(a.shape, lambda i: (0, 0))
    o_k = q_rank + kv_rank

    def body(dqn_ref, dqp_ref, dkv_ref, dkp_ref, dt_ref, pb_ref, gq_ref, gkv_ref, wq_ref, wkv_ref, c_ref, s_ref,
             dpb_ref, dqr_ref, dgq_ref, dgkv_ref):
        c, s = c_ref[...], -s_ref[...]
        for off in range(0, pw, LANES):
            dqr_ref[:, off:off + LANES] = _rotate(dqp_ref[:, off:off + LANES], c, s).astype(BF16)
        d_qn = _dot(dqn_ref[...], wq_ref[:nope, :], "nn") + _dot(dqr_ref[...], wq_ref[nope:, :], "nn")
        dq_lat, dgq = _norm_bwd(pb_ref[:, :q_rank], gq_ref[...], d_qn)
        dkv_lat, dgkv = _norm_bwd(pb_ref[:, q_rank:o_k], gkv_ref[...], _dot(dkv_ref[...], wkv_ref[...], "nn"))
        dpb_ref[:, :q_rank] = dq_lat.astype(BF16)
        dpb_ref[:, q_rank:o_k] = dkv_lat.astype(BF16)
        dpb_ref[:, o_k:o_k + LANES] = _rotate(dkp_ref[...], c, s).astype(BF16)
        dpb_ref[:, o_k + LANES:] = dt_ref[...].astype(BF16)

        @pl.when(pl.program_id(0) == 0)
        def _():
            dgq_ref[...] = jnp.zeros_like(dgq_ref)
            dgkv_ref[...] = jnp.zeros_like(dgkv_ref)

        dgq_ref[...] += dgq
        dgkv_ref[...] += dgkv

    return pl.pallas_call(
        body, name=name, grid=(t // tm,),
        in_specs=[rows(nope), rows(pw), rows(dkv_all.shape[1]), rows(LANES), rows(bw - o_k - LANES), rows(bw), whole(gq), whole(gkv),
                  whole(w_uq_p), whole(w_ukv_p), rows(LANES), rows(LANES)],
        out_specs=(rows(bw), rows(pw), whole(gq), whole(gkv)),
        out_shape=(jax.ShapeDtypeStruct((t, bw), BF16), jax.ShapeDtypeStruct((t, pw), BF16),
                   jax.ShapeDtypeStruct(gq.shape, F32), jax.ShapeDtypeStruct(gkv.shape, F32)),
        compiler_params=_params("arbitrary"),
    )(dq_nope, dq_pe, dkv_all, dk_pe, d_tail, proj_b, gq, gkv, w_uq_p, w_ukv_p, cs, sn)


def _lane_masks(pair, h, pe):
    lane = lax.broadcasted_iota(jnp.int32, (1, LANES), 1)
    in_head = lax.shift_right_logical(lane, _log2(HEAD_DIM)) == h
    in_rope = ((lax.shift_right_logical(lane, _log2(MLA_ROPE // 2)) & 3) == ((2 * pair + h) & 3)) if pe else None
    return in_head, in_rope


def _keep(mask, v):
    return jnp.where(mask, v, jnp.zeros_like(v))


def _to_row(col):
    n = col.shape[0]
    eye = lax.broadcasted_iota(jnp.int32, (n, n), 0) == lax.broadcasted_iota(jnp.int32, (n, n), 1)
    return jnp.sum(jnp.where(eye, col, 0.0), axis=0, keepdims=True)


def _to_col(row):
    n = row.shape[1]
    eye = lax.broadcasted_iota(jnp.int32, (n, n), 0) == lax.broadcasted_iota(jnp.int32, (n, n), 1)
    return jnp.sum(jnp.where(eye, row, 0.0), axis=1, keepdims=True)


def _first_step():
    return (pl.program_id(0) == 0) & (pl.program_id(1) == 0)


def _last_step(n0, n1):
    return (pl.program_id(0) == n0 - 1) & (pl.program_id(1) == n1 - 1)


def _attn_fwd(ops, bias, scale, bl, s, tq, name, traffic=None):
    pe = len(ops) == 3
    has_bias = bias is not None
    exact_scale = math.frexp(scale)[0] == 0.5
    nq = s // tq
    t = bl * s
    n_carried = len(traffic.pieces) if traffic else 0

    def body(*refs):
        sems = refs[len(refs) - 3:] if traffic else ()
        if pe:
            q_ref, qpe_ref, kv_ref, kpe_ref = refs[:4]
            n_in = 4
            q_at = lambda r0, r1: q_ref[r0:r1, :]
            v_at = lambda r0, r1: kv_ref[r0:r1, LANES:]
            kcat = refs[len(refs) - 1 - len(sems)]
            kcat[:, :LANES] = kv_ref[:, :LANES]
            kcat[:, LANES:] = kpe_ref[...]
            k_at = lambda r0, r1: kcat[r0:r1, :]
        else:
            qkv_ref = refs[0]
            n_in = 1
            q_at = lambda r0, r1: qkv_ref[r0:r1, :LANES]
            k_at = lambda r0, r1: qkv_ref[r0:r1, LANES:2 * LANES]
            v_at = lambda r0, r1: qkv_ref[r0:r1, 2 * LANES:]
        if has_bias:
            c_ref = refs[n_in]
            n_in += 1
        carried_in = refs[n_in:n_in + n_carried]
        n_in += n_carried
        o_ref, lse_ref = refs[n_in:n_in + 2]
        if traffic:
            carried_out = refs[n_in + 2]

            @pl.when(_first_step())
            def _():
                traffic.start(carried_in, carried_out, *sems)

        pair = pl.program_id(1)
        causal = lax.broadcasted_iota(jnp.int32, (tq, tq), 1) <= lax.broadcasted_iota(jnp.int32, (tq, tq), 0)
        o_ref[...] = jnp.zeros_like(o_ref)

        masks = [_lane_masks(pair, h, pe) for h in range(2)]

        def logits(i):
            r0, r1 = i * tq, (i + 1) * tq
            out = []
            for h in range(2):
                in_head, in_rope = masks[h]
                qm = _keep(in_head, q_at(r0, r1))
                if pe:
                    qm = jnp.concatenate([qm, _keep(in_rope, qpe_ref[r0:r1, :])], axis=1)
                if exact_scale:
                    qm = qm * scale
                spans = []
                for k0, k1 in [(r0, r1)] + ([(0, r0)] if i else []):
                    sc = _dot(qm, k_at(k0, k1), "nt")
                    if not exact_scale:
                        sc = sc * scale
                    if has_bias:
                        sc = sc - c_ref[h, :, k0:k1]
                    spans.append((k0, k1, jnp.where(causal, sc, MASKED) if k0 == r0 else sc))
                out.append(spans)
            return out

        def softmax(per_head):
            out = []
            for spans in per_head:
                m = None
                for _, _, sc in spans:
                    top = jnp.max(sc, axis=1, keepdims=True)
                    m = top if m is None else jnp.maximum(m, top)
                probs = [(k0, k1, jnp.exp(sc - m)) for k0, k1, sc in spans]
                l = sum(jnp.sum(p, axis=1, keepdims=True) for _, _, p in probs)
                out.append((m, l, probs))
            return out

        def weigh(i, per_head):
            r0, r1 = i * tq, (i + 1) * tq
            for h, (m, l, probs) in enumerate(per_head):
                acc = sum(_dot(p, v_at(k0, k1), "nn") for k0, k1, p in probs)
                o_ref[r0:r1, :] = jnp.where(masks[h][0], acc / l, o_ref[r0:r1, :])
                lse = _to_row(m + jnp.log(l))
                lse_ref[h, :, r0:r1] = lse + c_ref[h, :, r0:r1] if has_bias else lse

        ahead = logits(0)
        for i in range(nq):
            solved = softmax(ahead)
            if i + 1 < nq:
                ahead = logits(i + 1)
            weigh(i, solved)

        if traffic:
            @pl.when(_last_step(bl, PAIRS))
            def _():
                traffic.wait(carried_out, *sems)

    seq = lambda w, col: pl.BlockSpec((s, w), col)
    if pe:
        in_specs = [seq(LANES, lambda b, p: (b, p)), seq(LANES, lambda b, p: (b, PAIRS + p // 2)),
                    seq(2 * LANES, lambda b, p: (b, p)), seq(LANES, lambda b, p: (b, 0))]
        args = [ops[0], ops[0], ops[1], ops[2]]
        scratch = [pltpu.VMEM((s, 2 * LANES), BF16)]
    else:
        in_specs = [seq(3 * LANES, lambda b, p: (b, p))]
        args = [ops[0]]
        scratch = []
    per_head_row = pl.BlockSpec((2, 1, s), lambda b, p: (b * PAIRS + p, 0, 0))
    if has_bias:
        in_specs.append(per_head_row)
        args.append(bias)
    out_specs = [seq(LANES, lambda b, p: (b, p)), per_head_row]
    out_shape = [jax.ShapeDtypeStruct((t, HEADS * HEAD_DIM), F32), jax.ShapeDtypeStruct((bl * HEADS, 1, s), F32)]
    if traffic:
        in_specs += traffic.in_specs
        args += traffic.pieces
        out_specs.append(traffic.out_spec)
        out_shape.append(traffic.out_shape)
        scratch += traffic.scratch
    return pl.pallas_call(
        body, name=name, grid=(bl, PAIRS), in_specs=in_specs, out_specs=tuple(out_specs), out_shape=tuple(out_shape),
        scratch_shapes=scratch, compiler_params=_params(*(("arbitrary", "arbitrary") if traffic else ("parallel", "parallel"))),
    )(*args)


def _attn_bwd(ops, do, lse, delta, bias, scale, bl, s, tq, name, traffic=None):
    pe = len(ops) == 3
    has_bias = bias is not None
    exact_scale = math.frexp(scale)[0] == 0.5
    nq = s // tq
    t = bl * s
    width = 2 * LANES if pe else LANES
    n_carried = len(traffic.pieces) if traffic else 0

    def body(*refs):
        if pe:
            q_ref, qpe_ref, kv_ref, kpe_ref = refs[:4]
            n_in = 4
            k_at = lambda r0, r1: kv_ref[r0:r1, :LANES]
            v_at = lambda r0, r1: kv_ref[r0:r1, LANES:]
        else:
            qkv_ref = refs[0]
            n_in = 1
            k_at = lambda r0, r1: qkv_ref[r0:r1, LANES:2 * LANES]
            v_at = lambda r0, r1: qkv_ref[r0:r1, 2 * LANES:]
        do_ref, lse_ref, dl_ref = refs[n_in:n_in + 3]
        n_in += 3
        if has_bias:
            c_ref = refs[n_in]
            n_in += 1
        carried_in = refs[n_in:n_in + n_carried]
        rest = refs[n_in + n_carried:]
        if traffic:
            rest, sems = rest[:-3], rest[-3:]
            carried_out = rest[4 if pe else 2]
            rest = rest[:4 if pe else 2] + rest[(4 if pe else 2) + 1:]

            @pl.when(_first_step())
            def _():
                traffic.start(carried_in, carried_out, *sems)

        if pe:
            dqn_ref, dkv_ref, dqpe_ref, dkpe_ref, dq_acc, qcat = rest
            qcat[:, :LANES] = q_ref[...]
            qcat[:, LANES:] = qpe_ref[...]
            q_at = lambda r0, r1: qcat[r0:r1, :]
            dkv_ref[...] = jnp.zeros_like(dkv_ref)
        else:
            dqkv_ref, dc_ref, dq_acc = rest
            q_at = lambda r0, r1: qkv_ref[r0:r1, :LANES]
            dqkv_ref[...] = jnp.zeros_like(dqkv_ref)
            dc_ref[...] = jnp.zeros_like(dc_ref)
        pair = pl.program_id(1)
        dq_acc[...] = jnp.zeros_like(dq_acc)
        causal = lax.broadcasted_iota(jnp.int32, (tq, tq), 1) >= lax.broadcasted_iota(jnp.int32, (tq, tq), 0)
        if pe:
            @pl.when(pair == 0)
            def _():
                dkpe_ref[...] = jnp.zeros_like(dkpe_ref)

            @pl.when(pair % 2 == 0)
            def _():
                dqpe_ref[...] = jnp.zeros_like(dqpe_ref)

        masks = [_lane_masks(pair, h, pe) for h in range(2)]

        def logits(j):
            r0, r1 = j * tq, (j + 1) * tq
            units = []
            for h in range(2):
                in_head, in_rope = masks[h]
                kt = _keep(in_head, k_at(r0, r1))
                if pe:
                    kt = jnp.concatenate([kt, _keep(in_rope, kpe_ref[r0:r1, :])], axis=1)
                if exact_scale:
                    kt = kt * scale
                vt = _keep(in_head, v_at(r0, r1))
                ck = _to_col(c_ref[h, :, r0:r1]) if has_bias else None
                for q0, q1, diagonal in [(r0, r1, True)] + ([(r1, s, False)] if r1 < s else []):
                    qq, dd = q_at(q0, q1), do_ref[q0:q1, :]
                    st = _dot(kt, qq, "nt")
                    if not exact_scale:
                        st = st * scale
                    shift = lse_ref[h, :, q0:q1]
                    if has_bias:
                        shift = shift - c_ref[h, :, q0:q1]
                        st = st - ck
                    st = st - shift
                    if diagonal:
                        st = jnp.where(causal, st, MASKED)
                    units.append((h, q0, q1, kt, qq, dd, st, _dot(vt, dd, "nt")))
            return units

        def softmax_bwd(units):
            solved = []
            for h, q0, q1, kt, qq, dd, st, dpt in units:
                pt = jnp.exp(st)
                dst = pt * (dpt - dl_ref[h, :, q0:q1])
                solved.append((h, q0, q1, kt, qq, dd, pt, dst, (dst if exact_scale else dst * scale).astype(BF16)))
            return solved

        def products(j, solved):
            r0, r1 = j * tq, (j + 1) * tq
            dv_of, dk_of, cs_of = [None, None], [None, None], [None, None]
            add = lambda old, new: new if old is None else old + new
            for h, q0, q1, kt, qq, dd, pt, dst, dsb in solved:
                dq_acc[q0:q1, :] += _dot(dsb, kt, "tn")
                dv_of[h] = add(dv_of[h], _dot(pt, dd, "nn"))
                dk_of[h] = add(dk_of[h], _dot(dsb, qq, "nn"))
                if has_bias:
                    dc_ref[h, :, q0:q1] += jnp.sum(dst, axis=0, keepdims=True)
                    cs_of[h] = add(cs_of[h], jnp.sum(dst, axis=1, keepdims=True))
            for h in range(2):
                (in_head, in_rope), dv_c, dk_c, cs = masks[h], dv_of[h], dk_of[h], cs_of[h]
                if exact_scale:
                    dk_c = dk_c * scale
                if pe:
                    dkv_ref[r0:r1, :LANES] = jnp.where(in_head, dk_c[:, :LANES].astype(BF16), dkv_ref[r0:r1, :LANES])
                    dkv_ref[r0:r1, LANES:] = jnp.where(in_head, dv_c.astype(BF16), dkv_ref[r0:r1, LANES:])
                    dkpe_ref[r0:r1, :] += _keep(in_rope, dk_c[:, LANES:])
                else:
                    dqkv_ref[r0:r1, LANES:2 * LANES] = jnp.where(in_head, dk_c.astype(BF16), dqkv_ref[r0:r1, LANES:2 * LANES])
                    dqkv_ref[r0:r1, 2 * LANES:] = jnp.where(in_head, dv_c.astype(BF16), dqkv_ref[r0:r1, 2 * LANES:])
                    dc_ref[h, :, r0:r1] -= _to_row(cs)

        units = logits(0)
        for j in range(nq):
            solved = softmax_bwd(units)
            if j + 1 < nq:
                units = logits(j + 1)
            products(j, solved)

        if pe:
            dqn_ref[...] = dq_acc[:, :LANES].astype(BF16)
            dqpe_ref[...] += dq_acc[:, LANES:]
        else:
            dqkv_ref[:, :LANES] = dq_acc[...].astype(BF16)
        if traffic:
            @pl.when(_last_step(bl, PAIRS))
            def _():
                traffic.wait(carried_out, *sems)

    seq = lambda w, col: pl.BlockSpec((s, w), col)
    per_head_row = pl.BlockSpec((2, 1, s), lambda b, p: (b * PAIRS + p, 0, 0))
    if pe:
        in_specs = [seq(LANES, lambda b, p: (b, p)), seq(LANES, lambda b, p: (b, PAIRS + p // 2)),
                    seq(2 * LANES, lambda b, p: (b, p)), seq(LANES, lambda b, p: (b, 0))]
        args = [ops[0], ops[0], ops[1], ops[2]]
    else:
        in_specs = [seq(3 * LANES, lambda b, p: (b, p))]
        args = [ops[0]]
    in_specs += [seq(LANES, lambda b, p: (b, p)), per_head_row, per_head_row]
    args += [do, lse, delta]
    if has_bias:
        in_specs.append(per_head_row)
        args.append(bias)
    scratch = [pltpu.VMEM((s, width), F32)]
    if pe:
        out_specs = (seq(LANES, lambda b, p: (b, p)), seq(2 * LANES, lambda b, p: (b, p)),
                     seq(LANES, lambda b, p: (b, p // 2)), seq(LANES, lambda b, p: (b, 0)))
        out_shape = (jax.ShapeDtypeStruct((t, PAIRS * LANES), BF16), jax.ShapeDtypeStruct((t, PAIRS * 2 * LANES), BF16),
                     jax.ShapeDtypeStruct((t, 2 * LANES), F32), jax.ShapeDtypeStruct((t, LANES), F32))
        scratch.append(pltpu.VMEM((s, 2 * LANES), BF16))
    else:
        out_specs = (seq(3 * LANES, lambda b, p: (b, p)), per_head_row)
        out_shape = (jax.ShapeDtypeStruct((t, PAIRS * 3 * LANES), BF16), jax.ShapeDtypeStruct((bl * HEADS, 1, s), F32))
    if traffic:
        in_specs += traffic.in_specs
        args += traffic.pieces
        out_specs += (traffic.out_spec,)
        out_shape += (traffic.out_shape,)
        scratch += traffic.scratch
    return pl.pallas_call(
        body, name=name, grid=(bl, PAIRS), in_specs=in_specs, out_specs=out_specs, out_shape=out_shape,
        scratch_shapes=scratch, compiler_params=_params("arbitrary" if traffic else "parallel", "arbitrary"),
    )(*args)


def _my_place():
    return lax.axis_index("x"), lax.axis_index("y"), lax.axis_index("c")


def _flip(p, bit):
    return 1 - p if bit else p


def _relative(x, y, c, k):
    return _flip(x, k & 4), _flip(y, k & 2), _flip(c, k & 1)


def _linear(x, y, c):
    return 4 * x + 2 * y + c


class _Traffic:
    def __init__(self, kind, pieces):
        self.kind, self.pieces = kind, list(pieces)
        self.rows = [p.shape[-2] for p in self.pieces]
        self.starts = [sum(self.rows[:i]) for i in range(len(self.rows))]
        anywhere = pl.BlockSpec(memory_space=pl.ANY)
        self.in_specs = [anywhere] * len(self.pieces)
        self.out_spec = anywhere
        self.out_shape = jax.ShapeDtypeStruct((N_DEV, sum(self.rows), self.pieces[0].shape[-1]), self.pieces[0].dtype)
        self.scratch = [pltpu.SemaphoreType.DMA((7,)), pltpu.SemaphoreType.DMA((7,)), pltpu.SemaphoreType.DMA(())]

    def start(self, p_refs, out_ref, send_sems, recv_sems, local_sem):
        x, y, c = _my_place()
        me = _linear(x, y, c)
        mine = lambda i, dev: p_refs[i] if self.kind == "spread" else p_refs[i].at[dev]
        landing = lambda i: out_ref.at[me, pl.ds(self.starts[i], self.rows[i])]
        for i in range(len(p_refs)):
            pltpu.make_async_copy(mine(i, me), landing(i), local_sem).start()
        for k in range(1, N_DEV):
            peer = _relative(x, y, c, k)
            for i in range(len(p_refs)):
                pltpu.make_async_remote_copy(
                    src_ref=mine(i, _linear(*peer)), dst_ref=landing(i),
                    send_sem=send_sems.at[k - 1], recv_sem=recv_sems.at[k - 1], device_id=peer, device_id_type=MESH).start()

    def wait(self, out_ref, send_sems, recv_sems, local_sem):
        x, y, c = _my_place()
        whole = out_ref.at[_linear(x, y, c)]
        for k in range(1, N_DEV):
            both = pltpu.make_async_remote_copy(
                src_ref=whole, dst_ref=whole, send_sem=send_sems.at[k - 1], recv_sem=recv_sems.at[k - 1],
                device_id=_relative(x, y, c, k), device_id_type=MESH)
            both.wait_recv()
            both.wait_send()
        pltpu.make_async_copy(whole, whole, local_sem).wait()


def _sum_blocks(parts, name):
    n, r, cdim = parts.shape
    tr = _tile(r, 640, 16)

    def body(p_ref, o_ref):
        acc = p_ref[0].astype(F32)
        for d in range(1, n):
            acc = acc + p_ref[d].astype(F32)
        o_ref[...] = acc

    return pl.pallas_call(
        body, name=name, grid=(r // tr,), in_specs=[pl.BlockSpec((n, tr, cdim), lambda i: (0, i, 0))],
        out_specs=pl.BlockSpec((tr, cdim), lambda i: (i, 0)), out_shape=jax.ShapeDtypeStruct((r, cdim), F32),
        compiler_params=_params("parallel"),
    )(parts)


def _adamw_math(w, g, m, v):
    m = ADAM_B1 * m + (1.0 - ADAM_B1) * g
    v = ADAM_B2 * v + (1.0 - ADAM_B2) * (g * g)
    m_hat = m / (1.0 - ADAM_B1 ** ADAM_STEP)
    v_hat = v / (1.0 - ADAM_B2 ** ADAM_STEP)
    delta = -ADAM_LR * (m_hat / (jnp.sqrt(v_hat) + ADAM_EPS) + ADAM_WD * w)
    return delta, m, v


def _adamw(w, g, m, v, name):
    def body(w_ref, g_ref, m_ref, v_ref, d_ref, nm_ref, nv_ref):
        d_ref[...], nm_ref[...], nv_ref[...] = _adamw_math(w_ref[...], g_ref[...], m_ref[...], v_ref[...])

    out = jax.ShapeDtypeStruct(w.shape, F32)
    return pl.pallas_call(body, name=name, out_shape=(out, out, out),
                          compiler_params=pltpu.CompilerParams(vmem_limit_bytes=VMEM_LIMIT))(w, g, m, v)


def _small_all_reduce_adamw(parts, loss_part, ws, ms, vs, name):
    sizes = [p.shape[1] for p in parts] + [1]
    spots = [sum(-(-n // LANES) * LANES for n in sizes[:i]) for i in range(len(sizes))]
    width = spots[-1] + LANES
    k = len(parts)

    def reduce_body(*refs):
        p_refs, tot_ref, rows, send_sems, recv_sems = refs[:k + 1], *refs[k + 1:]
        x, y, c = _my_place()
        me = _linear(x, y, c)
        rows[me] = jnp.zeros((1, width), F32)
        for i in range(k + 1):
            rows[me, :, spots[i]:spots[i] + sizes[i]] = p_refs[i][...]
        copies = []
        for rel in range(1, N_DEV):
            copies.append(pltpu.make_async_remote_copy(
                src_ref=rows.at[me], dst_ref=rows.at[me], send_sem=send_sems.at[rel - 1], recv_sem=recv_sems.at[rel - 1],
                device_id=_relative(x, y, c, rel), device_id_type=MESH))
        for cp in copies:
            cp.start()
        for cp in copies:
            cp.wait_recv()
        for cp in copies:
            cp.wait_send()
        total = rows[0]
        for d in range(1, N_DEV):
            total = total + rows[d]
        tot_ref[...] = total

    total = pl.pallas_call(
        reduce_body, name=name, out_shape=jax.ShapeDtypeStruct((1, width), F32),
        scratch_shapes=[pltpu.VMEM((N_DEV, 1, width), F32), pltpu.SemaphoreType.DMA((7,)), pltpu.SemaphoreType.DMA((7,))],
    )(*parts, loss_part)

    def adamw_body(*refs):
        tot_ref, w_refs, m_refs, v_refs, outs = refs[0], refs[1:k + 1], refs[k + 1:2 * k + 1], refs[2 * k + 1:3 * k + 1], refs[3 * k + 1:]
        for i in range(k):
            g = tot_ref[:, spots[i]:spots[i] + sizes[i]]
            outs[4 * i][...] = g
            outs[4 * i + 1][...], outs[4 * i + 2][...], outs[4 * i + 3][...] = _adamw_math(w_refs[i][...], g, m_refs[i][...], v_refs[i][...])
        outs[4 * k][...] = tot_ref[:, spots[k]:spots[k] + 1]

    out_shape = [jax.ShapeDtypeStruct((1, n), F32) for n in sizes[:k] for _ in range(4)] + [jax.ShapeDtypeStruct((1, 1), F32)]
    res = pl.pallas_call(adamw_body, name=name + "_adamw", out_shape=tuple(out_shape))(total, *ws, *ms, *vs)
    return [res[4 * i:4 * i + 4] for i in range(k)], res[4 * k]


def _pad_rows(a, rows):
    return jnp.pad(a, ((0, rows - a.shape[0]), (0, 0)))


def kernel(x, positions, norm_mix_g, w_in, b_fgate, q_norm_g, w_uq, kv_norm_g, w_ukv, fox_out_g, mla_out_g, w_o, norm_ffn_g, w_gate, w_up, w_down, final_norm_g, loss_target, m_norm_mix_g, m_w_in, m_b_fgate, m_q_norm_g, m_w_uq, m_kv_norm_g, m_w_ukv, m_fox_out_g, m_mla_out_g, m_w_o, m_norm_ffn_g, m_w_gate, m_w_up, m_w_down, m_final_norm_g, v_norm_mix_g, v_w_in, v_b_fgate, v_q_norm_g, v_w_uq, v_kv_norm_g, v_w_ukv, v_fox_out_g, v_mla_out_g, v_w_o, v_norm_ffn_g, v_w_gate, v_w_up, v_w_down, v_final_norm_g):
    bl, s, d = x.shape
    t = bl * s
    bh = bl * HEADS
    tq = _tile(s, 256)
    grp = s // LANES
    fw = HEADS * HEAD_DIM
    q_rank, kv_rank = w_uq.shape[1], w_ukv.shape[1]
    in_cols = w_in.shape[2]
    n_in = N_DEV * in_cols
    ff = N_DEV * w_gate.shape[2]
    half = MLA_ROPE // 2
    o_kvlat, o_krope, o_flogit = q_rank, q_rank + kv_rank, q_rank + kv_rank + LANES
    b_cols = -(-(o_flogit + HEADS) // LANES) * LANES

    tr = lambda w: jnp.transpose(w[0])
    in_rows = -(-in_cols // 16) * 16
    uq_rows = w_uq.shape[2] * q_rank // d
    ukv_rows = w_ukv.shape[2] * kv_rank // d
    pieces = [_pad_rows(tr(w_in), in_rows), _pad_rows(tr(w_uq).reshape(uq_rows, d), -(-uq_rows // 16) * 16),
              tr(w_ukv).reshape(ukv_rows, d), w_o[0], tr(w_gate), tr(w_up), w_down[0]]
    pieces = [p.astype(BF16) for p in pieces]
    offs = [0]
    for p in pieces:
        offs.append(offs[-1] + p.shape[0])
    legs = [(0, 1), (1, 5), (5, 7)]
    gathered = {}

    def full(i, rows):
        leg = next(n for n, (lo, hi) in enumerate(legs) if lo <= i < hi)
        base = offs[legs[leg][0]]
        return gathered[leg][:, offs[i] - base:offs[i] - base + rows]

    x2d = x.reshape(t, d)
    h1, gathered[0] = _rmsnorm(x2d, 0, d, norm_mix_g, BF16, "norm_mix", traffic=_Traffic("spread", pieces[0:1]))

    w_in_t = full(0, in_cols).reshape(n_in, d)
    n_qkv = 3 * fw
    w_in_a = w_in_t[:n_qkv].reshape(3, PAIRS, LANES, d).transpose(1, 0, 2, 3).reshape(n_qkv, d)
    lat0, rope0 = n_qkv + HEADS, n_qkv + HEADS + q_rank + kv_rank
    k_rep = jnp.broadcast_to(w_in_t[rope0:].reshape(2, 1, half, d), (2, 4, half, d)).reshape(LANES, d)
    w_in_b = jnp.concatenate([w_in_t[lat0:rope0], k_rep, w_in_t[n_qkv:lat0],
                              jnp.zeros((b_cols - o_flogit - HEADS, d), BF16)], axis=0)

    def per_head_rows(a):
        return a.reshape(bl, s, HEADS).transpose(0, 2, 1).reshape(bh, 1, s)

    proj_a = _matmul(h1, w_in_a, "nt", BF16, "proj_fox", tm=1024, tn=6 * LANES)
    proj_b = _matmul(h1, w_in_b, "nt", F32, "proj_mla", tm=1024, tn=b_cols)

    z = proj_b[:, o_flogit:o_flogit + HEADS].reshape(bl, s, HEADS).transpose(0, 2, 1).reshape(bh * grp, LANES)
    bcol = jnp.broadcast_to(b_fgate.reshape(1, HEADS, 1), (bl, HEADS, grp)).reshape(bh * grp, 1)
    c = _fgate(z, bcol, grp, "forget_gate")
    c_bias = c.reshape(bh, 1, s)
    fox_o, fox_lse, gathered[1] = _attn_fwd((proj_a,), c_bias, HEAD_DIM ** -0.5, bl, s, tq, "fox_attention",
                                            traffic=_Traffic("spread", pieces[legs[1][0]:legs[1][1]]))
    w_uq_h = full(1, uq_rows).reshape(HEADS, MLA_QK, q_rank)
    w_uq_pe = jnp.concatenate([w_uq_h[:, HEAD_DIM:HEAD_DIM + half].reshape(2, 1, 4 * half, q_rank),
                               w_uq_h[:, HEAD_DIM + half:].reshape(2, 1, 4 * half, q_rank)], axis=1).reshape(2 * LANES, q_rank)
    w_uq_p = jnp.concatenate([w_uq_h[:, :HEAD_DIM].reshape(fw, q_rank), w_uq_pe], axis=0)
    w_ukv_p = full(2, ukv_rows).reshape(PAIRS, 2, 2, HEAD_DIM, kv_rank).transpose(0, 2, 1, 3, 4).reshape(2 * fw, kv_rank)
    w_o_f = full(3, w_o.shape[1]).reshape(-1, d)
    w_gate_t = full(4, ff // N_DEV).reshape(ff, d)

    inv_freq = ROPE_THETA ** (-jnp.arange(0, MLA_ROPE, 2, dtype=F32) / MLA_ROPE)
    ang = positions.astype(F32).reshape(t, 1) * inv_freq[None, :]
    cos4, sin4 = jnp.tile(jnp.cos(ang), (1, 4)), jnp.tile(jnp.sin(ang), (1, 4))
    rope_cos, rope_sin = jnp.concatenate([cos4, cos4], axis=1), jnp.concatenate([-sin4, sin4], axis=1)
    qn, kvn, q_all, kv_all, kpe = _mla_prep(proj_b, q_rank, kv_rank, q_norm_g, kv_norm_g, w_uq_p, w_ukv_p, fw,
                                            rope_cos, rope_sin, "mla_prep")
    mla_ops = (q_all, kv_all, kpe)
    mla_o, mla_lse, gathered[2] = _attn_fwd(mla_ops, None, MLA_QK ** -0.5, bl, s, tq, "mla_attention",
                                            traffic=_Traffic("spread", pieces[legs[2][0]:legs[2][1]]))
    w_up_t, w_down_f = full(5, ff // N_DEV).reshape(ff, d), full(6, ff // N_DEV).reshape(ff, d)

    both = [(d, F32), (d, BF16)]
    cat, x1, h2 = _rows_matmul([(None, w_o_f, "nn")], [fox_o, mla_o, x2d], [fox_out_g, mla_out_g, norm_ffn_g], _residual_norm,
                               [(2 * fw, BF16)] + both, [], "norm_out_proj_out_norm_ffn", prologue=_out_norm)
    act_by_gate, act_by_up, act = _ffn_up(h2, w_gate_t, w_up_t, "ffn_gate_up")
    dx2, dx2_b, dg_final, loss_part = _rows_matmul(
        [(act, w_down_f, "nn")], [x1, loss_target.reshape(t, d)], [final_norm_g.reshape(1, d)], _residual_loss_bwd,
        both, [d, 1], "ffn_down_final_norm_loss")

    d_gate, d_up = _ffn_down_bwd(dx2_b, w_down_f, act_by_gate, act_by_up, "d_ffn_down")
    dw_down = _matmul(act, dx2_b, "tn", BF16, "dw_down", tm=ff // 2, tn=d, tk=2048)
    dw_gate = _matmul(d_gate, h2, "tn", BF16, "dw_gate", tm=ff // 2, tn=d, tk=2048)
    dw_up = _matmul(d_up, h2, "tn", BF16, "dw_up", tm=ff // 2, tn=d, tk=2048)
    dx1, dx1_b, dg_ffn = _rows_matmul([(d_gate, w_gate_t, "nn"), (d_up, w_up_t, "nn")], [x1, dx2], [norm_ffn_g],
                                      _norm_bwd_residual, both, [d], "d_ffn_gate_up_norm_ffn", tm=256)
    dw_o = _matmul(cat, dx1_b, "tn", BF16, "dw_o", tn=d, tk=2048)
    d_fox_o, d_mla_o, fox_delta, mla_delta, dg_fox, dg_mla = _rows_matmul(
        [(dx1_b, w_o_f, "nt")], [fox_o, mla_o], [fox_out_g, mla_out_g], _out_norm_bwd,
        [(fw, BF16), (fw, BF16), (HEADS, F32), (HEADS, F32)], [fw, fw], "d_proj_out_norm_out")

    per_dev = lambda a: a.reshape(N_DEV, -1, d)
    late_grads = [per_dev(dw_o), per_dev(dw_gate), per_dev(dw_up), per_dev(dw_down)]
    dproj_a, dc, g_late = _attn_bwd((proj_a,), d_fox_o, fox_lse, per_head_rows(fox_delta),
                                    c_bias, HEAD_DIM ** -0.5, bl, s, tq, "d_fox_attention", traffic=_Traffic("swap", late_grads))
    dz, db_fgate = _fgate_bwd(z, bcol, dc.reshape(bh * grp, LANES), grp, "d_forget_gate")
    d_flogit = dz.reshape(bl, HEADS, s).transpose(0, 2, 1).reshape(t, HEADS)

    dq_nope, dkv_all, dq_pe, dk_pe = _attn_bwd(mla_ops, d_mla_o, mla_lse, per_head_rows(mla_delta),
                                               None, MLA_QK ** -0.5, bl, s, tq, "d_mla_attention")
    d_tail = jnp.pad(d_flogit, ((0, 0), (0, b_cols - o_flogit - HEADS)))
    dproj_b, dq_rot, dg_q, dg_kv = _mla_prep_bwd(dq_nope, dq_pe, dkv_all, dk_pe, d_tail, proj_b, q_rank, kv_rank,
                                                 q_norm_g, kv_norm_g, w_uq_p, w_ukv_p, rope_cos, rope_sin, "d_mla_prep")
    dw_uq_nope = _matmul(dq_nope, qn, "tn", BF16, "dw_uq_nope", tn=q_rank, tk=1024)
    dw_uq_pe = _matmul(dq_rot, qn, "tn", BF16, "dw_uq_rope", tn=q_rank, tk=1024)
    dw_ukv_p = _matmul(dkv_all, kvn, "tn", BF16, "dw_ukv", tn=kv_rank, tk=1024)
    dw_in_a = _matmul(dproj_a, h1, "tn", BF16, "dw_in_fox", tm=6 * LANES, tn=d, tk=2048)
    dw_in_b = _matmul(dproj_b, h1, "tn", F32, "dw_in_mla", tm=b_cols, tn=d, tk=1024)

    dw_krope = dw_in_b[o_krope:o_flogit].reshape(2, 4, half, d).sum(axis=1).reshape(MLA_ROPE, d)
    dw_in_t = jnp.concatenate([dw_in_a.reshape(PAIRS, 3, LANES, d).transpose(1, 0, 2, 3).reshape(n_qkv, d),
                               dw_in_b[o_flogit:o_flogit + HEADS].astype(BF16), dw_in_b[:o_krope].astype(BF16),
                               dw_krope.astype(BF16)], axis=0)
    pad_dev = lambda a, rows: jnp.pad(a, ((0, 0), (0, rows - a.shape[1]), (0, 0)))
    dw_uq_pe5 = dw_uq_pe.reshape(2, 2, 4, half, q_rank)
    dw_uq_h = jnp.concatenate([dw_uq_nope.reshape(HEADS, HEAD_DIM, q_rank), dw_uq_pe5[:, 0].reshape(HEADS, half, q_rank),
                               dw_uq_pe5[:, 1].reshape(HEADS, half, q_rank)], axis=1)
    dw_ukv_h = dw_ukv_p.reshape(PAIRS, 2, 2, HEAD_DIM, kv_rank).transpose(0, 2, 1, 3, 4).reshape(HEADS, 2 * HEAD_DIM, kv_rank)
    n_last = 3
    last_grads = [pad_dev(per_dev(dw_in_t), pieces[0].shape[0]), pad_dev(per_dev(dw_uq_h), pieces[1].shape[0]), per_dev(dw_ukv_h)]
    dh1_fox, g_last = _matmul(dproj_a, w_in_a, "nn", F32, "d_proj_fox", tn=d, traffic=_Traffic("swap", last_grads))
    grad_x, dg_mix = _rows_matmul([(dproj_b, w_in_b, "nn")], [x2d, dx1, dh1_fox], [norm_mix_g], _norm_bwd_residual,
                                  [(d, F32)], [d], "d_proj_mla_norm_mix")
    g_last = _sum_blocks(g_last, "sum_last_grads")
    g_late = _sum_blocks(g_late, "sum_late_grads")

    def mine(i, rows):
        src, base = (g_last, 0) if i < n_last else (g_late, offs[n_last])
        return src[offs[i] - base:offs[i] - base + rows]

    big = [
        ("w_in", w_in, m_w_in, v_w_in, mine(0, in_cols), True),
        ("w_uq", w_uq, m_w_uq, v_w_uq, mine(1, uq_rows).reshape(-1, q_rank), True),
        ("w_ukv", w_ukv, m_w_ukv, v_w_ukv, mine(2, ukv_rows).reshape(-1, kv_rank), True),
        ("w_o", w_o, m_w_o, v_w_o, mine(3, w_o.shape[1]), False),
        ("w_gate", w_gate, m_w_gate, v_w_gate, mine(4, ff // N_DEV), True),
        ("w_up", w_up, m_w_up, v_w_up, mine(5, ff // N_DEV), True),
        ("w_down", w_down, m_w_down, v_w_down, mine(6, ff // N_DEV), False),
    ]
    out = {}
    for nm, w, m, v, g, transposed in big:
        lay = (lambda a: a[0].T) if transposed else (lambda a: a[0])
        back = (lambda a: a.T[None]) if transposed else (lambda a: a[None])
        dl, new_m, new_v = _adamw(lay(w), g, lay(m), lay(v), "adamw_" + nm)
        out[nm] = (back(g), back(dl), back(new_m), back(new_v))

    smalls = [("norm_mix_g", norm_mix_g, m_norm_mix_g, v_norm_mix_g, dg_mix),
              ("b_fgate", b_fgate, m_b_fgate, v_b_fgate, db_fgate.reshape(1, HEADS)),
              ("q_norm_g", q_norm_g, m_q_norm_g, v_q_norm_g, dg_q),
              ("kv_norm_g", kv_norm_g, m_kv_norm_g, v_kv_norm_g, dg_kv),
              ("fox_out_g", fox_out_g, m_fox_out_g, v_fox_out_g, dg_fox),
              ("mla_out_g", mla_out_g, m_mla_out_g, v_mla_out_g, dg_mla),
              ("norm_ffn_g", norm_ffn_g, m_norm_ffn_g, v_norm_ffn_g, dg_ffn),
              ("final_norm_g", final_norm_g, m_final_norm_g, v_final_norm_g, dg_final)]
    flat = lambda a: a.reshape(1, -1)
    results, loss = _small_all_reduce_adamw([e[4] for e in smalls], loss_part, [flat(e[1]) for e in smalls],
                                            [flat(e[2]) for e in smalls], [flat(e[3]) for e in smalls], "reduce_small_adamw")
    for (nm, w, _, _, _), res in zip(smalls, results):
        out[nm] = tuple(a.reshape(w.shape) for a in res)
    loss = loss[0, 0]

    order = ["norm_mix_g", "w_in", "b_fgate", "q_norm_g", "w_uq", "kv_norm_g", "w_ukv", "fox_out_g", "mla_out_g", "w_o",
             "norm_ffn_g", "w_gate", "w_up", "w_down", "final_norm_g"]
    return (loss, grad_x.reshape(bl, s, d), *[out[n][0] for n in order], *[out[n][1] for n in order],
            *[out[n][2] for n in order], *[out[n][3] for n in order])
```

```python
import math

import jax
import jax.numpy as jnp
from jax import lax
from jax.experimental import pallas as pl
from jax.experimental.pallas import tpu as pltpu

F32 = jnp.float32
BF16 = jnp.bfloat16
MESH = pl.DeviceIdType.MESH

N_DEV = 8
HEADS = 8
HEAD_DIM = 64
PAIRS = HEADS // 2
MLA_ROPE = 32
MLA_QK = HEAD_DIM + MLA_ROPE
ROPE_THETA = 10000.0
NORM_EPS = 1e-6
ADAM_LR, ADAM_B1, ADAM_B2, ADAM_EPS, ADAM_WD, ADAM_STEP = 0.001, 0.9, 0.999, 1e-08, 0.01, 10

LANES = 128
MASKED = -1e30
VMEM_LIMIT = 48 * 1024 * 1024

_DIMS = {"nn": (((1,), (0,)), ((), ())), "nt": (((1,), (1,)), ((), ())), "tn": (((0,), (0,)), ((), ()))}


def _params(*sem):
    return pltpu.CompilerParams(dimension_semantics=sem, vmem_limit_bytes=VMEM_LIMIT)


def _dot(a, b, mode):
    return lax.dot_general(a.astype(BF16), b.astype(BF16), _DIMS[mode], preferred_element_type=F32)


def _tile(n, pref, unit=8):
    if n <= pref:
        return n
    t = pref - pref % unit
    while n % t:
        t -= unit
    return t


def _log2(n):
    assert n & (n - 1) == 0
    return n.bit_length() - 1


def _matmul(a, b, mode, out_dtype, name, tm=512, tn=512, tk=None, res=None, traffic=None):
    if mode == "nn":
        (m, kd), n = a.shape, b.shape[1]
    elif mode == "nt":
        (m, kd), n = a.shape, b.shape[0]
    else:
        (kd, m), n = a.shape, b.shape[1]
    tm, tn = _tile(m, tm, LANES if mode == "tn" else 16), _tile(n, tn, LANES)
    tk = kd if tk is None else _tile(kd, tk, LANES)
    nk = kd // tk
    a_spec = pl.BlockSpec((tk, tm), lambda i, j, k: (k, i)) if mode == "tn" else pl.BlockSpec((tm, tk), lambda i, j, k: (i, k))
    b_spec = pl.BlockSpec((tn, tk), lambda i, j, k: (j, k)) if mode == "nt" else pl.BlockSpec((tk, tn), lambda i, j, k: (k, j))
    o_spec = pl.BlockSpec((tm, tn), lambda i, j, k: (i, j))
    has_res = res is not None
    n_carried = len(traffic.pieces) if traffic else 0
    grid = (m // tm, n // tn, nk)

    def body(*refs):
        a_ref, b_ref = refs[:2]
        r_ref = refs[2] if has_res else None
        n_in = 2 + has_res + n_carried
        o_ref = refs[n_in]
        step = [pl.program_id(axis) for axis in range(3)]
        if traffic:
            carried_in, carried_out, sems = refs[2 + has_res:n_in], refs[n_in + 1], refs[len(refs) - 3:]

            @pl.when((step[0] == 0) & (step[1] == 0) & (step[2] == 0))
            def _():
                traffic.start(carried_in, carried_out, *sems)

        def finish(acc):
            if has_res:
                acc = acc + r_ref[...]
            o_ref[...] = acc.astype(out_dtype)

        part = _dot(a_ref[...], b_ref[...], mode)
        if nk == 1:
            finish(part)
        else:
            acc_ref = refs[n_in + 1 + bool(traffic)]

            @pl.when(step[2] == 0)
            def _():
                acc_ref[...] = part

            @pl.when(step[2] > 0)
            def _():
                acc_ref[...] += part

            @pl.when(step[2] == nk - 1)
            def _():
                finish(acc_ref[...])

        if traffic:
            @pl.when((step[0] == grid[0] - 1) & (step[1] == grid[1] - 1) & (step[2] == nk - 1))
            def _():
                traffic.wait(carried_out, *sems)

    in_specs = [a_spec, b_spec] + ([o_spec] if has_res else [])
    out_specs, out_shape = [o_spec], [jax.ShapeDtypeStruct((m, n), out_dtype)]
    scratch = [pltpu.VMEM((tm, tn), F32)] if nk > 1 else []
    if traffic:
        in_specs += traffic.in_specs
        out_specs.append(traffic.out_spec)
        out_shape.append(traffic.out_shape)
        scratch += traffic.scratch
    out = pl.pallas_call(
        body, name=name, grid=grid, in_specs=in_specs, out_specs=tuple(out_specs), out_shape=tuple(out_shape),
        scratch_shapes=scratch,
        compiler_params=_params(*(("arbitrary",) * 3 if traffic else ("parallel", "parallel", "arbitrary"))),
    )(*([a, b] + ([res] if has_res else []) + (traffic.pieces if traffic else [])))
    return out if traffic else out[0]


def _rstd(x):
    return lax.rsqrt(jnp.mean(x * x, axis=-1, keepdims=True) + NORM_EPS)


def _norm_bwd(x, g, dy):
    r = _rstd(x)
    xh = x * r
    u = dy * g
    dx = r * (u - xh * jnp.mean(u * xh, axis=-1, keepdims=True))
    return dx, jnp.sum(dy * xh, axis=0, keepdims=True)


def _rmsnorm(x, col, width, g, out_dtype, name, traffic=None):
    t = x.shape[0]
    tm = _tile(t, 512)
    steps = t // tm
    n_carried = len(traffic.pieces) if traffic else 0

    def body(*refs):
        x_ref, g_ref, o_ref = refs[0], refs[1], refs[2 + n_carried]
        if traffic:
            carried_in, carried_out, sems = refs[2:2 + n_carried], refs[3 + n_carried], refs[4 + n_carried:]

            @pl.when(pl.program_id(0) == 0)
            def _():
                traffic.start(carried_in, carried_out, *sems)

        xv = x_ref[...]
        o_ref[...] = ((xv * _rstd(xv)) * g_ref[...]).astype(out_dtype)
        if traffic:
            @pl.when(pl.program_id(0) == steps - 1)
            def _():
                traffic.wait(carried_out, *sems)

    in_specs = [pl.BlockSpec((tm, width), lambda i: (i, col)), pl.BlockSpec((1, width), lambda i: (0, 0))]
    out_specs = [pl.BlockSpec((tm, width), lambda i: (i, 0))]
    out_shape = [jax.ShapeDtypeStruct((t, width), out_dtype)]
    if traffic:
        in_specs += traffic.in_specs
        out_specs.append(traffic.out_spec)
        out_shape.append(traffic.out_shape)
    out = pl.pallas_call(
        body, name=name, grid=(steps,), in_specs=in_specs, out_specs=tuple(out_specs), out_shape=tuple(out_shape),
        scratch_shapes=traffic.scratch if traffic else [],
        compiler_params=_params("arbitrary" if traffic else "parallel"),
    )(x, g, *(traffic.pieces if traffic else []))
    return out if traffic else out[0]


def _split3(x):
    hi = x.astype(BF16)
    r1 = x - hi.astype(F32)
    mid = r1.astype(BF16)
    lo = (r1 - mid.astype(F32)).astype(BF16)
    return hi, mid, lo


def _dot_x01(x, m01):
    hi, mid, lo = _split3(x)
    d = lambda p: lax.dot_general(p, m01, _DIMS["nn"], preferred_element_type=F32)
    return (d(lo) + d(mid)) + d(hi)


def _dot_01x(m01, x):
    hi, mid, lo = _split3(x)
    d = lambda p: lax.dot_general(m01, p, _DIMS["nn"], preferred_element_type=F32)
    return (d(lo) + d(mid)) + d(hi)


def _rows_matmul(terms, rows_in, vecs_in, epilogue, rows_out, sums_out, name, tm=512, prologue=None):
    t = rows_in[0].shape[0]
    tm = _tile(t, tm, 16)
    n_rows, n_vecs = len(rows_in), len(vecs_in)
    n_ab = sum(1 + (a is not None) for a, _, _ in terms)

    halves = [slice(0, tm // 2), slice(tm // 2, tm)] if tm % 32 == 0 else [slice(0, tm)]

    def body(*refs):
        vecs = [r[...] for r in refs[n_ab + n_rows:n_ab + n_rows + n_vecs]]
        out_at = n_ab + n_rows + n_vecs
        sum_refs = refs[out_at + len(rows_out):]

        @pl.when(pl.program_id(0) == 0)
        def _():
            for ref in sum_refs:
                ref[...] = jnp.zeros_like(ref)

        staged = []
        for rows_of in halves:
            row_blocks = [r[rows_of, :] for r in refs[n_ab:n_ab + n_rows]]
            made = prologue(row_blocks, vecs) if prologue else None
            acc, at = None, 0
            for a, _, mode in terms:
                lhs = made if a is None else refs[at][rows_of, :]
                at += a is not None
                part = _dot(lhs, refs[at][...], mode)
                at += 1
                acc = part if acc is None else acc + part
            staged.append((rows_of, row_blocks, made, acc))
        for rows_of, row_blocks, made, acc in staged:
            row_vals, sum_vals = epilogue(acc, row_blocks, vecs)
            if prologue:
                row_vals = [made] + row_vals
            for ref, val, (_, dtype) in zip(refs[out_at:], row_vals, rows_out):
                ref[rows_of, :] = val.astype(dtype)
            for ref, val in zip(sum_refs, sum_vals):
                ref[...] += val

    rows = lambda w: pl.BlockSpec((tm, w), lambda i: (i, 0))
    whole = lambda a: pl.BlockSpec(a.shape, lambda i: (0, 0))
    in_specs, args = [], []
    for a, b, _ in terms:
        in_specs += ([rows(a.shape[1])] if a is not None else []) + [whole(b)]
        args += ([a] if a is not None else []) + [b]
    in_specs += [rows(r.shape[1]) for r in rows_in] + [whole(v) for v in vecs_in]
    args += list(rows_in) + list(vecs_in)
    return pl.pallas_call(
        body, name=name, grid=(t // tm,), in_specs=in_specs,
        out_specs=tuple([rows(w) for w, _ in rows_out] + [pl.BlockSpec((1, w), lambda i: (0, 0)) for w in sums_out]),
        out_shape=tuple([jax.ShapeDtypeStruct((t, w), dt) for w, dt in rows_out] + [jax.ShapeDtypeStruct((1, w), F32) for w in sums_out]),
        compiler_params=_params("arbitrary"),
    )(*args)


def _out_norm(rows, vecs):
    (f, m), (gf, gm) = rows[:2], vecs[:2]
    return jnp.concatenate([((f * _rstd(f)) * gf).astype(BF16), ((m * _rstd(m)) * gm).astype(BF16)], axis=1)


def _residual_norm(acc, rows, vecs):
    x1 = rows[-1] + acc
    return [x1, (x1 * _rstd(x1)) * vecs[-1]], []


def _residual_loss_bwd(acc, rows, vecs):
    x2, gv = rows[0] + acc, vecs[0]
    diff = (x2 * _rstd(x2)) * gv - rows[1]
    dx, dg = _norm_bwd(x2, gv, diff / x2.shape[1])
    return [dx, dx], [dg, 0.5 * jnp.sum(jnp.mean(diff * diff, axis=-1, keepdims=True), axis=0, keepdims=True)]


def _norm_bwd_residual(acc, rows, vecs):
    dy = acc + rows[2] if len(rows) > 2 else acc
    dx, dg = _norm_bwd(rows[0], vecs[0], dy)
    if len(rows) > 1:
        dx = dx + rows[1]
    return [dx, dx], [dg]


def _out_norm_bwd(acc, rows, vecs):
    (f, m), w = rows, rows[0].shape[1]
    nh = w // HEAD_DIM
    lane_head = lax.shift_right_logical(lax.broadcasted_iota(jnp.int32, (w, nh), 0), _log2(HEAD_DIM))
    sel = (lane_head == lax.broadcasted_iota(jnp.int32, (w, nh), 1)).astype(BF16)
    dfo, dgf = _norm_bwd(f, vecs[0], acc[:, :w])
    dmo, dgm = _norm_bwd(m, vecs[1], acc[:, w:])
    return [dfo, dmo, _dot_x01(dfo * f, sel), _dot_x01(dmo * m, sel)], [dgf, dgm]


def _ffn_up(h, wg_t, wu_t, name, tm=512, tf=1408):
    t, d = h.shape
    f = wg_t.shape[0]
    tm, tf = _tile(t, tm, 16), _tile(f, tf, LANES)
    tok = pl.BlockSpec((tm, tf), lambda j, i: (i, j))
    wt = pl.BlockSpec((tf, d), lambda j, i: (j, 0))

    def body(h_ref, wg_ref, wu_ref, dg_ref, du_ref, a_ref):
        hv = h_ref[...]
        g, u = _dot(hv, wg_ref[...], "nt"), _dot(hv, wu_ref[...], "nt")
        sg = jax.nn.sigmoid(g)
        silu = g * sg
        dg_ref[...] = (u * (sg * (1.0 + g * (1.0 - sg)))).astype(BF16)
        du_ref[...] = silu.astype(BF16)
        a_ref[...] = (silu * u).astype(BF16)

    return pl.pallas_call(
        body, name=name, grid=(f // tf, t // tm), in_specs=[pl.BlockSpec((tm, d), lambda j, i: (i, 0)), wt, wt],
        out_specs=(tok, tok, tok),
        out_shape=(jax.ShapeDtypeStruct((t, f), BF16), jax.ShapeDtypeStruct((t, f), BF16), jax.ShapeDtypeStruct((t, f), BF16)),
        compiler_params=_params("parallel", "parallel"),
    )(h, wg_t, wu_t)


def _ffn_down_bwd(dy, w_down, act_by_gate, act_by_up, name, tm=512, tf=1408):
    t, d = dy.shape
    f = w_down.shape[0]
    tm, tf = _tile(t, tm, 16), _tile(f, tf, LANES)
    tok = pl.BlockSpec((tm, tf), lambda j, i: (i, j))

    def body(dy_ref, w_ref, g_ref, u_ref, dg_ref, du_ref):
        da = _dot(dy_ref[...], w_ref[...], "nt")
        dg_ref[...] = (da * g_ref[...].astype(F32)).astype(BF16)
        du_ref[...] = (da * u_ref[...].astype(F32)).astype(BF16)

    return pl.pallas_call(
        body, name=name, grid=(f // tf, t // tm),
        in_specs=[pl.BlockSpec((tm, d), lambda j, i: (i, 0)), pl.BlockSpec((tf, d), lambda j, i: (j, 0)), tok, tok],
        out_specs=(tok, tok),
        out_shape=(jax.ShapeDtypeStruct((t, f), BF16), jax.ShapeDtypeStruct((t, f), BF16)),
        compiler_params=_params("parallel", "parallel"),
    )(dy, w_down, act_by_gate, act_by_up)


def _chunk_scan_mats(rows, grp, reverse):
    ii = lax.broadcasted_iota(jnp.int32, (LANES, LANES), 0)
    jj = lax.broadcasted_iota(jnp.int32, (LANES, LANES), 1)
    within = ((ii >= jj) if reverse else (ii <= jj)).astype(BF16)
    ones = jnp.ones((LANES, LANES), BF16)
    ri = lax.broadcasted_iota(jnp.int32, (rows, rows), 0)
    rj = lax.broadcasted_iota(jnp.int32, (rows, rows), 1)
    sh = _log2(grp)
    same = lax.shift_right_logical(ri, sh) == lax.shift_right_logical(rj, sh)
    across = (same & ((rj > ri) if reverse else (rj < ri))).astype(BF16)
    return within, ones, across


def _running_sum(v, mats):
    within, ones, across = mats
    return _dot_x01(v, within) + _dot_01x(across, _dot_x01(v, ones))


def _fgate(z, bcol, grp, name):
    rows = z.shape[0]

    def body(z_ref, b_ref, c_ref):
        zz = z_ref[...] + b_ref[...]
        log_f = jnp.minimum(zz, 0.0) - jnp.log1p(jnp.exp(-jnp.abs(zz)))
        c_ref[...] = _running_sum(log_f, _chunk_scan_mats(rows, grp, False))

    return pl.pallas_call(body, name=name, out_shape=jax.ShapeDtypeStruct(z.shape, F32),
                          compiler_params=pltpu.CompilerParams(vmem_limit_bytes=VMEM_LIMIT))(z, bcol)


def _fgate_bwd(z, bcol, dc, grp, name):
    rows = z.shape[0]

    def body(z_ref, b_ref, dc_ref, dz_ref, db_ref):
        zz = z_ref[...] + b_ref[...]
        dz = _running_sum(dc_ref[...], _chunk_scan_mats(rows, grp, True)) * jax.nn.sigmoid(-zz)
        dz_ref[...] = dz
        head = lax.shift_right_logical(lax.broadcasted_iota(jnp.int32, (HEADS, rows), 1), _log2(grp)) & (HEADS - 1)
        sel = (head == lax.broadcasted_iota(jnp.int32, (HEADS, rows), 0)).astype(BF16)
        db_ref[...] = jnp.sum(_dot_01x(sel, dz), axis=1, keepdims=True)

    return pl.pallas_call(
        body, name=name,
        out_shape=(jax.ShapeDtypeStruct(z.shape, F32), jax.ShapeDtypeStruct((HEADS, 1), F32)),
        compiler_params=pltpu.CompilerParams(vmem_limit_bytes=VMEM_LIMIT),
    )(z, bcol, dc)


def _rotate(x, cs, sn_signed):
    return x * cs + pltpu.roll(x, LANES // 2, axis=1) * sn_signed


def _mla_prep(proj_b, q_rank, kv_rank, gq, gkv, w_uq_p, w_ukv_p, nope, cs, sn, name):
    t, bw = proj_b.shape
    qw, kvw = w_uq_p.shape[0], w_ukv_p.shape[0]
    tm = _tile(t, 512)
    rows = lambda w: pl.BlockSpec((tm, w), lambda i: (i, 0))
    whole = lambda a: pl.BlockSpec(a.shape, lambda i: (0, 0))

    def body(pb_ref, gq_ref, gkv_ref, wq_ref, wkv_ref, c_ref, s_ref, qn_ref, kvn_ref, q_ref, kv_ref, kpe_ref):
        c, s = c_ref[...], s_ref[...]
        ql, kvl = pb_ref[:, :q_rank], pb_ref[:, q_rank:q_rank + kv_rank]
        qn = ((ql * _rstd(ql)) * gq_ref[...]).astype(BF16)
        kvn = ((kvl * _rstd(kvl)) * gkv_ref[...]).astype(BF16)
        qn_ref[...], kvn_ref[...] = qn, kvn
        q_raw = _dot(qn, wq_ref[...], "nt")
        q_ref[:, :nope] = q_raw[:, :nope].astype(BF16)
        for off in range(nope, qw, LANES):
            q_ref[:, off:off + LANES] = _rotate(q_raw[:, off:off + LANES], c, s).astype(BF16)
        kv_ref[...] = _dot(kvn, wkv_ref[...], "nt").astype(BF16)
        kpe_ref[...] = _rotate(pb_ref[:, q_rank + kv_rank:q_rank + kv_rank + LANES], c, s).astype(BF16)

    return pl.pallas_call(
        body, name=name, grid=(t // tm,),
        in_specs=[rows(bw), whole(gq), whole(gkv), whole(w_uq_p), whole(w_ukv_p), rows(LANES), rows(LANES)],
        out_specs=(rows(q_rank), rows(kv_rank), rows(qw), rows(kvw), rows(LANES)),
        out_shape=(jax.ShapeDtypeStruct((t, q_rank), BF16), jax.ShapeDtypeStruct((t, kv_rank), BF16),
                   jax.ShapeDtypeStruct((t, qw), BF16), jax.ShapeDtypeStruct((t, kvw), BF16), jax.ShapeDtypeStruct((t, LANES), BF16)),
        compiler_params=_params("parallel"),
    )(proj_b, gq, gkv, w_uq_p, w_ukv_p, cs, sn)


def _mla_prep_bwd(dq_nope, dq_pe, dkv_all, dk_pe, d_tail, proj_b, q_rank, kv_rank, gq, gkv, w_uq_p, w_ukv_p, cs, sn, name):
    t, bw = proj_b.shape
    nope, pw = dq_nope.shape[1], dq_pe.shape[1]
    tm = _tile(t, 512)
    rows = lambda w: pl.BlockSpec((tm, w), lambda i: (i, 0))
    whole = lambda a: pl.BlockSpec(a.shape, lambda i: (0, 0))
    o_k = q_rank + kv_rank

    def body(dqn_ref, dqp_ref, dkv_ref, dkp_ref, dt_ref, pb_ref, gq_ref, gkv_ref, wq_ref, wkv_ref, c_ref, s_ref,
             dpb_ref, dqr_ref, dgq_ref, dgkv_ref):
        c, s = c_ref[...], -s_ref[...]
        for off in range(0, pw, LANES):
            dqr_ref[:, off:off + LANES] = _rotate(dqp_ref[:, off:off + LANES], c, s).astype(BF16)
        d_qn = _dot(dqn_ref[...], wq_ref[:nope, :], "nn") + _dot(dqr_ref[...], wq_ref[nope:, :], "nn")
        dq_lat, dgq = _norm_bwd(pb_ref[:, :q_rank], gq_ref[...], d_qn)
        dkv_lat, dgkv = _norm_bwd(pb_ref[:, q_rank:o_k], gkv_ref[...], _dot(dkv_ref[...], wkv_ref[...], "nn"))
        dpb_ref[:, :q_rank] = dq_lat.astype(BF16)
        dpb_ref[:, q_rank:o_k] = dkv_lat.astype(BF16)
        dpb_ref[:, o_k:o_k + LANES] = _rotate(dkp_ref[...], c, s).astype(BF16)
        dpb_ref[:, o_k + LANES:] = dt_ref[...].astype(BF16)

        @pl.when(pl.program_id(0) == 0)
        def _():
            dgq_ref[...] = jnp.zeros_like(dgq_ref)
            dgkv_ref[...] = jnp.zeros_like(dgkv_ref)

        dgq_ref[...] += dgq
        dgkv_ref[...] += dgkv

    return pl.pallas_call(
        body, name=name, grid=(t // tm,),
        in_specs=[rows(nope), rows(pw), rows(dkv_all.shape[1]), rows(LANES), rows(bw - o_k - LANES), rows(bw), whole(gq), whole(gkv),
                  whole(w_uq_p), whole(w_ukv_p), rows(LANES), rows(LANES)],
        out_specs=(rows(bw), rows(pw), whole(gq), whole(gkv)),
        out_shape=(jax.ShapeDtypeStruct((t, bw), BF16), jax.ShapeDtypeStruct((t, pw), BF16),
                   jax.ShapeDtypeStruct(gq.shape, F32), jax.ShapeDtypeStruct(gkv.shape, F32)),
        compiler_params=_params("arbitrary"),
    )(dq_nope, dq_pe, dkv_all, dk_pe, d_tail, proj_b, gq, gkv, w_uq_p, w_ukv_p, cs, sn)


def _lane_masks(pair, h, pe):
    lane = lax.broadcasted_iota(jnp.int32, (1, LANES), 1)
    in_head = lax.shift_right_logical(lane, _log2(HEAD_DIM)) == h
    in_rope = ((lax.shift_right_logical(lane, _log2(MLA_ROPE // 2)) & 3) == ((2 * pair + h) & 3)) if pe else None
    return in_head, in_rope


def _keep(mask, v):
    return jnp.where(mask, v, jnp.zeros_like(v))


def _to_row(col):
    n = col.shape[0]
    eye = lax.broadcasted_iota(jnp.int32, (n, n), 0) == lax.broadcasted_iota(jnp.int32, (n, n), 1)
    return jnp.sum(jnp.where(eye, col, 0.0), axis=0, keepdims=True)


def _to_col(row):
    n = row.shape[1]
    eye = lax.broadcasted_iota(jnp.int32, (n, n), 0) == lax.broadcasted_iota(jnp.int32, (n, n), 1)
    return jnp.sum(jnp.where(eye, row, 0.0), axis=1, keepdims=True)


def _first_step():
    return (pl.program_id(0) == 0) & (pl.program_id(1) == 0)


def _last_step(n0, n1):
    return (pl.program_id(0) == n0 - 1) & (pl.program_id(1) == n1 - 1)


def _attn_fwd(ops, bias, scale, bl, s, tq, name, traffic=None):
    pe = len(ops) == 3
    has_bias = bias is not None
    exact_scale = math.frexp(scale)[0] == 0.5
    nq = s // tq
    t = bl * s
    n_carried = len(traffic.pieces) if traffic else 0

    def body(*refs):
        sems = refs[len(refs) - 3:] if traffic else ()
        if pe:
            q_ref, qpe_ref, kv_ref, kpe_ref = refs[:4]
            n_in = 4
            q_at = lambda r0, r1: q_ref[r0:r1, :]
            v_at = lambda r0, r1: kv_ref[r0:r1, LANES:]
            kcat = refs[len(refs) - 1 - len(sems)]
            kcat[:, :LANES] = kv_ref[:, :LANES]
            kcat[:, LANES:] = kpe_ref[...]
            k_at = lambda r0, r1: kcat[r0:r1, :]
        else:
            qkv_ref = refs[0]
            n_in = 1
            q_at = lambda r0, r1: qkv_ref[r0:r1, :LANES]
            k_at = lambda r0, r1: qkv_ref[r0:r1, LANES:2 * LANES]
            v_at = lambda r0, r1: qkv_ref[r0:r1, 2 * LANES:]
        if has_bias:
            c_ref = refs[n_in]
            n_in += 1
        carried_in = refs[n_in:n_in + n_carried]
        n_in += n_carried
        o_ref, lse_ref = refs[n_in:n_in + 2]
        if traffic:
            carried_out = refs[n_in + 2]

            @pl.when(_first_step())
            def _():
                traffic.start(carried_in, carried_out, *sems)

        pair = pl.program_id(1)
        causal = lax.broadcasted_iota(jnp.int32, (tq, tq), 1) <= lax.broadcasted_iota(jnp.int32, (tq, tq), 0)
        o_ref[...] = jnp.zeros_like(o_ref)

        masks = [_lane_masks(pair, h, pe) for h in range(2)]

        def logits(i):
            r0, r1 = i * tq, (i + 1) * tq
            out = []
            for h in range(2):
                in_head, in_rope = masks[h]
                qm = _keep(in_head, q_at(r0, r1))
                if pe:
                    qm = jnp.concatenate([qm, _keep(in_rope, qpe_ref[r0:r1, :])], axis=1)
                if exact_scale:
                    qm = qm * scale
                spans = []
                for k0, k1 in [(r0, r1)] + ([(0, r0)] if i else []):
                    sc = _dot(qm, k_at(k0, k1), "nt")
                    if not exact_scale:
                        sc = sc * scale
                    if has_bias:
                        sc = sc - c_ref[h, :, k0:k1]
                    spans.append((k0, k1, jnp.where(causal, sc, MASKED) if k0 == r0 else sc))
                out.append(spans)
            return out

        def softmax(per_head):
            out = []
            for spans in per_head:
                m = None
                for _, _, sc in spans:
                    top = jnp.max(sc, axis=1, keepdims=True)
                    m = top if m is None else jnp.maximum(m, top)
                probs = [(k0, k1, jnp.exp(sc - m)) for k0, k1, sc in spans]
                l = sum(jnp.sum(p, axis=1, keepdims=True) for _, _, p in probs)
                out.append((m, l, probs))
            return out

        def weigh(i, per_head):
            r0, r1 = i * tq, (i + 1) * tq
            for h, (m, l, probs) in enumerate(per_head):
                acc = sum(_dot(p, v_at(k0, k1), "nn") for k0, k1, p in probs)
                o_ref[r0:r1, :] = jnp.where(masks[h][0], acc / l, o_ref[r0:r1, :])
                lse = _to_row(m + jnp.log(l))
                lse_ref[h, :, r0:r1] = lse + c_ref[h, :, r0:r1] if has_bias else lse

        ahead = logits(0)
        for i in range(nq):
            solved = softmax(ahead)
            if i + 1 < nq:
                ahead = logits(i + 1)
            weigh(i, solved)

        if traffic:
            @pl.when(_last_step(bl, PAIRS))
            def _():
                traffic.wait(carried_out, *sems)

    seq = lambda w, col: pl.BlockSpec((s, w), col)
    if pe:
        in_specs = [seq(LANES, lambda b, p: (b, p)), seq(LANES, lambda b, p: (b, PAIRS + p // 2)),
                    seq(2 * LANES, lambda b, p: (b, p)), seq(LANES, lambda b, p: (b, 0))]
        args = [ops[0], ops[0], ops[1], ops[2]]
        scratch = [pltpu.VMEM((s, 2 * LANES), BF16)]
    else:
        in_specs = [seq(3 * LANES, lambda b, p: (b, p))]
        args = [ops[0]]
        scratch = []
    per_head_row = pl.BlockSpec((2, 1, s), lambda b, p: (b * PAIRS + p, 0, 0))
    if has_bias:
        in_specs.append(per_head_row)
        args.append(bias)
    out_specs = [seq(LANES, lambda b, p: (b, p)), per_head_row]
    out_shape = [jax.ShapeDtypeStruct((t, HEADS * HEAD_DIM), F32), jax.ShapeDtypeStruct((bl * HEADS, 1, s), F32)]
    if traffic:
        in_specs += traffic.in_specs
        args += traffic.pieces
        out_specs.append(traffic.out_spec)
        out_shape.append(traffic.out_shape)
        scratch += traffic.scratch
    return pl.pallas_call(
        body, name=name, grid=(bl, PAIRS), in_specs=in_specs, out_specs=tuple(out_specs), out_shape=tuple(out_shape),
        scratch_shapes=scratch, compiler_params=_params(*(("arbitrary", "arbitrary") if traffic else ("parallel", "parallel"))),
    )(*args)


def _attn_bwd(ops, do, lse, delta, bias, scale, bl, s, tq, name, traffic=None):
    pe = len(ops) == 3
    has_bias = bias is not None
    exact_scale = math.frexp(scale)[0] == 0.5
    nq = s // tq
    t = bl * s
    width = 2 * LANES if pe else LANES
    n_carried = len(traffic.pieces) if traffic else 0

    def body(*refs):
        if pe:
            q_ref, qpe_ref, kv_ref, kpe_ref = refs[:4]
            n_in = 4
            k_at = lambda r0, r1: kv_ref[r0:r1, :LANES]
            v_at = lambda r0, r1: kv_ref[r0:r1, LANES:]
        else:
            qkv_ref = refs[0]
            n_in = 1
            k_at = lambda r0, r1: qkv_ref[r0:r1, LANES:2 * LANES]
            v_at = lambda r0, r1: qkv_ref[r0:r1, 2 * LANES:]
        do_ref, lse_ref, dl_ref = refs[n_in:n_in + 3]
        n_in += 3
        if has_bias:
            c_ref = refs[n_in]
            n_in += 1
        carried_in = refs[n_in:n_in + n_carried]
        rest = refs[n_in + n_carried:]
        if traffic:
            rest, sems = rest[:-3], rest[-3:]
            carried_out = rest[4 if pe else 2]
            rest = rest[:4 if pe else 2] + rest[(4 if pe else 2) + 1:]

            @pl.when(_first_step())
            def _():
                traffic.start(carried_in, carried_out, *sems)

        if pe:
            dqn_ref, dkv_ref, dqpe_ref, dkpe_ref, dq_acc, qcat = rest
            qcat[:, :LANES] = q_ref[...]
            qcat[:, LANES:] = qpe_ref[...]
            q_at = lambda r0, r1: qcat[r0:r1, :]
            dkv_ref[...] = jnp.zeros_like(dkv_ref)
        else:
            dqkv_ref, dc_ref, dq_acc = rest
            q_at = lambda r0, r1: qkv_ref[r0:r1, :LANES]
            dqkv_ref[...] = jnp.zeros_like(dqkv_ref)
            dc_ref[...] = jnp.zeros_like(dc_ref)
        pair = pl.program_id(1)
        dq_acc[...] = jnp.zeros_like(dq_acc)
        causal = lax.broadcasted_iota(jnp.int32, (tq, tq), 1) >= lax.broadcasted_iota(jnp.int32, (tq, tq), 0)
        if pe:
            @pl.when(pair == 0)
            def _():
                dkpe_ref[...] = jnp.zeros_like(dkpe_ref)

            @pl.when(pair % 2 == 0)
            def _():
                dqpe_ref[...] = jnp.zeros_like(dqpe_ref)

        masks = [_lane_masks(pair, h, pe) for h in range(2)]

        def logits(j):
            r0, r1 = j * tq, (j + 1) * tq
            units = []
            for h in range(2):
                in_head, in_rope = masks[h]
                kt = _keep(in_head, k_at(r0, r1))
                if pe:
                    kt = jnp.concatenate([kt, _keep(in_rope, kpe_ref[r0:r1, :])], axis=1)
                if exact_scale:
                    kt = kt * scale
                vt = _keep(in_head, v_at(r0, r1))
                ck = _to_col(c_ref[h, :, r0:r1]) if has_bias else None
                for q0, q1, diagonal in [(r0, r1, True)] + ([(r1, s, False)] if r1 < s else []):
                    qq, dd = q_at(q0, q1), do_ref[q0:q1, :]
                    st = _dot(kt, qq, "nt")
                    if not exact_scale:
                        st = st * scale
                    shift = lse_ref[h, :, q0:q1]
                    if has_bias:
                        shift = shift - c_ref[h, :, q0:q1]
                        st = st - ck
                    st = st - shift
                    if diagonal:
                        st = jnp.where(causal, st, MASKED)
                    units.append((h, q0, q1, kt, qq, dd, st, _dot(vt, dd, "nt")))
            return units

        def softmax_bwd(units):
            solved = []
            for h, q0, q1, kt, qq, dd, st, dpt in units:
                pt = jnp.exp(st)
                dst = pt * (dpt - dl_ref[h, :, q0:q1])
                solved.append((h, q0, q1, kt, qq, dd, pt, dst, (dst if exact_scale else dst * scale).astype(BF16)))
            return solved

        def products(j, solved):
            r0, r1 = j * tq, (j + 1) * tq
            dv_of, dk_of, cs_of = [None, None], [None, None], [None, None]
            add = lambda old, new: new if old is None else old + new
            for h, q0, q1, kt, qq, dd, pt, dst, dsb in solved:
                dq_acc[q0:q1, :] += _dot(dsb, kt, "tn")
                dv_of[h] = add(dv_of[h], _dot(pt, dd, "nn"))
                dk_of[h] = add(dk_of[h], _dot(dsb, qq, "nn"))
                if has_bias:
                    dc_ref[h, :, q0:q1] += jnp.sum(dst, axis=0, keepdims=True)
                    cs_of[h] = add(cs_of[h], jnp.sum(dst, axis=1, keepdims=True))
            for h in range(2):
                (in_head, in_rope), dv_c, dk_c, cs = masks[h], dv_of[h], dk_of[h], cs_of[h]
                if exact_scale:
                    dk_c = dk_c * scale
                if pe:
                    dkv_ref[r0:r1, :LANES] = jnp.where(in_head, dk_c[:, :LANES].astype(BF16), dkv_ref[r0:r1, :LANES])
                    dkv_ref[r0:r1, LANES:] = jnp.where(in_head, dv_c.astype(BF16), dkv_ref[r0:r1, LANES:])
                    dkpe_ref[r0:r1, :] += _keep(in_rope, dk_c[:, LANES:])
                else:
                    dqkv_ref[r0:r1, LANES:2 * LANES] = jnp.where(in_head, dk_c.astype(BF16), dqkv_ref[r0:r1, LANES:2 * LANES])
                    dqkv_ref[r0:r1, 2 * LANES:] = jnp.where(in_head, dv_c.astype(BF16), dqkv_ref[r0:r1, 2 * LANES:])
                    dc_ref[h, :, r0:r1] -= _to_row(cs)

        units = logits(0)
        for j in range(nq):
            solved = softmax_bwd(units)
            if j + 1 < nq:
                units = logits(j + 1)
            products(j, solved)

        if pe:
            dqn_ref[...] = dq_acc[:, :LANES].astype(BF16)
            dqpe_ref[...] += dq_acc[:, LANES:]
        else:
            dqkv_ref[:, :LANES] = dq_acc[...].astype(BF16)
        if traffic:
            @pl.when(_last_step(bl, PAIRS))
            def _():
                traffic.wait(carried_out, *sems)

    seq = lambda w, col: pl.BlockSpec((s, w), col)
    per_head_row = pl.BlockSpec((2, 1, s), lambda b, p: (b * PAIRS + p, 0, 0))
    if pe:
        in_specs = [seq(LANES, lambda b, p: (b, p)), seq(LANES, lambda b, p: (b, PAIRS + p // 2)),
                    seq(2 * LANES, lambda b, p: (b, p)), seq(LANES, lambda b, p: (b, 0))]
        args = [ops[0], ops[0], ops[1], ops[2]]
    else:
        in_specs = [seq(3 * LANES, lambda b, p: (b, p))]
        args = [ops[0]]
    in_specs += [seq(LANES, lambda b, p: (b, p)), per_head_row, per_head_row]
    args += [do, lse, delta]
    if has_bias:
        in_specs.append(per_head_row)
        args.append(bias)
    scratch = [pltpu.VMEM((s, width), F32)]
    if pe:
        out_specs = (seq(LANES, lambda b, p: (b, p)), seq(2 * LANES, lambda b, p: (b, p)),
                     seq(LANES, lambda b, p: (b, p // 2)), seq(LANES, lambda b, p: (b, 0)))
        out_shape = (jax.ShapeDtypeStruct((t, PAIRS * LANES), BF16), jax.ShapeDtypeStruct((t, PAIRS * 2 * LANES), BF16),
                     jax.ShapeDtypeStruct((t, 2 * LANES), F32), jax.ShapeDtypeStruct((t, LANES), F32))
        scratch.append(pltpu.VMEM((s, 2 * LANES), BF16))
    else:
        out_specs = (seq(3 * LANES, lambda b, p: (b, p)), per_head_row)
        out_shape = (jax.ShapeDtypeStruct((t, PAIRS * 3 * LANES), BF16), jax.ShapeDtypeStruct((bl * HEADS, 1, s), F32))
    if traffic:
        in_specs += traffic.in_specs
        args += traffic.pieces
        out_specs += (traffic.out_spec,)
        out_shape += (traffic.out_shape,)
        scratch += traffic.scratch
    return pl.pallas_call(
        body, name=name, grid=(bl, PAIRS), in_specs=in_specs, out_specs=out_specs, out_shape=out_shape,
        scratch_shapes=scratch, compiler_params=_params("arbitrary" if traffic else "parallel", "arbitrary"),
    )(*args)


def _my_place():
    return lax.axis_index("x"), lax.axis_index("y"), lax.axis_index("c")


def _flip(p, bit):
    return 1 - p if bit else p


def _relative(x, y, c, k):
    return _flip(x, k & 4), _flip(y, k & 2), _flip(c, k & 1)


def _linear(x, y, c):
    return 4 * x + 2 * y + c


class _Traffic:
    def __init__(self, kind, pieces):
        self.kind, self.pieces = kind, list(pieces)
        self.rows = [p.shape[-2] for p in self.pieces]
        self.starts = [sum(self.rows[:i]) for i in range(len(self.rows))]
        anywhere = pl.BlockSpec(memory_space=pl.ANY)
        self.in_specs = [anywhere] * len(self.pieces)
        self.out_spec = anywhere
        self.out_shape = jax.ShapeDtypeStruct((N_DEV, sum(self.rows), self.pieces[0].shape[-1]), self.pieces[0].dtype)
        self.scratch = [pltpu.SemaphoreType.DMA((7,)), pltpu.SemaphoreType.DMA((7,)), pltpu.SemaphoreType.DMA(())]

    def start(self, p_refs, out_ref, send_sems, recv_sems, local_sem):
        x, y, c = _my_place()
        me = _linear(x, y, c)
        mine = lambda i, dev: p_refs[i] if self.kind == "spread" else p_refs[i].at[dev]
        landing = lambda i: out_ref.at[me, pl.ds(self.starts[i], self.rows[i])]
        for i in range(len(p_refs)):
            pltpu.make_async_copy(mine(i, me), landing(i), local_sem).start()
        for k in range(1, N_DEV):
            peer = _relative(x, y, c, k)
            for i in range(len(p_refs)):
                pltpu.make_async_remote_copy(
                    src_ref=mine(i, _linear(*peer)), dst_ref=landing(i),
                    send_sem=send_sems.at[k - 1], recv_sem=recv_sems.at[k - 1], device_id=peer, device_id_type=MESH).start()

    def wait(self, out_ref, send_sems, recv_sems, local_sem):
        x, y, c = _my_place()
        whole = out_ref.at[_linear(x, y, c)]
        for k in range(1, N_DEV):
            both = pltpu.make_async_remote_copy(
                src_ref=whole, dst_ref=whole, send_sem=send_sems.at[k - 1], recv_sem=recv_sems.at[k - 1],
                device_id=_relative(x, y, c, k), device_id_type=MESH)
            both.wait_recv()
            both.wait_send()
        pltpu.make_async_copy(whole, whole, local_sem).wait()


def _sum_blocks(parts, name):
    n, r, cdim = parts.shape
    tr = _tile(r, 640, 16)

    def body(p_ref, o_ref):
        acc = p_ref[0].astype(F32)
        for d in range(1, n):
            acc = acc + p_ref[d].astype(F32)
        o_ref[...] = acc

    return pl.pallas_call(
        body, name=name, grid=(r // tr,), in_specs=[pl.BlockSpec((n, tr, cdim), lambda i: (0, i, 0))],
        out_specs=pl.BlockSpec((tr, cdim), lambda i: (i, 0)), out_shape=jax.ShapeDtypeStruct((r, cdim), F32),
        compiler_params=_params("parallel"),
    )(parts)


def _adamw_math(w, g, m, v):
    m = ADAM_B1 * m + (1.0 - ADAM_B1) * g
    v = ADAM_B2 * v + (1.0 - ADAM_B2) * (g * g)
    m_hat = m / (1.0 - ADAM_B1 ** ADAM_STEP)
    v_hat = v / (1.0 - ADAM_B2 ** ADAM_STEP)
    delta = -ADAM_LR * (m_hat / (jnp.sqrt(v_hat) + ADAM_EPS) + ADAM_WD * w)
    return delta, m, v


def _adamw(w, g, m, v, name):
    def body(w_ref, g_ref, m_ref, v_ref, d_ref, nm_ref, nv_ref):
        d_ref[...], nm_ref[...], nv_ref[...] = _adamw_math(w_ref[...], g_ref[...], m_ref[...], v_ref[...])

    out = jax.ShapeDtypeStruct(w.shape, F32)
    return pl.pallas_call(body, name=name, out_shape=(out, out, out),
                          compiler_params=pltpu.CompilerParams(vmem_limit_bytes=VMEM_LIMIT))(w, g, m, v)


def _small_all_reduce_adamw(parts, loss_part, ws, ms, vs, name):
    sizes = [p.shape[1] for p in parts] + [1]
    spots = [sum(-(-n // LANES) * LANES for n in sizes[:i]) for i in range(len(sizes))]
    width = spots[-1] + LANES
    k = len(parts)

    def reduce_body(*refs):
        p_refs, tot_ref, rows, send_sems, recv_sems = refs[:k + 1], *refs[k + 1:]
        x, y, c = _my_place()
        me = _linear(x, y, c)
        rows[me] = jnp.zeros((1, width), F32)
        for i in range(k + 1):
            rows[me, :, spots[i]:spots[i] + sizes[i]] = p_refs[i][...]
        copies = []
        for rel in range(1, N_DEV):
            copies.append(pltpu.make_async_remote_copy(
                src_ref=rows.at[me], dst_ref=rows.at[me], send_sem=send_sems.at[rel - 1], recv_sem=recv_sems.at[rel - 1],
                device_id=_relative(x, y, c, rel), device_id_type=MESH))
        for cp in copies:
            cp.start()
        for cp in copies:
            cp.wait_recv()
        for cp in copies:
            cp.wait_send()
        total = rows[0]
        for d in range(1, N_DEV):
            total = total + rows[d]
        tot_ref[...] = total

    total = pl.pallas_call(
        reduce_body, name=name, out_shape=jax.ShapeDtypeStruct((1, width), F32),
        scratch_shapes=[pltpu.VMEM((N_DEV, 1, width), F32), pltpu.SemaphoreType.DMA((7,)), pltpu.SemaphoreType.DMA((7,))],
    )(*parts, loss_part)

    def adamw_body(*refs):
        tot_ref, w_refs, m_refs, v_refs, outs = refs[0], refs[1:k + 1], refs[k + 1:2 * k + 1], refs[2 * k + 1:3 * k + 1], refs[3 * k + 1:]
        for i in range(k):
            g = tot_ref[:, spots[i]:spots[i] + sizes[i]]
            outs[4 * i][...] = g
            outs[4 * i + 1][...], outs[4 * i + 2][...], outs[4 * i + 3][...] = _adamw_math(w_refs[i][...], g, m_refs[i][...], v_refs[i][...])
        outs[4 * k][...] = tot_ref[:, spots[k]:spots[k] + 1]

    out_shape = [jax.ShapeDtypeStruct((1, n), F32) for n in sizes[:k] for _ in range(4)] + [jax.ShapeDtypeStruct((1, 1), F32)]
    res = pl.pallas_call(adamw_body, name=name + "_adamw", out_shape=tuple(out_shape))(total, *ws, *ms, *vs)
    return [res[4 * i:4 * i + 4] for i in range(k)], res[4 * k]


def _pad_rows(a, rows):
    return jnp.pad(a, ((0, rows - a.shape[0]), (0, 0)))


def kernel(x, positions, norm_mix_g, w_in, b_fgate, q_norm_g, w_uq, kv_norm_g, w_ukv, fox_out_g, mla_out_g, w_o, norm_ffn_g, w_gate, w_up, w_down, final_norm_g, loss_target, m_norm_mix_g, m_w_in, m_b_fgate, m_q_norm_g, m_w_uq, m_kv_norm_g, m_w_ukv, m_fox_out_g, m_mla_out_g, m_w_o, m_norm_ffn_g, m_w_gate, m_w_up, m_w_down, m_final_norm_g, v_norm_mix_g, v_w_in, v_b_fgate, v_q_norm_g, v_w_uq, v_kv_norm_g, v_w_ukv, v_fox_out_g, v_mla_out_g, v_w_o, v_norm_ffn_g, v_w_gate, v_w_up, v_w_down, v_final_norm_g):
    bl, s, d = x.shape
    t = bl * s
    bh = bl * HEADS
    tq = _tile(s, 256)
    grp = s // LANES
    fw = HEADS * HEAD_DIM
    q_rank, kv_rank = w_uq.shape[1], w_ukv.shape[1]
    in_cols = w_in.shape[2]
    n_in = N_DEV * in_cols
    ff = N_DEV * w_gate.shape[2]
    half = MLA_ROPE // 2
    o_kvlat, o_krope, o_flogit = q_rank, q_rank + kv_rank, q_rank + kv_rank + LANES
    b_cols = -(-(o_flogit + HEADS) // LANES) * LANES

    tr = lambda w: jnp.transpose(w[0])
    in_rows = -(-in_cols // 16) * 16
    uq_rows = w_uq.shape[2] * q_rank // d
    ukv_rows = w_ukv.shape[2] * kv_rank // d
    pieces = [_pad_rows(tr(w_in), in_rows), _pad_rows(tr(w_uq).reshape(uq_rows, d), -(-uq_rows // 16) * 16),
              tr(w_ukv).reshape(ukv_rows, d), w_o[0], tr(w_gate), tr(w_up), w_down[0]]
    pieces = [p.astype(BF16) for p in pieces]
    offs = [0]
    for p in pieces:
        offs.append(offs[-1] + p.shape[0])
    legs = [(0, 1), (1, 5), (5, 7)]
    gathered = {}

    def full(i, rows):
        leg = next(n for n, (lo, hi) in enumerate(legs) if lo <= i < hi)
        base = offs[legs[leg][0]]
        return gathered[leg][:, offs[i] - base:offs[i] - base + rows]

    x2d = x.reshape(t, d)
    h1, gathered[0] = _rmsnorm(x2d, 0, d, norm_mix_g, BF16, "norm_mix", traffic=_Traffic("spread", pieces[0:1]))

    w_in_t = full(0, in_cols).reshape(n_in, d)
    n_qkv = 3 * fw
    w_in_a = w_in_t[:n_qkv].reshape(3, PAIRS, LANES, d).transpose(1, 0, 2, 3).reshape(n_qkv, d)
    lat0, rope0 = n_qkv + HEADS, n_qkv + HEADS + q_rank + kv_rank
    k_rep = jnp.broadcast_to(w_in_t[rope0:].reshape(2, 1, half, d), (2, 4, half, d)).reshape(LANES, d)
    w_in_b = jnp.concatenate([w_in_t[lat0:rope0], k_rep, w_in_t[n_qkv:lat0],
                              jnp.zeros((b_cols - o_flogit - HEADS, d), BF16)], axis=0)

    def per_head_rows(a):
        return a.reshape(bl, s, HEADS).transpose(0, 2, 1).reshape(bh, 1, s)

    proj_a = _matmul(h1, w_in_a, "nt", BF16, "proj_fox", tm=1024, tn=6 * LANES)
    proj_b = _matmul(h1, w_in_b, "nt", F32, "proj_mla", tm=1024, tn=b_cols)

    z = proj_b[:, o_flogit:o_flogit + HEADS].reshape(bl, s, HEADS).transpose(0, 2, 1).reshape(bh * grp, LANES)
    bcol = jnp.broadcast_to(b_fgate.reshape(1, HEADS, 1), (bl, HEADS, grp)).reshape(bh * grp, 1)
    c = _fgate(z, bcol, grp, "forget_gate")
    c_bias = c.reshape(bh, 1, s)
    fox_o, fox_lse, gathered[1] = _attn_fwd((proj_a,), c_bias, HEAD_DIM ** -0.5, bl, s, tq, "fox_attention",
                                            traffic=_Traffic("spread", pieces[legs[1][0]:legs[1][1]]))
    w_uq_h = full(1, uq_rows).reshape(HEADS, MLA_QK, q_rank)
    w_uq_pe = jnp.concatenate([w_uq_h[:, HEAD_DIM:HEAD_DIM + half].reshape(2, 1, 4 * half, q_rank),
                               w_uq_h[:, HEAD_DIM + half:].reshape(2, 1, 4 * half, q_rank)], axis=1).reshape(2 * LANES, q_rank)
    w_uq_p = jnp.concatenate([w_uq_h[:, :HEAD_DIM].reshape(fw, q_rank), w_uq_pe], axis=0)
    w_ukv_p = full(2, ukv_rows).reshape(PAIRS, 2, 2, HEAD_DIM, kv_rank).transpose(0, 2, 1, 3, 4).reshape(2 * fw, kv_rank)
    w_o_f = full(3, w_o.shape[1]).reshape(-1, d)
    w_gate_t = full(4, ff // N_DEV).reshape(ff, d)

    inv_freq = ROPE_THETA ** (-jnp.arange(0, MLA_ROPE, 2, dtype=F32) / MLA_ROPE)
    ang = positions.astype(F32).reshape(t, 1) * inv_freq[None, :]
    cos4, sin4 = jnp.tile(jnp.cos(ang), (1, 4)), jnp.tile(jnp.sin(ang), (1, 4))
    rope_cos, rope_sin = jnp.concatenate([cos4, cos4], axis=1), jnp.concatenate([-sin4, sin4], axis=1)
    qn, kvn, q_all, kv_all, kpe = _mla_prep(proj_b, q_rank, kv_rank, q_norm_g, kv_norm_g, w_uq_p, w_ukv_p, fw,
                                            rope_cos, rope_sin, "mla_prep")
    mla_ops = (q_all, kv_all, kpe)
    mla_o, mla_lse, gathered[2] = _attn_fwd(mla_ops, None, MLA_QK ** -0.5, bl, s, tq, "mla_attention",
                                            traffic=_Traffic("spread", pieces[legs[2][0]:legs[2][1]]))
    w_up_t, w_down_f = full(5, ff // N_DEV).reshape(ff, d), full(6, ff // N_DEV).reshape(ff, d)

    both = [(d, F32), (d, BF16)]
    cat, x1, h2 = _rows_matmul([(None, w_o_f, "nn")], [fox_o, mla_o, x2d], [fox_out_g, mla_out_g, norm_ffn_g], _residual_norm,
                               [(2 * fw, BF16)] + both, [], "norm_out_proj_out_norm_ffn", prologue=_out_norm)
    act_by_gate, act_by_up, act = _ffn_up(h2, w_gate_t, w_up_t, "ffn_gate_up")
    dx2, dx2_b, dg_final, loss_part = _rows_matmul(
        [(act, w_down_f, "nn")], [x1, loss_target.reshape(t, d)], [final_norm_g.reshape(1, d)], _residual_loss_bwd,
        both, [d, 1], "ffn_down_final_norm_loss")

    d_gate, d_up = _ffn_down_bwd(dx2_b, w_down_f, act_by_gate, act_by_up, "d_ffn_down")
    dw_down = _matmul(act, dx2_b, "tn", BF16, "dw_down", tm=ff // 2, tn=d, tk=2048)
    dw_gate = _matmul(d_gate, h2, "tn", BF16, "dw_gate", tm=ff // 2, tn=d, tk=2048)
    dw_up = _matmul(d_up, h2, "tn", BF16, "dw_up", tm=ff // 2, tn=d, tk=2048)
    dx1, dx1_b, dg_ffn = _rows_matmul([(d_gate, w_gate_t, "nn"), (d_up, w_up_t, "nn")], [x1, dx2], [norm_ffn_g],
                                      _norm_bwd_residual, both, [d], "d_ffn_gate_up_norm_ffn", tm=256)
    dw_o = _matmul(cat, dx1_b, "tn", BF16, "dw_o", tn=d, tk=2048)
    d_fox_o, d_mla_o, fox_delta, mla_delta, dg_fox, dg_mla = _rows_matmul(
        [(dx1_b, w_o_f, "nt")], [fox_o, mla_o], [fox_out_g, mla_out_g], _out_norm_bwd,
        [(fw, BF16), (fw, BF16), (HEADS, F32), (HEADS, F32)], [fw, fw], "d_proj_out_norm_out")

    per_dev = lambda a: a.reshape(N_DEV, -1, d)
    late_grads = [per_dev(dw_o), per_dev(dw_gate), per_dev(dw_up), per_dev(dw_down)]
    dproj_a, dc, g_late = _attn_bwd((proj_a,), d_fox_o, fox_lse, per_head_rows(fox_delta),
                                    c_bias, HEAD_DIM ** -0.5, bl, s, tq, "d_fox_attention", traffic=_Traffic("swap", late_grads))
    dz, db_fgate = _fgate_bwd(z, bcol, dc.reshape(bh * grp, LANES), grp, "d_forget_gate")
    d_flogit = dz.reshape(bl, HEADS, s).transpose(0, 2, 1).reshape(t, HEADS)

    dq_nope, dkv_all, dq_pe, dk_pe = _attn_bwd(mla_ops, d_mla_o, mla_lse, per_head_rows(mla_delta),
                                               None, MLA_QK ** -0.5, bl, s, tq, "d_mla_attention")
    d_tail = jnp.pad(d_flogit, ((0, 0), (0, b_cols - o_flogit - HEADS)))
    dproj_b, dq_rot, dg_q, dg_kv = _mla_prep_bwd(dq_nope, dq_pe, dkv_all, dk_pe, d_tail, proj_b, q_rank, kv_rank,
                                                 q_norm_g, kv_norm_g, w_uq_p, w_ukv_p, rope_cos, rope_sin, "d_mla_prep")
    dw_uq_nope = _matmul(dq_nope, qn, "tn", BF16, "dw_uq_nope", tn=q_rank, tk=1024)
    dw_uq_pe = _matmul(dq_rot, qn, "tn", BF16, "dw_uq_rope", tn=q_rank, tk=1024)
    dw_ukv_p = _matmul(dkv_all, kvn, "tn", BF16, "dw_ukv", tn=kv_rank, tk=1024)
    dw_in_a = _matmul(dproj_a, h1, "tn", BF16, "dw_in_fox", tm=6 * LANES, tn=d, tk=2048)
    dw_in_b = _matmul(dproj_b, h1, "tn", F32, "dw_in_mla", tm=b_cols, tn=d, tk=1024)

    dw_krope = dw_in_b[o_krope:o_flogit].reshape(2, 4, half, d).sum(axis=1).reshape(MLA_ROPE, d)
    dw_in_t = jnp.concatenate([dw_in_a.reshape(PAIRS, 3, LANES, d).transpose(1, 0, 2, 3).reshape(n_qkv, d),
                               dw_in_b[o_flogit:o_flogit + HEADS].astype(BF16), dw_in_b[:o_krope].astype(BF16),
                               dw_krope.astype(BF16)], axis=0)
    pad_dev = lambda a, rows: jnp.pad(a, ((0, 0), (0, rows - a.shape[1]), (0, 0)))
    dw_uq_pe5 = dw_uq_pe.reshape(2, 2, 4, half, q_rank)
    dw_uq_h = jnp.concatenate([dw_uq_nope.reshape(HEADS, HEAD_DIM, q_rank), dw_uq_pe5[:, 0].reshape(HEADS, half, q_rank),
                               dw_uq_pe5[:, 1].reshape(HEADS, half, q_rank)], axis=1)
    dw_ukv_h = dw_ukv_p.reshape(PAIRS, 2, 2, HEAD_DIM, kv_rank).transpose(0, 2, 1, 3, 4).reshape(HEADS, 2 * HEAD_DIM, kv_rank)
    n_last = 3
    last_grads = [pad_dev(per_dev(dw_in_t), pieces[0].shape[0]), pad_dev(per_dev(dw_uq_h), pieces[1].shape[0]), per_dev(dw_ukv_h)]
    dh1_fox, g_last = _matmul(dproj_a, w_in_a, "nn", F32, "d_proj_fox", tn=d, traffic=_Traffic("swap", last_grads))
    grad_x, dg_mix = _rows_matmul([(dproj_b, w_in_b, "nn")], [x2d, dx1, dh1_fox], [norm_mix_g], _norm_bwd_residual,
                                  [(d, F32)], [d], "d_proj_mla_norm_mix")
    g_last = _sum_blocks(g_last, "sum_last_grads")
    g_late = _sum_blocks(g_late, "sum_late_grads")

    def mine(i, rows):
        src, base = (g_last, 0) if i < n_last else (g_late, offs[n_last])
        return src[offs[i] - base:offs[i] - base + rows]

    big = [
        ("w_in", w_in, m_w_in, v_w_in, mine(0, in_cols), True),
        ("w_uq", w_uq, m_w_uq, v_w_uq, mine(1, uq_rows).reshape(-1, q_rank), True),
        ("w_ukv", w_ukv, m_w_ukv, v_w_ukv, mine(2, ukv_rows).reshape(-1, kv_rank), True),
        ("w_o", w_o, m_w_o, v_w_o, mine(3, w_o.shape[1]), False),
        ("w_gate", w_gate, m_w_gate, v_w_gate, mine(4, ff // N_DEV), True),
        ("w_up", w_up, m_w_up, v_w_up, mine(5, ff // N_DEV), True),
        ("w_down", w_down, m_w_down, v_w_down, mine(6, ff // N_DEV), False),
    ]
    out = {}
    for nm, w, m, v, g, transposed in big:
        lay = (lambda a: a[0].T) if transposed else (lambda a: a[0])
        back = (lambda a: a.T[None]) if transposed else (lambda a: a[None])
        dl, new_m, new_v = _adamw(lay(w), g, lay(m), lay(v), "adamw_" + nm)
        out[nm] = (back(g), back(dl), back(new_m), back(new_v))

    smalls = [("norm_mix_g", norm_mix_g, m_norm_mix_g, v_norm_mix_g, dg_mix),
              ("b_fgate", b_fgate, m_b_fgate, v_b_fgate, db_fgate.reshape(1, HEADS)),
              ("q_norm_g", q_norm_g, m_q_norm_g, v_q_norm_g, dg_q),
              ("kv_norm_g", kv_norm_g, m_kv_norm_g, v_kv_norm_g, dg_kv),
              ("fox_out_g", fox_out_g, m_fox_out_g, v_fox_out_g, dg_fox),
              ("mla_out_g", mla_out_g, m_mla_out_g, v_mla_out_g, dg_mla),
              ("norm_ffn_g", norm_ffn_g, m_norm_ffn_g, v_norm_ffn_g, dg_ffn),
              ("final_norm_g", final_norm_g, m_final_norm_g, v_final_norm_g, dg_final)]
    flat = lambda a: a.reshape(1, -1)
    results, loss = _small_all_reduce_adamw([e[4] for e in smalls], loss_part, [flat(e[1]) for e in smalls],
                                            [flat(e[2]) for e in smalls], [flat(e[3]) for e in smalls], "reduce_small_adamw")
    for (nm, w, _, _, _), res in zip(smalls, results):
        out[nm] = tuple(a.reshape(w.shape) for a in res)
    loss = loss[0, 0]

    order = ["norm_mix_g", "w_in", "b_fgate", "q_norm_g", "w_uq", "kv_norm_g", "w_ukv", "fox_out_g", "mla_out_g", "w_o",
             "norm_ffn_g", "w_gate", "w_up", "w_down", "final_norm_g"]
    return (loss, grad_x.reshape(bl, s, d), *[out[n][0] for n in order], *[out[n][1] for n in order],
            *[out[n][2] for n in order], *[out[n][3] for n in order])
```

```python
import math

import jax
import jax.numpy as jnp
from jax import lax
from jax.experimental import pallas as pl
from jax.experimental.pallas import tpu as pltpu

F32 = jnp.float32
BF16 = jnp.bfloat16
MESH = pl.DeviceIdType.MESH

N_DEV = 8
HEADS = 8
HEAD_DIM = 64
PAIRS = HEADS // 2
MLA_ROPE = 32
MLA_QK = HEAD_DIM + MLA_ROPE
ROPE_THETA = 10000.0
NORM_EPS = 1e-6
ADAM_LR, ADAM_B1, ADAM_B2, ADAM_EPS, ADAM_WD, ADAM_STEP = 0.001, 0.9, 0.999, 1e-08, 0.01, 10

LANES = 128
MASKED = -1e30
VMEM_LIMIT = 48 * 1024 * 1024

_DIMS = {"nn": (((1,), (0,)), ((), ())), "nt": (((1,), (1,)), ((), ())), "tn": (((0,), (0,)), ((), ()))}


def _params(*sem):
    return pltpu.CompilerParams(dimension_semantics=sem, vmem_limit_bytes=VMEM_LIMIT)


def _dot(a, b, mode):
    return lax.dot_general(a.astype(BF16), b.astype(BF16), _DIMS[mode], preferred_element_type=F32)


def _tile(n, pref, unit=8):
    if n <= pref:
        return n
    t = pref - pref % unit
    while n % t:
        t -= unit
    return t


def _log2(n):
    assert n & (n - 1) == 0
    return n.bit_length() - 1


def _matmul(a, b, mode, out_dtype, name, tm=512, tn=512, tk=None, res=None, traffic=None):
    if mode == "nn":
        (m, kd), n = a.shape, b.shape[1]
    elif mode == "nt":
        (m, kd), n = a.shape, b.shape[0]
    else:
        (kd, m), n = a.shape, b.shape[1]
    tm, tn = _tile(m, tm, LANES if mode == "tn" else 16), _tile(n, tn, LANES)
    tk = kd if tk is None else _tile(kd, tk, LANES)
    nk = kd // tk
    a_spec = pl.BlockSpec((tk, tm), lambda i, j, k: (k, i)) if mode == "tn" else pl.BlockSpec((tm, tk), lambda i, j, k: (i, k))
    b_spec = pl.BlockSpec((tn, tk), lambda i, j, k: (j, k)) if mode == "nt" else pl.BlockSpec((tk, tn), lambda i, j, k: (k, j))
    o_spec = pl.BlockSpec((tm, tn), lambda i, j, k: (i, j))
    has_res = res is not None
    n_carried = len(traffic.pieces) if traffic else 0
    grid = (m // tm, n // tn, nk)

    def body(*refs):
        a_ref, b_ref = refs[:2]
        r_ref = refs[2] if has_res else None
        n_in = 2 + has_res + n_carried
        o_ref = refs[n_in]
        step = [pl.program_id(axis) for axis in range(3)]
        if traffic:
            carried_in, carried_out, sems = refs[2 + has_res:n_in], refs[n_in + 1], refs[len(refs) - 3:]

            @pl.when((step[0] == 0) & (step[1] == 0) & (step[2] == 0))
            def _():
                traffic.start(carried_in, carried_out, *sems)

        def finish(acc):
            if has_res:
                acc = acc + r_ref[...]
            o_ref[...] = acc.astype(out_dtype)

        part = _dot(a_ref[...], b_ref[...], mode)
        if nk == 1:
            finish(part)
        else:
            acc_ref = refs[n_in + 1 + bool(traffic)]

            @pl.when(step[2] == 0)
            def _():
                acc_ref[...] = part

            @pl.when(step[2] > 0)
            def _():
                acc_ref[...] += part

            @pl.when(step[2] == nk - 1)
            def _():
                finish(acc_ref[...])

        if traffic:
            @pl.when((step[0] == grid[0] - 1) & (step[1] == grid[1] - 1) & (step[2] == nk - 1))
            def _():
                traffic.wait(carried_out, *sems)

    in_specs = [a_spec, b_spec] + ([o_spec] if has_res else [])
    out_specs, out_shape = [o_spec], [jax.ShapeDtypeStruct((m, n), out_dtype)]
    scratch = [pltpu.VMEM((tm, tn), F32)] if nk > 1 else []
    if traffic:
        in_specs += traffic.in_specs
        out_specs.append(traffic.out_spec)
        out_shape.append(traffic.out_shape)
        scratch += traffic.scratch
    out = pl.pallas_call(
        body, name=name, grid=grid, in_specs=in_specs, out_specs=tuple(out_specs), out_shape=tuple(out_shape),
        scratch_shapes=scratch,
        compiler_params=_params(*(("arbitrary",) * 3 if traffic else ("parallel", "parallel", "arbitrary"))),
    )(*([a, b] + ([res] if has_res else []) + (traffic.pieces if traffic else [])))
    return out if traffic else out[0]


def _rstd(x):
    return lax.rsqrt(jnp.mean(x * x, axis=-1, keepdims=True) + NORM_EPS)


def _norm_bwd(x, g, dy):
    r = _rstd(x)
    xh = x * r
    u = dy * g
    dx = r * (u - xh * jnp.mean(u * xh, axis=-1, keepdims=True))
    return dx, jnp.sum(dy * xh, axis=0, keepdims=True)


def _rmsnorm(x, col, width, g, out_dtype, name, traffic=None):
    t = x.shape[0]
    tm = _tile(t, 512)
    steps = t // tm
    n_carried = len(traffic.pieces) if traffic else 0

    def body(*refs):
        x_ref, g_ref, o_ref = refs[0], refs[1], refs[2 + n_carried]
        if traffic:
            carried_in, carried_out, sems = refs[2:2 + n_carried], refs[3 + n_carried], refs[4 + n_carried:]

            @pl.when(pl.program_id(0) == 0)
            def _():
                traffic.start(carried_in, carried_out, *sems)

            if isinstance(traffic, _Relay):
                @pl.when(pl.program_id(0) == max(steps - 2, 0))
                def _():
                    traffic.relay(carried_out, *sems)

        xv = x_ref[...]
        o_ref[...] = ((xv * _rstd(xv)) * g_ref[...]).astype(out_dtype)
        if traffic:
            @pl.when(pl.program_id(0) == steps - 1)
            def _():
                traffic.wait(carried_out, *sems)

    in_specs = [pl.BlockSpec((tm, width), lambda i: (i, col)), pl.BlockSpec((1, width), lambda i: (0, 0))]
    out_specs = [pl.BlockSpec((tm, width), lambda i: (i, 0))]
    out_shape = [jax.ShapeDtypeStruct((t, width), out_dtype)]
    if traffic:
        in_specs += traffic.in_specs
        out_specs.append(traffic.out_spec)
        out_shape.append(traffic.out_shape)
    out = pl.pallas_call(
        body, name=name, grid=(steps,), in_specs=in_specs, out_specs=tuple(out_specs), out_shape=tuple(out_shape),
        scratch_shapes=traffic.scratch if traffic else [],
        compiler_params=_params("arbitrary" if traffic else "parallel"),
    )(x, g, *(traffic.pieces if traffic else []))
    return out if traffic else out[0]


def _split3(x):
    hi = x.astype(BF16)
    r1 = x - hi.astype(F32)
    mid = r1.astype(BF16)
    lo = (r1 - mid.astype(F32)).astype(BF16)
    return hi, mid, lo


def _dot_x01(x, m01):
    hi, mid, lo = _split3(x)
    d = lambda p: lax.dot_general(p, m01, _DIMS["nn"], preferred_element_type=F32)
    return (d(lo) + d(mid)) + d(hi)


def _dot_01x(m01, x):
    hi, mid, lo = _split3(x)
    d = lambda p: lax.dot_general(m01, p, _DIMS["nn"], preferred_element_type=F32)
    return (d(lo) + d(mid)) + d(hi)


def _rows_matmul(terms, rows_in, vecs_in, epilogue, rows_out, sums_out, name, tm=512, prologue=None):
    t = rows_in[0].shape[0]
    tm = _tile(t, tm, 16)
    n_rows, n_vecs = len(rows_in), len(vecs_in)
    n_ab = sum(1 + (a is not None) for a, _, _ in terms)

    halves = [slice(0, tm // 2), slice(tm // 2, tm)] if tm % 32 == 0 else [slice(0, tm)]

    def body(*refs):
        vecs = [r[...] for r in refs[n_ab + n_rows:n_ab + n_rows + n_vecs]]
        out_at = n_ab + n_rows + n_vecs
        sum_refs = refs[out_at + len(rows_out):]

        @pl.when(pl.program_id(0) == 0)
        def _():
            for ref in sum_refs:
                ref[...] = jnp.zeros_like(ref)

        staged = []
        for rows_of in halves:
            row_blocks = [r[rows_of, :] for r in refs[n_ab:n_ab + n_rows]]
            made = prologue(row_blocks, vecs) if prologue else None
            acc, at = None, 0
            for a, _, mode in terms:
                lhs = made if a is None else refs[at][rows_of, :]
                at += a is not None
                part = _dot(lhs, refs[at][...], mode)
                at += 1
                acc = part if acc is None else acc + part
            staged.append((rows_of, row_blocks, made, acc))
        for rows_of, row_blocks, made, acc in staged:
            row_vals, sum_vals = epilogue(acc, row_blocks, vecs)
            if prologue:
                row_vals = [made] + row_vals
            for ref, val, (_, dtype) in zip(refs[out_at:], row_vals, rows_out):
                ref[rows_of, :] = val.astype(dtype)
            for ref, val in zip(sum_refs, sum_vals):
                ref[...] += val

    rows = lambda w: pl.BlockSpec((tm, w), lambda i: (i, 0))
    whole = lambda a: pl.BlockSpec(a.shape, lambda i: (0, 0))
    in_specs, args = [], []
    for a, b, _ in terms:
        in_specs += ([rows(a.shape[1])] if a is not None else []) + [whole(b)]
        args += ([a] if a is not None else []) + [b]
    in_specs += [rows(r.shape[1]) for r in rows_in] + [whole(v) for v in vecs_in]
    args += list(rows_in) + list(vecs_in)
    return pl.pallas_call(
        body, name=name, grid=(t // tm,), in_specs=in_specs,
        out_specs=tuple([rows(w) for w, _ in rows_out] + [pl.BlockSpec((1, w), lambda i: (0, 0)) for w in sums_out]),
        out_shape=tuple([jax.ShapeDtypeStruct((t, w), dt) for w, dt in rows_out] + [jax.ShapeDtypeStruct((1, w), F32) for w in sums_out]),
        compiler_params=_params("arbitrary"),
    )(*args)


def _out_norm(rows, vecs):
    (f, m), (gf, gm) = rows[:2], vecs[:2]
    return jnp.concatenate([((f * _rstd(f)) * gf).astype(BF16), ((m * _rstd(m)) * gm).astype(BF16)], axis=1)


def _residual_norm(acc, rows, vecs):
    x1 = rows[-1] + acc
    return [x1, (x1 * _rstd(x1)) * vecs[-1]], []


def _residual_loss_bwd(acc, rows, vecs):
    x2, gv = rows[0] + acc, vecs[0]
    diff = (x2 * _rstd(x2)) * gv - rows[1]
    dx, dg = _norm_bwd(x2, gv, diff / x2.shape[1])
    return [dx, dx], [dg, 0.5 * jnp.sum(jnp.mean(diff * diff, axis=-1, keepdims=True), axis=0, keepdims=True)]


def _norm_bwd_residual(acc, rows, vecs):
    dy = acc + rows[2] if len(rows) > 2 else acc
    dx, dg = _norm_bwd(rows[0], vecs[0], dy)
    if len(rows) > 1:
        dx = dx + rows[1]
    return [dx, dx], [dg]


def _out_norm_bwd(acc, rows, vecs):
    (f, m), w = rows, rows[0].shape[1]
    nh = w // HEAD_DIM
    lane_head = lax.shift_right_logical(lax.broadcasted_iota(jnp.int32, (w, nh), 0), _log2(HEAD_DIM))
    sel = (lane_head == lax.broadcasted_iota(jnp.int32, (w, nh), 1)).astype(BF16)
    dfo, dgf = _norm_bwd(f, vecs[0], acc[:, :w])
    dmo, dgm = _norm_bwd(m, vecs[1], acc[:, w:])
    return [dfo, dmo, _dot_x01(dfo * f, sel), _dot_x01(dmo * m, sel)], [dgf, dgm]


def _ffn_up(h, wg_t, wu_t, name, tm=512, tf=1408):
    t, d = h.shape
    f = wg_t.shape[0]
    tm, tf = _tile(t, tm, 16), _tile(f, tf, LANES)
    tok = pl.BlockSpec((tm, tf), lambda j, i: (i, j))
    wt = pl.BlockSpec((tf, d), lambda j, i: (j, 0))

    def body(h_ref, wg_ref, wu_ref, dg_ref, du_ref, a_ref):
        hv = h_ref[...]
        g, u = _dot(hv, wg_ref[...], "nt"), _dot(hv, wu_ref[...], "nt")
        sg = jax.nn.sigmoid(g)
        silu = g * sg
        dg_ref[...] = (u * (sg * (1.0 + g * (1.0 - sg)))).astype(BF16)
        du_ref[...] = silu.astype(BF16)
        a_ref[...] = (silu * u).astype(BF16)

    return pl.pallas_call(
        body, name=name, grid=(f // tf, t // tm), in_specs=[pl.BlockSpec((tm, d), lambda j, i: (i, 0)), wt, wt],
        out_specs=(tok, tok, tok),
        out_shape=(jax.ShapeDtypeStruct((t, f), BF16), jax.ShapeDtypeStruct((t, f), BF16), jax.ShapeDtypeStruct((t, f), BF16)),
        compiler_params=_params("parallel", "parallel"),
    )(h, wg_t, wu_t)


def _ffn_down_bwd(dy, w_down, act_by_gate, act_by_up, name, tm=512, tf=1408):
    t, d = dy.shape
    f = w_down.shape[0]
    tm, tf = _tile(t, tm, 16), _tile(f, tf, LANES)
    tok = pl.BlockSpec((tm, tf), lambda j, i: (i, j))

    def body(dy_ref, w_ref, g_ref, u_ref, dg_ref, du_ref):
        da = _dot(dy_ref[...], w_ref[...], "nt")
        dg_ref[...] = (da * g_ref[...].astype(F32)).astype(BF16)
        du_ref[...] = (da * u_ref[...].astype(F32)).astype(BF16)

    return pl.pallas_call(
        body, name=name, grid=(f // tf, t // tm),
        in_specs=[pl.BlockSpec((tm, d), lambda j, i: (i, 0)), pl.BlockSpec((tf, d), lambda j, i: (j, 0)), tok, tok],
        out_specs=(tok, tok),
        out_shape=(jax.ShapeDtypeStruct((t, f), BF16), jax.ShapeDtypeStruct((t, f), BF16)),
        compiler_params=_params("parallel", "parallel"),
    )(dy, w_down, act_by_gate, act_by_up)


def _chunk_scan_mats(rows, grp, reverse):
    ii = lax.broadcasted_iota(jnp.int32, (LANES, LANES), 0)
    jj = lax.broadcasted_iota(jnp.int32, (LANES, LANES), 1)
    within = ((ii >= jj) if reverse else (ii <= jj)).astype(BF16)
    ones = jnp.ones((LANES, LANES), BF16)
    ri = lax.broadcasted_iota(jnp.int32, (rows, rows), 0)
    rj = lax.broadcasted_iota(jnp.int32, (rows, rows), 1)
    sh = _log2(grp)
    same = lax.shift_right_logical(ri, sh) == lax.shift_right_logical(rj, sh)
    across = (same & ((rj > ri) if reverse else (rj < ri))).astype(BF16)
    return within, ones, across


def _running_sum(v, mats):
    within, ones, across = mats
    return _dot_x01(v, within) + _dot_01x(across, _dot_x01(v, ones))


def _fgate(z, bcol, grp, name):
    rows = z.shape[0]

    def body(z_ref, b_ref, c_ref):
        zz = z_ref[...] + b_ref[...]
        log_f = jnp.minimum(zz, 0.0) - jnp.log1p(jnp.exp(-jnp.abs(zz)))
        c_ref[...] = _running_sum(log_f, _chunk_scan_mats(rows, grp, False))

    return pl.pallas_call(body, name=name, out_shape=jax.ShapeDtypeStruct(z.shape, F32),
                          compiler_params=pltpu.CompilerParams(vmem_limit_bytes=VMEM_LIMIT))(z, bcol)


def _fgate_bwd(z, bcol, dc, grp, name):
    rows = z.shape[0]

    def body(z_ref, b_ref, dc_ref, dz_ref, db_ref):
        zz = z_ref[...] + b_ref[...]
        dz = _running_sum(dc_ref[...], _chunk_scan_mats(rows, grp, True)) * jax.nn.sigmoid(-zz)
        dz_ref[...] = dz
        head = lax.shift_right_logical(lax.broadcasted_iota(jnp.int32, (HEADS, rows), 1), _log2(grp)) & (HEADS - 1)
        sel = (head == lax.broadcasted_iota(jnp.int32, (HEADS, rows), 0)).astype(BF16)
        db_ref[...] = jnp.sum(_dot_01x(sel, dz), axis=1, keepdims=True)

    return pl.pallas_call(
        body, name=name,
        out_shape=(jax.ShapeDtypeStruct(z.shape, F32), jax.ShapeDtypeStruct((HEADS, 1), F32)),
        compiler_params=pltpu.CompilerParams(vmem_limit_bytes=VMEM_LIMIT),
    )(z, bcol, dc)


def _rotate(x, cs, sn_signed):
    return x * cs + pltpu.roll(x, LANES // 2, axis=1) * sn_signed


def _rope_tables(cos, sin):
    half = cos.shape[1]
    freq = lax.broadcasted_iota(jnp.int32, (half, LANES), 0)
    lane = lax.broadcasted_iota(jnp.int32, (half, LANES), 1)
    hit = (lane & (half - 1)) == freq
    sign = jnp.where(lane < LANES // 2, -1.0, 1.0)
    return _dot_x01(cos, hit.astype(BF16)), _dot_x01(sin, jnp.where(hit, sign, 0.0).astype(BF16))


def _mla_prep(proj_b, q_rank, kv_rank, gq, gkv, w_uq_p, w_ukv_p, nope, cs, sn, name):
    t, bw = proj_b.shape
    qw, kvw = w_uq_p.shape[0], w_ukv_p.shape[0]
    tm = _tile(t, 512)
    rows = lambda w: pl.BlockSpec((tm, w), lambda i: (i, 0))
    whole = lambda a: pl.BlockSpec(a.shape, lambda i: (0, 0))

    def body(pb_ref, gq_ref, gkv_ref, wq_ref, wkv_ref, c_ref, s_ref, qn_ref, kvn_ref, q_ref, kv_ref, kpe_ref):
        c, s = _rope_tables(c_ref[...], s_ref[...])
        ql, kvl = pb_ref[:, :q_rank], pb_ref[:, q_rank:q_rank + kv_rank]
        qn = ((ql * _rstd(ql)) * gq_ref[...]).astype(BF16)
        kvn = ((kvl * _rstd(kvl)) * gkv_ref[...]).astype(BF16)
        qn_ref[...], kvn_ref[...] = qn, kvn
        q_raw = _dot(qn, wq_ref[...], "nt")
        q_ref[:, :nope] = q_raw[:, :nope].astype(BF16)
        for off in range(nope, qw, LANES):
            q_ref[:, off:off + LANES] = _rotate(q_raw[:, off:off + LANES], c, s).astype(BF16)
        kv_ref[...] = _dot(kvn, wkv_ref[...], "nt").astype(BF16)
        kpe_ref[...] = _rotate(pb_ref[:, q_rank + kv_rank:q_rank + kv_rank + LANES], c, s).astype(BF16)

    return pl.pallas_call(
        body, name=name, grid=(t // tm,),
        in_specs=[rows(bw), whole(gq), whole(gkv), whole(w_uq_p), whole(w_ukv_p), rows(cs.shape[1]), rows(sn.shape[1])],
        out_specs=(rows(q_rank), rows(kv_rank), rows(qw), rows(kvw), rows(LANES)),
        out_shape=(jax.ShapeDtypeStruct((t, q_rank), BF16), jax.ShapeDtypeStruct((t, kv_rank), BF16),
                   jax.ShapeDtypeStruct((t, qw), BF16), jax.ShapeDtypeStruct((t, kvw), BF16), jax.ShapeDtypeStruct((t, LANES), BF16)),
        compiler_params=_params("parallel"),
    )(proj_b, gq, gkv, w_uq_p, w_ukv_p, cs, sn)


def _mla_prep_bwd(dq_nope, dq_pe, dkv_all, dk_pe, d_tail, proj_b, q_rank, kv_rank, gq, gkv, w_uq_p, w_ukv_p, cs, sn, name):
    t, bw = proj_b.shape
    nope, pw = dq_nope.shape[1], dq_pe.shape[1]
    tm = _tile(t, 512)
    rows = lambda w: pl.BlockSpec((tm, w), lambda i: (i, 0))
    whole = lambda a: pl.BlockSpec(a.shape, lambda i: (0, 0))
    o_k = q_rank + kv_rank

    def body(dqn_ref, dqp_ref, dkv_ref, dkp_ref, dt_ref, pb_ref, gq_ref, gkv_ref, wq_ref, wkv_ref, c_ref, s_ref,
             dpb_ref, dqr_ref, dgq_ref, dgkv_ref):
        c, s = _rope_tables(c_ref[...], -s_ref[...])
        for off in range(0, pw, LANES):
            dqr_ref[:, off:off + LANES] = _rotate(dqp_ref[:, off:off + LANES], c, s).astype(BF16)
        d_qn = _dot(dqn_ref[...], wq_ref[:nope, :], "nn") + _dot(dqr_ref[...], wq_ref[nope:, :], "nn")
        dq_lat, dgq = _norm_bwd(pb_ref[:, :q_rank], gq_ref[...], d_qn)
        dkv_lat, dgkv = _norm_bwd(pb_ref[:, q_rank:o_k], gkv_ref[...], _dot(dkv_ref[...], wkv_ref[...], "nn"))
        dpb_ref[:, :q_rank] = dq_lat.astype(BF16)
        dpb_ref[:, q_rank:o_k] = dkv_lat.astype(BF16)
        dpb_ref[:, o_k:o_k + LANES] = _rotate(dkp_ref[...], c, s).astype(BF16)
        dpb_ref[:, o_k + LANES:] = dt_ref[...].astype(BF16)

        @pl.when(pl.program_id(0) == 0)
        def _():
            dgq_ref[...] = jnp.zeros_like(dgq_ref)
            dgkv_ref[...] = jnp.zeros_like(dgkv_ref)

        dgq_ref[...] += dgq
        dgkv_ref[...] += dgkv

    return pl.pallas_call(
        body, name=name, grid=(t // tm,),
        in_specs=[rows(nope), rows(pw), rows(dkv_all.shape[1]), rows(LANES), rows(bw - o_k - LANES), rows(bw), whole(gq), whole(gkv),
                  whole(w_uq_p), whole(w_ukv_p), rows(cs.shape[1]), rows(sn.shape[1])],
        out_specs=(rows(bw), rows(pw), whole(gq), whole(gkv)),
        out_shape=(jax.ShapeDtypeStruct((t, bw), BF16), jax.ShapeDtypeStruct((t, pw), BF16),
                   jax.ShapeDtypeStruct(gq.shape, F32), jax.ShapeDtypeStruct(gkv.shape, F32)),
        compiler_params=_params("arbitrary"),
    )(dq_nope, dq_pe, dkv_all, dk_pe, d_tail, proj_b, gq, gkv, w_uq_p, w_ukv_p, cs, sn)


def _lane_masks(pair, h, pe):
    lane = lax.broadcasted_iota(jnp.int32, (1, LANES), 1)
    in_head = lax.shift_right_logical(lane, _log2(HEAD_DIM)) == h
    in_rope = ((lax.shift_right_logical(lane, _log2(MLA_ROPE // 2)) & 3) == ((2 * pair + h) & 3)) if pe else None
    return in_head, in_rope


def _keep(mask, v):
    return jnp.where(mask, v, jnp.zeros_like(v))


def _to_row(col):
    n = col.shape[0]
    eye = lax.broadcasted_iota(jnp.int32, (n, n), 0) == lax.broadcasted_iota(jnp.int32, (n, n), 1)
    return jnp.sum(jnp.where(eye, col, 0.0), axis=0, keepdims=True)


def _to_col(row):
    n = row.shape[1]
    eye = lax.broadcasted_iota(jnp.int32, (n, n), 0) == lax.broadcasted_iota(jnp.int32, (n, n), 1)
    return jnp.sum(jnp.where(eye, row, 0.0), axis=1, keepdims=True)


def _first_step():
    return (pl.program_id(0) == 0) & (pl.program_id(1) == 0)


def _last_step(n0, n1):
    return (pl.program_id(0) == n0 - 1) & (pl.program_id(1) == n1 - 1)


def _attn_fwd(ops, bias, scale, bl, s, tq, name, traffic=None):
    pe = len(ops) == 3
    has_bias = bias is not None
    exact_scale = math.frexp(scale)[0] == 0.5
    nq = s // tq
    t = bl * s
    n_carried = len(traffic.pieces) if traffic else 0

    def body(*refs):
        sems = refs[len(refs) - 3:] if traffic else ()
        if pe:
            q_ref, qpe_ref, kv_ref, kpe_ref = refs[:4]
            n_in = 4
            q_at = lambda r0, r1: q_ref[r0:r1, :]
            v_at = lambda r0, r1: kv_ref[r0:r1, LANES:]
            kcat = refs[len(refs) - 1 - len(sems)]
            kcat[:, :LANES] = kv_ref[:, :LANES]
            kcat[:, LANES:] = kpe_ref[...]
            k_at = lambda r0, r1: kcat[r0:r1, :]
        else:
            qkv_ref = refs[0]
            n_in = 1
            q_at = lambda r0, r1: qkv_ref[r0:r1, :LANES]
            k_at = lambda r0, r1: qkv_ref[r0:r1, LANES:2 * LANES]
            v_at = lambda r0, r1: qkv_ref[r0:r1, 2 * LANES:]
        if has_bias:
            c_ref = refs[n_in]
            n_in += 1
        carried_in = refs[n_in:n_in + n_carried]
        n_in += n_carried
        o_ref, lse_ref = refs[n_in:n_in + 2]
        if traffic:
            carried_out = refs[n_in + 2]

            @pl.when(_first_step())
            def _():
                traffic.start(carried_in, carried_out, *sems)

        pair = pl.program_id(1)
        causal = lax.broadcasted_iota(jnp.int32, (tq, tq), 1) <= lax.broadcasted_iota(jnp.int32, (tq, tq), 0)
        o_ref[...] = jnp.zeros_like(o_ref)

        masks = [_lane_masks(pair, h, pe) for h in range(2)]

        def logits(i):
            r0, r1 = i * tq, (i + 1) * tq
            out = []
            for h in range(2):
                in_head, in_rope = masks[h]
                qm = _keep(in_head, q_at(r0, r1))
                if pe:
                    qm = jnp.concatenate([qm, _keep(in_rope, qpe_ref[r0:r1, :])], axis=1)
                if exact_scale:
                    qm = qm * scale
                spans = []
                for k0, k1 in [(r0, r1)] + ([(0, r0)] if i else []):
                    sc = _dot(qm, k_at(k0, k1), "nt")
                    if not exact_scale:
                        sc = sc * scale
                    if has_bias:
                        sc = sc - c_ref[h, :, k0:k1]
                    spans.append((k0, k1, jnp.where(causal, sc, MASKED) if k0 == r0 else sc))
                out.append(spans)
            return out

        def softmax(per_head):
            out = []
            for spans in per_head:
                m = None
                for _, _, sc in spans:
                    top = jnp.max(sc, axis=1, keepdims=True)
                    m = top if m is None else jnp.maximum(m, top)
                probs = [(k0, k1, jnp.exp(sc - m)) for k0, k1, sc in spans]
                l = sum(jnp.sum(p, axis=1, keepdims=True) for _, _, p in probs)
                out.append((m, l, probs))
            return out

        def weigh(i, per_head):
            r0, r1 = i * tq, (i + 1) * tq
            for h, (m, l, probs) in enumerate(per_head):
                acc = sum(_dot(p, v_at(k0, k1), "nn") for k0, k1, p in probs)
                o_ref[r0:r1, :] = jnp.where(masks[h][0], acc / l, o_ref[r0:r1, :])
                lse = _to_row(m + jnp.log(l))
                lse_ref[h, :, r0:r1] = lse + c_ref[h, :, r0:r1] if has_bias else lse

        ahead = logits(0)
        for i in range(nq):
            solved = softmax(ahead)
            if i + 1 < nq:
                ahead = logits(i + 1)
            weigh(i, solved)

        if traffic:
            @pl.when(_last_step(bl, PAIRS))
            def _():
                traffic.wait(carried_out, *sems)

    seq = lambda w, col: pl.BlockSpec((s, w), col)
    if pe:
        in_specs = [seq(LANES, lambda b, p: (b, p)), seq(LANES, lambda b, p: (b, PAIRS + p // 2)),
                    seq(2 * LANES, lambda b, p: (b, p)), seq(LANES, lambda b, p: (b, 0))]
        args = [ops[0], ops[0], ops[1], ops[2]]
        scratch = [pltpu.VMEM((s, 2 * LANES), BF16)]
    else:
        in_specs = [seq(3 * LANES, lambda b, p: (b, p))]
        args = [ops[0]]
        scratch = []
    per_head_row = pl.BlockSpec((2, 1, s), lambda b, p: (b * PAIRS + p, 0, 0))
    if has_bias:
        in_specs.append(per_head_row)
        args.append(bias)
    out_specs = [seq(LANES, lambda b, p: (b, p)), per_head_row]
    out_shape = [jax.ShapeDtypeStruct((t, HEADS * HEAD_DIM), F32), jax.ShapeDtypeStruct((bl * HEADS, 1, s), F32)]
    if traffic:
        in_specs += traffic.in_specs
        args += traffic.pieces
        out_specs.append(traffic.out_spec)
        out_shape.append(traffic.out_shape)
        scratch += traffic.scratch
    return pl.pallas_call(
        body, name=name, grid=(bl, PAIRS), in_specs=in_specs, out_specs=tuple(out_specs), out_shape=tuple(out_shape),
        scratch_shapes=scratch, compiler_params=_params(*(("arbitrary", "arbitrary") if traffic else ("parallel", "parallel"))),
    )(*args)


def _attn_bwd(ops, do, lse, delta, bias, scale, bl, s, tq, name, traffic=None):
    pe = len(ops) == 3
    has_bias = bias is not None
    exact_scale = math.frexp(scale)[0] == 0.5
    nq = s // tq
    t = bl * s
    width = 2 * LANES if pe else LANES
    n_carried = len(traffic.pieces) if traffic else 0

    def body(*refs):
        if pe:
            q_ref, qpe_ref, kv_ref, kpe_ref = refs[:4]
            n_in = 4
            k_at = lambda r0, r1: kv_ref[r0:r1, :LANES]
            v_at = lambda r0, r1: kv_ref[r0:r1, LANES:]
        else:
            qkv_ref = refs[0]
            n_in = 1
            k_at = lambda r0, r1: qkv_ref[r0:r1, LANES:2 * LANES]
            v_at = lambda r0, r1: qkv_ref[r0:r1, 2 * LANES:]
        do_ref, lse_ref, dl_ref = refs[n_in:n_in + 3]
        n_in += 3
        if has_bias:
            c_ref = refs[n_in]
            n_in += 1
        carried_in = refs[n_in:n_in + n_carried]
        rest = refs[n_in + n_carried:]
        if traffic:
            rest, sems = rest[:-3], rest[-3:]
            carried_out = rest[4 if pe else 2]
            rest = rest[:4 if pe else 2] + rest[(4 if pe else 2) + 1:]

            @pl.when(_first_step())
            def _():
                traffic.start(carried_in, carried_out, *sems)

        if pe:
            dqn_ref, dkv_ref, dqpe_ref, dkpe_ref, dq_acc, qcat = rest
            qcat[:, :LANES] = q_ref[...]
            qcat[:, LANES:] = qpe_ref[...]
            q_at = lambda r0, r1: qcat[r0:r1, :]
            dkv_ref[...] = jnp.zeros_like(dkv_ref)
        else:
            dqkv_ref, dc_ref, dq_acc = rest
            q_at = lambda r0, r1: qkv_ref[r0:r1, :LANES]
            dqkv_ref[...] = jnp.zeros_like(dqkv_ref)
            dc_ref[...] = jnp.zeros_like(dc_ref)
        pair = pl.program_id(1)
        dq_acc[...] = jnp.zeros_like(dq_acc)
        causal = lax.broadcasted_iota(jnp.int32, (tq, tq), 1) >= lax.broadcasted_iota(jnp.int32, (tq, tq), 0)
        if pe:
            @pl.when(pair == 0)
            def _():
                dkpe_ref[...] = jnp.zeros_like(dkpe_ref)

            @pl.when(pair % 2 == 0)
            def _():
                dqpe_ref[...] = jnp.zeros_like(dqpe_ref)

        masks = [_lane_masks(pair, h, pe) for h in range(2)]

        def logits(j):
            r0, r1 = j * tq, (j + 1) * tq
            units = []
            for h in range(2):
                in_head, in_rope = masks[h]
                kt = _keep(in_head, k_at(r0, r1))
                if pe:
                    kt = jnp.concatenate([kt, _keep(in_rope, kpe_ref[r0:r1, :])], axis=1)
                if exact_scale:
                    kt = kt * scale
                vt = _keep(in_head, v_at(r0, r1))
                ck = _to_col(c_ref[h, :, r0:r1]) if has_bias else None
                for q0, q1, diagonal in [(r0, r1, True)] + ([(r1, s, False)] if r1 < s else []):
                    qq, dd = q_at(q0, q1), do_ref[q0:q1, :]
                    st = _dot(kt, qq, "nt")
                    if not exact_scale:
                        st = st * scale
                    shift = lse_ref[h, :, q0:q1]
                    if has_bias:
                        shift = shift - c_ref[h, :, q0:q1]
                        st = st - ck
                    st = st - shift
                    if diagonal:
                        st = jnp.where(causal, st, MASKED)
                    units.append((h, q0, q1, kt, qq, dd, st, _dot(vt, dd, "nt")))
            return units

        def softmax_bwd(units):
            solved = []
            for h, q0, q1, kt, qq, dd, st, dpt in units:
                pt = jnp.exp(st)
                dst = pt * (dpt - dl_ref[h, :, q0:q1])
                solved.append((h, q0, q1, kt, qq, dd, pt, dst, (dst if exact_scale else dst * scale).astype(BF16)))
            return solved

        def products(j, solved):
            r0, r1 = j * tq, (j + 1) * tq
            dv_of, dk_of, cs_of = [None, None], [None, None], [None, None]
            add = lambda old, new: new if old is None else old + new
            for h, q0, q1, kt, qq, dd, pt, dst, dsb in solved:
                dq_acc[q0:q1, :] += _dot(dsb, kt, "tn")
                dv_of[h] = add(dv_of[h], _dot(pt, dd, "nn"))
                dk_of[h] = add(dk_of[h], _dot(dsb, qq, "nn"))
                if has_bias:
                    dc_ref[h, :, q0:q1] += jnp.sum(dst, axis=0, keepdims=True)
                    cs_of[h] = add(cs_of[h], jnp.sum(dst, axis=1, keepdims=True))
            for h in range(2):
                (in_head, in_rope), dv_c, dk_c, cs = masks[h], dv_of[h], dk_of[h], cs_of[h]
                if exact_scale:
                    dk_c = dk_c * scale
                if pe:
                    dkv_ref[r0:r1, :LANES] = jnp.where(in_head, dk_c[:, :LANES].astype(BF16), dkv_ref[r0:r1, :LANES])
                    dkv_ref[r0:r1, LANES:] = jnp.where(in_head, dv_c.astype(BF16), dkv_ref[r0:r1, LANES:])
                    dkpe_ref[r0:r1, :] += _keep(in_rope, dk_c[:, LANES:])
                else:
                    dqkv_ref[r0:r1, LANES:2 * LANES] = jnp.where(in_head, dk_c.astype(BF16), dqkv_ref[r0:r1, LANES:2 * LANES])
                    dqkv_ref[r0:r1, 2 * LANES:] = jnp.where(in_head, dv_c.astype(BF16), dqkv_ref[r0:r1, 2 * LANES:])
                    dc_ref[h, :, r0:r1] -= _to_row(cs)

        units = logits(0)
        for j in range(nq):
            solved = softmax_bwd(units)
            if j + 1 < nq:
                units = logits(j + 1)
            products(j, solved)

        if pe:
            dqn_ref[...] = dq_acc[:, :LANES].astype(BF16)
            dqpe_ref[...] += dq_acc[:, LANES:]
        else:
            dqkv_ref[:, :LANES] = dq_acc[...].astype(BF16)
        if traffic:
            @pl.when(_last_step(bl, PAIRS))
            def _():
                traffic.wait(carried_out, *sems)

    seq = lambda w, col: pl.BlockSpec((s, w), col)
    per_head_row = pl.BlockSpec((2, 1, s), lambda b, p: (b * PAIRS + p, 0, 0))
    if pe:
        in_specs = [seq(LANES, lambda b, p: (b, p)), seq(LANES, lambda b, p: (b, PAIRS + p // 2)),
                    seq(2 * LANES, lambda b, p: (b, p)), seq(LANES, lambda b, p: (b, 0))]
        args = [ops[0], ops[0], ops[1], ops[2]]
    else:
        in_specs = [seq(3 * LANES, lambda b, p: (b, p))]
        args = [ops[0]]
    in_specs += [seq(LANES, lambda b, p: (b, p)), per_head_row, per_head_row]
    args += [do, lse, delta]
    if has_bias:
        in_specs.append(per_head_row)
        args.append(bias)
    scratch = [pltpu.VMEM((s, width), F32)]
    if pe:
        out_specs = (seq(LANES, lambda b, p: (b, p)), seq(2 * LANES, lambda b, p: (b, p)),
                     seq(LANES, lambda b, p: (b, p // 2)), seq(LANES, lambda b, p: (b, 0)))
        out_shape = (jax.ShapeDtypeStruct((t, PAIRS * LANES), BF16), jax.ShapeDtypeStruct((t, PAIRS * 2 * LANES), BF16),
                     jax.ShapeDtypeStruct((t, 2 * LANES), F32), jax.ShapeDtypeStruct((t, LANES), F32))
        scratch.append(pltpu.VMEM((s, 2 * LANES), BF16))
    else:
        out_specs = (seq(3 * LANES, lambda b, p: (b, p)), per_head_row)
        out_shape = (jax.ShapeDtypeStruct((t, PAIRS * 3 * LANES), BF16), jax.ShapeDtypeStruct((bl * HEADS, 1, s), F32))
    if traffic:
        in_specs += traffic.in_specs
        args += traffic.pieces
        out_specs += (traffic.out_spec,)
        out_shape += (traffic.out_shape,)
        scratch += traffic.scratch
    return pl.pallas_call(
        body, name=name, grid=(bl, PAIRS), in_specs=in_specs, out_specs=out_specs, out_shape=out_shape,
        scratch_shapes=scratch, compiler_params=_params("arbitrary" if traffic else "parallel", "arbitrary"),
    )(*args)


def _my_place():
    return lax.axis_index("x"), lax.axis_index("y"), lax.axis_index("c")


def _flip(p, bit):
    return 1 - p if bit else p


def _relative(x, y, c, k):
    return _flip(x, k & 4), _flip(y, k & 2), _flip(c, k & 1)


def _linear(x, y, c):
    return 4 * x + 2 * y + c


class _Traffic:
    def __init__(self, kind, pieces):
        self.kind, self.pieces = kind, list(pieces)
        self.rows = [p.shape[-2] for p in self.pieces]
        self.starts = [sum(self.rows[:i]) for i in range(len(self.rows))]
        anywhere = pl.BlockSpec(memory_space=pl.ANY)
        self.in_specs = [anywhere] * len(self.pieces)
        self.out_spec = anywhere
        self.out_shape = jax.ShapeDtypeStruct((N_DEV, sum(self.rows), self.pieces[0].shape[-1]), self.pieces[0].dtype)
        self.scratch = [pltpu.SemaphoreType.DMA((7,)), pltpu.SemaphoreType.DMA((7,)), pltpu.SemaphoreType.DMA(())]

    def start(self, p_refs, out_ref, send_sems, recv_sems, local_sem):
        x, y, c = _my_place()
        me = _linear(x, y, c)
        mine = lambda i, dev: p_refs[i] if self.kind == "spread" else p_refs[i].at[dev]
        landing = lambda i: out_ref.at[me, pl.ds(self.starts[i], self.rows[i])]
        for i in range(len(p_refs)):
            pltpu.make_async_copy(mine(i, me), landing(i), local_sem).start()
        for k in range(1, N_DEV):
            peer = _relative(x, y, c, k)
            for i in range(len(p_refs)):
                pltpu.make_async_remote_copy(
                    src_ref=mine(i, _linear(*peer)), dst_ref=landing(i),
                    send_sem=send_sems.at[k - 1], recv_sem=recv_sems.at[k - 1], device_id=peer, device_id_type=MESH).start()

    def wait(self, out_ref, send_sems, recv_sems, local_sem):
        x, y, c = _my_place()
        whole = out_ref.at[_linear(x, y, c)]
        for k in range(1, N_DEV):
            both = pltpu.make_async_remote_copy(
                src_ref=whole, dst_ref=whole, send_sem=send_sems.at[k - 1], recv_sem=recv_sems.at[k - 1],
                device_id=_relative(x, y, c, k), device_id_type=MESH)
            both.wait_recv()
            both.wait_send()
        pltpu.make_async_copy(whole, whole, local_sem).wait()


class _Relay(_Traffic):
    def __init__(self, piece):
        super().__init__("spread", [piece])

    @staticmethod
    def _chips(x, y):
        return [(1 - x, y), (x, 1 - y), (1 - x, 1 - y)]

    @staticmethod
    def _copy(k, block, to, out_ref, send_sems, recv_sems, src=None):
        slot = out_ref.at[_linear(*block)]
        return pltpu.make_async_remote_copy(src_ref=slot if src is None else src, dst_ref=slot, send_sem=send_sems.at[k],
                                            recv_sem=recv_sems.at[k], device_id=to, device_id_type=MESH)

    def start(self, p_refs, out_ref, send_sems, recv_sems, local_sem):
        x, y, c = _my_place()
        me, sems = (x, y, c), (out_ref, send_sems, recv_sems)
        pltpu.make_async_copy(p_refs[0], out_ref.at[_linear(*me)], local_sem).start()
        self._copy(0, me, (x, y, 1 - c), *sems, src=p_refs[0]).start()
        for j, chip in enumerate(self._chips(x, y)):
            self._copy(1 + j, me, (*chip, c), *sems, src=p_refs[0]).start()

    def relay(self, out_ref, send_sems, recv_sems, local_sem):
        x, y, c = _my_place()
        sems = (out_ref, send_sems, recv_sems)
        for j, chip in enumerate(self._chips(x, y)):
            self._copy(1 + j, (*chip, c), (x, y, c), *sems).wait_recv()
            self._copy(4 + j, (*chip, c), (x, y, 1 - c), *sems).start()

    def wait(self, out_ref, send_sems, recv_sems, local_sem):
        x, y, c = _my_place()
        me, sems = (x, y, c), (out_ref, send_sems, recv_sems)
        self._copy(0, (x, y, 1 - c), me, *sems).wait_recv()
        for j, chip in enumerate(self._chips(x, y)):
            self._copy(4 + j, (*chip, 1 - c), me, *sems).wait_recv()
        for k in range(N_DEV - 1):
            self._copy(k, me, (x, y, 1 - c), *sems).wait_send()
        whole = out_ref.at[_linear(*me)]
        pltpu.make_async_copy(whole, whole, local_sem).wait()


def _sum_blocks(parts, name):
    n, r, cdim = parts.shape
    tr = _tile(r, 640, 16)

    def body(p_ref, o_ref):
        acc = p_ref[0].astype(F32)
        for d in range(1, n):
            acc = acc + p_ref[d].astype(F32)
        o_ref[...] = acc

    return pl.pallas_call(
        body, name=name, grid=(r // tr,), in_specs=[pl.BlockSpec((n, tr, cdim), lambda i: (0, i, 0))],
        out_specs=pl.BlockSpec((tr, cdim), lambda i: (i, 0)), out_shape=jax.ShapeDtypeStruct((r, cdim), F32),
        compiler_params=_params("parallel"),
    )(parts)


def _adamw_math(w, g, m, v):
    m = ADAM_B1 * m + (1.0 - ADAM_B1) * g
    v = ADAM_B2 * v + (1.0 - ADAM_B2) * (g * g)
    m_hat = m / (1.0 - ADAM_B1 ** ADAM_STEP)
    v_hat = v / (1.0 - ADAM_B2 ** ADAM_STEP)
    delta = -ADAM_LR * (m_hat / (jnp.sqrt(v_hat) + ADAM_EPS) + ADAM_WD * w)
    return delta, m, v


def _adamw(w, g, m, v, name):
    def body(w_ref, g_ref, m_ref, v_ref, d_ref, nm_ref, nv_ref):
        d_ref[...], nm_ref[...], nv_ref[...] = _adamw_math(w_ref[...], g_ref[...], m_ref[...], v_ref[...])

    out = jax.ShapeDtypeStruct(w.shape, F32)
    return pl.pallas_call(body, name=name, out_shape=(out, out, out),
                          compiler_params=pltpu.CompilerParams(vmem_limit_bytes=VMEM_LIMIT))(w, g, m, v)


def _small_all_reduce_adamw(parts, loss_part, ws, ms, vs, name):
    sizes = [p.shape[1] for p in parts] + [1]
    spots = [sum(-(-n // LANES) * LANES for n in sizes[:i]) for i in range(len(sizes))]
    width = spots[-1] + LANES
    k = len(parts)

    def reduce_body(*refs):
        p_refs, tot_ref, rows, send_sems, recv_sems = refs[:k + 1], *refs[k + 1:]
        x, y, c = _my_place()
        me = _linear(x, y, c)
        rows[me] = jnp.zeros((1, width), F32)
        for i in range(k + 1):
            rows[me, :, spots[i]:spots[i] + sizes[i]] = p_refs[i][...]
        copies = []
        for rel in range(1, N_DEV):
            copies.append(pltpu.make_async_remote_copy(
                src_ref=rows.at[me], dst_ref=rows.at[me], send_sem=send_sems.at[rel - 1], recv_sem=recv_sems.at[rel - 1],
                device_id=_relative(x, y, c, rel), device_id_type=MESH))
        for cp in copies:
            cp.start()
        for cp in copies:
            cp.wait_recv()
        for cp in copies:
            cp.wait_send()
        total = rows[0]
        for d in range(1, N_DEV):
            total = total + rows[d]
        tot_ref[...] = total

    total = pl.pallas_call(
        reduce_body, name=name, out_shape=jax.ShapeDtypeStruct((1, width), F32),
        scratch_shapes=[pltpu.VMEM((N_DEV, 1, width), F32), pltpu.SemaphoreType.DMA((7,)), pltpu.SemaphoreType.DMA((7,))],
    )(*parts, loss_part)

    def adamw_body(*refs):
        tot_ref, w_refs, m_refs, v_refs, outs = refs[0], refs[1:k + 1], refs[k + 1:2 * k + 1], refs[2 * k + 1:3 * k + 1], refs[3 * k + 1:]
        for i in range(k):
            g = tot_ref[:, spots[i]:spots[i] + sizes[i]]
            outs[4 * i][...] = g
            outs[4 * i + 1][...], outs[4 * i + 2][...], outs[4 * i + 3][...] = _adamw_math(w_refs[i][...], g, m_refs[i][...], v_refs[i][...])
        outs[4 * k][...] = tot_ref[:, spots[k]:spots[k] + 1]

    out_shape = [jax.ShapeDtypeStruct((1, n), F32) for n in sizes[:k] for _ in range(4)] + [jax.ShapeDtypeStruct((1, 1), F32)]
    res = pl.pallas_call(adamw_body, name=name + "_adamw", out_shape=tuple(out_shape))(total, *ws, *ms, *vs)
    return [res[4 * i:4 * i + 4] for i in range(k)], res[4 * k]


def _pad_rows(a, rows):
    return jnp.pad(a, ((0, rows - a.shape[0]), (0, 0)))


def kernel(x, positions, norm_mix_g, w_in, b_fgate, q_norm_g, w_uq, kv_norm_g, w_ukv, fox_out_g, mla_out_g, w_o, norm_ffn_g, w_gate, w_up, w_down, final_norm_g, loss_target, m_norm_mix_g, m_w_in, m_b_fgate, m_q_norm_g, m_w_uq, m_kv_norm_g, m_w_ukv, m_fox_out_g, m_mla_out_g, m_w_o, m_norm_ffn_g, m_w_gate, m_w_up, m_w_down, m_final_norm_g, v_norm_mix_g, v_w_in, v_b_fgate, v_q_norm_g, v_w_uq, v_kv_norm_g, v_w_ukv, v_fox_out_g, v_mla_out_g, v_w_o, v_norm_ffn_g, v_w_gate, v_w_up, v_w_down, v_final_norm_g):
    bl, s, d = x.shape
    t = bl * s
    bh = bl * HEADS
    tq = _tile(s, 256)
    grp = s // LANES
    fw = HEADS * HEAD_DIM
    q_rank, kv_rank = w_uq.shape[1], w_ukv.shape[1]
    in_cols = w_in.shape[2]
    n_in = N_DEV * in_cols
    ff = N_DEV * w_gate.shape[2]
    half = MLA_ROPE // 2
    o_kvlat, o_krope, o_flogit = q_rank, q_rank + kv_rank, q_rank + kv_rank + LANES
    b_cols = -(-(o_flogit + HEADS) // LANES) * LANES

    tr = lambda w: jnp.transpose(w[0])
    in_rows = -(-in_cols // 16) * 16
    uq_rows = w_uq.shape[2] * q_rank // d
    ukv_rows = w_ukv.shape[2] * kv_rank // d
    pieces = [_pad_rows(tr(w_in), in_rows), _pad_rows(tr(w_uq).reshape(uq_rows, d), -(-uq_rows // 16) * 16),
              tr(w_ukv).reshape(ukv_rows, d), w_o[0], tr(w_gate), tr(w_up), w_down[0]]
    pieces = [p.astype(BF16) for p in pieces]
    offs = [0]
    for p in pieces:
        offs.append(offs[-1] + p.shape[0])
    legs = [(0, 1), (1, 5), (5, 7)]
    gathered = {}

    def full(i, rows):
        leg = next(n for n, (lo, hi) in enumerate(legs) if lo <= i < hi)
        base = offs[legs[leg][0]]
        return gathered[leg][:, offs[i] - base:offs[i] - base + rows]

    x2d = x.reshape(t, d)
    h1, gathered[0] = _rmsnorm(x2d, 0, d, norm_mix_g, BF16, "norm_mix", traffic=_Relay(pieces[0]))

    w_in_t = full(0, in_cols).reshape(n_in, d)
    n_qkv = 3 * fw
    w_in_a = w_in_t[:n_qkv].reshape(3, PAIRS, LANES, d).transpose(1, 0, 2, 3).reshape(n_qkv, d)
    lat0, rope0 = n_qkv + HEADS, n_qkv + HEADS + q_rank + kv_rank
    k_rep = jnp.broadcast_to(w_in_t[rope0:].reshape(2, 1, half, d), (2, 4, half, d)).reshape(LANES, d)
    w_in_b = jnp.concatenate([w_in_t[lat0:rope0], k_rep, w_in_t[n_qkv:lat0],
                              jnp.zeros((b_cols - o_flogit - HEADS, d), BF16)], axis=0)

    def per_head_rows(a):
        return a.reshape(bl, s, HEADS).transpose(0, 2, 1).reshape(bh, 1, s)

    proj_a = _matmul(h1, w_in_a, "nt", BF16, "proj_fox", tm=1024, tn=6 * LANES)
    proj_b = _matmul(h1, w_in_b, "nt", F32, "proj_mla", tm=1024, tn=b_cols)

    z = proj_b[:, o_flogit:o_flogit + HEADS].reshape(bl, s, HEADS).transpose(0, 2, 1).reshape(bh * grp, LANES)
    bcol = jnp.broadcast_to(b_fgate.reshape(1, HEADS, 1), (bl, HEADS, grp)).reshape(bh * grp, 1)
    c = _fgate(z, bcol, grp, "forget_gate")
    c_bias = c.reshape(bh, 1, s)
    fox_o, fox_lse, gathered[1] = _attn_fwd((proj_a,), c_bias, HEAD_DIM ** -0.5, bl, s, tq, "fox_attention",
                                            traffic=_Traffic("spread", pieces[legs[1][0]:legs[1][1]]))
    w_uq_h = full(1, uq_rows).reshape(HEADS, MLA_QK, q_rank)
    w_uq_pe = jnp.concatenate([w_uq_h[:, HEAD_DIM:HEAD_DIM + half].reshape(2, 1, 4 * half, q_rank),
                               w_uq_h[:, HEAD_DIM + half:].reshape(2, 1, 4 * half, q_rank)], axis=1).reshape(2 * LANES, q_rank)
    w_uq_p = jnp.concatenate([w_uq_h[:, :HEAD_DIM].reshape(fw, q_rank), w_uq_pe], axis=0)
    w_ukv_p = full(2, ukv_rows).reshape(PAIRS, 2, 2, HEAD_DIM, kv_rank).transpose(0, 2, 1, 3, 4).reshape(2 * fw, kv_rank)
    w_o_f = full(3, w_o.shape[1]).reshape(-1, d)
    w_gate_t = full(4, ff // N_DEV).reshape(ff, d)

    inv_freq = ROPE_THETA ** (-jnp.arange(0, MLA_ROPE, 2, dtype=F32) / MLA_ROPE)
    ang = positions.astype(F32).reshape(t, 1) * inv_freq[None, :]
    rope_cos, rope_sin = jnp.cos(ang), jnp.sin(ang)
    qn, kvn, q_all, kv_all, kpe = _mla_prep(proj_b, q_rank, kv_rank, q_norm_g, kv_norm_g, w_uq_p, w_ukv_p, fw,
                                            rope_cos, rope_sin, "mla_prep")
    mla_ops = (q_all, kv_all, kpe)
    mla_o, mla_lse, gathered[2] = _attn_fwd(mla_ops, None, MLA_QK ** -0.5, bl, s, tq, "mla_attention",
                                            traffic=_Traffic("spread", pieces[legs[2][0]:legs[2][1]]))
    w_up_t, w_down_f = full(5, ff // N_DEV).reshape(ff, d), full(6, ff // N_DEV).reshape(ff, d)

    both = [(d, F32), (d, BF16)]
    cat, x1, h2 = _rows_matmul([(None, w_o_f, "nn")], [fox_o, mla_o, x2d], [fox_out_g, mla_out_g, norm_ffn_g], _residual_norm,
                               [(2 * fw, BF16)] + both, [], "norm_out_proj_out_norm_ffn", prologue=_out_norm)
    act_by_gate, act_by_up, act = _ffn_up(h2, w_gate_t, w_up_t, "ffn_gate_up")
    dx2, dx2_b, dg_final, loss_part = _rows_matmul(
        [(act, w_down_f, "nn")], [x1, loss_target.reshape(t, d)], [final_norm_g.reshape(1, d)], _residual_loss_bwd,
        both, [d, 1], "ffn_down_final_norm_loss")

    d_gate, d_up = _ffn_down_bwd(dx2_b, w_down_f, act_by_gate, act_by_up, "d_ffn_down")
    dw_down = _matmul(act, dx2_b, "tn", BF16, "dw_down", tm=ff // 2, tn=d, tk=2048)
    dw_gate = _matmul(d_gate, h2, "tn", BF16, "dw_gate", tm=ff // 2, tn=d, tk=2048)
    dw_up = _matmul(d_up, h2, "tn", BF16, "dw_up", tm=ff // 2, tn=d, tk=2048)
    dx1, dx1_b, dg_ffn = _rows_matmul([(d_gate, w_gate_t, "nn"), (d_up, w_up_t, "nn")], [x1, dx2], [norm_ffn_g],
                                      _norm_bwd_residual, both, [d], "d_ffn_gate_up_norm_ffn", tm=256)
    dw_o = _matmul(cat, dx1_b, "tn", BF16, "dw_o", tn=d, tk=2048)
    d_fox_o, d_mla_o, fox_delta, mla_delta, dg_fox, dg_mla = _rows_matmul(
        [(dx1_b, w_o_f, "nt")], [fox_o, mla_o], [fox_out_g, mla_out_g], _out_norm_bwd,
        [(fw, BF16), (fw, BF16), (HEADS, F32), (HEADS, F32)], [fw, fw], "d_proj_out_norm_out")

    per_dev = lambda a: a.reshape(N_DEV, -1, d)
    late_grads = [per_dev(dw_o), per_dev(dw_gate), per_dev(dw_up), per_dev(dw_down)]
    dproj_a, dc, g_late = _attn_bwd((proj_a,), d_fox_o, fox_lse, per_head_rows(fox_delta),
                                    c_bias, HEAD_DIM ** -0.5, bl, s, tq, "d_fox_attention", traffic=_Traffic("swap", late_grads))
    dz, db_fgate = _fgate_bwd(z, bcol, dc.reshape(bh * grp, LANES), grp, "d_forget_gate")
    d_flogit = dz.reshape(bl, HEADS, s).transpose(0, 2, 1).reshape(t, HEADS)

    dq_nope, dkv_all, dq_pe, dk_pe = _attn_bwd(mla_ops, d_mla_o, mla_lse, per_head_rows(mla_delta),
                                               None, MLA_QK ** -0.5, bl, s, tq, "d_mla_attention")
    d_tail = jnp.pad(d_flogit, ((0, 0), (0, b_cols - o_flogit - HEADS)))
    dproj_b, dq_rot, dg_q, dg_kv = _mla_prep_bwd(dq_nope, dq_pe, dkv_all, dk_pe, d_tail, proj_b, q_rank, kv_rank,
                                                 q_norm_g, kv_norm_g, w_uq_p, w_ukv_p, rope_cos, rope_sin, "d_mla_prep")
    dw_uq_nope = _matmul(dq_nope, qn, "tn", BF16, "dw_uq_nope", tn=q_rank, tk=1024)
    dw_uq_pe = _matmul(dq_rot, qn, "tn", BF16, "dw_uq_rope", tn=q_rank, tk=1024)
    dw_ukv_p = _matmul(dkv_all, kvn, "tn", BF16, "dw_ukv", tn=kv_rank, tk=1024)
    dw_in_a = _matmul(dproj_a, h1, "tn", BF16, "dw_in_fox", tm=6 * LANES, tn=d, tk=2048)
    dw_in_b = _matmul(dproj_b, h1, "tn", F32, "dw_in_mla", tm=b_cols, tn=d, tk=1024)

    dw_krope = dw_in_b[o_krope:o_flogit].reshape(2, 4, half, d).sum(axis=1).reshape(MLA_ROPE, d)
    dw_in_t = jnp.concatenate([dw_in_a.reshape(PAIRS, 3, LANES, d).transpose(1, 0, 2, 3).reshape(n_qkv, d),
                               dw_in_b[o_flogit:o_flogit + HEADS].astype(BF16), dw_in_b[:o_krope].astype(BF16),
                               dw_krope.astype(BF16)], axis=0)
    pad_dev = lambda a, rows: jnp.pad(a, ((0, 0), (0, rows - a.shape[1]), (0, 0)))
    dw_uq_pe5 = dw_uq_pe.reshape(2, 2, 4, half, q_rank)
    dw_uq_h = jnp.concatenate([dw_uq_nope.reshape(HEADS, HEAD_DIM, q_rank), dw_uq_pe5[:, 0].reshape(HEADS, half, q_rank),
                               dw_uq_pe5[:, 1].reshape(HEADS, half, q_rank)], axis=1)
    dw_ukv_h = dw_ukv_p.reshape(PAIRS, 2, 2, HEAD_DIM, kv_rank).transpose(0, 2, 1, 3, 4).reshape(HEADS, 2 * HEAD_DIM, kv_rank)
    n_last = 3
    last_grads = [pad_dev(per_dev(dw_in_t), pieces[0].shape[0]), pad_dev(per_dev(dw_uq_h), pieces[1].shape[0]), per_dev(dw_ukv_h)]
    dh1_fox, g_last = _matmul(dproj_a, w_in_a, "nn", F32, "d_proj_fox", tn=d, traffic=_Traffic("swap", last_grads))
    grad_x, dg_mix = _rows_matmul([(dproj_b, w_in_b, "nn")], [x2d, dx1, dh1_fox], [norm_mix_g], _norm_bwd_residual,
                                  [(d, F32)], [d], "d_proj_mla_norm_mix")
    g_last = _sum_blocks(g_last, "sum_last_grads")
    g_late = _sum_blocks(g_late, "sum_late_grads")

    def mine(i, rows):
        src, base = (g_last, 0) if i < n_last else (g_late, offs[n_last])
        return src[offs[i] - base:offs[i] - base + rows]

    big = [
        ("w_in", w_in, m_w_in, v_w_in, mine(0, in_cols), True),
        ("w_uq", w_uq, m_w_uq, v_w_uq, mine(1, uq_rows).reshape(-1, q_rank), True),
        ("w_ukv", w_ukv, m_w_ukv, v_w_ukv, mine(2, ukv_rows).reshape(-1, kv_rank), True),
        ("w_o", w_o, m_w_o, v_w_o, mine(3, w_o.shape[1]), False),
        ("w_gate", w_gate, m_w_gate, v_w_gate, mine(4, ff // N_DEV), True),
        ("w_up", w_up, m_w_up, v_w_up, mine(5, ff // N_DEV), True),
        ("w_down", w_down, m_w_down, v_w_down, mine(6, ff // N_DEV), False),
    ]
    out = {}
    for nm, w, m, v, g, transposed in big:
        lay = (lambda a: a[0].T) if transposed else (lambda a: a[0])
        back = (lambda a: a.T[None]) if transposed else (lambda a: a[None])
        dl, new_m, new_v = _adamw(lay(w), g, lay(m), lay(v), "adamw_" + nm)
        out[nm] = (back(g), back(dl), back(new_m), back(new_v))

    smalls = [("norm_mix_g", norm_mix_g, m_norm_mix_g, v_norm_mix_g, dg_mix),
              ("b_fgate", b_fgate, m_b_fgate, v_b_fgate, db_fgate.reshape(1, HEADS)),
              ("q_norm_g", q_norm_g, m_q_norm_g, v_q_norm_g, dg_q),
              ("kv_norm_g", kv_norm_g, m_kv_norm_g, v_kv_norm_g, dg_kv),
              ("fox_out_g", fox_out_g, m_fox_out_g, v_fox_out_g, dg_fox),
              ("mla_out_g", mla_out_g, m_mla_out_g, v_mla_out_g, dg_mla),
              ("norm_ffn_g", norm_ffn_g, m_norm_ffn_g, v_norm_ffn_g, dg_ffn),
              ("final_norm_g", final_norm_g, m_final_norm_g, v_final_norm_g, dg_final)]
    flat = lambda a: a.reshape(1, -1)
    results, loss = _small_all_reduce_adamw([e[4] for e in smalls], loss_part, [flat(e[1]) for e in smalls],
                                            [flat(e[2]) for e in smalls], [flat(e[3]) for e in smalls], "reduce_small_adamw")
    for (nm, w, _, _, _), res in zip(smalls, results):
        out[nm] = tuple(a.reshape(w.shape) for a in res)
    loss = loss[0, 0]

    order = ["norm_mix_g", "w_in", "b_fgate", "q_norm_g", "w_uq", "kv_norm_g", "w_ukv", "fox_out_g", "mla_out_g", "w_o",
             "norm_ffn_g", "w_gate", "w_up", "w_down", "final_norm_g"]
    return (loss, grad_x.reshape(bl, s, d), *[out[n][0] for n in order], *[out[n][1] for n in order],
            *[out[n][2] for n in order], *[out[n][3] for n in order])
```

```python
import math

import jax
import jax.numpy as jnp
from jax import lax
from jax.experimental import pallas as pl
from jax.experimental.pallas import tpu as pltpu

F32 = jnp.float32
BF16 = jnp.bfloat16
MESH = pl.DeviceIdType.MESH

N_DEV = 8
HEADS = 8
HEAD_DIM = 64
PAIRS = HEADS // 2
MLA_ROPE = 32
MLA_QK = HEAD_DIM + MLA_ROPE
ROPE_THETA = 10000.0
NORM_EPS = 1e-6
ADAM_LR, ADAM_B1, ADAM_B2, ADAM_EPS, ADAM_WD, ADAM_STEP = 0.001, 0.9, 0.999, 1e-08, 0.01, 10

LANES = 128
MASKED = -1e30
VMEM_LIMIT = 48 * 1024 * 1024

_DIMS = {"nn": (((1,), (0,)), ((), ())), "nt": (((1,), (1,)), ((), ())), "tn": (((0,), (0,)), ((), ()))}


def _params(*sem):
    return pltpu.CompilerParams(dimension_semantics=sem, vmem_limit_bytes=VMEM_LIMIT)


def _dot(a, b, mode):
    return lax.dot_general(a.astype(BF16), b.astype(BF16), _DIMS[mode], preferred_element_type=F32)


def _tile(n, pref, unit=8):
    if n <= pref:
        return n
    t = pref - pref % unit
    while n % t:
        t -= unit
    return t


def _log2(n):
    assert n & (n - 1) == 0
    return n.bit_length() - 1


def _matmul(a, b, mode, out_dtype, name, tm=512, tn=512, tk=None):
    if mode == "nn":
        (m, kd), n = a.shape, b.shape[1]
    elif mode == "nt":
        (m, kd), n = a.shape, b.shape[0]
    else:
        (kd, m), n = a.shape, b.shape[1]
    tm, tn = _tile(m, tm, LANES if mode == "tn" else 16), _tile(n, tn, LANES)
    tk = kd if tk is None else _tile(kd, tk, LANES)
    nk = kd // tk
    a_spec = pl.BlockSpec((tk, tm), lambda i, j, k: (k, i)) if mode == "tn" else pl.BlockSpec((tm, tk), lambda i, j, k: (i, k))
    b_spec = pl.BlockSpec((tn, tk), lambda i, j, k: (j, k)) if mode == "nt" else pl.BlockSpec((tk, tn), lambda i, j, k: (k, j))
    o_spec = pl.BlockSpec((tm, tn), lambda i, j, k: (i, j))

    def body(a_ref, b_ref, o_ref, *acc):
        part = _dot(a_ref[...], b_ref[...], mode)
        if nk == 1:
            o_ref[...] = part.astype(out_dtype)
        else:
            acc_ref, k = acc[0], pl.program_id(2)

            @pl.when(k == 0)
            def _():
                acc_ref[...] = part

            @pl.when(k > 0)
            def _():
                acc_ref[...] += part

            @pl.when(k == nk - 1)
            def _():
                o_ref[...] = acc_ref[...].astype(out_dtype)

    return pl.pallas_call(
        body, name=name, grid=(m // tm, n // tn, nk), in_specs=[a_spec, b_spec], out_specs=o_spec,
        out_shape=jax.ShapeDtypeStruct((m, n), out_dtype),
        scratch_shapes=[pltpu.VMEM((tm, tn), F32)] if nk > 1 else [],
        compiler_params=_params("parallel", "parallel", "arbitrary"),
    )(a, b)


def _rstd(x):
    return lax.rsqrt(jnp.mean(x * x, axis=-1, keepdims=True) + NORM_EPS)


def _norm_bwd(x, g, dy):
    r = _rstd(x)
    xh = x * r
    u = dy * g
    dx = r * (u - xh * jnp.mean(u * xh, axis=-1, keepdims=True))
    return dx, jnp.sum(dy * xh, axis=0, keepdims=True)


def _rmsnorm(x, col, width, g, out_dtype, name, traffic=None):
    t = x.shape[0]
    tm = _tile(t, 512)
    steps = t // tm
    n_carried = len(traffic.pieces) if traffic else 0

    def body(*refs):
        x_ref, g_ref, o_ref = refs[0], refs[1], refs[2 + n_carried]
        if traffic:
            carried_in, carried_out, sems = refs[2:2 + n_carried], refs[3 + n_carried], refs[4 + n_carried:]

            @pl.when(pl.program_id(0) == 0)
            def _():
                traffic.start(carried_in, carried_out, *sems)

            if isinstance(traffic, _Relay):
                @pl.when(pl.program_id(0) == max(steps - 2, 0))
                def _():
                    traffic.relay(carried_out, *sems)

        xv = x_ref[...]
        o_ref[...] = ((xv * _rstd(xv)) * g_ref[...]).astype(out_dtype)
        if traffic:
            @pl.when(pl.program_id(0) == steps - 1)
            def _():
                traffic.wait(carried_out, *sems)

    in_specs = [pl.BlockSpec((tm, width), lambda i: (i, col)), pl.BlockSpec((1, width), lambda i: (0, 0))]
    out_specs = [pl.BlockSpec((tm, width), lambda i: (i, 0))]
    out_shape = [jax.ShapeDtypeStruct((t, width), out_dtype)]
    if traffic:
        in_specs += traffic.in_specs
        out_specs.append(traffic.out_spec)
        out_shape.append(traffic.out_shape)
    out = pl.pallas_call(
        body, name=name, grid=(steps,), in_specs=in_specs, out_specs=tuple(out_specs), out_shape=tuple(out_shape),
        scratch_shapes=traffic.scratch if traffic else [],
        compiler_params=_params("arbitrary" if traffic else "parallel"),
    )(x, g, *(traffic.pieces if traffic else []))
    return out if traffic else out[0]


def _split3(x):
    hi = x.astype(BF16)
    r1 = x - hi.astype(F32)
    mid = r1.astype(BF16)
    lo = (r1 - mid.astype(F32)).astype(BF16)
    return hi, mid, lo


def _dot_x01(x, m01):
    hi, mid, lo = _split3(x)
    d = lambda p: lax.dot_general(p, m01, _DIMS["nn"], preferred_element_type=F32)
    return (d(lo) + d(mid)) + d(hi)


def _dot_01x(m01, x):
    hi, mid, lo = _split3(x)
    d = lambda p: lax.dot_general(m01, p, _DIMS["nn"], preferred_element_type=F32)
    return (d(lo) + d(mid)) + d(hi)


def _rows_matmul(terms, rows_in, vecs_in, epilogue, rows_out, sums_out, name, tm=512, prologue=None, traffic=None):
    t = rows_in[0].shape[0]
    tm = _tile(t, tm, 16)
    steps = t // tm
    n_rows, n_vecs = len(rows_in), len(vecs_in)
    n_ab = sum(1 + (a is not None) for a, _, _ in terms)
    n_carried = len(traffic.pieces) if traffic else 0
    halves = [slice(0, tm // 2), slice(tm // 2, tm)] if tm % 32 == 0 else [slice(0, tm)]

    def body(*refs):
        vecs = [r[...] for r in refs[n_ab + n_rows:n_ab + n_rows + n_vecs]]
        out_at = n_ab + n_rows + n_vecs + n_carried
        sum_refs = refs[out_at + len(rows_out):out_at + len(rows_out) + len(sums_out)]
        if traffic:
            carried_in, carried_out, sems = refs[out_at - n_carried:out_at], refs[len(refs) - 4], refs[len(refs) - 3:]

            @pl.when(pl.program_id(0) == 0)
            def _():
                traffic.start(carried_in, carried_out, *sems)

        @pl.when(pl.program_id(0) == 0)
        def _():
            for ref in sum_refs:
                ref[...] = jnp.zeros_like(ref)

        staged = []
        for rows_of in halves:
            row_blocks = [r[rows_of, :] for r in refs[n_ab:n_ab + n_rows]]
            made = prologue(row_blocks, vecs) if prologue else None
            acc, at = None, 0
            for a, _, mode in terms:
                lhs = made if a is None else refs[at][rows_of, :]
                at += a is not None
                part = _dot(lhs, refs[at][...], mode)
                at += 1
                acc = part if acc is None else acc + part
            staged.append((rows_of, row_blocks, made, acc))
        for rows_of, row_blocks, made, acc in staged:
            row_vals, sum_vals = epilogue(acc, row_blocks, vecs)
            if prologue:
                row_vals = [made] + row_vals
            for ref, val, (_, dtype) in zip(refs[out_at:], row_vals, rows_out):
                ref[rows_of, :] = val.astype(dtype)
            for ref, val in zip(sum_refs, sum_vals):
                ref[...] += val
        if traffic:
            @pl.when(pl.program_id(0) == steps - 1)
            def _():
                traffic.wait(carried_out, *sems)

    rows = lambda w: pl.BlockSpec((tm, w), lambda i: (i, 0))
    whole = lambda a: pl.BlockSpec(a.shape, lambda i: (0, 0))
    in_specs, args = [], []
    for a, b, _ in terms:
        in_specs += ([rows(a.shape[1])] if a is not None else []) + [whole(b)]
        args += ([a] if a is not None else []) + [b]
    in_specs += [rows(r.shape[1]) for r in rows_in] + [whole(v) for v in vecs_in]
    args += list(rows_in) + list(vecs_in)
    out_specs = [rows(w) for w, _ in rows_out] + [pl.BlockSpec((1, w), lambda i: (0, 0)) for w in sums_out]
    out_shape = [jax.ShapeDtypeStruct((t, w), dt) for w, dt in rows_out] + [jax.ShapeDtypeStruct((1, w), F32) for w in sums_out]
    if traffic:
        in_specs += traffic.in_specs
        args += traffic.pieces
        out_specs.append(traffic.out_spec)
        out_shape.append(traffic.out_shape)
    return pl.pallas_call(
        body, name=name, grid=(steps,), in_specs=in_specs, out_specs=tuple(out_specs), out_shape=tuple(out_shape),
        scratch_shapes=traffic.scratch if traffic else [], compiler_params=_params("arbitrary"),
    )(*args)


def _out_norm(rows, vecs):
    (f, m), (gf, gm) = rows[:2], vecs[:2]
    return jnp.concatenate([((f * _rstd(f)) * gf).astype(BF16), ((m * _rstd(m)) * gm).astype(BF16)], axis=1)


def _residual_norm(acc, rows, vecs):
    x1 = rows[-1] + acc
    return [x1, (x1 * _rstd(x1)) * vecs[-1]], []


def _residual_loss_bwd(acc, rows, vecs):
    x2, gv = rows[0] + acc, vecs[0]
    diff = (x2 * _rstd(x2)) * gv - rows[1]
    dx, dg = _norm_bwd(x2, gv, diff / x2.shape[1])
    return [dx, dx], [dg, 0.5 * jnp.sum(jnp.mean(diff * diff, axis=-1, keepdims=True), axis=0, keepdims=True)]


def _norm_bwd_residual(acc, rows, vecs):
    dy = acc + rows[2] if len(rows) > 2 else acc
    dx, dg = _norm_bwd(rows[0], vecs[0], dy)
    if len(rows) > 1:
        dx = dx + rows[1]
    return [dx, dx], [dg]


def _out_norm_bwd(acc, rows, vecs):
    (f, m), w = rows, rows[0].shape[1]
    nh = w // HEAD_DIM
    lane_head = lax.shift_right_logical(lax.broadcasted_iota(jnp.int32, (w, nh), 0), _log2(HEAD_DIM))
    sel = (lane_head == lax.broadcasted_iota(jnp.int32, (w, nh), 1)).astype(BF16)
    dfo, dgf = _norm_bwd(f, vecs[0], acc[:, :w])
    dmo, dgm = _norm_bwd(m, vecs[1], acc[:, w:])
    return [dfo, dmo, _dot_x01(dfo * f, sel), _dot_x01(dmo * m, sel)], [dgf, dgm]


def _ffn_up(h, wg_t, wu_t, name, tm=512, tf=1408):
    t, d = h.shape
    f = wg_t.shape[0]
    tm, tf = _tile(t, tm, 16), _tile(f, tf, LANES)
    tok = pl.BlockSpec((tm, tf), lambda j, i: (i, j))
    wt = pl.BlockSpec((tf, d), lambda j, i: (j, 0))

    def body(h_ref, wg_ref, wu_ref, dg_ref, du_ref, a_ref):
        hv = h_ref[...]
        g, u = _dot(hv, wg_ref[...], "nt"), _dot(hv, wu_ref[...], "nt")
        sg = jax.nn.sigmoid(g)
        silu = g * sg
        dg_ref[...] = (u * (sg * (1.0 + g * (1.0 - sg)))).astype(BF16)
        du_ref[...] = silu.astype(BF16)
        a_ref[...] = (silu * u).astype(BF16)

    return pl.pallas_call(
        body, name=name, grid=(f // tf, t // tm), in_specs=[pl.BlockSpec((tm, d), lambda j, i: (i, 0)), wt, wt],
        out_specs=(tok, tok, tok),
        out_shape=(jax.ShapeDtypeStruct((t, f), BF16), jax.ShapeDtypeStruct((t, f), BF16), jax.ShapeDtypeStruct((t, f), BF16)),
        compiler_params=_params("parallel", "parallel"),
    )(h, wg_t, wu_t)


def _ffn_down_bwd(dy, w_down, act_by_gate, act_by_up, name, tm=512, tf=1408):
    t, d = dy.shape
    f = w_down.shape[0]
    tm, tf = _tile(t, tm, 16), _tile(f, tf, LANES)
    tok = pl.BlockSpec((tm, tf), lambda j, i: (i, j))

    def body(dy_ref, w_ref, g_ref, u_ref, dg_ref, du_ref):
        da = _dot(dy_ref[...], w_ref[...], "nt")
        dg_ref[...] = (da * g_ref[...].astype(F32)).astype(BF16)
        du_ref[...] = (da * u_ref[...].astype(F32)).astype(BF16)

    return pl.pallas_call(
        body, name=name, grid=(f // tf, t // tm),
        in_specs=[pl.BlockSpec((tm, d), lambda j, i: (i, 0)), pl.BlockSpec((tf, d), lambda j, i: (j, 0)), tok, tok],
        out_specs=(tok, tok),
        out_shape=(jax.ShapeDtypeStruct((t, f), BF16), jax.ShapeDtypeStruct((t, f), BF16)),
        compiler_params=_params("parallel", "parallel"),
    )(dy, w_down, act_by_gate, act_by_up)


def _chunk_scan_mats(rows, grp, reverse):
    ii = lax.broadcasted_iota(jnp.int32, (LANES, LANES), 0)
    jj = lax.broadcasted_iota(jnp.int32, (LANES, LANES), 1)
    within = ((ii >= jj) if reverse else (ii <= jj)).astype(BF16)
    ones = jnp.ones((LANES, LANES), BF16)
    ri = lax.broadcasted_iota(jnp.int32, (rows, rows), 0)
    rj = lax.broadcasted_iota(jnp.int32, (rows, rows), 1)
    sh = _log2(grp)
    same = lax.shift_right_logical(ri, sh) == lax.shift_right_logical(rj, sh)
    across = (same & ((rj > ri) if reverse else (rj < ri))).astype(BF16)
    return within, ones, across


def _running_sum(v, mats):
    within, ones, across = mats
    return _dot_x01(v, within) + _dot_01x(across, _dot_x01(v, ones))


def _fgate(z, bcol, grp, name):
    rows = z.shape[0]

    def body(z_ref, b_ref, c_ref):
        zz = z_ref[...] + b_ref[...]
        log_f = jnp.minimum(zz, 0.0) - jnp.log1p(jnp.exp(-jnp.abs(zz)))
        c_ref[...] = _running_sum(log_f, _chunk_scan_mats(rows, grp, False))

    return pl.pallas_call(body, name=name, out_shape=jax.ShapeDtypeStruct(z.shape, F32),
                          compiler_params=pltpu.CompilerParams(vmem_limit_bytes=VMEM_LIMIT))(z, bcol)


def _fgate_bwd(z, bcol, dc, grp, name):
    rows = z.shape[0]

    def body(z_ref, b_ref, dc_ref, dz_ref, db_ref):
        zz = z_ref[...] + b_ref[...]
        dz = _running_sum(dc_ref[...], _chunk_scan_mats(rows, grp, True)) * jax.nn.sigmoid(-zz)
        dz_ref[...] = dz
        head = lax.shift_right_logical(lax.broadcasted_iota(jnp.int32, (HEADS, rows), 1), _log2(grp)) & (HEADS - 1)
        sel = (head == lax.broadcasted_iota(jnp.int32, (HEADS, rows), 0)).astype(BF16)
        db_ref[...] = jnp.sum(_dot_01x(sel, dz), axis=1, keepdims=True)

    return pl.pallas_call(
        body, name=name,
        out_shape=(jax.ShapeDtypeStruct(z.shape, F32), jax.ShapeDtypeStruct((HEADS, 1), F32)),
        compiler_params=pltpu.CompilerParams(vmem_limit_bytes=VMEM_LIMIT),
    )(z, bcol, dc)


def _rotate(x, cs, sn_signed):
    return x * cs + pltpu.roll(x, LANES // 2, axis=1) * sn_signed


def _rope_tables(cos, sin):
    half = cos.shape[1]
    freq = lax.broadcasted_iota(jnp.int32, (half, LANES), 0)
    lane = lax.broadcasted_iota(jnp.int32, (half, LANES), 1)
    hit = (lane & (half - 1)) == freq
    sign = jnp.where(lane < LANES // 2, -1.0, 1.0)
    return _dot_x01(cos, hit.astype(BF16)), _dot_x01(sin, jnp.where(hit, sign, 0.0).astype(BF16))


def _mla_prep(proj_b, q_rank, kv_rank, gq, gkv, w_uq_p, w_ukv_p, nope, cs, sn, name):
    t, bw = proj_b.shape
    qw, kvw = w_uq_p.shape[0], w_ukv_p.shape[0]
    tm = _tile(t, 512)
    rows = lambda w: pl.BlockSpec((tm, w), lambda i: (i, 0))
    whole = lambda a: pl.BlockSpec(a.shape, lambda i: (0, 0))

    def body(pb_ref, gq_ref, gkv_ref, wq_ref, wkv_ref, c_ref, s_ref, qn_ref, kvn_ref, q_ref, kv_ref, kpe_ref):
        c, s = _rope_tables(c_ref[...], s_ref[...])
        ql, kvl = pb_ref[:, :q_rank], pb_ref[:, q_rank:q_rank + kv_rank]
        qn = ((ql * _rstd(ql)) * gq_ref[...]).astype(BF16)
        kvn = ((kvl * _rstd(kvl)) * gkv_ref[...]).astype(BF16)
        qn_ref[...], kvn_ref[...] = qn, kvn
        q_raw = _dot(qn, wq_ref[...], "nt")
        q_ref[:, :nope] = q_raw[:, :nope].astype(BF16)
        for off in range(nope, qw, LANES):
            q_ref[:, off:off + LANES] = _rotate(q_raw[:, off:off + LANES], c, s).astype(BF16)
        kv_ref[...] = _dot(kvn, wkv_ref[...], "nt").astype(BF16)
        kpe_ref[...] = _rotate(pb_ref[:, q_rank + kv_rank:q_rank + kv_rank + LANES], c, s).astype(BF16)

    return pl.pallas_call(
        body, name=name, grid=(t // tm,),
        in_specs=[rows(bw), whole(gq), whole(gkv), whole(w_uq_p), whole(w_ukv_p), rows(cs.shape[1]), rows(sn.shape[1])],
        out_specs=(rows(q_rank), rows(kv_rank), rows(qw), rows(kvw), rows(LANES)),
        out_shape=(jax.ShapeDtypeStruct((t, q_rank), BF16), jax.ShapeDtypeStruct((t, kv_rank), BF16),
                   jax.ShapeDtypeStruct((t, qw), BF16), jax.ShapeDtypeStruct((t, kvw), BF16), jax.ShapeDtypeStruct((t, LANES), BF16)),
        compiler_params=_params("parallel"),
    )(proj_b, gq, gkv, w_uq_p, w_ukv_p, cs, sn)


def _mla_prep_bwd(dq_nope, dq_pe, dkv_all, dk_pe, d_tail, proj_b, q_rank, kv_rank, gq, gkv, w_uq_p, w_ukv_p, cs, sn, name):
    t, bw = proj_b.shape
    nope, pw = dq_nope.shape[1], dq_pe.shape[1]
    tm = _tile(t, 512)
    rows = lambda w: pl.BlockSpec((tm, w), lambda i: (i, 0))
    whole = lambda a: pl.BlockSpec(a.shape, lambda i: (0, 0))
    o_k = q_rank + kv_rank

    def body(dqn_ref, dqp_ref, dkv_ref, dkp_ref, dt_ref, pb_ref, gq_ref, gkv_ref, wq_ref, wkv_ref, c_ref, s_ref,
             dpb_ref, dqr_ref, dgq_ref, dgkv_ref):
        c, s = _rope_tables(c_ref[...], -s_ref[...])
        for off in range(0, pw, LANES):
            dqr_ref[:, off:off + LANES] = _rotate(dqp_ref[:, off:off + LANES], c, s).astype(BF16)
        d_qn = _dot(dqn_ref[...], wq_ref[:nope, :], "nn") + _dot(dqr_ref[...], wq_ref[nope:, :], "nn")
        dq_lat, dgq = _norm_bwd(pb_ref[:, :q_rank], gq_ref[...], d_qn)
        dkv_lat, dgkv = _norm_bwd(pb_ref[:, q_rank:o_k], gkv_ref[...], _dot(dkv_ref[...], wkv_ref[...], "nn"))
        dpb_ref[:, :q_rank] = dq_lat.astype(BF16)
        dpb_ref[:, q_rank:o_k] = dkv_lat.astype(BF16)
        dpb_ref[:, o_k:o_k + LANES] = _rotate(dkp_ref[...], c, s).astype(BF16)
        dpb_ref[:, o_k + LANES:] = dt_ref[...].astype(BF16)

        @pl.when(pl.program_id(0) == 0)
        def _():
            dgq_ref[...] = jnp.zeros_like(dgq_ref)
            dgkv_ref[...] = jnp.zeros_like(dgkv_ref)

        dgq_ref[...] += dgq
        dgkv_ref[...] += dgkv

    return pl.pallas_call(
        body, name=name, grid=(t // tm,),
        in_specs=[rows(nope), rows(pw), rows(dkv_all.shape[1]), rows(LANES), rows(bw - o_k - LANES), rows(bw), whole(gq), whole(gkv),
                  whole(w_uq_p), whole(w_ukv_p), rows(cs.shape[1]), rows(sn.shape[1])],
        out_specs=(rows(bw), rows(pw), whole(gq), whole(gkv)),
        out_shape=(jax.ShapeDtypeStruct((t, bw), BF16), jax.ShapeDtypeStruct((t, pw), BF16),
                   jax.ShapeDtypeStruct(gq.shape, F32), jax.ShapeDtypeStruct(gkv.shape, F32)),
        compiler_params=_params("arbitrary"),
    )(dq_nope, dq_pe, dkv_all, dk_pe, d_tail, proj_b, gq, gkv, w_uq_p, w_ukv_p, cs, sn)


def _lane_masks(pair, h, pe):
    lane = lax.broadcasted_iota(jnp.int32, (1, LANES), 1)
    in_head = lax.shift_right_logical(lane, _log2(HEAD_DIM)) == h
    in_rope = ((lax.shift_right_logical(lane, _log2(MLA_ROPE // 2)) & 3) == ((2 * pair + h) & 3)) if pe else None
    return in_head, in_rope


def _keep(mask, v):
    return jnp.where(mask, v, jnp.zeros_like(v))


def _to_row(col):
    n = col.shape[0]
    eye = lax.broadcasted_iota(jnp.int32, (n, n), 0) == lax.broadcasted_iota(jnp.int32, (n, n), 1)
    return jnp.sum(jnp.where(eye, col, 0.0), axis=0, keepdims=True)


def _to_col(row):
    n = row.shape[1]
    eye = lax.broadcasted_iota(jnp.int32, (n, n), 0) == lax.broadcasted_iota(jnp.int32, (n, n), 1)
    return jnp.sum(jnp.where(eye, row, 0.0), axis=1, keepdims=True)


def _first_step():
    return (pl.program_id(0) == 0) & (pl.program_id(1) == 0)


def _last_step(n0, n1):
    return (pl.program_id(0) == n0 - 1) & (pl.program_id(1) == n1 - 1)


def _attn_fwd(ops, bias, scale, bl, s, tq, name, traffic=None):
    pe = len(ops) == 3
    has_bias = bias is not None
    exact_scale = math.frexp(scale)[0] == 0.5
    nq = s // tq
    t = bl * s
    n_carried = len(traffic.pieces) if traffic else 0

    def body(*refs):
        sems = refs[len(refs) - 3:] if traffic else ()
        if pe:
            q_ref, qpe_ref, kv_ref, kpe_ref = refs[:4]
            n_in = 4
            q_at = lambda r0, r1: q_ref[r0:r1, :]
            v_at = lambda r0, r1: kv_ref[r0:r1, LANES:]
            kcat = refs[len(refs) - 1 - len(sems)]
            kcat[:, :LANES] = kv_ref[:, :LANES]
            kcat[:, LANES:] = kpe_ref[...]
            k_at = lambda r0, r1: kcat[r0:r1, :]
        else:
            qkv_ref = refs[0]
            n_in = 1
            q_at = lambda r0, r1: qkv_ref[r0:r1, :LANES]
            k_at = lambda r0, r1: qkv_ref[r0:r1, LANES:2 * LANES]
            v_at = lambda r0, r1: qkv_ref[r0:r1, 2 * LANES:]
        if has_bias:
            c_ref = refs[n_in]
            n_in += 1
        carried_in = refs[n_in:n_in + n_carried]
        n_in += n_carried
        o_ref, lse_ref = refs[n_in:n_in + 2]
        if traffic:
            carried_out = refs[n_in + 2]

            @pl.when(_first_step())
            def _():
                traffic.start(carried_in, carried_out, *sems)

        pair = pl.program_id(1)
        causal = lax.broadcasted_iota(jnp.int32, (tq, tq), 1) <= lax.broadcasted_iota(jnp.int32, (tq, tq), 0)
        o_ref[...] = jnp.zeros_like(o_ref)

        masks = [_lane_masks(pair, h, pe) for h in range(2)]

        def logits(i):
            r0, r1 = i * tq, (i + 1) * tq
            out = []
            for h in range(2):
                in_head, in_rope = masks[h]
                qm = _keep(in_head, q_at(r0, r1))
                if pe:
                    qm = jnp.concatenate([qm, _keep(in_rope, qpe_ref[r0:r1, :])], axis=1)
                if exact_scale:
                    qm = qm * scale
                spans = []
                for k0, k1 in [(r0, r1)] + ([(0, r0)] if i else []):
                    sc = _dot(qm, k_at(k0, k1), "nt")
                    if not exact_scale:
                        sc = sc * scale
                    if has_bias:
                        sc = sc - c_ref[h, :, k0:k1]
                    spans.append((k0, k1, jnp.where(causal, sc, MASKED) if k0 == r0 else sc))
                out.append(spans)
            return out

        def softmax(per_head):
            out = []
            for spans in per_head:
                m = None
                for _, _, sc in spans:
                    top = jnp.max(sc, axis=1, keepdims=True)
                    m = top if m is None else jnp.maximum(m, top)
                probs = [(k0, k1, jnp.exp(sc - m)) for k0, k1, sc in spans]
                l = sum(jnp.sum(p, axis=1, keepdims=True) for _, _, p in probs)
                out.append((m, l, probs))
            return out

        def weigh(i, per_head):
            r0, r1 = i * tq, (i + 1) * tq
            for h, (m, l, probs) in enumerate(per_head):
                acc = sum(_dot(p, v_at(k0, k1), "nn") for k0, k1, p in probs)
                o_ref[r0:r1, :] = jnp.where(masks[h][0], acc / l, o_ref[r0:r1, :])
                lse = _to_row(m + jnp.log(l))
                lse_ref[h, :, r0:r1] = lse + c_ref[h, :, r0:r1] if has_bias else lse

        ahead = logits(0)
        for i in range(nq):
            solved = softmax(ahead)
            if i + 1 < nq:
                ahead = logits(i + 1)
            weigh(i, solved)

        if traffic:
            @pl.when(_last_step(bl, PAIRS))
            def _():
                traffic.wait(carried_out, *sems)

    seq = lambda w, col: pl.BlockSpec((s, w), col)
    if pe:
        in_specs = [seq(LANES, lambda b, p: (b, p)), seq(LANES, lambda b, p: (b, PAIRS + p // 2)),
                    seq(2 * LANES, lambda b, p: (b, p)), seq(LANES, lambda b, p: (b, 0))]
        args = [ops[0], ops[0], ops[1], ops[2]]
        scratch = [pltpu.VMEM((s, 2 * LANES), BF16)]
    else:
        in_specs = [seq(3 * LANES, lambda b, p: (b, p))]
        args = [ops[0]]
        scratch = []
    per_head_row = pl.BlockSpec((2, 1, s), lambda b, p: (b * PAIRS + p, 0, 0))
    if has_bias:
        in_specs.append(per_head_row)
        args.append(bias)
    out_specs = [seq(LANES, lambda b, p: (b, p)), per_head_row]
    out_shape = [jax.ShapeDtypeStruct((t, HEADS * HEAD_DIM), F32), jax.ShapeDtypeStruct((bl * HEADS, 1, s), F32)]
    if traffic:
        in_specs += traffic.in_specs
        args += traffic.pieces
        out_specs.append(traffic.out_spec)
        out_shape.append(traffic.out_shape)
        scratch += traffic.scratch
    return pl.pallas_call(
        body, name=name, grid=(bl, PAIRS), in_specs=in_specs, out_specs=tuple(out_specs), out_shape=tuple(out_shape),
        scratch_shapes=scratch, compiler_params=_params(*(("arbitrary", "arbitrary") if traffic else ("parallel", "parallel"))),
    )(*args)


def _attn_bwd(ops, do, lse, delta, bias, scale, bl, s, tq, name, traffic=None):
    pe = len(ops) == 3
    has_bias = bias is not None
    exact_scale = math.frexp(scale)[0] == 0.5
    nq = s // tq
    t = bl * s
    width = 2 * LANES if pe else LANES
    n_carried = len(traffic.pieces) if traffic else 0

    def body(*refs):
        if pe:
            q_ref, qpe_ref, kv_ref, kpe_ref = refs[:4]
            n_in = 4
            k_at = lambda r0, r1: kv_ref[r0:r1, :LANES]
            v_at = lambda r0, r1: kv_ref[r0:r1, LANES:]
        else:
            qkv_ref = refs[0]
            n_in = 1
            k_at = lambda r0, r1: qkv_ref[r0:r1, LANES:2 * LANES]
            v_at = lambda r0, r1: qkv_ref[r0:r1, 2 * LANES:]
        do_ref, lse_ref, dl_ref = refs[n_in:n_in + 3]
        n_in += 3
        if has_bias:
            c_ref = refs[n_in]
            n_in += 1
        carried_in = refs[n_in:n_in + n_carried]
        rest = refs[n_in + n_carried:]
        if traffic:
            rest, sems = rest[:-3], rest[-3:]
            carried_out = rest[4 if pe else 2]
            rest = rest[:4 if pe else 2] + rest[(4 if pe else 2) + 1:]

            @pl.when(_first_step())
            def _():
                traffic.start(carried_in, carried_out, *sems)

        if pe:
            dqn_ref, dkv_ref, dqpe_ref, dkpe_ref, dq_acc, qcat = rest
            qcat[:, :LANES] = q_ref[...]
            qcat[:, LANES:] = qpe_ref[...]
            q_at = lambda r0, r1: qcat[r0:r1, :]
            dkv_ref[...] = jnp.zeros_like(dkv_ref)
        else:
            dqkv_ref, dc_ref, dq_acc = rest
            q_at = lambda r0, r1: qkv_ref[r0:r1, :LANES]
            dqkv_ref[...] = jnp.zeros_like(dqkv_ref)
            dc_ref[...] = jnp.zeros_like(dc_ref)
        pair = pl.program_id(1)
        dq_acc[...] = jnp.zeros_like(dq_acc)
        causal = lax.broadcasted_iota(jnp.int32, (tq, tq), 1) >= lax.broadcasted_iota(jnp.int32, (tq, tq), 0)
        if pe:
            @pl.when(pair == 0)
            def _():
                dkpe_ref[...] = jnp.zeros_like(dkpe_ref)

            @pl.when(pair % 2 == 0)
            def _():
                dqpe_ref[...] = jnp.zeros_like(dqpe_ref)

        masks = [_lane_masks(pair, h, pe) for h in range(2)]

        def logits(j):
            r0, r1 = j * tq, (j + 1) * tq
            units = []
            for h in range(2):
                in_head, in_rope = masks[h]
                kt = _keep(in_head, k_at(r0, r1))
                if pe:
                    kt = jnp.concatenate([kt, _keep(in_rope, kpe_ref[r0:r1, :])], axis=1)
                if exact_scale:
                    kt = kt * scale
                vt = _keep(in_head, v_at(r0, r1))
                ck = _to_col(c_ref[h, :, r0:r1]) if has_bias else None
                for q0, q1, diagonal in [(r0, r1, True)] + ([(r1, s, False)] if r1 < s else []):
                    qq, dd = q_at(q0, q1), do_ref[q0:q1, :]
                    st = _dot(kt, qq, "nt")
                    if not exact_scale:
                        st = st * scale
                    shift = lse_ref[h, :, q0:q1]
                    if has_bias:
                        shift = shift - c_ref[h, :, q0:q1]
                        st = st - ck
                    st = st - shift
                    if diagonal:
                        st = jnp.where(causal, st, MASKED)
                    units.append((h, q0, q1, kt, qq, dd, st, _dot(vt, dd, "nt")))
            return units

        def softmax_bwd(units):
            solved = []
            for h, q0, q1, kt, qq, dd, st, dpt in units:
                pt = jnp.exp(st)
                dst = pt * (dpt - dl_ref[h, :, q0:q1])
                solved.append((h, q0, q1, kt, qq, dd, pt, dst, (dst if exact_scale else dst * scale).astype(BF16)))
            return solved

        def products(j, solved):
            r0, r1 = j * tq, (j + 1) * tq
            dv_of, dk_of, cs_of = [None, None], [None, None], [None, None]
            add = lambda old, new: new if old is None else old + new
            for h, q0, q1, kt, qq, dd, pt, dst, dsb in solved:
                dq_acc[q0:q1, :] += _dot(dsb, kt, "tn")
                dv_of[h] = add(dv_of[h], _dot(pt, dd, "nn"))
                dk_of[h] = add(dk_of[h], _dot(dsb, qq, "nn"))
                if has_bias:
                    dc_ref[h, :, q0:q1] += jnp.sum(dst, axis=0, keepdims=True)
                    cs_of[h] = add(cs_of[h], jnp.sum(dst, axis=1, keepdims=True))
            for h in range(2):
                (in_head, in_rope), dv_c, dk_c, cs = masks[h], dv_of[h], dk_of[h], cs_of[h]
                if exact_scale:
                    dk_c = dk_c * scale
                if pe:
                    dkv_ref[r0:r1, :LANES] = jnp.where(in_head, dk_c[:, :LANES].astype(BF16), dkv_ref[r0:r1, :LANES])
                    dkv_ref[r0:r1, LANES:] = jnp.where(in_head, dv_c.astype(BF16), dkv_ref[r0:r1, LANES:])
                    dkpe_ref[r0:r1, :] += _keep(in_rope, dk_c[:, LANES:])
                else:
                    dqkv_ref[r0:r1, LANES:2 * LANES] = jnp.where(in_head, dk_c.astype(BF16), dqkv_ref[r0:r1, LANES:2 * LANES])
                    dqkv_ref[r0:r1, 2 * LANES:] = jnp.where(in_head, dv_c.astype(BF16), dqkv_ref[r0:r1, 2 * LANES:])
                    dc_ref[h, :, r0:r1] -= _to_row(cs)

        units = logits(0)
        for j in range(nq):
            solved = softmax_bwd(units)
            if j + 1 < nq:
                units = logits(j + 1)
            products(j, solved)

        if pe:
            dqn_ref[...] = dq_acc[:, :LANES].astype(BF16)
            dqpe_ref[...] += dq_acc[:, LANES:]
        else:
            dqkv_ref[:, :LANES] = dq_acc[...].astype(BF16)
        if traffic:
            @pl.when(_last_step(bl, PAIRS))
            def _():
                traffic.wait(carried_out, *sems)

    seq = lambda w, col: pl.BlockSpec((s, w), col)
    per_head_row = pl.BlockSpec((2, 1, s), lambda b, p: (b * PAIRS + p, 0, 0))
    if pe:
        in_specs = [seq(LANES, lambda b, p: (b, p)), seq(LANES, lambda b, p: (b, PAIRS + p // 2)),
                    seq(2 * LANES, lambda b, p: (b, p)), seq(LANES, lambda b, p: (b, 0))]
        args = [ops[0], ops[0], ops[1], ops[2]]
    else:
        in_specs = [seq(3 * LANES, lambda b, p: (b, p))]
        args = [ops[0]]
    in_specs += [seq(LANES, lambda b, p: (b, p)), per_head_row, per_head_row]
    args += [do, lse, delta]
    if has_bias:
        in_specs.append(per_head_row)
        args.append(bias)
    scratch = [pltpu.VMEM((s, width), F32)]
    if pe:
        out_specs = (seq(LANES, lambda b, p: (b, p)), seq(2 * LANES, lambda b, p: (b, p)),
                     seq(LANES, lambda b, p: (b, p // 2)), seq(LANES, lambda b, p: (b, 0)))
        out_shape = (jax.ShapeDtypeStruct((t, PAIRS * LANES), BF16), jax.ShapeDtypeStruct((t, PAIRS * 2 * LANES), BF16),
                     jax.ShapeDtypeStruct((t, 2 * LANES), F32), jax.ShapeDtypeStruct((t, LANES), F32))
        scratch.append(pltpu.VMEM((s, 2 * LANES), BF16))
    else:
        out_specs = (seq(3 * LANES, lambda b, p: (b, p)), per_head_row)
        out_shape = (jax.ShapeDtypeStruct((t, PAIRS * 3 * LANES), BF16), jax.ShapeDtypeStruct((bl * HEADS, 1, s), F32))
    if traffic:
        in_specs += traffic.in_specs
        args += traffic.pieces
        out_specs += (traffic.out_spec,)
        out_shape += (traffic.out_shape,)
        scratch += traffic.scratch
    return pl.pallas_call(
        body, name=name, grid=(bl, PAIRS), in_specs=in_specs, out_specs=out_specs, out_shape=out_shape,
        scratch_shapes=scratch, compiler_params=_params("arbitrary" if traffic else "parallel", "arbitrary"),
    )(*args)


def _my_place():
    return lax.axis_index("x"), lax.axis_index("y"), lax.axis_index("c")


def _flip(p, bit):
    return 1 - p if bit else p


def _relative(x, y, c, k):
    return _flip(x, k & 4), _flip(y, k & 2), _flip(c, k & 1)


def _linear(x, y, c):
    return 4 * x + 2 * y + c


class _Traffic:
    def __init__(self, kind, pieces):
        self.kind, self.pieces = kind, list(pieces)
        self.rows = [p.shape[-2] for p in self.pieces]
        self.starts = [sum(self.rows[:i]) for i in range(len(self.rows))]
        anywhere = pl.BlockSpec(memory_space=pl.ANY)
        self.in_specs = [anywhere] * len(self.pieces)
        self.out_spec = anywhere
        self.out_shape = jax.ShapeDtypeStruct((N_DEV, sum(self.rows), self.pieces[0].shape[-1]), self.pieces[0].dtype)
        self.scratch = [pltpu.SemaphoreType.DMA((7,)), pltpu.SemaphoreType.DMA((7,)), pltpu.SemaphoreType.DMA(())]

    def start(self, p_refs, out_ref, send_sems, recv_sems, local_sem):
        x, y, c = _my_place()
        me = _linear(x, y, c)
        mine = lambda i, dev: p_refs[i] if self.kind == "spread" else p_refs[i].at[dev]
        landing = lambda i: out_ref.at[me, pl.ds(self.starts[i], self.rows[i])]
        for i in range(len(p_refs)):
            pltpu.make_async_copy(mine(i, me), landing(i), local_sem).start()
        for k in range(1, N_DEV):
            peer = _relative(x, y, c, k)
            for i in range(len(p_refs)):
                pltpu.make_async_remote_copy(
                    src_ref=mine(i, _linear(*peer)), dst_ref=landing(i),
                    send_sem=send_sems.at[k - 1], recv_sem=recv_sems.at[k - 1], device_id=peer, device_id_type=MESH).start()

    def wait(self, out_ref, send_sems, recv_sems, local_sem):
        x, y, c = _my_place()
        whole = out_ref.at[_linear(x, y, c)]
        for k in range(1, N_DEV):
            both = pltpu.make_async_remote_copy(
                src_ref=whole, dst_ref=whole, send_sem=send_sems.at[k - 1], recv_sem=recv_sems.at[k - 1],
                device_id=_relative(x, y, c, k), device_id_type=MESH)
            both.wait_recv()
            both.wait_send()
        pltpu.make_async_copy(whole, whole, local_sem).wait()


class _Relay(_Traffic):
    def __init__(self, piece):
        super().__init__("spread", [piece])

    @staticmethod
    def _chips(x, y):
        return [(1 - x, y), (x, 1 - y), (1 - x, 1 - y)]

    @staticmethod
    def _copy(k, block, to, out_ref, send_sems, recv_sems, src=None):
        slot = out_ref.at[_linear(*block)]
        return pltpu.make_async_remote_copy(src_ref=slot if src is None else src, dst_ref=slot, send_sem=send_sems.at[k],
                                            recv_sem=recv_sems.at[k], device_id=to, device_id_type=MESH)

    def start(self, p_refs, out_ref, send_sems, recv_sems, local_sem):
        x, y, c = _my_place()
        me, sems = (x, y, c), (out_ref, send_sems, recv_sems)
        pltpu.make_async_copy(p_refs[0], out_ref.at[_linear(*me)], local_sem).start()
        self._copy(0, me, (x, y, 1 - c), *sems, src=p_refs[0]).start()
        for j, chip in enumerate(self._chips(x, y)):
            self._copy(1 + j, me, (*chip, c), *sems, src=p_refs[0]).start()

    def relay(self, out_ref, send_sems, recv_sems, local_sem):
        x, y, c = _my_place()
        sems = (out_ref, send_sems, recv_sems)
        for j, chip in enumerate(self._chips(x, y)):
            self._copy(1 + j, (*chip, c), (x, y, c), *sems).wait_recv()
            self._copy(4 + j, (*chip, c), (x, y, 1 - c), *sems).start()

    def wait(self, out_ref, send_sems, recv_sems, local_sem):
        x, y, c = _my_place()
        me, sems = (x, y, c), (out_ref, send_sems, recv_sems)
        self._copy(0, (x, y, 1 - c), me, *sems).wait_recv()
        for j, chip in enumerate(self._chips(x, y)):
            self._copy(4 + j, (*chip, 1 - c), me, *sems).wait_recv()
        for k in range(N_DEV - 1):
            self._copy(k, me, (x, y, 1 - c), *sems).wait_send()
        whole = out_ref.at[_linear(*me)]
        pltpu.make_async_copy(whole, whole, local_sem).wait()


def _sum_blocks(parts, name):
    n, r, cdim = parts.shape
    tr = _tile(r, 640, 16)

    def body(p_ref, o_ref):
        acc = p_ref[0].astype(F32)
        for d in range(1, n):
            acc = acc + p_ref[d].astype(F32)
        o_ref[...] = acc

    return pl.pallas_call(
        body, name=name, grid=(r // tr,), in_specs=[pl.BlockSpec((n, tr, cdim), lambda i: (0, i, 0))],
        out_specs=pl.BlockSpec((tr, cdim), lambda i: (i, 0)), out_shape=jax.ShapeDtypeStruct((r, cdim), F32),
        compiler_params=_params("parallel"),
    )(parts)


def _adamw_math(w, g, m, v):
    m = ADAM_B1 * m + (1.0 - ADAM_B1) * g
    v = ADAM_B2 * v + (1.0 - ADAM_B2) * (g * g)
    m_hat = m / (1.0 - ADAM_B1 ** ADAM_STEP)
    v_hat = v / (1.0 - ADAM_B2 ** ADAM_STEP)
    delta = -ADAM_LR * (m_hat / (jnp.sqrt(v_hat) + ADAM_EPS) + ADAM_WD * w)
    return delta, m, v


def _adamw(w, g, m, v, name):
    def body(w_ref, g_ref, m_ref, v_ref, d_ref, nm_ref, nv_ref):
        d_ref[...], nm_ref[...], nv_ref[...] = _adamw_math(w_ref[...], g_ref[...], m_ref[...], v_ref[...])

    out = jax.ShapeDtypeStruct(w.shape, F32)
    return pl.pallas_call(body, name=name, out_shape=(out, out, out),
                          compiler_params=pltpu.CompilerParams(vmem_limit_bytes=VMEM_LIMIT))(w, g, m, v)


def _small_all_reduce_adamw(parts, loss_part, ws, ms, vs, name):
    sizes = [p.shape[1] for p in parts] + [1]
    spots = [sum(-(-n // LANES) * LANES for n in sizes[:i]) for i in range(len(sizes))]
    width = spots[-1] + LANES
    k = len(parts)

    def reduce_body(*refs):
        p_refs, tot_ref, rows, send_sems, recv_sems = refs[:k + 1], *refs[k + 1:]
        x, y, c = _my_place()
        me = _linear(x, y, c)
        rows[me] = jnp.zeros((1, width), F32)
        for i in range(k + 1):
            rows[me, :, spots[i]:spots[i] + sizes[i]] = p_refs[i][...]
        copies = []
        for rel in range(1, N_DEV):
            copies.append(pltpu.make_async_remote_copy(
                src_ref=rows.at[me], dst_ref=rows.at[me], send_sem=send_sems.at[rel - 1], recv_sem=recv_sems.at[rel - 1],
                device_id=_relative(x, y, c, rel), device_id_type=MESH))
        for cp in copies:
            cp.start()
        for cp in copies:
            cp.wait_recv()
        for cp in copies:
            cp.wait_send()
        total = rows[0]
        for d in range(1, N_DEV):
            total = total + rows[d]
        tot_ref[...] = total

    total = pl.pallas_call(
        reduce_body, name=name, out_shape=jax.ShapeDtypeStruct((1, width), F32),
        scratch_shapes=[pltpu.VMEM((N_DEV, 1, width), F32), pltpu.SemaphoreType.DMA((7,)), pltpu.SemaphoreType.DMA((7,))],
    )(*parts, loss_part)

    def adamw_body(*refs):
        tot_ref, w_refs, m_refs, v_refs, outs = refs[0], refs[1:k + 1], refs[k + 1:2 * k + 1], refs[2 * k + 1:3 * k + 1], refs[3 * k + 1:]
        for i in range(k):
            g = tot_ref[:, spots[i]:spots[i] + sizes[i]]
            outs[4 * i][...] = g
            outs[4 * i + 1][...], outs[4 * i + 2][...], outs[4 * i + 3][...] = _adamw_math(w_refs[i][...], g, m_refs[i][...], v_refs[i][...])
        outs[4 * k][...] = tot_ref[:, spots[k]:spots[k] + 1]

    out_shape = [jax.ShapeDtypeStruct((1, n), F32) for n in sizes[:k] for _ in range(4)] + [jax.ShapeDtypeStruct((1, 1), F32)]
    res = pl.pallas_call(adamw_body, name=name + "_adamw", out_shape=tuple(out_shape))(total, *ws, *ms, *vs)
    return [res[4 * i:4 * i + 4] for i in range(k)], res[4 * k]


def _pad_rows(a, rows):
    return jnp.pad(a, ((0, rows - a.shape[0]), (0, 0)))


def kernel(x, positions, norm_mix_g, w_in, b_fgate, q_norm_g, w_uq, kv_norm_g, w_ukv, fox_out_g, mla_out_g, w_o, norm_ffn_g, w_gate, w_up, w_down, final_norm_g, loss_target, m_norm_mix_g, m_w_in, m_b_fgate, m_q_norm_g, m_w_uq, m_kv_norm_g, m_w_ukv, m_fox_out_g, m_mla_out_g, m_w_o, m_norm_ffn_g, m_w_gate, m_w_up, m_w_down, m_final_norm_g, v_norm_mix_g, v_w_in, v_b_fgate, v_q_norm_g, v_w_uq, v_kv_norm_g, v_w_ukv, v_fox_out_g, v_mla_out_g, v_w_o, v_norm_ffn_g, v_w_gate, v_w_up, v_w_down, v_final_norm_g):
    bl, s, d = x.shape
    t = bl * s
    bh = bl * HEADS
    tq = _tile(s, 256)
    grp = s // LANES
    fw = HEADS * HEAD_DIM
    q_rank, kv_rank = w_uq.shape[1], w_ukv.shape[1]
    in_cols = w_in.shape[2]
    n_in = N_DEV * in_cols
    ff = N_DEV * w_gate.shape[2]
    half = MLA_ROPE // 2
    o_kvlat, o_krope, o_flogit = q_rank, q_rank + kv_rank, q_rank + kv_rank + LANES
    b_cols = -(-(o_flogit + HEADS) // LANES) * LANES

    tr = lambda w: jnp.transpose(w[0])
    in_rows = -(-in_cols // 16) * 16
    uq_rows = w_uq.shape[2] * q_rank // d
    ukv_rows = w_ukv.shape[2] * kv_rank // d
    pieces = [_pad_rows(tr(w_in), in_rows), _pad_rows(tr(w_uq).reshape(uq_rows, d), -(-uq_rows // 16) * 16),
              tr(w_ukv).reshape(ukv_rows, d), w_o[0], tr(w_gate), tr(w_up), w_down[0]]
    pieces = [p.astype(BF16) for p in pieces]
    offs = [0]
    for p in pieces:
        offs.append(offs[-1] + p.shape[0])
    legs = [(0, 1), (1, 5), (5, 7)]
    gathered = {}

    def full(i, rows):
        leg = next(n for n, (lo, hi) in enumerate(legs) if lo <= i < hi)
        base = offs[legs[leg][0]]
        return gathered[leg][:, offs[i] - base:offs[i] - base + rows]

    x2d = x.reshape(t, d)
    h1, gathered[0] = _rmsnorm(x2d, 0, d, norm_mix_g, BF16, "norm_mix", traffic=_Relay(pieces[0]))

    w_in_t = full(0, in_cols).reshape(n_in, d)
    n_qkv = 3 * fw
    w_in_a = w_in_t[:n_qkv].reshape(3, PAIRS, LANES, d).transpose(1, 0, 2, 3).reshape(n_qkv, d)
    lat0, rope0 = n_qkv + HEADS, n_qkv + HEADS + q_rank + kv_rank
    k_rep = jnp.broadcast_to(w_in_t[rope0:].reshape(2, 1, half, d), (2, 4, half, d)).reshape(LANES, d)
    w_in_b = jnp.concatenate([w_in_t[lat0:rope0], k_rep, w_in_t[n_qkv:lat0],
                              jnp.zeros((b_cols - o_flogit - HEADS, d), BF16)], axis=0)

    def per_head_rows(a):
        return a.reshape(bl, s, HEADS).transpose(0, 2, 1).reshape(bh, 1, s)

    proj_a = _matmul(h1, w_in_a, "nt", BF16, "proj_fox", tm=1024, tn=6 * LANES)
    proj_b = _matmul(h1, w_in_b, "nt", F32, "proj_mla", tm=1024, tn=b_cols)

    z = proj_b[:, o_flogit:o_flogit + HEADS].reshape(bl, s, HEADS).transpose(0, 2, 1).reshape(bh * grp, LANES)
    bcol = jnp.broadcast_to(b_fgate.reshape(1, HEADS, 1), (bl, HEADS, grp)).reshape(bh * grp, 1)
    c = _fgate(z, bcol, grp, "forget_gate")
    c_bias = c.reshape(bh, 1, s)
    fox_o, fox_lse, gathered[1] = _attn_fwd((proj_a,), c_bias, HEAD_DIM ** -0.5, bl, s, tq, "fox_attention",
                                            traffic=_Traffic("spread", pieces[legs[1][0]:legs[1][1]]))
    w_uq_h = full(1, uq_rows).reshape(HEADS, MLA_QK, q_rank)
    w_uq_pe = jnp.concatenate([w_uq_h[:, HEAD_DIM:HEAD_DIM + half].reshape(2, 1, 4 * half, q_rank),
                               w_uq_h[:, HEAD_DIM + half:].reshape(2, 1, 4 * half, q_rank)], axis=1).reshape(2 * LANES, q_rank)
    w_uq_p = jnp.concatenate([w_uq_h[:, :HEAD_DIM].reshape(fw, q_rank), w_uq_pe], axis=0)
    w_ukv_p = full(2, ukv_rows).reshape(PAIRS, 2, 2, HEAD_DIM, kv_rank).transpose(0, 2, 1, 3, 4).reshape(2 * fw, kv_rank)
    w_o_f = full(3, w_o.shape[1]).reshape(-1, d)
    w_gate_t = full(4, ff // N_DEV).reshape(ff, d)

    inv_freq = ROPE_THETA ** (-jnp.arange(0, MLA_ROPE, 2, dtype=F32) / MLA_ROPE)
    ang = positions.astype(F32).reshape(t, 1) * inv_freq[None, :]
    rope_cos, rope_sin = jnp.cos(ang), jnp.sin(ang)
    qn, kvn, q_all, kv_all, kpe = _mla_prep(proj_b, q_rank, kv_rank, q_norm_g, kv_norm_g, w_uq_p, w_ukv_p, fw,
                                            rope_cos, rope_sin, "mla_prep")
    mla_ops = (q_all, kv_all, kpe)
    mla_o, mla_lse, gathered[2] = _attn_fwd(mla_ops, None, MLA_QK ** -0.5, bl, s, tq, "mla_attention",
                                            traffic=_Traffic("spread", pieces[legs[2][0]:legs[2][1]]))
    w_up_t, w_down_f = full(5, ff // N_DEV).reshape(ff, d), full(6, ff // N_DEV).reshape(ff, d)

    both = [(d, F32), (d, BF16)]
    cat, x1, h2 = _rows_matmul([(None, w_o_f, "nn")], [fox_o, mla_o, x2d], [fox_out_g, mla_out_g, norm_ffn_g], _residual_norm,
                               [(2 * fw, BF16)] + both, [], "norm_out_proj_out_norm_ffn", prologue=_out_norm)
    act_by_gate, act_by_up, act = _ffn_up(h2, w_gate_t, w_up_t, "ffn_gate_up")
    dx2, dx2_b, dg_final, loss_part = _rows_matmul(
        [(act, w_down_f, "nn")], [x1, loss_target.reshape(t, d)], [final_norm_g.reshape(1, d)], _residual_loss_bwd,
        both, [d, 1], "ffn_down_final_norm_loss")

    d_gate, d_up = _ffn_down_bwd(dx2_b, w_down_f, act_by_gate, act_by_up, "d_ffn_down")
    dw_down = _matmul(act, dx2_b, "tn", BF16, "dw_down", tm=ff // 2, tn=d, tk=2048)
    dw_gate = _matmul(d_gate, h2, "tn", BF16, "dw_gate", tm=ff // 2, tn=d, tk=2048)
    dw_up = _matmul(d_up, h2, "tn", BF16, "dw_up", tm=ff // 2, tn=d, tk=2048)
    dx1, dx1_b, dg_ffn = _rows_matmul([(d_gate, w_gate_t, "nn"), (d_up, w_up_t, "nn")], [x1, dx2], [norm_ffn_g],
                                      _norm_bwd_residual, both, [d], "d_ffn_gate_up_norm_ffn", tm=256)
    dw_o = _matmul(cat, dx1_b, "tn", BF16, "dw_o", tn=d, tk=2048)
    d_fox_o, d_mla_o, fox_delta, mla_delta, dg_fox, dg_mla = _rows_matmul(
        [(dx1_b, w_o_f, "nt")], [fox_o, mla_o], [fox_out_g, mla_out_g], _out_norm_bwd,
        [(fw, BF16), (fw, BF16), (HEADS, F32), (HEADS, F32)], [fw, fw], "d_proj_out_norm_out")

    per_dev = lambda a: a.reshape(N_DEV, -1, d)
    late_grads = [per_dev(dw_o), per_dev(dw_gate), per_dev(dw_up), per_dev(dw_down)]
    dproj_a, dc, g_late = _attn_bwd((proj_a,), d_fox_o, fox_lse, per_head_rows(fox_delta),
                                    c_bias, HEAD_DIM ** -0.5, bl, s, tq, "d_fox_attention", traffic=_Traffic("swap", late_grads))
    dz, db_fgate = _fgate_bwd(z, bcol, dc.reshape(bh * grp, LANES), grp, "d_forget_gate")
    d_flogit = dz.reshape(bl, HEADS, s).transpose(0, 2, 1).reshape(t, HEADS)

    dq_nope, dkv_all, dq_pe, dk_pe = _attn_bwd(mla_ops, d_mla_o, mla_lse, per_head_rows(mla_delta),
                                               None, MLA_QK ** -0.5, bl, s, tq, "d_mla_attention")
    d_tail = jnp.pad(d_flogit, ((0, 0), (0, b_cols - o_flogit - HEADS)))
    dproj_b, dq_rot, dg_q, dg_kv = _mla_prep_bwd(dq_nope, dq_pe, dkv_all, dk_pe, d_tail, proj_b, q_rank, kv_rank,
                                                 q_norm_g, kv_norm_g, w_uq_p, w_ukv_p, rope_cos, rope_sin, "d_mla_prep")
    dw_uq_nope = _matmul(dq_nope, qn, "tn", BF16, "dw_uq_nope", tn=q_rank, tk=1024)
    dw_uq_pe = _matmul(dq_rot, qn, "tn", BF16, "dw_uq_rope", tn=q_rank, tk=1024)
    dw_ukv_p = _matmul(dkv_all, kvn, "tn", BF16, "dw_ukv", tn=kv_rank, tk=1024)
    dw_in_a = _matmul(dproj_a, h1, "tn", BF16, "dw_in_fox", tm=6 * LANES, tn=d, tk=2048)
    dw_in_b = _matmul(dproj_b, h1, "tn", F32, "dw_in_mla", tm=b_cols, tn=d, tk=1024)

    dw_krope = dw_in_b[o_krope:o_flogit].reshape(2, 4, half, d).sum(axis=1).reshape(MLA_ROPE, d)
    dw_in_t = jnp.concatenate([dw_in_a.reshape(PAIRS, 3, LANES, d).transpose(1, 0, 2, 3).reshape(n_qkv, d),
                               dw_in_b[o_flogit:o_flogit + HEADS].astype(BF16), dw_in_b[:o_krope].astype(BF16),
                               dw_krope.astype(BF16)], axis=0)
    pad_dev = lambda a, rows: jnp.pad(a, ((0, 0), (0, rows - a.shape[1]), (0, 0)))
    dw_uq_pe5 = dw_uq_pe.reshape(2, 2, 4, half, q_rank)
    dw_uq_h = jnp.concatenate([dw_uq_nope.reshape(HEADS, HEAD_DIM, q_rank), dw_uq_pe5[:, 0].reshape(HEADS, half, q_rank),
                               dw_uq_pe5[:, 1].reshape(HEADS, half, q_rank)], axis=1)
    dw_ukv_h = dw_ukv_p.reshape(PAIRS, 2, 2, HEAD_DIM, kv_rank).transpose(0, 2, 1, 3, 4).reshape(HEADS, 2 * HEAD_DIM, kv_rank)
    n_last = 3
    last_grads = [pad_dev(per_dev(dw_in_t), pieces[0].shape[0]), pad_dev(per_dev(dw_uq_h), pieces[1].shape[0]), per_dev(dw_ukv_h)]
    grad_x, dg_mix, g_last = _rows_matmul([(dproj_a, w_in_a, "nn"), (dproj_b, w_in_b, "nn")], [x2d, dx1], [norm_mix_g],
                                          _norm_bwd_residual, [(d, F32)], [d], "d_proj_in_norm_mix",
                                          traffic=_Traffic("swap", last_grads))
    g_last = _sum_blocks(g_last, "sum_last_grads")
    g_late = _sum_blocks(g_late, "sum_late_grads")

    def mine(i, rows):
        src, base = (g_last, 0) if i < n_last else (g_late, offs[n_last])
        return src[offs[i] - base:offs[i] - base + rows]

    big = [
        ("w_in", w_in, m_w_in, v_w_in, mine(0, in_cols), True),
        ("w_uq", w_uq, m_w_uq, v_w_uq, mine(1, uq_rows).reshape(-1, q_rank), True),
        ("w_ukv", w_ukv, m_w_ukv, v_w_ukv, mine(2, ukv_rows).reshape(-1, kv_rank), True),
        ("w_o", w_o, m_w_o, v_w_o, mine(3, w_o.shape[1]), False),
        ("w_gate", w_gate, m_w_gate, v_w_gate, mine(4, ff // N_DEV), True),
        ("w_up", w_up, m_w_up, v_w_up, mine(5, ff // N_DEV), True),
        ("w_down", w_down, m_w_down, v_w_down, mine(6, ff // N_DEV), False),
    ]
    out = {}
    for nm, w, m, v, g, transposed in big:
        lay = (lambda a: a[0].T) if transposed else (lambda a: a[0])
        back = (lambda a: a.T[None]) if transposed else (lambda a: a[None])
        dl, new_m, new_v = _adamw(lay(w), g, lay(m), lay(v), "adamw_" + nm)
        out[nm] = (back(g), back(dl), back(new_m), back(new_v))

    smalls = [("norm_mix_g", norm_mix_g, m_norm_mix_g, v_norm_mix_g, dg_mix),
              ("b_fgate", b_fgate, m_b_fgate, v_b_fgate, db_fgate.reshape(1, HEADS)),
              ("q_norm_g", q_norm_g, m_q_norm_g, v_q_norm_g, dg_q),
              ("kv_norm_g", kv_norm_g, m_kv_norm_g, v_kv_norm_g, dg_kv),
              ("fox_out_g", fox_out_g, m_fox_out_g, v_fox_out_g, dg_fox),
              ("mla_out_g", mla_out_g, m_mla_out_g, v_mla_out_g, dg_mla),
              ("norm_ffn_g", norm_ffn_g, m_norm_ffn_g, v_norm_ffn_g, dg_ffn),
              ("final_norm_g", final_norm_g, m_final_norm_g, v_final_norm_g, dg_final)]
    flat = lambda a: a.reshape(1, -1)
    results, loss = _small_all_reduce_adamw([e[4] for e in smalls], loss_part, [flat(e[1]) for e in smalls],
                                            [flat(e[2]) for e in smalls], [flat(e[3]) for e in smalls], "reduce_small_adamw")
    for (nm, w, _, _, _), res in zip(smalls, results):
        out[nm] = tuple(a.reshape(w.shape) for a in res)
    loss = loss[0, 0]

    order = ["norm_mix_g", "w_in", "b_fgate", "q_norm_g", "w_uq", "kv_norm_g", "w_ukv", "fox_out_g", "mla_out_g", "w_o",
             "norm_ffn_g", "w_gate", "w_up", "w_down", "final_norm_g"]
    return (loss, grad_x.reshape(bl, s, d), *[out[n][0] for n in order], *[out[n][1] for n in order],
            *[out[n][2] for n in order], *[out[n][3] for n in order])
```

```python
import math

import jax
import jax.numpy as jnp
from jax import lax
from jax.experimental import pallas as pl
from jax.experimental.pallas import tpu as pltpu

F32 = jnp.float32
BF16 = jnp.bfloat16
MESH = pl.DeviceIdType.MESH

N_DEV = 8
HEADS = 8
HEAD_DIM = 64
PAIRS = HEADS // 2
MLA_ROPE = 32
MLA_QK = HEAD_DIM + MLA_ROPE
ROPE_THETA = 10000.0
NORM_EPS = 1e-6
ADAM_LR, ADAM_B1, ADAM_B2, ADAM_EPS, ADAM_WD, ADAM_STEP = 0.001, 0.9, 0.999, 1e-08, 0.01, 10

LANES = 128
MASKED = -1e30
VMEM_LIMIT = 48 * 1024 * 1024

_DIMS = {"nn": (((1,), (0,)), ((), ())), "nt": (((1,), (1,)), ((), ())), "tn": (((0,), (0,)), ((), ()))}


def _params(*sem):
    return pltpu.CompilerParams(dimension_semantics=sem, vmem_limit_bytes=VMEM_LIMIT)


def _dot(a, b, mode):
    return lax.dot_general(a.astype(BF16), b.astype(BF16), _DIMS[mode], preferred_element_type=F32)


def _tile(n, pref, unit=8):
    if n <= pref:
        return n
    t = pref - pref % unit
    while n % t:
        t -= unit
    return t


def _log2(n):
    assert n & (n - 1) == 0
    return n.bit_length() - 1


def _matmul(a, b, mode, out_dtype, name, tm=512, tn=512, tk=None):
    if mode == "nn":
        (m, kd), n = a.shape, b.shape[1]
    elif mode == "nt":
        (m, kd), n = a.shape, b.shape[0]
    else:
        (kd, m), n = a.shape, b.shape[1]
    tm, tn = _tile(m, tm, LANES if mode == "tn" else 16), _tile(n, tn, LANES)
    tk = kd if tk is None else _tile(kd, tk, LANES)
    nk = kd // tk
    a_spec = pl.BlockSpec((tk, tm), lambda i, j, k: (k, i)) if mode == "tn" else pl.BlockSpec((tm, tk), lambda i, j, k: (i, k))
    b_spec = pl.BlockSpec((tn, tk), lambda i, j, k: (j, k)) if mode == "nt" else pl.BlockSpec((tk, tn), lambda i, j, k: (k, j))
    o_spec = pl.BlockSpec((tm, tn), lambda i, j, k: (i, j))

    def body(a_ref, b_ref, o_ref, *acc):
        part = _dot(a_ref[...], b_ref[...], mode)
        if nk == 1:
            o_ref[...] = part.astype(out_dtype)
        else:
            acc_ref, k = acc[0], pl.program_id(2)

            @pl.when(k == 0)
            def _():
                acc_ref[...] = part

            @pl.when(k > 0)
            def _():
                acc_ref[...] += part

            @pl.when(k == nk - 1)
            def _():
                o_ref[...] = acc_ref[...].astype(out_dtype)

    return pl.pallas_call(
        body, name=name, grid=(m // tm, n // tn, nk), in_specs=[a_spec, b_spec], out_specs=o_spec,
        out_shape=jax.ShapeDtypeStruct((m, n), out_dtype),
        scratch_shapes=[pltpu.VMEM((tm, tn), F32)] if nk > 1 else [],
        compiler_params=_params("parallel", "parallel", "arbitrary"),
    )(a, b)


def _rstd(x):
    return lax.rsqrt(jnp.mean(x * x, axis=-1, keepdims=True) + NORM_EPS)


def _norm_bwd(x, g, dy):
    r = _rstd(x)
    xh = x * r
    u = dy * g
    dx = r * (u - xh * jnp.mean(u * xh, axis=-1, keepdims=True))
    return dx, jnp.sum(dy * xh, axis=0, keepdims=True)


def _rmsnorm(x, col, width, g, out_dtype, name, traffic=None):
    t = x.shape[0]
    tm = _tile(t, 512)
    steps = t // tm
    n_carried = len(traffic.pieces) if traffic else 0

    def body(*refs):
        x_ref, g_ref, o_ref = refs[0], refs[1], refs[2 + n_carried]
        if traffic:
            carried_in, carried_out, sems = refs[2:2 + n_carried], refs[3 + n_carried], refs[4 + n_carried:]

            @pl.when(pl.program_id(0) == 0)
            def _():
                traffic.start(carried_in, carried_out, *sems)

            if isinstance(traffic, _Relay):
                @pl.when(pl.program_id(0) == max(steps - 2, 0))
                def _():
                    traffic.relay(carried_out, *sems)

        xv = x_ref[...]
        o_ref[...] = ((xv * _rstd(xv)) * g_ref[...]).astype(out_dtype)
        if traffic:
            @pl.when(pl.program_id(0) == steps - 1)
            def _():
                traffic.wait(carried_out, *sems)

    in_specs = [pl.BlockSpec((tm, width), lambda i: (i, col)), pl.BlockSpec((1, width), lambda i: (0, 0))]
    out_specs = [pl.BlockSpec((tm, width), lambda i: (i, 0))]
    out_shape = [jax.ShapeDtypeStruct((t, width), out_dtype)]
    if traffic:
        in_specs += traffic.in_specs
        out_specs.append(traffic.out_spec)
        out_shape.append(traffic.out_shape)
    out = pl.pallas_call(
        body, name=name, grid=(steps,), in_specs=in_specs, out_specs=tuple(out_specs), out_shape=tuple(out_shape),
        scratch_shapes=traffic.scratch if traffic else [],
        compiler_params=_params("arbitrary" if traffic else "parallel"),
    )(x, g, *(traffic.pieces if traffic else []))
    return out if traffic else out[0]


def _split3(x):
    hi = x.astype(BF16)
    r1 = x - hi.astype(F32)
    mid = r1.astype(BF16)
    lo = (r1 - mid.astype(F32)).astype(BF16)
    return hi, mid, lo


def _dot_x01(x, m01):
    hi, mid, lo = _split3(x)
    d = lambda p: lax.dot_general(p, m01, _DIMS["nn"], preferred_element_type=F32)
    return (d(lo) + d(mid)) + d(hi)


def _dot_01x(m01, x):
    hi, mid, lo = _split3(x)
    d = lambda p: lax.dot_general(m01, p, _DIMS["nn"], preferred_element_type=F32)
    return (d(lo) + d(mid)) + d(hi)


def _rows_matmul(terms, rows_in, vecs_in, epilogue, rows_out, sums_out, name, tm=512, prologue=None, traffic=None,
                 weights_to_epilogue=False):
    t = rows_in[0].shape[0]
    tm = _tile(t, tm, 16)
    steps = t // tm
    n_rows, n_vecs = len(rows_in), len(vecs_in)
    n_ab = sum(1 + (a is not None) for a, _, _ in terms)
    n_carried = len(traffic.pieces) if traffic else 0
    halves = [slice(0, tm // 2), slice(tm // 2, tm)] if tm % 32 == 0 else [slice(0, tm)]

    def body(*refs):
        vecs = [r[...] for r in refs[n_ab + n_rows:n_ab + n_rows + n_vecs]]
        out_at = n_ab + n_rows + n_vecs + n_carried
        sum_refs = refs[out_at + len(rows_out):out_at + len(rows_out) + len(sums_out)]
        if traffic:
            carried_in, carried_out, sems = refs[out_at - n_carried:out_at], refs[len(refs) - 4], refs[len(refs) - 3:]

            @pl.when(pl.program_id(0) == 0)
            def _():
                traffic.start(carried_in, carried_out, *sems)

        @pl.when(pl.program_id(0) == 0)
        def _():
            for ref in sum_refs:
                ref[...] = jnp.zeros_like(ref)

        staged = []
        for rows_of in halves:
            row_blocks = [r[rows_of, :] for r in refs[n_ab:n_ab + n_rows]]
            made = prologue(row_blocks, vecs) if prologue else None
            acc, at, weights = None, 0, []
            for a, _, mode in terms:
                lhs = made if a is None else refs[at][rows_of, :]
                at += a is not None
                part = _dot(lhs, refs[at][...], mode)
                weights.append(refs[at])
                at += 1
                acc = part if acc is None else acc + part
            staged.append((rows_of, row_blocks, made, acc))
        for rows_of, row_blocks, made, acc in staged:
            row_vals, sum_vals = epilogue(acc, row_blocks, vecs, *([weights] if weights_to_epilogue else []))
            if prologue:
                row_vals = [made] + row_vals
            for ref, val, (_, dtype) in zip(refs[out_at:], row_vals, rows_out):
                ref[rows_of, :] = val.astype(dtype)
            for ref, val in zip(sum_refs, sum_vals):
                ref[...] += val
        if traffic:
            @pl.when(pl.program_id(0) == steps - 1)
            def _():
                traffic.wait(carried_out, *sems)

    rows = lambda w: pl.BlockSpec((tm, w), lambda i: (i, 0))
    whole = lambda a: pl.BlockSpec(a.shape, lambda i: (0, 0))
    in_specs, args = [], []
    for a, b, _ in terms:
        in_specs += ([rows(a.shape[1])] if a is not None else []) + [whole(b)]
        args += ([a] if a is not None else []) + [b]
    in_specs += [rows(r.shape[1]) for r in rows_in] + [whole(v) for v in vecs_in]
    args += list(rows_in) + list(vecs_in)
    out_specs = [rows(w) for w, _ in rows_out] + [pl.BlockSpec((1, w), lambda i: (0, 0)) for w in sums_out]
    out_shape = [jax.ShapeDtypeStruct((t, w), dt) for w, dt in rows_out] + [jax.ShapeDtypeStruct((1, w), F32) for w in sums_out]
    if traffic:
        in_specs += traffic.in_specs
        args += traffic.pieces
        out_specs.append(traffic.out_spec)
        out_shape.append(traffic.out_shape)
    return pl.pallas_call(
        body, name=name, grid=(steps,), in_specs=in_specs, out_specs=tuple(out_specs), out_shape=tuple(out_shape),
        scratch_shapes=traffic.scratch if traffic else [], compiler_params=_params("arbitrary"),
    )(*args)


def _out_norm(rows, vecs):
    (f, m), (gf, gm) = rows[:2], vecs[:2]
    return jnp.concatenate([((f * _rstd(f)) * gf).astype(BF16), ((m * _rstd(m)) * gm).astype(BF16)], axis=1)


def _residual_norm(acc, rows, vecs):
    x1 = rows[-1] + acc
    return [x1, (x1 * _rstd(x1)) * vecs[-1]], []


def _residual_loss_bwd(acc, rows, vecs, weights):
    x2, gv = rows[0] + acc, vecs[0]
    diff = (x2 * _rstd(x2)) * gv - rows[1]
    dx, dg = _norm_bwd(x2, gv, diff / x2.shape[1])
    d_act = _dot(dx, weights[0][...], "nt")
    return ([dx, dx, d_act * rows[2].astype(F32), d_act * rows[3].astype(F32)],
            [dg, 0.5 * jnp.sum(jnp.mean(diff * diff, axis=-1, keepdims=True), axis=0, keepdims=True)])


def _norm_bwd_residual(acc, rows, vecs):
    dy = acc + rows[2] if len(rows) > 2 else acc
    dx, dg = _norm_bwd(rows[0], vecs[0], dy)
    if len(rows) > 1:
        dx = dx + rows[1]
    return [dx, dx], [dg]


def _ffn_to_mixer_bwd(acc, rows, vecs):
    (dx1, _), (dg,) = _norm_bwd_residual(acc, rows[:2], vecs[:1])
    out_rows, out_sums = _out_norm_bwd(_dot(dx1, vecs[3], "nt"), rows[2:], vecs[1:3])
    return [dx1, dx1] + out_rows, [dg] + out_sums


def _out_norm_bwd(acc, rows, vecs):
    (f, m), w = rows, rows[0].shape[1]
    nh = w // HEAD_DIM
    lane_head = lax.shift_right_logical(lax.broadcasted_iota(jnp.int32, (w, nh), 0), _log2(HEAD_DIM))
    sel = (lane_head == lax.broadcasted_iota(jnp.int32, (w, nh), 1)).astype(BF16)
    dfo, dgf = _norm_bwd(f, vecs[0], acc[:, :w])
    dmo, dgm = _norm_bwd(m, vecs[1], acc[:, w:])
    return [dfo, dmo, _dot_x01(dfo * f, sel), _dot_x01(dmo * m, sel)], [dgf, dgm]


def _ffn_up(h, wg_t, wu_t, name, tm=512, tf=1408):
    t, d = h.shape
    f = wg_t.shape[0]
    tm, tf = _tile(t, tm, 16), _tile(f, tf, LANES)
    tok = pl.BlockSpec((tm, tf), lambda j, i: (i, j))
    wt = pl.BlockSpec((tf, d), lambda j, i: (j, 0))

    def body(h_ref, wg_ref, wu_ref, dg_ref, du_ref, a_ref):
        hv = h_ref[...]
        g, u = _dot(hv, wg_ref[...], "nt"), _dot(hv, wu_ref[...], "nt")
        sg = jax.nn.sigmoid(g)
        silu = g * sg
        dg_ref[...] = (u * (sg * (1.0 + g * (1.0 - sg)))).astype(BF16)
        du_ref[...] = silu.astype(BF16)
        a_ref[...] = (silu * u).astype(BF16)

    return pl.pallas_call(
        body, name=name, grid=(f // tf, t // tm), in_specs=[pl.BlockSpec((tm, d), lambda j, i: (i, 0)), wt, wt],
        out_specs=(tok, tok, tok),
        out_shape=(jax.ShapeDtypeStruct((t, f), BF16), jax.ShapeDtypeStruct((t, f), BF16), jax.ShapeDtypeStruct((t, f), BF16)),
        compiler_params=_params("parallel", "parallel"),
    )(h, wg_t, wu_t)


def _chunk_scan_mats(rows, grp, reverse):
    ii = lax.broadcasted_iota(jnp.int32, (LANES, LANES), 0)
    jj = lax.broadcasted_iota(jnp.int32, (LANES, LANES), 1)
    within = ((ii >= jj) if reverse else (ii <= jj)).astype(BF16)
    ones = jnp.ones((LANES, LANES), BF16)
    ri = lax.broadcasted_iota(jnp.int32, (rows, rows), 0)
    rj = lax.broadcasted_iota(jnp.int32, (rows, rows), 1)
    sh = _log2(grp)
    same = lax.shift_right_logical(ri, sh) == lax.shift_right_logical(rj, sh)
    across = (same & ((rj > ri) if reverse else (rj < ri))).astype(BF16)
    return within, ones, across


def _running_sum(v, mats):
    within, ones, across = mats
    return _dot_x01(v, within) + _dot_01x(across, _dot_x01(v, ones))


def _fgate(z, bcol, grp, name):
    rows = z.shape[0]

    def body(z_ref, b_ref, c_ref):
        zz = z_ref[...] + b_ref[...]
        log_f = jnp.minimum(zz, 0.0) - jnp.log1p(jnp.exp(-jnp.abs(zz)))
        c_ref[...] = _running_sum(log_f, _chunk_scan_mats(rows, grp, False))

    return pl.pallas_call(body, name=name, out_shape=jax.ShapeDtypeStruct(z.shape, F32),
                          compiler_params=pltpu.CompilerParams(vmem_limit_bytes=VMEM_LIMIT))(z, bcol)


def _fgate_bwd(z, bcol, dc, grp, name):
    rows = z.shape[0]

    def body(z_ref, b_ref, dc_ref, dz_ref, db_ref):
        zz = z_ref[...] + b_ref[...]
        dz = _running_sum(dc_ref[...], _chunk_scan_mats(rows, grp, True)) * jax.nn.sigmoid(-zz)
        dz_ref[...] = dz
        head = lax.shift_right_logical(lax.broadcasted_iota(jnp.int32, (HEADS, rows), 1), _log2(grp)) & (HEADS - 1)
        sel = (head == lax.broadcasted_iota(jnp.int32, (HEADS, rows), 0)).astype(BF16)
        db_ref[...] = jnp.sum(_dot_01x(sel, dz), axis=1, keepdims=True)

    return pl.pallas_call(
        body, name=name,
        out_shape=(jax.ShapeDtypeStruct(z.shape, F32), jax.ShapeDtypeStruct((HEADS, 1), F32)),
        compiler_params=pltpu.CompilerParams(vmem_limit_bytes=VMEM_LIMIT),
    )(z, bcol, dc)


def _rotate(x, cs, sn_signed):
    return x * cs + pltpu.roll(x, LANES // 2, axis=1) * sn_signed


def _rope_tables(cos, sin):
    half = cos.shape[1]
    freq = lax.broadcasted_iota(jnp.int32, (half, LANES), 0)
    lane = lax.broadcasted_iota(jnp.int32, (half, LANES), 1)
    hit = (lane & (half - 1)) == freq
    sign = jnp.where(lane < LANES // 2, -1.0, 1.0)
    return _dot_x01(cos, hit.astype(BF16)), _dot_x01(sin, jnp.where(hit, sign, 0.0).astype(BF16))


def _mla_prep(proj_b, q_rank, kv_rank, gq, gkv, w_uq_p, w_ukv_p, nope, cs, sn, name):
    t, bw = proj_b.shape
    qw, kvw = w_uq_p.shape[0], w_ukv_p.shape[0]
    tm = _tile(t, 512)
    rows = lambda w: pl.BlockSpec((tm, w), lambda i: (i, 0))
    whole = lambda a: pl.BlockSpec(a.shape, lambda i: (0, 0))

    def body(pb_ref, gq_ref, gkv_ref, wq_ref, wkv_ref, c_ref, s_ref, qn_ref, kvn_ref, q_ref, kv_ref, kpe_ref):
        c, s = _rope_tables(c_ref[...], s_ref[...])
        ql, kvl = pb_ref[:, :q_rank], pb_ref[:, q_rank:q_rank + kv_rank]
        qn = ((ql * _rstd(ql)) * gq_ref[...]).astype(BF16)
        kvn = ((kvl * _rstd(kvl)) * gkv_ref[...]).astype(BF16)
        qn_ref[...], kvn_ref[...] = qn, kvn
        q_raw = _dot(qn, wq_ref[...], "nt")
        q_ref[:, :nope] = q_raw[:, :nope].astype(BF16)
        for off in range(nope, qw, LANES):
            q_ref[:, off:off + LANES] = _rotate(q_raw[:, off:off + LANES], c, s).astype(BF16)
        kv_ref[...] = _dot(kvn, wkv_ref[...], "nt").astype(BF16)
        kpe_ref[...] = _rotate(pb_ref[:, q_rank + kv_rank:q_rank + kv_rank + LANES], c, s).astype(BF16)

    return pl.pallas_call(
        body, name=name, grid=(t // tm,),
        in_specs=[rows(bw), whole(gq), whole(gkv), whole(w_uq_p), whole(w_ukv_p), rows(cs.shape[1]), rows(sn.shape[1])],
        out_specs=(rows(q_rank), rows(kv_rank), rows(qw), rows(kvw), rows(LANES)),
        out_shape=(jax.ShapeDtypeStruct((t, q_rank), BF16), jax.ShapeDtypeStruct((t, kv_rank), BF16),
                   jax.ShapeDtypeStruct((t, qw), BF16), jax.ShapeDtypeStruct((t, kvw), BF16), jax.ShapeDtypeStruct((t, LANES), BF16)),
        compiler_params=_params("parallel"),
    )(proj_b, gq, gkv, w_uq_p, w_ukv_p, cs, sn)


def _mla_prep_bwd(dq_nope, dq_pe, dkv_all, dk_pe, d_tail, proj_b, q_rank, kv_rank, gq, gkv, w_uq_p, w_ukv_p, cs, sn, name):
    t, bw = proj_b.shape
    nope, pw = dq_nope.shape[1], dq_pe.shape[1]
    tm = _tile(t, 512)
    rows = lambda w: pl.BlockSpec((tm, w), lambda i: (i, 0))
    whole = lambda a: pl.BlockSpec(a.shape, lambda i: (0, 0))
    o_k = q_rank + kv_rank

    def body(dqn_ref, dqp_ref, dkv_ref, dkp_ref, dt_ref, pb_ref, gq_ref, gkv_ref, wq_ref, wkv_ref, c_ref, s_ref,
             dpb_ref, dqr_ref, dgq_ref, dgkv_ref):
        c, s = _rope_tables(c_ref[...], -s_ref[...])
        for off in range(0, pw, LANES):
            dqr_ref[:, off:off + LANES] = _rotate(dqp_ref[:, off:off + LANES], c, s).astype(BF16)
        d_qn = _dot(dqn_ref[...], wq_ref[:nope, :], "nn") + _dot(dqr_ref[...], wq_ref[nope:, :], "nn")
        dq_lat, dgq = _norm_bwd(pb_ref[:, :q_rank], gq_ref[...], d_qn)
        dkv_lat, dgkv = _norm_bwd(pb_ref[:, q_rank:o_k], gkv_ref[...], _dot(dkv_ref[...], wkv_ref[...], "nn"))
        dpb_ref[:, :q_rank] = dq_lat.astype(BF16)
        dpb_ref[:, q_rank:o_k] = dkv_lat.astype(BF16)
        dpb_ref[:, o_k:o_k + LANES] = _rotate(dkp_ref[...], c, s).astype(BF16)
        dpb_ref[:, o_k + LANES:] = dt_ref[...].astype(BF16)

        @pl.when(pl.program_id(0) == 0)
        def _():
            dgq_ref[...] = jnp.zeros_like(dgq_ref)
            dgkv_ref[...] = jnp.zeros_like(dgkv_ref)

        dgq_ref[...] += dgq
        dgkv_ref[...] += dgkv

    return pl.pallas_call(
        body, name=name, grid=(t // tm,),
        in_specs=[rows(nope), rows(pw), rows(dkv_all.shape[1]), rows(LANES), rows(bw - o_k - LANES), rows(bw), whole(gq), whole(gkv),
                  whole(w_uq_p), whole(w_ukv_p), rows(cs.shape[1]), rows(sn.shape[1])],
        out_specs=(rows(bw), rows(pw), whole(gq), whole(gkv)),
        out_shape=(jax.ShapeDtypeStruct((t, bw), BF16), jax.ShapeDtypeStruct((t, pw), BF16),
                   jax.ShapeDtypeStruct(gq.shape, F32), jax.ShapeDtypeStruct(gkv.shape, F32)),
        compiler_params=_params("arbitrary"),
    )(dq_nope, dq_pe, dkv_all, dk_pe, d_tail, proj_b, gq, gkv, w_uq_p, w_ukv_p, cs, sn)


def _lane_masks(pair, h, pe):
    lane = lax.broadcasted_iota(jnp.int32, (1, LANES), 1)
    in_head = lax.shift_right_logical(lane, _log2(HEAD_DIM)) == h
    in_rope = ((lax.shift_right_logical(lane, _log2(MLA_ROPE // 2)) & 3) == ((2 * pair + h) & 3)) if pe else None
    return in_head, in_rope


def _keep(mask, v):
    return jnp.where(mask, v, jnp.zeros_like(v))


def _to_row(col):
    n = col.shape[0]
    eye = lax.broadcasted_iota(jnp.int32, (n, n), 0) == lax.broadcasted_iota(jnp.int32, (n, n), 1)
    return jnp.sum(jnp.where(eye, col, 0.0), axis=0, keepdims=True)


def _to_col(row):
    n = row.shape[1]
    eye = lax.broadcasted_iota(jnp.int32, (n, n), 0) == lax.broadcasted_iota(jnp.int32, (n, n), 1)
    return jnp.sum(jnp.where(eye, row, 0.0), axis=1, keepdims=True)


def _first_step():
    return (pl.program_id(0) == 0) & (pl.program_id(1) == 0)


def _last_step(n0, n1):
    return (pl.program_id(0) == n0 - 1) & (pl.program_id(1) == n1 - 1)


def _attn_fwd(ops, bias, scale, bl, s, tq, name, traffic=None):
    pe = len(ops) == 3
    has_bias = bias is not None
    exact_scale = math.frexp(scale)[0] == 0.5
    nq = s // tq
    t = bl * s
    n_carried = len(traffic.pieces) if traffic else 0

    def body(*refs):
        sems = refs[len(refs) - 3:] if traffic else ()
        if pe:
            q_ref, qpe_ref, kv_ref, kpe_ref = refs[:4]
            n_in = 4
            q_at = lambda r0, r1: q_ref[r0:r1, :]
            v_at = lambda r0, r1: kv_ref[r0:r1, LANES:]
            kcat = refs[len(refs) - 1 - len(sems)]
            kcat[:, :LANES] = kv_ref[:, :LANES]
            kcat[:, LANES:] = kpe_ref[...]
            k_at = lambda r0, r1: kcat[r0:r1, :]
        else:
            qkv_ref = refs[0]
            n_in = 1
            q_at = lambda r0, r1: qkv_ref[r0:r1, :LANES]
            k_at = lambda r0, r1: qkv_ref[r0:r1, LANES:2 * LANES]
            v_at = lambda r0, r1: qkv_ref[r0:r1, 2 * LANES:]
        if has_bias:
            c_ref = refs[n_in]
            n_in += 1
        carried_in = refs[n_in:n_in + n_carried]
        n_in += n_carried
        o_ref, lse_ref = refs[n_in:n_in + 2]
        if traffic:
            carried_out = refs[n_in + 2]

            @pl.when(_first_step())
            def _():
                traffic.start(carried_in, carried_out, *sems)

        pair = pl.program_id(1)
        causal = lax.broadcasted_iota(jnp.int32, (tq, tq), 1) <= lax.broadcasted_iota(jnp.int32, (tq, tq), 0)
        o_ref[...] = jnp.zeros_like(o_ref)

        masks = [_lane_masks(pair, h, pe) for h in range(2)]

        def logits(i):
            r0, r1 = i * tq, (i + 1) * tq
            out = []
            for h in range(2):
                in_head, in_rope = masks[h]
                qm = _keep(in_head, q_at(r0, r1))
                if pe:
                    qm = jnp.concatenate([qm, _keep(in_rope, qpe_ref[r0:r1, :])], axis=1)
                if exact_scale:
                    qm = qm * scale
                spans = []
                for k0, k1 in [(r0, r1)] + ([(0, r0)] if i else []):
                    sc = _dot(qm, k_at(k0, k1), "nt")
                    if not exact_scale:
                        sc = sc * scale
                    if has_bias:
                        sc = sc - c_ref[h, :, k0:k1]
                    spans.append((k0, k1, jnp.where(causal, sc, MASKED) if k0 == r0 else sc))
                out.append(spans)
            return out

        def softmax(per_head):
            out = []
            for spans in per_head:
                m = None
                for _, _, sc in spans:
                    top = jnp.max(sc, axis=1, keepdims=True)
                    m = top if m is None else jnp.maximum(m, top)
                probs = [(k0, k1, jnp.exp(sc - m)) for k0, k1, sc in spans]
                l = sum(jnp.sum(p, axis=1, keepdims=True) for _, _, p in probs)
                out.append((m, l, probs))
            return out

        def weigh(i, per_head):
            r0, r1 = i * tq, (i + 1) * tq
            for h, (m, l, probs) in enumerate(per_head):
                acc = sum(_dot(p, v_at(k0, k1), "nn") for k0, k1, p in probs)
                o_ref[r0:r1, :] = jnp.where(masks[h][0], acc / l, o_ref[r0:r1, :])
                lse = _to_row(m + jnp.log(l))
                lse_ref[h, :, r0:r1] = lse + c_ref[h, :, r0:r1] if has_bias else lse

        ahead = logits(0)
        for i in range(nq):
            solved = softmax(ahead)
            if i + 1 < nq:
                ahead = logits(i + 1)
            weigh(i, solved)

        if traffic:
            @pl.when(_last_step(bl, PAIRS))
            def _():
                traffic.wait(carried_out, *sems)

    seq = lambda w, col: pl.BlockSpec((s, w), col)
    if pe:
        in_specs = [seq(LANES, lambda b, p: (b, p)), seq(LANES, lambda b, p: (b, PAIRS + p // 2)),
                    seq(2 * LANES, lambda b, p: (b, p)), seq(LANES, lambda b, p: (b, 0))]
        args = [ops[0], ops[0], ops[1], ops[2]]
        scratch = [pltpu.VMEM((s, 2 * LANES), BF16)]
    else:
        in_specs = [seq(3 * LANES, lambda b, p: (b, p))]
        args = [ops[0]]
        scratch = []
    per_head_row = pl.BlockSpec((2, 1, s), lambda b, p: (b * PAIRS + p, 0, 0))
    if has_bias:
        in_specs.append(per_head_row)
        args.append(bias)
    out_specs = [seq(LANES, lambda b, p: (b, p)), per_head_row]
    out_shape = [jax.ShapeDtypeStruct((t, HEADS * HEAD_DIM), F32), jax.ShapeDtypeStruct((bl * HEADS, 1, s), F32)]
    if traffic:
        in_specs += traffic.in_specs
        args += traffic.pieces
        out_specs.append(traffic.out_spec)
        out_shape.append(traffic.out_shape)
        scratch += traffic.scratch
    return pl.pallas_call(
        body, name=name, grid=(bl, PAIRS), in_specs=in_specs, out_specs=tuple(out_specs), out_shape=tuple(out_shape),
        scratch_shapes=scratch, compiler_params=_params(*(("arbitrary", "arbitrary") if traffic else ("parallel", "parallel"))),
    )(*args)


def _attn_bwd(ops, do, lse, delta, bias, scale, bl, s, tq, name, traffic=None):
    pe = len(ops) == 3
    has_bias = bias is not None
    exact_scale = math.frexp(scale)[0] == 0.5
    nq = s // tq
    t = bl * s
    width = 2 * LANES if pe else LANES
    n_carried = len(traffic.pieces) if traffic else 0

    def body(*refs):
        if pe:
            q_ref, qpe_ref, kv_ref, kpe_ref = refs[:4]
            n_in = 4
            k_at = lambda r0, r1: kv_ref[r0:r1, :LANES]
            v_at = lambda r0, r1: kv_ref[r0:r1, LANES:]
        else:
            qkv_ref = refs[0]
            n_in = 1
            k_at = lambda r0, r1: qkv_ref[r0:r1, LANES:2 * LANES]
            v_at = lambda r0, r1: qkv_ref[r0:r1, 2 * LANES:]
        do_ref, lse_ref, dl_ref = refs[n_in:n_in + 3]
        n_in += 3
        if has_bias:
            c_ref = refs[n_in]
            n_in += 1
        carried_in = refs[n_in:n_in + n_carried]
        rest = refs[n_in + n_carried:]
        if traffic:
            rest, sems = rest[:-3], rest[-3:]
            carried_out = rest[4 if pe else 2]
            rest = rest[:4 if pe else 2] + rest[(4 if pe else 2) + 1:]

            @pl.when(_first_step())
            def _():
                traffic.start(carried_in, carried_out, *sems)

        if pe:
            dqn_ref, dkv_ref, dqpe_ref, dkpe_ref, dq_acc, qcat = rest
            qcat[:, :LANES] = q_ref[...]
            qcat[:, LANES:] = qpe_ref[...]
            q_at = lambda r0, r1: qcat[r0:r1, :]
            dkv_ref[...] = jnp.zeros_like(dkv_ref)
        else:
            dqkv_ref, dc_ref, dq_acc = rest
            q_at = lambda r0, r1: qkv_ref[r0:r1, :LANES]
            dqkv_ref[...] = jnp.zeros_like(dqkv_ref)
            dc_ref[...] = jnp.zeros_like(dc_ref)
        pair = pl.program_id(1)
        dq_acc[...] = jnp.zeros_like(dq_acc)
        causal = lax.broadcasted_iota(jnp.int32, (tq, tq), 1) >= lax.broadcasted_iota(jnp.int32, (tq, tq), 0)
        if pe:
            @pl.when(pair == 0)
            def _():
                dkpe_ref[...] = jnp.zeros_like(dkpe_ref)

            @pl.when(pair % 2 == 0)
            def _():
                dqpe_ref[...] = jnp.zeros_like(dqpe_ref)

        masks = [_lane_masks(pair, h, pe) for h in range(2)]

        def logits(j):
            r0, r1 = j * tq, (j + 1) * tq
            units = []
            for h in range(2):
                in_head, in_rope = masks[h]
                kt = _keep(in_head, k_at(r0, r1))
                if pe:
                    kt = jnp.concatenate([kt, _keep(in_rope, kpe_ref[r0:r1, :])], axis=1)
                if exact_scale:
                    kt = kt * scale
                vt = _keep(in_head, v_at(r0, r1))
                ck = _to_col(c_ref[h, :, r0:r1]) if has_bias else None
                for q0, q1, diagonal in [(r0, r1, True)] + ([(r1, s, False)] if r1 < s else []):
                    qq, dd = q_at(q0, q1), do_ref[q0:q1, :]
                    st = _dot(kt, qq, "nt")
                    if not exact_scale:
                        st = st * scale
                    shift = lse_ref[h, :, q0:q1]
                    if has_bias:
                        shift = shift - c_ref[h, :, q0:q1]
                        st = st - ck
                    st = st - shift
                    if diagonal:
                        st = jnp.where(causal, st, MASKED)
                    units.append((h, q0, q1, kt, qq, dd, st, _dot(vt, dd, "nt")))
            return units

        def softmax_bwd(units):
            solved = []
            for h, q0, q1, kt, qq, dd, st, dpt in units:
                pt = jnp.exp(st)
                dst = pt * (dpt - dl_ref[h, :, q0:q1])
                solved.append((h, q0, q1, kt, qq, dd, pt, dst, (dst if exact_scale else dst * scale).astype(BF16)))
            return solved

        def products(j, solved):
            r0, r1 = j * tq, (j + 1) * tq
            dv_of, dk_of, cs_of = [None, None], [None, None], [None, None]
            add = lambda old, new: new if old is None else old + new
            for h, q0, q1, kt, qq, dd, pt, dst, dsb in solved:
                dq_acc[q0:q1, :] += _dot(dsb, kt, "tn")
                dv_of[h] = add(dv_of[h], _dot(pt, dd, "nn"))
                dk_of[h] = add(dk_of[h], _dot(dsb, qq, "nn"))
                if has_bias:
                    dc_ref[h, :, q0:q1] += jnp.sum(dst, axis=0, keepdims=True)
                    cs_of[h] = add(cs_of[h], jnp.sum(dst, axis=1, keepdims=True))
            for h in range(2):
                (in_head, in_rope), dv_c, dk_c, cs = masks[h], dv_of[h], dk_of[h], cs_of[h]
                if exact_scale:
                    dk_c = dk_c * scale
                if pe:
                    dkv_ref[r0:r1, :LANES] = jnp.where(in_head, dk_c[:, :LANES].astype(BF16), dkv_ref[r0:r1, :LANES])
                    dkv_ref[r0:r1, LANES:] = jnp.where(in_head, dv_c.astype(BF16), dkv_ref[r0:r1, LANES:])
                    dkpe_ref[r0:r1, :] += _keep(in_rope, dk_c[:, LANES:])
                else:
                    dqkv_ref[r0:r1, LANES:2 * LANES] = jnp.where(in_head, dk_c.astype(BF16), dqkv_ref[r0:r1, LANES:2 * LANES])
                    dqkv_ref[r0:r1, 2 * LANES:] = jnp.where(in_head, dv_c.astype(BF16), dqkv_ref[r0:r1, 2 * LANES:])
                    dc_ref[h, :, r0:r1] -= _to_row(cs)

        units = logits(0)
        for j in range(nq):
            solved = softmax_bwd(units)
            if j + 1 < nq:
                units = logits(j + 1)
            products(j, solved)

        if pe:
            dqn_ref[...] = dq_acc[:, :LANES].astype(BF16)
            dqpe_ref[...] += dq_acc[:, LANES:]
        else:
            dqkv_ref[:, :LANES] = dq_acc[...].astype(BF16)
        if traffic:
            @pl.when(_last_step(bl, PAIRS))
            def _():
                traffic.wait(carried_out, *sems)

    seq = lambda w, col: pl.BlockSpec((s, w), col)
    per_head_row = pl.BlockSpec((2, 1, s), lambda b, p: (b * PAIRS + p, 0, 0))
    if pe:
        in_specs = [seq(LANES, lambda b, p: (b, p)), seq(LANES, lambda b, p: (b, PAIRS + p // 2)),
                    seq(2 * LANES, lambda b, p: (b, p)), seq(LANES, lambda b, p: (b, 0))]
        args = [ops[0], ops[0], ops[1], ops[2]]
    else:
        in_specs = [seq(3 * LANES, lambda b, p: (b, p))]
        args = [ops[0]]
    in_specs += [seq(LANES, lambda b, p: (b, p)), per_head_row, per_head_row]
    args += [do, lse, delta]
    if has_bias:
        in_specs.append(per_head_row)
        args.append(bias)
    scratch = [pltpu.VMEM((s, width), F32)]
    if pe:
        out_specs = (seq(LANES, lambda b, p: (b, p)), seq(2 * LANES, lambda b, p: (b, p)),
                     seq(LANES, lambda b, p: (b, p // 2)), seq(LANES, lambda b, p: (b, 0)))
        out_shape = (jax.ShapeDtypeStruct((t, PAIRS * LANES), BF16), jax.ShapeDtypeStruct((t, PAIRS * 2 * LANES), BF16),
                     jax.ShapeDtypeStruct((t, 2 * LANES), F32), jax.ShapeDtypeStruct((t, LANES), F32))
        scratch.append(pltpu.VMEM((s, 2 * LANES), BF16))
    else:
        out_specs = (seq(3 * LANES, lambda b, p: (b, p)), per_head_row)
        out_shape = (jax.ShapeDtypeStruct((t, PAIRS * 3 * LANES), BF16), jax.ShapeDtypeStruct((bl * HEADS, 1, s), F32))
    if traffic:
        in_specs += traffic.in_specs
        args += traffic.pieces
        out_specs += (traffic.out_spec,)
        out_shape += (traffic.out_shape,)
        scratch += traffic.scratch
    return pl.pallas_call(
        body, name=name, grid=(bl, PAIRS), in_specs=in_specs, out_specs=out_specs, out_shape=out_shape,
        scratch_shapes=scratch, compiler_params=_params("arbitrary" if traffic else "parallel", "arbitrary"),
    )(*args)


def _my_place():
    return lax.axis_index("x"), lax.axis_index("y"), lax.axis_index("c")


def _flip(p, bit):
    return 1 - p if bit else p


def _relative(x, y, c, k):
    return _flip(x, k & 4), _flip(y, k & 2), _flip(c, k & 1)


def _linear(x, y, c):
    return 4 * x + 2 * y + c


class _Traffic:
    def __init__(self, kind, pieces):
        self.kind, self.pieces = kind, list(pieces)
        self.rows = [p.shape[-2] for p in self.pieces]
        self.starts = [sum(self.rows[:i]) for i in range(len(self.rows))]
        anywhere = pl.BlockSpec(memory_space=pl.ANY)
        self.in_specs = [anywhere] * len(self.pieces)
        self.out_spec = anywhere
        self.out_shape = jax.ShapeDtypeStruct((N_DEV, sum(self.rows), self.pieces[0].shape[-1]), self.pieces[0].dtype)
        self.scratch = [pltpu.SemaphoreType.DMA((7,)), pltpu.SemaphoreType.DMA((7,)), pltpu.SemaphoreType.DMA(())]

    def start(self, p_refs, out_ref, send_sems, recv_sems, local_sem):
        x, y, c = _my_place()
        me = _linear(x, y, c)
        mine = lambda i, dev: p_refs[i] if self.kind == "spread" else p_refs[i].at[dev]
        landing = lambda i: out_ref.at[me, pl.ds(self.starts[i], self.rows[i])]
        for i in range(len(p_refs)):
            pltpu.make_async_copy(mine(i, me), landing(i), local_sem).start()
        for k in range(1, N_DEV):
            peer = _relative(x, y, c, k)
            for i in range(len(p_refs)):
                pltpu.make_async_remote_copy(
                    src_ref=mine(i, _linear(*peer)), dst_ref=landing(i),
                    send_sem=send_sems.at[k - 1], recv_sem=recv_sems.at[k - 1], device_id=peer, device_id_type=MESH).start()

    def wait(self, out_ref, send_sems, recv_sems, local_sem):
        x, y, c = _my_place()
        whole = out_ref.at[_linear(x, y, c)]
        for k in range(1, N_DEV):
            both = pltpu.make_async_remote_copy(
                src_ref=whole, dst_ref=whole, send_sem=send_sems.at[k - 1], recv_sem=recv_sems.at[k - 1],
                device_id=_relative(x, y, c, k), device_id_type=MESH)
            both.wait_recv()
            both.wait_send()
        pltpu.make_async_copy(whole, whole, local_sem).wait()


class _Relay(_Traffic):
    def __init__(self, piece):
        super().__init__("spread", [piece])

    @staticmethod
    def _chips(x, y):
        return [(1 - x, y), (x, 1 - y), (1 - x, 1 - y)]

    @staticmethod
    def _copy(k, block, to, out_ref, send_sems, recv_sems, src=None):
        slot = out_ref.at[_linear(*block)]
        return pltpu.make_async_remote_copy(src_ref=slot if src is None else src, dst_ref=slot, send_sem=send_sems.at[k],
                                            recv_sem=recv_sems.at[k], device_id=to, device_id_type=MESH)

    def start(self, p_refs, out_ref, send_sems, recv_sems, local_sem):
        x, y, c = _my_place()
        me, sems = (x, y, c), (out_ref, send_sems, recv_sems)
        pltpu.make_async_copy(p_refs[0], out_ref.at[_linear(*me)], local_sem).start()
        self._copy(0, me, (x, y, 1 - c), *sems, src=p_refs[0]).start()
        for j, chip in enumerate(self._chips(x, y)):
            self._copy(1 + j, me, (*chip, c), *sems, src=p_refs[0]).start()

    def relay(self, out_ref, send_sems, recv_sems, local_sem):
        x, y, c = _my_place()
        sems = (out_ref, send_sems, recv_sems)
        for j, chip in enumerate(self._chips(x, y)):
            self._copy(1 + j, (*chip, c), (x, y, c), *sems).wait_recv()
            self._copy(4 + j, (*chip, c), (x, y, 1 - c), *sems).start()

    def wait(self, out_ref, send_sems, recv_sems, local_sem):
        x, y, c = _my_place()
        me, sems = (x, y, c), (out_ref, send_sems, recv_sems)
        self._copy(0, (x, y, 1 - c), me, *sems).wait_recv()
        for j, chip in enumerate(self._chips(x, y)):
            self._copy(4 + j, (*chip, 1 - c), me, *sems).wait_recv()
        for k in range(N_DEV - 1):
            self._copy(k, me, (x, y, 1 - c), *sems).wait_send()
        whole = out_ref.at[_linear(*me)]
        pltpu.make_async_copy(whole, whole, local_sem).wait()


def _sum_blocks(parts, name):
    n, r, cdim = parts.shape
    tr = _tile(r, 640, 16)

    def body(p_ref, o_ref):
        acc = p_ref[0].astype(F32)
        for d in range(1, n):
            acc = acc + p_ref[d].astype(F32)
        o_ref[...] = acc

    return pl.pallas_call(
        body, name=name, grid=(r // tr,), in_specs=[pl.BlockSpec((n, tr, cdim), lambda i: (0, i, 0))],
        out_specs=pl.BlockSpec((tr, cdim), lambda i: (i, 0)), out_shape=jax.ShapeDtypeStruct((r, cdim), F32),
        compiler_params=_params("parallel"),
    )(parts)


def _adamw_math(w, g, m, v):
    m = ADAM_B1 * m + (1.0 - ADAM_B1) * g
    v = ADAM_B2 * v + (1.0 - ADAM_B2) * (g * g)
    m_hat = m / (1.0 - ADAM_B1 ** ADAM_STEP)
    v_hat = v / (1.0 - ADAM_B2 ** ADAM_STEP)
    delta = -ADAM_LR * (m_hat / (jnp.sqrt(v_hat) + ADAM_EPS) + ADAM_WD * w)
    return delta, m, v


def _adamw(w, g, m, v, name):
    def body(w_ref, g_ref, m_ref, v_ref, d_ref, nm_ref, nv_ref):
        d_ref[...], nm_ref[...], nv_ref[...] = _adamw_math(w_ref[...], g_ref[...], m_ref[...], v_ref[...])

    out = jax.ShapeDtypeStruct(w.shape, F32)
    return pl.pallas_call(body, name=name, out_shape=(out, out, out),
                          compiler_params=pltpu.CompilerParams(vmem_limit_bytes=VMEM_LIMIT))(w, g, m, v)


def _small_all_reduce_adamw(parts, loss_part, ws, ms, vs, name):
    sizes = [p.shape[1] for p in parts] + [1]
    spots = [sum(-(-n // LANES) * LANES for n in sizes[:i]) for i in range(len(sizes))]
    width = spots[-1] + LANES
    k = len(parts)

    def reduce_body(*refs):
        p_refs, tot_ref, rows, send_sems, recv_sems = refs[:k + 1], *refs[k + 1:]
        x, y, c = _my_place()
        me = _linear(x, y, c)
        rows[me] = jnp.zeros((1, width), F32)
        for i in range(k + 1):
            rows[me, :, spots[i]:spots[i] + sizes[i]] = p_refs[i][...]
        copies = []
        for rel in range(1, N_DEV):
            copies.append(pltpu.make_async_remote_copy(
                src_ref=rows.at[me], dst_ref=rows.at[me], send_sem=send_sems.at[rel - 1], recv_sem=recv_sems.at[rel - 1],
                device_id=_relative(x, y, c, rel), device_id_type=MESH))
        for cp in copies:
            cp.start()
        for cp in copies:
            cp.wait_recv()
        for cp in copies:
            cp.wait_send()
        total = rows[0]
        for d in range(1, N_DEV):
            total = total + rows[d]
        tot_ref[...] = total

    total = pl.pallas_call(
        reduce_body, name=name, out_shape=jax.ShapeDtypeStruct((1, width), F32),
        scratch_shapes=[pltpu.VMEM((N_DEV, 1, width), F32), pltpu.SemaphoreType.DMA((7,)), pltpu.SemaphoreType.DMA((7,))],
    )(*parts, loss_part)

    def adamw_body(*refs):
        tot_ref, w_refs, m_refs, v_refs, outs = refs[0], refs[1:k + 1], refs[k + 1:2 * k + 1], refs[2 * k + 1:3 * k + 1], refs[3 * k + 1:]
        for i in range(k):
            g = tot_ref[:, spots[i]:spots[i] + sizes[i]]
            outs[4 * i][...] = g
            outs[4 * i + 1][...], outs[4 * i + 2][...], outs[4 * i + 3][...] = _adamw_math(w_refs[i][...], g, m_refs[i][...], v_refs[i][...])
        outs[4 * k][...] = tot_ref[:, spots[k]:spots[k] + 1]

    out_shape = [jax.ShapeDtypeStruct((1, n), F32) for n in sizes[:k] for _ in range(4)] + [jax.ShapeDtypeStruct((1, 1), F32)]
    res = pl.pallas_call(adamw_body, name=name + "_adamw", out_shape=tuple(out_shape))(total, *ws, *ms, *vs)
    return [res[4 * i:4 * i + 4] for i in range(k)], res[4 * k]


def _pad_rows(a, rows):
    return jnp.pad(a, ((0, rows - a.shape[0]), (0, 0)))


def kernel(x, positions, norm_mix_g, w_in, b_fgate, q_norm_g, w_uq, kv_norm_g, w_ukv, fox_out_g, mla_out_g, w_o, norm_ffn_g, w_gate, w_up, w_down, final_norm_g, loss_target, m_norm_mix_g, m_w_in, m_b_fgate, m_q_norm_g, m_w_uq, m_kv_norm_g, m_w_ukv, m_fox_out_g, m_mla_out_g, m_w_o, m_norm_ffn_g, m_w_gate, m_w_up, m_w_down, m_final_norm_g, v_norm_mix_g, v_w_in, v_b_fgate, v_q_norm_g, v_w_uq, v_kv_norm_g, v_w_ukv, v_fox_out_g, v_mla_out_g, v_w_o, v_norm_ffn_g, v_w_gate, v_w_up, v_w_down, v_final_norm_g):
    bl, s, d = x.shape
    t = bl * s
    bh = bl * HEADS
    tq = _tile(s, 256)
    grp = s // LANES
    fw = HEADS * HEAD_DIM
    q_rank, kv_rank = w_uq.shape[1], w_ukv.shape[1]
    in_cols = w_in.shape[2]
    n_in = N_DEV * in_cols
    ff = N_DEV * w_gate.shape[2]
    half = MLA_ROPE // 2
    o_kvlat, o_krope, o_flogit = q_rank, q_rank + kv_rank, q_rank + kv_rank + LANES
    b_cols = -(-(o_flogit + HEADS) // LANES) * LANES

    tr = lambda w: jnp.transpose(w[0])
    in_rows = -(-in_cols // 16) * 16
    uq_rows = w_uq.shape[2] * q_rank // d
    ukv_rows = w_ukv.shape[2] * kv_rank // d
    pieces = [_pad_rows(tr(w_in), in_rows), _pad_rows(tr(w_uq).reshape(uq_rows, d), -(-uq_rows // 16) * 16),
              tr(w_ukv).reshape(ukv_rows, d), w_o[0], tr(w_gate), tr(w_up), w_down[0]]
    pieces = [p.astype(BF16) for p in pieces]
    offs = [0]
    for p in pieces:
        offs.append(offs[-1] + p.shape[0])
    legs = [(0, 1), (1, 5), (5, 7)]
    gathered = {}

    def full(i, rows):
        leg = next(n for n, (lo, hi) in enumerate(legs) if lo <= i < hi)
        base = offs[legs[leg][0]]
        return gathered[leg][:, offs[i] - base:offs[i] - base + rows]

    x2d = x.reshape(t, d)
    h1, gathered[0] = _rmsnorm(x2d, 0, d, norm_mix_g, BF16, "norm_mix", traffic=_Relay(pieces[0]))

    w_in_t = full(0, in_cols).reshape(n_in, d)
    n_qkv = 3 * fw
    w_in_a = w_in_t[:n_qkv].reshape(3, PAIRS, LANES, d).transpose(1, 0, 2, 3).reshape(n_qkv, d)
    lat0, rope0 = n_qkv + HEADS, n_qkv + HEADS + q_rank + kv_rank
    k_rep = jnp.broadcast_to(w_in_t[rope0:].reshape(2, 1, half, d), (2, 4, half, d)).reshape(LANES, d)
    w_in_b = jnp.concatenate([w_in_t[lat0:rope0], k_rep, w_in_t[n_qkv:lat0],
                              jnp.zeros((b_cols - o_flogit - HEADS, d), BF16)], axis=0)

    def per_head_rows(a):
        return a.reshape(bl, s, HEADS).transpose(0, 2, 1).reshape(bh, 1, s)

    proj_a = _matmul(h1, w_in_a, "nt", BF16, "proj_fox", tm=1024, tn=6 * LANES)
    proj_b = _matmul(h1, w_in_b, "nt", F32, "proj_mla", tm=1024, tn=b_cols)

    z = proj_b[:, o_flogit:o_flogit + HEADS].reshape(bl, s, HEADS).transpose(0, 2, 1).reshape(bh * grp, LANES)
    bcol = jnp.broadcast_to(b_fgate.reshape(1, HEADS, 1), (bl, HEADS, grp)).reshape(bh * grp, 1)
    c = _fgate(z, bcol, grp, "forget_gate")
    c_bias = c.reshape(bh, 1, s)
    fox_o, fox_lse, gathered[1] = _attn_fwd((proj_a,), c_bias, HEAD_DIM ** -0.5, bl, s, tq, "fox_attention",
                                            traffic=_Traffic("spread", pieces[legs[1][0]:legs[1][1]]))
    w_uq_h = full(1, uq_rows).reshape(HEADS, MLA_QK, q_rank)
    w_uq_pe = jnp.concatenate([w_uq_h[:, HEAD_DIM:HEAD_DIM + half].reshape(2, 1, 4 * half, q_rank),
                               w_uq_h[:, HEAD_DIM + half:].reshape(2, 1, 4 * half, q_rank)], axis=1).reshape(2 * LANES, q_rank)
    w_uq_p = jnp.concatenate([w_uq_h[:, :HEAD_DIM].reshape(fw, q_rank), w_uq_pe], axis=0)
    w_ukv_p = full(2, ukv_rows).reshape(PAIRS, 2, 2, HEAD_DIM, kv_rank).transpose(0, 2, 1, 3, 4).reshape(2 * fw, kv_rank)
    w_o_f = full(3, w_o.shape[1]).reshape(-1, d)
    w_gate_t = full(4, ff // N_DEV).reshape(ff, d)

    inv_freq = ROPE_THETA ** (-jnp.arange(0, MLA_ROPE, 2, dtype=F32) / MLA_ROPE)
    ang = positions.astype(F32).reshape(t, 1) * inv_freq[None, :]
    rope_cos, rope_sin = jnp.cos(ang), jnp.sin(ang)
    qn, kvn, q_all, kv_all, kpe = _mla_prep(proj_b, q_rank, kv_rank, q_norm_g, kv_norm_g, w_uq_p, w_ukv_p, fw,
                                            rope_cos, rope_sin, "mla_prep")
    mla_ops = (q_all, kv_all, kpe)
    mla_o, mla_lse, gathered[2] = _attn_fwd(mla_ops, None, MLA_QK ** -0.5, bl, s, tq, "mla_attention",
                                            traffic=_Traffic("spread", pieces[legs[2][0]:legs[2][1]]))
    w_up_t, w_down_f = full(5, ff // N_DEV).reshape(ff, d), full(6, ff // N_DEV).reshape(ff, d)

    both = [(d, F32), (d, BF16)]
    cat, x1, h2 = _rows_matmul([(None, w_o_f, "nn")], [fox_o, mla_o, x2d], [fox_out_g, mla_out_g, norm_ffn_g], _residual_norm,
                               [(2 * fw, BF16)] + both, [], "norm_out_proj_out_norm_ffn", prologue=_out_norm)
    act_by_gate, act_by_up, act = _ffn_up(h2, w_gate_t, w_up_t, "ffn_gate_up")
    dx2, dx2_b, d_gate, d_up, dg_final, loss_part = _rows_matmul(
        [(act, w_down_f, "nn")], [x1, loss_target.reshape(t, d), act_by_gate, act_by_up], [final_norm_g.reshape(1, d)],
        _residual_loss_bwd, both + [(ff, BF16), (ff, BF16)], [d, 1], "ffn_down_loss_d_ffn_down", tm=256, weights_to_epilogue=True)

    dw_down = _matmul(act, dx2_b, "tn", BF16, "dw_down", tm=ff // 2, tn=d, tk=2048)
    dw_gate = _matmul(d_gate, h2, "tn", BF16, "dw_gate", tm=ff // 2, tn=d, tk=2048)
    dw_up = _matmul(d_up, h2, "tn", BF16, "dw_up", tm=ff // 2, tn=d, tk=2048)
    dx1, dx1_b, d_fox_o, d_mla_o, fox_delta, mla_delta, dg_ffn, dg_fox, dg_mla = _rows_matmul(
        [(d_gate, w_gate_t, "nn"), (d_up, w_up_t, "nn")], [x1, dx2, fox_o, mla_o], [norm_ffn_g, fox_out_g, mla_out_g, w_o_f],
        _ffn_to_mixer_bwd, both + [(fw, BF16), (fw, BF16), (HEADS, F32), (HEADS, F32)], [d, fw, fw],
        "d_ffn_gate_up_norm_ffn_proj_out_norm_out", tm=256)
    dw_o = _matmul(cat, dx1_b, "tn", BF16, "dw_o", tn=d, tk=2048)

    per_dev = lambda a: a.reshape(N_DEV, -1, d)
    late_grads = [per_dev(dw_o), per_dev(dw_gate), per_dev(dw_up), per_dev(dw_down)]
    dproj_a, dc, g_late = _attn_bwd((proj_a,), d_fox_o, fox_lse, per_head_rows(fox_delta),
                                    c_bias, HEAD_DIM ** -0.5, bl, s, tq, "d_fox_attention", traffic=_Traffic("swap", late_grads))
    dz, db_fgate = _fgate_bwd(z, bcol, dc.reshape(bh * grp, LANES), grp, "d_forget_gate")
    d_flogit = dz.reshape(bl, HEADS, s).transpose(0, 2, 1).reshape(t, HEADS)

    dq_nope, dkv_all, dq_pe, dk_pe = _attn_bwd(mla_ops, d_mla_o, mla_lse, per_head_rows(mla_delta),
                                               None, MLA_QK ** -0.5, bl, s, tq, "d_mla_attention")
    d_tail = jnp.pad(d_flogit, ((0, 0), (0, b_cols - o_flogit - HEADS)))
    dproj_b, dq_rot, dg_q, dg_kv = _mla_prep_bwd(dq_nope, dq_pe, dkv_all, dk_pe, d_tail, proj_b, q_rank, kv_rank,
                                                 q_norm_g, kv_norm_g, w_uq_p, w_ukv_p, rope_cos, rope_sin, "d_mla_prep")
    dw_uq_nope = _matmul(dq_nope, qn, "tn", BF16, "dw_uq_nope", tn=q_rank, tk=1024)
    dw_uq_pe = _matmul(dq_rot, qn, "tn", BF16, "dw_uq_rope", tn=q_rank, tk=1024)
    dw_ukv_p = _matmul(dkv_all, kvn, "tn", BF16, "dw_ukv", tn=kv_rank, tk=1024)
    dw_in_a = _matmul(dproj_a, h1, "tn", BF16, "dw_in_fox", tm=6 * LANES, tn=d, tk=2048)
    dw_in_b = _matmul(dproj_b, h1, "tn", F32, "dw_in_mla", tm=b_cols, tn=d, tk=1024)

    dw_krope = dw_in_b[o_krope:o_flogit].reshape(2, 4, half, d).sum(axis=1).reshape(MLA_ROPE, d)
    dw_in_t = jnp.concatenate([dw_in_a.reshape(PAIRS, 3, LANES, d).transpose(1, 0, 2, 3).reshape(n_qkv, d),
                               dw_in_b[o_flogit:o_flogit + HEADS].astype(BF16), dw_in_b[:o_krope].astype(BF16),
                               dw_krope.astype(BF16)], axis=0)
    pad_dev = lambda a, rows: jnp.pad(a, ((0, 0), (0, rows - a.shape[1]), (0, 0)))
    dw_uq_pe5 = dw_uq_pe.reshape(2, 2, 4, half, q_rank)
    dw_uq_h = jnp.concatenate([dw_uq_nope.reshape(HEADS, HEAD_DIM, q_rank), dw_uq_pe5[:, 0].reshape(HEADS, half, q_rank),
                               dw_uq_pe5[:, 1].reshape(HEADS, half, q_rank)], axis=1)
    dw_ukv_h = dw_ukv_p.reshape(PAIRS, 2, 2, HEAD_DIM, kv_rank).transpose(0, 2, 1, 3, 4).reshape(HEADS, 2 * HEAD_DIM, kv_rank)
    n_last = 3
    last_grads = [pad_dev(per_dev(dw_in_t), pieces[0].shape[0]), pad_dev(per_dev(dw_uq_h), pieces[1].shape[0]), per_dev(dw_ukv_h)]
    grad_x, dg_mix, g_last = _rows_matmul([(dproj_a, w_in_a, "nn"), (dproj_b, w_in_b, "nn")], [x2d, dx1], [norm_mix_g],
                                          _norm_bwd_residual, [(d, F32)], [d], "d_proj_in_norm_mix",
                                          traffic=_Traffic("swap", last_grads))
    g_last = _sum_blocks(g_last, "sum_last_grads")
    g_late = _sum_blocks(g_late, "sum_late_grads")

    def mine(i, rows):
        src, base = (g_last, 0) if i < n_last else (g_late, offs[n_last])
        return src[offs[i] - base:offs[i] - base + rows]

    big = [
        ("w_in", w_in, m_w_in, v_w_in, mine(0, in_cols), True),
        ("w_uq", w_uq, m_w_uq, v_w_uq, mine(1, uq_rows).reshape(-1, q_rank), True),
        ("w_ukv", w_ukv, m_w_ukv, v_w_ukv, mine(2, ukv_rows).reshape(-1, kv_rank), True),
        ("w_o", w_o, m_w_o, v_w_o, mine(3, w_o.shape[1]), False),
        ("w_gate", w_gate, m_w_gate, v_w_gate, mine(4, ff // N_DEV), True),
        ("w_up", w_up, m_w_up, v_w_up, mine(5, ff // N_DEV), True),
        ("w_down", w_down, m_w_down, v_w_down, mine(6, ff // N_DEV), False),
    ]
    out = {}
    for nm, w, m, v, g, transposed in big:
        lay = (lambda a: a[0].T) if transposed else (lambda a: a[0])
        back = (lambda a: a.T[None]) if transposed else (lambda a: a[None])
        dl, new_m, new_v = _adamw(lay(w), g, lay(m), lay(v), "adamw_" + nm)
        out[nm] = (back(g), back(dl), back(new_m), back(new_v))

    smalls = [("norm_mix_g", norm_mix_g, m_norm_mix_g, v_norm_mix_g, dg_mix),
              ("b_fgate", b_fgate, m_b_fgate, v_b_fgate, db_fgate.reshape(1, HEADS)),
              ("q_norm_g", q_norm_g, m_q_norm_g, v_q_norm_g, dg_q),
              ("kv_norm_g", kv_norm_g, m_kv_norm_g, v_kv_norm_g, dg_kv),
              ("fox_out_g", fox_out_g, m_fox_out_g, v_fox_out_g, dg_fox),
              ("mla_out_g", mla_out_g, m_mla_out_g, v_mla_out_g, dg_mla),
              ("norm_ffn_g", norm_ffn_g, m_norm_ffn_g, v_norm_ffn_g, dg_ffn),
              ("final_norm_g", final_norm_g, m_final_norm_g, v_final_norm_g, dg_final)]
    flat = lambda a: a.reshape(1, -1)
    results, loss = _small_all_reduce_adamw([e[4] for e in smalls], loss_part, [flat(e[1]) for e in smalls],
                                            [flat(e[2]) for e in smalls], [flat(e[3]) for e in smalls], "reduce_small_adamw")
    for (nm, w, _, _, _), res in zip(smalls, results):
        out[nm] = tuple(a.reshape(w.shape) for a in res)
    loss = loss[0, 0]

    order = ["norm_mix_g", "w_in", "b_fgate", "q_norm_g", "w_uq", "kv_norm_g", "w_ukv", "fox_out_g", "mla_out_g", "w_o",
             "norm_ffn_g", "w_gate", "w_up", "w_down", "final_norm_g"]
    return (loss, grad_x.reshape(bl, s, d), *[out[n][0] for n in order], *[out[n][1] for n in order],
            *[out[n][2] for n in order], *[out[n][3] for n in order])
```

```python
import math

import jax
import jax.numpy as jnp
from jax import lax
from jax.experimental import pallas as pl
from jax.experimental.pallas import tpu as pltpu

F32 = jnp.float32
BF16 = jnp.bfloat16
MESH = pl.DeviceIdType.MESH

N_DEV = 8
HEADS = 8
HEAD_DIM = 64
PAIRS = HEADS // 2
MLA_ROPE = 32
MLA_QK = HEAD_DIM + MLA_ROPE
ROPE_THETA = 10000.0
NORM_EPS = 1e-6
ADAM_LR, ADAM_B1, ADAM_B2, ADAM_EPS, ADAM_WD, ADAM_STEP = 0.001, 0.9, 0.999, 1e-08, 0.01, 10

LANES = 128
MASKED = -1e30
VMEM_LIMIT = 48 * 1024 * 1024

_DIMS = {"nn": (((1,), (0,)), ((), ())), "nt": (((1,), (1,)), ((), ())), "tn": (((0,), (0,)), ((), ()))}


def _params(*sem):
    return pltpu.CompilerParams(dimension_semantics=sem, vmem_limit_bytes=VMEM_LIMIT)


def _dot(a, b, mode):
    return lax.dot_general(a.astype(BF16), b.astype(BF16), _DIMS[mode], preferred_element_type=F32)


def _tile(n, pref, unit=8):
    if n <= pref:
        return n
    t = pref - pref % unit
    while n % t:
        t -= unit
    return t


def _log2(n):
    assert n & (n - 1) == 0
    return n.bit_length() - 1


def _matmul(a, b, mode, out_dtype, name, tm=512, tn=512, tk=None):
    if mode == "nn":
        (m, kd), n = a.shape, b.shape[1]
    elif mode == "nt":
        (m, kd), n = a.shape, b.shape[0]
    else:
        (kd, m), n = a.shape, b.shape[1]
    tm, tn = _tile(m, tm, LANES if mode == "tn" else 16), _tile(n, tn, LANES)
    tk = kd if tk is None else _tile(kd, tk, LANES)
    nk = kd // tk
    a_spec = pl.BlockSpec((tk, tm), lambda i, j, k: (k, i)) if mode == "tn" else pl.BlockSpec((tm, tk), lambda i, j, k: (i, k))
    b_spec = pl.BlockSpec((tn, tk), lambda i, j, k: (j, k)) if mode == "nt" else pl.BlockSpec((tk, tn), lambda i, j, k: (k, j))
    o_spec = pl.BlockSpec((tm, tn), lambda i, j, k: (i, j))

    def body(a_ref, b_ref, o_ref, *acc):
        part = _dot(a_ref[...], b_ref[...], mode)
        if nk == 1:
            o_ref[...] = part.astype(out_dtype)
        else:
            acc_ref, k = acc[0], pl.program_id(2)

            @pl.when(k == 0)
            def _():
                acc_ref[...] = part

            @pl.when(k > 0)
            def _():
                acc_ref[...] += part

            @pl.when(k == nk - 1)
            def _():
                o_ref[...] = acc_ref[...].astype(out_dtype)

    return pl.pallas_call(
        body, name=name, grid=(m // tm, n // tn, nk), in_specs=[a_spec, b_spec], out_specs=o_spec,
        out_shape=jax.ShapeDtypeStruct((m, n), out_dtype),
        scratch_shapes=[pltpu.VMEM((tm, tn), F32)] if nk > 1 else [],
        compiler_params=_params("parallel", "parallel", "arbitrary"),
    )(a, b)


def _rstd(x):
    return lax.rsqrt(jnp.mean(x * x, axis=-1, keepdims=True) + NORM_EPS)


def _norm_bwd(x, g, dy):
    r = _rstd(x)
    xh = x * r
    u = dy * g
    dx = r * (u - xh * jnp.mean(u * xh, axis=-1, keepdims=True))
    return dx, jnp.sum(dy * xh, axis=0, keepdims=True)


def _rmsnorm(x, col, width, g, out_dtype, name, traffic=None):
    t = x.shape[0]
    tm = _tile(t, 512)
    steps = t // tm
    n_carried = len(traffic.pieces) if traffic else 0

    def body(*refs):
        x_ref, g_ref, o_ref = refs[0], refs[1], refs[2 + n_carried]
        if traffic:
            carried_in, carried_out, sems = refs[2:2 + n_carried], refs[3 + n_carried], refs[4 + n_carried:]

            @pl.when(pl.program_id(0) == 0)
            def _():
                traffic.start(carried_in, carried_out, *sems)

            if isinstance(traffic, _Relay):
                @pl.when(pl.program_id(0) == max(steps - 2, 0))
                def _():
                    traffic.relay(carried_out, *sems)

        xv = x_ref[...]
        o_ref[...] = ((xv * _rstd(xv)) * g_ref[...]).astype(out_dtype)
        if traffic:
            @pl.when(pl.program_id(0) == steps - 1)
            def _():
                traffic.wait(carried_out, *sems)

    in_specs = [pl.BlockSpec((tm, width), lambda i: (i, col)), pl.BlockSpec((1, width), lambda i: (0, 0))]
    out_specs = [pl.BlockSpec((tm, width), lambda i: (i, 0))]
    out_shape = [jax.ShapeDtypeStruct((t, width), out_dtype)]
    if traffic:
        in_specs += traffic.in_specs
        out_specs.append(traffic.out_spec)
        out_shape.append(traffic.out_shape)
    out = pl.pallas_call(
        body, name=name, grid=(steps,), in_specs=in_specs, out_specs=tuple(out_specs), out_shape=tuple(out_shape),
        scratch_shapes=traffic.scratch if traffic else [],
        compiler_params=_params("arbitrary" if traffic else "parallel"),
    )(x, g, *(traffic.pieces if traffic else []))
    return out if traffic else out[0]


def _split3(x):
    hi = x.astype(BF16)
    r1 = x - hi.astype(F32)
    mid = r1.astype(BF16)
    lo = (r1 - mid.astype(F32)).astype(BF16)
    return hi, mid, lo


def _dot_x01(x, m01):
    hi, mid, lo = _split3(x)
    d = lambda p: lax.dot_general(p, m01, _DIMS["nn"], preferred_element_type=F32)
    return (d(lo) + d(mid)) + d(hi)


def _dot_01x(m01, x):
    hi, mid, lo = _split3(x)
    d = lambda p: lax.dot_general(m01, p, _DIMS["nn"], preferred_element_type=F32)
    return (d(lo) + d(mid)) + d(hi)


def _rows_matmul(terms, rows_in, vecs_in, epilogue, rows_out, sums_out, name, tm=512, prologue=None, traffic=None):
    t = rows_in[0].shape[0]
    tm = _tile(t, tm, 16)
    steps = t // tm
    n_rows, n_vecs = len(rows_in), len(vecs_in)
    n_ab = sum(1 + (a is not None) for a, _, _ in terms)
    n_carried = len(traffic.pieces) if traffic else 0
    halves = [slice(0, tm // 2), slice(tm // 2, tm)] if tm % 32 == 0 else [slice(0, tm)]

    def body(*refs):
        vecs = [r[...] for r in refs[n_ab + n_rows:n_ab + n_rows + n_vecs]]
        out_at = n_ab + n_rows + n_vecs + n_carried
        sum_refs = refs[out_at + len(rows_out):out_at + len(rows_out) + len(sums_out)]
        if traffic:
            carried_in, carried_out, sems = refs[out_at - n_carried:out_at], refs[len(refs) - 4], refs[len(refs) - 3:]

            @pl.when(pl.program_id(0) == 0)
            def _():
                traffic.start(carried_in, carried_out, *sems)

        @pl.when(pl.program_id(0) == 0)
        def _():
            for ref in sum_refs:
                ref[...] = jnp.zeros_like(ref)

        staged = []
        for rows_of in halves:
            row_blocks = [r[rows_of, :] for r in refs[n_ab:n_ab + n_rows]]
            made = prologue(row_blocks, vecs) if prologue else None
            acc, at = None, 0
            for a, _, mode in terms:
                lhs = made if a is None else refs[at][rows_of, :]
                at += a is not None
                part = _dot(lhs, refs[at][...], mode)
                at += 1
                acc = part if acc is None else acc + part
            staged.append((rows_of, row_blocks, made, acc))
        for rows_of, row_blocks, made, acc in staged:
            row_vals, sum_vals = epilogue(acc, row_blocks, vecs)
            if prologue:
                row_vals = [made] + row_vals
            for ref, val, (_, dtype) in zip(refs[out_at:], row_vals, rows_out):
                ref[rows_of, :] = val.astype(dtype)
            for ref, val in zip(sum_refs, sum_vals):
                ref[...] += val
        if traffic:
            @pl.when(pl.program_id(0) == steps - 1)
            def _():
                traffic.wait(carried_out, *sems)

    rows = lambda w: pl.BlockSpec((tm, w), lambda i: (i, 0))
    whole = lambda a: pl.BlockSpec(a.shape, lambda i: (0, 0))
    in_specs, args = [], []
    for a, b, _ in terms:
        in_specs += ([rows(a.shape[1])] if a is not None else []) + [whole(b)]
        args += ([a] if a is not None else []) + [b]
    in_specs += [rows(r.shape[1]) for r in rows_in] + [whole(v) for v in vecs_in]
    args += list(rows_in) + list(vecs_in)
    out_specs = [rows(w) for w, _ in rows_out] + [pl.BlockSpec((1, w), lambda i: (0, 0)) for w in sums_out]
    out_shape = [jax.ShapeDtypeStruct((t, w), dt) for w, dt in rows_out] + [jax.ShapeDtypeStruct((1, w), F32) for w in sums_out]
    if traffic:
        in_specs += traffic.in_specs
        args += traffic.pieces
        out_specs.append(traffic.out_spec)
        out_shape.append(traffic.out_shape)
    return pl.pallas_call(
        body, name=name, grid=(steps,), in_specs=in_specs, out_specs=tuple(out_specs), out_shape=tuple(out_shape),
        scratch_shapes=traffic.scratch if traffic else [], compiler_params=_params("arbitrary"),
    )(*args)


def _out_norm(rows, vecs):
    (f, m), (gf, gm) = rows[:2], vecs[:2]
    return jnp.concatenate([((f * _rstd(f)) * gf).astype(BF16), ((m * _rstd(m)) * gm).astype(BF16)], axis=1)


def _residual_norm(acc, rows, vecs):
    x1 = rows[-1] + acc
    return [x1, (x1 * _rstd(x1)) * vecs[-1]], []


def _residual_loss_bwd(acc, rows, vecs):
    x2, gv = rows[0] + acc, vecs[0]
    diff = (x2 * _rstd(x2)) * gv - rows[1]
    dx, dg = _norm_bwd(x2, gv, diff / x2.shape[1])
    return [dx, dx], [dg, 0.5 * jnp.sum(jnp.mean(diff * diff, axis=-1, keepdims=True), axis=0, keepdims=True)]


def _norm_bwd_residual(acc, rows, vecs):
    dy = acc + rows[2] if len(rows) > 2 else acc
    dx, dg = _norm_bwd(rows[0], vecs[0], dy)
    if len(rows) > 1:
        dx = dx + rows[1]
    return [dx, dx], [dg]


def _out_norm_bwd(acc, rows, vecs):
    (f, m), w = rows, rows[0].shape[1]
    nh = w // HEAD_DIM
    lane_head = lax.shift_right_logical(lax.broadcasted_iota(jnp.int32, (w, nh), 0), _log2(HEAD_DIM))
    sel = (lane_head == lax.broadcasted_iota(jnp.int32, (w, nh), 1)).astype(BF16)
    dfo, dgf = _norm_bwd(f, vecs[0], acc[:, :w])
    dmo, dgm = _norm_bwd(m, vecs[1], acc[:, w:])
    return [dfo, dmo, _dot_x01(dfo * f, sel), _dot_x01(dmo * m, sel)], [dgf, dgm]


def _ffn_up(h, wg_t, wu_t, name, tm=512, tf=1408):
    t, d = h.shape
    f = wg_t.shape[0]
    tm, tf = _tile(t, tm, 16), _tile(f, tf, LANES)
    tok = pl.BlockSpec((tm, tf), lambda j, i: (i, j))
    wt = pl.BlockSpec((tf, d), lambda j, i: (j, 0))

    def body(h_ref, wg_ref, wu_ref, dg_ref, du_ref, a_ref):
        hv = h_ref[...]
        g, u = _dot(hv, wg_ref[...], "nt"), _dot(hv, wu_ref[...], "nt")
        sg = jax.nn.sigmoid(g)
        silu = g * sg
        dg_ref[...] = (u * (sg * (1.0 + g * (1.0 - sg)))).astype(BF16)
        du_ref[...] = silu.astype(BF16)
        a_ref[...] = (silu * u).astype(BF16)

    return pl.pallas_call(
        body, name=name, grid=(f // tf, t // tm), in_specs=[pl.BlockSpec((tm, d), lambda j, i: (i, 0)), wt, wt],
        out_specs=(tok, tok, tok),
        out_shape=(jax.ShapeDtypeStruct((t, f), BF16), jax.ShapeDtypeStruct((t, f), BF16), jax.ShapeDtypeStruct((t, f), BF16)),
        compiler_params=_params("parallel", "parallel"),
    )(h, wg_t, wu_t)


def _ffn_down_bwd(dy, w_down, act_by_gate, act_by_up, name, tm=512, tf=1408):
    t, d = dy.shape
    f = w_down.shape[0]
    tm, tf = _tile(t, tm, 16), _tile(f, tf, LANES)
    tok = pl.BlockSpec((tm, tf), lambda j, i: (i, j))

    def body(dy_ref, w_ref, g_ref, u_ref, dg_ref, du_ref):
        da = _dot(dy_ref[...], w_ref[...], "nt")
        dg_ref[...] = (da * g_ref[...].astype(F32)).astype(BF16)
        du_ref[...] = (da * u_ref[...].astype(F32)).astype(BF16)

    return pl.pallas_call(
        body, name=name, grid=(f // tf, t // tm),
        in_specs=[pl.BlockSpec((tm, d), lambda j, i: (i, 0)), pl.BlockSpec((tf, d), lambda j, i: (j, 0)), tok, tok],
        out_specs=(tok, tok),
        out_shape=(jax.ShapeDtypeStruct((t, f), BF16), jax.ShapeDtypeStruct((t, f), BF16)),
        compiler_params=_params("parallel", "parallel"),
    )(dy, w_down, act_by_gate, act_by_up)


def _chunk_scan_mats(rows, grp, reverse):
    ii = lax.broadcasted_iota(jnp.int32, (LANES, LANES), 0)
    jj = lax.broadcasted_iota(jnp.int32, (LANES, LANES), 1)
    within = ((ii >= jj) if reverse else (ii <= jj)).astype(BF16)
    ones = jnp.ones((LANES, LANES), BF16)
    ri = lax.broadcasted_iota(jnp.int32, (rows, rows), 0)
    rj = lax.broadcasted_iota(jnp.int32, (rows, rows), 1)
    sh = _log2(grp)
    same = lax.shift_right_logical(ri, sh) == lax.shift_right_logical(rj, sh)
    across = (same & ((rj > ri) if reverse else (rj < ri))).astype(BF16)
    return within, ones, across


def _running_sum(v, mats):
    within, ones, across = mats
    return _dot_x01(v, within) + _dot_01x(across, _dot_x01(v, ones))


def _fgate(z, bcol, grp, name):
    rows = z.shape[0]

    def body(z_ref, b_ref, c_ref):
        zz = z_ref[...] + b_ref[...]
        log_f = jnp.minimum(zz, 0.0) - jnp.log1p(jnp.exp(-jnp.abs(zz)))
        c_ref[...] = _running_sum(log_f, _chunk_scan_mats(rows, grp, False))

    return pl.pallas_call(body, name=name, out_shape=jax.ShapeDtypeStruct(z.shape, F32),
                          compiler_params=pltpu.CompilerParams(vmem_limit_bytes=VMEM_LIMIT))(z, bcol)


def _fgate_bwd(z, bcol, dc, grp, name):
    rows = z.shape[0]

    def body(z_ref, b_ref, dc_ref, dz_ref, db_ref):
        zz = z_ref[...] + b_ref[...]
        dz = _running_sum(dc_ref[...], _chunk_scan_mats(rows, grp, True)) * jax.nn.sigmoid(-zz)
        dz_ref[...] = dz
        head = lax.shift_right_logical(lax.broadcasted_iota(jnp.int32, (HEADS, rows), 1), _log2(grp)) & (HEADS - 1)
        sel = (head == lax.broadcasted_iota(jnp.int32, (HEADS, rows), 0)).astype(BF16)
        db_ref[...] = jnp.sum(_dot_01x(sel, dz), axis=1, keepdims=True)

    return pl.pallas_call(
        body, name=name,
        out_shape=(jax.ShapeDtypeStruct(z.shape, F32), jax.ShapeDtypeStruct((HEADS, 1), F32)),
        compiler_params=pltpu.CompilerParams(vmem_limit_bytes=VMEM_LIMIT),
    )(z, bcol, dc)


def _rotate(x, cs, sn_signed):
    return x * cs + pltpu.roll(x, LANES // 2, axis=1) * sn_signed


def _rope_tables(cos, sin):
    half = cos.shape[1]
    freq = lax.broadcasted_iota(jnp.int32, (half, LANES), 0)
    lane = lax.broadcasted_iota(jnp.int32, (half, LANES), 1)
    hit = (lane & (half - 1)) == freq
    sign = jnp.where(lane < LANES // 2, -1.0, 1.0)
    return _dot_x01(cos, hit.astype(BF16)), _dot_x01(sin, jnp.where(hit, sign, 0.0).astype(BF16))


def _mla_prep(proj_b, q_rank, kv_rank, gq, gkv, w_uq_p, w_ukv_p, nope, cs, sn, name):
    t, bw = proj_b.shape
    qw, kvw = w_uq_p.shape[0], w_ukv_p.shape[0]
    tm = _tile(t, 512)
    rows = lambda w: pl.BlockSpec((tm, w), lambda i: (i, 0))
    whole = lambda a: pl.BlockSpec(a.shape, lambda i: (0, 0))

    def body(pb_ref, gq_ref, gkv_ref, wq_ref, wkv_ref, c_ref, s_ref, qn_ref, kvn_ref, q_ref, kv_ref, kpe_ref):
        c, s = _rope_tables(c_ref[...], s_ref[...])
        ql, kvl = pb_ref[:, :q_rank], pb_ref[:, q_rank:q_rank + kv_rank]
        qn = ((ql * _rstd(ql)) * gq_ref[...]).astype(BF16)
        kvn = ((kvl * _rstd(kvl)) * gkv_ref[...]).astype(BF16)
        qn_ref[...], kvn_ref[...] = qn, kvn
        q_raw = _dot(qn, wq_ref[...], "nt")
        q_ref[:, :nope] = q_raw[:, :nope].astype(BF16)
        for off in range(nope, qw, LANES):
            q_ref[:, off:off + LANES] = _rotate(q_raw[:, off:off + LANES], c, s).astype(BF16)
        kv_ref[...] = _dot(kvn, wkv_ref[...], "nt").astype(BF16)
        kpe_ref[...] = _rotate(pb_ref[:, q_rank + kv_rank:q_rank + kv_rank + LANES], c, s).astype(BF16)

    return pl.pallas_call(
        body, name=name, grid=(t // tm,),
        in_specs=[rows(bw), whole(gq), whole(gkv), whole(w_uq_p), whole(w_ukv_p), rows(cs.shape[1]), rows(sn.shape[1])],
        out_specs=(rows(q_rank), rows(kv_rank), rows(qw), rows(kvw), rows(LANES)),
        out_shape=(jax.ShapeDtypeStruct((t, q_rank), BF16), jax.ShapeDtypeStruct((t, kv_rank), BF16),
                   jax.ShapeDtypeStruct((t, qw), BF16), jax.ShapeDtypeStruct((t, kvw), BF16), jax.ShapeDtypeStruct((t, LANES), BF16)),
        compiler_params=_params("parallel"),
    )(proj_b, gq, gkv, w_uq_p, w_ukv_p, cs, sn)


def _mla_prep_bwd(dq_nope, dq_pe, dkv_all, dk_pe, d_tail, proj_b, q_rank, kv_rank, gq, gkv, w_uq_p, w_ukv_p, cs, sn, name):
    t, bw = proj_b.shape
    nope, pw = dq_nope.shape[1], dq_pe.shape[1]
    tm = _tile(t, 512)
    rows = lambda w: pl.BlockSpec((tm, w), lambda i: (i, 0))
    whole = lambda a: pl.BlockSpec(a.shape, lambda i: (0, 0))
    o_k = q_rank + kv_rank

    def body(dqn_ref, dqp_ref, dkv_ref, dkp_ref, dt_ref, pb_ref, gq_ref, gkv_ref, wq_ref, wkv_ref, c_ref, s_ref,
             dpb_ref, dqr_ref, dgq_ref, dgkv_ref):
        c, s = _rope_tables(c_ref[...], -s_ref[...])
        for off in range(0, pw, LANES):
            dqr_ref[:, off:off + LANES] = _rotate(dqp_ref[:, off:off + LANES], c, s).astype(BF16)
        d_qn = _dot(dqn_ref[...], wq_ref[:nope, :], "nn") + _dot(dqr_ref[...], wq_ref[nope:, :], "nn")
        dq_lat, dgq = _norm_bwd(pb_ref[:, :q_rank], gq_ref[...], d_qn)
        dkv_lat, dgkv = _norm_bwd(pb_ref[:, q_rank:o_k], gkv_ref[...], _dot(dkv_ref[...], wkv_ref[...], "nn"))
        dpb_ref[:, :q_rank] = dq_lat.astype(BF16)
        dpb_ref[:, q_rank:o_k] = dkv_lat.astype(BF16)
        dpb_ref[:, o_k:o_k + LANES] = _rotate(dkp_ref[...], c, s).astype(BF16)
        dpb_ref[:, o_k + LANES:] = dt_ref[...].astype(BF16)

        @pl.when(pl.program_id(0) == 0)
        def _():
            dgq_ref[...] = jnp.zeros_like(dgq_ref)
            dgkv_ref[...] = jnp.zeros_like(dgkv_ref)

        dgq_ref[...] += dgq
        dgkv_ref[...] += dgkv

    return pl.pallas_call(
        body, name=name, grid=(t // tm,),
        in_specs=[rows(nope), rows(pw), rows(dkv_all.shape[1]), rows(LANES), rows(bw - o_k - LANES), rows(bw), whole(gq), whole(gkv),
                  whole(w_uq_p), whole(w_ukv_p), rows(cs.shape[1]), rows(sn.shape[1])],
        out_specs=(rows(bw), rows(pw), whole(gq), whole(gkv)),
        out_shape=(jax.ShapeDtypeStruct((t, bw), BF16), jax.ShapeDtypeStruct((t, pw), BF16),
                   jax.ShapeDtypeStruct(gq.shape, F32), jax.ShapeDtypeStruct(gkv.shape, F32)),
        compiler_params=_params("arbitrary"),
    )(dq_nope, dq_pe, dkv_all, dk_pe, d_tail, proj_b, gq, gkv, w_uq_p, w_ukv_p, cs, sn)


def _lane_masks(pair, h, pe):
    lane = lax.broadcasted_iota(jnp.int32, (1, LANES), 1)
    in_head = lax.shift_right_logical(lane, _log2(HEAD_DIM)) == h
    in_rope = ((lax.shift_right_logical(lane, _log2(MLA_ROPE // 2)) & 3) == ((2 * pair + h) & 3)) if pe else None
    return in_head, in_rope


def _keep(mask, v):
    return jnp.where(mask, v, jnp.zeros_like(v))


def _to_row(col):
    n = col.shape[0]
    eye = lax.broadcasted_iota(jnp.int32, (n, n), 0) == lax.broadcasted_iota(jnp.int32, (n, n), 1)
    return jnp.sum(jnp.where(eye, col, 0.0), axis=0, keepdims=True)


def _to_col(row):
    n = row.shape[1]
    eye = lax.broadcasted_iota(jnp.int32, (n, n), 0) == lax.broadcasted_iota(jnp.int32, (n, n), 1)
    return jnp.sum(jnp.where(eye, row, 0.0), axis=1, keepdims=True)


def _first_step():
    return (pl.program_id(0) == 0) & (pl.program_id(1) == 0)


def _last_step(n0, n1):
    return (pl.program_id(0) == n0 - 1) & (pl.program_id(1) == n1 - 1)


def _attn_fwd(ops, bias, scale, bl, s, tq, name, traffic=None):
    pe = len(ops) == 3
    has_bias = bias is not None
    exact_scale = math.frexp(scale)[0] == 0.5
    span = 4 * tq
    nq = s // tq
    t = bl * s
    n_carried = len(traffic.pieces) if traffic else 0

    def body(*refs):
        sems = refs[len(refs) - 3:] if traffic else ()
        if pe:
            q_ref, qpe_ref, kv_ref, kpe_ref = refs[:4]
            n_in = 4
            q_at = lambda r0, r1: q_ref[r0:r1, :]
            v_at = lambda r0, r1: kv_ref[r0:r1, LANES:]
            kcat = refs[len(refs) - 1 - len(sems)]
            kcat[:, :LANES] = kv_ref[:, :LANES]
            kcat[:, LANES:] = kpe_ref[...]
            k_at = lambda r0, r1: kcat[r0:r1, :]
        else:
            qkv_ref = refs[0]
            n_in = 1
            q_at = lambda r0, r1: qkv_ref[r0:r1, :LANES]
            k_at = lambda r0, r1: qkv_ref[r0:r1, LANES:2 * LANES]
            v_at = lambda r0, r1: qkv_ref[r0:r1, 2 * LANES:]
        if has_bias:
            c_ref = refs[n_in]
            n_in += 1
        carried_in = refs[n_in:n_in + n_carried]
        n_in += n_carried
        o_ref, lse_ref = refs[n_in:n_in + 2]
        if traffic:
            carried_out = refs[n_in + 2]

            @pl.when(_first_step())
            def _():
                traffic.start(carried_in, carried_out, *sems)

        pair = pl.program_id(1)
        causal = lax.broadcasted_iota(jnp.int32, (tq, tq), 1) <= lax.broadcasted_iota(jnp.int32, (tq, tq), 0)
        o_ref[...] = jnp.zeros_like(o_ref)

        masks = [_lane_masks(pair, h, pe) for h in range(2)]

        def logits(i):
            r0, r1 = i * tq, (i + 1) * tq
            out = []
            for h in range(2):
                in_head, in_rope = masks[h]
                qm = _keep(in_head, q_at(r0, r1))
                if pe:
                    qm = jnp.concatenate([qm, _keep(in_rope, qpe_ref[r0:r1, :])], axis=1)
                if exact_scale:
                    qm = qm * scale
                spans = []
                for k0, k1 in [(r0, r1)] + [(k, min(k + span, r0)) for k in range(0, r0, span)]:
                    sc = _dot(qm, k_at(k0, k1), "nt")
                    if not exact_scale:
                        sc = sc * scale
                    if has_bias:
                        sc = sc - c_ref[h, :, k0:k1]
                    spans.append((k0, k1, jnp.where(causal, sc, MASKED) if k0 == r0 else sc))
                out.append(spans)
            return out

        def softmax(per_head):
            out = []
            for spans in per_head:
                m = None
                for _, _, sc in spans:
                    top = jnp.max(sc, axis=1, keepdims=True)
                    m = top if m is None else jnp.maximum(m, top)
                probs = [(k0, k1, jnp.exp(sc - m)) for k0, k1, sc in spans]
                l = sum(jnp.sum(p, axis=1, keepdims=True) for _, _, p in probs)
                out.append((m, l, probs))
            return out

        def weigh(i, per_head):
            r0, r1 = i * tq, (i + 1) * tq
            for h, (m, l, probs) in enumerate(per_head):
                acc = sum(_dot(p, v_at(k0, k1), "nn") for k0, k1, p in probs)
                o_ref[r0:r1, :] = jnp.where(masks[h][0], acc / l, o_ref[r0:r1, :])
                lse = _to_row(m + jnp.log(l))
                lse_ref[h, :, r0:r1] = lse + c_ref[h, :, r0:r1] if has_bias else lse

        ahead = logits(0)
        for i in range(nq):
            solved = softmax(ahead)
            if i + 1 < nq:
                ahead = logits(i + 1)
            weigh(i, solved)

        if traffic:
            @pl.when(_last_step(bl, PAIRS))
            def _():
                traffic.wait(carried_out, *sems)

    seq = lambda w, col: pl.BlockSpec((s, w), col)
    if pe:
        in_specs = [seq(LANES, lambda b, p: (b, p)), seq(LANES, lambda b, p: (b, PAIRS + p // 2)),
                    seq(2 * LANES, lambda b, p: (b, p)), seq(LANES, lambda b, p: (b, 0))]
        args = [ops[0], ops[0], ops[1], ops[2]]
        scratch = [pltpu.VMEM((s, 2 * LANES), BF16)]
    else:
        in_specs = [seq(3 * LANES, lambda b, p: (b, p))]
        args = [ops[0]]
        scratch = []
    per_head_row = pl.BlockSpec((2, 1, s), lambda b, p: (b * PAIRS + p, 0, 0))
    if has_bias:
        in_specs.append(per_head_row)
        args.append(bias)
    out_specs = [seq(LANES, lambda b, p: (b, p)), per_head_row]
    out_shape = [jax.ShapeDtypeStruct((t, HEADS * HEAD_DIM), F32), jax.ShapeDtypeStruct((bl * HEADS, 1, s), F32)]
    if traffic:
        in_specs += traffic.in_specs
        args += traffic.pieces
        out_specs.append(traffic.out_spec)
        out_shape.append(traffic.out_shape)
        scratch += traffic.scratch
    return pl.pallas_call(
        body, name=name, grid=(bl, PAIRS), in_specs=in_specs, out_specs=tuple(out_specs), out_shape=tuple(out_shape),
        scratch_shapes=scratch, compiler_params=_params(*(("arbitrary", "arbitrary") if traffic else ("parallel", "parallel"))),
    )(*args)


def _attn_bwd(ops, do, lse, delta, bias, scale, bl, s, tq, name, traffic=None):
    pe = len(ops) == 3
    has_bias = bias is not None
    exact_scale = math.frexp(scale)[0] == 0.5
    span = 2 * tq
    nq = s // tq
    t = bl * s
    width = 2 * LANES if pe else LANES
    n_carried = len(traffic.pieces) if traffic else 0

    def body(*refs):
        if pe:
            q_ref, qpe_ref, kv_ref, kpe_ref = refs[:4]
            n_in = 4
            k_at = lambda r0, r1: kv_ref[r0:r1, :LANES]
            v_at = lambda r0, r1: kv_ref[r0:r1, LANES:]
        else:
            qkv_ref = refs[0]
            n_in = 1
            k_at = lambda r0, r1: qkv_ref[r0:r1, LANES:2 * LANES]
            v_at = lambda r0, r1: qkv_ref[r0:r1, 2 * LANES:]
        do_ref, lse_ref, dl_ref = refs[n_in:n_in + 3]
        n_in += 3
        if has_bias:
            c_ref = refs[n_in]
            n_in += 1
        carried_in = refs[n_in:n_in + n_carried]
        rest = refs[n_in + n_carried:]
        if traffic:
            rest, sems = rest[:-3], rest[-3:]
            carried_out = rest[4 if pe else 2]
            rest = rest[:4 if pe else 2] + rest[(4 if pe else 2) + 1:]

            @pl.when(_first_step())
            def _():
                traffic.start(carried_in, carried_out, *sems)

        if pe:
            dqn_ref, dkv_ref, dqpe_ref, dkpe_ref, dq_acc, qcat = rest
            qcat[:, :LANES] = q_ref[...]
            qcat[:, LANES:] = qpe_ref[...]
            q_at = lambda r0, r1: qcat[r0:r1, :]
            dkv_ref[...] = jnp.zeros_like(dkv_ref)
        else:
            dqkv_ref, dc_ref, dq_acc = rest
            q_at = lambda r0, r1: qkv_ref[r0:r1, :LANES]
            dqkv_ref[...] = jnp.zeros_like(dqkv_ref)
            dc_ref[...] = jnp.zeros_like(dc_ref)
        pair = pl.program_id(1)
        dq_acc[...] = jnp.zeros_like(dq_acc)
        causal = lax.broadcasted_iota(jnp.int32, (tq, tq), 1) >= lax.broadcasted_iota(jnp.int32, (tq, tq), 0)
        if pe:
            @pl.when(pair == 0)
            def _():
                dkpe_ref[...] = jnp.zeros_like(dkpe_ref)

            @pl.when(pair % 2 == 0)
            def _():
                dqpe_ref[...] = jnp.zeros_like(dqpe_ref)

        masks = [_lane_masks(pair, h, pe) for h in range(2)]

        def logits(j):
            r0, r1 = j * tq, (j + 1) * tq
            units = []
            for h in range(2):
                in_head, in_rope = masks[h]
                kt = _keep(in_head, k_at(r0, r1))
                if pe:
                    kt = jnp.concatenate([kt, _keep(in_rope, kpe_ref[r0:r1, :])], axis=1)
                if exact_scale:
                    kt = kt * scale
                vt = _keep(in_head, v_at(r0, r1))
                ck = _to_col(c_ref[h, :, r0:r1]) if has_bias else None
                for q0, q1, diagonal in [(r0, r1, True)] + [(q, min(q + span, s), False) for q in range(r1, s, span)]:
                    qq, dd = q_at(q0, q1), do_ref[q0:q1, :]
                    st = _dot(kt, qq, "nt")
                    if not exact_scale:
                        st = st * scale
                    shift = lse_ref[h, :, q0:q1]
                    if has_bias:
                        shift = shift - c_ref[h, :, q0:q1]
                        st = st - ck
                    st = st - shift
                    if diagonal:
                        st = jnp.where(causal, st, MASKED)
                    units.append((h, q0, q1, kt, qq, dd, st, _dot(vt, dd, "nt")))
            return units

        def softmax_bwd(units):
            solved = []
            for h, q0, q1, kt, qq, dd, st, dpt in units:
                pt = jnp.exp(st)
                dst = pt * (dpt - dl_ref[h, :, q0:q1])
                solved.append((h, q0, q1, kt, qq, dd, pt, dst, (dst if exact_scale else dst * scale).astype(BF16)))
            return solved

        def products(j, solved):
            r0, r1 = j * tq, (j + 1) * tq
            dv_of, dk_of, cs_of = [None, None], [None, None], [None, None]
            add = lambda old, new: new if old is None else old + new
            for h, q0, q1, kt, qq, dd, pt, dst, dsb in solved:
                dq_acc[q0:q1, :] += _dot(dsb, kt, "tn")
                dv_of[h] = add(dv_of[h], _dot(pt, dd, "nn"))
                dk_of[h] = add(dk_of[h], _dot(dsb, qq, "nn"))
                if has_bias:
                    dc_ref[h, :, q0:q1] += jnp.sum(dst, axis=0, keepdims=True)
                    cs_of[h] = add(cs_of[h], jnp.sum(dst, axis=1, keepdims=True))
            for h in range(2):
                (in_head, in_rope), dv_c, dk_c, cs = masks[h], dv_of[h], dk_of[h], cs_of[h]
                if exact_scale:
                    dk_c = dk_c * scale
                if pe:
                    dkv_ref[r0:r1, :LANES] = jnp.where(in_head, dk_c[:, :LANES].astype(BF16), dkv_ref[r0:r1, :LANES])
                    dkv_ref[r0:r1, LANES:] = jnp.where(in_head, dv_c.astype(BF16), dkv_ref[r0:r1, LANES:])
                    dkpe_ref[r0:r1, :] += _keep(in_rope, dk_c[:, LANES:])
                else:
                    dqkv_ref[r0:r1, LANES:2 * LANES] = jnp.where(in_head, dk_c.astype(BF16), dqkv_ref[r0:r1, LANES:2 * LANES])
                    dqkv_ref[r0:r1, 2 * LANES:] = jnp.where(in_head, dv_c.astype(BF16), dqkv_ref[r0:r1, 2 * LANES:])
                    dc_ref[h, :, r0:r1] -= _to_row(cs)

        units = logits(0)
        for j in range(nq):
            solved = softmax_bwd(units)
            if j + 1 < nq:
                units = logits(j + 1)
            products(j, solved)

        if pe:
            dqn_ref[...] = dq_acc[:, :LANES].astype(BF16)
            dqpe_ref[...] += dq_acc[:, LANES:]
        else:
            dqkv_ref[:, :LANES] = dq_acc[...].astype(BF16)
        if traffic:
            @pl.when(_last_step(bl, PAIRS))
            def _():
                traffic.wait(carried_out, *sems)

    seq = lambda w, col: pl.BlockSpec((s, w), col)
    per_head_row = pl.BlockSpec((2, 1, s), lambda b, p: (b * PAIRS + p, 0, 0))
    if pe:
        in_specs = [seq(LANES, lambda b, p: (b, p)), seq(LANES, lambda b, p: (b, PAIRS + p // 2)),
                    seq(2 * LANES, lambda b, p: (b, p)), seq(LANES, lambda b, p: (b, 0))]
        args = [ops[0], ops[0], ops[1], ops[2]]
    else:
        in_specs = [seq(3 * LANES, lambda b, p: (b, p))]
        args = [ops[0]]
    in_specs += [seq(LANES, lambda b, p: (b, p)), per_head_row, per_head_row]
    args += [do, lse, delta]
    if has_bias:
        in_specs.append(per_head_row)
        args.append(bias)
    scratch = [pltpu.VMEM((s, width), F32)]
    if pe:
        out_specs = (seq(LANES, lambda b, p: (b, p)), seq(2 * LANES, lambda b, p: (b, p)),
                     seq(LANES, lambda b, p: (b, p // 2)), seq(LANES, lambda b, p: (b, 0)))
        out_shape = (jax.ShapeDtypeStruct((t, PAIRS * LANES), BF16), jax.ShapeDtypeStruct((t, PAIRS * 2 * LANES), BF16),
                     jax.ShapeDtypeStruct((t, 2 * LANES), F32), jax.ShapeDtypeStruct((t, LANES), F32))
        scratch.append(pltpu.VMEM((s, 2 * LANES), BF16))
    else:
        out_specs = (seq(3 * LANES, lambda b, p: (b, p)), per_head_row)
        out_shape = (jax.ShapeDtypeStruct((t, PAIRS * 3 * LANES), BF16), jax.ShapeDtypeStruct((bl * HEADS, 1, s), F32))
    if traffic:
        in_specs += traffic.in_specs
        args += traffic.pieces
        out_specs += (traffic.out_spec,)
        out_shape += (traffic.out_shape,)
        scratch += traffic.scratch
    return pl.pallas_call(
        body, name=name, grid=(bl, PAIRS), in_specs=in_specs, out_specs=out_specs, out_shape=out_shape,
        scratch_shapes=scratch, compiler_params=_params("arbitrary" if traffic else "parallel", "arbitrary"),
    )(*args)


def _my_place():
    return lax.axis_index("x"), lax.axis_index("y"), lax.axis_index("c")


def _flip(p, bit):
    return 1 - p if bit else p


def _relative(x, y, c, k):
    return _flip(x, k & 4), _flip(y, k & 2), _flip(c, k & 1)


def _linear(x, y, c):
    return 4 * x + 2 * y + c


class _Traffic:
    def __init__(self, kind, pieces):
        self.kind, self.pieces = kind, list(pieces)
        self.rows = [p.shape[-2] for p in self.pieces]
        self.starts = [sum(self.rows[:i]) for i in range(len(self.rows))]
        anywhere = pl.BlockSpec(memory_space=pl.ANY)
        self.in_specs = [anywhere] * len(self.pieces)
        self.out_spec = anywhere
        self.out_shape = jax.ShapeDtypeStruct((N_DEV, sum(self.rows), self.pieces[0].shape[-1]), self.pieces[0].dtype)
        self.scratch = [pltpu.SemaphoreType.DMA((7,)), pltpu.SemaphoreType.DMA((7,)), pltpu.SemaphoreType.DMA(())]

    def start(self, p_refs, out_ref, send_sems, recv_sems, local_sem):
        x, y, c = _my_place()
        me = _linear(x, y, c)
        mine = lambda i, dev: p_refs[i] if self.kind == "spread" else p_refs[i].at[dev]
        landing = lambda i: out_ref.at[me, pl.ds(self.starts[i], self.rows[i])]
        for i in range(len(p_refs)):
            pltpu.make_async_copy(mine(i, me), landing(i), local_sem).start()
        for k in range(1, N_DEV):
            peer = _relative(x, y, c, k)
            for i in range(len(p_refs)):
                pltpu.make_async_remote_copy(
                    src_ref=mine(i, _linear(*peer)), dst_ref=landing(i),
                    send_sem=send_sems.at[k - 1], recv_sem=recv_sems.at[k - 1], device_id=peer, device_id_type=MESH).start()

    def wait(self, out_ref, send_sems, recv_sems, local_sem):
        x, y, c = _my_place()
        whole = out_ref.at[_linear(x, y, c)]
        for k in range(1, N_DEV):
            both = pltpu.make_async_remote_copy(
                src_ref=whole, dst_ref=whole, send_sem=send_sems.at[k - 1], recv_sem=recv_sems.at[k - 1],
                device_id=_relative(x, y, c, k), device_id_type=MESH)
            both.wait_recv()
            both.wait_send()
        pltpu.make_async_copy(whole, whole, local_sem).wait()


class _Relay(_Traffic):
    def __init__(self, piece):
        super().__init__("spread", [piece])

    @staticmethod
    def _chips(x, y):
        return [(1 - x, y), (x, 1 - y), (1 - x, 1 - y)]

    @staticmethod
    def _copy(k, block, to, out_ref, send_sems, recv_sems, src=None):
        slot = out_ref.at[_linear(*block)]
        return pltpu.make_async_remote_copy(src_ref=slot if src is None else src, dst_ref=slot, send_sem=send_sems.at[k],
                                            recv_sem=recv_sems.at[k], device_id=to, device_id_type=MESH)

    def start(self, p_refs, out_ref, send_sems, recv_sems, local_sem):
        x, y, c = _my_place()
        me, sems = (x, y, c), (out_ref, send_sems, recv_sems)
        pltpu.make_async_copy(p_refs[0], out_ref.at[_linear(*me)], local_sem).start()
        self._copy(0, me, (x, y, 1 - c), *sems, src=p_refs[0]).start()
        for j, chip in enumerate(self._chips(x, y)):
            self._copy(1 + j, me, (*chip, c), *sems, src=p_refs[0]).start()

    def relay(self, out_ref, send_sems, recv_sems, local_sem):
        x, y, c = _my_place()
        sems = (out_ref, send_sems, recv_sems)
        for j, chip in enumerate(self._chips(x, y)):
            self._copy(1 + j, (*chip, c), (x, y, c), *sems).wait_recv()
            self._copy(4 + j, (*chip, c), (x, y, 1 - c), *sems).start()

    def wait(self, out_ref, send_sems, recv_sems, local_sem):
        x, y, c = _my_place()
        me, sems = (x, y, c), (out_ref, send_sems, recv_sems)
        self._copy(0, (x, y, 1 - c), me, *sems).wait_recv()
        for j, chip in enumerate(self._chips(x, y)):
            self._copy(4 + j, (*chip, 1 - c), me, *sems).wait_recv()
        for k in range(N_DEV - 1):
            self._copy(k, me, (x, y, 1 - c), *sems).wait_send()
        whole = out_ref.at[_linear(*me)]
        pltpu.make_async_copy(whole, whole, local_sem).wait()


def _sum_blocks(parts, name):
    n, r, cdim = parts.shape
    tr = _tile(r, 640, 16)

    def body(p_ref, o_ref):
        acc = p_ref[0].astype(F32)
        for d in range(1, n):
            acc = acc + p_ref[d].astype(F32)
        o_ref[...] = acc

    return pl.pallas_call(
        body, name=name, grid=(r // tr,), in_specs=[pl.BlockSpec((n, tr, cdim), lambda i: (0, i, 0))],
        out_specs=pl.BlockSpec((tr, cdim), lambda i: (i, 0)), out_shape=jax.ShapeDtypeStruct((r, cdim), F32),
        compiler_params=_params("parallel"),
    )(parts)


def _adamw_math(w, g, m, v):
    m = ADAM_B1 * m + (1.0 - ADAM_B1) * g
    v = ADAM_B2 * v + (1.0 - ADAM_B2) * (g * g)
    m_hat = m / (1.0 - ADAM_B1 ** ADAM_STEP)
    v_hat = v / (1.0 - ADAM_B2 ** ADAM_STEP)
    delta = -ADAM_LR * (m_hat / (jnp.sqrt(v_hat) + ADAM_EPS) + ADAM_WD * w)
    return delta, m, v


def _adamw(w, g, m, v, name):
    def body(w_ref, g_ref, m_ref, v_ref, d_ref, nm_ref, nv_ref):
        d_ref[...], nm_ref[...], nv_ref[...] = _adamw_math(w_ref[...], g_ref[...], m_ref[...], v_ref[...])

    out = jax.ShapeDtypeStruct(w.shape, F32)
    return pl.pallas_call(body, name=name, out_shape=(out, out, out),
                          compiler_params=pltpu.CompilerParams(vmem_limit_bytes=VMEM_LIMIT))(w, g, m, v)


def _small_all_reduce_adamw(parts, loss_part, ws, ms, vs, name):
    sizes = [p.shape[1] for p in parts] + [1]
    spots = [sum(-(-n // LANES) * LANES for n in sizes[:i]) for i in range(len(sizes))]
    width = spots[-1] + LANES
    k = len(parts)

    def reduce_body(*refs):
        p_refs, tot_ref, rows, send_sems, recv_sems = refs[:k + 1], *refs[k + 1:]
        x, y, c = _my_place()
        me = _linear(x, y, c)
        rows[me] = jnp.zeros((1, width), F32)
        for i in range(k + 1):
            rows[me, :, spots[i]:spots[i] + sizes[i]] = p_refs[i][...]
        copies = []
        for rel in range(1, N_DEV):
            copies.append(pltpu.make_async_remote_copy(
                src_ref=rows.at[me], dst_ref=rows.at[me], send_sem=send_sems.at[rel - 1], recv_sem=recv_sems.at[rel - 1],
                device_id=_relative(x, y, c, rel), device_id_type=MESH))
        for cp in copies:
            cp.start()
        for cp in copies:
            cp.wait_recv()
        for cp in copies:
            cp.wait_send()
        total = rows[0]
        for d in range(1, N_DEV):
            total = total + rows[d]
        tot_ref[...] = total

    total = pl.pallas_call(
        reduce_body, name=name, out_shape=jax.ShapeDtypeStruct((1, width), F32),
        scratch_shapes=[pltpu.VMEM((N_DEV, 1, width), F32), pltpu.SemaphoreType.DMA((7,)), pltpu.SemaphoreType.DMA((7,))],
    )(*parts, loss_part)

    def adamw_body(*refs):
        tot_ref, w_refs, m_refs, v_refs, outs = refs[0], refs[1:k + 1], refs[k + 1:2 * k + 1], refs[2 * k + 1:3 * k + 1], refs[3 * k + 1:]
        for i in range(k):
            g = tot_ref[:, spots[i]:spots[i] + sizes[i]]
            outs[4 * i][...] = g
            outs[4 * i + 1][...], outs[4 * i + 2][...], outs[4 * i + 3][...] = _adamw_math(w_refs[i][...], g, m_refs[i][...], v_refs[i][...])
        outs[4 * k][...] = tot_ref[:, spots[k]:spots[k] + 1]

    out_shape = [jax.ShapeDtypeStruct((1, n), F32) for n in sizes[:k] for _ in range(4)] + [jax.ShapeDtypeStruct((1, 1), F32)]
    res = pl.pallas_call(adamw_body, name=name + "_adamw", out_shape=tuple(out_shape))(total, *ws, *ms, *vs)
    return [res[4 * i:4 * i + 4] for i in range(k)], res[4 * k]


def _pad_rows(a, rows):
    return jnp.pad(a, ((0, rows - a.shape[0]), (0, 0)))


def kernel(x, positions, norm_mix_g, w_in, b_fgate, q_norm_g, w_uq, kv_norm_g, w_ukv, fox_out_g, mla_out_g, w_o, norm_ffn_g, w_gate, w_up, w_down, final_norm_g, loss_target, m_norm_mix_g, m_w_in, m_b_fgate, m_q_norm_g, m_w_uq, m_kv_norm_g, m_w_ukv, m_fox_out_g, m_mla_out_g, m_w_o, m_norm_ffn_g, m_w_gate, m_w_up, m_w_down, m_final_norm_g, v_norm_mix_g, v_w_in, v_b_fgate, v_q_norm_g, v_w_uq, v_kv_norm_g, v_w_ukv, v_fox_out_g, v_mla_out_g, v_w_o, v_norm_ffn_g, v_w_gate, v_w_up, v_w_down, v_final_norm_g):
    bl, s, d = x.shape
    t = bl * s
    bh = bl * HEADS
    tq = _tile(s, 256)
    grp = s // LANES
    fw = HEADS * HEAD_DIM
    q_rank, kv_rank = w_uq.shape[1], w_ukv.shape[1]
    in_cols = w_in.shape[2]
    n_in = N_DEV * in_cols
    ff = N_DEV * w_gate.shape[2]
    half = MLA_ROPE // 2
    o_kvlat, o_krope, o_flogit = q_rank, q_rank + kv_rank, q_rank + kv_rank + LANES
    b_cols = -(-(o_flogit + HEADS) // LANES) * LANES

    tr = lambda w: jnp.transpose(w[0])
    in_rows = -(-in_cols // 16) * 16
    uq_rows = w_uq.shape[2] * q_rank // d
    ukv_rows = w_ukv.shape[2] * kv_rank // d
    pieces = [_pad_rows(tr(w_in), in_rows), _pad_rows(tr(w_uq).reshape(uq_rows, d), -(-uq_rows // 16) * 16),
              tr(w_ukv).reshape(ukv_rows, d), w_o[0], tr(w_gate), tr(w_up), w_down[0]]
    pieces = [p.astype(BF16) for p in pieces]
    offs = [0]
    for p in pieces:
        offs.append(offs[-1] + p.shape[0])
    legs = [(0, 1), (1, 5), (5, 7)]
    gathered = {}

    def full(i, rows):
        leg = next(n for n, (lo, hi) in enumerate(legs) if lo <= i < hi)
        base = offs[legs[leg][0]]
        return gathered[leg][:, offs[i] - base:offs[i] - base + rows]

    x2d = x.reshape(t, d)
    h1, gathered[0] = _rmsnorm(x2d, 0, d, norm_mix_g, BF16, "norm_mix", traffic=_Relay(pieces[0]))

    w_in_t = full(0, in_cols).reshape(n_in, d)
    n_qkv = 3 * fw
    w_in_a = w_in_t[:n_qkv].reshape(3, PAIRS, LANES, d).transpose(1, 0, 2, 3).reshape(n_qkv, d)
    lat0, rope0 = n_qkv + HEADS, n_qkv + HEADS + q_rank + kv_rank
    k_rep = jnp.broadcast_to(w_in_t[rope0:].reshape(2, 1, half, d), (2, 4, half, d)).reshape(LANES, d)
    w_in_b = jnp.concatenate([w_in_t[lat0:rope0], k_rep, w_in_t[n_qkv:lat0],
                              jnp.zeros((b_cols - o_flogit - HEADS, d), BF16)], axis=0)

    def per_head_rows(a):
        return a.reshape(bl, s, HEADS).transpose(0, 2, 1).reshape(bh, 1, s)

    proj_a = _matmul(h1, w_in_a, "nt", BF16, "proj_fox", tm=1024, tn=6 * LANES)
    proj_b = _matmul(h1, w_in_b, "nt", F32, "proj_mla", tm=1024, tn=b_cols)

    z = proj_b[:, o_flogit:o_flogit + HEADS].reshape(bl, s, HEADS).transpose(0, 2, 1).reshape(bh * grp, LANES)
    bcol = jnp.broadcast_to(b_fgate.reshape(1, HEADS, 1), (bl, HEADS, grp)).reshape(bh * grp, 1)
    c = _fgate(z, bcol, grp, "forget_gate")
    c_bias = c.reshape(bh, 1, s)
    fox_o, fox_lse, gathered[1] = _attn_fwd((proj_a,), c_bias, HEAD_DIM ** -0.5, bl, s, tq, "fox_attention",
                                            traffic=_Traffic("spread", pieces[legs[1][0]:legs[1][1]]))
    w_uq_h = full(1, uq_rows).reshape(HEADS, MLA_QK, q_rank)
    w_uq_pe = jnp.concatenate([w_uq_h[:, HEAD_DIM:HEAD_DIM + half].reshape(2, 1, 4 * half, q_rank),
                               w_uq_h[:, HEAD_DIM + half:].reshape(2, 1, 4 * half, q_rank)], axis=1).reshape(2 * LANES, q_rank)
    w_uq_p = jnp.concatenate([w_uq_h[:, :HEAD_DIM].reshape(fw, q_rank), w_uq_pe], axis=0)
    w_ukv_p = full(2, ukv_rows).reshape(PAIRS, 2, 2, HEAD_DIM, kv_rank).transpose(0, 2, 1, 3, 4).reshape(2 * fw, kv_rank)
    w_o_f = full(3, w_o.shape[1]).reshape(-1, d)
    w_gate_t = full(4, ff // N_DEV).reshape(ff, d)

    inv_freq = ROPE_THETA ** (-jnp.arange(0, MLA_ROPE, 2, dtype=F32) / MLA_ROPE)
    ang = positions.astype(F32).reshape(t, 1) * inv_freq[None, :]
    rope_cos, rope_sin = jnp.cos(ang), jnp.sin(ang)
    qn, kvn, q_all, kv_all, kpe = _mla_prep(proj_b, q_rank, kv_rank, q_norm_g, kv_norm_g, w_uq_p, w_ukv_p, fw,
                                            rope_cos, rope_sin, "mla_prep")
    mla_ops = (q_all, kv_all, kpe)
    mla_o, mla_lse, gathered[2] = _attn_fwd(mla_ops, None, MLA_QK ** -0.5, bl, s, tq, "mla_attention",
                                            traffic=_Traffic("spread", pieces[legs[2][0]:legs[2][1]]))
    w_up_t, w_down_f = full(5, ff // N_DEV).reshape(ff, d), full(6, ff // N_DEV).reshape(ff, d)

    both = [(d, F32), (d, BF16)]
    cat, x1, h2 = _rows_matmul([(None, w_o_f, "nn")], [fox_o, mla_o, x2d], [fox_out_g, mla_out_g, norm_ffn_g], _residual_norm,
                               [(2 * fw, BF16)] + both, [], "norm_out_proj_out_norm_ffn", prologue=_out_norm)
    act_by_gate, act_by_up, act = _ffn_up(h2, w_gate_t, w_up_t, "ffn_gate_up")
    dx2, dx2_b, dg_final, loss_part = _rows_matmul(
        [(act, w_down_f, "nn")], [x1, loss_target.reshape(t, d)], [final_norm_g.reshape(1, d)], _residual_loss_bwd,
        both, [d, 1], "ffn_down_final_norm_loss")

    d_gate, d_up = _ffn_down_bwd(dx2_b, w_down_f, act_by_gate, act_by_up, "d_ffn_down")
    dw_down = _matmul(act, dx2_b, "tn", BF16, "dw_down", tm=ff // 2, tn=d, tk=2048)
    dw_gate = _matmul(d_gate, h2, "tn", BF16, "dw_gate", tm=ff // 2, tn=d, tk=2048)
    dw_up = _matmul(d_up, h2, "tn", BF16, "dw_up", tm=ff // 2, tn=d, tk=2048)
    dx1, dx1_b, dg_ffn = _rows_matmul([(d_gate, w_gate_t, "nn"), (d_up, w_up_t, "nn")], [x1, dx2], [norm_ffn_g],
                                      _norm_bwd_residual, both, [d], "d_ffn_gate_up_norm_ffn", tm=256)
    dw_o = _matmul(cat, dx1_b, "tn", BF16, "dw_o", tn=d, tk=2048)
    d_fox_o, d_mla_o, fox_delta, mla_delta, dg_fox, dg_mla = _rows_matmul(
        [(dx1_b, w_o_f, "nt")], [fox_o, mla_o], [fox_out_g, mla_out_g], _out_norm_bwd,
        [(fw, BF16), (fw, BF16), (HEADS, F32), (HEADS, F32)], [fw, fw], "d_proj_out_norm_out")

    per_dev = lambda a: a.reshape(N_DEV, -1, d)
    late_grads = [per_dev(dw_o), per_dev(dw_gate), per_dev(dw_up), per_dev(dw_down)]
    dproj_a, dc, g_late = _attn_bwd((proj_a,), d_fox_o, fox_lse, per_head_rows(fox_delta),
                                    c_bias, HEAD_DIM ** -0.5, bl, s, tq, "d_fox_attention", traffic=_Traffic("swap", late_grads))
    dz, db_fgate = _fgate_bwd(z, bcol, dc.reshape(bh * grp, LANES), grp, "d_forget_gate")
    d_flogit = dz.reshape(bl, HEADS, s).transpose(0, 2, 1).reshape(t, HEADS)

    dq_nope, dkv_all, dq_pe, dk_pe = _attn_bwd(mla_ops, d_mla_o, mla_lse, per_head_rows(mla_delta),
                                               None, MLA_QK ** -0.5, bl, s, tq, "d_mla_attention")
    d_tail = jnp.pad(d_flogit, ((0, 0), (0, b_cols - o_flogit - HEADS)))
    dproj_b, dq_rot, dg_q, dg_kv = _mla_prep_bwd(dq_nope, dq_pe, dkv_all, dk_pe, d_tail, proj_b, q_rank, kv_rank,
                                                 q_norm_g, kv_norm_g, w_uq_p, w_ukv_p, rope_cos, rope_sin, "d_mla_prep")
    dw_uq_nope = _matmul(dq_nope, qn, "tn", BF16, "dw_uq_nope", tn=q_rank, tk=1024)
    dw_uq_pe = _matmul(dq_rot, qn, "tn", BF16, "dw_uq_rope", tn=q_rank, tk=1024)
    dw_ukv_p = _matmul(dkv_all, kvn, "tn", BF16, "dw_ukv", tn=kv_rank, tk=1024)
    dw_in_a = _matmul(dproj_a, h1, "tn", BF16, "dw_in_fox", tm=6 * LANES, tn=d, tk=2048)
    dw_in_b = _matmul(dproj_b, h1, "tn", F32, "dw_in_mla", tm=b_cols, tn=d, tk=1024)

    dw_krope = dw_in_b[o_krope:o_flogit].reshape(2, 4, half, d).sum(axis=1).reshape(MLA_ROPE, d)
    dw_in_t = jnp.concatenate([dw_in_a.reshape(PAIRS, 3, LANES, d).transpose(1, 0, 2, 3).reshape(n_qkv, d),
                               dw_in_b[o_flogit:o_flogit + HEADS].astype(BF16), dw_in_b[:o_krope].astype(BF16),
                               dw_krope.astype(BF16)], axis=0)
    pad_dev = lambda a, rows: jnp.pad(a, ((0, 0), (0, rows - a.shape[1]), (0, 0)))
    dw_uq_pe5 = dw_uq_pe.reshape(2, 2, 4, half, q_rank)
    dw_uq_h = jnp.concatenate([dw_uq_nope.reshape(HEADS, HEAD_DIM, q_rank), dw_uq_pe5[:, 0].reshape(HEADS, half, q_rank),
                               dw_uq_pe5[:, 1].reshape(HEADS, half, q_rank)], axis=1)
    dw_ukv_h = dw_ukv_p.reshape(PAIRS, 2, 2, HEAD_DIM, kv_rank).transpose(0, 2, 1, 3, 4).reshape(HEADS, 2 * HEAD_DIM, kv_rank)
    n_last = 3
    last_grads = [pad_dev(per_dev(dw_in_t), pieces[0].shape[0]), pad_dev(per_dev(dw_uq_h), pieces[1].shape[0]), per_dev(dw_ukv_h)]
    grad_x, dg_mix, g_last = _rows_matmul([(dproj_a, w_in_a, "nn"), (dproj_b, w_in_b, "nn")], [x2d, dx1], [norm_mix_g],
                                          _norm_bwd_residual, [(d, F32)], [d], "d_proj_in_norm_mix",
                                          traffic=_Traffic("swap", last_grads))
    g_last = _sum_blocks(g_last, "sum_last_grads")
    g_late = _sum_blocks(g_late, "sum_late_grads")

    def mine(i, rows):
        src, base = (g_last, 0) if i < n_last else (g_late, offs[n_last])
        return src[offs[i] - base:offs[i] - base + rows]

    big = [
        ("w_in", w_in, m_w_in, v_w_in, mine(0, in_cols), True),
        ("w_uq", w_uq, m_w_uq, v_w_uq, mine(1, uq_rows).reshape(-1, q_rank), True),
        ("w_ukv", w_ukv, m_w_ukv, v_w_ukv, mine(2, ukv_rows).reshape(-1, kv_rank), True),
        ("w_o", w_o, m_w_o, v_w_o, mine(3, w_o.shape[1]), False),
        ("w_gate", w_gate, m_w_gate, v_w_gate, mine(4, ff // N_DEV), True),
        ("w_up", w_up, m_w_up, v_w_up, mine(5, ff // N_DEV), True),
        ("w_down", w_down, m_w_down, v_w_down, mine(6, ff // N_DEV), False),
    ]
    out = {}
    for nm, w, m, v, g, transposed in big:
        lay = (lambda a: a[0].T) if transposed else (lambda a: a[0])
        back = (lambda a: a.T[None]) if transposed else (lambda a: a[None])
        dl, new_m, new_v = _adamw(lay(w), g, lay(m), lay(v), "adamw_" + nm)
        out[nm] = (back(g), back(dl), back(new_m), back(new_v))

    smalls = [("norm_mix_g", norm_mix_g, m_norm_mix_g, v_norm_mix_g, dg_mix),
              ("b_fgate", b_fgate, m_b_fgate, v_b_fgate, db_fgate.reshape(1, HEADS)),
              ("q_norm_g", q_norm_g, m_q_norm_g, v_q_norm_g, dg_q),
              ("kv_norm_g", kv_norm_g, m_kv_norm_g, v_kv_norm_g, dg_kv),
              ("fox_out_g", fox_out_g, m_fox_out_g, v_fox_out_g, dg_fox),
              ("mla_out_g", mla_out_g, m_mla_out_g, v_mla_out_g, dg_mla),
              ("norm_ffn_g", norm_ffn_g, m_norm_ffn_g, v_norm_ffn_g, dg_ffn),
              ("final_norm_g", final_norm_g, m_final_norm_g, v_final_norm_g, dg_final)]
    flat = lambda a: a.reshape(1, -1)
    results, loss = _small_all_reduce_adamw([e[4] for e in smalls], loss_part, [flat(e[1]) for e in smalls],
                                            [flat(e[2]) for e in smalls], [flat(e[3]) for e in smalls], "reduce_small_adamw")
    for (nm, w, _, _, _), res in zip(smalls, results):
        out[nm] = tuple(a.reshape(w.shape) for a in res)
    loss = loss[0, 0]

    order = ["norm_mix_g", "w_in", "b_fgate", "q_norm_g", "w_uq", "kv_norm_g", "w_ukv", "fox_out_g", "mla_out_g", "w_o",
             "norm_ffn_g", "w_gate", "w_up", "w_down", "final_norm_g"]
    return (loss, grad_x.reshape(bl, s, d), *[out[n][0] for n in order], *[out[n][1] for n in order],
            *[out[n][2] for n in order], *[out[n][3] for n in order])
```

```python
import math

import jax
import jax.numpy as jnp
from jax import lax
from jax.experimental import pallas as pl
from jax.experimental.pallas import tpu as pltpu

F32 = jnp.float32
BF16 = jnp.bfloat16
MESH = pl.DeviceIdType.MESH

N_DEV = 8
HEADS = 8
HEAD_DIM = 64
PAIRS = HEADS // 2
MLA_ROPE = 32
MLA_QK = HEAD_DIM + MLA_ROPE
ROPE_THETA = 10000.0
NORM_EPS = 1e-6
ADAM_LR, ADAM_B1, ADAM_B2, ADAM_EPS, ADAM_WD, ADAM_STEP = 0.001, 0.9, 0.999, 1e-08, 0.01, 10

LANES = 128
MASKED = -1e30
VMEM_LIMIT = 48 * 1024 * 1024

_DIMS = {"nn": (((1,), (0,)), ((), ())), "nt": (((1,), (1,)), ((), ())), "tn": (((0,), (0,)), ((), ()))}


def _params(*sem):
    return pltpu.CompilerParams(dimension_semantics=sem, vmem_limit_bytes=VMEM_LIMIT)


def _dot(a, b, mode):
    return lax.dot_general(a.astype(BF16), b.astype(BF16), _DIMS[mode], preferred_element_type=F32)


def _tile(n, pref, unit=8):
    if n <= pref:
        return n
    t = pref - pref % unit
    while n % t:
        t -= unit
    return t


def _log2(n):
    assert n & (n - 1) == 0
    return n.bit_length() - 1


def _matmul(a, b, mode, out_dtype, name, tm=512, tn=512, tk=None):
    if mode == "nn":
        (m, kd), n = a.shape, b.shape[1]
    elif mode == "nt":
        (m, kd), n = a.shape, b.shape[0]
    else:
        (kd, m), n = a.shape, b.shape[1]
    tm, tn = _tile(m, tm, LANES if mode == "tn" else 16), _tile(n, tn, LANES)
    tk = kd if tk is None else _tile(kd, tk, LANES)
    nk = kd // tk
    a_spec = pl.BlockSpec((tk, tm), lambda i, j, k: (k, i)) if mode == "tn" else pl.BlockSpec((tm, tk), lambda i, j, k: (i, k))
    b_spec = pl.BlockSpec((tn, tk), lambda i, j, k: (j, k)) if mode == "nt" else pl.BlockSpec((tk, tn), lambda i, j, k: (k, j))
    o_spec = pl.BlockSpec((tm, tn), lambda i, j, k: (i, j))

    def body(a_ref, b_ref, o_ref, *acc):
        part = _dot(a_ref[...], b_ref[...], mode)
        if nk == 1:
            o_ref[...] = part.astype(out_dtype)
        else:
            acc_ref, k = acc[0], pl.program_id(2)

            @pl.when(k == 0)
            def _():
                acc_ref[...] = part

            @pl.when(k > 0)
            def _():
                acc_ref[...] += part

            @pl.when(k == nk - 1)
            def _():
                o_ref[...] = acc_ref[...].astype(out_dtype)

    return pl.pallas_call(
        body, name=name, grid=(m // tm, n // tn, nk), in_specs=[a_spec, b_spec], out_specs=o_spec,
        out_shape=jax.ShapeDtypeStruct((m, n), out_dtype),
        scratch_shapes=[pltpu.VMEM((tm, tn), F32)] if nk > 1 else [],
        compiler_params=_params("parallel", "parallel", "arbitrary"),
    )(a, b)


def _rstd(x):
    return lax.rsqrt(jnp.mean(x * x, axis=-1, keepdims=True) + NORM_EPS)


def _norm_bwd(x, g, dy):
    r = _rstd(x)
    xh = x * r
    u = dy * g
    dx = r * (u - xh * jnp.mean(u * xh, axis=-1, keepdims=True))
    return dx, jnp.sum(dy * xh, axis=0, keepdims=True)


def _rmsnorm(x, col, width, g, out_dtype, name, traffic=None):
    t = x.shape[0]
    tm = _tile(t, 512)
    steps = t // tm
    n_carried = len(traffic.pieces) if traffic else 0

    def body(*refs):
        x_ref, g_ref, o_ref = refs[0], refs[1], refs[2 + n_carried]
        if traffic:
            carried_in, carried_out, sems = refs[2:2 + n_carried], refs[3 + n_carried], refs[4 + n_carried:]

            @pl.when(pl.program_id(0) == 0)
            def _():
                traffic.start(carried_in, carried_out, *sems)

            if isinstance(traffic, _Relay):
                @pl.when(pl.program_id(0) == steps - 1)
                def _():
                    traffic.relay(carried_out, *sems)

        xv = x_ref[...]
        o_ref[...] = ((xv * _rstd(xv)) * g_ref[...]).astype(out_dtype)
        if traffic:
            @pl.when(pl.program_id(0) == steps - 1)
            def _():
                traffic.wait(carried_out, *sems)

    in_specs = [pl.BlockSpec((tm, width), lambda i: (i, col)), pl.BlockSpec((1, width), lambda i: (0, 0))]
    out_specs = [pl.BlockSpec((tm, width), lambda i: (i, 0))]
    out_shape = [jax.ShapeDtypeStruct((t, width), out_dtype)]
    if traffic:
        in_specs += traffic.in_specs
        out_specs.append(traffic.out_spec)
        out_shape.append(traffic.out_shape)
    out = pl.pallas_call(
        body, name=name, grid=(steps,), in_specs=in_specs, out_specs=tuple(out_specs), out_shape=tuple(out_shape),
        scratch_shapes=traffic.scratch if traffic else [],
        compiler_params=_params("arbitrary" if traffic else "parallel"),
    )(x, g, *(traffic.pieces if traffic else []))
    return out if traffic else out[0]


def _split3(x):
    hi = x.astype(BF16)
    r1 = x - hi.astype(F32)
    mid = r1.astype(BF16)
    lo = (r1 - mid.astype(F32)).astype(BF16)
    return hi, mid, lo


def _dot_x01(x, m01):
    hi, mid, lo = _split3(x)
    d = lambda p: lax.dot_general(p, m01, _DIMS["nn"], preferred_element_type=F32)
    return (d(lo) + d(mid)) + d(hi)


def _dot_01x(m01, x):
    hi, mid, lo = _split3(x)
    d = lambda p: lax.dot_general(m01, p, _DIMS["nn"], preferred_element_type=F32)
    return (d(lo) + d(mid)) + d(hi)


def _rows_matmul(terms, rows_in, vecs_in, epilogue, rows_out, sums_out, name, tm=512, prologue=None, traffic=None):
    t = rows_in[0].shape[0]
    tm = _tile(t, tm, 16)
    steps = t // tm
    n_rows, n_vecs = len(rows_in), len(vecs_in)
    n_ab = sum(1 + (a is not None) for a, _, _ in terms)
    n_carried = len(traffic.pieces) if traffic else 0
    halves = [slice(0, tm // 2), slice(tm // 2, tm)] if tm % 32 == 0 else [slice(0, tm)]

    def body(*refs):
        vecs = [r[...] for r in refs[n_ab + n_rows:n_ab + n_rows + n_vecs]]
        out_at = n_ab + n_rows + n_vecs + n_carried
        sum_refs = refs[out_at + len(rows_out):out_at + len(rows_out) + len(sums_out)]
        if traffic:
            carried_in, carried_out, sems = refs[out_at - n_carried:out_at], refs[len(refs) - 4], refs[len(refs) - 3:]

            @pl.when(pl.program_id(0) == 0)
            def _():
                traffic.start(carried_in, carried_out, *sems)

        @pl.when(pl.program_id(0) == 0)
        def _():
            for ref in sum_refs:
                ref[...] = jnp.zeros_like(ref)

        staged = []
        for rows_of in halves:
            row_blocks = [r[rows_of, :] for r in refs[n_ab:n_ab + n_rows]]
            made = prologue(row_blocks, vecs) if prologue else None
            acc, at = None, 0
            for a, _, mode in terms:
                lhs = made if a is None else refs[at][rows_of, :]
                at += a is not None
                part = _dot(lhs, refs[at][...], mode)
                at += 1
                acc = part if acc is None else acc + part
            staged.append((rows_of, row_blocks, made, acc))
        for rows_of, row_blocks, made, acc in staged:
            row_vals, sum_vals = epilogue(acc, row_blocks, vecs)
            if prologue:
                row_vals = [made] + row_vals
            for ref, val, (_, dtype) in zip(refs[out_at:], row_vals, rows_out):
                ref[rows_of, :] = val.astype(dtype)
            for ref, val in zip(sum_refs, sum_vals):
                ref[...] += val
        if traffic:
            @pl.when(pl.program_id(0) == steps - 1)
            def _():
                traffic.wait(carried_out, *sems)

    rows = lambda w: pl.BlockSpec((tm, w), lambda i: (i, 0))
    whole = lambda a: pl.BlockSpec(a.shape, lambda i: (0, 0))
    in_specs, args = [], []
    for a, b, _ in terms:
        in_specs += ([rows(a.shape[1])] if a is not None else []) + [whole(b)]
        args += ([a] if a is not None else []) + [b]
    in_specs += [rows(r.shape[1]) for r in rows_in] + [whole(v) for v in vecs_in]
    args += list(rows_in) + list(vecs_in)
    out_specs = [rows(w) for w, _ in rows_out] + [pl.BlockSpec((1, w), lambda i: (0, 0)) for w in sums_out]
    out_shape = [jax.ShapeDtypeStruct((t, w), dt) for w, dt in rows_out] + [jax.ShapeDtypeStruct((1, w), F32) for w in sums_out]
    if traffic:
        in_specs += traffic.in_specs
        args += traffic.pieces
        out_specs.append(traffic.out_spec)
        out_shape.append(traffic.out_shape)
    return pl.pallas_call(
        body, name=name, grid=(steps,), in_specs=in_specs, out_specs=tuple(out_specs), out_shape=tuple(out_shape),
        scratch_shapes=traffic.scratch if traffic else [], compiler_params=_params("arbitrary"),
    )(*args)


def _out_norm(rows, vecs):
    (f, m), (gf, gm) = rows[:2], vecs[:2]
    return jnp.concatenate([((f * _rstd(f)) * gf).astype(BF16), ((m * _rstd(m)) * gm).astype(BF16)], axis=1)


def _residual_norm(acc, rows, vecs):
    x1 = rows[-1] + acc
    return [x1, (x1 * _rstd(x1)) * vecs[-1]], []


def _residual_loss_bwd(acc, rows, vecs):
    x2, gv = rows[0] + acc, vecs[0]
    diff = (x2 * _rstd(x2)) * gv - rows[1]
    dx, dg = _norm_bwd(x2, gv, diff / x2.shape[1])
    return [dx, dx], [dg, 0.5 * jnp.sum(jnp.mean(diff * diff, axis=-1, keepdims=True), axis=0, keepdims=True)]


def _norm_bwd_residual(acc, rows, vecs):
    dy = acc + rows[2] if len(rows) > 2 else acc
    dx, dg = _norm_bwd(rows[0], vecs[0], dy)
    if len(rows) > 1:
        dx = dx + rows[1]
    return [dx, dx], [dg]


def _out_norm_bwd(acc, rows, vecs):
    (f, m), w = rows, rows[0].shape[1]
    nh = w // HEAD_DIM
    lane_head = lax.shift_right_logical(lax.broadcasted_iota(jnp.int32, (w, nh), 0), _log2(HEAD_DIM))
    sel = (lane_head == lax.broadcasted_iota(jnp.int32, (w, nh), 1)).astype(BF16)
    dfo, dgf = _norm_bwd(f, vecs[0], acc[:, :w])
    dmo, dgm = _norm_bwd(m, vecs[1], acc[:, w:])
    return [dfo, dmo, _dot_x01(dfo * f, sel), _dot_x01(dmo * m, sel)], [dgf, dgm]


def _ffn_up(h, wg_t, wu_t, name, tm=512, tf=1408):
    t, d = h.shape
    f = wg_t.shape[0]
    tm, tf = _tile(t, tm, 16), _tile(f, tf, LANES)
    tok = pl.BlockSpec((tm, tf), lambda j, i: (i, j))
    wt = pl.BlockSpec((tf, d), lambda j, i: (j, 0))

    def body(h_ref, wg_ref, wu_ref, dg_ref, du_ref, a_ref):
        hv = h_ref[...]
        g, u = _dot(hv, wg_ref[...], "nt"), _dot(hv, wu_ref[...], "nt")
        sg = jax.nn.sigmoid(g)
        silu = g * sg
        dg_ref[...] = (u * (sg * (1.0 + g * (1.0 - sg)))).astype(BF16)
        du_ref[...] = silu.astype(BF16)
        a_ref[...] = (silu * u).astype(BF16)

    return pl.pallas_call(
        body, name=name, grid=(f // tf, t // tm), in_specs=[pl.BlockSpec((tm, d), lambda j, i: (i, 0)), wt, wt],
        out_specs=(tok, tok, tok),
        out_shape=(jax.ShapeDtypeStruct((t, f), BF16), jax.ShapeDtypeStruct((t, f), BF16), jax.ShapeDtypeStruct((t, f), BF16)),
        compiler_params=_params("parallel", "parallel"),
    )(h, wg_t, wu_t)


def _ffn_down_bwd(dy, w_down, act_by_gate, act_by_up, name, tm=512, tf=1408):
    t, d = dy.shape
    f = w_down.shape[0]
    tm, tf = _tile(t, tm, 16), _tile(f, tf, LANES)
    tok = pl.BlockSpec((tm, tf), lambda j, i: (i, j))

    def body(dy_ref, w_ref, g_ref, u_ref, dg_ref, du_ref):
        da = _dot(dy_ref[...], w_ref[...], "nt")
        dg_ref[...] = (da * g_ref[...].astype(F32)).astype(BF16)
        du_ref[...] = (da * u_ref[...].astype(F32)).astype(BF16)

    return pl.pallas_call(
        body, name=name, grid=(f // tf, t // tm),
        in_specs=[pl.BlockSpec((tm, d), lambda j, i: (i, 0)), pl.BlockSpec((tf, d), lambda j, i: (j, 0)), tok, tok],
        out_specs=(tok, tok),
        out_shape=(jax.ShapeDtypeStruct((t, f), BF16), jax.ShapeDtypeStruct((t, f), BF16)),
        compiler_params=_params("parallel", "parallel"),
    )(dy, w_down, act_by_gate, act_by_up)


def _chunk_scan_mats(rows, grp, reverse):
    ii = lax.broadcasted_iota(jnp.int32, (LANES, LANES), 0)
    jj = lax.broadcasted_iota(jnp.int32, (LANES, LANES), 1)
    within = ((ii >= jj) if reverse else (ii <= jj)).astype(BF16)
    ones = jnp.ones((LANES, LANES), BF16)
    ri = lax.broadcasted_iota(jnp.int32, (rows, rows), 0)
    rj = lax.broadcasted_iota(jnp.int32, (rows, rows), 1)
    sh = _log2(grp)
    same = lax.shift_right_logical(ri, sh) == lax.shift_right_logical(rj, sh)
    across = (same & ((rj > ri) if reverse else (rj < ri))).astype(BF16)
    return within, ones, across


def _running_sum(v, mats):
    within, ones, across = mats
    return _dot_x01(v, within) + _dot_01x(across, _dot_x01(v, ones))


def _fgate(z, bcol, grp, name):
    rows = z.shape[0]

    def body(z_ref, b_ref, c_ref):
        zz = z_ref[...] + b_ref[...]
        log_f = jnp.minimum(zz, 0.0) - jnp.log1p(jnp.exp(-jnp.abs(zz)))
        c_ref[...] = _running_sum(log_f, _chunk_scan_mats(rows, grp, False))

    return pl.pallas_call(body, name=name, out_shape=jax.ShapeDtypeStruct(z.shape, F32),
                          compiler_params=pltpu.CompilerParams(vmem_limit_bytes=VMEM_LIMIT))(z, bcol)


def _fgate_bwd(z, bcol, dc, grp, name):
    rows = z.shape[0]

    def body(z_ref, b_ref, dc_ref, dz_ref, db_ref):
        zz = z_ref[...] + b_ref[...]
        dz = _running_sum(dc_ref[...], _chunk_scan_mats(rows, grp, True)) * jax.nn.sigmoid(-zz)
        dz_ref[...] = dz
        head = lax.shift_right_logical(lax.broadcasted_iota(jnp.int32, (HEADS, rows), 1), _log2(grp)) & (HEADS - 1)
        sel = (head == lax.broadcasted_iota(jnp.int32, (HEADS, rows), 0)).astype(BF16)
        db_ref[...] = jnp.sum(_dot_01x(sel, dz), axis=1, keepdims=True)

    return pl.pallas_call(
        body, name=name,
        out_shape=(jax.ShapeDtypeStruct(z.shape, F32), jax.ShapeDtypeStruct((HEADS, 1), F32)),
        compiler_params=pltpu.CompilerParams(vmem_limit_bytes=VMEM_LIMIT),
    )(z, bcol, dc)


def _rotate(x, cs, sn_signed):
    return x * cs + pltpu.roll(x, LANES // 2, axis=1) * sn_signed


def _rope_tables(cos, sin):
    half = cos.shape[1]
    freq = lax.broadcasted_iota(jnp.int32, (half, LANES), 0)
    lane = lax.broadcasted_iota(jnp.int32, (half, LANES), 1)
    hit = (lane & (half - 1)) == freq
    sign = jnp.where(lane < LANES // 2, -1.0, 1.0)
    return _dot_x01(cos, hit.astype(BF16)), _dot_x01(sin, jnp.where(hit, sign, 0.0).astype(BF16))


def _mla_prep(proj_b, q_rank, kv_rank, gq, gkv, w_uq_p, w_ukv_p, nope, cs, sn, name):
    t, bw = proj_b.shape
    qw, kvw = w_uq_p.shape[0], w_ukv_p.shape[0]
    tm = _tile(t, 512)
    rows = lambda w: pl.BlockSpec((tm, w), lambda i: (i, 0))
    whole = lambda a: pl.BlockSpec(a.shape, lambda i: (0, 0))

    def body(pb_ref, gq_ref, gkv_ref, wq_ref, wkv_ref, c_ref, s_ref, qn_ref, kvn_ref, q_ref, kv_ref, kpe_ref):
        c, s = _rope_tables(c_ref[...], s_ref[...])
        ql, kvl = pb_ref[:, :q_rank], pb_ref[:, q_rank:q_rank + kv_rank]
        qn = ((ql * _rstd(ql)) * gq_ref[...]).astype(BF16)
        kvn = ((kvl * _rstd(kvl)) * gkv_ref[...]).astype(BF16)
        qn_ref[...], kvn_ref[...] = qn, kvn
        q_raw = _dot(qn, wq_ref[...], "nt")
        q_ref[:, :nope] = q_raw[:, :nope].astype(BF16)
        for off in range(nope, qw, LANES):
            q_ref[:, off:off + LANES] = _rotate(q_raw[:, off:off + LANES], c, s).astype(BF16)
        kv_ref[...] = _dot(kvn, wkv_ref[...], "nt").astype(BF16)
        kpe_ref[...] = _rotate(pb_ref[:, q_rank + kv_rank:q_rank + kv_rank + LANES], c, s).astype(BF16)

    return pl.pallas_call(
        body, name=name, grid=(t // tm,),
        in_specs=[rows(bw), whole(gq), whole(gkv), whole(w_uq_p), whole(w_ukv_p), rows(cs.shape[1]), rows(sn.shape[1])],
        out_specs=(rows(q_rank), rows(kv_rank), rows(qw), rows(kvw), rows(LANES)),
        out_shape=(jax.ShapeDtypeStruct((t, q_rank), BF16), jax.ShapeDtypeStruct((t, kv_rank), BF16),
                   jax.ShapeDtypeStruct((t, qw), BF16), jax.ShapeDtypeStruct((t, kvw), BF16), jax.ShapeDtypeStruct((t, LANES), BF16)),
        compiler_params=_params("parallel"),
    )(proj_b, gq, gkv, w_uq_p, w_ukv_p, cs, sn)


def _mla_prep_bwd(dq_nope, dq_pe, dkv_all, dk_pe, d_tail, proj_b, q_rank, kv_rank, gq, gkv, w_uq_p, w_ukv_p, cs, sn, name):
    t, bw = proj_b.shape
    nope, pw = dq_nope.shape[1], dq_pe.shape[1]
    tm = _tile(t, 512)
    rows = lambda w: pl.BlockSpec((tm, w), lambda i: (i, 0))
    whole = lambda a: pl.BlockSpec(a.shape, lambda i: (0, 0))
    o_k = q_rank + kv_rank

    def body(dqn_ref, dqp_ref, dkv_ref, dkp_ref, dt_ref, pb_ref, gq_ref, gkv_ref, wq_ref, wkv_ref, c_ref, s_ref,
             dpb_ref, dqr_ref, dgq_ref, dgkv_ref):
        c, s = _rope_tables(c_ref[...], -s_ref[...])
        for off in range(0, pw, LANES):
            dqr_ref[:, off:off + LANES] = _rotate(dqp_ref[:, off:off + LANES], c, s).astype(BF16)
        d_qn = _dot(dqn_ref[...], wq_ref[:nope, :], "nn") + _dot(dqr_ref[...], wq_ref[nope:, :], "nn")
        dq_lat, dgq = _norm_bwd(pb_ref[:, :q_rank], gq_ref[...], d_qn)
        dkv_lat, dgkv = _norm_bwd(pb_ref[:, q_rank:o_k], gkv_ref[...], _dot(dkv_ref[...], wkv_ref[...], "nn"))
        dpb_ref[:, :q_rank] = dq_lat.astype(BF16)
        dpb_ref[:, q_rank:o_k] = dkv_lat.astype(BF16)
        dpb_ref[:, o_k:o_k + LANES] = _rotate(dkp_ref[...], c, s).astype(BF16)
        dpb_ref[:, o_k + LANES:] = dt_ref[...].astype(BF16)

        @pl.when(pl.program_id(0) == 0)
        def _():
            dgq_ref[...] = jnp.zeros_like(dgq_ref)
            dgkv_ref[...] = jnp.zeros_like(dgkv_ref)

        dgq_ref[...] += dgq
        dgkv_ref[...] += dgkv

    return pl.pallas_call(
        body, name=name, grid=(t // tm,),
        in_specs=[rows(nope), rows(pw), rows(dkv_all.shape[1]), rows(LANES), rows(bw - o_k - LANES), rows(bw), whole(gq), whole(gkv),
                  whole(w_uq_p), whole(w_ukv_p), rows(cs.shape[1]), rows(sn.shape[1])],
        out_specs=(rows(bw), rows(pw), whole(gq), whole(gkv)),
        out_shape=(jax.ShapeDtypeStruct((t, bw), BF16), jax.ShapeDtypeStruct((t, pw), BF16),
                   jax.ShapeDtypeStruct(gq.shape, F32), jax.ShapeDtypeStruct(gkv.shape, F32)),
        compiler_params=_params("arbitrary"),
    )(dq_nope, dq_pe, dkv_all, dk_pe, d_tail, proj_b, gq, gkv, w_uq_p, w_ukv_p, cs, sn)


def _lane_masks(pair, h, pe):
    lane = lax.broadcasted_iota(jnp.int32, (1, LANES), 1)
    in_head = lax.shift_right_logical(lane, _log2(HEAD_DIM)) == h
    in_rope = ((lax.shift_right_logical(lane, _log2(MLA_ROPE // 2)) & 3) == ((2 * pair + h) & 3)) if pe else None
    return in_head, in_rope


def _keep(mask, v):
    return jnp.where(mask, v, jnp.zeros_like(v))


def _to_row(col):
    n = col.shape[0]
    eye = lax.broadcasted_iota(jnp.int32, (n, n), 0) == lax.broadcasted_iota(jnp.int32, (n, n), 1)
    return jnp.sum(jnp.where(eye, col, 0.0), axis=0, keepdims=True)


def _to_col(row):
    n = row.shape[1]
    eye = lax.broadcasted_iota(jnp.int32, (n, n), 0) == lax.broadcasted_iota(jnp.int32, (n, n), 1)
    return jnp.sum(jnp.where(eye, row, 0.0), axis=1, keepdims=True)


def _first_step():
    return (pl.program_id(0) == 0) & (pl.program_id(1) == 0)


def _last_step(n0, n1):
    return (pl.program_id(0) == n0 - 1) & (pl.program_id(1) == n1 - 1)


def _attn_fwd(ops, bias, scale, bl, s, tq, name, traffic=None):
    pe = len(ops) == 3
    has_bias = bias is not None
    exact_scale = math.frexp(scale)[0] == 0.5
    span = 4 * tq
    nq = s // tq
    t = bl * s
    n_carried = len(traffic.pieces) if traffic else 0

    def body(*refs):
        sems = refs[len(refs) - 3:] if traffic else ()
        if pe:
            q_ref, qpe_ref, kv_ref, kpe_ref = refs[:4]
            n_in = 4
            q_at = lambda r0, r1: q_ref[r0:r1, :]
            v_at = lambda r0, r1: kv_ref[r0:r1, LANES:]
            kcat = refs[len(refs) - 1 - len(sems)]
            kcat[:, :LANES] = kv_ref[:, :LANES]
            kcat[:, LANES:] = kpe_ref[...]
            k_at = lambda r0, r1: kcat[r0:r1, :]
        else:
            qkv_ref = refs[0]
            n_in = 1
            q_at = lambda r0, r1: qkv_ref[r0:r1, :LANES]
            k_at = lambda r0, r1: qkv_ref[r0:r1, LANES:2 * LANES]
            v_at = lambda r0, r1: qkv_ref[r0:r1, 2 * LANES:]
        if has_bias:
            c_ref = refs[n_in]
            n_in += 1
        carried_in = refs[n_in:n_in + n_carried]
        n_in += n_carried
        o_ref, lse_ref = refs[n_in:n_in + 2]
        if traffic:
            carried_out = refs[n_in + 2]

            @pl.when(_first_step())
            def _():
                traffic.start(carried_in, carried_out, *sems)

        pair = pl.program_id(1)
        causal = lax.broadcasted_iota(jnp.int32, (tq, tq), 1) <= lax.broadcasted_iota(jnp.int32, (tq, tq), 0)
        o_ref[...] = jnp.zeros_like(o_ref)

        masks = [_lane_masks(pair, h, pe) for h in range(2)]

        def logits(i):
            r0, r1 = i * tq, (i + 1) * tq
            out = []
            for h in range(2):
                in_head, in_rope = masks[h]
                qm = _keep(in_head, q_at(r0, r1))
                if pe:
                    qm = jnp.concatenate([qm, _keep(in_rope, qpe_ref[r0:r1, :])], axis=1)
                if exact_scale:
                    qm = qm * scale
                spans = []
                for k0, k1 in [(r0, r1)] + [(k, min(k + span, r0)) for k in range(0, r0, span)]:
                    sc = _dot(qm, k_at(k0, k1), "nt")
                    if not exact_scale:
                        sc = sc * scale
                    if has_bias:
                        sc = sc - c_ref[h, :, k0:k1]
                    spans.append((k0, k1, jnp.where(causal, sc, MASKED) if k0 == r0 else sc))
                out.append(spans)
            return out

        def softmax(per_head):
            out = []
            for spans in per_head:
                m = None
                for _, _, sc in spans:
                    top = jnp.max(sc, axis=1, keepdims=True)
                    m = top if m is None else jnp.maximum(m, top)
                probs = [(k0, k1, jnp.exp(sc - m)) for k0, k1, sc in spans]
                l = sum(jnp.sum(p, axis=1, keepdims=True) for _, _, p in probs)
                out.append((m, l, probs))
            return out

        def weigh(i, per_head):
            r0, r1 = i * tq, (i + 1) * tq
            for h, (m, l, probs) in enumerate(per_head):
                acc = sum(_dot(p, v_at(k0, k1), "nn") for k0, k1, p in probs)
                o_ref[r0:r1, :] = jnp.where(masks[h][0], acc / l, o_ref[r0:r1, :])
                lse = _to_row(m + jnp.log(l))
                lse_ref[h, :, r0:r1] = lse + c_ref[h, :, r0:r1] if has_bias else lse

        ahead = logits(0)
        for i in range(nq):
            solved = softmax(ahead)
            if i + 1 < nq:
                ahead = logits(i + 1)
            weigh(i, solved)

        if traffic:
            @pl.when(_last_step(bl, PAIRS))
            def _():
                traffic.wait(carried_out, *sems)

    seq = lambda w, col: pl.BlockSpec((s, w), col)
    if pe:
        in_specs = [seq(LANES, lambda b, p: (b, p)), seq(LANES, lambda b, p: (b, PAIRS + p // 2)),
                    seq(2 * LANES, lambda b, p: (b, p)), seq(LANES, lambda b, p: (b, 0))]
        args = [ops[0], ops[0], ops[1], ops[2]]
        scratch = [pltpu.VMEM((s, 2 * LANES), BF16)]
    else:
        in_specs = [seq(3 * LANES, lambda b, p: (b, p))]
        args = [ops[0]]
        scratch = []
    per_head_row = pl.BlockSpec((2, 1, s), lambda b, p: (b * PAIRS + p, 0, 0))
    if has_bias:
        in_specs.append(per_head_row)
        args.append(bias)
    out_specs = [seq(LANES, lambda b, p: (b, p)), per_head_row]
    out_shape = [jax.ShapeDtypeStruct((t, HEADS * HEAD_DIM), F32), jax.ShapeDtypeStruct((bl * HEADS, 1, s), F32)]
    if traffic:
        in_specs += traffic.in_specs
        args += traffic.pieces
        out_specs.append(traffic.out_spec)
        out_shape.append(traffic.out_shape)
        scratch += traffic.scratch
    return pl.pallas_call(
        body, name=name, grid=(bl, PAIRS), in_specs=in_specs, out_specs=tuple(out_specs), out_shape=tuple(out_shape),
        scratch_shapes=scratch, compiler_params=_params(*(("arbitrary", "arbitrary") if traffic else ("parallel", "parallel"))),
    )(*args)


def _attn_bwd(ops, do, lse, delta, bias, scale, bl, s, tq, name, traffic=None):
    pe = len(ops) == 3
    has_bias = bias is not None
    exact_scale = math.frexp(scale)[0] == 0.5
    span = 2 * tq
    nq = s // tq
    t = bl * s
    width = 2 * LANES if pe else LANES
    n_carried = len(traffic.pieces) if traffic else 0

    def body(*refs):
        if pe:
            q_ref, qpe_ref, kv_ref, kpe_ref = refs[:4]
            n_in = 4
            k_at = lambda r0, r1: kv_ref[r0:r1, :LANES]
            v_at = lambda r0, r1: kv_ref[r0:r1, LANES:]
        else:
            qkv_ref = refs[0]
            n_in = 1
            k_at = lambda r0, r1: qkv_ref[r0:r1, LANES:2 * LANES]
            v_at = lambda r0, r1: qkv_ref[r0:r1, 2 * LANES:]
        do_ref, lse_ref, dl_ref = refs[n_in:n_in + 3]
        n_in += 3
        if has_bias:
            c_ref = refs[n_in]
            n_in += 1
        carried_in = refs[n_in:n_in + n_carried]
        rest = refs[n_in + n_carried:]
        if traffic:
            rest, sems = rest[:-3], rest[-3:]
            carried_out = rest[4 if pe else 2]
            rest = rest[:4 if pe else 2] + rest[(4 if pe else 2) + 1:]

            @pl.when(_first_step())
            def _():
                traffic.start(carried_in, carried_out, *sems)

        if pe:
            dqn_ref, dkv_ref, dqpe_ref, dkpe_ref, dq_acc, qcat = rest
            qcat[:, :LANES] = q_ref[...]
            qcat[:, LANES:] = qpe_ref[...]
            q_at = lambda r0, r1: qcat[r0:r1, :]
            dkv_ref[...] = jnp.zeros_like(dkv_ref)
        else:
            dqkv_ref, dc_ref, dq_acc = rest
            q_at = lambda r0, r1: qkv_ref[r0:r1, :LANES]
            dqkv_ref[...] = jnp.zeros_like(dqkv_ref)
            dc_ref[...] = jnp.zeros_like(dc_ref)
        pair = pl.program_id(1)
        dq_acc[...] = jnp.zeros_like(dq_acc)
        causal = lax.broadcasted_iota(jnp.int32, (tq, tq), 1) >= lax.broadcasted_iota(jnp.int32, (tq, tq), 0)
        if pe:
            @pl.when(pair == 0)
            def _():
                dkpe_ref[...] = jnp.zeros_like(dkpe_ref)

            @pl.when(pair % 2 == 0)
            def _():
                dqpe_ref[...] = jnp.zeros_like(dqpe_ref)

        masks = [_lane_masks(pair, h, pe) for h in range(2)]

        def logits(j):
            r0, r1 = j * tq, (j + 1) * tq
            units = []
            for h in range(2):
                in_head, in_rope = masks[h]
                kt = _keep(in_head, k_at(r0, r1))
                if pe:
                    kt = jnp.concatenate([kt, _keep(in_rope, kpe_ref[r0:r1, :])], axis=1)
                if exact_scale:
                    kt = kt * scale
                vt = _keep(in_head, v_at(r0, r1))
                ck = _to_col(c_ref[h, :, r0:r1]) if has_bias else None
                for q0, q1, diagonal in [(r0, r1, True)] + [(q, min(q + span, s), False) for q in range(r1, s, span)]:
                    qq, dd = q_at(q0, q1), do_ref[q0:q1, :]
                    st = _dot(kt, qq, "nt")
                    if not exact_scale:
                        st = st * scale
                    shift = lse_ref[h, :, q0:q1]
                    if has_bias:
                        shift = shift - c_ref[h, :, q0:q1]
                        st = st - ck
                    st = st - shift
                    if diagonal:
                        st = jnp.where(causal, st, MASKED)
                    units.append((h, q0, q1, kt, qq, dd, st, _dot(vt, dd, "nt")))
            return units

        def softmax_bwd(units):
            solved = []
            for h, q0, q1, kt, qq, dd, st, dpt in units:
                pt = jnp.exp(st)
                dst = pt * (dpt - dl_ref[h, :, q0:q1])
                solved.append((h, q0, q1, kt, qq, dd, pt, dst, (dst if exact_scale else dst * scale).astype(BF16)))
            return solved

        def products(j, solved):
            r0, r1 = j * tq, (j + 1) * tq
            dv_of, dk_of, cs_of = [None, None], [None, None], [None, None]
            add = lambda old, new: new if old is None else old + new
            for h, q0, q1, kt, qq, dd, pt, dst, dsb in solved:
                dq_acc[q0:q1, :] += _dot(dsb, kt, "tn")
                dv_of[h] = add(dv_of[h], _dot(pt, dd, "nn"))
                dk_of[h] = add(dk_of[h], _dot(dsb, qq, "nn"))
                if has_bias:
                    dc_ref[h, :, q0:q1] += jnp.sum(dst, axis=0, keepdims=True)
                    cs_of[h] = add(cs_of[h], jnp.sum(dst, axis=1, keepdims=True))
            for h in range(2):
                (in_head, in_rope), dv_c, dk_c, cs = masks[h], dv_of[h], dk_of[h], cs_of[h]
                if exact_scale:
                    dk_c = dk_c * scale
                if pe:
                    dkv_ref[r0:r1, :LANES] = jnp.where(in_head, dk_c[:, :LANES].astype(BF16), dkv_ref[r0:r1, :LANES])
                    dkv_ref[r0:r1, LANES:] = jnp.where(in_head, dv_c.astype(BF16), dkv_ref[r0:r1, LANES:])
                    dkpe_ref[r0:r1, :] += _keep(in_rope, dk_c[:, LANES:])
                else:
                    dqkv_ref[r0:r1, LANES:2 * LANES] = jnp.where(in_head, dk_c.astype(BF16), dqkv_ref[r0:r1, LANES:2 * LANES])
                    dqkv_ref[r0:r1, 2 * LANES:] = jnp.where(in_head, dv_c.astype(BF16), dqkv_ref[r0:r1, 2 * LANES:])
                    dc_ref[h, :, r0:r1] -= _to_row(cs)

        units = logits(0)
        for j in range(nq):
            solved = softmax_bwd(units)
            if j + 1 < nq:
                units = logits(j + 1)
            products(j, solved)

        if pe:
            dqn_ref[...] = dq_acc[:, :LANES].astype(BF16)
            dqpe_ref[...] += dq_acc[:, LANES:]
        else:
            dqkv_ref[:, :LANES] = dq_acc[...].astype(BF16)
        if traffic:
            @pl.when(_last_step(bl, PAIRS))
            def _():
                traffic.wait(carried_out, *sems)

    seq = lambda w, col: pl.BlockSpec((s, w), col)
    per_head_row = pl.BlockSpec((2, 1, s), lambda b, p: (b * PAIRS + p, 0, 0))
    if pe:
        in_specs = [seq(LANES, lambda b, p: (b, p)), seq(LANES, lambda b, p: (b, PAIRS + p // 2)),
                    seq(2 * LANES, lambda b, p: (b, p)), seq(LANES, lambda b, p: (b, 0))]
        args = [ops[0], ops[0], ops[1], ops[2]]
    else:
        in_specs = [seq(3 * LANES, lambda b, p: (b, p))]
        args = [ops[0]]
    in_specs += [seq(LANES, lambda b, p: (b, p)), per_head_row, per_head_row]
    args += [do, lse, delta]
    if has_bias:
        in_specs.append(per_head_row)
        args.append(bias)
    scratch = [pltpu.VMEM((s, width), F32)]
    if pe:
        out_specs = (seq(LANES, lambda b, p: (b, p)), seq(2 * LANES, lambda b, p: (b, p)),
                     seq(LANES, lambda b, p: (b, p // 2)), seq(LANES, lambda b, p: (b, 0)))
        out_shape = (jax.ShapeDtypeStruct((t, PAIRS * LANES), BF16), jax.ShapeDtypeStruct((t, PAIRS * 2 * LANES), BF16),
                     jax.ShapeDtypeStruct((t, 2 * LANES), F32), jax.ShapeDtypeStruct((t, LANES), F32))
        scratch.append(pltpu.VMEM((s, 2 * LANES), BF16))
    else:
        out_specs = (seq(3 * LANES, lambda b, p: (b, p)), per_head_row)
        out_shape = (jax.ShapeDtypeStruct((t, PAIRS * 3 * LANES), BF16), jax.ShapeDtypeStruct((bl * HEADS, 1, s), F32))
    if traffic:
        in_specs += traffic.in_specs
        args += traffic.pieces
        out_specs += (traffic.out_spec,)
        out_shape += (traffic.out_shape,)
        scratch += traffic.scratch
    return pl.pallas_call(
        body, name=name, grid=(bl, PAIRS), in_specs=in_specs, out_specs=out_specs, out_shape=out_shape,
        scratch_shapes=scratch, compiler_params=_params("arbitrary" if traffic else "parallel", "arbitrary"),
    )(*args)


def _my_place():
    return lax.axis_index("x"), lax.axis_index("y"), lax.axis_index("c")


def _flip(p, bit):
    return 1 - p if bit else p


def _relative(x, y, c, k):
    return _flip(x, k & 4), _flip(y, k & 2), _flip(c, k & 1)


def _linear(x, y, c):
    return 4 * x + 2 * y + c


class _Traffic:
    def __init__(self, kind, pieces):
        self.kind, self.pieces = kind, list(pieces)
        self.rows = [p.shape[-2] for p in self.pieces]
        self.starts = [sum(self.rows[:i]) for i in range(len(self.rows))]
        anywhere = pl.BlockSpec(memory_space=pl.ANY)
        self.in_specs = [anywhere] * len(self.pieces)
        self.out_spec = anywhere
        self.out_shape = jax.ShapeDtypeStruct((N_DEV, sum(self.rows), self.pieces[0].shape[-1]), self.pieces[0].dtype)
        self.scratch = [pltpu.SemaphoreType.DMA((7,)), pltpu.SemaphoreType.DMA((7,)), pltpu.SemaphoreType.DMA(())]

    def start(self, p_refs, out_ref, send_sems, recv_sems, local_sem):
        x, y, c = _my_place()
        me = _linear(x, y, c)
        mine = lambda i, dev: p_refs[i] if self.kind == "spread" else p_refs[i].at[dev]
        landing = lambda i: out_ref.at[me, pl.ds(self.starts[i], self.rows[i])]
        for i in range(len(p_refs)):
            pltpu.make_async_copy(mine(i, me), landing(i), local_sem).start()
        for k in range(1, N_DEV):
            peer = _relative(x, y, c, k)
            for i in range(len(p_refs)):
                pltpu.make_async_remote_copy(
                    src_ref=mine(i, _linear(*peer)), dst_ref=landing(i),
                    send_sem=send_sems.at[k - 1], recv_sem=recv_sems.at[k - 1], device_id=peer, device_id_type=MESH).start()

    def wait(self, out_ref, send_sems, recv_sems, local_sem):
        x, y, c = _my_place()
        whole = out_ref.at[_linear(x, y, c)]
        for k in range(1, N_DEV):
            both = pltpu.make_async_remote_copy(
                src_ref=whole, dst_ref=whole, send_sem=send_sems.at[k - 1], recv_sem=recv_sems.at[k - 1],
                device_id=_relative(x, y, c, k), device_id_type=MESH)
            both.wait_recv()
            both.wait_send()
        pltpu.make_async_copy(whole, whole, local_sem).wait()


class _Relay(_Traffic):
    def __init__(self, piece):
        super().__init__("spread", [piece])

    @staticmethod
    def _chips(x, y):
        return [(1 - x, y), (x, 1 - y), (1 - x, 1 - y)]

    @staticmethod
    def _copy(k, block, to, out_ref, send_sems, recv_sems, src=None):
        slot = out_ref.at[_linear(*block)]
        return pltpu.make_async_remote_copy(src_ref=slot if src is None else src, dst_ref=slot, send_sem=send_sems.at[k],
                                            recv_sem=recv_sems.at[k], device_id=to, device_id_type=MESH)

    def start(self, p_refs, out_ref, send_sems, recv_sems, local_sem):
        x, y, c = _my_place()
        me, sems = (x, y, c), (out_ref, send_sems, recv_sems)
        pltpu.make_async_copy(p_refs[0], out_ref.at[_linear(*me)], local_sem).start()
        self._copy(0, me, (x, y, 1 - c), *sems, src=p_refs[0]).start()
        for j, chip in enumerate(self._chips(x, y)):
            self._copy(1 + j, me, (*chip, c), *sems, src=p_refs[0]).start()

    def relay(self, out_ref, send_sems, recv_sems, local_sem):
        x, y, c = _my_place()
        sems = (out_ref, send_sems, recv_sems)
        for j, chip in enumerate(self._chips(x, y)):
            self._copy(1 + j, (*chip, c), (x, y, c), *sems).wait_recv()
            self._copy(4 + j, (*chip, c), (x, y, 1 - c), *sems).start()

    def wait(self, out_ref, send_sems, recv_sems, local_sem):
        x, y, c = _my_place()
        me, sems = (x, y, c), (out_ref, send_sems, recv_sems)
        self._copy(0, (x, y, 1 - c), me, *sems).wait_recv()
        for j, chip in enumerate(self._chips(x, y)):
            self._copy(4 + j, (*chip, 1 - c), me, *sems).wait_recv()
        for k in range(N_DEV - 1):
            self._copy(k, me, (x, y, 1 - c), *sems).wait_send()
        whole = out_ref.at[_linear(*me)]
        pltpu.make_async_copy(whole, whole, local_sem).wait()


def _sum_blocks(parts, name):
    n, r, cdim = parts.shape
    tr = _tile(r, 640, 16)

    def body(p_ref, o_ref):
        acc = p_ref[0].astype(F32)
        for d in range(1, n):
            acc = acc + p_ref[d].astype(F32)
        o_ref[...] = acc

    return pl.pallas_call(
        body, name=name, grid=(r // tr,), in_specs=[pl.BlockSpec((n, tr, cdim), lambda i: (0, i, 0))],
        out_specs=pl.BlockSpec((tr, cdim), lambda i: (i, 0)), out_shape=jax.ShapeDtypeStruct((r, cdim), F32),
        compiler_params=_params("parallel"),
    )(parts)


def _adamw_math(w, g, m, v):
    m = ADAM_B1 * m + (1.0 - ADAM_B1) * g
    v = ADAM_B2 * v + (1.0 - ADAM_B2) * (g * g)
    m_hat = m / (1.0 - ADAM_B1 ** ADAM_STEP)
    v_hat = v / (1.0 - ADAM_B2 ** ADAM_STEP)
    delta = -ADAM_LR * (m_hat / (jnp.sqrt(v_hat) + ADAM_EPS) + ADAM_WD * w)
    return delta, m, v


def _adamw(w, g, m, v, name):
    def body(w_ref, g_ref, m_ref, v_ref, d_ref, nm_ref, nv_ref):
        d_ref[...], nm_ref[...], nv_ref[...] = _adamw_math(w_ref[...], g_ref[...], m_ref[...], v_ref[...])

    out = jax.ShapeDtypeStruct(w.shape, F32)
    return pl.pallas_call(body, name=name, out_shape=(out, out, out),
                          compiler_params=pltpu.CompilerParams(vmem_limit_bytes=VMEM_LIMIT))(w, g, m, v)


def _small_all_reduce_adamw(parts, loss_part, ws, ms, vs, name):
    sizes = [p.shape[1] for p in parts] + [1]
    spots = [sum(-(-n // LANES) * LANES for n in sizes[:i]) for i in range(len(sizes))]
    width = spots[-1] + LANES
    k = len(parts)

    def reduce_body(*refs):
        p_refs, tot_ref, rows, send_sems, recv_sems = refs[:k + 1], *refs[k + 1:]
        x, y, c = _my_place()
        me = _linear(x, y, c)
        rows[me] = jnp.zeros((1, width), F32)
        for i in range(k + 1):
            rows[me, :, spots[i]:spots[i] + sizes[i]] = p_refs[i][...]
        copies = []
        for rel in range(1, N_DEV):
            copies.append(pltpu.make_async_remote_copy(
                src_ref=rows.at[me], dst_ref=rows.at[me], send_sem=send_sems.at[rel - 1], recv_sem=recv_sems.at[rel - 1],
                device_id=_relative(x, y, c, rel), device_id_type=MESH))
        for cp in copies:
            cp.start()
        for cp in copies:
            cp.wait_recv()
        for cp in copies:
            cp.wait_send()
        total = rows[0]
        for d in range(1, N_DEV):
            total = total + rows[d]
        tot_ref[...] = total

    total = pl.pallas_call(
        reduce_body, name=name, out_shape=jax.ShapeDtypeStruct((1, width), F32),
        scratch_shapes=[pltpu.VMEM((N_DEV, 1, width), F32), pltpu.SemaphoreType.DMA((7,)), pltpu.SemaphoreType.DMA((7,))],
    )(*parts, loss_part)

    def adamw_body(*refs):
        tot_ref, w_refs, m_refs, v_refs, outs = refs[0], refs[1:k + 1], refs[k + 1:2 * k + 1], refs[2 * k + 1:3 * k + 1], refs[3 * k + 1:]
        for i in range(k):
            g = tot_ref[:, spots[i]:spots[i] + sizes[i]]
            outs[4 * i][...] = g
            outs[4 * i + 1][...], outs[4 * i + 2][...], outs[4 * i + 3][...] = _adamw_math(w_refs[i][...], g, m_refs[i][...], v_refs[i][...])
        outs[4 * k][...] = tot_ref[:, spots[k]:spots[k] + 1]

    out_shape = [jax.ShapeDtypeStruct((1, n), F32) for n in sizes[:k] for _ in range(4)] + [jax.ShapeDtypeStruct((1, 1), F32)]
    res = pl.pallas_call(adamw_body, name=name + "_adamw", out_shape=tuple(out_shape))(total, *ws, *ms, *vs)
    return [res[4 * i:4 * i + 4] for i in range(k)], res[4 * k]


def _pad_rows(a, rows):
    return jnp.pad(a, ((0, rows - a.shape[0]), (0, 0)))


def kernel(x, positions, norm_mix_g, w_in, b_fgate, q_norm_g, w_uq, kv_norm_g, w_ukv, fox_out_g, mla_out_g, w_o, norm_ffn_g, w_gate, w_up, w_down, final_norm_g, loss_target, m_norm_mix_g, m_w_in, m_b_fgate, m_q_norm_g, m_w_uq, m_kv_norm_g, m_w_ukv, m_fox_out_g, m_mla_out_g, m_w_o, m_norm_ffn_g, m_w_gate, m_w_up, m_w_down, m_final_norm_g, v_norm_mix_g, v_w_in, v_b_fgate, v_q_norm_g, v_w_uq, v_kv_norm_g, v_w_ukv, v_fox_out_g, v_mla_out_g, v_w_o, v_norm_ffn_g, v_w_gate, v_w_up, v_w_down, v_final_norm_g):
    bl, s, d = x.shape
    t = bl * s
    bh = bl * HEADS
    tq = _tile(s, 256)
    grp = s // LANES
    fw = HEADS * HEAD_DIM
    q_rank, kv_rank = w_uq.shape[1], w_ukv.shape[1]
    in_cols = w_in.shape[2]
    n_in = N_DEV * in_cols
    ff = N_DEV * w_gate.shape[2]
    half = MLA_ROPE // 2
    o_kvlat, o_krope, o_flogit = q_rank, q_rank + kv_rank, q_rank + kv_rank + LANES
    b_cols = -(-(o_flogit + HEADS) // LANES) * LANES

    tr = lambda w: jnp.transpose(w[0])
    in_rows = -(-in_cols // 16) * 16
    uq_rows = w_uq.shape[2] * q_rank // d
    ukv_rows = w_ukv.shape[2] * kv_rank // d
    pieces = [_pad_rows(tr(w_in), in_rows), _pad_rows(tr(w_uq).reshape(uq_rows, d), -(-uq_rows // 16) * 16),
              tr(w_ukv).reshape(ukv_rows, d), w_o[0], tr(w_gate), tr(w_up), w_down[0]]
    pieces = [p.astype(BF16) for p in pieces]
    offs = [0]
    for p in pieces:
        offs.append(offs[-1] + p.shape[0])
    legs = [(0, 1), (1, 5), (5, 7)]
    gathered = {}

    def full(i, rows):
        leg = next(n for n, (lo, hi) in enumerate(legs) if lo <= i < hi)
        base = offs[legs[leg][0]]
        return gathered[leg][:, offs[i] - base:offs[i] - base + rows]

    x2d = x.reshape(t, d)
    h1, gathered[0] = _rmsnorm(x2d, 0, d, norm_mix_g, BF16, "norm_mix", traffic=_Relay(pieces[0]))

    w_in_t = full(0, in_cols).reshape(n_in, d)
    n_qkv = 3 * fw
    w_in_a = w_in_t[:n_qkv].reshape(3, PAIRS, LANES, d).transpose(1, 0, 2, 3).reshape(n_qkv, d)
    lat0, rope0 = n_qkv + HEADS, n_qkv + HEADS + q_rank + kv_rank
    k_rep = jnp.broadcast_to(w_in_t[rope0:].reshape(2, 1, half, d), (2, 4, half, d)).reshape(LANES, d)
    w_in_b = jnp.concatenate([w_in_t[lat0:rope0], k_rep, w_in_t[n_qkv:lat0],
                              jnp.zeros((b_cols - o_flogit - HEADS, d), BF16)], axis=0)

    def per_head_rows(a):
        return a.reshape(bl, s, HEADS).transpose(0, 2, 1).reshape(bh, 1, s)

    proj_a = _matmul(h1, w_in_a, "nt", BF16, "proj_fox", tm=1024, tn=6 * LANES)
    proj_b = _matmul(h1, w_in_b, "nt", F32, "proj_mla", tm=1024, tn=b_cols)

    z = proj_b[:, o_flogit:o_flogit + HEADS].reshape(bl, s, HEADS).transpose(0, 2, 1).reshape(bh * grp, LANES)
    bcol = jnp.broadcast_to(b_fgate.reshape(1, HEADS, 1), (bl, HEADS, grp)).reshape(bh * grp, 1)
    c = _fgate(z, bcol, grp, "forget_gate")
    c_bias = c.reshape(bh, 1, s)
    fox_o, fox_lse, gathered[1] = _attn_fwd((proj_a,), c_bias, HEAD_DIM ** -0.5, bl, s, tq, "fox_attention",
                                            traffic=_Traffic("spread", pieces[legs[1][0]:legs[1][1]]))
    w_uq_h = full(1, uq_rows).reshape(HEADS, MLA_QK, q_rank)
    w_uq_pe = jnp.concatenate([w_uq_h[:, HEAD_DIM:HEAD_DIM + half].reshape(2, 1, 4 * half, q_rank),
                               w_uq_h[:, HEAD_DIM + half:].reshape(2, 1, 4 * half, q_rank)], axis=1).reshape(2 * LANES, q_rank)
    w_uq_p = jnp.concatenate([w_uq_h[:, :HEAD_DIM].reshape(fw, q_rank), w_uq_pe], axis=0)
    w_ukv_p = full(2, ukv_rows).reshape(PAIRS, 2, 2, HEAD_DIM, kv_rank).transpose(0, 2, 1, 3, 4).reshape(2 * fw, kv_rank)
    w_o_f = full(3, w_o.shape[1]).reshape(-1, d)
    w_gate_t = full(4, ff // N_DEV).reshape(ff, d)

    inv_freq = ROPE_THETA ** (-jnp.arange(0, MLA_ROPE, 2, dtype=F32) / MLA_ROPE)
    ang = positions.astype(F32).reshape(t, 1) * inv_freq[None, :]
    rope_cos, rope_sin = jnp.cos(ang), jnp.sin(ang)
    qn, kvn, q_all, kv_all, kpe = _mla_prep(proj_b, q_rank, kv_rank, q_norm_g, kv_norm_g, w_uq_p, w_ukv_p, fw,
                                            rope_cos, rope_sin, "mla_prep")
    mla_ops = (q_all, kv_all, kpe)
    mla_o, mla_lse, gathered[2] = _attn_fwd(mla_ops, None, MLA_QK ** -0.5, bl, s, tq, "mla_attention",
                                            traffic=_Traffic("spread", pieces[legs[2][0]:legs[2][1]]))
    w_up_t, w_down_f = full(5, ff // N_DEV).reshape(ff, d), full(6, ff // N_DEV).reshape(ff, d)

    both = [(d, F32), (d, BF16)]
    cat, x1, h2 = _rows_matmul([(None, w_o_f, "nn")], [fox_o, mla_o, x2d], [fox_out_g, mla_out_g, norm_ffn_g], _residual_norm,
                               [(2 * fw, BF16)] + both, [], "norm_out_proj_out_norm_ffn", prologue=_out_norm)
    act_by_gate, act_by_up, act = _ffn_up(h2, w_gate_t, w_up_t, "ffn_gate_up")
    dx2, dx2_b, dg_final, loss_part = _rows_matmul(
        [(act, w_down_f, "nn")], [x1, loss_target.reshape(t, d)], [final_norm_g.reshape(1, d)], _residual_loss_bwd,
        both, [d, 1], "ffn_down_final_norm_loss")

    d_gate, d_up = _ffn_down_bwd(dx2_b, w_down_f, act_by_gate, act_by_up, "d_ffn_down")
    dw_down = _matmul(act, dx2_b, "tn", BF16, "dw_down", tm=ff // 2, tn=d, tk=2048)
    dw_gate = _matmul(d_gate, h2, "tn", BF16, "dw_gate", tm=ff // 2, tn=d, tk=2048)
    dw_up = _matmul(d_up, h2, "tn", BF16, "dw_up", tm=ff // 2, tn=d, tk=2048)
    dx1, dx1_b, dg_ffn = _rows_matmul([(d_gate, w_gate_t, "nn"), (d_up, w_up_t, "nn")], [x1, dx2], [norm_ffn_g],
                                      _norm_bwd_residual, both, [d], "d_ffn_gate_up_norm_ffn", tm=256)
    dw_o = _matmul(cat, dx1_b, "tn", BF16, "dw_o", tn=d, tk=2048)
    d_fox_o, d_mla_o, fox_delta, mla_delta, dg_fox, dg_mla = _rows_matmul(
        [(dx1_b, w_o_f, "nt")], [fox_o, mla_o], [fox_out_g, mla_out_g], _out_norm_bwd,
        [(fw, BF16), (fw, BF16), (HEADS, F32), (HEADS, F32)], [fw, fw], "d_proj_out_norm_out")

    per_dev = lambda a: a.reshape(N_DEV, -1, d)
    late_grads = [per_dev(dw_o), per_dev(dw_gate), per_dev(dw_up), per_dev(dw_down)]
    dproj_a, dc, g_late = _attn_bwd((proj_a,), d_fox_o, fox_lse, per_head_rows(fox_delta),
                                    c_bias, HEAD_DIM ** -0.5, bl, s, tq, "d_fox_attention", traffic=_Traffic("swap", late_grads))
    dz, db_fgate = _fgate_bwd(z, bcol, dc.reshape(bh * grp, LANES), grp, "d_forget_gate")
    d_flogit = dz.reshape(bl, HEADS, s).transpose(0, 2, 1).reshape(t, HEADS)

    dq_nope, dkv_all, dq_pe, dk_pe = _attn_bwd(mla_ops, d_mla_o, mla_lse, per_head_rows(mla_delta),
                                               None, MLA_QK ** -0.5, bl, s, tq, "d_mla_attention")
    d_tail = jnp.pad(d_flogit, ((0, 0), (0, b_cols - o_flogit - HEADS)))
    dproj_b, dq_rot, dg_q, dg_kv = _mla_prep_bwd(dq_nope, dq_pe, dkv_all, dk_pe, d_tail, proj_b, q_rank, kv_rank,
                                                 q_norm_g, kv_norm_g, w_uq_p, w_ukv_p, rope_cos, rope_sin, "d_mla_prep")
    dw_uq_nope = _matmul(dq_nope, qn, "tn", BF16, "dw_uq_nope", tn=q_rank, tk=1024)
    dw_uq_pe = _matmul(dq_rot, qn, "tn", BF16, "dw_uq_rope", tn=q_rank, tk=1024)
    dw_ukv_p = _matmul(dkv_all, kvn, "tn", BF16, "dw_ukv", tn=kv_rank, tk=1024)
    dw_in_a = _matmul(dproj_a, h1, "tn", BF16, "dw_in_fox", tm=6 * LANES, tn=d, tk=2048)
    dw_in_b = _matmul(dproj_b, h1, "tn", F32, "dw_in_mla", tm=b_cols, tn=d, tk=1024)

    dw_krope = dw_in_b[o_krope:o_flogit].reshape(2, 4, half, d).sum(axis=1).reshape(MLA_ROPE, d)
    dw_in_t = jnp.concatenate([dw_in_a.reshape(PAIRS, 3, LANES, d).transpose(1, 0, 2, 3).reshape(n_qkv, d),
                               dw_in_b[o_flogit:o_flogit + HEADS].astype(BF16), dw_in_b[:o_krope].astype(BF16),
                               dw_krope.astype(BF16)], axis=0)
    pad_dev = lambda a, rows: jnp.pad(a, ((0, 0), (0, rows - a.shape[1]), (0, 0)))
    dw_uq_pe5 = dw_uq_pe.reshape(2, 2, 4, half, q_rank)
    dw_uq_h = jnp.concatenate([dw_uq_nope.reshape(HEADS, HEAD_DIM, q_rank), dw_uq_pe5[:, 0].reshape(HEADS, half, q_rank),
                               dw_uq_pe5[:, 1].reshape(HEADS, half, q_rank)], axis=1)
    dw_ukv_h = dw_ukv_p.reshape(PAIRS, 2, 2, HEAD_DIM, kv_rank).transpose(0, 2, 1, 3, 4).reshape(HEADS, 2 * HEAD_DIM, kv_rank)
    n_last = 3
    last_grads = [pad_dev(per_dev(dw_in_t), pieces[0].shape[0]), pad_dev(per_dev(dw_uq_h), pieces[1].shape[0]), per_dev(dw_ukv_h)]
    grad_x, dg_mix, g_last = _rows_matmul([(dproj_a, w_in_a, "nn"), (dproj_b, w_in_b, "nn")], [x2d, dx1], [norm_mix_g],
                                          _norm_bwd_residual, [(d, F32)], [d], "d_proj_in_norm_mix",
                                          traffic=_Traffic("swap", last_grads))
    g_last = _sum_blocks(g_last, "sum_last_grads")
    g_late = _sum_blocks(g_late, "sum_late_grads")

    def mine(i, rows):
        src, base = (g_last, 0) if i < n_last else (g_late, offs[n_last])
        return src[offs[i] - base:offs[i] - base + rows]

    big = [
        ("w_in", w_in, m_w_in, v_w_in, mine(0, in_cols), True),
        ("w_uq", w_uq, m_w_uq, v_w_uq, mine(1, uq_rows).reshape(-1, q_rank), True),
        ("w_ukv", w_ukv, m_w_ukv, v_w_ukv, mine(2, ukv_rows).reshape(-1, kv_rank), True),
        ("w_o", w_o, m_w_o, v_w_o, mine(3, w_o.shape[1]), False),
        ("w_gate", w_gate, m_w_gate, v_w_gate, mine(4, ff // N_DEV), True),
        ("w_up", w_up, m_w_up, v_w_up, mine(5, ff // N_DEV), True),
        ("w_down", w_down, m_w_down, v_w_down, mine(6, ff // N_DEV), False),
    ]
    out = {}
    for nm, w, m, v, g, transposed in big:
        lay = (lambda a: a[0].T) if transposed else (lambda a: a[0])
        back = (lambda a: a.T[None]) if transposed else (lambda a: a[None])
        dl, new_m, new_v = _adamw(lay(w), g, lay(m), lay(v), "adamw_" + nm)
        out[nm] = (back(g), back(dl), back(new_m), back(new_v))

    smalls = [("norm_mix_g", norm_mix_g, m_norm_mix_g, v_norm_mix_g, dg_mix),
              ("b_fgate", b_fgate, m_b_fgate, v_b_fgate, db_fgate.reshape(1, HEADS)),
              ("q_norm_g", q_norm_g, m_q_norm_g, v_q_norm_g, dg_q),
              ("kv_norm_g", kv_norm_g, m_kv_norm_g, v_kv_norm_g, dg_kv),
              ("fox_out_g", fox_out_g, m_fox_out_g, v_fox_out_g, dg_fox),
              ("mla_out_g", mla_out_g, m_mla_out_g, v_mla_out_g, dg_mla),
              ("norm_ffn_g", norm_ffn_g, m_norm_ffn_g, v_norm_ffn_g, dg_ffn),
              ("final_norm_g", final_norm_g, m_final_norm_g, v_final_norm_g, dg_final)]
    flat = lambda a: a.reshape(1, -1)
    results, loss = _small_all_reduce_adamw([e[4] for e in smalls], loss_part, [flat(e[1]) for e in smalls],
                                            [flat(e[2]) for e in smalls], [flat(e[3]) for e in smalls], "reduce_small_adamw")
    for (nm, w, _, _, _), res in zip(smalls, results):
        out[nm] = tuple(a.reshape(w.shape) for a in res)
    loss = loss[0, 0]

    order = ["norm_mix_g", "w_in", "b_fgate", "q_norm_g", "w_uq", "kv_norm_g", "w_ukv", "fox_out_g", "mla_out_g", "w_o",
             "norm_ffn_g", "w_gate", "w_up", "w_down", "final_norm_g"]
    return (loss, grad_x.reshape(bl, s, d), *[out[n][0] for n in order], *[out[n][1] for n in order],
            *[out[n][2] for n in order], *[out[n][3] for n in order])
```

```python
import math

import jax
import jax.numpy as jnp
from jax import lax
from jax.experimental import pallas as pl
from jax.experimental.pallas import tpu as pltpu

F32 = jnp.float32
BF16 = jnp.bfloat16
MESH = pl.DeviceIdType.MESH

N_DEV = 8
HEADS = 8
HEAD_DIM = 64
PAIRS = HEADS // 2
MLA_ROPE = 32
MLA_QK = HEAD_DIM + MLA_ROPE
ROPE_THETA = 10000.0
NORM_EPS = 1e-6
ADAM_LR, ADAM_B1, ADAM_B2, ADAM_EPS, ADAM_WD, ADAM_STEP = 0.001, 0.9, 0.999, 1e-08, 0.01, 10

LANES = 128
MASKED = -1e30
VMEM_LIMIT = 56 * 1024 * 1024

_DIMS = {"nn": (((1,), (0,)), ((), ())), "nt": (((1,), (1,)), ((), ())), "tn": (((0,), (0,)), ((), ()))}


def _params(*sem):
    return pltpu.CompilerParams(dimension_semantics=sem, vmem_limit_bytes=VMEM_LIMIT)


def _dot(a, b, mode):
    return lax.dot_general(a.astype(BF16), b.astype(BF16), _DIMS[mode], preferred_element_type=F32)


def _tile(n, pref, unit=8):
    if n <= pref:
        return n
    t = pref - pref % unit
    while n % t:
        t -= unit
    return t


def _log2(n):
    assert n & (n - 1) == 0
    return n.bit_length() - 1


def _matmul(a, b, mode, out_dtype, name, tm=512, tn=512, tk=None):
    if mode == "nn":
        (m, kd), n = a.shape, b.shape[1]
    elif mode == "nt":
        (m, kd), n = a.shape, b.shape[0]
    else:
        (kd, m), n = a.shape, b.shape[1]
    tm, tn = _tile(m, tm, LANES if mode == "tn" else 16), _tile(n, tn, LANES)
    tk = kd if tk is None else _tile(kd, tk, LANES)
    nk = kd // tk
    a_spec = pl.BlockSpec((tk, tm), lambda i, j, k: (k, i)) if mode == "tn" else pl.BlockSpec((tm, tk), lambda i, j, k: (i, k))
    b_spec = pl.BlockSpec((tn, tk), lambda i, j, k: (j, k)) if mode == "nt" else pl.BlockSpec((tk, tn), lambda i, j, k: (k, j))
    o_spec = pl.BlockSpec((tm, tn), lambda i, j, k: (i, j))

    def body(a_ref, b_ref, o_ref, *acc):
        part = _dot(a_ref[...], b_ref[...], mode)
        if nk == 1:
            o_ref[...] = part.astype(out_dtype)
        else:
            acc_ref, k = acc[0], pl.program_id(2)

            @pl.when(k == 0)
            def _():
                acc_ref[...] = part

            @pl.when(k > 0)
            def _():
                acc_ref[...] += part

            @pl.when(k == nk - 1)
            def _():
                o_ref[...] = acc_ref[...].astype(out_dtype)

    return pl.pallas_call(
        body, name=name, grid=(m // tm, n // tn, nk), in_specs=[a_spec, b_spec], out_specs=o_spec,
        out_shape=jax.ShapeDtypeStruct((m, n), out_dtype),
        scratch_shapes=[pltpu.VMEM((tm, tn), F32)] if nk > 1 else [],
        compiler_params=_params("parallel", "parallel", "arbitrary"),
    )(a, b)


def _rstd(x):
    return lax.rsqrt(jnp.mean(x * x, axis=-1, keepdims=True) + NORM_EPS)


def _norm_bwd(x, g, dy):
    r = _rstd(x)
    xh = x * r
    u = dy * g
    dx = r * (u - xh * jnp.mean(u * xh, axis=-1, keepdims=True))
    return dx, jnp.sum(dy * xh, axis=0, keepdims=True)


def _rmsnorm(x, col, width, g, out_dtype, name, traffic=None):
    t = x.shape[0]
    tm = _tile(t, 512)
    steps = t // tm
    n_carried = len(traffic.pieces) if traffic else 0

    def body(*refs):
        x_ref, g_ref, o_ref = refs[0], refs[1], refs[2 + n_carried]
        if traffic:
            carried_in, carried_out, sems = refs[2:2 + n_carried], refs[3 + n_carried], refs[4 + n_carried:]

            @pl.when(pl.program_id(0) == 0)
            def _():
                traffic.start(carried_in, carried_out, *sems)

            if isinstance(traffic, _Relay):
                @pl.when(pl.program_id(0) == steps - 1)
                def _():
                    traffic.relay(carried_out, *sems)

        xv = x_ref[...]
        o_ref[...] = ((xv * _rstd(xv)) * g_ref[...]).astype(out_dtype)
        if traffic:
            @pl.when(pl.program_id(0) == steps - 1)
            def _():
                traffic.wait(carried_out, *sems)

    in_specs = [pl.BlockSpec((tm, width), lambda i: (i, col)), pl.BlockSpec((1, width), lambda i: (0, 0))]
    out_specs = [pl.BlockSpec((tm, width), lambda i: (i, 0))]
    out_shape = [jax.ShapeDtypeStruct((t, width), out_dtype)]
    if traffic:
        in_specs += traffic.in_specs
        out_specs.append(traffic.out_spec)
        out_shape.append(traffic.out_shape)
    out = pl.pallas_call(
        body, name=name, grid=(steps,), in_specs=in_specs, out_specs=tuple(out_specs), out_shape=tuple(out_shape),
        scratch_shapes=traffic.scratch if traffic else [],
        compiler_params=_params("arbitrary" if traffic else "parallel"),
    )(x, g, *(traffic.pieces if traffic else []))
    return out if traffic else out[0]


def _split3(x):
    hi = x.astype(BF16)
    r1 = x - hi.astype(F32)
    mid = r1.astype(BF16)
    lo = (r1 - mid.astype(F32)).astype(BF16)
    return hi, mid, lo


def _dot_x01(x, m01):
    hi, mid, lo = _split3(x)
    d = lambda p: lax.dot_general(p, m01, _DIMS["nn"], preferred_element_type=F32)
    return (d(lo) + d(mid)) + d(hi)


def _dot_01x(m01, x):
    hi, mid, lo = _split3(x)
    d = lambda p: lax.dot_general(m01, p, _DIMS["nn"], preferred_element_type=F32)
    return (d(lo) + d(mid)) + d(hi)


def _rows_matmul(terms, rows_in, vecs_in, epilogue, rows_out, sums_out, name, tm=512, prologue=None, traffic=None):
    t = rows_in[0].shape[0]
    tm = _tile(t, tm, 16)
    steps = t // tm
    n_rows, n_vecs = len(rows_in), len(vecs_in)
    n_ab = sum(1 + (a is not None) for a, _, _ in terms)
    n_carried = len(traffic.pieces) if traffic else 0
    halves = [slice(0, tm // 2), slice(tm // 2, tm)] if tm % 32 == 0 else [slice(0, tm)]

    def body(*refs):
        vecs = [r[...] for r in refs[n_ab + n_rows:n_ab + n_rows + n_vecs]]
        out_at = n_ab + n_rows + n_vecs + n_carried
        sum_refs = refs[out_at + len(rows_out):out_at + len(rows_out) + len(sums_out)]
        if traffic:
            carried_in, carried_out, sems = refs[out_at - n_carried:out_at], refs[len(refs) - 4], refs[len(refs) - 3:]

            @pl.when(pl.program_id(0) == 0)
            def _():
                traffic.start(carried_in, carried_out, *sems)

        @pl.when(pl.program_id(0) == 0)
        def _():
            for ref in sum_refs:
                ref[...] = jnp.zeros_like(ref)

        staged = []
        for rows_of in halves:
            row_blocks = [r[rows_of, :] for r in refs[n_ab:n_ab + n_rows]]
            made = prologue(row_blocks, vecs) if prologue else None
            acc, at = None, 0
            for a, _, mode in terms:
                lhs = made if a is None else refs[at][rows_of, :]
                at += a is not None
                part = _dot(lhs, refs[at][...], mode)
                at += 1
                acc = part if acc is None else acc + part
            staged.append((rows_of, row_blocks, made, acc))
        for rows_of, row_blocks, made, acc in staged:
            row_vals, sum_vals = epilogue(acc, row_blocks, vecs)
            if prologue:
                row_vals = [made] + row_vals
            for ref, val, (_, dtype) in zip(refs[out_at:], row_vals, rows_out):
                ref[rows_of, :] = val.astype(dtype)
            for ref, val in zip(sum_refs, sum_vals):
                ref[...] += val
        if traffic:
            @pl.when(pl.program_id(0) == steps - 1)
            def _():
                traffic.wait(carried_out, *sems)

    rows = lambda w: pl.BlockSpec((tm, w), lambda i: (i, 0))
    whole = lambda a: pl.BlockSpec(a.shape, lambda i: (0, 0))
    in_specs, args = [], []
    for a, b, _ in terms:
        in_specs += ([rows(a.shape[1])] if a is not None else []) + [whole(b)]
        args += ([a] if a is not None else []) + [b]
    in_specs += [rows(r.shape[1]) for r in rows_in] + [whole(v) for v in vecs_in]
    args += list(rows_in) + list(vecs_in)
    out_specs = [rows(w) for w, _ in rows_out] + [pl.BlockSpec((1, w), lambda i: (0, 0)) for w in sums_out]
    out_shape = [jax.ShapeDtypeStruct((t, w), dt) for w, dt in rows_out] + [jax.ShapeDtypeStruct((1, w), F32) for w in sums_out]
    if traffic:
        in_specs += traffic.in_specs
        args += traffic.pieces
        out_specs.append(traffic.out_spec)
        out_shape.append(traffic.out_shape)
    return pl.pallas_call(
        body, name=name, grid=(steps,), in_specs=in_specs, out_specs=tuple(out_specs), out_shape=tuple(out_shape),
        scratch_shapes=traffic.scratch if traffic else [], compiler_params=_params("arbitrary"),
    )(*args)


def _out_norm(rows, vecs):
    (f, m), (gf, gm) = rows[:2], vecs[:2]
    return jnp.concatenate([((f * _rstd(f)) * gf).astype(BF16), ((m * _rstd(m)) * gm).astype(BF16)], axis=1)


def _residual_norm(acc, rows, vecs):
    x1 = rows[-1] + acc
    return [x1, (x1 * _rstd(x1)) * vecs[-1]], []


def _residual_loss_bwd(acc, rows, vecs):
    x2, gv = rows[0] + acc, vecs[0]
    diff = (x2 * _rstd(x2)) * gv - rows[1]
    dx, dg = _norm_bwd(x2, gv, diff / x2.shape[1])
    return [dx, dx], [dg, 0.5 * jnp.sum(jnp.mean(diff * diff, axis=-1, keepdims=True), axis=0, keepdims=True)]


def _norm_bwd_residual(acc, rows, vecs):
    dy = acc + rows[2] if len(rows) > 2 else acc
    dx, dg = _norm_bwd(rows[0], vecs[0], dy)
    if len(rows) > 1:
        dx = dx + rows[1]
    return [dx, dx], [dg]


def _out_norm_bwd(acc, rows, vecs):
    (f, m), w = rows, rows[0].shape[1]
    nh = w // HEAD_DIM
    lane_head = lax.shift_right_logical(lax.broadcasted_iota(jnp.int32, (w, nh), 0), _log2(HEAD_DIM))
    sel = (lane_head == lax.broadcasted_iota(jnp.int32, (w, nh), 1)).astype(BF16)
    dfo, dgf = _norm_bwd(f, vecs[0], acc[:, :w])
    dmo, dgm = _norm_bwd(m, vecs[1], acc[:, w:])
    return [dfo, dmo, _dot_x01(dfo * f, sel), _dot_x01(dmo * m, sel)], [dgf, dgm]


def _ffn_up(h, wg_t, wu_t, name, tm=512, tf=1408):
    t, d = h.shape
    f = wg_t.shape[0]
    tm, tf = _tile(t, tm, 16), _tile(f, tf, LANES)
    tok = pl.BlockSpec((tm, tf), lambda j, i: (i, j))
    wt = pl.BlockSpec((tf, d), lambda j, i: (j, 0))

    def body(h_ref, wg_ref, wu_ref, dg_ref, du_ref, a_ref):
        hv = h_ref[...]
        g, u = _dot(hv, wg_ref[...], "nt"), _dot(hv, wu_ref[...], "nt")
        sg = jax.nn.sigmoid(g)
        silu = g * sg
        dg_ref[...] = (u * (sg * (1.0 + g * (1.0 - sg)))).astype(BF16)
        du_ref[...] = silu.astype(BF16)
        a_ref[...] = (silu * u).astype(BF16)

    return pl.pallas_call(
        body, name=name, grid=(f // tf, t // tm), in_specs=[pl.BlockSpec((tm, d), lambda j, i: (i, 0)), wt, wt],
        out_specs=(tok, tok, tok),
        out_shape=(jax.ShapeDtypeStruct((t, f), BF16), jax.ShapeDtypeStruct((t, f), BF16), jax.ShapeDtypeStruct((t, f), BF16)),
        compiler_params=_params("parallel", "parallel"),
    )(h, wg_t, wu_t)


def _ffn_down_bwd(dy, w_down, act_by_gate, act_by_up, name, tm=512, tf=1408):
    t, d = dy.shape
    f = w_down.shape[0]
    tm, tf = _tile(t, tm, 16), _tile(f, tf, LANES)
    tok = pl.BlockSpec((tm, tf), lambda j, i: (i, j))

    def body(dy_ref, w_ref, g_ref, u_ref, dg_ref, du_ref):
        da = _dot(dy_ref[...], w_ref[...], "nt")
        dg_ref[...] = (da * g_ref[...].astype(F32)).astype(BF16)
        du_ref[...] = (da * u_ref[...].astype(F32)).astype(BF16)

    return pl.pallas_call(
        body, name=name, grid=(f // tf, t // tm),
        in_specs=[pl.BlockSpec((tm, d), lambda j, i: (i, 0)), pl.BlockSpec((tf, d), lambda j, i: (j, 0)), tok, tok],
        out_specs=(tok, tok),
        out_shape=(jax.ShapeDtypeStruct((t, f), BF16), jax.ShapeDtypeStruct((t, f), BF16)),
        compiler_params=_params("parallel", "parallel"),
    )(dy, w_down, act_by_gate, act_by_up)


def _chunk_scan_mats(rows, grp, reverse):
    ii = lax.broadcasted_iota(jnp.int32, (LANES, LANES), 0)
    jj = lax.broadcasted_iota(jnp.int32, (LANES, LANES), 1)
    within = ((ii >= jj) if reverse else (ii <= jj)).astype(BF16)
    ones = jnp.ones((LANES, LANES), BF16)
    ri = lax.broadcasted_iota(jnp.int32, (rows, rows), 0)
    rj = lax.broadcasted_iota(jnp.int32, (rows, rows), 1)
    sh = _log2(grp)
    same = lax.shift_right_logical(ri, sh) == lax.shift_right_logical(rj, sh)
    across = (same & ((rj > ri) if reverse else (rj < ri))).astype(BF16)
    return within, ones, across


def _running_sum(v, mats):
    within, ones, across = mats
    return _dot_x01(v, within) + _dot_01x(across, _dot_x01(v, ones))


def _fgate(z, bcol, grp, name):
    rows = z.shape[0]

    def body(z_ref, b_ref, c_ref):
        zz = z_ref[...] + b_ref[...]
        log_f = jnp.minimum(zz, 0.0) - jnp.log1p(jnp.exp(-jnp.abs(zz)))
        c_ref[...] = _running_sum(log_f, _chunk_scan_mats(rows, grp, False))

    return pl.pallas_call(body, name=name, out_shape=jax.ShapeDtypeStruct(z.shape, F32),
                          compiler_params=pltpu.CompilerParams(vmem_limit_bytes=VMEM_LIMIT))(z, bcol)


def _fgate_bwd(z, bcol, dc, grp, name):
    rows = z.shape[0]

    def body(z_ref, b_ref, dc_ref, dz_ref, db_ref):
        zz = z_ref[...] + b_ref[...]
        dz = _running_sum(dc_ref[...], _chunk_scan_mats(rows, grp, True)) * jax.nn.sigmoid(-zz)
        dz_ref[...] = dz
        head = lax.shift_right_logical(lax.broadcasted_iota(jnp.int32, (HEADS, rows), 1), _log2(grp)) & (HEADS - 1)
        sel = (head == lax.broadcasted_iota(jnp.int32, (HEADS, rows), 0)).astype(BF16)
        db_ref[...] = jnp.sum(_dot_01x(sel, dz), axis=1, keepdims=True)

    return pl.pallas_call(
        body, name=name,
        out_shape=(jax.ShapeDtypeStruct(z.shape, F32), jax.ShapeDtypeStruct((HEADS, 1), F32)),
        compiler_params=pltpu.CompilerParams(vmem_limit_bytes=VMEM_LIMIT),
    )(z, bcol, dc)


def _rotate(x, cs, sn_signed):
    return x * cs + pltpu.roll(x, LANES // 2, axis=1) * sn_signed


def _rope_tables(cos, sin):
    half = cos.shape[1]
    freq = lax.broadcasted_iota(jnp.int32, (half, LANES), 0)
    lane = lax.broadcasted_iota(jnp.int32, (half, LANES), 1)
    hit = (lane & (half - 1)) == freq
    sign = jnp.where(lane < LANES // 2, -1.0, 1.0)
    return _dot_x01(cos, hit.astype(BF16)), _dot_x01(sin, jnp.where(hit, sign, 0.0).astype(BF16))


def _mla_prep(proj_b, q_rank, kv_rank, gq, gkv, w_uq_p, w_ukv_p, nope, cs, sn, name):
    t, bw = proj_b.shape
    qw, kvw = w_uq_p.shape[0], w_ukv_p.shape[0]
    tm = _tile(t, 512)
    rows = lambda w: pl.BlockSpec((tm, w), lambda i: (i, 0))
    whole = lambda a: pl.BlockSpec(a.shape, lambda i: (0, 0))

    def body(pb_ref, gq_ref, gkv_ref, wq_ref, wkv_ref, c_ref, s_ref, qn_ref, kvn_ref, q_ref, kv_ref, kpe_ref):
        c, s = _rope_tables(c_ref[...], s_ref[...])
        ql, kvl = pb_ref[:, :q_rank], pb_ref[:, q_rank:q_rank + kv_rank]
        qn = ((ql * _rstd(ql)) * gq_ref[...]).astype(BF16)
        kvn = ((kvl * _rstd(kvl)) * gkv_ref[...]).astype(BF16)
        qn_ref[...], kvn_ref[...] = qn, kvn
        q_raw = _dot(qn, wq_ref[...], "nt")
        q_ref[:, :nope] = q_raw[:, :nope].astype(BF16)
        for off in range(nope, qw, LANES):
            q_ref[:, off:off + LANES] = _rotate(q_raw[:, off:off + LANES], c, s).astype(BF16)
        kv_ref[...] = _dot(kvn, wkv_ref[...], "nt").astype(BF16)
        kpe_ref[...] = _rotate(pb_ref[:, q_rank + kv_rank:q_rank + kv_rank + LANES], c, s).astype(BF16)

    return pl.pallas_call(
        body, name=name, grid=(t // tm,),
        in_specs=[rows(bw), whole(gq), whole(gkv), whole(w_uq_p), whole(w_ukv_p), rows(cs.shape[1]), rows(sn.shape[1])],
        out_specs=(rows(q_rank), rows(kv_rank), rows(qw), rows(kvw), rows(LANES)),
        out_shape=(jax.ShapeDtypeStruct((t, q_rank), BF16), jax.ShapeDtypeStruct((t, kv_rank), BF16),
                   jax.ShapeDtypeStruct((t, qw), BF16), jax.ShapeDtypeStruct((t, kvw), BF16), jax.ShapeDtypeStruct((t, LANES), BF16)),
        compiler_params=_params("parallel"),
    )(proj_b, gq, gkv, w_uq_p, w_ukv_p, cs, sn)


def _mla_prep_bwd(dq_nope, dq_pe, dkv_all, dk_pe, d_tail, proj_b, q_rank, kv_rank, gq, gkv, w_uq_p, w_ukv_p, cs, sn, name):
    t, bw = proj_b.shape
    nope, pw = dq_nope.shape[1], dq_pe.shape[1]
    tm = _tile(t, 512)
    rows = lambda w: pl.BlockSpec((tm, w), lambda i: (i, 0))
    whole = lambda a: pl.BlockSpec(a.shape, lambda i: (0, 0))
    o_k = q_rank + kv_rank

    def body(dqn_ref, dqp_ref, dkv_ref, dkp_ref, dt_ref, pb_ref, gq_ref, gkv_ref, wq_ref, wkv_ref, c_ref, s_ref,
             dpb_ref, dqr_ref, dgq_ref, dgkv_ref):
        c, s = _rope_tables(c_ref[...], -s_ref[...])
        for off in range(0, pw, LANES):
            dqr_ref[:, off:off + LANES] = _rotate(dqp_ref[:, off:off + LANES], c, s).astype(BF16)
        d_qn = _dot(dqn_ref[...], wq_ref[:nope, :], "nn") + _dot(dqr_ref[...], wq_ref[nope:, :], "nn")
        dq_lat, dgq = _norm_bwd(pb_ref[:, :q_rank], gq_ref[...], d_qn)
        dkv_lat, dgkv = _norm_bwd(pb_ref[:, q_rank:o_k], gkv_ref[...], _dot(dkv_ref[...], wkv_ref[...], "nn"))
        dpb_ref[:, :q_rank] = dq_lat.astype(BF16)
        dpb_ref[:, q_rank:o_k] = dkv_lat.astype(BF16)
        dpb_ref[:, o_k:o_k + LANES] = _rotate(dkp_ref[...], c, s).astype(BF16)
        dpb_ref[:, o_k + LANES:] = dt_ref[...].astype(BF16)

        @pl.when(pl.program_id(0) == 0)
        def _():
            dgq_ref[...] = jnp.zeros_like(dgq_ref)
            dgkv_ref[...] = jnp.zeros_like(dgkv_ref)

        dgq_ref[...] += dgq
        dgkv_ref[...] += dgkv

    return pl.pallas_call(
        body, name=name, grid=(t // tm,),
        in_specs=[rows(nope), rows(pw), rows(dkv_all.shape[1]), rows(LANES), rows(bw - o_k - LANES), rows(bw), whole(gq), whole(gkv),
                  whole(w_uq_p), whole(w_ukv_p), rows(cs.shape[1]), rows(sn.shape[1])],
        out_specs=(rows(bw), rows(pw), whole(gq), whole(gkv)),
        out_shape=(jax.ShapeDtypeStruct((t, bw), BF16), jax.ShapeDtypeStruct((t, pw), BF16),
                   jax.ShapeDtypeStruct(gq.shape, F32), jax.ShapeDtypeStruct(gkv.shape, F32)),
        compiler_params=_params("arbitrary"),
    )(dq_nope, dq_pe, dkv_all, dk_pe, d_tail, proj_b, gq, gkv, w_uq_p, w_ukv_p, cs, sn)


def _lane_masks(pair, h, pe):
    lane = lax.broadcasted_iota(jnp.int32, (1, LANES), 1)
    in_head = lax.shift_right_logical(lane, _log2(HEAD_DIM)) == h
    in_rope = ((lax.shift_right_logical(lane, _log2(MLA_ROPE // 2)) & 3) == ((2 * pair + h) & 3)) if pe else None
    return in_head, in_rope


def _keep(mask, v):
    return jnp.where(mask, v, jnp.zeros_like(v))


def _to_row(col):
    n = col.shape[0]
    eye = lax.broadcasted_iota(jnp.int32, (n, n), 0) == lax.broadcasted_iota(jnp.int32, (n, n), 1)
    return jnp.sum(jnp.where(eye, col, 0.0), axis=0, keepdims=True)


def _to_col(row):
    n = row.shape[1]
    eye = lax.broadcasted_iota(jnp.int32, (n, n), 0) == lax.broadcasted_iota(jnp.int32, (n, n), 1)
    return jnp.sum(jnp.where(eye, row, 0.0), axis=1, keepdims=True)


def _first_step():
    return (pl.program_id(0) == 0) & (pl.program_id(1) == 0)


def _last_step(n0, n1):
    return (pl.program_id(0) == n0 - 1) & (pl.program_id(1) == n1 - 1)


def _attn_fwd(ops, bias, scale, bl, s, tq, name, traffic=None):
    pe = len(ops) == 3
    has_bias = bias is not None
    exact_scale = math.frexp(scale)[0] == 0.5
    span = 4 * tq
    nq = s // tq
    t = bl * s
    n_carried = len(traffic.pieces) if traffic else 0

    def body(*refs):
        sems = refs[len(refs) - 3:] if traffic else ()
        if pe:
            q_ref, qpe_ref, kv_ref, kpe_ref = refs[:4]
            n_in = 4
            q_at = lambda r0, r1: q_ref[r0:r1, :]
            v_at = lambda r0, r1: kv_ref[r0:r1, LANES:]
            kcat = refs[len(refs) - 1 - len(sems)]
            kcat[:, :LANES] = kv_ref[:, :LANES]
            kcat[:, LANES:] = kpe_ref[...]
            k_at = lambda r0, r1: kcat[r0:r1, :]
        else:
            qkv_ref = refs[0]
            n_in = 1
            q_at = lambda r0, r1: qkv_ref[r0:r1, :LANES]
            k_at = lambda r0, r1: qkv_ref[r0:r1, LANES:2 * LANES]
            v_at = lambda r0, r1: qkv_ref[r0:r1, 2 * LANES:]
        if has_bias:
            c_ref = refs[n_in]
            n_in += 1
        carried_in = refs[n_in:n_in + n_carried]
        n_in += n_carried
        o_ref, lse_ref = refs[n_in:n_in + 2]
        if traffic:
            carried_out = refs[n_in + 2]

            @pl.when(_first_step())
            def _():
                traffic.start(carried_in, carried_out, *sems)

        pair = pl.program_id(1)
        causal = lax.broadcasted_iota(jnp.int32, (tq, tq), 1) <= lax.broadcasted_iota(jnp.int32, (tq, tq), 0)
        o_ref[...] = jnp.zeros_like(o_ref)

        masks = [_lane_masks(pair, h, pe) for h in range(2)]

        def logits(i):
            r0, r1 = i * tq, (i + 1) * tq
            out = []
            for h in range(2):
                in_head, in_rope = masks[h]
                qm = _keep(in_head, q_at(r0, r1))
                if pe:
                    qm = jnp.concatenate([qm, _keep(in_rope, qpe_ref[r0:r1, :])], axis=1)
                if exact_scale:
                    qm = qm * scale
                spans = []
                for k0, k1 in [(r0, r1)] + [(k, min(k + span, r0)) for k in range(0, r0, span)]:
                    sc = _dot(qm, k_at(k0, k1), "nt")
                    if not exact_scale:
                        sc = sc * scale
                    if has_bias:
                        sc = sc - c_ref[h, :, k0:k1]
                    spans.append((k0, k1, jnp.where(causal, sc, MASKED) if k0 == r0 else sc))
                out.append(spans)
            return out

        def softmax(per_head):
            out = []
            for spans in per_head:
                m = None
                for _, _, sc in spans:
                    top = jnp.max(sc, axis=1, keepdims=True)
                    m = top if m is None else jnp.maximum(m, top)
                probs = [(k0, k1, jnp.exp(sc - m)) for k0, k1, sc in spans]
                l = sum(jnp.sum(p, axis=1, keepdims=True) for _, _, p in probs)
                out.append((m, l, probs))
            return out

        def weigh(i, per_head):
            r0, r1 = i * tq, (i + 1) * tq
            for h, (m, l, probs) in enumerate(per_head):
                acc = sum(_dot(p, v_at(k0, k1), "nn") for k0, k1, p in probs)
                o_ref[r0:r1, :] = jnp.where(masks[h][0], acc / l, o_ref[r0:r1, :])
                lse = _to_row(m + jnp.log(l))
                lse_ref[h, :, r0:r1] = lse + c_ref[h, :, r0:r1] if has_bias else lse

        ahead = logits(0)
        for i in range(nq):
            solved = softmax(ahead)
            if i + 1 < nq:
                ahead = logits(i + 1)
            weigh(i, solved)

        if traffic:
            @pl.when(_last_step(bl, PAIRS))
            def _():
                traffic.wait(carried_out, *sems)

    seq = lambda w, col: pl.BlockSpec((s, w), col)
    if pe:
        in_specs = [seq(LANES, lambda b, p: (b, p)), seq(LANES, lambda b, p: (b, PAIRS + p // 2)),
                    seq(2 * LANES, lambda b, p: (b, p)), seq(LANES, lambda b, p: (b, 0))]
        args = [ops[0], ops[0], ops[1], ops[2]]
        scratch = [pltpu.VMEM((s, 2 * LANES), BF16)]
    else:
        in_specs = [seq(3 * LANES, lambda b, p: (b, p))]
        args = [ops[0]]
        scratch = []
    per_head_row = pl.BlockSpec((2, 1, s), lambda b, p: (b * PAIRS + p, 0, 0))
    if has_bias:
        in_specs.append(per_head_row)
        args.append(bias)
    out_specs = [seq(LANES, lambda b, p: (b, p)), per_head_row]
    out_shape = [jax.ShapeDtypeStruct((t, HEADS * HEAD_DIM), F32), jax.ShapeDtypeStruct((bl * HEADS, 1, s), F32)]
    if traffic:
        in_specs += traffic.in_specs
        args += traffic.pieces
        out_specs.append(traffic.out_spec)
        out_shape.append(traffic.out_shape)
        scratch += traffic.scratch
    return pl.pallas_call(
        body, name=name, grid=(bl, PAIRS), in_specs=in_specs, out_specs=tuple(out_specs), out_shape=tuple(out_shape),
        scratch_shapes=scratch, compiler_params=_params(*(("arbitrary", "arbitrary") if traffic else ("parallel", "parallel"))),
    )(*args)


def _attn_bwd(ops, do, lse, delta, bias, scale, bl, s, tq, name, traffic=None):
    pe = len(ops) == 3
    has_bias = bias is not None
    exact_scale = math.frexp(scale)[0] == 0.5
    span = 2 * tq
    nq = s // tq
    t = bl * s
    width = 2 * LANES if pe else LANES
    n_carried = len(traffic.pieces) if traffic else 0

    def body(*refs):
        if pe:
            q_ref, qpe_ref, kv_ref, kpe_ref = refs[:4]
            n_in = 4
            k_at = lambda r0, r1: kv_ref[r0:r1, :LANES]
            v_at = lambda r0, r1: kv_ref[r0:r1, LANES:]
        else:
            qkv_ref = refs[0]
            n_in = 1
            k_at = lambda r0, r1: qkv_ref[r0:r1, LANES:2 * LANES]
            v_at = lambda r0, r1: qkv_ref[r0:r1, 2 * LANES:]
        do_ref, lse_ref, dl_ref = refs[n_in:n_in + 3]
        n_in += 3
        if has_bias:
            c_ref = refs[n_in]
            n_in += 1
        carried_in = refs[n_in:n_in + n_carried]
        rest = refs[n_in + n_carried:]
        if traffic:
            rest, sems = rest[:-3], rest[-3:]
            carried_out = rest[4 if pe else 2]
            rest = rest[:4 if pe else 2] + rest[(4 if pe else 2) + 1:]

            @pl.when(_first_step())
            def _():
                traffic.start(carried_in, carried_out, *sems)

        if pe:
            dqn_ref, dkv_ref, dqpe_ref, dkpe_ref, dq_acc, qcat = rest
            qcat[:, :LANES] = q_ref[...]
            qcat[:, LANES:] = qpe_ref[...]
            q_at = lambda r0, r1: qcat[r0:r1, :]
            dkv_ref[...] = jnp.zeros_like(dkv_ref)
        else:
            dqkv_ref, dc_ref, dq_acc = rest
            q_at = lambda r0, r1: qkv_ref[r0:r1, :LANES]
            dqkv_ref[...] = jnp.zeros_like(dqkv_ref)
            dc_ref[...] = jnp.zeros_like(dc_ref)
        pair = pl.program_id(1)
        dq_acc[...] = jnp.zeros_like(dq_acc)
        causal = lax.broadcasted_iota(jnp.int32, (tq, tq), 1) >= lax.broadcasted_iota(jnp.int32, (tq, tq), 0)
        if pe:
            @pl.when(pair == 0)
            def _():
                dkpe_ref[...] = jnp.zeros_like(dkpe_ref)

            @pl.when(pair % 2 == 0)
            def _():
                dqpe_ref[...] = jnp.zeros_like(dqpe_ref)

        masks = [_lane_masks(pair, h, pe) for h in range(2)]

        def logits(j):
            r0, r1 = j * tq, (j + 1) * tq
            units = []
            for h in range(2):
                in_head, in_rope = masks[h]
                kt = _keep(in_head, k_at(r0, r1))
                if pe:
                    kt = jnp.concatenate([kt, _keep(in_rope, kpe_ref[r0:r1, :])], axis=1)
                if exact_scale:
                    kt = kt * scale
                vt = _keep(in_head, v_at(r0, r1))
                ck = _to_col(c_ref[h, :, r0:r1]) if has_bias else None
                for q0, q1, diagonal in [(r0, r1, True)] + [(q, min(q + span, s), False) for q in range(r1, s, span)]:
                    qq, dd = q_at(q0, q1), do_ref[q0:q1, :]
                    st = _dot(kt, qq, "nt")
                    if not exact_scale:
                        st = st * scale
                    shift = lse_ref[h, :, q0:q1]
                    if has_bias:
                        shift = shift - c_ref[h, :, q0:q1]
                        st = st - ck
                    st = st - shift
                    if diagonal:
                        st = jnp.where(causal, st, MASKED)
                    units.append((h, q0, q1, kt, qq, dd, st, _dot(vt, dd, "nt")))
            return units

        def softmax_bwd(units):
            solved = []
            for h, q0, q1, kt, qq, dd, st, dpt in units:
                pt = jnp.exp(st)
                dst = pt * (dpt - dl_ref[h, :, q0:q1])
                solved.append((h, q0, q1, kt, qq, dd, pt, dst, (dst if exact_scale else dst * scale).astype(BF16)))
            return solved

        def products(j, solved):
            r0, r1 = j * tq, (j + 1) * tq
            dv_of, dk_of, cs_of = [None, None], [None, None], [None, None]
            add = lambda old, new: new if old is None else old + new
            for h, q0, q1, kt, qq, dd, pt, dst, dsb in solved:
                dq_acc[q0:q1, :] += _dot(dsb, kt, "tn")
                dv_of[h] = add(dv_of[h], _dot(pt, dd, "nn"))
                dk_of[h] = add(dk_of[h], _dot(dsb, qq, "nn"))
                if has_bias:
                    dc_ref[h, :, q0:q1] += jnp.sum(dst, axis=0, keepdims=True)
                    cs_of[h] = add(cs_of[h], jnp.sum(dst, axis=1, keepdims=True))
            for h in range(2):
                (in_head, in_rope), dv_c, dk_c, cs = masks[h], dv_of[h], dk_of[h], cs_of[h]
                if exact_scale:
                    dk_c = dk_c * scale
                if pe:
                    dkv_ref[r0:r1, :LANES] = jnp.where(in_head, dk_c[:, :LANES].astype(BF16), dkv_ref[r0:r1, :LANES])
                    dkv_ref[r0:r1, LANES:] = jnp.where(in_head, dv_c.astype(BF16), dkv_ref[r0:r1, LANES:])
                    dkpe_ref[r0:r1, :] += _keep(in_rope, dk_c[:, LANES:])
                else:
                    dqkv_ref[r0:r1, LANES:2 * LANES] = jnp.where(in_head, dk_c.astype(BF16), dqkv_ref[r0:r1, LANES:2 * LANES])
                    dqkv_ref[r0:r1, 2 * LANES:] = jnp.where(in_head, dv_c.astype(BF16), dqkv_ref[r0:r1, 2 * LANES:])
                    dc_ref[h, :, r0:r1] -= _to_row(cs)

        units = logits(0)
        for j in range(nq):
            solved = softmax_bwd(units)
            if j + 1 < nq:
                units = logits(j + 1)
            products(j, solved)

        if pe:
            dqn_ref[...] = dq_acc[:, :LANES].astype(BF16)
            dqpe_ref[...] += dq_acc[:, LANES:]
        else:
            dqkv_ref[:, :LANES] = dq_acc[...].astype(BF16)
        if traffic:
            @pl.when(_last_step(bl, PAIRS))
            def _():
                traffic.wait(carried_out, *sems)

    seq = lambda w, col: pl.BlockSpec((s, w), col)
    per_head_row = pl.BlockSpec((2, 1, s), lambda b, p: (b * PAIRS + p, 0, 0))
    if pe:
        in_specs = [seq(LANES, lambda b, p: (b, p)), seq(LANES, lambda b, p: (b, PAIRS + p // 2)),
                    seq(2 * LANES, lambda b, p: (b, p)), seq(LANES, lambda b, p: (b, 0))]
        args = [ops[0], ops[0], ops[1], ops[2]]
    else:
        in_specs = [seq(3 * LANES, lambda b, p: (b, p))]
        args = [ops[0]]
    in_specs += [seq(LANES, lambda b, p: (b, p)), per_head_row, per_head_row]
    args += [do, lse, delta]
    if has_bias:
        in_specs.append(per_head_row)
        args.append(bias)
    scratch = [pltpu.VMEM((s, width), F32)]
    if pe:
        out_specs = (seq(LANES, lambda b, p: (b, p)), seq(2 * LANES, lambda b, p: (b, p)),
                     seq(LANES, lambda b, p: (b, p // 2)), seq(LANES, lambda b, p: (b, 0)))
        out_shape = (jax.ShapeDtypeStruct((t, PAIRS * LANES), BF16), jax.ShapeDtypeStruct((t, PAIRS * 2 * LANES), BF16),
                     jax.ShapeDtypeStruct((t, 2 * LANES), F32), jax.ShapeDtypeStruct((t, LANES), F32))
        scratch.append(pltpu.VMEM((s, 2 * LANES), BF16))
    else:
        out_specs = (seq(3 * LANES, lambda b, p: (b, p)), per_head_row)
        out_shape = (jax.ShapeDtypeStruct((t, PAIRS * 3 * LANES), BF16), jax.ShapeDtypeStruct((bl * HEADS, 1, s), F32))
    if traffic:
        in_specs += traffic.in_specs
        args += traffic.pieces
        out_specs += (traffic.out_spec,)
        out_shape += (traffic.out_shape,)
        scratch += traffic.scratch
    return pl.pallas_call(
        body, name=name, grid=(bl, PAIRS), in_specs=in_specs, out_specs=out_specs, out_shape=out_shape,
        scratch_shapes=scratch, compiler_params=_params("arbitrary" if traffic else "parallel", "arbitrary"),
    )(*args)


def _my_place():
    return lax.axis_index("x"), lax.axis_index("y"), lax.axis_index("c")


def _flip(p, bit):
    return 1 - p if bit else p


def _relative(x, y, c, k):
    return _flip(x, k & 4), _flip(y, k & 2), _flip(c, k & 1)


def _linear(x, y, c):
    return 4 * x + 2 * y + c


class _Traffic:
    def __init__(self, kind, pieces):
        self.kind, self.pieces = kind, list(pieces)
        self.rows = [p.shape[-2] for p in self.pieces]
        self.starts = [sum(self.rows[:i]) for i in range(len(self.rows))]
        anywhere = pl.BlockSpec(memory_space=pl.ANY)
        self.in_specs = [anywhere] * len(self.pieces)
        self.out_spec = anywhere
        self.out_shape = jax.ShapeDtypeStruct((N_DEV, sum(self.rows), self.pieces[0].shape[-1]), self.pieces[0].dtype)
        self.scratch = [pltpu.SemaphoreType.DMA((7,)), pltpu.SemaphoreType.DMA((7,)), pltpu.SemaphoreType.DMA(())]

    def start(self, p_refs, out_ref, send_sems, recv_sems, local_sem):
        x, y, c = _my_place()
        me = _linear(x, y, c)
        mine = lambda i, dev: p_refs[i] if self.kind == "spread" else p_refs[i].at[dev]
        landing = lambda i: out_ref.at[me, pl.ds(self.starts[i], self.rows[i])]
        for i in range(len(p_refs)):
            pltpu.make_async_copy(mine(i, me), landing(i), local_sem).start()
        for k in range(1, N_DEV):
            peer = _relative(x, y, c, k)
            for i in range(len(p_refs)):
                pltpu.make_async_remote_copy(
                    src_ref=mine(i, _linear(*peer)), dst_ref=landing(i),
                    send_sem=send_sems.at[k - 1], recv_sem=recv_sems.at[k - 1], device_id=peer, device_id_type=MESH).start()

    def wait(self, out_ref, send_sems, recv_sems, local_sem):
        x, y, c = _my_place()
        whole = out_ref.at[_linear(x, y, c)]
        for k in range(1, N_DEV):
            both = pltpu.make_async_remote_copy(
                src_ref=whole, dst_ref=whole, send_sem=send_sems.at[k - 1], recv_sem=recv_sems.at[k - 1],
                device_id=_relative(x, y, c, k), device_id_type=MESH)
            both.wait_recv()
            both.wait_send()
        pltpu.make_async_copy(whole, whole, local_sem).wait()


class _Relay(_Traffic):
    def __init__(self, piece):
        super().__init__("spread", [piece])

    @staticmethod
    def _chips(x, y):
        return [(1 - x, y), (x, 1 - y), (1 - x, 1 - y)]

    @staticmethod
    def _copy(k, block, to, out_ref, send_sems, recv_sems, src=None):
        slot = out_ref.at[_linear(*block)]
        return pltpu.make_async_remote_copy(src_ref=slot if src is None else src, dst_ref=slot, send_sem=send_sems.at[k],
                                            recv_sem=recv_sems.at[k], device_id=to, device_id_type=MESH)

    def start(self, p_refs, out_ref, send_sems, recv_sems, local_sem):
        x, y, c = _my_place()
        me, sems = (x, y, c), (out_ref, send_sems, recv_sems)
        pltpu.make_async_copy(p_refs[0], out_ref.at[_linear(*me)], local_sem).start()
        self._copy(0, me, (x, y, 1 - c), *sems, src=p_refs[0]).start()
        for j, chip in enumerate(self._chips(x, y)):
            self._copy(1 + j, me, (*chip, c), *sems, src=p_refs[0]).start()

    def relay(self, out_ref, send_sems, recv_sems, local_sem):
        x, y, c = _my_place()
        sems = (out_ref, send_sems, recv_sems)
        for j, chip in enumerate(self._chips(x, y)):
            self._copy(1 + j, (*chip, c), (x, y, c), *sems).wait_recv()
            self._copy(4 + j, (*chip, c), (x, y, 1 - c), *sems).start()

    def wait(self, out_ref, send_sems, recv_sems, local_sem):
        x, y, c = _my_place()
        me, sems = (x, y, c), (out_ref, send_sems, recv_sems)
        self._copy(0, (x, y, 1 - c), me, *sems).wait_recv()
        for j, chip in enumerate(self._chips(x, y)):
            self._copy(4 + j, (*chip, 1 - c), me, *sems).wait_recv()
        for k in range(N_DEV - 1):
            self._copy(k, me, (x, y, 1 - c), *sems).wait_send()
        whole = out_ref.at[_linear(*me)]
        pltpu.make_async_copy(whole, whole, local_sem).wait()


def _sum_blocks(parts, name):
    n, r, cdim = parts.shape
    tr = _tile(r, 640, 16)

    def body(p_ref, o_ref):
        acc = p_ref[0].astype(F32)
        for d in range(1, n):
            acc = acc + p_ref[d].astype(F32)
        o_ref[...] = acc

    return pl.pallas_call(
        body, name=name, grid=(r // tr,), in_specs=[pl.BlockSpec((n, tr, cdim), lambda i: (0, i, 0))],
        out_specs=pl.BlockSpec((tr, cdim), lambda i: (i, 0)), out_shape=jax.ShapeDtypeStruct((r, cdim), F32),
        compiler_params=_params("parallel"),
    )(parts)


def _adamw_math(w, g, m, v):
    m = ADAM_B1 * m + (1.0 - ADAM_B1) * g
    v = ADAM_B2 * v + (1.0 - ADAM_B2) * (g * g)
    m_hat = m / (1.0 - ADAM_B1 ** ADAM_STEP)
    v_hat = v / (1.0 - ADAM_B2 ** ADAM_STEP)
    delta = -ADAM_LR * (m_hat / (jnp.sqrt(v_hat) + ADAM_EPS) + ADAM_WD * w)
    return delta, m, v


def _adamw(w, g, m, v, name):
    def body(w_ref, g_ref, m_ref, v_ref, d_ref, nm_ref, nv_ref):
        d_ref[...], nm_ref[...], nv_ref[...] = _adamw_math(w_ref[...], g_ref[...], m_ref[...], v_ref[...])

    out = jax.ShapeDtypeStruct(w.shape, F32)
    return pl.pallas_call(body, name=name, out_shape=(out, out, out),
                          compiler_params=pltpu.CompilerParams(vmem_limit_bytes=VMEM_LIMIT))(w, g, m, v)


def _small_all_reduce_adamw(parts, loss_part, ws, ms, vs, name):
    sizes = [p.shape[1] for p in parts] + [1]
    spots = [sum(-(-n // LANES) * LANES for n in sizes[:i]) for i in range(len(sizes))]
    width = spots[-1] + LANES
    k = len(parts)

    def reduce_body(*refs):
        p_refs, tot_ref, rows, send_sems, recv_sems = refs[:k + 1], *refs[k + 1:]
        x, y, c = _my_place()
        me = _linear(x, y, c)
        rows[me] = jnp.zeros((1, width), F32)
        for i in range(k + 1):
            rows[me, :, spots[i]:spots[i] + sizes[i]] = p_refs[i][...]
        copies = []
        for rel in range(1, N_DEV):
            copies.append(pltpu.make_async_remote_copy(
                src_ref=rows.at[me], dst_ref=rows.at[me], send_sem=send_sems.at[rel - 1], recv_sem=recv_sems.at[rel - 1],
                device_id=_relative(x, y, c, rel), device_id_type=MESH))
        for cp in copies:
            cp.start()
        for cp in copies:
            cp.wait_recv()
        for cp in copies:
            cp.wait_send()
        total = rows[0]
        for d in range(1, N_DEV):
            total = total + rows[d]
        tot_ref[...] = total

    total = pl.pallas_call(
        reduce_body, name=name, out_shape=jax.ShapeDtypeStruct((1, width), F32),
        scratch_shapes=[pltpu.VMEM((N_DEV, 1, width), F32), pltpu.SemaphoreType.DMA((7,)), pltpu.SemaphoreType.DMA((7,))],
    )(*parts, loss_part)

    def adamw_body(*refs):
        tot_ref, w_refs, m_refs, v_refs, outs = refs[0], refs[1:k + 1], refs[k + 1:2 * k + 1], refs[2 * k + 1:3 * k + 1], refs[3 * k + 1:]
        for i in range(k):
            g = tot_ref[:, spots[i]:spots[i] + sizes[i]]
            outs[4 * i][...] = g
            outs[4 * i + 1][...], outs[4 * i + 2][...], outs[4 * i + 3][...] = _adamw_math(w_refs[i][...], g, m_refs[i][...], v_refs[i][...])
        outs[4 * k][...] = tot_ref[:, spots[k]:spots[k] + 1]

    out_shape = [jax.ShapeDtypeStruct((1, n), F32) for n in sizes[:k] for _ in range(4)] + [jax.ShapeDtypeStruct((1, 1), F32)]
    res = pl.pallas_call(adamw_body, name=name + "_adamw", out_shape=tuple(out_shape))(total, *ws, *ms, *vs)
    return [res[4 * i:4 * i + 4] for i in range(k)], res[4 * k]


def _pad_rows(a, rows):
    return jnp.pad(a, ((0, rows - a.shape[0]), (0, 0)))


def kernel(x, positions, norm_mix_g, w_in, b_fgate, q_norm_g, w_uq, kv_norm_g, w_ukv, fox_out_g, mla_out_g, w_o, norm_ffn_g, w_gate, w_up, w_down, final_norm_g, loss_target, m_norm_mix_g, m_w_in, m_b_fgate, m_q_norm_g, m_w_uq, m_kv_norm_g, m_w_ukv, m_fox_out_g, m_mla_out_g, m_w_o, m_norm_ffn_g, m_w_gate, m_w_up, m_w_down, m_final_norm_g, v_norm_mix_g, v_w_in, v_b_fgate, v_q_norm_g, v_w_uq, v_kv_norm_g, v_w_ukv, v_fox_out_g, v_mla_out_g, v_w_o, v_norm_ffn_g, v_w_gate, v_w_up, v_w_down, v_final_norm_g):
    bl, s, d = x.shape
    t = bl * s
    bh = bl * HEADS
    tq = _tile(s, 256)
    grp = s // LANES
    fw = HEADS * HEAD_DIM
    q_rank, kv_rank = w_uq.shape[1], w_ukv.shape[1]
    in_cols = w_in.shape[2]
    n_in = N_DEV * in_cols
    ff = N_DEV * w_gate.shape[2]
    half = MLA_ROPE // 2
    o_kvlat, o_krope, o_flogit = q_rank, q_rank + kv_rank, q_rank + kv_rank + LANES
    b_cols = -(-(o_flogit + HEADS) // LANES) * LANES

    tr = lambda w: jnp.transpose(w[0])
    in_rows = -(-in_cols // 16) * 16
    uq_rows = w_uq.shape[2] * q_rank // d
    ukv_rows = w_ukv.shape[2] * kv_rank // d
    pieces = [_pad_rows(tr(w_in), in_rows), _pad_rows(tr(w_uq).reshape(uq_rows, d), -(-uq_rows // 16) * 16),
              tr(w_ukv).reshape(ukv_rows, d), w_o[0], tr(w_gate), tr(w_up), w_down[0]]
    pieces = [p.astype(BF16) for p in pieces]
    offs = [0]
    for p in pieces:
        offs.append(offs[-1] + p.shape[0])
    legs = [(0, 1), (1, 5), (5, 7)]
    gathered = {}

    def full(i, rows):
        leg = next(n for n, (lo, hi) in enumerate(legs) if lo <= i < hi)
        base = offs[legs[leg][0]]
        return gathered[leg][:, offs[i] - base:offs[i] - base + rows]

    x2d = x.reshape(t, d)
    h1, gathered[0] = _rmsnorm(x2d, 0, d, norm_mix_g, BF16, "norm_mix", traffic=_Relay(pieces[0]))

    w_in_t = full(0, in_cols).reshape(n_in, d)
    n_qkv = 3 * fw
    w_in_a = w_in_t[:n_qkv].reshape(3, PAIRS, LANES, d).transpose(1, 0, 2, 3).reshape(n_qkv, d)
    lat0, rope0 = n_qkv + HEADS, n_qkv + HEADS + q_rank + kv_rank
    k_rep = jnp.broadcast_to(w_in_t[rope0:].reshape(2, 1, half, d), (2, 4, half, d)).reshape(LANES, d)
    w_in_b = jnp.concatenate([w_in_t[lat0:rope0], k_rep, w_in_t[n_qkv:lat0],
                              jnp.zeros((b_cols - o_flogit - HEADS, d), BF16)], axis=0)

    def per_head_rows(a):
        return a.reshape(bl, s, HEADS).transpose(0, 2, 1).reshape(bh, 1, s)

    proj_a = _matmul(h1, w_in_a, "nt", BF16, "proj_fox", tm=1024, tn=6 * LANES)
    proj_b = _matmul(h1, w_in_b, "nt", F32, "proj_mla", tm=1024, tn=b_cols)

    z = proj_b[:, o_flogit:o_flogit + HEADS].reshape(bl, s, HEADS).transpose(0, 2, 1).reshape(bh * grp, LANES)
    bcol = jnp.broadcast_to(b_fgate.reshape(1, HEADS, 1), (bl, HEADS, grp)).reshape(bh * grp, 1)
    c = _fgate(z, bcol, grp, "forget_gate")
    c_bias = c.reshape(bh, 1, s)
    fox_o, fox_lse, gathered[1] = _attn_fwd((proj_a,), c_bias, HEAD_DIM ** -0.5, bl, s, tq, "fox_attention",
                                            traffic=_Traffic("spread", pieces[legs[1][0]:legs[1][1]]))
    w_uq_h = full(1, uq_rows).reshape(HEADS, MLA_QK, q_rank)
    w_uq_pe = jnp.concatenate([w_uq_h[:, HEAD_DIM:HEAD_DIM + half].reshape(2, 1, 4 * half, q_rank),
                               w_uq_h[:, HEAD_DIM + half:].reshape(2, 1, 4 * half, q_rank)], axis=1).reshape(2 * LANES, q_rank)
    w_uq_p = jnp.concatenate([w_uq_h[:, :HEAD_DIM].reshape(fw, q_rank), w_uq_pe], axis=0)
    w_ukv_p = full(2, ukv_rows).reshape(PAIRS, 2, 2, HEAD_DIM, kv_rank).transpose(0, 2, 1, 3, 4).reshape(2 * fw, kv_rank)
    w_o_f = full(3, w_o.shape[1]).reshape(-1, d)
    w_gate_t = full(4, ff // N_DEV).reshape(ff, d)

    inv_freq = ROPE_THETA ** (-jnp.arange(0, MLA_ROPE, 2, dtype=F32) / MLA_ROPE)
    ang = positions.astype(F32).reshape(t, 1) * inv_freq[None, :]
    rope_cos, rope_sin = jnp.cos(ang), jnp.sin(ang)
    qn, kvn, q_all, kv_all, kpe = _mla_prep(proj_b, q_rank, kv_rank, q_norm_g, kv_norm_g, w_uq_p, w_ukv_p, fw,
                                            rope_cos, rope_sin, "mla_prep")
    mla_ops = (q_all, kv_all, kpe)
    mla_o, mla_lse, gathered[2] = _attn_fwd(mla_ops, None, MLA_QK ** -0.5, bl, s, tq, "mla_attention",
                                            traffic=_Traffic("spread", pieces[legs[2][0]:legs[2][1]]))
    w_up_t, w_down_f = full(5, ff // N_DEV).reshape(ff, d), full(6, ff // N_DEV).reshape(ff, d)

    both = [(d, F32), (d, BF16)]
    cat, x1, h2 = _rows_matmul([(None, w_o_f, "nn")], [fox_o, mla_o, x2d], [fox_out_g, mla_out_g, norm_ffn_g], _residual_norm,
                               [(2 * fw, BF16)] + both, [], "norm_out_proj_out_norm_ffn", prologue=_out_norm)
    act_by_gate, act_by_up, act = _ffn_up(h2, w_gate_t, w_up_t, "ffn_gate_up", tm=1024)
    dx2, dx2_b, dg_final, loss_part = _rows_matmul(
        [(act, w_down_f, "nn")], [x1, loss_target.reshape(t, d)], [final_norm_g.reshape(1, d)], _residual_loss_bwd,
        both, [d, 1], "ffn_down_final_norm_loss")

    d_gate, d_up = _ffn_down_bwd(dx2_b, w_down_f, act_by_gate, act_by_up, "d_ffn_down")
    dw_down = _matmul(act, dx2_b, "tn", BF16, "dw_down", tm=ff // 2, tn=d, tk=2048)
    dw_gate = _matmul(d_gate, h2, "tn", BF16, "dw_gate", tm=ff // 2, tn=d, tk=2048)
    dw_up = _matmul(d_up, h2, "tn", BF16, "dw_up", tm=ff // 2, tn=d, tk=2048)
    dx1, dx1_b, dg_ffn = _rows_matmul([(d_gate, w_gate_t, "nn"), (d_up, w_up_t, "nn")], [x1, dx2], [norm_ffn_g],
                                      _norm_bwd_residual, both, [d], "d_ffn_gate_up_norm_ffn", tm=512)
    dw_o = _matmul(cat, dx1_b, "tn", BF16, "dw_o", tn=d, tk=2048)
    d_fox_o, d_mla_o, fox_delta, mla_delta, dg_fox, dg_mla = _rows_matmul(
        [(dx1_b, w_o_f, "nt")], [fox_o, mla_o], [fox_out_g, mla_out_g], _out_norm_bwd,
        [(fw, BF16), (fw, BF16), (HEADS, F32), (HEADS, F32)], [fw, fw], "d_proj_out_norm_out")

    per_dev = lambda a: a.reshape(N_DEV, -1, d)
    late_grads = [per_dev(dw_o), per_dev(dw_gate), per_dev(dw_up), per_dev(dw_down)]
    dproj_a, dc, g_late = _attn_bwd((proj_a,), d_fox_o, fox_lse, per_head_rows(fox_delta),
                                    c_bias, HEAD_DIM ** -0.5, bl, s, tq, "d_fox_attention", traffic=_Traffic("swap", late_grads))
    dz, db_fgate = _fgate_bwd(z, bcol, dc.reshape(bh * grp, LANES), grp, "d_forget_gate")
    d_flogit = dz.reshape(bl, HEADS, s).transpose(0, 2, 1).reshape(t, HEADS)

    dq_nope, dkv_all, dq_pe, dk_pe = _attn_bwd(mla_ops, d_mla_o, mla_lse, per_head_rows(mla_delta),
                                               None, MLA_QK ** -0.5, bl, s, tq, "d_mla_attention")
    d_tail = jnp.pad(d_flogit, ((0, 0), (0, b_cols - o_flogit - HEADS)))
    dproj_b, dq_rot, dg_q, dg_kv = _mla_prep_bwd(dq_nope, dq_pe, dkv_all, dk_pe, d_tail, proj_b, q_rank, kv_rank,
                                                 q_norm_g, kv_norm_g, w_uq_p, w_ukv_p, rope_cos, rope_sin, "d_mla_prep")
    dw_uq_nope = _matmul(dq_nope, qn, "tn", BF16, "dw_uq_nope", tn=q_rank, tk=1024)
    dw_uq_pe = _matmul(dq_rot, qn, "tn", BF16, "dw_uq_rope", tn=q_rank, tk=1024)
    dw_ukv_p = _matmul(dkv_all, kvn, "tn", BF16, "dw_ukv", tn=kv_rank, tk=1024)
    dw_in_a = _matmul(dproj_a, h1, "tn", BF16, "dw_in_fox", tm=6 * LANES, tn=d, tk=2048)
    dw_in_b = _matmul(dproj_b, h1, "tn", F32, "dw_in_mla", tm=b_cols, tn=d, tk=1024)

    dw_krope = dw_in_b[o_krope:o_flogit].reshape(2, 4, half, d).sum(axis=1).reshape(MLA_ROPE, d)
    dw_in_t = jnp.concatenate([dw_in_a.reshape(PAIRS, 3, LANES, d).transpose(1, 0, 2, 3).reshape(n_qkv, d),
                               dw_in_b[o_flogit:o_flogit + HEADS].astype(BF16), dw_in_b[:o_krope].astype(BF16),
                               dw_krope.astype(BF16)], axis=0)
    pad_dev = lambda a, rows: jnp.pad(a, ((0, 0), (0, rows - a.shape[1]), (0, 0)))
    dw_uq_pe5 = dw_uq_pe.reshape(2, 2, 4, half, q_rank)
    dw_uq_h = jnp.concatenate([dw_uq_nope.reshape(HEADS, HEAD_DIM, q_rank), dw_uq_pe5[:, 0].reshape(HEADS, half, q_rank),
                               dw_uq_pe5[:, 1].reshape(HEADS, half, q_rank)], axis=1)
    dw_ukv_h = dw_ukv_p.reshape(PAIRS, 2, 2, HEAD_DIM, kv_rank).transpose(0, 2, 1, 3, 4).reshape(HEADS, 2 * HEAD_DIM, kv_rank)
    n_last = 3
    last_grads = [pad_dev(per_dev(dw_in_t), pieces[0].shape[0]), pad_dev(per_dev(dw_uq_h), pieces[1].shape[0]), per_dev(dw_ukv_h)]
    grad_x, dg_mix, g_last = _rows_matmul([(dproj_a, w_in_a, "nn"), (dproj_b, w_in_b, "nn")], [x2d, dx1], [norm_mix_g],
                                          _norm_bwd_residual, [(d, F32)], [d], "d_proj_in_norm_mix",
                                          traffic=_Traffic("swap", last_grads))
    g_last = _sum_blocks(g_last, "sum_last_grads")
    g_late = _sum_blocks(g_late, "sum_late_grads")

    def mine(i, rows):
        src, base = (g_last, 0) if i < n_last else (g_late, offs[n_last])
        return src[offs[i] - base:offs[i] - base + rows]

    big = [
        ("w_in", w_in, m_w_in, v_w_in, mine(0, in_cols), True),
        ("w_uq", w_uq, m_w_uq, v_w_uq, mine(1, uq_rows).reshape(-1, q_rank), True),
        ("w_ukv", w_ukv, m_w_ukv, v_w_ukv, mine(2, ukv_rows).reshape(-1, kv_rank), True),
        ("w_o", w_o, m_w_o, v_w_o, mine(3, w_o.shape[1]), False),
        ("w_gate", w_gate, m_w_gate, v_w_gate, mine(4, ff // N_DEV), True),
        ("w_up", w_up, m_w_up, v_w_up, mine(5, ff // N_DEV), True),
        ("w_down", w_down, m_w_down, v_w_down, mine(6, ff // N_DEV), False),
    ]
    out = {}
    for nm, w, m, v, g, transposed in big:
        lay = (lambda a: a[0].T) if transposed else (lambda a: a[0])
        back = (lambda a: a.T[None]) if transposed else (lambda a: a[None])
        dl, new_m, new_v = _adamw(lay(w), g, lay(m), lay(v), "adamw_" + nm)
        out[nm] = (back(g), back(dl), back(new_m), back(new_v))

    smalls = [("norm_mix_g", norm_mix_g, m_norm_mix_g, v_norm_mix_g, dg_mix),
              ("b_fgate", b_fgate, m_b_fgate, v_b_fgate, db_fgate.reshape(1, HEADS)),
              ("q_norm_g", q_norm_g, m_q_norm_g, v_q_norm_g, dg_q),
              ("kv_norm_g", kv_norm_g, m_kv_norm_g, v_kv_norm_g, dg_kv),
              ("fox_out_g", fox_out_g, m_fox_out_g, v_fox_out_g, dg_fox),
              ("mla_out_g", mla_out_g, m_mla_out_g, v_mla_out_g, dg_mla),
              ("norm_ffn_g", norm_ffn_g, m_norm_ffn_g, v_norm_ffn_g, dg_ffn),
              ("final_norm_g", final_norm_g, m_final_norm_g, v_final_norm_g, dg_final)]
    flat = lambda a: a.reshape(1, -1)
    results, loss = _small_all_reduce_adamw([e[4] for e in smalls], loss_part, [flat(e[1]) for e in smalls],
                                            [flat(e[2]) for e in smalls], [flat(e[3]) for e in smalls], "reduce_small_adamw")
    for (nm, w, _, _, _), res in zip(smalls, results):
        out[nm] = tuple(a.reshape(w.shape) for a in res)
    loss = loss[0, 0]

    order = ["norm_mix_g", "w_in", "b_fgate", "q_norm_g", "w_uq", "kv_norm_g", "w_ukv", "fox_out_g", "mla_out_g", "w_o",
             "norm_ffn_g", "w_gate", "w_up", "w_down", "final_norm_g"]
    return (loss, grad_x.reshape(bl, s, d), *[out[n][0] for n in order], *[out[n][1] for n in order],
            *[out[n][2] for n in order], *[out[n][3] for n in order])
```

```python
import math

import jax
import jax.numpy as jnp
from jax import lax
from jax.experimental import pallas as pl
from jax.experimental.pallas import tpu as pltpu

F32 = jnp.float32
BF16 = jnp.bfloat16
MESH = pl.DeviceIdType.MESH

N_DEV = 8
HEADS = 8
HEAD_DIM = 64
PAIRS = HEADS // 2
MLA_ROPE = 32
MLA_QK = HEAD_DIM + MLA_ROPE
ROPE_THETA = 10000.0
NORM_EPS = 1e-6
ADAM_LR, ADAM_B1, ADAM_B2, ADAM_EPS, ADAM_WD, ADAM_STEP = 0.001, 0.9, 0.999, 1e-08, 0.01, 10

LANES = 128
MASKED = -1e30
VMEM_LIMIT = 48 * 1024 * 1024
VMEM_SMALL = 32 * 1024 * 1024

_DIMS = {"nn": (((1,), (0,)), ((), ())), "nt": (((1,), (1,)), ((), ())), "tn": (((0,), (0,)), ((), ()))}


def _params(*sem, vmem=VMEM_LIMIT):
    return pltpu.CompilerParams(dimension_semantics=sem, vmem_limit_bytes=vmem)


def _dot(a, b, mode):
    return lax.dot_general(a.astype(BF16), b.astype(BF16), _DIMS[mode], preferred_element_type=F32)


def _tile(n, pref, unit=8):
    if n <= pref:
        return n
    t = pref - pref % unit
    while n % t:
        t -= unit
    return t


def _log2(n):
    assert n & (n - 1) == 0
    return n.bit_length() - 1


def _matmul(a, b, mode, out_dtype, name, tm=512, tn=512, tk=None):
    if mode == "nn":
        (m, kd), n = a.shape, b.shape[1]
    elif mode == "nt":
        (m, kd), n = a.shape, b.shape[0]
    else:
        (kd, m), n = a.shape, b.shape[1]
    tm, tn = _tile(m, tm, LANES if mode == "tn" else 16), _tile(n, tn, LANES)
    tk = kd if tk is None else _tile(kd, tk, LANES)
    nk = kd // tk
    a_spec = pl.BlockSpec((tk, tm), lambda i, j, k: (k, i)) if mode == "tn" else pl.BlockSpec((tm, tk), lambda i, j, k: (i, k))
    b_spec = pl.BlockSpec((tn, tk), lambda i, j, k: (j, k)) if mode == "nt" else pl.BlockSpec((tk, tn), lambda i, j, k: (k, j))
    o_spec = pl.BlockSpec((tm, tn), lambda i, j, k: (i, j))

    def body(a_ref, b_ref, o_ref, *acc):
        part = _dot(a_ref[...], b_ref[...], mode)
        if nk == 1:
            o_ref[...] = part.astype(out_dtype)
        else:
            acc_ref, k = acc[0], pl.program_id(2)

            @pl.when(k == 0)
            def _():
                acc_ref[...] = part

            @pl.when(k > 0)
            def _():
                acc_ref[...] += part

            @pl.when(k == nk - 1)
            def _():
                o_ref[...] = acc_ref[...].astype(out_dtype)

    return pl.pallas_call(
        body, name=name, grid=(m // tm, n // tn, nk), in_specs=[a_spec, b_spec], out_specs=o_spec,
        out_shape=jax.ShapeDtypeStruct((m, n), out_dtype),
        scratch_shapes=[pltpu.VMEM((tm, tn), F32)] if nk > 1 else [],
        compiler_params=_params("parallel", "parallel", "arbitrary"),
    )(a, b)


def _rstd(x):
    return lax.rsqrt(jnp.mean(x * x, axis=-1, keepdims=True) + NORM_EPS)


def _norm_bwd(x, g, dy):
    r = _rstd(x)
    xh = x * r
    u = dy * g
    dx = r * (u - xh * jnp.mean(u * xh, axis=-1, keepdims=True))
    return dx, jnp.sum(dy * xh, axis=0, keepdims=True)


def _rmsnorm(x, col, width, g, out_dtype, name, traffic=None):
    t = x.shape[0]
    tm = _tile(t, 512)
    steps = t // tm
    n_carried = len(traffic.pieces) if traffic else 0

    def body(*refs):
        x_ref, g_ref, o_ref = refs[0], refs[1], refs[2 + n_carried]
        if traffic:
            carried_in, carried_out, sems = refs[2:2 + n_carried], refs[3 + n_carried], refs[4 + n_carried:]

            @pl.when(pl.program_id(0) == 0)
            def _():
                traffic.start(carried_in, carried_out, *sems)

            if isinstance(traffic, _Relay):
                @pl.when(pl.program_id(0) == steps - 1)
                def _():
                    traffic.relay(carried_out, *sems)

        xv = x_ref[...]
        o_ref[...] = ((xv * _rstd(xv)) * g_ref[...]).astype(out_dtype)
        if traffic:
            @pl.when(pl.program_id(0) == steps - 1)
            def _():
                traffic.wait(carried_out, *sems)

    in_specs = [pl.BlockSpec((tm, width), lambda i: (i, col)), pl.BlockSpec((1, width), lambda i: (0, 0))]
    out_specs = [pl.BlockSpec((tm, width), lambda i: (i, 0))]
    out_shape = [jax.ShapeDtypeStruct((t, width), out_dtype)]
    if traffic:
        in_specs += traffic.in_specs
        out_specs.append(traffic.out_spec)
        out_shape.append(traffic.out_shape)
    out = pl.pallas_call(
        body, name=name, grid=(steps,), in_specs=in_specs, out_specs=tuple(out_specs), out_shape=tuple(out_shape),
        scratch_shapes=traffic.scratch if traffic else [],
        compiler_params=_params("arbitrary" if traffic else "parallel", vmem=VMEM_SMALL),
    )(x, g, *(traffic.pieces if traffic else []))
    return out if traffic else out[0]


def _split3(x):
    hi = x.astype(BF16)
    r1 = x - hi.astype(F32)
    mid = r1.astype(BF16)
    lo = (r1 - mid.astype(F32)).astype(BF16)
    return hi, mid, lo


def _dot_x01(x, m01):
    hi, mid, lo = _split3(x)
    d = lambda p: lax.dot_general(p, m01, _DIMS["nn"], preferred_element_type=F32)
    return (d(lo) + d(mid)) + d(hi)


def _dot_01x(m01, x):
    hi, mid, lo = _split3(x)
    d = lambda p: lax.dot_general(m01, p, _DIMS["nn"], preferred_element_type=F32)
    return (d(lo) + d(mid)) + d(hi)


def _rows_matmul(terms, rows_in, vecs_in, epilogue, rows_out, sums_out, name, tm=512, prologue=None, traffic=None):
    t = rows_in[0].shape[0]
    tm = _tile(t, tm, 16)
    steps = t // tm
    n_rows, n_vecs = len(rows_in), len(vecs_in)
    n_ab = sum(1 + (a is not None) for a, _, _ in terms)
    n_carried = len(traffic.pieces) if traffic else 0
    halves = [slice(0, tm // 2), slice(tm // 2, tm)] if tm % 32 == 0 else [slice(0, tm)]

    def body(*refs):
        vecs = [r[...] for r in refs[n_ab + n_rows:n_ab + n_rows + n_vecs]]
        out_at = n_ab + n_rows + n_vecs + n_carried
        sum_refs = refs[out_at + len(rows_out):out_at + len(rows_out) + len(sums_out)]
        if traffic:
            carried_in, carried_out, sems = refs[out_at - n_carried:out_at], refs[len(refs) - 4], refs[len(refs) - 3:]

            @pl.when(pl.program_id(0) == 0)
            def _():
                traffic.start(carried_in, carried_out, *sems)

        @pl.when(pl.program_id(0) == 0)
        def _():
            for ref in sum_refs:
                ref[...] = jnp.zeros_like(ref)

        staged = []
        for rows_of in halves:
            row_blocks = [r[rows_of, :] for r in refs[n_ab:n_ab + n_rows]]
            made = prologue(row_blocks, vecs) if prologue else None
            acc, at = None, 0
            for a, _, mode in terms:
                lhs = made if a is None else refs[at][rows_of, :]
                at += a is not None
                part = _dot(lhs, refs[at][...], mode)
                at += 1
                acc = part if acc is None else acc + part
            staged.append((rows_of, row_blocks, made, acc))
        for rows_of, row_blocks, made, acc in staged:
            row_vals, sum_vals = epilogue(acc, row_blocks, vecs)
            if prologue:
                row_vals = [made] + row_vals
            for ref, val, (_, dtype) in zip(refs[out_at:], row_vals, rows_out):
                ref[rows_of, :] = val.astype(dtype)
            for ref, val in zip(sum_refs, sum_vals):
                ref[...] += val
        if traffic:
            @pl.when(pl.program_id(0) == steps - 1)
            def _():
                traffic.wait(carried_out, *sems)

    rows = lambda w: pl.BlockSpec((tm, w), lambda i: (i, 0))
    whole = lambda a: pl.BlockSpec(a.shape, lambda i: (0, 0))
    in_specs, args = [], []
    for a, b, _ in terms:
        in_specs += ([rows(a.shape[1])] if a is not None else []) + [whole(b)]
        args += ([a] if a is not None else []) + [b]
    in_specs += [rows(r.shape[1]) for r in rows_in] + [whole(v) for v in vecs_in]
    args += list(rows_in) + list(vecs_in)
    out_specs = [rows(w) for w, _ in rows_out] + [pl.BlockSpec((1, w), lambda i: (0, 0)) for w in sums_out]
    out_shape = [jax.ShapeDtypeStruct((t, w), dt) for w, dt in rows_out] + [jax.ShapeDtypeStruct((1, w), F32) for w in sums_out]
    if traffic:
        in_specs += traffic.in_specs
        args += traffic.pieces
        out_specs.append(traffic.out_spec)
        out_shape.append(traffic.out_shape)
    return pl.pallas_call(
        body, name=name, grid=(steps,), in_specs=in_specs, out_specs=tuple(out_specs), out_shape=tuple(out_shape),
        scratch_shapes=traffic.scratch if traffic else [], compiler_params=_params("arbitrary"),
    )(*args)


def _out_norm(rows, vecs):
    (f, m), (gf, gm) = rows[:2], vecs[:2]
    return jnp.concatenate([((f * _rstd(f)) * gf).astype(BF16), ((m * _rstd(m)) * gm).astype(BF16)], axis=1)


def _residual_norm(acc, rows, vecs):
    x1 = rows[-1] + acc
    return [x1, (x1 * _rstd(x1)) * vecs[-1]], []


def _residual_loss_bwd(acc, rows, vecs):
    x2, gv = rows[0] + acc, vecs[0]
    diff = (x2 * _rstd(x2)) * gv - rows[1]
    dx, dg = _norm_bwd(x2, gv, diff / x2.shape[1])
    return [dx, dx], [dg, 0.5 * jnp.sum(jnp.mean(diff * diff, axis=-1, keepdims=True), axis=0, keepdims=True)]


def _norm_bwd_residual(acc, rows, vecs):
    dy = acc + rows[2] if len(rows) > 2 else acc
    dx, dg = _norm_bwd(rows[0], vecs[0], dy)
    if len(rows) > 1:
        dx = dx + rows[1]
    return [dx, dx], [dg]


def _out_norm_bwd(acc, rows, vecs):
    (f, m), w = rows, rows[0].shape[1]
    nh = w // HEAD_DIM
    lane_head = lax.shift_right_logical(lax.broadcasted_iota(jnp.int32, (w, nh), 0), _log2(HEAD_DIM))
    sel = (lane_head == lax.broadcasted_iota(jnp.int32, (w, nh), 1)).astype(BF16)
    dfo, dgf = _norm_bwd(f, vecs[0], acc[:, :w])
    dmo, dgm = _norm_bwd(m, vecs[1], acc[:, w:])
    return [dfo, dmo, _dot_x01(dfo * f, sel), _dot_x01(dmo * m, sel)], [dgf, dgm]


def _ffn_up(h, wg_t, wu_t, name, tm=512, tf=1408):
    t, d = h.shape
    f = wg_t.shape[0]
    tm, tf = _tile(t, tm, 16), _tile(f, tf, LANES)
    tok = pl.BlockSpec((tm, tf), lambda j, i: (i, j))
    wt = pl.BlockSpec((tf, d), lambda j, i: (j, 0))

    def body(h_ref, wg_ref, wu_ref, dg_ref, du_ref, a_ref):
        hv = h_ref[...]
        g, u = _dot(hv, wg_ref[...], "nt"), _dot(hv, wu_ref[...], "nt")
        sg = jax.nn.sigmoid(g)
        silu = g * sg
        dg_ref[...] = (u * (sg * (1.0 + g * (1.0 - sg)))).astype(BF16)
        du_ref[...] = silu.astype(BF16)
        a_ref[...] = (silu * u).astype(BF16)

    return pl.pallas_call(
        body, name=name, grid=(f // tf, t // tm), in_specs=[pl.BlockSpec((tm, d), lambda j, i: (i, 0)), wt, wt],
        out_specs=(tok, tok, tok),
        out_shape=(jax.ShapeDtypeStruct((t, f), BF16), jax.ShapeDtypeStruct((t, f), BF16), jax.ShapeDtypeStruct((t, f), BF16)),
        compiler_params=_params("parallel", "parallel"),
    )(h, wg_t, wu_t)


def _ffn_down_bwd(dy, w_down, act_by_gate, act_by_up, name, tm=512, tf=1408):
    t, d = dy.shape
    f = w_down.shape[0]
    tm, tf = _tile(t, tm, 16), _tile(f, tf, LANES)
    tok = pl.BlockSpec((tm, tf), lambda j, i: (i, j))

    def body(dy_ref, w_ref, g_ref, u_ref, dg_ref, du_ref):
        da = _dot(dy_ref[...], w_ref[...], "nt")
        dg_ref[...] = (da * g_ref[...].astype(F32)).astype(BF16)
        du_ref[...] = (da * u_ref[...].astype(F32)).astype(BF16)

    return pl.pallas_call(
        body, name=name, grid=(f // tf, t // tm),
        in_specs=[pl.BlockSpec((tm, d), lambda j, i: (i, 0)), pl.BlockSpec((tf, d), lambda j, i: (j, 0)), tok, tok],
        out_specs=(tok, tok),
        out_shape=(jax.ShapeDtypeStruct((t, f), BF16), jax.ShapeDtypeStruct((t, f), BF16)),
        compiler_params=_params("parallel", "parallel", vmem=VMEM_SMALL),
    )(dy, w_down, act_by_gate, act_by_up)


def _chunk_scan_mats(rows, grp, reverse):
    ii = lax.broadcasted_iota(jnp.int32, (LANES, LANES), 0)
    jj = lax.broadcasted_iota(jnp.int32, (LANES, LANES), 1)
    within = ((ii >= jj) if reverse else (ii <= jj)).astype(BF16)
    ones = jnp.ones((LANES, LANES), BF16)
    ri = lax.broadcasted_iota(jnp.int32, (rows, rows), 0)
    rj = lax.broadcasted_iota(jnp.int32, (rows, rows), 1)
    sh = _log2(grp)
    same = lax.shift_right_logical(ri, sh) == lax.shift_right_logical(rj, sh)
    across = (same & ((rj > ri) if reverse else (rj < ri))).astype(BF16)
    return within, ones, across


def _running_sum(v, mats):
    within, ones, across = mats
    return _dot_x01(v, within) + _dot_01x(across, _dot_x01(v, ones))


def _fgate(z, bcol, grp, name):
    rows = z.shape[0]

    def body(z_ref, b_ref, c_ref):
        zz = z_ref[...] + b_ref[...]
        log_f = jnp.minimum(zz, 0.0) - jnp.log1p(jnp.exp(-jnp.abs(zz)))
        c_ref[...] = _running_sum(log_f, _chunk_scan_mats(rows, grp, False))

    return pl.pallas_call(body, name=name, out_shape=jax.ShapeDtypeStruct(z.shape, F32),
                          compiler_params=pltpu.CompilerParams(vmem_limit_bytes=VMEM_SMALL))(z, bcol)


def _fgate_bwd(z, bcol, dc, grp, name):
    rows = z.shape[0]

    def body(z_ref, b_ref, dc_ref, dz_ref, db_ref):
        zz = z_ref[...] + b_ref[...]
        dz = _running_sum(dc_ref[...], _chunk_scan_mats(rows, grp, True)) * jax.nn.sigmoid(-zz)
        dz_ref[...] = dz
        head = lax.shift_right_logical(lax.broadcasted_iota(jnp.int32, (HEADS, rows), 1), _log2(grp)) & (HEADS - 1)
        sel = (head == lax.broadcasted_iota(jnp.int32, (HEADS, rows), 0)).astype(BF16)
        db_ref[...] = jnp.sum(_dot_01x(sel, dz), axis=1, keepdims=True)

    return pl.pallas_call(
        body, name=name,
        out_shape=(jax.ShapeDtypeStruct(z.shape, F32), jax.ShapeDtypeStruct((HEADS, 1), F32)),
        compiler_params=pltpu.CompilerParams(vmem_limit_bytes=VMEM_SMALL),
    )(z, bcol, dc)


def _rotate(x, cs, sn_signed):
    return x * cs + pltpu.roll(x, LANES // 2, axis=1) * sn_signed


def _rope_tables(cos, sin):
    half = cos.shape[1]
    freq = lax.broadcasted_iota(jnp.int32, (half, LANES), 0)
    lane = lax.broadcasted_iota(jnp.int32, (half, LANES), 1)
    hit = (lane & (half - 1)) == freq
    sign = jnp.where(lane < LANES // 2, -1.0, 1.0)
    return _dot_x01(cos, hit.astype(BF16)), _dot_x01(sin, jnp.where(hit, sign, 0.0).astype(BF16))


def _mla_prep(proj_b, q_rank, kv_rank, gq, gkv, w_uq_p, w_ukv_p, nope, cs, sn, name):
    t, bw = proj_b.shape
    qw, kvw = w_uq_p.shape[0], w_ukv_p.shape[0]
    tm = _tile(t, 512)
    rows = lambda w: pl.BlockSpec((tm, w), lambda i: (i, 0))
    whole = lambda a: pl.BlockSpec(a.shape, lambda i: (0, 0))

    def body(pb_ref, gq_ref, gkv_ref, wq_ref, wkv_ref, c_ref, s_ref, qn_ref, kvn_ref, q_ref, kv_ref, kpe_ref):
        c, s = _rope_tables(c_ref[...], s_ref[...])
        ql, kvl = pb_ref[:, :q_rank], pb_ref[:, q_rank:q_rank + kv_rank]
        qn = ((ql * _rstd(ql)) * gq_ref[...]).astype(BF16)
        kvn = ((kvl * _rstd(kvl)) * gkv_ref[...]).astype(BF16)
        qn_ref[...], kvn_ref[...] = qn, kvn
        q_raw = _dot(qn, wq_ref[...], "nt")
        q_ref[:, :nope] = q_raw[:, :nope].astype(BF16)
        for off in range(nope, qw, LANES):
            q_ref[:, off:off + LANES] = _rotate(q_raw[:, off:off + LANES], c, s).astype(BF16)
        kv_ref[...] = _dot(kvn, wkv_ref[...], "nt").astype(BF16)
        kpe_ref[...] = _rotate(pb_ref[:, q_rank + kv_rank:q_rank + kv_rank + LANES], c, s).astype(BF16)

    return pl.pallas_call(
        body, name=name, grid=(t // tm,),
        in_specs=[rows(bw), whole(gq), whole(gkv), whole(w_uq_p), whole(w_ukv_p), rows(cs.shape[1]), rows(sn.shape[1])],
        out_specs=(rows(q_rank), rows(kv_rank), rows(qw), rows(kvw), rows(LANES)),
        out_shape=(jax.ShapeDtypeStruct((t, q_rank), BF16), jax.ShapeDtypeStruct((t, kv_rank), BF16),
                   jax.ShapeDtypeStruct((t, qw), BF16), jax.ShapeDtypeStruct((t, kvw), BF16), jax.ShapeDtypeStruct((t, LANES), BF16)),
        compiler_params=_params("parallel", vmem=VMEM_SMALL),
    )(proj_b, gq, gkv, w_uq_p, w_ukv_p, cs, sn)


def _mla_prep_bwd(dq_nope, dq_pe, dkv_all, dk_pe, d_tail, proj_b, q_rank, kv_rank, gq, gkv, w_uq_p, w_ukv_p, cs, sn, name):
    t, bw = proj_b.shape
    nope, pw = dq_nope.shape[1], dq_pe.shape[1]
    tm = _tile(t, 512)
    rows = lambda w: pl.BlockSpec((tm, w), lambda i: (i, 0))
    whole = lambda a: pl.BlockSpec(a.shape, lambda i: (0, 0))
    o_k = q_rank + kv_rank

    def body(dqn_ref, dqp_ref, dkv_ref, dkp_ref, dt_ref, pb_ref, gq_ref, gkv_ref, wq_ref, wkv_ref, c_ref, s_ref,
             dpb_ref, dqr_ref, dgq_ref, dgkv_ref):
        c, s = _rope_tables(c_ref[...], -s_ref[...])
        for off in range(0, pw, LANES):
            dqr_ref[:, off:off + LANES] = _rotate(dqp_ref[:, off:off + LANES], c, s).astype(BF16)
        d_qn = _dot(dqn_ref[...], wq_ref[:nope, :], "nn") + _dot(dqr_ref[...], wq_ref[nope:, :], "nn")
        dq_lat, dgq = _norm_bwd(pb_ref[:, :q_rank], gq_ref[...], d_qn)
        dkv_lat, dgkv = _norm_bwd(pb_ref[:, q_rank:o_k], gkv_ref[...], _dot(dkv_ref[...], wkv_ref[...], "nn"))
        dpb_ref[:, :q_rank] = dq_lat.astype(BF16)
        dpb_ref[:, q_rank:o_k] = dkv_lat.astype(BF16)
        dpb_ref[:, o_k:o_k + LANES] = _rotate(dkp_ref[...], c, s).astype(BF16)
        dpb_ref[:, o_k + LANES:] = dt_ref[...].astype(BF16)

        @pl.when(pl.program_id(0) == 0)
        def _():
            dgq_ref[...] = jnp.zeros_like(dgq_ref)
            dgkv_ref[...] = jnp.zeros_like(dgkv_ref)

        dgq_ref[...] += dgq
        dgkv_ref[...] += dgkv

    return pl.pallas_call(
        body, name=name, grid=(t // tm,),
        in_specs=[rows(nope), rows(pw), rows(dkv_all.shape[1]), rows(LANES), rows(bw - o_k - LANES), rows(bw), whole(gq), whole(gkv),
                  whole(w_uq_p), whole(w_ukv_p), rows(cs.shape[1]), rows(sn.shape[1])],
        out_specs=(rows(bw), rows(pw), whole(gq), whole(gkv)),
        out_shape=(jax.ShapeDtypeStruct((t, bw), BF16), jax.ShapeDtypeStruct((t, pw), BF16),
                   jax.ShapeDtypeStruct(gq.shape, F32), jax.ShapeDtypeStruct(gkv.shape, F32)),
        compiler_params=_params("arbitrary", vmem=VMEM_SMALL),
    )(dq_nope, dq_pe, dkv_all, dk_pe, d_tail, proj_b, gq, gkv, w_uq_p, w_ukv_p, cs, sn)


def _lane_masks(pair, h, pe):
    lane = lax.broadcasted_iota(jnp.int32, (1, LANES), 1)
    in_head = lax.shift_right_logical(lane, _log2(HEAD_DIM)) == h
    in_rope = ((lax.shift_right_logical(lane, _log2(MLA_ROPE // 2)) & 3) == ((2 * pair + h) & 3)) if pe else None
    return in_head, in_rope


def _keep(mask, v):
    return jnp.where(mask, v, jnp.zeros_like(v))


def _to_row(col):
    n = col.shape[0]
    eye = lax.broadcasted_iota(jnp.int32, (n, n), 0) == lax.broadcasted_iota(jnp.int32, (n, n), 1)
    return jnp.sum(jnp.where(eye, col, 0.0), axis=0, keepdims=True)


def _to_col(row):
    n = row.shape[1]
    eye = lax.broadcasted_iota(jnp.int32, (n, n), 0) == lax.broadcasted_iota(jnp.int32, (n, n), 1)
    return jnp.sum(jnp.where(eye, row, 0.0), axis=1, keepdims=True)


def _first_step():
    return (pl.program_id(0) == 0) & (pl.program_id(1) == 0)


def _last_step(n0, n1):
    return (pl.program_id(0) == n0 - 1) & (pl.program_id(1) == n1 - 1)


def _attn_fwd(ops, bias, scale, bl, s, tq, name, traffic=None):
    pe = len(ops) == 3
    has_bias = bias is not None
    exact_scale = math.frexp(scale)[0] == 0.5
    span = 4 * tq
    nq = s // tq
    t = bl * s
    n_carried = len(traffic.pieces) if traffic else 0

    def body(*refs):
        sems = refs[len(refs) - 3:] if traffic else ()
        if pe:
            q_ref, qpe_ref, kv_ref, kpe_ref = refs[:4]
            n_in = 4
            q_at = lambda r0, r1: q_ref[r0:r1, :]
            v_at = lambda r0, r1: kv_ref[r0:r1, LANES:]
            kcat = refs[len(refs) - 1 - len(sems)]
            kcat[:, :LANES] = kv_ref[:, :LANES]
            kcat[:, LANES:] = kpe_ref[...]
            k_at = lambda r0, r1: kcat[r0:r1, :]
        else:
            qkv_ref = refs[0]
            n_in = 1
            q_at = lambda r0, r1: qkv_ref[r0:r1, :LANES]
            k_at = lambda r0, r1: qkv_ref[r0:r1, LANES:2 * LANES]
            v_at = lambda r0, r1: qkv_ref[r0:r1, 2 * LANES:]
        if has_bias:
            c_ref = refs[n_in]
            n_in += 1
        carried_in = refs[n_in:n_in + n_carried]
        n_in += n_carried
        o_ref, lse_ref = refs[n_in:n_in + 2]
        if traffic:
            carried_out = refs[n_in + 2]

            @pl.when(_first_step())
            def _():
                traffic.start(carried_in, carried_out, *sems)

        pair = pl.program_id(1)
        causal = lax.broadcasted_iota(jnp.int32, (tq, tq), 1) <= lax.broadcasted_iota(jnp.int32, (tq, tq), 0)
        o_ref[...] = jnp.zeros_like(o_ref)

        masks = [_lane_masks(pair, h, pe) for h in range(2)]

        def logits(i):
            r0, r1 = i * tq, (i + 1) * tq
            out = []
            for h in range(2):
                in_head, in_rope = masks[h]
                qm = _keep(in_head, q_at(r0, r1))
                if pe:
                    qm = jnp.concatenate([qm, _keep(in_rope, qpe_ref[r0:r1, :])], axis=1)
                if exact_scale:
                    qm = qm * scale
                spans = []
                for k0, k1 in [(r0, r1)] + [(k, min(k + span, r0)) for k in range(0, r0, span)]:
                    sc = _dot(qm, k_at(k0, k1), "nt")
                    if not exact_scale:
                        sc = sc * scale
                    if has_bias:
                        sc = sc - c_ref[h, :, k0:k1]
                    spans.append((k0, k1, jnp.where(causal, sc, MASKED) if k0 == r0 else sc))
                out.append(spans)
            return out

        def softmax(per_head):
            out = []
            for spans in per_head:
                m = None
                for _, _, sc in spans:
                    top = jnp.max(sc, axis=1, keepdims=True)
                    m = top if m is None else jnp.maximum(m, top)
                probs = [(k0, k1, jnp.exp(sc - m)) for k0, k1, sc in spans]
                l = sum(jnp.sum(p, axis=1, keepdims=True) for _, _, p in probs)
                out.append((m, l, probs))
            return out

        def weigh(i, per_head):
            r0, r1 = i * tq, (i + 1) * tq
            for h, (m, l, probs) in enumerate(per_head):
                acc = sum(_dot(p, v_at(k0, k1), "nn") for k0, k1, p in probs)
                o_ref[r0:r1, :] = jnp.where(masks[h][0], acc / l, o_ref[r0:r1, :])
                lse = _to_row(m + jnp.log(l))
                lse_ref[h, :, r0:r1] = lse + c_ref[h, :, r0:r1] if has_bias else lse

        ahead = logits(0)
        for i in range(nq):
            solved = softmax(ahead)
            if i + 1 < nq:
                ahead = logits(i + 1)
            weigh(i, solved)

        if traffic:
            @pl.when(_last_step(bl, PAIRS))
            def _():
                traffic.wait(carried_out, *sems)

    seq = lambda w, col: pl.BlockSpec((s, w), col)
    if pe:
        in_specs = [seq(LANES, lambda b, p: (b, p)), seq(LANES, lambda b, p: (b, PAIRS + p // 2)),
                    seq(2 * LANES, lambda b, p: (b, p)), seq(LANES, lambda b, p: (b, 0))]
        args = [ops[0], ops[0], ops[1], ops[2]]
        scratch = [pltpu.VMEM((s, 2 * LANES), BF16)]
    else:
        in_specs = [seq(3 * LANES, lambda b, p: (b, p))]
        args = [ops[0]]
        scratch = []
    per_head_row = pl.BlockSpec((2, 1, s), lambda b, p: (b * PAIRS + p, 0, 0))
    if has_bias:
        in_specs.append(per_head_row)
        args.append(bias)
    out_specs = [seq(LANES, lambda b, p: (b, p)), per_head_row]
    out_shape = [jax.ShapeDtypeStruct((t, HEADS * HEAD_DIM), F32), jax.ShapeDtypeStruct((bl * HEADS, 1, s), F32)]
    if traffic:
        in_specs += traffic.in_specs
        args += traffic.pieces
        out_specs.append(traffic.out_spec)
        out_shape.append(traffic.out_shape)
        scratch += traffic.scratch
    return pl.pallas_call(
        body, name=name, grid=(bl, PAIRS), in_specs=in_specs, out_specs=tuple(out_specs), out_shape=tuple(out_shape),
        scratch_shapes=scratch, compiler_params=_params(*(("arbitrary", "arbitrary") if traffic else ("parallel", "parallel"))),
    )(*args)


def _attn_bwd(ops, do, lse, delta, bias, scale, bl, s, tq, name, traffic=None):
    pe = len(ops) == 3
    has_bias = bias is not None
    exact_scale = math.frexp(scale)[0] == 0.5
    span = 2 * tq
    nq = s // tq
    t = bl * s
    width = 2 * LANES if pe else LANES
    n_carried = len(traffic.pieces) if traffic else 0

    def body(*refs):
        if pe:
            q_ref, qpe_ref, kv_ref, kpe_ref = refs[:4]
            n_in = 4
            k_at = lambda r0, r1: kv_ref[r0:r1, :LANES]
            v_at = lambda r0, r1: kv_ref[r0:r1, LANES:]
        else:
            qkv_ref = refs[0]
            n_in = 1
            k_at = lambda r0, r1: qkv_ref[r0:r1, LANES:2 * LANES]
            v_at = lambda r0, r1: qkv_ref[r0:r1, 2 * LANES:]
        do_ref, lse_ref, dl_ref = refs[n_in:n_in + 3]
        n_in += 3
        if has_bias:
            c_ref = refs[n_in]
            n_in += 1
        carried_in = refs[n_in:n_in + n_carried]
        rest = refs[n_in + n_carried:]
        if traffic:
            rest, sems = rest[:-3], rest[-3:]
            carried_out = rest[4 if pe else 2]
            rest = rest[:4 if pe else 2] + rest[(4 if pe else 2) + 1:]

            @pl.when(_first_step())
            def _():
                traffic.start(carried_in, carried_out, *sems)

        if pe:
            dqn_ref, dkv_ref, dqpe_ref, dkpe_ref, dq_acc, qcat = rest
            qcat[:, :LANES] = q_ref[...]
            qcat[:, LANES:] = qpe_ref[...]
            q_at = lambda r0, r1: qcat[r0:r1, :]
            dkv_ref[...] = jnp.zeros_like(dkv_ref)
        else:
            dqkv_ref, dc_ref, dq_acc = rest
            q_at = lambda r0, r1: qkv_ref[r0:r1, :LANES]
            dqkv_ref[...] = jnp.zeros_like(dqkv_ref)
            dc_ref[...] = jnp.zeros_like(dc_ref)
        pair = pl.program_id(1)
        dq_acc[...] = jnp.zeros_like(dq_acc)
        causal = lax.broadcasted_iota(jnp.int32, (tq, tq), 1) >= lax.broadcasted_iota(jnp.int32, (tq, tq), 0)
        if pe:
            @pl.when(pair == 0)
            def _():
                dkpe_ref[...] = jnp.zeros_like(dkpe_ref)

            @pl.when(pair % 2 == 0)
            def _():
                dqpe_ref[...] = jnp.zeros_like(dqpe_ref)

        masks = [_lane_masks(pair, h, pe) for h in range(2)]

        def logits(j):
            r0, r1 = j * tq, (j + 1) * tq
            units = []
            for h in range(2):
                in_head, in_rope = masks[h]
                kt = _keep(in_head, k_at(r0, r1))
                if pe:
                    kt = jnp.concatenate([kt, _keep(in_rope, kpe_ref[r0:r1, :])], axis=1)
                if exact_scale:
                    kt = kt * scale
                vt = _keep(in_head, v_at(r0, r1))
                ck = _to_col(c_ref[h, :, r0:r1]) if has_bias else None
                for q0, q1, diagonal in [(r0, r1, True)] + [(q, min(q + span, s), False) for q in range(r1, s, span)]:
                    qq, dd = q_at(q0, q1), do_ref[q0:q1, :]
                    st = _dot(kt, qq, "nt")
                    if not exact_scale:
                        st = st * scale
                    shift = lse_ref[h, :, q0:q1]
                    if has_bias:
                        shift = shift - c_ref[h, :, q0:q1]
                        st = st - ck
                    st = st - shift
                    if diagonal:
                        st = jnp.where(causal, st, MASKED)
                    units.append((h, q0, q1, kt, qq, dd, st, _dot(vt, dd, "nt")))
            return units

        def softmax_bwd(units):
            solved = []
            for h, q0, q1, kt, qq, dd, st, dpt in units:
                pt = jnp.exp(st)
                dst = pt * (dpt - dl_ref[h, :, q0:q1])
                solved.append((h, q0, q1, kt, qq, dd, pt, dst, (dst if exact_scale else dst * scale).astype(BF16)))
            return solved

        def products(j, solved):
            r0, r1 = j * tq, (j + 1) * tq
            dv_of, dk_of, cs_of = [None, None], [None, None], [None, None]
            add = lambda old, new: new if old is None else old + new
            for h, q0, q1, kt, qq, dd, pt, dst, dsb in solved:
                dq_acc[q0:q1, :] += _dot(dsb, kt, "tn")
                dv_of[h] = add(dv_of[h], _dot(pt, dd, "nn"))
                dk_of[h] = add(dk_of[h], _dot(dsb, qq, "nn"))
                if has_bias:
                    dc_ref[h, :, q0:q1] += jnp.sum(dst, axis=0, keepdims=True)
                    cs_of[h] = add(cs_of[h], jnp.sum(dst, axis=1, keepdims=True))
            for h in range(2):
                (in_head, in_rope), dv_c, dk_c, cs = masks[h], dv_of[h], dk_of[h], cs_of[h]
                if exact_scale:
                    dk_c = dk_c * scale
                if pe:
                    dkv_ref[r0:r1, :LANES] = jnp.where(in_head, dk_c[:, :LANES].astype(BF16), dkv_ref[r0:r1, :LANES])
                    dkv_ref[r0:r1, LANES:] = jnp.where(in_head, dv_c.astype(BF16), dkv_ref[r0:r1, LANES:])
                    dkpe_ref[r0:r1, :] += _keep(in_rope, dk_c[:, LANES:])
                else:
                    dqkv_ref[r0:r1, LANES:2 * LANES] = jnp.where(in_head, dk_c.astype(BF16), dqkv_ref[r0:r1, LANES:2 * LANES])
                    dqkv_ref[r0:r1, 2 * LANES:] = jnp.where(in_head, dv_c.astype(BF16), dqkv_ref[r0:r1, 2 * LANES:])
                    dc_ref[h, :, r0:r1] -= _to_row(cs)

        units = logits(0)
        for j in range(nq):
            solved = softmax_bwd(units)
            if j + 1 < nq:
                units = logits(j + 1)
            products(j, solved)

        if pe:
            dqn_ref[...] = dq_acc[:, :LANES].astype(BF16)
            dqpe_ref[...] += dq_acc[:, LANES:]
        else:
            dqkv_ref[:, :LANES] = dq_acc[...].astype(BF16)
        if traffic:
            @pl.when(_last_step(bl, PAIRS))
            def _():
                traffic.wait(carried_out, *sems)

    seq = lambda w, col: pl.BlockSpec((s, w), col)
    per_head_row = pl.BlockSpec((2, 1, s), lambda b, p: (b * PAIRS + p, 0, 0))
    if pe:
        in_specs = [seq(LANES, lambda b, p: (b, p)), seq(LANES, lambda b, p: (b, PAIRS + p // 2)),
                    seq(2 * LANES, lambda b, p: (b, p)), seq(LANES, lambda b, p: (b, 0))]
        args = [ops[0], ops[0], ops[1], ops[2]]
    else:
        in_specs = [seq(3 * LANES, lambda b, p: (b, p))]
        args = [ops[0]]
    in_specs += [seq(LANES, lambda b, p: (b, p)), per_head_row, per_head_row]
    args += [do, lse, delta]
    if has_bias:
        in_specs.append(per_head_row)
        args.append(bias)
    scratch = [pltpu.VMEM((s, width), F32)]
    if pe:
        out_specs = (seq(LANES, lambda b, p: (b, p)), seq(2 * LANES, lambda b, p: (b, p)),
                     seq(LANES, lambda b, p: (b, p // 2)), seq(LANES, lambda b, p: (b, 0)))
        out_shape = (jax.ShapeDtypeStruct((t, PAIRS * LANES), BF16), jax.ShapeDtypeStruct((t, PAIRS * 2 * LANES), BF16),
                     jax.ShapeDtypeStruct((t, 2 * LANES), F32), jax.ShapeDtypeStruct((t, LANES), F32))
        scratch.append(pltpu.VMEM((s, 2 * LANES), BF16))
    else:
        out_specs = (seq(3 * LANES, lambda b, p: (b, p)), per_head_row)
        out_shape = (jax.ShapeDtypeStruct((t, PAIRS * 3 * LANES), BF16), jax.ShapeDtypeStruct((bl * HEADS, 1, s), F32))
    if traffic:
        in_specs += traffic.in_specs
        args += traffic.pieces
        out_specs += (traffic.out_spec,)
        out_shape += (traffic.out_shape,)
        scratch += traffic.scratch
    return pl.pallas_call(
        body, name=name, grid=(bl, PAIRS), in_specs=in_specs, out_specs=out_specs, out_shape=out_shape,
        scratch_shapes=scratch, compiler_params=_params("arbitrary" if traffic else "parallel", "arbitrary"),
    )(*args)


def _my_place():
    return lax.axis_index("x"), lax.axis_index("y"), lax.axis_index("c")


def _flip(p, bit):
    return 1 - p if bit else p


def _relative(x, y, c, k):
    return _flip(x, k & 4), _flip(y, k & 2), _flip(c, k & 1)


def _linear(x, y, c):
    return 4 * x + 2 * y + c


class _Traffic:
    def __init__(self, kind, pieces):
        self.kind, self.pieces = kind, list(pieces)
        self.rows = [p.shape[-2] for p in self.pieces]
        self.starts = [sum(self.rows[:i]) for i in range(len(self.rows))]
        anywhere = pl.BlockSpec(memory_space=pl.ANY)
        self.in_specs = [anywhere] * len(self.pieces)
        self.out_spec = anywhere
        self.out_shape = jax.ShapeDtypeStruct((N_DEV, sum(self.rows), self.pieces[0].shape[-1]), self.pieces[0].dtype)
        self.scratch = [pltpu.SemaphoreType.DMA((7,)), pltpu.SemaphoreType.DMA((7,)), pltpu.SemaphoreType.DMA(())]

    def start(self, p_refs, out_ref, send_sems, recv_sems, local_sem):
        x, y, c = _my_place()
        me = _linear(x, y, c)
        mine = lambda i, dev: p_refs[i] if self.kind == "spread" else p_refs[i].at[dev]
        landing = lambda i: out_ref.at[me, pl.ds(self.starts[i], self.rows[i])]
        for i in range(len(p_refs)):
            pltpu.make_async_copy(mine(i, me), landing(i), local_sem).start()
        for k in range(1, N_DEV):
            peer = _relative(x, y, c, k)
            for i in range(len(p_refs)):
                pltpu.make_async_remote_copy(
                    src_ref=mine(i, _linear(*peer)), dst_ref=landing(i),
                    send_sem=send_sems.at[k - 1], recv_sem=recv_sems.at[k - 1], device_id=peer, device_id_type=MESH).start()

    def wait(self, out_ref, send_sems, recv_sems, local_sem):
        x, y, c = _my_place()
        whole = out_ref.at[_linear(x, y, c)]
        for k in range(1, N_DEV):
            both = pltpu.make_async_remote_copy(
                src_ref=whole, dst_ref=whole, send_sem=send_sems.at[k - 1], recv_sem=recv_sems.at[k - 1],
                device_id=_relative(x, y, c, k), device_id_type=MESH)
            both.wait_recv()
            both.wait_send()
        pltpu.make_async_copy(whole, whole, local_sem).wait()


class _Relay(_Traffic):
    def __init__(self, piece):
        super().__init__("spread", [piece])

    @staticmethod
    def _chips(x, y):
        return [(1 - x, y), (x, 1 - y), (1 - x, 1 - y)]

    @staticmethod
    def _copy(k, block, to, out_ref, send_sems, recv_sems, src=None):
        slot = out_ref.at[_linear(*block)]
        return pltpu.make_async_remote_copy(src_ref=slot if src is None else src, dst_ref=slot, send_sem=send_sems.at[k],
                                            recv_sem=recv_sems.at[k], device_id=to, device_id_type=MESH)

    def start(self, p_refs, out_ref, send_sems, recv_sems, local_sem):
        x, y, c = _my_place()
        me, sems = (x, y, c), (out_ref, send_sems, recv_sems)
        pltpu.make_async_copy(p_refs[0], out_ref.at[_linear(*me)], local_sem).start()
        self._copy(0, me, (x, y, 1 - c), *sems, src=p_refs[0]).start()
        for j, chip in enumerate(self._chips(x, y)):
            self._copy(1 + j, me, (*chip, c), *sems, src=p_refs[0]).start()

    def relay(self, out_ref, send_sems, recv_sems, local_sem):
        x, y, c = _my_place()
        sems = (out_ref, send_sems, recv_sems)
        for j, chip in enumerate(self._chips(x, y)):
            self._copy(1 + j, (*chip, c), (x, y, c), *sems).wait_recv()
            self._copy(4 + j, (*chip, c), (x, y, 1 - c), *sems).start()

    def wait(self, out_ref, send_sems, recv_sems, local_sem):
        x, y, c = _my_place()
        me, sems = (x, y, c), (out_ref, send_sems, recv_sems)
        self._copy(0, (x, y, 1 - c), me, *sems).wait_recv()
        for j, chip in enumerate(self._chips(x, y)):
            self._copy(4 + j, (*chip, 1 - c), me, *sems).wait_recv()
        for k in range(N_DEV - 1):
            self._copy(k, me, (x, y, 1 - c), *sems).wait_send()
        whole = out_ref.at[_linear(*me)]
        pltpu.make_async_copy(whole, whole, local_sem).wait()


def _sum_blocks(parts, name):
    n, r, cdim = parts.shape
    tr = _tile(r, 640, 16)

    def body(p_ref, o_ref):
        acc = p_ref[0].astype(F32)
        for d in range(1, n):
            acc = acc + p_ref[d].astype(F32)
        o_ref[...] = acc

    return pl.pallas_call(
        body, name=name, grid=(r // tr,), in_specs=[pl.BlockSpec((n, tr, cdim), lambda i: (0, i, 0))],
        out_specs=pl.BlockSpec((tr, cdim), lambda i: (i, 0)), out_shape=jax.ShapeDtypeStruct((r, cdim), F32),
        compiler_params=_params("parallel"),
    )(parts)


def _adamw_math(w, g, m, v):
    m = ADAM_B1 * m + (1.0 - ADAM_B1) * g
    v = ADAM_B2 * v + (1.0 - ADAM_B2) * (g * g)
    m_hat = m / (1.0 - ADAM_B1 ** ADAM_STEP)
    v_hat = v / (1.0 - ADAM_B2 ** ADAM_STEP)
    delta = -ADAM_LR * (m_hat / (jnp.sqrt(v_hat) + ADAM_EPS) + ADAM_WD * w)
    return delta, m, v


def _adamw(w, g, m, v, name):
    def body(w_ref, g_ref, m_ref, v_ref, d_ref, nm_ref, nv_ref):
        d_ref[...], nm_ref[...], nv_ref[...] = _adamw_math(w_ref[...], g_ref[...], m_ref[...], v_ref[...])

    out = jax.ShapeDtypeStruct(w.shape, F32)
    return pl.pallas_call(body, name=name, out_shape=(out, out, out),
                          compiler_params=pltpu.CompilerParams(vmem_limit_bytes=VMEM_SMALL))(w, g, m, v)


def _small_all_reduce_adamw(parts, loss_part, ws, ms, vs, name):
    sizes = [p.shape[1] for p in parts] + [1]
    spots = [sum(-(-n // LANES) * LANES for n in sizes[:i]) for i in range(len(sizes))]
    width = spots[-1] + LANES
    k = len(parts)

    def reduce_body(*refs):
        p_refs, tot_ref, rows, send_sems, recv_sems = refs[:k + 1], *refs[k + 1:]
        x, y, c = _my_place()
        me = _linear(x, y, c)
        rows[me] = jnp.zeros((1, width), F32)
        for i in range(k + 1):
            rows[me, :, spots[i]:spots[i] + sizes[i]] = p_refs[i][...]
        copies = []
        for rel in range(1, N_DEV):
            copies.append(pltpu.make_async_remote_copy(
                src_ref=rows.at[me], dst_ref=rows.at[me], send_sem=send_sems.at[rel - 1], recv_sem=recv_sems.at[rel - 1],
                device_id=_relative(x, y, c, rel), device_id_type=MESH))
        for cp in copies:
            cp.start()
        for cp in copies:
            cp.wait_recv()
        for cp in copies:
            cp.wait_send()
        total = rows[0]
        for d in range(1, N_DEV):
            total = total + rows[d]
        tot_ref[...] = total

    total = pl.pallas_call(
        reduce_body, name=name, out_shape=jax.ShapeDtypeStruct((1, width), F32),
        scratch_shapes=[pltpu.VMEM((N_DEV, 1, width), F32), pltpu.SemaphoreType.DMA((7,)), pltpu.SemaphoreType.DMA((7,))],
    )(*parts, loss_part)

    def adamw_body(*refs):
        tot_ref, w_refs, m_refs, v_refs, outs = refs[0], refs[1:k + 1], refs[k + 1:2 * k + 1], refs[2 * k + 1:3 * k + 1], refs[3 * k + 1:]
        for i in range(k):
            g = tot_ref[:, spots[i]:spots[i] + sizes[i]]
            outs[4 * i][...] = g
            outs[4 * i + 1][...], outs[4 * i + 2][...], outs[4 * i + 3][...] = _adamw_math(w_refs[i][...], g, m_refs[i][...], v_refs[i][...])
        outs[4 * k][...] = tot_ref[:, spots[k]:spots[k] + 1]

    out_shape = [jax.ShapeDtypeStruct((1, n), F32) for n in sizes[:k] for _ in range(4)] + [jax.ShapeDtypeStruct((1, 1), F32)]
    res = pl.pallas_call(adamw_body, name=name + "_adamw", out_shape=tuple(out_shape))(total, *ws, *ms, *vs)
    return [res[4 * i:4 * i + 4] for i in range(k)], res[4 * k]


def _pad_rows(a, rows):
    return jnp.pad(a, ((0, rows - a.shape[0]), (0, 0)))


def kernel(x, positions, norm_mix_g, w_in, b_fgate, q_norm_g, w_uq, kv_norm_g, w_ukv, fox_out_g, mla_out_g, w_o, norm_ffn_g, w_gate, w_up, w_down, final_norm_g, loss_target, m_norm_mix_g, m_w_in, m_b_fgate, m_q_norm_g, m_w_uq, m_kv_norm_g, m_w_ukv, m_fox_out_g, m_mla_out_g, m_w_o, m_norm_ffn_g, m_w_gate, m_w_up, m_w_down, m_final_norm_g, v_norm_mix_g, v_w_in, v_b_fgate, v_q_norm_g, v_w_uq, v_kv_norm_g, v_w_ukv, v_fox_out_g, v_mla_out_g, v_w_o, v_norm_ffn_g, v_w_gate, v_w_up, v_w_down, v_final_norm_g):
    bl, s, d = x.shape
    t = bl * s
    bh = bl * HEADS
    tq = _tile(s, 256)
    grp = s // LANES
    fw = HEADS * HEAD_DIM
    q_rank, kv_rank = w_uq.shape[1], w_ukv.shape[1]
    in_cols = w_in.shape[2]
    n_in = N_DEV * in_cols
    ff = N_DEV * w_gate.shape[2]
    half = MLA_ROPE // 2
    o_kvlat, o_krope, o_flogit = q_rank, q_rank + kv_rank, q_rank + kv_rank + LANES
    b_cols = -(-(o_flogit + HEADS) // LANES) * LANES

    tr = lambda w: jnp.transpose(w[0])
    in_rows = -(-in_cols // 16) * 16
    uq_rows = w_uq.shape[2] * q_rank // d
    ukv_rows = w_ukv.shape[2] * kv_rank // d
    pieces = [_pad_rows(tr(w_in), in_rows), _pad_rows(tr(w_uq).reshape(uq_rows, d), -(-uq_rows // 16) * 16),
              tr(w_ukv).reshape(ukv_rows, d), w_o[0], tr(w_gate), tr(w_up), w_down[0]]
    pieces = [p.astype(BF16) for p in pieces]
    offs = [0]
    for p in pieces:
        offs.append(offs[-1] + p.shape[0])
    legs = [(0, 1), (1, 5), (5, 7)]
    gathered = {}

    def full(i, rows):
        leg = next(n for n, (lo, hi) in enumerate(legs) if lo <= i < hi)
        base = offs[legs[leg][0]]
        return gathered[leg][:, offs[i] - base:offs[i] - base + rows]

    x2d = x.reshape(t, d)
    h1, gathered[0] = _rmsnorm(x2d, 0, d, norm_mix_g, BF16, "norm_mix", traffic=_Relay(pieces[0]))

    w_in_t = full(0, in_cols).reshape(n_in, d)
    n_qkv = 3 * fw
    w_in_a = w_in_t[:n_qkv].reshape(3, PAIRS, LANES, d).transpose(1, 0, 2, 3).reshape(n_qkv, d)
    lat0, rope0 = n_qkv + HEADS, n_qkv + HEADS + q_rank + kv_rank
    k_rep = jnp.broadcast_to(w_in_t[rope0:].reshape(2, 1, half, d), (2, 4, half, d)).reshape(LANES, d)
    w_in_b = jnp.concatenate([w_in_t[lat0:rope0], k_rep, w_in_t[n_qkv:lat0],
                              jnp.zeros((b_cols - o_flogit - HEADS, d), BF16)], axis=0)

    def per_head_rows(a):
        return a.reshape(bl, s, HEADS).transpose(0, 2, 1).reshape(bh, 1, s)

    proj_a = _matmul(h1, w_in_a, "nt", BF16, "proj_fox", tm=1024, tn=6 * LANES)
    proj_b = _matmul(h1, w_in_b, "nt", F32, "proj_mla", tm=1024, tn=b_cols)

    z = proj_b[:, o_flogit:o_flogit + HEADS].reshape(bl, s, HEADS).transpose(0, 2, 1).reshape(bh * grp, LANES)
    bcol = jnp.broadcast_to(b_fgate.reshape(1, HEADS, 1), (bl, HEADS, grp)).reshape(bh * grp, 1)
    c = _fgate(z, bcol, grp, "forget_gate")
    c_bias = c.reshape(bh, 1, s)
    fox_o, fox_lse, gathered[1] = _attn_fwd((proj_a,), c_bias, HEAD_DIM ** -0.5, bl, s, tq, "fox_attention",
                                            traffic=_Traffic("spread", pieces[legs[1][0]:legs[1][1]]))
    w_uq_h = full(1, uq_rows).reshape(HEADS, MLA_QK, q_rank)
    w_uq_pe = jnp.concatenate([w_uq_h[:, HEAD_DIM:HEAD_DIM + half].reshape(2, 1, 4 * half, q_rank),
                               w_uq_h[:, HEAD_DIM + half:].reshape(2, 1, 4 * half, q_rank)], axis=1).reshape(2 * LANES, q_rank)
    w_uq_p = jnp.concatenate([w_uq_h[:, :HEAD_DIM].reshape(fw, q_rank), w_uq_pe], axis=0)
    w_ukv_p = full(2, ukv_rows).reshape(PAIRS, 2, 2, HEAD_DIM, kv_rank).transpose(0, 2, 1, 3, 4).reshape(2 * fw, kv_rank)
    w_o_f = full(3, w_o.shape[1]).reshape(-1, d)
    w_gate_t = full(4, ff // N_DEV).reshape(ff, d)

    inv_freq = ROPE_THETA ** (-jnp.arange(0, MLA_ROPE, 2, dtype=F32) / MLA_ROPE)
    ang = positions.astype(F32).reshape(t, 1) * inv_freq[None, :]
    rope_cos, rope_sin = jnp.cos(ang), jnp.sin(ang)
    qn, kvn, q_all, kv_all, kpe = _mla_prep(proj_b, q_rank, kv_rank, q_norm_g, kv_norm_g, w_uq_p, w_ukv_p, fw,
                                            rope_cos, rope_sin, "mla_prep")
    mla_ops = (q_all, kv_all, kpe)
    mla_o, mla_lse, gathered[2] = _attn_fwd(mla_ops, None, MLA_QK ** -0.5, bl, s, tq, "mla_attention",
                                            traffic=_Traffic("spread", pieces[legs[2][0]:legs[2][1]]))
    w_up_t, w_down_f = full(5, ff // N_DEV).reshape(ff, d), full(6, ff // N_DEV).reshape(ff, d)

    both = [(d, F32), (d, BF16)]
    cat, x1, h2 = _rows_matmul([(None, w_o_f, "nn")], [fox_o, mla_o, x2d], [fox_out_g, mla_out_g, norm_ffn_g], _residual_norm,
                               [(2 * fw, BF16)] + both, [], "norm_out_proj_out_norm_ffn", prologue=_out_norm)
    act_by_gate, act_by_up, act = _ffn_up(h2, w_gate_t, w_up_t, "ffn_gate_up")
    dx2, dx2_b, dg_final, loss_part = _rows_matmul(
        [(act, w_down_f, "nn")], [x1, loss_target.reshape(t, d)], [final_norm_g.reshape(1, d)], _residual_loss_bwd,
        both, [d, 1], "ffn_down_final_norm_loss")

    d_gate, d_up = _ffn_down_bwd(dx2_b, w_down_f, act_by_gate, act_by_up, "d_ffn_down")
    dw_down = _matmul(act, dx2_b, "tn", BF16, "dw_down", tm=ff // 2, tn=d, tk=2048)
    dw_gate = _matmul(d_gate, h2, "tn", BF16, "dw_gate", tm=ff // 2, tn=d, tk=2048)
    dw_up = _matmul(d_up, h2, "tn", BF16, "dw_up", tm=ff // 2, tn=d, tk=2048)
    dx1, dx1_b, dg_ffn = _rows_matmul([(d_gate, w_gate_t, "nn"), (d_up, w_up_t, "nn")], [x1, dx2], [norm_ffn_g],
                                      _norm_bwd_residual, both, [d], "d_ffn_gate_up_norm_ffn", tm=256)
    dw_o = _matmul(cat, dx1_b, "tn", BF16, "dw_o", tn=d, tk=2048)
    d_fox_o, d_mla_o, fox_delta, mla_delta, dg_fox, dg_mla = _rows_matmul(
        [(dx1_b, w_o_f, "nt")], [fox_o, mla_o], [fox_out_g, mla_out_g], _out_norm_bwd,
        [(fw, BF16), (fw, BF16), (HEADS, F32), (HEADS, F32)], [fw, fw], "d_proj_out_norm_out")

    per_dev = lambda a: a.reshape(N_DEV, -1, d)
    late_grads = [per_dev(dw_o), per_dev(dw_gate), per_dev(dw_up), per_dev(dw_down)]
    dproj_a, dc, g_late = _attn_bwd((proj_a,), d_fox_o, fox_lse, per_head_rows(fox_delta),
                                    c_bias, HEAD_DIM ** -0.5, bl, s, tq, "d_fox_attention", traffic=_Traffic("swap", late_grads))
    dz, db_fgate = _fgate_bwd(z, bcol, dc.reshape(bh * grp, LANES), grp, "d_forget_gate")
    d_flogit = dz.reshape(bl, HEADS, s).transpose(0, 2, 1).reshape(t, HEADS)

    dq_nope, dkv_all, dq_pe, dk_pe = _attn_bwd(mla_ops, d_mla_o, mla_lse, per_head_rows(mla_delta),
                                               None, MLA_QK ** -0.5, bl, s, tq, "d_mla_attention")
    d_tail = jnp.pad(d_flogit, ((0, 0), (0, b_cols - o_flogit - HEADS)))
    dproj_b, dq_rot, dg_q, dg_kv = _mla_prep_bwd(dq_nope, dq_pe, dkv_all, dk_pe, d_tail, proj_b, q_rank, kv_rank,
                                                 q_norm_g, kv_norm_g, w_uq_p, w_ukv_p, rope_cos, rope_sin, "d_mla_prep")
    dw_uq_nope = _matmul(dq_nope, qn, "tn", BF16, "dw_uq_nope", tn=q_rank, tk=1024)
    dw_uq_pe = _matmul(dq_rot, qn, "tn", BF16, "dw_uq_rope", tn=q_rank, tk=1024)
    dw_ukv_p = _matmul(dkv_all, kvn, "tn", BF16, "dw_ukv", tn=kv_rank, tk=1024)
    dw_in_a = _matmul(dproj_a, h1, "tn", BF16, "dw_in_fox", tm=6 * LANES, tn=d, tk=2048)
    dw_in_b = _matmul(dproj_b, h1, "tn", F32, "dw_in_mla", tm=b_cols, tn=d, tk=1024)

    dw_krope = dw_in_b[o_krope:o_flogit].reshape(2, 4, half, d).sum(axis=1).reshape(MLA_ROPE, d)
    dw_in_t = jnp.concatenate([dw_in_a.reshape(PAIRS, 3, LANES, d).transpose(1, 0, 2, 3).reshape(n_qkv, d),
                               dw_in_b[o_flogit:o_flogit + HEADS].astype(BF16), dw_in_b[:o_krope].astype(BF16),
                               dw_krope.astype(BF16)], axis=0)
    pad_dev = lambda a, rows: jnp.pad(a, ((0, 0), (0, rows - a.shape[1]), (0, 0)))
    dw_uq_pe5 = dw_uq_pe.reshape(2, 2, 4, half, q_rank)
    dw_uq_h = jnp.concatenate([dw_uq_nope.reshape(HEADS, HEAD_DIM, q_rank), dw_uq_pe5[:, 0].reshape(HEADS, half, q_rank),
                               dw_uq_pe5[:, 1].reshape(HEADS, half, q_rank)], axis=1)
    dw_ukv_h = dw_ukv_p.reshape(PAIRS, 2, 2, HEAD_DIM, kv_rank).transpose(0, 2, 1, 3, 4).reshape(HEADS, 2 * HEAD_DIM, kv_rank)
    n_last = 3
    last_grads = [pad_dev(per_dev(dw_in_t), pieces[0].shape[0]), pad_dev(per_dev(dw_uq_h), pieces[1].shape[0]), per_dev(dw_ukv_h)]
    grad_x, dg_mix, g_last = _rows_matmul([(dproj_a, w_in_a, "nn"), (dproj_b, w_in_b, "nn")], [x2d, dx1], [norm_mix_g],
                                          _norm_bwd_residual, [(d, F32)], [d], "d_proj_in_norm_mix",
                                          traffic=_Traffic("swap", last_grads))
    g_last = _sum_blocks(g_last, "sum_last_grads")
    g_late = _sum_blocks(g_late, "sum_late_grads")

    def mine(i, rows):
        src, base = (g_last, 0) if i < n_last else (g_late, offs[n_last])
        return src[offs[i] - base:offs[i] - base + rows]

    big = [
        ("w_in", w_in, m_w_in, v_w_in, mine(0, in_cols), True),
        ("w_uq", w_uq, m_w_uq, v_w_uq, mine(1, uq_rows).reshape(-1, q_rank), True),
        ("w_ukv", w_ukv, m_w_ukv, v_w_ukv, mine(2, ukv_rows).reshape(-1, kv_rank), True),
        ("w_o", w_o, m_w_o, v_w_o, mine(3, w_o.shape[1]), False),
        ("w_gate", w_gate, m_w_gate, v_w_gate, mine(4, ff // N_DEV), True),
        ("w_up", w_up, m_w_up, v_w_up, mine(5, ff // N_DEV), True),
        ("w_down", w_down, m_w_down, v_w_down, mine(6, ff // N_DEV), False),
    ]
    out = {}
    for nm, w, m, v, g, transposed in big:
        lay = (lambda a: a[0].T) if transposed else (lambda a: a[0])
        back = (lambda a: a.T[None]) if transposed else (lambda a: a[None])
        dl, new_m, new_v = _adamw(lay(w), g, lay(m), lay(v), "adamw_" + nm)
        out[nm] = (back(g), back(dl), back(new_m), back(new_v))

    smalls = [("norm_mix_g", norm_mix_g, m_norm_mix_g, v_norm_mix_g, dg_mix),
              ("b_fgate", b_fgate, m_b_fgate, v_b_fgate, db_fgate.reshape(1, HEADS)),
              ("q_norm_g", q_norm_g, m_q_norm_g, v_q_norm_g, dg_q),
              ("kv_norm_g", kv_norm_g, m_kv_norm_g, v_kv_norm_g, dg_kv),
              ("fox_out_g", fox_out_g, m_fox_out_g, v_fox_out_g, dg_fox),
              ("mla_out_g", mla_out_g, m_mla_out_g, v_mla_out_g, dg_mla),
              ("norm_ffn_g", norm_ffn_g, m_norm_ffn_g, v_norm_ffn_g, dg_ffn),
              ("final_norm_g", final_norm_g, m_final_norm_g, v_final_norm_g, dg_final)]
    flat = lambda a: a.reshape(1, -1)
    results, loss = _small_all_reduce_adamw([e[4] for e in smalls], loss_part, [flat(e[1]) for e in smalls],
                                            [flat(e[2]) for e in smalls], [flat(e[3]) for e in smalls], "reduce_small_adamw")
    for (nm, w, _, _, _), res in zip(smalls, results):
        out[nm] = tuple(a.reshape(w.shape) for a in res)
    loss = loss[0, 0]

    order = ["norm_mix_g", "w_in", "b_fgate", "q_norm_g", "w_uq", "kv_norm_g", "w_ukv", "fox_out_g", "mla_out_g", "w_o",
             "norm_ffn_g", "w_gate", "w_up", "w_down", "final_norm_g"]
    return (loss, grad_x.reshape(bl, s, d), *[out[n][0] for n in order], *[out[n][1] for n in order],
            *[out[n][2] for n in order], *[out[n][3] for n in order])
```

```python
import math

import jax
import jax.numpy as jnp
from jax import lax
from jax.experimental import pallas as pl
from jax.experimental.pallas import tpu as pltpu

F32 = jnp.float32
BF16 = jnp.bfloat16
MESH = pl.DeviceIdType.MESH

N_DEV = 8
HEADS = 8
HEAD_DIM = 64
PAIRS = HEADS // 2
MLA_ROPE = 32
MLA_QK = HEAD_DIM + MLA_ROPE
ROPE_THETA = 10000.0
NORM_EPS = 1e-6
ADAM_LR, ADAM_B1, ADAM_B2, ADAM_EPS, ADAM_WD, ADAM_STEP = 0.001, 0.9, 0.999, 1e-08, 0.01, 10

LANES = 128
MASKED = -1e30
VMEM_LIMIT = 48 * 1024 * 1024

_DIMS = {"nn": (((1,), (0,)), ((), ())), "nt": (((1,), (1,)), ((), ())), "tn": (((0,), (0,)), ((), ()))}


def _params(*sem):
    return pltpu.CompilerParams(dimension_semantics=sem, vmem_limit_bytes=VMEM_LIMIT)


def _dot(a, b, mode):
    return lax.dot_general(a.astype(BF16), b.astype(BF16), _DIMS[mode], preferred_element_type=F32)


def _tile(n, pref, unit=8):
    if n <= pref:
        return n
    t = pref - pref % unit
    while n % t:
        t -= unit
    return t


def _log2(n):
    assert n & (n - 1) == 0
    return n.bit_length() - 1


def _matmul(a, b, mode, out_dtype, name, tm=512, tn=512, tk=None):
    if mode == "nn":
        (m, kd), n = a.shape, b.shape[1]
    elif mode == "nt":
        (m, kd), n = a.shape, b.shape[0]
    else:
        (kd, m), n = a.shape, b.shape[1]
    tm, tn = _tile(m, tm, LANES if mode == "tn" else 16), _tile(n, tn, LANES)
    tk = kd if tk is None else _tile(kd, tk, LANES)
    nk = kd // tk
    a_spec = pl.BlockSpec((tk, tm), lambda i, j, k: (k, i)) if mode == "tn" else pl.BlockSpec((tm, tk), lambda i, j, k: (i, k))
    b_spec = pl.BlockSpec((tn, tk), lambda i, j, k: (j, k)) if mode == "nt" else pl.BlockSpec((tk, tn), lambda i, j, k: (k, j))
    o_spec = pl.BlockSpec((tm, tn), lambda i, j, k: (i, j))

    def body(a_ref, b_ref, o_ref, *acc):
        part = _dot(a_ref[...], b_ref[...], mode)
        if nk == 1:
            o_ref[...] = part.astype(out_dtype)
        else:
            acc_ref, k = acc[0], pl.program_id(2)

            @pl.when(k == 0)
            def _():
                acc_ref[...] = part

            @pl.when(k > 0)
            def _():
                acc_ref[...] += part

            @pl.when(k == nk - 1)
            def _():
                o_ref[...] = acc_ref[...].astype(out_dtype)

    return pl.pallas_call(
        body, name=name, grid=(m // tm, n // tn, nk), in_specs=[a_spec, b_spec], out_specs=o_spec,
        out_shape=jax.ShapeDtypeStruct((m, n), out_dtype),
        scratch_shapes=[pltpu.VMEM((tm, tn), F32)] if nk > 1 else [],
        compiler_params=_params("parallel", "parallel", "arbitrary"),
    )(a, b)


def _rstd(x):
    return lax.rsqrt(jnp.mean(x * x, axis=-1, keepdims=True) + NORM_EPS)


def _norm_bwd(x, g, dy):
    r = _rstd(x)
    xh = x * r
    u = dy * g
    dx = r * (u - xh * jnp.mean(u * xh, axis=-1, keepdims=True))
    return dx, jnp.sum(dy * xh, axis=0, keepdims=True)


def _rmsnorm(x, col, width, g, out_dtype, name, traffic=None):
    t = x.shape[0]
    tm = _tile(t, 512)
    steps = t // tm
    n_carried = len(traffic.pieces) if traffic else 0

    def body(*refs):
        x_ref, g_ref, o_ref = refs[0], refs[1], refs[2 + n_carried]
        if traffic:
            carried_in, carried_out, sems = refs[2:2 + n_carried], refs[3 + n_carried], refs[4 + n_carried:]

            @pl.when(pl.program_id(0) == 0)
            def _():
                traffic.start(carried_in, carried_out, *sems)

            if isinstance(traffic, _Relay):
                @pl.when(pl.program_id(0) == steps - 1)
                def _():
                    traffic.relay(carried_out, *sems)

        xv = x_ref[...]
        o_ref[...] = ((xv * _rstd(xv)) * g_ref[...]).astype(out_dtype)
        if traffic:
            @pl.when(pl.program_id(0) == steps - 1)
            def _():
                traffic.wait(carried_out, *sems)

    in_specs = [pl.BlockSpec((tm, width), lambda i: (i, col)), pl.BlockSpec((1, width), lambda i: (0, 0))]
    out_specs = [pl.BlockSpec((tm, width), lambda i: (i, 0))]
    out_shape = [jax.ShapeDtypeStruct((t, width), out_dtype)]
    if traffic:
        in_specs += traffic.in_specs
        out_specs.append(traffic.out_spec)
        out_shape.append(traffic.out_shape)
    out = pl.pallas_call(
        body, name=name, grid=(steps,), in_specs=in_specs, out_specs=tuple(out_specs), out_shape=tuple(out_shape),
        scratch_shapes=traffic.scratch if traffic else [],
        compiler_params=_params("arbitrary" if traffic else "parallel"),
    )(x, g, *(traffic.pieces if traffic else []))
    return out if traffic else out[0]


def _split3(x):
    hi = x.astype(BF16)
    r1 = x - hi.astype(F32)
    mid = r1.astype(BF16)
    lo = (r1 - mid.astype(F32)).astype(BF16)
    return hi, mid, lo


def _dot_x01(x, m01):
    hi, mid, lo = _split3(x)
    d = lambda p: lax.dot_general(p, m01, _DIMS["nn"], preferred_element_type=F32)
    return (d(lo) + d(mid)) + d(hi)


def _dot_01x(m01, x):
    hi, mid, lo = _split3(x)
    d = lambda p: lax.dot_general(m01, p, _DIMS["nn"], preferred_element_type=F32)
    return (d(lo) + d(mid)) + d(hi)


def _rows_matmul(terms, rows_in, vecs_in, epilogue, rows_out, sums_out, name, tm=512, prologue=None, traffic=None):
    t = rows_in[0].shape[0]
    tm = _tile(t, tm, 16)
    steps = t // tm
    n_rows, n_vecs = len(rows_in), len(vecs_in)
    n_ab = sum(1 + (a is not None) for a, _, _ in terms)
    n_carried = len(traffic.pieces) if traffic else 0
    halves = [slice(0, tm // 2), slice(tm // 2, tm)] if tm % 32 == 0 else [slice(0, tm)]

    def body(*refs):
        vecs = [r[...] for r in refs[n_ab + n_rows:n_ab + n_rows + n_vecs]]
        out_at = n_ab + n_rows + n_vecs + n_carried
        sum_refs = refs[out_at + len(rows_out):out_at + len(rows_out) + len(sums_out)]
        if traffic:
            carried_in, carried_out, sems = refs[out_at - n_carried:out_at], refs[len(refs) - 4], refs[len(refs) - 3:]

            @pl.when(pl.program_id(0) == 0)
            def _():
                traffic.start(carried_in, carried_out, *sems)

        @pl.when(pl.program_id(0) == 0)
        def _():
            for ref in sum_refs:
                ref[...] = jnp.zeros_like(ref)

        staged = []
        for rows_of in halves:
            row_blocks = [r[rows_of, :] for r in refs[n_ab:n_ab + n_rows]]
            made = prologue(row_blocks, vecs) if prologue else None
            acc, at = None, 0
            for a, _, mode in terms:
                lhs = made if a is None else refs[at][rows_of, :]
                at += a is not None
                part = _dot(lhs, refs[at][...], mode)
                at += 1
                acc = part if acc is None else acc + part
            staged.append((rows_of, row_blocks, made, acc))
        for rows_of, row_blocks, made, acc in staged:
            row_vals, sum_vals = epilogue(acc, row_blocks, vecs)
            if prologue:
                row_vals = [made] + row_vals
            for ref, val, (_, dtype) in zip(refs[out_at:], row_vals, rows_out):
                ref[rows_of, :] = val.astype(dtype)
            for ref, val in zip(sum_refs, sum_vals):
                ref[...] += val
        if traffic:
            @pl.when(pl.program_id(0) == steps - 1)
            def _():
                traffic.wait(carried_out, *sems)

    rows = lambda w: pl.BlockSpec((tm, w), lambda i: (i, 0))
    whole = lambda a: pl.BlockSpec(a.shape, lambda i: (0, 0), pipeline_mode=pl.Buffered(1))
    in_specs, args = [], []
    for a, b, _ in terms:
        in_specs += ([rows(a.shape[1])] if a is not None else []) + [whole(b)]
        args += ([a] if a is not None else []) + [b]
    in_specs += [rows(r.shape[1]) for r in rows_in] + [whole(v) for v in vecs_in]
    args += list(rows_in) + list(vecs_in)
    out_specs = [rows(w) for w, _ in rows_out] + [pl.BlockSpec((1, w), lambda i: (0, 0)) for w in sums_out]
    out_shape = [jax.ShapeDtypeStruct((t, w), dt) for w, dt in rows_out] + [jax.ShapeDtypeStruct((1, w), F32) for w in sums_out]
    if traffic:
        in_specs += traffic.in_specs
        args += traffic.pieces
        out_specs.append(traffic.out_spec)
        out_shape.append(traffic.out_shape)
    return pl.pallas_call(
        body, name=name, grid=(steps,), in_specs=in_specs, out_specs=tuple(out_specs), out_shape=tuple(out_shape),
        scratch_shapes=traffic.scratch if traffic else [], compiler_params=_params("arbitrary"),
    )(*args)


def _out_norm(rows, vecs):
    (f, m), (gf, gm) = rows[:2], vecs[:2]
    return jnp.concatenate([((f * _rstd(f)) * gf).astype(BF16), ((m * _rstd(m)) * gm).astype(BF16)], axis=1)


def _residual_norm(acc, rows, vecs):
    x1 = rows[-1] + acc
    return [x1, (x1 * _rstd(x1)) * vecs[-1]], []


def _residual_loss_bwd(acc, rows, vecs):
    x2, gv = rows[0] + acc, vecs[0]
    diff = (x2 * _rstd(x2)) * gv - rows[1]
    dx, dg = _norm_bwd(x2, gv, diff / x2.shape[1])
    return [dx, dx], [dg, 0.5 * jnp.sum(jnp.mean(diff * diff, axis=-1, keepdims=True), axis=0, keepdims=True)]


def _norm_bwd_residual(acc, rows, vecs):
    dy = acc + rows[2] if len(rows) > 2 else acc
    dx, dg = _norm_bwd(rows[0], vecs[0], dy)
    if len(rows) > 1:
        dx = dx + rows[1]
    return [dx, dx], [dg]


def _out_norm_bwd(acc, rows, vecs):
    (f, m), w = rows, rows[0].shape[1]
    nh = w // HEAD_DIM
    lane_head = lax.shift_right_logical(lax.broadcasted_iota(jnp.int32, (w, nh), 0), _log2(HEAD_DIM))
    sel = (lane_head == lax.broadcasted_iota(jnp.int32, (w, nh), 1)).astype(BF16)
    dfo, dgf = _norm_bwd(f, vecs[0], acc[:, :w])
    dmo, dgm = _norm_bwd(m, vecs[1], acc[:, w:])
    return [dfo, dmo, _dot_x01(dfo * f, sel), _dot_x01(dmo * m, sel)], [dgf, dgm]


def _ffn_up(h, wg_t, wu_t, name, tm=512, tf=1408):
    t, d = h.shape
    f = wg_t.shape[0]
    tm, tf = _tile(t, tm, 16), _tile(f, tf, LANES)
    tok = pl.BlockSpec((tm, tf), lambda j, i: (i, j))
    wt = pl.BlockSpec((tf, d), lambda j, i: (j, 0))

    def body(h_ref, wg_ref, wu_ref, dg_ref, du_ref, a_ref):
        hv = h_ref[...]
        g, u = _dot(hv, wg_ref[...], "nt"), _dot(hv, wu_ref[...], "nt")
        sg = jax.nn.sigmoid(g)
        silu = g * sg
        dg_ref[...] = (u * (sg * (1.0 + g * (1.0 - sg)))).astype(BF16)
        du_ref[...] = silu.astype(BF16)
        a_ref[...] = (silu * u).astype(BF16)

    return pl.pallas_call(
        body, name=name, grid=(f // tf, t // tm), in_specs=[pl.BlockSpec((tm, d), lambda j, i: (i, 0)), wt, wt],
        out_specs=(tok, tok, tok),
        out_shape=(jax.ShapeDtypeStruct((t, f), BF16), jax.ShapeDtypeStruct((t, f), BF16), jax.ShapeDtypeStruct((t, f), BF16)),
        compiler_params=_params("parallel", "parallel"),
    )(h, wg_t, wu_t)


def _ffn_down_bwd(dy, w_down, act_by_gate, act_by_up, name, tm=512, tf=1408):
    t, d = dy.shape
    f = w_down.shape[0]
    tm, tf = _tile(t, tm, 16), _tile(f, tf, LANES)
    tok = pl.BlockSpec((tm, tf), lambda j, i: (i, j))

    def body(dy_ref, w_ref, g_ref, u_ref, dg_ref, du_ref):
        da = _dot(dy_ref[...], w_ref[...], "nt")
        dg_ref[...] = (da * g_ref[...].astype(F32)).astype(BF16)
        du_ref[...] = (da * u_ref[...].astype(F32)).astype(BF16)

    return pl.pallas_call(
        body, name=name, grid=(f // tf, t // tm),
        in_specs=[pl.BlockSpec((tm, d), lambda j, i: (i, 0)), pl.BlockSpec((tf, d), lambda j, i: (j, 0)), tok, tok],
        out_specs=(tok, tok),
        out_shape=(jax.ShapeDtypeStruct((t, f), BF16), jax.ShapeDtypeStruct((t, f), BF16)),
        compiler_params=_params("parallel", "parallel"),
    )(dy, w_down, act_by_gate, act_by_up)


def _chunk_scan_mats(rows, grp, reverse):
    ii = lax.broadcasted_iota(jnp.int32, (LANES, LANES), 0)
    jj = lax.broadcasted_iota(jnp.int32, (LANES, LANES), 1)
    within = ((ii >= jj) if reverse else (ii <= jj)).astype(BF16)
    ones = jnp.ones((LANES, LANES), BF16)
    ri = lax.broadcasted_iota(jnp.int32, (rows, rows), 0)
    rj = lax.broadcasted_iota(jnp.int32, (rows, rows), 1)
    sh = _log2(grp)
    same = lax.shift_right_logical(ri, sh) == lax.shift_right_logical(rj, sh)
    across = (same & ((rj > ri) if reverse else (rj < ri))).astype(BF16)
    return within, ones, across


def _running_sum(v, mats):
    within, ones, across = mats
    return _dot_x01(v, within) + _dot_01x(across, _dot_x01(v, ones))


def _fgate(z, bcol, grp, name):
    rows = z.shape[0]

    def body(z_ref, b_ref, c_ref):
        zz = z_ref[...] + b_ref[...]
        log_f = jnp.minimum(zz, 0.0) - jnp.log1p(jnp.exp(-jnp.abs(zz)))
        c_ref[...] = _running_sum(log_f, _chunk_scan_mats(rows, grp, False))

    return pl.pallas_call(body, name=name, out_shape=jax.ShapeDtypeStruct(z.shape, F32),
                          compiler_params=pltpu.CompilerParams(vmem_limit_bytes=VMEM_LIMIT))(z, bcol)


def _fgate_bwd(z, bcol, dc, grp, name):
    rows = z.shape[0]

    def body(z_ref, b_ref, dc_ref, dz_ref, db_ref):
        zz = z_ref[...] + b_ref[...]
        dz = _running_sum(dc_ref[...], _chunk_scan_mats(rows, grp, True)) * jax.nn.sigmoid(-zz)
        dz_ref[...] = dz
        head = lax.shift_right_logical(lax.broadcasted_iota(jnp.int32, (HEADS, rows), 1), _log2(grp)) & (HEADS - 1)
        sel = (head == lax.broadcasted_iota(jnp.int32, (HEADS, rows), 0)).astype(BF16)
        db_ref[...] = jnp.sum(_dot_01x(sel, dz), axis=1, keepdims=True)

    return pl.pallas_call(
        body, name=name,
        out_shape=(jax.ShapeDtypeStruct(z.shape, F32), jax.ShapeDtypeStruct((HEADS, 1), F32)),
        compiler_params=pltpu.CompilerParams(vmem_limit_bytes=VMEM_LIMIT),
    )(z, bcol, dc)


def _rotate(x, cs, sn_signed):
    return x * cs + pltpu.roll(x, LANES // 2, axis=1) * sn_signed


def _rope_tables(cos, sin):
    half = cos.shape[1]
    freq = lax.broadcasted_iota(jnp.int32, (half, LANES), 0)
    lane = lax.broadcasted_iota(jnp.int32, (half, LANES), 1)
    hit = (lane & (half - 1)) == freq
    sign = jnp.where(lane < LANES // 2, -1.0, 1.0)
    return _dot_x01(cos, hit.astype(BF16)), _dot_x01(sin, jnp.where(hit, sign, 0.0).astype(BF16))


def _mla_prep(proj_b, q_rank, kv_rank, gq, gkv, w_uq_p, w_ukv_p, nope, cs, sn, name):
    t, bw = proj_b.shape
    qw, kvw = w_uq_p.shape[0], w_ukv_p.shape[0]
    tm = _tile(t, 512)
    rows = lambda w: pl.BlockSpec((tm, w), lambda i: (i, 0))
    whole = lambda a: pl.BlockSpec(a.shape, lambda i: (0, 0))

    def body(pb_ref, gq_ref, gkv_ref, wq_ref, wkv_ref, c_ref, s_ref, qn_ref, kvn_ref, q_ref, kv_ref, kpe_ref):
        c, s = _rope_tables(c_ref[...], s_ref[...])
        ql, kvl = pb_ref[:, :q_rank], pb_ref[:, q_rank:q_rank + kv_rank]
        qn = ((ql * _rstd(ql)) * gq_ref[...]).astype(BF16)
        kvn = ((kvl * _rstd(kvl)) * gkv_ref[...]).astype(BF16)
        qn_ref[...], kvn_ref[...] = qn, kvn
        q_raw = _dot(qn, wq_ref[...], "nt")
        q_ref[:, :nope] = q_raw[:, :nope].astype(BF16)
        for off in range(nope, qw, LANES):
            q_ref[:, off:off + LANES] = _rotate(q_raw[:, off:off + LANES], c, s).astype(BF16)
        kv_ref[...] = _dot(kvn, wkv_ref[...], "nt").astype(BF16)
        kpe_ref[...] = _rotate(pb_ref[:, q_rank + kv_rank:q_rank + kv_rank + LANES], c, s).astype(BF16)

    return pl.pallas_call(
        body, name=name, grid=(t // tm,),
        in_specs=[rows(bw), whole(gq), whole(gkv), whole(w_uq_p), whole(w_ukv_p), rows(cs.shape[1]), rows(sn.shape[1])],
        out_specs=(rows(q_rank), rows(kv_rank), rows(qw), rows(kvw), rows(LANES)),
        out_shape=(jax.ShapeDtypeStruct((t, q_rank), BF16), jax.ShapeDtypeStruct((t, kv_rank), BF16),
                   jax.ShapeDtypeStruct((t, qw), BF16), jax.ShapeDtypeStruct((t, kvw), BF16), jax.ShapeDtypeStruct((t, LANES), BF16)),
        compiler_params=_params("parallel"),
    )(proj_b, gq, gkv, w_uq_p, w_ukv_p, cs, sn)


def _mla_prep_bwd(dq_nope, dq_pe, dkv_all, dk_pe, d_tail, proj_b, q_rank, kv_rank, gq, gkv, w_uq_p, w_ukv_p, cs, sn, name):
    t, bw = proj_b.shape
    nope, pw = dq_nope.shape[1], dq_pe.shape[1]
    tm = _tile(t, 512)
    rows = lambda w: pl.BlockSpec((tm, w), lambda i: (i, 0))
    whole = lambda a: pl.BlockSpec(a.shape, lambda i: (0, 0))
    o_k = q_rank + kv_rank

    def body(dqn_ref, dqp_ref, dkv_ref, dkp_ref, dt_ref, pb_ref, gq_ref, gkv_ref, wq_ref, wkv_ref, c_ref, s_ref,
             dpb_ref, dqr_ref, dgq_ref, dgkv_ref):
        c, s = _rope_tables(c_ref[...], -s_ref[...])
        for off in range(0, pw, LANES):
            dqr_ref[:, off:off + LANES] = _rotate(dqp_ref[:, off:off + LANES], c, s).astype(BF16)
        d_qn = _dot(dqn_ref[...], wq_ref[:nope, :], "nn") + _dot(dqr_ref[...], wq_ref[nope:, :], "nn")
        dq_lat, dgq = _norm_bwd(pb_ref[:, :q_rank], gq_ref[...], d_qn)
        dkv_lat, dgkv = _norm_bwd(pb_ref[:, q_rank:o_k], gkv_ref[...], _dot(dkv_ref[...], wkv_ref[...], "nn"))
        dpb_ref[:, :q_rank] = dq_lat.astype(BF16)
        dpb_ref[:, q_rank:o_k] = dkv_lat.astype(BF16)
        dpb_ref[:, o_k:o_k + LANES] = _rotate(dkp_ref[...], c, s).astype(BF16)
        dpb_ref[:, o_k + LANES:] = dt_ref[...].astype(BF16)

        @pl.when(pl.program_id(0) == 0)
        def _():
            dgq_ref[...] = jnp.zeros_like(dgq_ref)
            dgkv_ref[...] = jnp.zeros_like(dgkv_ref)

        dgq_ref[...] += dgq
        dgkv_ref[...] += dgkv

    return pl.pallas_call(
        body, name=name, grid=(t // tm,),
        in_specs=[rows(nope), rows(pw), rows(dkv_all.shape[1]), rows(LANES), rows(bw - o_k - LANES), rows(bw), whole(gq), whole(gkv),
                  whole(w_uq_p), whole(w_ukv_p), rows(cs.shape[1]), rows(sn.shape[1])],
        out_specs=(rows(bw), rows(pw), whole(gq), whole(gkv)),
        out_shape=(jax.ShapeDtypeStruct((t, bw), BF16), jax.ShapeDtypeStruct((t, pw), BF16),
                   jax.ShapeDtypeStruct(gq.shape, F32), jax.ShapeDtypeStruct(gkv.shape, F32)),
        compiler_params=_params("arbitrary"),
    )(dq_nope, dq_pe, dkv_all, dk_pe, d_tail, proj_b, gq, gkv, w_uq_p, w_ukv_p, cs, sn)


def _lane_masks(pair, h, pe):
    lane = lax.broadcasted_iota(jnp.int32, (1, LANES), 1)
    in_head = lax.shift_right_logical(lane, _log2(HEAD_DIM)) == h
    in_rope = ((lax.shift_right_logical(lane, _log2(MLA_ROPE // 2)) & 3) == ((2 * pair + h) & 3)) if pe else None
    return in_head, in_rope


def _keep(mask, v):
    return jnp.where(mask, v, jnp.zeros_like(v))


def _to_row(col):
    n = col.shape[0]
    eye = lax.broadcasted_iota(jnp.int32, (n, n), 0) == lax.broadcasted_iota(jnp.int32, (n, n), 1)
    return jnp.sum(jnp.where(eye, col, 0.0), axis=0, keepdims=True)


def _to_col(row):
    n = row.shape[1]
    eye = lax.broadcasted_iota(jnp.int32, (n, n), 0) == lax.broadcasted_iota(jnp.int32, (n, n), 1)
    return jnp.sum(jnp.where(eye, row, 0.0), axis=1, keepdims=True)


def _first_step():
    return (pl.program_id(0) == 0) & (pl.program_id(1) == 0)


def _last_step(n0, n1):
    return (pl.program_id(0) == n0 - 1) & (pl.program_id(1) == n1 - 1)


def _attn_fwd(ops, bias, scale, bl, s, tq, name, traffic=None):
    pe = len(ops) == 3
    has_bias = bias is not None
    exact_scale = math.frexp(scale)[0] == 0.5
    span = 4 * tq
    nq = s // tq
    t = bl * s
    n_carried = len(traffic.pieces) if traffic else 0

    def body(*refs):
        sems = refs[len(refs) - 3:] if traffic else ()
        if pe:
            q_ref, qpe_ref, kv_ref, kpe_ref = refs[:4]
            n_in = 4
            q_at = lambda r0, r1: q_ref[r0:r1, :]
            v_at = lambda r0, r1: kv_ref[r0:r1, LANES:]
            kcat = refs[len(refs) - 1 - len(sems)]
            kcat[:, :LANES] = kv_ref[:, :LANES]
            kcat[:, LANES:] = kpe_ref[...]
            k_at = lambda r0, r1: kcat[r0:r1, :]
        else:
            qkv_ref = refs[0]
            n_in = 1
            q_at = lambda r0, r1: qkv_ref[r0:r1, :LANES]
            k_at = lambda r0, r1: qkv_ref[r0:r1, LANES:2 * LANES]
            v_at = lambda r0, r1: qkv_ref[r0:r1, 2 * LANES:]
        if has_bias:
            c_ref = refs[n_in]
            n_in += 1
        carried_in = refs[n_in:n_in + n_carried]
        n_in += n_carried
        o_ref, lse_ref = refs[n_in:n_in + 2]
        if traffic:
            carried_out = refs[n_in + 2]

            @pl.when(_first_step())
            def _():
                traffic.start(carried_in, carried_out, *sems)

        pair = pl.program_id(1)
        causal = lax.broadcasted_iota(jnp.int32, (tq, tq), 1) <= lax.broadcasted_iota(jnp.int32, (tq, tq), 0)
        o_ref[...] = jnp.zeros_like(o_ref)

        masks = [_lane_masks(pair, h, pe) for h in range(2)]

        def logits(i):
            r0, r1 = i * tq, (i + 1) * tq
            out = []
            for h in range(2):
                in_head, in_rope = masks[h]
                qm = _keep(in_head, q_at(r0, r1))
                if pe:
                    qm = jnp.concatenate([qm, _keep(in_rope, qpe_ref[r0:r1, :])], axis=1)
                if exact_scale:
                    qm = qm * scale
                spans = []
                for k0, k1 in [(r0, r1)] + [(k, min(k + span, r0)) for k in range(0, r0, span)]:
                    sc = _dot(qm, k_at(k0, k1), "nt")
                    if not exact_scale:
                        sc = sc * scale
                    if has_bias:
                        sc = sc - c_ref[h, :, k0:k1]
                    spans.append((k0, k1, jnp.where(causal, sc, MASKED) if k0 == r0 else sc))
                out.append(spans)
            return out

        def softmax(per_head):
            out = []
            for spans in per_head:
                m = None
                for _, _, sc in spans:
                    top = jnp.max(sc, axis=1, keepdims=True)
                    m = top if m is None else jnp.maximum(m, top)
                probs = [(k0, k1, jnp.exp(sc - m)) for k0, k1, sc in spans]
                l = sum(jnp.sum(p, axis=1, keepdims=True) for _, _, p in probs)
                out.append((m, l, probs))
            return out

        def weigh(i, per_head):
            r0, r1 = i * tq, (i + 1) * tq
            for h, (m, l, probs) in enumerate(per_head):
                acc = sum(_dot(p, v_at(k0, k1), "nn") for k0, k1, p in probs)
                o_ref[r0:r1, :] = jnp.where(masks[h][0], acc / l, o_ref[r0:r1, :])
                lse = _to_row(m + jnp.log(l))
                lse_ref[h, :, r0:r1] = lse + c_ref[h, :, r0:r1] if has_bias else lse

        ahead = logits(0)
        for i in range(nq):
            solved = softmax(ahead)
            if i + 1 < nq:
                ahead = logits(i + 1)
            weigh(i, solved)

        if traffic:
            @pl.when(_last_step(bl, PAIRS))
            def _():
                traffic.wait(carried_out, *sems)

    seq = lambda w, col: pl.BlockSpec((s, w), col)
    if pe:
        in_specs = [seq(LANES, lambda b, p: (b, p)), seq(LANES, lambda b, p: (b, PAIRS + p // 2)),
                    seq(2 * LANES, lambda b, p: (b, p)), seq(LANES, lambda b, p: (b, 0))]
        args = [ops[0], ops[0], ops[1], ops[2]]
        scratch = [pltpu.VMEM((s, 2 * LANES), BF16)]
    else:
        in_specs = [seq(3 * LANES, lambda b, p: (b, p))]
        args = [ops[0]]
        scratch = []
    per_head_row = pl.BlockSpec((2, 1, s), lambda b, p: (b * PAIRS + p, 0, 0))
    if has_bias:
        in_specs.append(per_head_row)
        args.append(bias)
    out_specs = [seq(LANES, lambda b, p: (b, p)), per_head_row]
    out_shape = [jax.ShapeDtypeStruct((t, HEADS * HEAD_DIM), F32), jax.ShapeDtypeStruct((bl * HEADS, 1, s), F32)]
    if traffic:
        in_specs += traffic.in_specs
        args += traffic.pieces
        out_specs.append(traffic.out_spec)
        out_shape.append(traffic.out_shape)
        scratch += traffic.scratch
    return pl.pallas_call(
        body, name=name, grid=(bl, PAIRS), in_specs=in_specs, out_specs=tuple(out_specs), out_shape=tuple(out_shape),
        scratch_shapes=scratch, compiler_params=_params(*(("arbitrary", "arbitrary") if traffic else ("parallel", "parallel"))),
    )(*args)


def _attn_bwd(ops, do, lse, delta, bias, scale, bl, s, tq, name, traffic=None):
    pe = len(ops) == 3
    has_bias = bias is not None
    exact_scale = math.frexp(scale)[0] == 0.5
    span = 2 * tq
    nq = s // tq
    t = bl * s
    width = 2 * LANES if pe else LANES
    n_carried = len(traffic.pieces) if traffic else 0

    def body(*refs):
        if pe:
            q_ref, qpe_ref, kv_ref, kpe_ref = refs[:4]
            n_in = 4
            k_at = lambda r0, r1: kv_ref[r0:r1, :LANES]
            v_at = lambda r0, r1: kv_ref[r0:r1, LANES:]
        else:
            qkv_ref = refs[0]
            n_in = 1
            k_at = lambda r0, r1: qkv_ref[r0:r1, LANES:2 * LANES]
            v_at = lambda r0, r1: qkv_ref[r0:r1, 2 * LANES:]
        do_ref, lse_ref, dl_ref = refs[n_in:n_in + 3]
        n_in += 3
        if has_bias:
            c_ref = refs[n_in]
            n_in += 1
        carried_in = refs[n_in:n_in + n_carried]
        rest = refs[n_in + n_carried:]
        if traffic:
            rest, sems = rest[:-3], rest[-3:]
            carried_out = rest[4 if pe else 2]
            rest = rest[:4 if pe else 2] + rest[(4 if pe else 2) + 1:]

            @pl.when(_first_step())
            def _():
                traffic.start(carried_in, carried_out, *sems)

        if pe:
            dqn_ref, dkv_ref, dqpe_ref, dkpe_ref, dq_acc, qcat = rest
            qcat[:, :LANES] = q_ref[...]
            qcat[:, LANES:] = qpe_ref[...]
            q_at = lambda r0, r1: qcat[r0:r1, :]
            dkv_ref[...] = jnp.zeros_like(dkv_ref)
        else:
            dqkv_ref, dc_ref, dq_acc = rest
            q_at = lambda r0, r1: qkv_ref[r0:r1, :LANES]
            dqkv_ref[...] = jnp.zeros_like(dqkv_ref)
            dc_ref[...] = jnp.zeros_like(dc_ref)
        pair = pl.program_id(1)
        dq_acc[...] = jnp.zeros_like(dq_acc)
        causal = lax.broadcasted_iota(jnp.int32, (tq, tq), 1) >= lax.broadcasted_iota(jnp.int32, (tq, tq), 0)
        if pe:
            @pl.when(pair == 0)
            def _():
                dkpe_ref[...] = jnp.zeros_like(dkpe_ref)

            @pl.when(pair % 2 == 0)
            def _():
                dqpe_ref[...] = jnp.zeros_like(dqpe_ref)

        masks = [_lane_masks(pair, h, pe) for h in range(2)]

        def logits(j):
            r0, r1 = j * tq, (j + 1) * tq
            units = []
            for h in range(2):
                in_head, in_rope = masks[h]
                kt = _keep(in_head, k_at(r0, r1))
                if pe:
                    kt = jnp.concatenate([kt, _keep(in_rope, kpe_ref[r0:r1, :])], axis=1)
                if exact_scale:
                    kt = kt * scale
                vt = _keep(in_head, v_at(r0, r1))
                ck = _to_col(c_ref[h, :, r0:r1]) if has_bias else None
                for q0, q1, diagonal in [(r0, r1, True)] + [(q, min(q + span, s), False) for q in range(r1, s, span)]:
                    qq, dd = q_at(q0, q1), do_ref[q0:q1, :]
                    st = _dot(kt, qq, "nt")
                    if not exact_scale:
                        st = st * scale
                    shift = lse_ref[h, :, q0:q1]
                    if has_bias:
                        shift = shift - c_ref[h, :, q0:q1]
                        st = st - ck
                    st = st - shift
                    if diagonal:
                        st = jnp.where(causal, st, MASKED)
                    units.append((h, q0, q1, kt, qq, dd, st, _dot(vt, dd, "nt")))
            return units

        def softmax_bwd(units):
            solved = []
            for h, q0, q1, kt, qq, dd, st, dpt in units:
                pt = jnp.exp(st)
                dst = pt * (dpt - dl_ref[h, :, q0:q1])
                solved.append((h, q0, q1, kt, qq, dd, pt, dst, (dst if exact_scale else dst * scale).astype(BF16)))
            return solved

        def products(j, solved):
            r0, r1 = j * tq, (j + 1) * tq
            dv_of, dk_of, cs_of = [None, None], [None, None], [None, None]
            add = lambda old, new: new if old is None else old + new
            for h, q0, q1, kt, qq, dd, pt, dst, dsb in solved:
                dq_acc[q0:q1, :] += _dot(dsb, kt, "tn")
                dv_of[h] = add(dv_of[h], _dot(pt, dd, "nn"))
                dk_of[h] = add(dk_of[h], _dot(dsb, qq, "nn"))
                if has_bias:
                    dc_ref[h, :, q0:q1] += jnp.sum(dst, axis=0, keepdims=True)
                    cs_of[h] = add(cs_of[h], jnp.sum(dst, axis=1, keepdims=True))
            for h in range(2):
                (in_head, in_rope), dv_c, dk_c, cs = masks[h], dv_of[h], dk_of[h], cs_of[h]
                if exact_scale:
                    dk_c = dk_c * scale
                if pe:
                    dkv_ref[r0:r1, :LANES] = jnp.where(in_head, dk_c[:, :LANES].astype(BF16), dkv_ref[r0:r1, :LANES])
                    dkv_ref[r0:r1, LANES:] = jnp.where(in_head, dv_c.astype(BF16), dkv_ref[r0:r1, LANES:])
                    dkpe_ref[r0:r1, :] += _keep(in_rope, dk_c[:, LANES:])
                else:
                    dqkv_ref[r0:r1, LANES:2 * LANES] = jnp.where(in_head, dk_c.astype(BF16), dqkv_ref[r0:r1, LANES:2 * LANES])
                    dqkv_ref[r0:r1, 2 * LANES:] = jnp.where(in_head, dv_c.astype(BF16), dqkv_ref[r0:r1, 2 * LANES:])
                    dc_ref[h, :, r0:r1] -= _to_row(cs)

        units = logits(0)
        for j in range(nq):
            solved = softmax_bwd(units)
            if j + 1 < nq:
                units = logits(j + 1)
            products(j, solved)

        if pe:
            dqn_ref[...] = dq_acc[:, :LANES].astype(BF16)
            dqpe_ref[...] += dq_acc[:, LANES:]
        else:
            dqkv_ref[:, :LANES] = dq_acc[...].astype(BF16)
        if traffic:
            @pl.when(_last_step(bl, PAIRS))
            def _():
                traffic.wait(carried_out, *sems)

    seq = lambda w, col: pl.BlockSpec((s, w), col)
    per_head_row = pl.BlockSpec((2, 1, s), lambda b, p: (b * PAIRS + p, 0, 0))
    if pe:
        in_specs = [seq(LANES, lambda b, p: (b, p)), seq(LANES, lambda b, p: (b, PAIRS + p // 2)),
                    seq(2 * LANES, lambda b, p: (b, p)), seq(LANES, lambda b, p: (b, 0))]
        args = [ops[0], ops[0], ops[1], ops[2]]
    else:
        in_specs = [seq(3 * LANES, lambda b, p: (b, p))]
        args = [ops[0]]
    in_specs += [seq(LANES, lambda b, p: (b, p)), per_head_row, per_head_row]
    args += [do, lse, delta]
    if has_bias:
        in_specs.append(per_head_row)
        args.append(bias)
    scratch = [pltpu.VMEM((s, width), F32)]
    if pe:
        out_specs = (seq(LANES, lambda b, p: (b, p)), seq(2 * LANES, lambda b, p: (b, p)),
                     seq(LANES, lambda b, p: (b, p // 2)), seq(LANES, lambda b, p: (b, 0)))
        out_shape = (jax.ShapeDtypeStruct((t, PAIRS * LANES), BF16), jax.ShapeDtypeStruct((t, PAIRS * 2 * LANES), BF16),
                     jax.ShapeDtypeStruct((t, 2 * LANES), F32), jax.ShapeDtypeStruct((t, LANES), F32))
        scratch.append(pltpu.VMEM((s, 2 * LANES), BF16))
    else:
        out_specs = (seq(3 * LANES, lambda b, p: (b, p)), per_head_row)
        out_shape = (jax.ShapeDtypeStruct((t, PAIRS * 3 * LANES), BF16), jax.ShapeDtypeStruct((bl * HEADS, 1, s), F32))
    if traffic:
        in_specs += traffic.in_specs
        args += traffic.pieces
        out_specs += (traffic.out_spec,)
        out_shape += (traffic.out_shape,)
        scratch += traffic.scratch
    return pl.pallas_call(
        body, name=name, grid=(bl, PAIRS), in_specs=in_specs, out_specs=out_specs, out_shape=out_shape,
        scratch_shapes=scratch, compiler_params=_params("arbitrary" if traffic else "parallel", "arbitrary"),
    )(*args)


def _my_place():
    return lax.axis_index("x"), lax.axis_index("y"), lax.axis_index("c")


def _flip(p, bit):
    return 1 - p if bit else p


def _relative(x, y, c, k):
    return _flip(x, k & 4), _flip(y, k & 2), _flip(c, k & 1)


def _linear(x, y, c):
    return 4 * x + 2 * y + c


class _Traffic:
    def __init__(self, kind, pieces):
        self.kind, self.pieces = kind, list(pieces)
        self.rows = [p.shape[-2] for p in self.pieces]
        self.starts = [sum(self.rows[:i]) for i in range(len(self.rows))]
        anywhere = pl.BlockSpec(memory_space=pl.ANY)
        self.in_specs = [anywhere] * len(self.pieces)
        self.out_spec = anywhere
        self.out_shape = jax.ShapeDtypeStruct((N_DEV, sum(self.rows), self.pieces[0].shape[-1]), self.pieces[0].dtype)
        self.scratch = [pltpu.SemaphoreType.DMA((7,)), pltpu.SemaphoreType.DMA((7,)), pltpu.SemaphoreType.DMA(())]

    def start(self, p_refs, out_ref, send_sems, recv_sems, local_sem):
        x, y, c = _my_place()
        me = _linear(x, y, c)
        mine = lambda i, dev: p_refs[i] if self.kind == "spread" else p_refs[i].at[dev]
        landing = lambda i: out_ref.at[me, pl.ds(self.starts[i], self.rows[i])]
        for i in range(len(p_refs)):
            pltpu.make_async_copy(mine(i, me), landing(i), local_sem).start()
        for k in range(1, N_DEV):
            peer = _relative(x, y, c, k)
            for i in range(len(p_refs)):
                pltpu.make_async_remote_copy(
                    src_ref=mine(i, _linear(*peer)), dst_ref=landing(i),
                    send_sem=send_sems.at[k - 1], recv_sem=recv_sems.at[k - 1], device_id=peer, device_id_type=MESH).start()

    def wait(self, out_ref, send_sems, recv_sems, local_sem):
        x, y, c = _my_place()
        whole = out_ref.at[_linear(x, y, c)]
        for k in range(1, N_DEV):
            both = pltpu.make_async_remote_copy(
                src_ref=whole, dst_ref=whole, send_sem=send_sems.at[k - 1], recv_sem=recv_sems.at[k - 1],
                device_id=_relative(x, y, c, k), device_id_type=MESH)
            both.wait_recv()
            both.wait_send()
        pltpu.make_async_copy(whole, whole, local_sem).wait()


class _Relay(_Traffic):
    def __init__(self, piece):
        super().__init__("spread", [piece])

    @staticmethod
    def _chips(x, y):
        return [(1 - x, y), (x, 1 - y), (1 - x, 1 - y)]

    @staticmethod
    def _copy(k, block, to, out_ref, send_sems, recv_sems, src=None):
        slot = out_ref.at[_linear(*block)]
        return pltpu.make_async_remote_copy(src_ref=slot if src is None else src, dst_ref=slot, send_sem=send_sems.at[k],
                                            recv_sem=recv_sems.at[k], device_id=to, device_id_type=MESH)

    def start(self, p_refs, out_ref, send_sems, recv_sems, local_sem):
        x, y, c = _my_place()
        me, sems = (x, y, c), (out_ref, send_sems, recv_sems)
        pltpu.make_async_copy(p_refs[0], out_ref.at[_linear(*me)], local_sem).start()
        self._copy(0, me, (x, y, 1 - c), *sems, src=p_refs[0]).start()
        for j, chip in enumerate(self._chips(x, y)):
            self._copy(1 + j, me, (*chip, c), *sems, src=p_refs[0]).start()

    def relay(self, out_ref, send_sems, recv_sems, local_sem):
        x, y, c = _my_place()
        sems = (out_ref, send_sems, recv_sems)
        for j, chip in enumerate(self._chips(x, y)):
            self._copy(1 + j, (*chip, c), (x, y, c), *sems).wait_recv()
            self._copy(4 + j, (*chip, c), (x, y, 1 - c), *sems).start()

    def wait(self, out_ref, send_sems, recv_sems, local_sem):
        x, y, c = _my_place()
        me, sems = (x, y, c), (out_ref, send_sems, recv_sems)
        self._copy(0, (x, y, 1 - c), me, *sems).wait_recv()
        for j, chip in enumerate(self._chips(x, y)):
            self._copy(4 + j, (*chip, 1 - c), me, *sems).wait_recv()
        for k in range(N_DEV - 1):
            self._copy(k, me, (x, y, 1 - c), *sems).wait_send()
        whole = out_ref.at[_linear(*me)]
        pltpu.make_async_copy(whole, whole, local_sem).wait()


def _sum_blocks(parts, name):
    n, r, cdim = parts.shape
    tr = _tile(r, 640, 16)

    def body(p_ref, o_ref):
        acc = p_ref[0].astype(F32)
        for d in range(1, n):
            acc = acc + p_ref[d].astype(F32)
        o_ref[...] = acc

    return pl.pallas_call(
        body, name=name, grid=(r // tr,), in_specs=[pl.BlockSpec((n, tr, cdim), lambda i: (0, i, 0))],
        out_specs=pl.BlockSpec((tr, cdim), lambda i: (i, 0)), out_shape=jax.ShapeDtypeStruct((r, cdim), F32),
        compiler_params=_params("parallel"),
    )(parts)


def _adamw_math(w, g, m, v):
    m = ADAM_B1 * m + (1.0 - ADAM_B1) * g
    v = ADAM_B2 * v + (1.0 - ADAM_B2) * (g * g)
    m_hat = m / (1.0 - ADAM_B1 ** ADAM_STEP)
    v_hat = v / (1.0 - ADAM_B2 ** ADAM_STEP)
    delta = -ADAM_LR * (m_hat / (jnp.sqrt(v_hat) + ADAM_EPS) + ADAM_WD * w)
    return delta, m, v


def _adamw(w, g, m, v, name):
    def body(w_ref, g_ref, m_ref, v_ref, d_ref, nm_ref, nv_ref):
        d_ref[...], nm_ref[...], nv_ref[...] = _adamw_math(w_ref[...], g_ref[...], m_ref[...], v_ref[...])

    out = jax.ShapeDtypeStruct(w.shape, F32)
    return pl.pallas_call(body, name=name, out_shape=(out, out, out),
                          compiler_params=pltpu.CompilerParams(vmem_limit_bytes=VMEM_LIMIT))(w, g, m, v)


def _small_all_reduce_adamw(parts, loss_part, ws, ms, vs, name):
    sizes = [p.shape[1] for p in parts] + [1]
    spots = [sum(-(-n // LANES) * LANES for n in sizes[:i]) for i in range(len(sizes))]
    width = spots[-1] + LANES
    k = len(parts)

    def reduce_body(*refs):
        p_refs, tot_ref, rows, send_sems, recv_sems = refs[:k + 1], *refs[k + 1:]
        x, y, c = _my_place()
        me = _linear(x, y, c)
        rows[me] = jnp.zeros((1, width), F32)
        for i in range(k + 1):
            rows[me, :, spots[i]:spots[i] + sizes[i]] = p_refs[i][...]
        copies = []
        for rel in range(1, N_DEV):
            copies.append(pltpu.make_async_remote_copy(
                src_ref=rows.at[me], dst_ref=rows.at[me], send_sem=send_sems.at[rel - 1], recv_sem=recv_sems.at[rel - 1],
                device_id=_relative(x, y, c, rel), device_id_type=MESH))
        for cp in copies:
            cp.start()
        for cp in copies:
            cp.wait_recv()
        for cp in copies:
            cp.wait_send()
        total = rows[0]
        for d in range(1, N_DEV):
            total = total + rows[d]
        tot_ref[...] = total

    total = pl.pallas_call(
        reduce_body, name=name, out_shape=jax.ShapeDtypeStruct((1, width), F32),
        scratch_shapes=[pltpu.VMEM((N_DEV, 1, width), F32), pltpu.SemaphoreType.DMA((7,)), pltpu.SemaphoreType.DMA((7,))],
    )(*parts, loss_part)

    def adamw_body(*refs):
        tot_ref, w_refs, m_refs, v_refs, outs = refs[0], refs[1:k + 1], refs[k + 1:2 * k + 1], refs[2 * k + 1:3 * k + 1], refs[3 * k + 1:]
        for i in range(k):
            g = tot_ref[:, spots[i]:spots[i] + sizes[i]]
            outs[4 * i][...] = g
            outs[4 * i + 1][...], outs[4 * i + 2][...], outs[4 * i + 3][...] = _adamw_math(w_refs[i][...], g, m_refs[i][...], v_refs[i][...])
        outs[4 * k][...] = tot_ref[:, spots[k]:spots[k] + 1]

    out_shape = [jax.ShapeDtypeStruct((1, n), F32) for n in sizes[:k] for _ in range(4)] + [jax.ShapeDtypeStruct((1, 1), F32)]
    res = pl.pallas_call(adamw_body, name=name + "_adamw", out_shape=tuple(out_shape))(total, *ws, *ms, *vs)
    return [res[4 * i:4 * i + 4] for i in range(k)], res[4 * k]


def _pad_rows(a, rows):
    return jnp.pad(a, ((0, rows - a.shape[0]), (0, 0)))


def kernel(x, positions, norm_mix_g, w_in, b_fgate, q_norm_g, w_uq, kv_norm_g, w_ukv, fox_out_g, mla_out_g, w_o, norm_ffn_g, w_gate, w_up, w_down, final_norm_g, loss_target, m_norm_mix_g, m_w_in, m_b_fgate, m_q_norm_g, m_w_uq, m_kv_norm_g, m_w_ukv, m_fox_out_g, m_mla_out_g, m_w_o, m_norm_ffn_g, m_w_gate, m_w_up, m_w_down, m_final_norm_g, v_norm_mix_g, v_w_in, v_b_fgate, v_q_norm_g, v_w_uq, v_kv_norm_g, v_w_ukv, v_fox_out_g, v_mla_out_g, v_w_o, v_norm_ffn_g, v_w_gate, v_w_up, v_w_down, v_final_norm_g):
    bl, s, d = x.shape
    t = bl * s
    bh = bl * HEADS
    tq = _tile(s, 256)
    grp = s // LANES
    fw = HEADS * HEAD_DIM
    q_rank, kv_rank = w_uq.shape[1], w_ukv.shape[1]
    in_cols = w_in.shape[2]
    n_in = N_DEV * in_cols
    ff = N_DEV * w_gate.shape[2]
    half = MLA_ROPE // 2
    o_kvlat, o_krope, o_flogit = q_rank, q_rank + kv_rank, q_rank + kv_rank + LANES
    b_cols = -(-(o_flogit + HEADS) // LANES) * LANES

    tr = lambda w: jnp.transpose(w[0])
    in_rows = -(-in_cols // 16) * 16
    uq_rows = w_uq.shape[2] * q_rank // d
    ukv_rows = w_ukv.shape[2] * kv_rank // d
    pieces = [_pad_rows(tr(w_in), in_rows), _pad_rows(tr(w_uq).reshape(uq_rows, d), -(-uq_rows // 16) * 16),
              tr(w_ukv).reshape(ukv_rows, d), w_o[0], tr(w_gate), tr(w_up), w_down[0]]
    pieces = [p.astype(BF16) for p in pieces]
    offs = [0]
    for p in pieces:
        offs.append(offs[-1] + p.shape[0])
    legs = [(0, 1), (1, 5), (5, 7)]
    gathered = {}

    def full(i, rows):
        leg = next(n for n, (lo, hi) in enumerate(legs) if lo <= i < hi)
        base = offs[legs[leg][0]]
        return gathered[leg][:, offs[i] - base:offs[i] - base + rows]

    x2d = x.reshape(t, d)
    h1, gathered[0] = _rmsnorm(x2d, 0, d, norm_mix_g, BF16, "norm_mix", traffic=_Relay(pieces[0]))

    w_in_t = full(0, in_cols).reshape(n_in, d)
    n_qkv = 3 * fw
    w_in_a = w_in_t[:n_qkv].reshape(3, PAIRS, LANES, d).transpose(1, 0, 2, 3).reshape(n_qkv, d)
    lat0, rope0 = n_qkv + HEADS, n_qkv + HEADS + q_rank + kv_rank
    k_rep = jnp.broadcast_to(w_in_t[rope0:].reshape(2, 1, half, d), (2, 4, half, d)).reshape(LANES, d)
    w_in_b = jnp.concatenate([w_in_t[lat0:rope0], k_rep, w_in_t[n_qkv:lat0],
                              jnp.zeros((b_cols - o_flogit - HEADS, d), BF16)], axis=0)

    def per_head_rows(a):
        return a.reshape(bl, s, HEADS).transpose(0, 2, 1).reshape(bh, 1, s)

    proj_a = _matmul(h1, w_in_a, "nt", BF16, "proj_fox", tm=1024, tn=6 * LANES)
    proj_b = _matmul(h1, w_in_b, "nt", F32, "proj_mla", tm=1024, tn=b_cols)

    z = proj_b[:, o_flogit:o_flogit + HEADS].reshape(bl, s, HEADS).transpose(0, 2, 1).reshape(bh * grp, LANES)
    bcol = jnp.broadcast_to(b_fgate.reshape(1, HEADS, 1), (bl, HEADS, grp)).reshape(bh * grp, 1)
    c = _fgate(z, bcol, grp, "forget_gate")
    c_bias = c.reshape(bh, 1, s)
    fox_o, fox_lse, gathered[1] = _attn_fwd((proj_a,), c_bias, HEAD_DIM ** -0.5, bl, s, tq, "fox_attention",
                                            traffic=_Traffic("spread", pieces[legs[1][0]:legs[1][1]]))
    w_uq_h = full(1, uq_rows).reshape(HEADS, MLA_QK, q_rank)
    w_uq_pe = jnp.concatenate([w_uq_h[:, HEAD_DIM:HEAD_DIM + half].reshape(2, 1, 4 * half, q_rank),
                               w_uq_h[:, HEAD_DIM + half:].reshape(2, 1, 4 * half, q_rank)], axis=1).reshape(2 * LANES, q_rank)
    w_uq_p = jnp.concatenate([w_uq_h[:, :HEAD_DIM].reshape(fw, q_rank), w_uq_pe], axis=0)
    w_ukv_p = full(2, ukv_rows).reshape(PAIRS, 2, 2, HEAD_DIM, kv_rank).transpose(0, 2, 1, 3, 4).reshape(2 * fw, kv_rank)
    w_o_f = full(3, w_o.shape[1]).reshape(-1, d)
    w_gate_t = full(4, ff // N_DEV).reshape(ff, d)

    inv_freq = ROPE_THETA ** (-jnp.arange(0, MLA_ROPE, 2, dtype=F32) / MLA_ROPE)
    ang = positions.astype(F32).reshape(t, 1) * inv_freq[None, :]
    rope_cos, rope_sin = jnp.cos(ang), jnp.sin(ang)
    qn, kvn, q_all, kv_all, kpe = _mla_prep(proj_b, q_rank, kv_rank, q_norm_g, kv_norm_g, w_uq_p, w_ukv_p, fw,
                                            rope_cos, rope_sin, "mla_prep")
    mla_ops = (q_all, kv_all, kpe)
    mla_o, mla_lse, gathered[2] = _attn_fwd(mla_ops, None, MLA_QK ** -0.5, bl, s, tq, "mla_attention",
                                            traffic=_Traffic("spread", pieces[legs[2][0]:legs[2][1]]))
    w_up_t, w_down_f = full(5, ff // N_DEV).reshape(ff, d), full(6, ff // N_DEV).reshape(ff, d)

    both = [(d, F32), (d, BF16)]
    cat, x1, h2 = _rows_matmul([(None, w_o_f, "nn")], [fox_o, mla_o, x2d], [fox_out_g, mla_out_g, norm_ffn_g], _residual_norm,
                               [(2 * fw, BF16)] + both, [], "norm_out_proj_out_norm_ffn", prologue=_out_norm)
    act_by_gate, act_by_up, act = _ffn_up(h2, w_gate_t, w_up_t, "ffn_gate_up")
    dx2, dx2_b, dg_final, loss_part = _rows_matmul(
        [(act, w_down_f, "nn")], [x1, loss_target.reshape(t, d)], [final_norm_g.reshape(1, d)], _residual_loss_bwd,
        both, [d, 1], "ffn_down_final_norm_loss")

    d_gate, d_up = _ffn_down_bwd(dx2_b, w_down_f, act_by_gate, act_by_up, "d_ffn_down")
    dw_down = _matmul(act, dx2_b, "tn", BF16, "dw_down", tm=ff // 2, tn=d, tk=2048)
    dw_gate = _matmul(d_gate, h2, "tn", BF16, "dw_gate", tm=ff // 2, tn=d, tk=2048)
    dw_up = _matmul(d_up, h2, "tn", BF16, "dw_up", tm=ff // 2, tn=d, tk=2048)
    dx1, dx1_b, dg_ffn = _rows_matmul([(d_gate, w_gate_t, "nn"), (d_up, w_up_t, "nn")], [x1, dx2], [norm_ffn_g],
                                      _norm_bwd_residual, both, [d], "d_ffn_gate_up_norm_ffn", tm=512)
    dw_o = _matmul(cat, dx1_b, "tn", BF16, "dw_o", tn=d, tk=2048)
    d_fox_o, d_mla_o, fox_delta, mla_delta, dg_fox, dg_mla = _rows_matmul(
        [(dx1_b, w_o_f, "nt")], [fox_o, mla_o], [fox_out_g, mla_out_g], _out_norm_bwd,
        [(fw, BF16), (fw, BF16), (HEADS, F32), (HEADS, F32)], [fw, fw], "d_proj_out_norm_out")

    per_dev = lambda a: a.reshape(N_DEV, -1, d)
    late_grads = [per_dev(dw_o), per_dev(dw_gate), per_dev(dw_up), per_dev(dw_down)]
    dproj_a, dc, g_late = _attn_bwd((proj_a,), d_fox_o, fox_lse, per_head_rows(fox_delta),
                                    c_bias, HEAD_DIM ** -0.5, bl, s, tq, "d_fox_attention", traffic=_Traffic("swap", late_grads))
    dz, db_fgate = _fgate_bwd(z, bcol, dc.reshape(bh * grp, LANES), grp, "d_forget_gate")
    d_flogit = dz.reshape(bl, HEADS, s).transpose(0, 2, 1).reshape(t, HEADS)

    dq_nope, dkv_all, dq_pe, dk_pe = _attn_bwd(mla_ops, d_mla_o, mla_lse, per_head_rows(mla_delta),
                                               None, MLA_QK ** -0.5, bl, s, tq, "d_mla_attention")
    d_tail = jnp.pad(d_flogit, ((0, 0), (0, b_cols - o_flogit - HEADS)))
    dproj_b, dq_rot, dg_q, dg_kv = _mla_prep_bwd(dq_nope, dq_pe, dkv_all, dk_pe, d_tail, proj_b, q_rank, kv_rank,
                                                 q_norm_g, kv_norm_g, w_uq_p, w_ukv_p, rope_cos, rope_sin, "d_mla_prep")
    dw_uq_nope = _matmul(dq_nope, qn, "tn", BF16, "dw_uq_nope", tn=q_rank, tk=1024)
    dw_uq_pe = _matmul(dq_rot, qn, "tn", BF16, "dw_uq_rope", tn=q_rank, tk=1024)
    dw_ukv_p = _matmul(dkv_all, kvn, "tn", BF16, "dw_ukv", tn=kv_rank, tk=1024)
    dw_in_a = _matmul(dproj_a, h1, "tn", BF16, "dw_in_fox", tm=6 * LANES, tn=d, tk=2048)
    dw_in_b = _matmul(dproj_b, h1, "tn", F32, "dw_in_mla", tm=b_cols, tn=d, tk=1024)

    dw_krope = dw_in_b[o_krope:o_flogit].reshape(2, 4, half, d).sum(axis=1).reshape(MLA_ROPE, d)
    dw_in_t = jnp.concatenate([dw_in_a.reshape(PAIRS, 3, LANES, d).transpose(1, 0, 2, 3).reshape(n_qkv, d),
                               dw_in_b[o_flogit:o_flogit + HEADS].astype(BF16), dw_in_b[:o_krope].astype(BF16),
                               dw_krope.astype(BF16)], axis=0)
    pad_dev = lambda a, rows: jnp.pad(a, ((0, 0), (0, rows - a.shape[1]), (0, 0)))
    dw_uq_pe5 = dw_uq_pe.reshape(2, 2, 4, half, q_rank)
    dw_uq_h = jnp.concatenate([dw_uq_nope.reshape(HEADS, HEAD_DIM, q_rank), dw_uq_pe5[:, 0].reshape(HEADS, half, q_rank),
                               dw_uq_pe5[:, 1].reshape(HEADS, half, q_rank)], axis=1)
    dw_ukv_h = dw_ukv_p.reshape(PAIRS, 2, 2, HEAD_DIM, kv_rank).transpose(0, 2, 1, 3, 4).reshape(HEADS, 2 * HEAD_DIM, kv_rank)
    n_last = 3
    last_grads = [pad_dev(per_dev(dw_in_t), pieces[0].shape[0]), pad_dev(per_dev(dw_uq_h), pieces[1].shape[0]), per_dev(dw_ukv_h)]
    grad_x, dg_mix, g_last = _rows_matmul([(dproj_a, w_in_a, "nn"), (dproj_b, w_in_b, "nn")], [x2d, dx1], [norm_mix_g],
                                          _norm_bwd_residual, [(d, F32)], [d], "d_proj_in_norm_mix",
                                          traffic=_Traffic("swap", last_grads))
    g_last = _sum_blocks(g_last, "sum_last_grads")
    g_late = _sum_blocks(g_late, "sum_late_grads")

    def mine(i, rows):
        src, base = (g_last, 0) if i < n_last else (g_late, offs[n_last])
        return src[offs[i] - base:offs[i] - base + rows]

    big = [
        ("w_in", w_in, m_w_in, v_w_in, mine(0, in_cols), True),
        ("w_uq", w_uq, m_w_uq, v_w_uq, mine(1, uq_rows).reshape(-1, q_rank), True),
        ("w_ukv", w_ukv, m_w_ukv, v_w_ukv, mine(2, ukv_rows).reshape(-1, kv_rank), True),
        ("w_o", w_o, m_w_o, v_w_o, mine(3, w_o.shape[1]), False),
        ("w_gate", w_gate, m_w_gate, v_w_gate, mine(4, ff // N_DEV), True),
        ("w_up", w_up, m_w_up, v_w_up, mine(5, ff // N_DEV), True),
        ("w_down", w_down, m_w_down, v_w_down, mine(6, ff // N_DEV), False),
    ]
    out = {}
    for nm, w, m, v, g, transposed in big:
        lay = (lambda a: a[0].T) if transposed else (lambda a: a[0])
        back = (lambda a: a.T[None]) if transposed else (lambda a: a[None])
        dl, new_m, new_v = _adamw(lay(w), g, lay(m), lay(v), "adamw_" + nm)
        out[nm] = (back(g), back(dl), back(new_m), back(new_v))

    smalls = [("norm_mix_g", norm_mix_g, m_norm_mix_g, v_norm_mix_g, dg_mix),
              ("b_fgate", b_fgate, m_b_fgate, v_b_fgate, db_fgate.reshape(1, HEADS)),
              ("q_norm_g", q_norm_g, m_q_norm_g, v_q_norm_g, dg_q),
              ("kv_norm_g", kv_norm_g, m_kv_norm_g, v_kv_norm_g, dg_kv),
              ("fox_out_g", fox_out_g, m_fox_out_g, v_fox_out_g, dg_fox),
              ("mla_out_g", mla_out_g, m_mla_out_g, v_mla_out_g, dg_mla),
              ("norm_ffn_g", norm_ffn_g, m_norm_ffn_g, v_norm_ffn_g, dg_ffn),
              ("final_norm_g", final_norm_g, m_final_norm_g, v_final_norm_g, dg_final)]
    flat = lambda a: a.reshape(1, -1)
    results, loss = _small_all_reduce_adamw([e[4] for e in smalls], loss_part, [flat(e[1]) for e in smalls],
                                            [flat(e[2]) for e in smalls], [flat(e[3]) for e in smalls], "reduce_small_adamw")
    for (nm, w, _, _, _), res in zip(smalls, results):
        out[nm] = tuple(a.reshape(w.shape) for a in res)
    loss = loss[0, 0]

    order = ["norm_mix_g", "w_in", "b_fgate", "q_norm_g", "w_uq", "kv_norm_g", "w_ukv", "fox_out_g", "mla_out_g", "w_o",
             "norm_ffn_g", "w_gate", "w_up", "w_down", "final_norm_g"]
    return (loss, grad_x.reshape(bl, s, d), *[out[n][0] for n in order], *[out[n][1] for n in order],
            *[out[n][2] for n in order], *[out[n][3] for n in order])
```
